```python
import jax, jax.numpy as jnp
from jax import lax
import numpy as np

D_MODEL = 1024
BATCH = 32
SEQ = 256
DEPTH = 2
DEC_BATCH = 8
DEC_SEQ = 1024
PAST_LEN = 512

GRID_W = 64
HEAD_DIM = 64
N_HEADS = D_MODEL // 128
N_KV_HEADS = N_HEADS // 4
ATT_Q = N_HEADS * HEAD_DIM
ATT_KV = N_KV_HEADS * HEAD_DIM
RET_HEADS = 4
RET_DK = D_MODEL // 16
RET_DV = 2 * RET_DK
RET_QK = RET_HEADS * RET_DK
RET_V = RET_HEADS * RET_DV
FOURIER_GROUPS = 4
FOURIER_GROUP_DIM = D_MODEL // 8
FOURIER_W = FOURIER_GROUPS * FOURIER_GROUP_DIM
N_BRANCH = 3
D_IN = ATT_Q + 2 * ATT_KV + 2 * RET_QK + 2 * RET_V + FOURIER_W + N_BRANCH * D_MODEL
D_FF = ((8 * D_MODEL // 3 + 127) // 128) * 128
CONV_W = 3
CHUNK = 128
Q_BLOCK = 128
ROPE_THETA = 10000.0
EPS = 1e-6

kernel_name = 'hybrid_prefix_diffusion_step'


def rmsnorm(x, g):
    xf = x.astype(jnp.float32)
    y = xf * lax.rsqrt(jnp.mean(xf * xf, axis=-1, keepdims=True) + EPS)
    return y.astype(x.dtype) * g


def axial_rope_tables(n_tok):
    rows = n_tok // GRID_W
    row_id = jnp.repeat(jnp.arange(rows), GRID_W).astype(jnp.float32)
    col_id = jnp.tile(jnp.arange(GRID_W), rows).astype(jnp.float32)
    n_freq = HEAD_DIM // 4
    inv = ROPE_THETA ** (-jnp.arange(n_freq, dtype=jnp.float32) / n_freq)
    ang = jnp.stack([row_id[:, None] * inv, col_id[:, None] * inv], axis=1)
    return jnp.cos(ang), jnp.sin(ang)


def apply_rope(x, cos, sin):
    b, n, h, hd = x.shape
    xr = x.reshape(b, n, h, 2, 2, hd // 4)
    x1, x2 = xr[..., 0, :], xr[..., 1, :]
    c = cos[None, :, None].astype(x.dtype)
    s = sin[None, :, None].astype(x.dtype)
    out = jnp.stack([x1 * c - x2 * s, x2 * c + x1 * s], axis=-2)
    return out.reshape(b, n, h, hd)


def blocked_gqa(q, k, v):
    b, lq, h, hd = q.shape
    g = h // N_KV_HEADS
    nb = lq // Q_BLOCK
    qb = q.reshape(b, nb, Q_BLOCK, N_KV_HEADS, g, hd).transpose(1, 0, 2, 3, 4, 5)
    scale = HEAD_DIM ** -0.5

    def attend(qblk):
        s = jnp.einsum('bqkgd,bskd->bkgqs', qblk, k).astype(jnp.float32) * scale
        p = jax.nn.softmax(s, axis=-1).astype(v.dtype)
        return jnp.einsum('bkgqs,bskd->bqkgd', p, v)

    o = lax.map(attend, qb)
    return o.transpose(1, 0, 2, 3, 4, 5).reshape(b, lq, h * hd)


def retention_chunked(q, k, v, log_gamma, s0):
    b, n_tok, h, _ = q.shape
    dv = v.shape[-1]
    n = n_tok // CHUNK

    def chunks(t):
        return t.reshape(b, n, CHUNK, h, t.shape[-1]).transpose(1, 0, 2, 3, 4)

    idx = jnp.arange(CHUNK, dtype=jnp.float32)
    rel = idx[:, None] - idx[None, :]
    intra = jnp.where(rel >= 0, jnp.exp(jnp.maximum(rel, 0.0)[None] * log_gamma[:, None, None]), 0.0)
    q_dec = jnp.exp((idx[:, None] + 1.0) * log_gamma[None, :])
    k_dec = jnp.exp((CHUNK - 1.0 - idx)[:, None] * log_gamma[None, :])
    c_dec = jnp.exp(CHUNK * log_gamma)

    def step(state, inp):
        qc, kc, vc = inp
        att = jnp.einsum('bihd,bjhd->bhij', qc, kc) * intra
        o = (jnp.einsum('bhij,bjhe->bihe', att, vc)
             + jnp.einsum('bihd,bhde->bihe', qc, state) * q_dec[None, :, :, None])
        state = (state * c_dec[None, :, None, None]
                 + jnp.einsum('bjhd,bjhe->bhde', kc * k_dec[None, :, :, None], vc))
        return state, o

    s_fin, o = lax.scan(step, s0, (chunks(q), chunks(k), chunks(v)))
    return o.transpose(1, 0, 2, 3, 4).reshape(b, n_tok, h, dv), s_fin


def bidir_retention(q, k, v, lg_f, lg_b, s0_f, s0_b):
    o_f, s_f = retention_chunked(q, k, v, lg_f, s0_f)
    o_b, s_b = retention_chunked(jnp.flip(q, 1), jnp.flip(k, 1), jnp.flip(v, 1), lg_b, s0_b)
    return o_f + jnp.flip(o_b, 1), s_f, s_b


def dwconv3(a, w, bias):
    n_tok = a.shape[1]
    ap = jnp.pad(a, ((0, 0), (1, 1), (0, 0)))
    return ap[:, :n_tok] * w[0] + ap[:, 1:n_tok + 1] * w[1] + ap[:, 2:n_tok + 2] * w[2] + bias


def trunk_layer(x, cond, rope, ctx, w_ada, b_ada, norm1, w_in, q_norm, k_norm,
                ret_decay_f, ret_decay_b, ret_norm, w_br_att, w_br_ret, w_br_four,
                w_out, norm2, w_up, conv_w, conv_b, w_down):
    bsz, n_tok, _ = x.shape
    mod = (jax.nn.silu(cond) @ w_ada + b_ada)[:, None, :]
    sh1, sc1, g1, sh2, sc2, g2 = jnp.split(mod, 6, axis=-1)

    h = rmsnorm(x, norm1) * (1 + sc1) + sh1
    splits = np.cumsum([ATT_Q, ATT_KV, ATT_KV, RET_QK, RET_QK, RET_V, RET_V, FOURIER_W]).tolist()
    q_a, k_a, v_a, q_r, k_r, v_r, g_r, u_f, gate_logits = jnp.split(h @ w_in, splits, axis=-1)

    q_a = rmsnorm(q_a.reshape(bsz, n_tok, N_HEADS, HEAD_DIM), q_norm)
    k_a = rmsnorm(k_a.reshape(bsz, n_tok, N_KV_HEADS, HEAD_DIM), k_norm)
    v_a = v_a.reshape(bsz, n_tok, N_KV_HEADS, HEAD_DIM)
    if ctx is None:
        keys, vals = k_a, v_a
        s0_f = jnp.zeros((bsz, RET_HEADS, RET_DK, RET_DV), jnp.float32)
        s0_b = s0_f
    else:
        ctx_k, ctx_v, ctx_sf, ctx_sb = ctx
        cos, sin = rope
        q_a = apply_rope(q_a, cos, sin)
        keys = jnp.concatenate([apply_rope(k_a, cos, sin), ctx_k.astype(k_a.dtype)], axis=1)
        vals = jnp.concatenate([v_a, ctx_v.astype(v_a.dtype)], axis=1)
        s0_f = ctx_sf.astype(jnp.float32)
        s0_b = ctx_sb.astype(jnp.float32)
    o_att = blocked_gqa(q_a, keys, vals)

    qr = q_r.reshape(bsz, n_tok, RET_HEADS, RET_DK).astype(jnp.float32)
    kr = k_r.reshape(bsz, n_tok, RET_HEADS, RET_DK).astype(jnp.float32) * (RET_DK ** -0.5)
    vr = v_r.reshape(bsz, n_tok, RET_HEADS, RET_DV).astype(jnp.float32)
    lg_f = jax.nn.log_sigmoid(ret_decay_f.astype(jnp.float32))
    lg_b = jax.nn.log_sigmoid(ret_decay_b.astype(jnp.float32))
    o_r, s_f, s_b = bidir_retention(qr, kr, vr, lg_f, lg_b, s0_f, s0_b)
    o_r = rmsnorm(o_r, ret_norm).reshape(bsz, n_tok, RET_V).astype(x.dtype) * jax.nn.silu(g_r)

    u = u_f.reshape(bsz, n_tok, FOURIER_GROUPS, FOURIER_GROUP_DIM).astype(jnp.float32)
    o_f = jnp.fft.fft2(u, axes=(1, 3), norm='ortho').real.astype(x.dtype).reshape(bsz, n_tok, FOURIER_W)

    ga, gr, gf = jnp.split(jax.nn.sigmoid(gate_logits), N_BRANCH, axis=-1)
    merged = ga * (o_att @ w_br_att) + gr * (o_r @ w_br_ret) + gf * (o_f @ w_br_four)
    x = x + g1 * (merged @ w_out)

    h2 = rmsnorm(x, norm2) * (1 + sc2) + sh2
    a, val = jnp.split(h2 @ w_up, 2, axis=-1)
    a = dwconv3(a, conv_w, conv_b)
    x = x + g2 * ((jax.nn.gelu(a) * val) @ w_down)

    new_ctx = (k_a, v_a, s_f, s_b) if ctx is None else None
    return x, new_ctx


def setup_inputs(seed: int = 0) -> dict:
    key = jax.random.key(seed)
    ks = jax.random.split(key, 32)

    def nrm(k, shape, scale):
        return jax.random.normal(k, shape, jnp.float32) * scale

    ls_f = jnp.linspace(jnp.log(1.0 / 32), jnp.log(1.0 / 512), RET_HEADS)
    ls_b = ls_f[::-1]
    logit_f = jnp.log1p(-jnp.exp(ls_f)) - ls_f
    logit_b = jnp.log1p(-jnp.exp(ls_b)) - ls_b
    return {
        'x_prompt': nrm(ks[0], (BATCH, SEQ, D_MODEL), 1.0),
        'x_sample': nrm(ks[1], (DEC_BATCH, DEC_SEQ, D_MODEL), 1.0),
        'cache_k': nrm(ks[2], (DEC_BATCH, DEPTH, PAST_LEN, N_KV_HEADS, HEAD_DIM), 1.0),
        'cache_v': nrm(ks[3], (DEC_BATCH, DEPTH, PAST_LEN, N_KV_HEADS, HEAD_DIM), 1.0),
        'state_ret_fwd': nrm(ks[4], (DEC_BATCH, DEPTH, RET_HEADS, RET_DK, RET_DV), 0.5),
        'state_ret_bwd': nrm(ks[5], (DEC_BATCH, DEPTH, RET_HEADS, RET_DK, RET_DV), 0.5),
        'c': nrm(ks[6], (DEC_BATCH, D_MODEL), 1.0),
        'c_ctx': nrm(ks[7], (D_MODEL,), 1.0),
        'w_ada': nrm(ks[8], (DEPTH, D_MODEL, 6 * D_MODEL), 0.5 * D_MODEL ** -0.5),
        'b_ada': nrm(ks[9], (DEPTH, 6 * D_MODEL), 0.02),
        'norm1': 1.0 + nrm(ks[10], (DEPTH, D_MODEL), 0.02),
        'w_in': nrm(ks[11], (DEPTH, D_MODEL, D_IN), D_MODEL ** -0.5),
        'q_norm': 1.0 + nrm(ks[12], (DEPTH, HEAD_DIM), 0.02),
        'k_norm': 1.0 + nrm(ks[13], (DEPTH, HEAD_DIM), 0.02),
        'ret_decay_f': logit_f[None, :] + nrm(ks[14], (DEPTH, RET_HEADS), 0.1),
        'ret_decay_b': logit_b[None, :] + nrm(ks[15], (DEPTH, RET_HEADS), 0.1),
        'ret_norm': 1.0 + nrm(ks[16], (DEPTH, RET_DV), 0.02),
        'w_br_att': nrm(ks[17], (DEPTH, ATT_Q, D_MODEL), ATT_Q ** -0.5),
        'w_br_ret': nrm(ks[18], (DEPTH, RET_V, D_MODEL), RET_V ** -0.5),
        'w_br_four': nrm(ks[19], (DEPTH, FOURIER_W, D_MODEL), FOURIER_W ** -0.5),
        'w_out': nrm(ks[20], (DEPTH, D_MODEL, D_MODEL), D_MODEL ** -0.5),
        'norm2': 1.0 + nrm(ks[21], (DEPTH, D_MODEL), 0.02),
        'w_up': nrm(ks[22], (DEPTH, D_MODEL, 2 * D_FF), D_MODEL ** -0.5),
        'conv_w': nrm(ks[23], (DEPTH, CONV_W, D_FF), CONV_W ** -0.5),
        'conv_b': nrm(ks[24], (DEPTH, D_FF), 0.02),
        'w_down': nrm(ks[25], (DEPTH, D_FF, D_MODEL), D_FF ** -0.5),
    }


def reference(x_prompt, x_sample, cache_k, cache_v, state_ret_fwd, state_ret_bwd, c, c_ctx,
              w_ada, b_ada, norm1, w_in, q_norm, k_norm, ret_decay_f, ret_decay_b, ret_norm,
              w_br_att, w_br_ret, w_br_four, w_out, norm2, w_up, conv_w, conv_b, w_down):
    rope = axial_rope_tables(x_sample.shape[1])
    cond_ctx = c_ctx[None, :]
    xp, xs = x_prompt, x_sample
    new_k, new_v, new_sf, new_sb = [], [], [], []
    for l in range(DEPTH):
        lw = (w_ada[l], b_ada[l], norm1[l], w_in[l], q_norm[l], k_norm[l], ret_decay_f[l],
              ret_decay_b[l], ret_norm[l], w_br_att[l], w_br_ret[l], w_br_four[l], w_out[l],
              norm2[l], w_up[l], conv_w[l], conv_b[l], w_down[l])
        xp, (k_l, v_l, sf_l, sb_l) = trunk_layer(xp, cond_ctx, None, None, *lw)
        new_k.append(k_l)
        new_v.append(v_l)
        new_sf.append(sf_l)
        new_sb.append(sb_l)
        ctx_l = (cache_k[:, l], cache_v[:, l], state_ret_fwd[:, l], state_ret_bwd[:, l])
        xs, _ = trunk_layer(xs, c, rope, ctx_l, *lw)
    return (xp, xs, jnp.stack(new_k, axis=1), jnp.stack(new_v, axis=1),
            jnp.stack(new_sf, axis=1), jnp.stack(new_sb, axis=1))
```

```python
import functools

import numpy as np
import jax
import jax.numpy as jnp
from jax import lax
from jax.experimental import pallas as pl
from jax.experimental.pallas import tpu as pltpu

D_MODEL = 1024
DEPTH = 2
GRID_W = 64
HEAD_DIM = 64
N_HEADS = 8
N_KV_HEADS = 2
Q_PER_KV = N_HEADS // N_KV_HEADS
ATT_Q = N_HEADS * HEAD_DIM
ATT_KV = N_KV_HEADS * HEAD_DIM
RET_HEADS = 4
RET_DK = 64
RET_DV = 128
RET_QK = RET_HEADS * RET_DK
RET_V = RET_HEADS * RET_DV
FOURIER_GROUPS = 4
FOURIER_GROUP_DIM = 128
FOURIER_W = FOURIER_GROUPS * FOURIER_GROUP_DIM
D_FF = 2816
CHUNK = 128
Q_BLOCK = 128
ROPE_THETA = 10000.0
EPS = 1e-6

F32 = jnp.float32
BF16 = jnp.bfloat16

P_W = 6144
OFF_QA, OFF_VR, OFF_GR, OFF_UF = 0, 512, 1024, 1536
OFF_GATE = 2048
OFF_QR, OFF_KR, OFF_KA, OFF_VA = 5120, 5376, 5632, 5760

VMEM_LIMIT = 56 * 1024 * 1024


def _cparams(sem):
    return pltpu.CompilerParams(dimension_semantics=sem, vmem_limit_bytes=VMEM_LIMIT)


def _dot(a, b):
    return jnp.dot(a, b, preferred_element_type=F32)


def _dot_nt(a, b):
    return lax.dot_general(a, b, (((1,), (1,)), ((), ())), preferred_element_type=F32)


def _ada_kernel(cond_ref, w_ref, b_ref, o_ref):
    cnd = cond_ref[...]
    s = cnd * jax.nn.sigmoid(cnd)
    o_ref[0] = _dot(s.astype(BF16), w_ref[0].astype(BF16)) + b_ref[0]


def _ada(cond_all, w_ada, b_ada):
    n = cond_all.shape[0]
    tn = 1024
    return pl.pallas_call(
        _ada_kernel,
        grid=(DEPTH, 6 * D_MODEL // tn),
        in_specs=[
            pl.BlockSpec((n, D_MODEL), lambda l, j: (0, 0)),
            pl.BlockSpec((1, D_MODEL, tn), lambda l, j: (l, 0, j)),
            pl.BlockSpec((1, 1, tn), lambda l, j: (l, 0, j)),
        ],
        out_specs=pl.BlockSpec((1, n, tn), lambda l, j: (l, 0, j)),
        out_shape=jax.ShapeDtypeStruct((DEPTH, n, 6 * D_MODEL), F32),
        compiler_params=_cparams(("parallel", "parallel")),
        name="ada",
    )(cond_all, w_ada, b_ada.reshape(DEPTH, 1, 6 * D_MODEL))


def _in_kernel(x_ref, sh_ref, sc_ref, g_ref, w_ref, o_ref, h_scr):
    @pl.when(pl.program_id(1) == 0)
    def _():
        x = x_ref[...]
        r = lax.rsqrt(jnp.mean(x * x, axis=-1, keepdims=True) + EPS)
        h = (x * r) * g_ref[...] * (1.0 + sc_ref[0]) + sh_ref[0]
        h_scr[...] = h.astype(BF16)

    o_ref[...] = _dot(h_scr[...], w_ref[...])


def _in_proj(x2d, mod, rows_per_mod, norm1, w_in_p):
    n_tok = x2d.shape[0]
    tm, tn = 1024, 1024
    mod_idx = lambda i: (i * tm) // rows_per_mod
    return pl.pallas_call(
        _in_kernel,
        grid=(n_tok // tm, P_W // tn),
        in_specs=[
            pl.BlockSpec((tm, D_MODEL), lambda i, j: (i, 0)),
            pl.BlockSpec((1, 1, D_MODEL), lambda i, j: (mod_idx(i), 0, 0)),
            pl.BlockSpec((1, 1, D_MODEL), lambda i, j: (mod_idx(i), 0, 1)),
            pl.BlockSpec((1, D_MODEL), lambda i, j: (0, 0)),
            pl.BlockSpec((D_MODEL, tn), lambda i, j: (0, j)),
        ],
        out_specs=pl.BlockSpec((tm, tn), lambda i, j: (i, j)),
        out_shape=jax.ShapeDtypeStruct((n_tok, P_W), F32),
        scratch_shapes=[pltpu.VMEM((tm, D_MODEL), BF16)],
        compiler_params=_cparams(("parallel", "arbitrary")),
        name="in_proj",
    )(x2d, mod, mod, norm1, w_in_p)


def _head_rms(x_h):
    return lax.rsqrt(jnp.mean(x_h * x_h, axis=-1, keepdims=True) + EPS)


def _rope_pair(y, cos_t, sin_t):
    lane = lax.broadcasted_iota(jnp.int32, y.shape, 1)
    partner = jnp.where((lane % 32) < 16, pltpu.roll(y, 128 - 16, 1), pltpu.roll(y, 16, 1))
    return y * cos_t + partner * sin_t


def _softmax_rows(s):
    m = jnp.max(s, axis=-1, keepdims=True)
    e = jnp.exp(s - m)
    return e / jnp.sum(e, axis=-1, keepdims=True)


def _attend(qs, k_bf, v_bf):
    rows = qs[0].shape[0]
    q4 = jnp.concatenate(qs, axis=0).astype(BF16)
    s = _dot_nt(q4, k_bf) * (HEAD_DIM ** -0.5)
    p = _softmax_rows(s).astype(BF16)
    o4 = _dot(p, v_bf)
    return [o4[g * rows:(g + 1) * rows] for g in range(Q_PER_KV)]


def _att_ctx_kernel(q_ref, k_ref, v_ref, qn_ref, kn_ref, o_ref, ko_ref):
    k = k_ref[...]
    v = v_ref[...]
    kn = []
    for kv in range(N_KV_HEADS):
        sl = slice(kv * HEAD_DIM, (kv + 1) * HEAD_DIM)
        k_h = k[:, sl]
        k_h = (k_h * _head_rms(k_h)) * kn_ref[...]
        ko_ref[:, sl] = k_h
        kn.append(k_h.astype(BF16))
    for kv in range(N_KV_HEADS):
        v_bf = v[:, kv * HEAD_DIM:(kv + 1) * HEAD_DIM].astype(BF16)
        qs = []
        for g in range(Q_PER_KV):
            h = kv * Q_PER_KV + g
            q_h = q_ref[:, h * HEAD_DIM:(h + 1) * HEAD_DIM]
            qs.append((q_h * _head_rms(q_h)) * qn_ref[...])
        outs = _attend(qs, kn[kv], v_bf)
        for g in range(Q_PER_KV):
            h = kv * Q_PER_KV + g
            o_ref[:, h * HEAD_DIM:(h + 1) * HEAD_DIM] = outs[g]


def _att_ctx(p_act, seq, q_norm, k_norm):
    n_tok = p_act.shape[0]
    return pl.pallas_call(
        _att_ctx_kernel,
        grid=(n_tok // seq,),
        in_specs=[
            pl.BlockSpec((seq, ATT_Q), lambda b: (b, OFF_QA // ATT_Q)),
            pl.BlockSpec((seq, ATT_KV), lambda b: (b, OFF_KA // ATT_KV)),
            pl.BlockSpec((seq, ATT_KV), lambda b: (b, OFF_VA // ATT_KV)),
            pl.BlockSpec((1, HEAD_DIM), lambda b: (0, 0)),
            pl.BlockSpec((1, HEAD_DIM), lambda b: (0, 0)),
        ],
        out_specs=[
            pl.BlockSpec((seq, ATT_Q), lambda b: (b, 0)),
            pl.BlockSpec((seq, ATT_KV), lambda b: (b, 0)),
        ],
        out_shape=[
            jax.ShapeDtypeStruct((n_tok, ATT_Q), F32),
            jax.ShapeDtypeStruct((n_tok, ATT_KV), F32),
        ],
        compiler_params=_cparams(("parallel",)),
        name="att_ctx",
    )(p_act, p_act, p_act, q_norm, k_norm)


def _att_lat_kernel(q_ref, k_ref, v_ref, ck_ref, cv_ref, qn_ref, kn_ref,
                    cq_ref, sq_ref, ck_t_ref, sk_t_ref, o_ref, kf_scr, vf_scr, *, seq):
    @pl.when(pl.program_id(1) == 0)
    def _():
        k = k_ref[...]
        r = jnp.concatenate(
            [jnp.broadcast_to(_head_rms(k[:, kv * HEAD_DIM:(kv + 1) * HEAD_DIM]), (seq, HEAD_DIM))
             for kv in range(N_KV_HEADS)], axis=1)
        y = (k * r) * kn_ref[...]
        kf_scr[0:seq, :] = _rope_pair(y, ck_t_ref[...], sk_t_ref[...]).astype(BF16)
        kf_scr[seq:, :] = ck_ref[0, 0].astype(BF16)
        vf_scr[0:seq, :] = v_ref[...].astype(BF16)
        vf_scr[seq:, :] = cv_ref[0, 0].astype(BF16)

    cos_t = cq_ref[...]
    sin_t = sq_ref[...]
    q_heads = []
    for c in range(N_HEADS // 2):
        x = q_ref[:, c * 128:(c + 1) * 128]
        rq = x.shape[0]
        r = jnp.concatenate(
            [jnp.broadcast_to(_head_rms(x[:, t * HEAD_DIM:(t + 1) * HEAD_DIM]), (rq, HEAD_DIM))
             for t in range(2)], axis=1)
        z = _rope_pair((x * r) * qn_ref[...], cos_t, sin_t)
        q_heads.append(z[:, :HEAD_DIM])
        q_heads.append(z[:, HEAD_DIM:])
    for kv in range(N_KV_HEADS):
        sl = slice(kv * HEAD_DIM, (kv + 1) * HEAD_DIM)
        outs = _attend(q_heads[kv * Q_PER_KV:(kv + 1) * Q_PER_KV], kf_scr[:, sl], vf_scr[:, sl])
        for g in range(Q_PER_KV):
            h = kv * Q_PER_KV + g
            o_ref[:, h * HEAD_DIM:(h + 1) * HEAD_DIM] = outs[g]


def _att_lat(p_act, seq, cache_k, cache_v, layer, q_norm2, k_norm2, cos_t, sin_t):
    n_tok = p_act.shape[0]
    nb = seq // Q_BLOCK
    past = cache_k.shape[2]
    return pl.pallas_call(
        functools.partial(_att_lat_kernel, seq=seq),
        grid=(n_tok // seq, nb),
        in_specs=[
            pl.BlockSpec((Q_BLOCK, ATT_Q), lambda b, i: (b * nb + i, OFF_QA // ATT_Q)),
            pl.BlockSpec((seq, ATT_KV), lambda b, i: (b, OFF_KA // ATT_KV)),
            pl.BlockSpec((seq, ATT_KV), lambda b, i: (b, OFF_VA // ATT_KV)),
            pl.BlockSpec((1, 1, past, ATT_KV), lambda b, i: (b, layer, 0, 0)),
            pl.BlockSpec((1, 1, past, ATT_KV), lambda b, i: (b, layer, 0, 0)),
            pl.BlockSpec((1, 128), lambda b, i: (0, 0)),
            pl.BlockSpec((1, 128), lambda b, i: (0, 0)),
            pl.BlockSpec((Q_BLOCK, 128), lambda b, i: (i, 0)),
            pl.BlockSpec((Q_BLOCK, 128), lambda b, i: (i, 0)),
            pl.BlockSpec((seq, 128), lambda b, i: (0, 0)),
            pl.BlockSpec((seq, 128), lambda b, i: (0, 0)),
        ],
        out_specs=pl.BlockSpec((Q_BLOCK, ATT_Q), lambda b, i: (b * nb + i, 0)),
        out_shape=jax.ShapeDtypeStruct((n_tok, ATT_Q), F32),
        scratch_shapes=[pltpu.VMEM((seq + past, ATT_KV), BF16),
                        pltpu.VMEM((seq + past, ATT_KV), BF16)],
        compiler_params=_cparams(("parallel", "arbitrary")),
        name="att_lat",
    )(p_act, p_act, p_act, cache_k, cache_v, q_norm2, k_norm2, cos_t, sin_t, cos_t, sin_t)


def _log_sigmoid(d):
    return jnp.minimum(d, 0.0) - jnp.log1p(jnp.exp(-jnp.abs(d)))


def _ret_kernel(q_ref, k_ref, v_ref, g_ref, dec_ref, rn_ref, *rest, seq, has_state):
    if has_state:
        s0f_ref, s0b_ref, o_ref = rest
    else:
        o_ref, sf_ref, sb_ref = rest
    n_chunks = seq // CHUNK
    lg = _log_sigmoid(dec_ref[...])
    ii = lax.broadcasted_iota(jnp.int32, (CHUNK, CHUNK), 0)
    jj = lax.broadcasted_iota(jnp.int32, (CHUNK, CHUNK), 1)
    rel = (ii - jj).astype(F32)
    pos = ii.astype(F32)

    for h in range(RET_HEADS):
        lgf = lg[h:h + 1]
        lgb = lg[RET_HEADS + h:RET_HEADS + h + 1]
        mask = jnp.where(rel > 0, jnp.exp(jnp.maximum(rel, 0.0) * lgf),
                         jnp.where(rel < 0, jnp.exp(jnp.maximum(-rel, 0.0) * lgb), 2.0))
        qdec_f = jnp.exp((pos + 1.0) * lgf)
        kdec_f = jnp.exp((CHUNK - 1.0 - pos) * lgf)[:, :RET_DK]
        cdec_f = jnp.exp(CHUNK * lgf)
        qdec_b = jnp.exp((CHUNK - pos) * lgb)
        kdec_b = jnp.exp(pos * lgb)[:, :RET_DK]
        cdec_b = jnp.exp(CHUNK * lgb)
        qsl = slice(h * RET_DK, (h + 1) * RET_DK)
        vsl = slice(h * RET_DV, (h + 1) * RET_DV)

        def load(c):
            rows = pl.ds(pl.multiple_of(c * CHUNK, CHUNK), CHUNK)
            q = q_ref[rows, qsl].astype(BF16)
            k = k_ref[rows, qsl] * (RET_DK ** -0.5)
            v = v_ref[rows, vsl].astype(BF16)
            return rows, q, k, v

        def fwd(c, state):
            rows, q, k, v = load(c)
            att = _dot_nt(q, k.astype(BF16)) * mask
            o = _dot(att.astype(BF16), v) + _dot(q, state.astype(BF16)) * qdec_f
            o_ref[rows, vsl] = o
            kd = (k * kdec_f).T.astype(BF16)
            return state * cdec_f + _dot(kd, v)

        def bwd(t, state):
            rows, q, k, v = load(n_chunks - 1 - t)
            o_ref[rows, vsl] += _dot(q, state.astype(BF16)) * qdec_b
            kd = (k * kdec_b).T.astype(BF16)
            return state * cdec_b + _dot(kd, v)

        if has_state:
            s0f = s0f_ref[0, 0, h]
            s0b = s0b_ref[0, 0, h]
        else:
            s0f = jnp.zeros((RET_DK, RET_DV), F32)
            s0b = s0f
        s_f = lax.fori_loop(0, n_chunks, fwd, s0f)
        s_b = lax.fori_loop(0, n_chunks, bwd, s0b)
        if not has_state:
            sf_ref[0, h] = s_f
            sb_ref[0, h] = s_b

        o = o_ref[:, vsl]
        o = (o * lax.rsqrt(jnp.mean(o * o, axis=-1, keepdims=True) + EPS)) * rn_ref[...]
        g = g_ref[:, vsl]
        o_ref[:, vsl] = o * (g * jax.nn.sigmoid(g))


def _retention(p_act, seq, dec, ret_norm, states=None, layer=0):
    n_tok = p_act.shape[0]
    n_seq = n_tok // seq
    has_state = states is not None
    in_specs = [
        pl.BlockSpec((seq, RET_QK), lambda s: (s, OFF_QR // RET_QK)),
        pl.BlockSpec((seq, RET_QK), lambda s: (s, OFF_KR // RET_QK)),
        pl.BlockSpec((seq, RET_V), lambda s: (s, OFF_VR // RET_V)),
        pl.BlockSpec((seq, RET_V), lambda s: (s, OFF_GR // RET_V)),
        pl.BlockSpec((8, 128), lambda s: (0, 0)),
        pl.BlockSpec((1, RET_DV), lambda s: (0, 0)),
    ]
    args = [p_act, p_act, p_act, p_act, dec, ret_norm]
    o_spec = pl.BlockSpec((seq, RET_V), lambda s: (s, 0))
    o_shape = jax.ShapeDtypeStruct((n_tok, RET_V), F32)
    if has_state:
        st_spec = pl.BlockSpec((1, 1, RET_HEADS, RET_DK, RET_DV), lambda s: (s, layer, 0, 0, 0))
        in_specs += [st_spec, st_spec]
        args += list(states)
        out_specs, out_shape = o_spec, o_shape
    else:
        st_spec = pl.BlockSpec((1, RET_HEADS, RET_DK, RET_DV), lambda s: (s, 0, 0, 0))
        st_shape = jax.ShapeDtypeStruct((n_seq, RET_HEADS, RET_DK, RET_DV), F32)
        out_specs, out_shape = [o_spec, st_spec, st_spec], [o_shape, st_shape, st_shape]
    return pl.pallas_call(
        functools.partial(_ret_kernel, seq=seq, has_state=has_state),
        grid=(n_seq,),
        in_specs=in_specs,
        out_specs=out_specs,
        out_shape=out_shape,
        compiler_params=_cparams(("parallel",)),
        name="retention_lat" if has_state else "retention_ctx",
    )(*args)


def _split(x):
    hi = x.astype(BF16)
    lo = (x - hi.astype(F32)).astype(BF16)
    return hi, lo


def _dot3(a_hi, a_lo, b_hi, b_lo):
    return _dot(a_hi, b_hi) + (_dot(a_hi, b_lo) + _dot(a_lo, b_hi))


def _four_kernel(u_ref, wc_hi_ref, wc_lo_ref, cn_hi_ref, cn_lo_ref, sn_hi_ref, sn_lo_ref, o_ref):
    gd = FOURIER_GROUP_DIM
    tc, ts = [], []
    for g in range(FOURIER_GROUPS):
        u_hi, u_lo = _split(u_ref[:, g * gd:(g + 1) * gd])
        t = _dot3(u_hi, u_lo, wc_hi_ref[...], wc_lo_ref[...])
        tc.append(t[:, :gd])
        ts.append(t[:, gd:])
    tc_hi, tc_lo = _split(jnp.concatenate(tc, axis=1))
    ts_hi, ts_lo = _split(jnp.concatenate(ts, axis=1))
    o_ref[...] = (_dot3(cn_hi_ref[...], cn_lo_ref[...], tc_hi, tc_lo)
                  + _dot3(sn_hi_ref[...], sn_lo_ref[...], ts_hi, ts_lo))


def _dft_tables(n):
    k = np.arange(n, dtype=np.int64)
    ang = 2.0 * np.pi * ((k[:, None] * k[None, :]) % n).astype(np.float64) / n
    scale = 1.0 / np.sqrt(n)
    return np.cos(ang) * scale, np.sin(ang) * scale


def _hi_lo(x64):
    x = jnp.asarray(x64, F32)
    hi = x.astype(BF16)
    lo = (x - hi.astype(F32)).astype(BF16)
    return hi, lo


def _fourier(p_act, seq):
    n_tok = p_act.shape[0]
    cc, sc = _dft_tables(FOURIER_GROUP_DIM)
    wc_hi, wc_lo = _hi_lo(np.concatenate([cc, -sc], axis=1))
    cn, sn = _dft_tables(seq)
    cn_hi, cn_lo = _hi_lo(cn)
    sn_hi, sn_lo = _hi_lo(sn)
    const = lambda shape: pl.BlockSpec(shape, lambda s: (0, 0))
    return pl.pallas_call(
        _four_kernel,
        grid=(n_tok // seq,),
        in_specs=[
            pl.BlockSpec((seq, FOURIER_W), lambda s: (s, OFF_UF // FOURIER_W)),
            const((FOURIER_GROUP_DIM, 2 * FOURIER_GROUP_DIM)),
            const((FOURIER_GROUP_DIM, 2 * FOURIER_GROUP_DIM)),
            const((seq, seq)), const((seq, seq)), const((seq, seq)), const((seq, seq)),
        ],
        out_specs=pl.BlockSpec((seq, FOURIER_W), lambda s: (s, 0)),
        out_shape=jax.ShapeDtypeStruct((n_tok, FOURIER_W), F32),
        compiler_params=_cparams(("parallel",)),
        name="fourier",
    )(p_act, wc_hi, wc_lo, cn_hi, cn_lo, sn_hi, sn_lo)


def _merge_kernel(oa_ref, or_ref, of_ref, ga_ref, gr_ref, gf_ref, x_ref, g1_ref,
                  wa_ref, wr_ref, wf_ref, wo_ref, o_ref):
    merged = (jax.nn.sigmoid(ga_ref[...]) * _dot(oa_ref[...].astype(BF16), wa_ref[...])
              + jax.nn.sigmoid(gr_ref[...]) * _dot(or_ref[...].astype(BF16), wr_ref[...])
              + jax.nn.sigmoid(gf_ref[...]) * _dot(of_ref[...].astype(BF16), wf_ref[...]))
    o_ref[...] = x_ref[...] + g1_ref[0] * _dot(merged.astype(BF16), wo_ref[...])


def _merge(o_att, o_ret, o_four, p_act, x2d, mod, rows_per_mod, w_att, w_ret, w_four, w_out):
    n_tok = x2d.shape[0]
    tm = 256
    mod_idx = lambda i: (i * tm) // rows_per_mod
    br = lambda: pl.BlockSpec((tm, 512), lambda i: (i, 0))
    gate = lambda k: pl.BlockSpec((tm, D_MODEL), lambda i: (i, OFF_GATE // D_MODEL + k))
    wbr = lambda: pl.BlockSpec((512, D_MODEL), lambda i: (0, 0))
    return pl.pallas_call(
        _merge_kernel,
        grid=(n_tok // tm,),
        in_specs=[
            br(), br(), br(), gate(0), gate(1), gate(2),
            pl.BlockSpec((tm, D_MODEL), lambda i: (i, 0)),
            pl.BlockSpec((1, 1, D_MODEL), lambda i: (mod_idx(i), 0, 2)),
            wbr(), wbr(), wbr(),
            pl.BlockSpec((D_MODEL, D_MODEL), lambda i: (0, 0)),
        ],
        out_specs=pl.BlockSpec((tm, D_MODEL), lambda i: (i, 0)),
        out_shape=jax.ShapeDtypeStruct((n_tok, D_MODEL), F32),
        compiler_params=_cparams(("parallel",)),
        name="merge",
    )(o_att, o_ret, o_four, p_act, p_act, p_act, x2d, mod, w_att, w_ret, w_four, w_out)


def _ffn_kernel(x_ref, sh_ref, sc_ref, g2_ref, n2_ref, wa_ref, wv_ref, cw_ref, cb_ref, wd_ref,
                o_ref, h_scr, acc_scr, *, seq):
    j = pl.program_id(1)

    @pl.when(j == 0)
    def _():
        x = x_ref[...]
        r = lax.rsqrt(jnp.mean(x * x, axis=-1, keepdims=True) + EPS)
        h = (x * r) * n2_ref[...] * (1.0 + sc_ref[0]) + sh_ref[0]
        h_scr[...] = h.astype(BF16)
        acc_scr[...] = jnp.zeros_like(acc_scr)

    h = h_scr[...]
    a = _dot(h, wa_ref[...])
    val = _dot(h, wv_ref[...])
    tm = a.shape[0]
    pos = lax.broadcasted_iota(jnp.int32, a.shape, 0) % seq
    prev = jnp.where(pos == 0, 0.0, pltpu.roll(a, 1, 0))
    nxt = jnp.where(pos == seq - 1, 0.0, pltpu.roll(a, tm - 1, 0))
    cw = cw_ref[...]
    ac = prev * cw[0:1] + a * cw[1:2] + nxt * cw[2:3] + cb_ref[...]
    act = jax.nn.gelu(ac) * val
    acc_scr[...] += _dot(act.astype(BF16), wd_ref[...])

    @pl.when(j == pl.num_programs(1) - 1)
    def _():
        o_ref[...] = x_ref[...] + g2_ref[0] * acc_scr[...]


def _ffn(x2d, seq, mod, rows_per_mod, norm2, w_up, conv_w, conv_b, w_down):
    n_tok = x2d.shape[0]
    tm, tf = 1024, 256
    nf = D_FF // tf
    mod_idx = lambda i: (i * tm) // rows_per_mod
    return pl.pallas_call(
        functools.partial(_ffn_kernel, seq=seq),
        grid=(n_tok // tm, nf),
        in_specs=[
            pl.BlockSpec((tm, D_MODEL), lambda i, j: (i, 0)),
            pl.BlockSpec((1, 1, D_MODEL), lambda i, j: (mod_idx(i), 0, 3)),
            pl.BlockSpec((1, 1, D_MODEL), lambda i, j: (mod_idx(i), 0, 4)),
            pl.BlockSpec((1, 1, D_MODEL), lambda i, j: (mod_idx(i), 0, 5)),
            pl.BlockSpec((1, D_MODEL), lambda i, j: (0, 0)),
            pl.BlockSpec((D_MODEL, tf), lambda i, j: (0, j)),
            pl.BlockSpec((D_MODEL, tf), lambda i, j: (0, nf + j)),
            pl.BlockSpec((3, tf), lambda i, j: (0, j)),
            pl.BlockSpec((1, tf), lambda i, j: (0, j)),
            pl.BlockSpec((tf, D_MODEL), lambda i, j: (j, 0)),
        ],
        out_specs=pl.BlockSpec((tm, D_MODEL), lambda i, j: (i, 0)),
        out_shape=jax.ShapeDtypeStruct((n_tok, D_MODEL), F32),
        scratch_shapes=[pltpu.VMEM((tm, D_MODEL), BF16), pltpu.VMEM((tm, D_MODEL), F32)],
        compiler_params=_cparams(("parallel", "arbitrary")),
        name="ffn",
    )(x2d, mod, mod, mod, norm2, w_up, w_up, conv_w, conv_b, w_down)


def _permute_w_in(w):
    q_a, k_a, v_a = w[:, 0:512], w[:, 512:640], w[:, 640:768]
    q_r, k_r = w[:, 768:1024], w[:, 1024:1280]
    v_r, g_r, u_f, gates = w[:, 1280:1792], w[:, 1792:2304], w[:, 2304:2816], w[:, 2816:5888]
    pad = jnp.zeros((w.shape[0], P_W - 5888), w.dtype)
    return jnp.concatenate([q_a, v_r, g_r, u_f, gates, q_r, k_r, k_a, v_a, pad], axis=1).astype(BF16)


def _rope_tables(n_tok):
    rows = n_tok // GRID_W
    row_id = jnp.repeat(jnp.arange(rows), GRID_W).astype(F32)
    col_id = jnp.tile(jnp.arange(GRID_W), rows).astype(F32)
    n_freq = HEAD_DIM // 4
    inv = ROPE_THETA ** (-jnp.arange(n_freq, dtype=F32) / n_freq)
    ang_r = row_id[:, None] * inv
    ang_c = col_id[:, None] * inv
    cos_h = jnp.concatenate([jnp.cos(ang_r), jnp.cos(ang_r), jnp.cos(ang_c), jnp.cos(ang_c)], axis=1)
    sin_h = jnp.concatenate([-jnp.sin(ang_r), jnp.sin(ang_r), -jnp.sin(ang_c), jnp.sin(ang_c)], axis=1)
    return jnp.tile(cos_h, (1, 2)), jnp.tile(sin_h, (1, 2))


def kernel(x_prompt, x_sample, cache_k, cache_v, state_ret_fwd, state_ret_bwd, c, c_ctx, w_ada, b_ada, norm1, w_in, q_norm, k_norm, ret_decay_f, ret_decay_b, ret_norm, w_br_att, w_br_ret, w_br_four, w_out, norm2, w_up, conv_w, conv_b, w_down):
    batch, seq, _ = x_prompt.shape
    dec_batch, dec_seq, _ = x_sample.shape
    past = cache_k.shape[2]

    cond_all = jnp.concatenate([c_ctx[None, :], c], axis=0)
    mod_all = _ada(cond_all, w_ada, b_ada)
    cos_t, sin_t = _rope_tables(dec_seq)
    ck = cache_k.reshape(dec_batch, DEPTH, past, ATT_KV)
    cv = cache_v.reshape(dec_batch, DEPTH, past, ATT_KV)

    xp = x_prompt.reshape(batch * seq, D_MODEL)
    xs = x_sample.reshape(dec_batch * dec_seq, D_MODEL)
    new_k, new_v, new_sf, new_sb = [], [], [], []
    for l in range(DEPTH):
        w_in_p = _permute_w_in(w_in[l])
        w_att, w_ret, w_four = (w_br_att[l].astype(BF16), w_br_ret[l].astype(BF16),
                                w_br_four[l].astype(BF16))
        w_o, w_u, w_d = w_out[l].astype(BF16), w_up[l].astype(BF16), w_down[l].astype(BF16)
        n1, n2 = norm1[l][None, :], norm2[l][None, :]
        qn, kn = q_norm[l][None, :], k_norm[l][None, :]
        qn2, kn2 = jnp.tile(qn, (1, 2)), jnp.tile(kn, (1, 2))
        rn = ret_norm[l][None, :]
        dec = jnp.broadcast_to(
            jnp.concatenate([ret_decay_f[l], ret_decay_b[l]])[:, None].astype(F32), (2 * RET_HEADS, 128))
        cw, cb = conv_w[l], conv_b[l][None, :]
        mod_ctx = mod_all[l, 0:1].reshape(1, 1, 6 * D_MODEL)
        mod_lat = mod_all[l, 1:].reshape(dec_batch, 1, 6 * D_MODEL)

        p_act = _in_proj(xp, mod_ctx, batch * seq, n1, w_in_p)
        o_att, k_l = _att_ctx(p_act, seq, qn, kn)
        o_ret, sf_l, sb_l = _retention(p_act, seq, dec, rn)
        o_four = _fourier(p_act, seq)
        x1 = _merge(o_att, o_ret, o_four, p_act, xp, mod_ctx, batch * seq, w_att, w_ret, w_four, w_o)
        xp = _ffn(x1, seq, mod_ctx, batch * seq, n2, w_u, cw, cb, w_d)
        new_k.append(k_l.reshape(batch, seq, N_KV_HEADS, HEAD_DIM))
        new_v.append(p_act[:, OFF_VA:OFF_VA + ATT_KV].reshape(batch, seq, N_KV_HEADS, HEAD_DIM))
        new_sf.append(sf_l)
        new_sb.append(sb_l)

        p_act = _in_proj(xs, mod_lat, dec_seq, n1, w_in_p)
        o_att = _att_lat(p_act, dec_seq, ck, cv, l, qn2, kn2, cos_t, sin_t)
        o_ret = _retention(p_act, dec_seq, dec, rn, states=(state_ret_fwd, state_ret_bwd), layer=l)
        o_four = _fourier(p_act, dec_seq)
        x1 = _merge(o_att, o_ret, o_four, p_act, xs, mod_lat, dec_seq, w_att, w_ret, w_four, w_o)
        xs = _ffn(x1, dec_seq, mod_lat, dec_seq, n2, w_u, cw, cb, w_d)

    return (xp.reshape(batch, seq, D_MODEL), xs.reshape(dec_batch, dec_seq, D_MODEL),
            jnp.stack(new_k, axis=1), jnp.stack(new_v, axis=1),
            jnp.stack(new_sf, axis=1), jnp.stack(new_sb, axis=1))
```

```python
import functools

import numpy as np
import jax
import jax.numpy as jnp
from jax import lax
from jax.experimental import pallas as pl
from jax.experimental.pallas import tpu as pltpu

D_MODEL = 1024
DEPTH = 2
GRID_W = 64
HEAD_DIM = 64
N_HEADS = 8
N_KV_HEADS = 2
Q_PER_KV = N_HEADS // N_KV_HEADS
ATT_Q = N_HEADS * HEAD_DIM
ATT_KV = N_KV_HEADS * HEAD_DIM
RET_HEADS = 4
RET_DK = 64
RET_DV = 128
RET_QK = RET_HEADS * RET_DK
RET_V = RET_HEADS * RET_DV
FOURIER_GROUPS = 4
FOURIER_GROUP_DIM = 128
FOURIER_W = FOURIER_GROUPS * FOURIER_GROUP_DIM
D_FF = 2816
CHUNK = 128
Q_BLOCK = 128
ROPE_THETA = 10000.0
EPS = 1e-6

F32 = jnp.float32
BF16 = jnp.bfloat16

P_W = 6144
OFF_QA, OFF_VR, OFF_GR, OFF_UF = 0, 512, 1024, 1536
OFF_GATE = 2048
OFF_QR, OFF_KR, OFF_KA, OFF_VA = 5120, 5376, 5632, 5760

VMEM_LIMIT = 56 * 1024 * 1024


def _cparams(sem):
    return pltpu.CompilerParams(dimension_semantics=sem, vmem_limit_bytes=VMEM_LIMIT)


def _dot(a, b):
    return jnp.dot(a, b, preferred_element_type=F32)


def _dot_nt(a, b):
    return lax.dot_general(a, b, (((1,), (1,)), ((), ())), preferred_element_type=F32)


def _ada_kernel(cond_ref, w_ref, b_ref, o_ref):
    cnd = cond_ref[...]
    s = cnd * jax.nn.sigmoid(cnd)
    o_ref[0] = _dot(s.astype(BF16), w_ref[0].astype(BF16)) + b_ref[0]


def _ada(cond_all, w_ada, b_ada):
    n = cond_all.shape[0]
    tn = 1024
    return pl.pallas_call(
        _ada_kernel,
        grid=(DEPTH, 6 * D_MODEL // tn),
        in_specs=[
            pl.BlockSpec((n, D_MODEL), lambda l, j: (0, 0)),
            pl.BlockSpec((1, D_MODEL, tn), lambda l, j: (l, 0, j)),
            pl.BlockSpec((1, 1, tn), lambda l, j: (l, 0, j)),
        ],
        out_specs=pl.BlockSpec((1, n, tn), lambda l, j: (l, 0, j)),
        out_shape=jax.ShapeDtypeStruct((DEPTH, n, 6 * D_MODEL), F32),
        compiler_params=_cparams(("parallel", "parallel")),
        name="ada",
    )(cond_all, w_ada, b_ada.reshape(DEPTH, 1, 6 * D_MODEL))


def _in_kernel(x_ref, sh_ref, sc_ref, g_ref, w_ref, o_ref, h_scr):
    @pl.when(pl.program_id(1) == 0)
    def _():
        x = x_ref[...]
        r = lax.rsqrt(jnp.mean(x * x, axis=-1, keepdims=True) + EPS)
        h = (x * r) * g_ref[...] * (1.0 + sc_ref[0]) + sh_ref[0]
        h_scr[...] = h.astype(BF16)

    o_ref[...] = _dot(h_scr[...], w_ref[...])


def _in_proj(x2d, mod, rows_per_mod, norm1, w_in_p):
    n_tok = x2d.shape[0]
    tm, tn = 1024, 1024
    mod_idx = lambda i: (i * tm) // rows_per_mod
    return pl.pallas_call(
        _in_kernel,
        grid=(n_tok // tm, P_W // tn),
        in_specs=[
            pl.BlockSpec((tm, D_MODEL), lambda i, j: (i, 0)),
            pl.BlockSpec((1, 1, D_MODEL), lambda i, j: (mod_idx(i), 0, 0)),
            pl.BlockSpec((1, 1, D_MODEL), lambda i, j: (mod_idx(i), 0, 1)),
            pl.BlockSpec((1, D_MODEL), lambda i, j: (0, 0)),
            pl.BlockSpec((D_MODEL, tn), lambda i, j: (0, j)),
        ],
        out_specs=pl.BlockSpec((tm, tn), lambda i, j: (i, j)),
        out_shape=jax.ShapeDtypeStruct((n_tok, P_W), F32),
        scratch_shapes=[pltpu.VMEM((tm, D_MODEL), BF16)],
        compiler_params=_cparams(("parallel", "arbitrary")),
        name="in_proj",
    )(x2d, mod, mod, norm1, w_in_p)


def _norm_rope_heads_t(x_t, n_heads, g_col, cos_t=None, sin_t=None):
    quarter = HEAD_DIM // 4
    outs = []
    for h in range(n_heads):
        x = x_t[h * HEAD_DIM:(h + 1) * HEAD_DIM, :]
        r = lax.rsqrt(jnp.mean(x * x, axis=0, keepdims=True) + EPS)
        y = (x * r) * g_col
        if cos_t is not None:
            pieces = []
            for a in range(2):
                c = cos_t[a * quarter:(a + 1) * quarter]
                s = sin_t[a * quarter:(a + 1) * quarter]
                x1 = y[2 * a * quarter:(2 * a + 1) * quarter]
                x2 = y[(2 * a + 1) * quarter:(2 * a + 2) * quarter]
                pieces += [x1 * c - x2 * s, x2 * c + x1 * s]
            y = jnp.concatenate(pieces, axis=0)
        outs.append(y)
    return outs


def _attend_t(q_heads_t, k_bf, v_t_bf):
    zeros = jnp.zeros((HEAD_DIM, Q_PER_KV * q_heads_t[0].shape[1]), F32)
    tq = q_heads_t[0].shape[1]
    out_rows = []
    for kv in range(N_KV_HEADS):
        q_kv = jnp.concatenate(q_heads_t[kv * Q_PER_KV:(kv + 1) * Q_PER_KV], axis=1) * (HEAD_DIM ** -0.5)
        rhs = jnp.concatenate([q_kv, zeros] if kv == 0 else [zeros, q_kv], axis=0).astype(BF16)
        s_t = _dot(k_bf, rhs)
        e = jnp.exp(s_t - jnp.max(s_t, axis=0, keepdims=True))
        inv = 1.0 / jnp.sum(e, axis=0, keepdims=True)
        o_t = _dot(v_t_bf[kv * HEAD_DIM:(kv + 1) * HEAD_DIM, :], e.astype(BF16)) * inv
        out_rows += [o_t[:, g * tq:(g + 1) * tq] for g in range(Q_PER_KV)]
    return jnp.concatenate(out_rows, axis=0)


def _att_ctx_kernel(q_ref, k_ref, v_ref, qn_ref, kn_ref, o_ref, ko_ref):
    k_heads = _norm_rope_heads_t(k_ref[...].T, N_KV_HEADS, kn_ref[...])
    k_n = jnp.concatenate(k_heads, axis=0).T
    ko_ref[...] = k_n
    q_heads = _norm_rope_heads_t(q_ref[...].T, N_HEADS, qn_ref[...])
    o_t = _attend_t(q_heads, k_n.astype(BF16), v_ref[...].T.astype(BF16))
    o_ref[...] = o_t.T


def _att_ctx(p_act, seq, q_norm, k_norm):
    n_tok = p_act.shape[0]
    return pl.pallas_call(
        _att_ctx_kernel,
        grid=(n_tok // seq,),
        in_specs=[
            pl.BlockSpec((seq, ATT_Q), lambda b: (b, OFF_QA // ATT_Q)),
            pl.BlockSpec((seq, ATT_KV), lambda b: (b, OFF_KA // ATT_KV)),
            pl.BlockSpec((seq, ATT_KV), lambda b: (b, OFF_VA // ATT_KV)),
            pl.BlockSpec((HEAD_DIM, 1), lambda b: (0, 0)),
            pl.BlockSpec((HEAD_DIM, 1), lambda b: (0, 0)),
        ],
        out_specs=[
            pl.BlockSpec((seq, ATT_Q), lambda b: (b, 0)),
            pl.BlockSpec((seq, ATT_KV), lambda b: (b, 0)),
        ],
        out_shape=[
            jax.ShapeDtypeStruct((n_tok, ATT_Q), F32),
            jax.ShapeDtypeStruct((n_tok, ATT_KV), F32),
        ],
        compiler_params=_cparams(("parallel",)),
        name="att_ctx",
    )(p_act, p_act, p_act, q_norm, k_norm)


def _att_lat_kernel(q_ref, k_ref, v_ref, ck_ref, cv_ref, qn_ref, kn_ref,
                    cq_ref, sq_ref, ck_t_ref, sk_t_ref, o_ref, kf_scr, vt_scr, *, seq):
    @pl.when(pl.program_id(1) == 0)
    def _():
        k_heads = _norm_rope_heads_t(k_ref[...].T, N_KV_HEADS, kn_ref[...], ck_t_ref[...], sk_t_ref[...])
        kf_scr[0:seq, :] = jnp.concatenate(k_heads, axis=0).T.astype(BF16)
        kf_scr[seq:, :] = ck_ref[0, 0].astype(BF16)
        vt_scr[:, 0:seq] = v_ref[...].T.astype(BF16)
        vt_scr[:, seq:] = cv_ref[0, 0].T.astype(BF16)

    q_heads = _norm_rope_heads_t(q_ref[...].T, N_HEADS, qn_ref[...], cq_ref[...], sq_ref[...])
    o_ref[...] = _attend_t(q_heads, kf_scr[...], vt_scr[...]).T


def _att_lat(p_act, seq, cache_k, cache_v, layer, q_norm, k_norm, cos_t, sin_t):
    n_tok = p_act.shape[0]
    nb = seq // Q_BLOCK
    past = cache_k.shape[2]
    return pl.pallas_call(
        functools.partial(_att_lat_kernel, seq=seq),
        grid=(n_tok // seq, nb),
        in_specs=[
            pl.BlockSpec((Q_BLOCK, ATT_Q), lambda b, i: (b * nb + i, OFF_QA // ATT_Q)),
            pl.BlockSpec((seq, ATT_KV), lambda b, i: (b, OFF_KA // ATT_KV)),
            pl.BlockSpec((seq, ATT_KV), lambda b, i: (b, OFF_VA // ATT_KV)),
            pl.BlockSpec((1, 1, past, ATT_KV), lambda b, i: (b, layer, 0, 0)),
            pl.BlockSpec((1, 1, past, ATT_KV), lambda b, i: (b, layer, 0, 0)),
            pl.BlockSpec((HEAD_DIM, 1), lambda b, i: (0, 0)),
            pl.BlockSpec((HEAD_DIM, 1), lambda b, i: (0, 0)),
            pl.BlockSpec((HEAD_DIM // 2, Q_BLOCK), lambda b, i: (0, i)),
            pl.BlockSpec((HEAD_DIM // 2, Q_BLOCK), lambda b, i: (0, i)),
            pl.BlockSpec((HEAD_DIM // 2, seq), lambda b, i: (0, 0)),
            pl.BlockSpec((HEAD_DIM // 2, seq), lambda b, i: (0, 0)),
        ],
        out_specs=pl.BlockSpec((Q_BLOCK, ATT_Q), lambda b, i: (b * nb + i, 0)),
        out_shape=jax.ShapeDtypeStruct((n_tok, ATT_Q), F32),
        scratch_shapes=[pltpu.VMEM((seq + past, ATT_KV), BF16),
                        pltpu.VMEM((ATT_KV, seq + past), BF16)],
        compiler_params=_cparams(("parallel", "arbitrary")),
        name="att_lat",
    )(p_act, p_act, p_act, cache_k, cache_v, q_norm, k_norm, cos_t, sin_t, cos_t, sin_t)


def _log_sigmoid(d):
    return jnp.minimum(d, 0.0) - jnp.log1p(jnp.exp(-jnp.abs(d)))


def _ret_kernel(q_ref, k_ref, v_ref, g_ref, dec_ref, rn_ref, *rest, seq, has_state):
    if has_state:
        s0f_ref, s0b_ref, o_ref = rest
    else:
        o_ref, sf_ref, sb_ref = rest
    hp = pl.program_id(1)
    n_chunks = seq // CHUNK
    ii = lax.broadcasted_iota(jnp.int32, (CHUNK, CHUNK), 0)
    jj = lax.broadcasted_iota(jnp.int32, (CHUNK, CHUNK), 1)
    rel = (ii - jj).astype(F32)
    row = ii.astype(F32)
    lane = jj.astype(F32)
    lgf = [_log_sigmoid(dec_ref[pl.ds(2 * hp + t, 1), :]) for t in range(2)]
    lgb = [_log_sigmoid(dec_ref[pl.ds(RET_HEADS + 2 * hp + t, 1), :]) for t in range(2)]

    mask2 = jnp.concatenate(
        [jnp.where(rel > 0, jnp.exp(jnp.maximum(rel, 0.0) * lgf[t]),
                   jnp.where(rel < 0, jnp.exp(jnp.maximum(-rel, 0.0) * lgb[t]), 2.0))
         for t in range(2)], axis=0)
    qdec_f = [jnp.exp((row + 1.0) * lgf[t]) for t in range(2)]
    qdec_b = [jnp.exp((CHUNK - row) * lgb[t]) for t in range(2)]
    cdec_f = [jnp.exp(CHUNK * lgf[t]) for t in range(2)]
    cdec_b = [jnp.exp(CHUNK * lgb[t]) for t in range(2)]
    kdec_f = jnp.exp((CHUNK - 1.0 - lane) * jnp.where(ii < RET_DK, lgf[0], lgf[1]))
    kdec_b = jnp.exp(lane * jnp.where(ii < RET_DK, lgb[0], lgb[1]))

    k_t = (k_ref[...] * (RET_DK ** -0.5)).T
    first_head = jj < RET_DK

    def chunk(c):
        return slice(c * CHUNK, (c + 1) * CHUNK)

    kv_f = [[None] * n_chunks for _ in range(2)]
    kv_b = [[None] * n_chunks for _ in range(2)]
    for c in range(n_chunks):
        k_c = k_t[:, chunk(c)]
        kd = jnp.concatenate([k_c * kdec_f, k_c * kdec_b], axis=0).astype(BF16)
        kv = _dot(kd, v_ref[chunk(c), :].astype(BF16))
        for t in range(2):
            kv_f[t][c] = kv[t * RET_DK:(t + 1) * RET_DK, t * RET_DV:(t + 1) * RET_DV]
            kv_b[t][c] = kv[CHUNK + t * RET_DK:CHUNK + (t + 1) * RET_DK, t * RET_DV:(t + 1) * RET_DV]

    st_f = [[None] * n_chunks for _ in range(2)]
    st_b = [[None] * n_chunks for _ in range(2)]
    for t in range(2):
        if has_state:
            s_f = s0f_ref[0, 0, t]
            s_b = s0b_ref[0, 0, t]
        else:
            s_f = jnp.zeros((RET_DK, RET_DV), F32)
            s_b = s_f
        for c in range(n_chunks):
            st_f[t][c] = s_f
            s_f = s_f * cdec_f[t] + kv_f[t][c]
        for c in reversed(range(n_chunks)):
            st_b[t][c] = s_b
            s_b = s_b * cdec_b[t] + kv_b[t][c]
        if not has_state:
            sf_ref[0, t] = s_f
            sb_ref[0, t] = s_b

    for c in range(n_chunks):
        q_c = q_ref[chunk(c), :]
        qm = jnp.concatenate([jnp.where(first_head, q_c, 0.0), jnp.where(first_head, 0.0, q_c)],
                             axis=0).astype(BF16)
        att = (_dot(qm, k_t[:, chunk(c)].astype(BF16)) * mask2).astype(BF16)
        states = jnp.concatenate(
            [jnp.concatenate([st_f[t][c], st_b[t][c]], axis=1) for t in range(2)], axis=0)
        qs = _dot(qm, states.astype(BF16))
        v_c = v_ref[chunk(c), :].astype(BF16)
        for t in range(2):
            rows = slice(t * CHUNK, (t + 1) * CHUNK)
            vsl = slice(t * RET_DV, (t + 1) * RET_DV)
            o = (_dot(att[rows], v_c[:, vsl]) + qs[rows, :RET_DV] * qdec_f[t]
                 + qs[rows, RET_DV:] * qdec_b[t])
            o = (o * lax.rsqrt(jnp.mean(o * o, axis=-1, keepdims=True) + EPS)) * rn_ref[...]
            g = g_ref[chunk(c), vsl]
            o_ref[chunk(c), vsl] = o * (g * jax.nn.sigmoid(g))


def _retention(p_act, seq, dec, ret_norm, states=None, layer=0):
    n_tok = p_act.shape[0]
    n_seq = n_tok // seq
    has_state = states is not None
    pair_qk, pair_v = 2 * RET_DK, 2 * RET_DV
    in_specs = [
        pl.BlockSpec((seq, pair_qk), lambda s, p: (s, OFF_QR // pair_qk + p)),
        pl.BlockSpec((seq, pair_qk), lambda s, p: (s, OFF_KR // pair_qk + p)),
        pl.BlockSpec((seq, pair_v), lambda s, p: (s, OFF_VR // pair_v + p)),
        pl.BlockSpec((seq, pair_v), lambda s, p: (s, OFF_GR // pair_v + p)),
        pl.BlockSpec((8, 128), lambda s, p: (0, 0)),
        pl.BlockSpec((1, RET_DV), lambda s, p: (0, 0)),
    ]
    args = [p_act, p_act, p_act, p_act, dec, ret_norm]
    o_spec = pl.BlockSpec((seq, pair_v), lambda s, p: (s, p))
    o_shape = jax.ShapeDtypeStruct((n_tok, RET_V), F32)
    if has_state:
        st_spec = pl.BlockSpec((1, 1, 2, RET_DK, RET_DV), lambda s, p: (s, layer, p, 0, 0))
        in_specs += [st_spec, st_spec]
        args += list(states)
        out_specs, out_shape = o_spec, o_shape
    else:
        st_spec = pl.BlockSpec((1, 2, RET_DK, RET_DV), lambda s, p: (s, p, 0, 0))
        st_shape = jax.ShapeDtypeStruct((n_seq, RET_HEADS, RET_DK, RET_DV), F32)
        out_specs, out_shape = [o_spec, st_spec, st_spec], [o_shape, st_shape, st_shape]
    return pl.pallas_call(
        functools.partial(_ret_kernel, seq=seq, has_state=has_state),
        grid=(n_seq, RET_HEADS // 2),
        in_specs=in_specs,
        out_specs=out_specs,
        out_shape=out_shape,
        compiler_params=_cparams(("parallel", "parallel")),
        name="retention_lat" if has_state else "retention_ctx",
    )(*args)


def _split(x):
    hi = x.astype(BF16)
    lo = (x - hi.astype(F32)).astype(BF16)
    return hi, lo


def _dot3(a_hi, a_lo, b_hi, b_lo):
    return _dot(a_hi, b_hi) + (_dot(a_hi, b_lo) + _dot(a_lo, b_hi))


def _four_kernel(u_ref, wc_hi_ref, wc_lo_ref, cn_hi_ref, cn_lo_ref, sn_hi_ref, sn_lo_ref, o_ref):
    gd = FOURIER_GROUP_DIM
    tc, ts = [], []
    for g in range(FOURIER_GROUPS):
        u_hi, u_lo = _split(u_ref[:, g * gd:(g + 1) * gd])
        t = _dot3(u_hi, u_lo, wc_hi_ref[...], wc_lo_ref[...])
        tc.append(t[:, :gd])
        ts.append(t[:, gd:])
    tc_hi, tc_lo = _split(jnp.concatenate(tc, axis=1))
    ts_hi, ts_lo = _split(jnp.concatenate(ts, axis=1))
    o_ref[...] = (_dot3(cn_hi_ref[...], cn_lo_ref[...], tc_hi, tc_lo)
                  + _dot3(sn_hi_ref[...], sn_lo_ref[...], ts_hi, ts_lo))


def _dft_tables(n):
    k = np.arange(n, dtype=np.int64)
    ang = 2.0 * np.pi * ((k[:, None] * k[None, :]) % n).astype(np.float64) / n
    scale = 1.0 / np.sqrt(n)
    return np.cos(ang) * scale, np.sin(ang) * scale


def _hi_lo(x64):
    x = jnp.asarray(x64, F32)
    hi = x.astype(BF16)
    lo = (x - hi.astype(F32)).astype(BF16)
    return hi, lo


def _fourier(p_act, seq):
    n_tok = p_act.shape[0]
    cc, sc = _dft_tables(FOURIER_GROUP_DIM)
    wc_hi, wc_lo = _hi_lo(np.concatenate([cc, -sc], axis=1))
    cn, sn = _dft_tables(seq)
    cn_hi, cn_lo = _hi_lo(cn)
    sn_hi, sn_lo = _hi_lo(sn)
    const = lambda shape: pl.BlockSpec(shape, lambda s: (0, 0))
    return pl.pallas_call(
        _four_kernel,
        grid=(n_tok // seq,),
        in_specs=[
            pl.BlockSpec((seq, FOURIER_W), lambda s: (s, OFF_UF // FOURIER_W)),
            const((FOURIER_GROUP_DIM, 2 * FOURIER_GROUP_DIM)),
            const((FOURIER_GROUP_DIM, 2 * FOURIER_GROUP_DIM)),
            const((seq, seq)), const((seq, seq)), const((seq, seq)), const((seq, seq)),
        ],
        out_specs=pl.BlockSpec((seq, FOURIER_W), lambda s: (s, 0)),
        out_shape=jax.ShapeDtypeStruct((n_tok, FOURIER_W), F32),
        compiler_params=_cparams(("parallel",)),
        name="fourier",
    )(p_act, wc_hi, wc_lo, cn_hi, cn_lo, sn_hi, sn_lo)


def _merge_kernel(oa_ref, or_ref, of_ref, ga_ref, gr_ref, gf_ref, x_ref, g1_ref,
                  wa_ref, wr_ref, wf_ref, wo_ref, o_ref):
    merged = (jax.nn.sigmoid(ga_ref[...]) * _dot(oa_ref[...].astype(BF16), wa_ref[...])
              + jax.nn.sigmoid(gr_ref[...]) * _dot(or_ref[...].astype(BF16), wr_ref[...])
              + jax.nn.sigmoid(gf_ref[...]) * _dot(of_ref[...].astype(BF16), wf_ref[...]))
    o_ref[...] = x_ref[...] + g1_ref[0] * _dot(merged.astype(BF16), wo_ref[...])


def _merge(o_att, o_ret, o_four, p_act, x2d, mod, rows_per_mod, w_att, w_ret, w_four, w_out):
    n_tok = x2d.shape[0]
    tm = 256
    mod_idx = lambda i: (i * tm) // rows_per_mod
    br = lambda: pl.BlockSpec((tm, 512), lambda i: (i, 0))
    gate = lambda k: pl.BlockSpec((tm, D_MODEL), lambda i: (i, OFF_GATE // D_MODEL + k))
    wbr = lambda: pl.BlockSpec((512, D_MODEL), lambda i: (0, 0))
    return pl.pallas_call(
        _merge_kernel,
        grid=(n_tok // tm,),
        in_specs=[
            br(), br(), br(), gate(0), gate(1), gate(2),
            pl.BlockSpec((tm, D_MODEL), lambda i: (i, 0)),
            pl.BlockSpec((1, 1, D_MODEL), lambda i: (mod_idx(i), 0, 2)),
            wbr(), wbr(), wbr(),
            pl.BlockSpec((D_MODEL, D_MODEL), lambda i: (0, 0)),
        ],
        out_specs=pl.BlockSpec((tm, D_MODEL), lambda i: (i, 0)),
        out_shape=jax.ShapeDtypeStruct((n_tok, D_MODEL), F32),
        compiler_params=_cparams(("parallel",)),
        name="merge",
    )(o_att, o_ret, o_four, p_act, p_act, p_act, x2d, mod, w_att, w_ret, w_four, w_out)


def _ffn_kernel(x_ref, sh_ref, sc_ref, g2_ref, n2_ref, wa_ref, wv_ref, cw_ref, cb_ref, wd_ref,
                o_ref, h_scr, acc_scr, *, seq):
    j = pl.program_id(1)

    @pl.when(j == 0)
    def _():
        x = x_ref[...]
        r = lax.rsqrt(jnp.mean(x * x, axis=-1, keepdims=True) + EPS)
        h = (x * r) * n2_ref[...] * (1.0 + sc_ref[0]) + sh_ref[0]
        h_scr[...] = h.astype(BF16)
        acc_scr[...] = jnp.zeros_like(acc_scr)

    h = h_scr[...]
    a = _dot(h, wa_ref[...])
    val = _dot(h, wv_ref[...])
    tm = a.shape[0]
    pos = lax.broadcasted_iota(jnp.int32, a.shape, 0) % seq
    prev = jnp.where(pos == 0, 0.0, pltpu.roll(a, 1, 0))
    nxt = jnp.where(pos == seq - 1, 0.0, pltpu.roll(a, tm - 1, 0))
    cw = cw_ref[...]
    ac = prev * cw[0:1] + a * cw[1:2] + nxt * cw[2:3] + cb_ref[...]
    act = jax.nn.gelu(ac) * val
    acc_scr[...] += _dot(act.astype(BF16), wd_ref[...])

    @pl.when(j == pl.num_programs(1) - 1)
    def _():
        o_ref[...] = x_ref[...] + g2_ref[0] * acc_scr[...]


def _ffn(x2d, seq, mod, rows_per_mod, norm2, w_up, conv_w, conv_b, w_down):
    n_tok = x2d.shape[0]
    tm, tf = 1024, 256
    nf = D_FF // tf
    mod_idx = lambda i: (i * tm) // rows_per_mod
    return pl.pallas_call(
        functools.partial(_ffn_kernel, seq=seq),
        grid=(n_tok // tm, nf),
        in_specs=[
            pl.BlockSpec((tm, D_MODEL), lambda i, j: (i, 0)),
            pl.BlockSpec((1, 1, D_MODEL), lambda i, j: (mod_idx(i), 0, 3)),
            pl.BlockSpec((1, 1, D_MODEL), lambda i, j: (mod_idx(i), 0, 4)),
            pl.BlockSpec((1, 1, D_MODEL), lambda i, j: (mod_idx(i), 0, 5)),
            pl.BlockSpec((1, D_MODEL), lambda i, j: (0, 0)),
            pl.BlockSpec((D_MODEL, tf), lambda i, j: (0, j)),
            pl.BlockSpec((D_MODEL, tf), lambda i, j: (0, nf + j)),
            pl.BlockSpec((3, tf), lambda i, j: (0, j)),
            pl.BlockSpec((1, tf), lambda i, j: (0, j)),
            pl.BlockSpec((tf, D_MODEL), lambda i, j: (j, 0)),
        ],
        out_specs=pl.BlockSpec((tm, D_MODEL), lambda i, j: (i, 0)),
        out_shape=jax.ShapeDtypeStruct((n_tok, D_MODEL), F32),
        scratch_shapes=[pltpu.VMEM((tm, D_MODEL), BF16), pltpu.VMEM((tm, D_MODEL), F32)],
        compiler_params=_cparams(("parallel", "arbitrary")),
        name="ffn",
    )(x2d, mod, mod, mod, norm2, w_up, w_up, conv_w, conv_b, w_down)


def _permute_w_in(w):
    q_a, k_a, v_a = w[:, 0:512], w[:, 512:640], w[:, 640:768]
    q_r, k_r = w[:, 768:1024], w[:, 1024:1280]
    v_r, g_r, u_f, gates = w[:, 1280:1792], w[:, 1792:2304], w[:, 2304:2816], w[:, 2816:5888]
    pad = jnp.zeros((w.shape[0], P_W - 5888), w.dtype)
    return jnp.concatenate([q_a, v_r, g_r, u_f, gates, q_r, k_r, k_a, v_a, pad], axis=1).astype(BF16)


def _rope_tables(n_tok):
    rows = n_tok // GRID_W
    row_id = jnp.repeat(jnp.arange(rows), GRID_W).astype(F32)
    col_id = jnp.tile(jnp.arange(GRID_W), rows).astype(F32)
    n_freq = HEAD_DIM // 4
    inv = ROPE_THETA ** (-jnp.arange(n_freq, dtype=F32) / n_freq)
    ang = jnp.concatenate([row_id[None, :] * inv[:, None], col_id[None, :] * inv[:, None]], axis=0)
    return jnp.cos(ang), jnp.sin(ang)


def kernel(x_prompt, x_sample, cache_k, cache_v, state_ret_fwd, state_ret_bwd, c, c_ctx, w_ada, b_ada, norm1, w_in, q_norm, k_norm, ret_decay_f, ret_decay_b, ret_norm, w_br_att, w_br_ret, w_br_four, w_out, norm2, w_up, conv_w, conv_b, w_down):
    batch, seq, _ = x_prompt.shape
    dec_batch, dec_seq, _ = x_sample.shape
    past = cache_k.shape[2]

    cond_all = jnp.concatenate([c_ctx[None, :], c], axis=0)
    mod_all = _ada(cond_all, w_ada, b_ada)
    cos_t, sin_t = _rope_tables(dec_seq)
    ck = cache_k.reshape(dec_batch, DEPTH, past, ATT_KV)
    cv = cache_v.reshape(dec_batch, DEPTH, past, ATT_KV)

    xp = x_prompt.reshape(batch * seq, D_MODEL)
    xs = x_sample.reshape(dec_batch * dec_seq, D_MODEL)
    new_k, new_v, new_sf, new_sb = [], [], [], []
    for l in range(DEPTH):
        w_in_p = _permute_w_in(w_in[l])
        w_att, w_ret, w_four = (w_br_att[l].astype(BF16), w_br_ret[l].astype(BF16),
                                w_br_four[l].astype(BF16))
        w_o, w_u, w_d = w_out[l].astype(BF16), w_up[l].astype(BF16), w_down[l].astype(BF16)
        n1, n2 = norm1[l][None, :], norm2[l][None, :]
        qn, kn = q_norm[l][:, None], k_norm[l][:, None]
        rn = ret_norm[l][None, :]
        dec = jnp.broadcast_to(
            jnp.concatenate([ret_decay_f[l], ret_decay_b[l]])[:, None].astype(F32), (2 * RET_HEADS, 128))
        cw, cb = conv_w[l], conv_b[l][None, :]
        mod_ctx = mod_all[l, 0:1].reshape(1, 1, 6 * D_MODEL)
        mod_lat = mod_all[l, 1:].reshape(dec_batch, 1, 6 * D_MODEL)

        p_act = _in_proj(xp, mod_ctx, batch * seq, n1, w_in_p)
        o_att, k_l = _att_ctx(p_act, seq, qn, kn)
        o_ret, sf_l, sb_l = _retention(p_act, seq, dec, rn)
        o_four = _fourier(p_act, seq)
        x1 = _merge(o_att, o_ret, o_four, p_act, xp, mod_ctx, batch * seq, w_att, w_ret, w_four, w_o)
        xp = _ffn(x1, seq, mod_ctx, batch * seq, n2, w_u, cw, cb, w_d)
        new_k.append(k_l.reshape(batch, seq, N_KV_HEADS, HEAD_DIM))
        new_v.append(p_act[:, OFF_VA:OFF_VA + ATT_KV].reshape(batch, seq, N_KV_HEADS, HEAD_DIM))
        new_sf.append(sf_l)
        new_sb.append(sb_l)

        p_act = _in_proj(xs, mod_lat, dec_seq, n1, w_in_p)
        o_att = _att_lat(p_act, dec_seq, ck, cv, l, qn, kn, cos_t, sin_t)
        o_ret = _retention(p_act, dec_seq, dec, rn, states=(state_ret_fwd, state_ret_bwd), layer=l)
        o_four = _fourier(p_act, dec_seq)
        x1 = _merge(o_att, o_ret, o_four, p_act, xs, mod_lat, dec_seq, w_att, w_ret, w_four, w_o)
        xs = _ffn(x1, dec_seq, mod_lat, dec_seq, n2, w_u, cw, cb, w_d)

    return (xp.reshape(batch, seq, D_MODEL), xs.reshape(dec_batch, dec_seq, D_MODEL),
            jnp.stack(new_k, axis=1), jnp.stack(new_v, axis=1),
            jnp.stack(new_sf, axis=1), jnp.stack(new_sb, axis=1))
```

```python
import functools

import numpy as np
import jax
import jax.numpy as jnp
from jax import lax
from jax.experimental import pallas as pl
from jax.experimental.pallas import tpu as pltpu

D_MODEL = 1024
DEPTH = 2
GRID_W = 64
HEAD_DIM = 64
N_HEADS = 8
N_KV_HEADS = 2
Q_PER_KV = N_HEADS // N_KV_HEADS
ATT_Q = N_HEADS * HEAD_DIM
ATT_KV = N_KV_HEADS * HEAD_DIM
RET_HEADS = 4
RET_DK = 64
RET_DV = 128
RET_QK = RET_HEADS * RET_DK
RET_V = RET_HEADS * RET_DV
FOURIER_GROUPS = 4
FOURIER_GROUP_DIM = 128
FOURIER_W = FOURIER_GROUPS * FOURIER_GROUP_DIM
D_FF = 2816
CHUNK = 128
Q_BLOCK = 128
ROPE_THETA = 10000.0
EPS = 1e-6

F32 = jnp.float32
BF16 = jnp.bfloat16

OFF_QA, OFF_KA, OFF_VA = 0, 512, 640
OFF_QR, OFF_KR, OFF_VR, OFF_GR, OFF_UF = 768, 1024, 1280, 1792, 2304
W_IN_SPLIT = 2816
OFF_GATE = 3072
P_W = OFF_GATE + 3 * D_MODEL

VMEM_LIMIT = 56 * 1024 * 1024


def _cparams(sem):
    return pltpu.CompilerParams(dimension_semantics=sem, vmem_limit_bytes=VMEM_LIMIT)


def _dot(a, b):
    return jnp.dot(a, b, preferred_element_type=F32)


def _ada_kernel(cond_ref, w_ref, b_ref, o_ref):
    cnd = cond_ref[...]
    s = cnd * jax.nn.sigmoid(cnd)
    o_ref[0] = _dot(s.astype(BF16), w_ref[0].astype(BF16)) + b_ref[0]


def _ada(cond_all, w_ada, b_ada):
    n = cond_all.shape[0]
    tn = 1024
    return pl.pallas_call(
        _ada_kernel,
        grid=(DEPTH, 6 * D_MODEL // tn),
        in_specs=[
            pl.BlockSpec((n, D_MODEL), lambda l, j: (0, 0)),
            pl.BlockSpec((1, D_MODEL, tn), lambda l, j: (l, 0, j)),
            pl.BlockSpec((1, 1, tn), lambda l, j: (l, 0, j)),
        ],
        out_specs=pl.BlockSpec((1, n, tn), lambda l, j: (l, 0, j)),
        out_shape=jax.ShapeDtypeStruct((DEPTH, n, 6 * D_MODEL), F32),
        compiler_params=_cparams(("parallel", "parallel")),
        name="ada",
    )(cond_all, w_ada, b_ada.reshape(DEPTH, 1, 6 * D_MODEL))


def _in_kernel(x_ref, sh_ref, sc_ref, g_ref, w_ref, o_ref, h_scr):
    @pl.when(pl.program_id(1) == 0)
    def _():
        x = x_ref[...]
        r = lax.rsqrt(jnp.mean(x * x, axis=-1, keepdims=True) + EPS)
        h = (x * r) * g_ref[...] * (1.0 + sc_ref[0]) + sh_ref[0]
        h_scr[...] = h.astype(BF16)

    o_ref[...] = _dot(h_scr[...], w_ref[...])


def _in_proj(x2d, mod, rows_per_mod, norm1, w_in_p):
    n_tok = x2d.shape[0]
    tm, tn = 1024, 1024
    mod_idx = lambda i: (i * tm) // rows_per_mod
    return pl.pallas_call(
        _in_kernel,
        grid=(n_tok // tm, P_W // tn),
        in_specs=[
            pl.BlockSpec((tm, D_MODEL), lambda i, j: (i, 0)),
            pl.BlockSpec((1, 1, D_MODEL), lambda i, j: (mod_idx(i), 0, 0)),
            pl.BlockSpec((1, 1, D_MODEL), lambda i, j: (mod_idx(i), 0, 1)),
            pl.BlockSpec((1, D_MODEL), lambda i, j: (0, 0)),
            pl.BlockSpec((D_MODEL, tn), lambda i, j: (0, j)),
        ],
        out_specs=pl.BlockSpec((tm, tn), lambda i, j: (i, j)),
        out_shape=jax.ShapeDtypeStruct((n_tok, P_W), F32),
        scratch_shapes=[pltpu.VMEM((tm, D_MODEL), BF16)],
        compiler_params=_cparams(("parallel", "arbitrary")),
        name="in_proj",
    )(x2d, mod, mod, norm1, w_in_p)


def _norm_rope_heads_t(x_t, n_heads, g_col, cos_t=None, sin_t=None):
    quarter = HEAD_DIM // 4
    outs = []
    for h in range(n_heads):
        x = x_t[h * HEAD_DIM:(h + 1) * HEAD_DIM, :]
        r = lax.rsqrt(jnp.mean(x * x, axis=0, keepdims=True) + EPS)
        y = (x * r) * g_col
        if cos_t is not None:
            pieces = []
            for a in range(2):
                c = cos_t[a * quarter:(a + 1) * quarter]
                s = sin_t[a * quarter:(a + 1) * quarter]
                x1 = y[2 * a * quarter:(2 * a + 1) * quarter]
                x2 = y[(2 * a + 1) * quarter:(2 * a + 2) * quarter]
                pieces += [x1 * c - x2 * s, x2 * c + x1 * s]
            y = jnp.concatenate(pieces, axis=0)
        outs.append(y)
    return outs


def _attend_t(q_heads_t, k_bf, v_t_bf):
    zeros = jnp.zeros((HEAD_DIM, Q_PER_KV * q_heads_t[0].shape[1]), F32)
    tq = q_heads_t[0].shape[1]
    out_rows = []
    for kv in range(N_KV_HEADS):
        q_kv = jnp.concatenate(q_heads_t[kv * Q_PER_KV:(kv + 1) * Q_PER_KV], axis=1) * (HEAD_DIM ** -0.5)
        rhs = jnp.concatenate([q_kv, zeros] if kv == 0 else [zeros, q_kv], axis=0).astype(BF16)
        s_t = _dot(k_bf, rhs)
        e = jnp.exp(s_t - jnp.max(s_t, axis=0, keepdims=True))
        inv = 1.0 / jnp.sum(e, axis=0, keepdims=True)
        o_t = _dot(v_t_bf[kv * HEAD_DIM:(kv + 1) * HEAD_DIM, :], e.astype(BF16)) * inv
        out_rows += [o_t[:, g * tq:(g + 1) * tq] for g in range(Q_PER_KV)]
    return jnp.concatenate(out_rows, axis=0)


def _att_ctx_kernel(q_ref, k_ref, v_ref, qn_ref, kn_ref, o_ref, ko_ref):
    k_heads = _norm_rope_heads_t(k_ref[...].T, N_KV_HEADS, kn_ref[...])
    k_n = jnp.concatenate(k_heads, axis=0).T
    ko_ref[...] = k_n
    q_heads = _norm_rope_heads_t(q_ref[...].T, N_HEADS, qn_ref[...])
    o_t = _attend_t(q_heads, k_n.astype(BF16), v_ref[...].T.astype(BF16))
    o_ref[...] = o_t.T.astype(o_ref.dtype)


def _att_ctx(p_act, seq, q_norm, k_norm):
    n_tok = p_act.shape[0]
    return pl.pallas_call(
        _att_ctx_kernel,
        grid=(n_tok // seq,),
        in_specs=[
            pl.BlockSpec((seq, ATT_Q), lambda b: (b, OFF_QA // ATT_Q)),
            pl.BlockSpec((seq, ATT_KV), lambda b: (b, OFF_KA // ATT_KV)),
            pl.BlockSpec((seq, ATT_KV), lambda b: (b, OFF_VA // ATT_KV)),
            pl.BlockSpec((HEAD_DIM, 1), lambda b: (0, 0)),
            pl.BlockSpec((HEAD_DIM, 1), lambda b: (0, 0)),
        ],
        out_specs=[
            pl.BlockSpec((seq, ATT_Q), lambda b: (b, 0)),
            pl.BlockSpec((seq, ATT_KV), lambda b: (b, 0)),
        ],
        out_shape=[
            jax.ShapeDtypeStruct((n_tok, ATT_Q), BF16),
            jax.ShapeDtypeStruct((n_tok, ATT_KV), F32),
        ],
        compiler_params=_cparams(("parallel",)),
        name="att_ctx",
    )(p_act, p_act, p_act, q_norm, k_norm)


def _att_lat_kernel(q_ref, k_ref, v_ref, ck_ref, cv_ref, qn_ref, kn_ref,
                    cq_ref, sq_ref, ck_t_ref, sk_t_ref, o_ref, kf_scr, vt_scr, *, seq):
    @pl.when(pl.program_id(1) == 0)
    def _():
        k_heads = _norm_rope_heads_t(k_ref[...].T, N_KV_HEADS, kn_ref[...], ck_t_ref[...], sk_t_ref[...])
        kf_scr[0:seq, :] = jnp.concatenate(k_heads, axis=0).T.astype(BF16)
        kf_scr[seq:, :] = ck_ref[0, 0].astype(BF16)
        vt_scr[:, 0:seq] = v_ref[...].T.astype(BF16)
        vt_scr[:, seq:] = cv_ref[0, 0].T.astype(BF16)

    q_heads = _norm_rope_heads_t(q_ref[...].T, N_HEADS, qn_ref[...], cq_ref[...], sq_ref[...])
    o_ref[...] = _attend_t(q_heads, kf_scr[...], vt_scr[...]).T.astype(o_ref.dtype)


def _att_lat(p_act, seq, cache_k, cache_v, layer, q_norm, k_norm, cos_t, sin_t):
    n_tok = p_act.shape[0]
    nb = seq // Q_BLOCK
    past = cache_k.shape[2]
    return pl.pallas_call(
        functools.partial(_att_lat_kernel, seq=seq),
        grid=(n_tok // seq, nb),
        in_specs=[
            pl.BlockSpec((Q_BLOCK, ATT_Q), lambda b, i: (b * nb + i, OFF_QA // ATT_Q)),
            pl.BlockSpec((seq, ATT_KV), lambda b, i: (b, OFF_KA // ATT_KV)),
            pl.BlockSpec((seq, ATT_KV), lambda b, i: (b, OFF_VA // ATT_KV)),
            pl.BlockSpec((1, 1, past, ATT_KV), lambda b, i: (b, layer, 0, 0)),
            pl.BlockSpec((1, 1, past, ATT_KV), lambda b, i: (b, layer, 0, 0)),
            pl.BlockSpec((HEAD_DIM, 1), lambda b, i: (0, 0)),
            pl.BlockSpec((HEAD_DIM, 1), lambda b, i: (0, 0)),
            pl.BlockSpec((HEAD_DIM // 2, Q_BLOCK), lambda b, i: (0, i)),
            pl.BlockSpec((HEAD_DIM // 2, Q_BLOCK), lambda b, i: (0, i)),
            pl.BlockSpec((HEAD_DIM // 2, seq), lambda b, i: (0, 0)),
            pl.BlockSpec((HEAD_DIM // 2, seq), lambda b, i: (0, 0)),
        ],
        out_specs=pl.BlockSpec((Q_BLOCK, ATT_Q), lambda b, i: (b * nb + i, 0)),
        out_shape=jax.ShapeDtypeStruct((n_tok, ATT_Q), BF16),
        scratch_shapes=[pltpu.VMEM((seq + past, ATT_KV), BF16),
                        pltpu.VMEM((ATT_KV, seq + past), BF16)],
        compiler_params=_cparams(("parallel", "arbitrary")),
        name="att_lat",
    )(p_act, p_act, p_act, cache_k, cache_v, q_norm, k_norm, cos_t, sin_t, cos_t, sin_t)


def _log_sigmoid(d):
    return jnp.minimum(d, 0.0) - jnp.log1p(jnp.exp(-jnp.abs(d)))


def _ret_kernel(q_ref, k_ref, v_ref, g_ref, dec_ref, rn_ref, *rest, seq, has_state):
    if has_state:
        s0f_ref, s0b_ref, o_ref = rest
    else:
        o_ref, sf_ref, sb_ref = rest
    hp = pl.program_id(1)
    n_chunks = seq // CHUNK
    ii = lax.broadcasted_iota(jnp.int32, (CHUNK, CHUNK), 0)
    jj = lax.broadcasted_iota(jnp.int32, (CHUNK, CHUNK), 1)
    rel = (ii - jj).astype(F32)
    row = ii.astype(F32)
    lane = jj.astype(F32)
    lgf = [_log_sigmoid(dec_ref[pl.ds(2 * hp + t, 1), :]) for t in range(2)]
    lgb = [_log_sigmoid(dec_ref[pl.ds(RET_HEADS + 2 * hp + t, 1), :]) for t in range(2)]

    mask2 = jnp.concatenate(
        [jnp.where(rel > 0, jnp.exp(jnp.maximum(rel, 0.0) * lgf[t]),
                   jnp.where(rel < 0, jnp.exp(jnp.maximum(-rel, 0.0) * lgb[t]), 2.0))
         for t in range(2)], axis=0)
    qdec_f = [jnp.exp((row + 1.0) * lgf[t]) for t in range(2)]
    qdec_b = [jnp.exp((CHUNK - row) * lgb[t]) for t in range(2)]
    cdec_f = [jnp.exp(CHUNK * lgf[t]) for t in range(2)]
    cdec_b = [jnp.exp(CHUNK * lgb[t]) for t in range(2)]
    kdec_f = jnp.exp((CHUNK - 1.0 - lane) * jnp.where(ii < RET_DK, lgf[0], lgf[1]))
    kdec_b = jnp.exp(lane * jnp.where(ii < RET_DK, lgb[0], lgb[1]))

    k_t = (k_ref[...] * (RET_DK ** -0.5)).T
    first_head = jj < RET_DK

    def chunk(c):
        return slice(c * CHUNK, (c + 1) * CHUNK)

    kv_f = [[None] * n_chunks for _ in range(2)]
    kv_b = [[None] * n_chunks for _ in range(2)]
    for c in range(n_chunks):
        k_c = k_t[:, chunk(c)]
        kd = jnp.concatenate([k_c * kdec_f, k_c * kdec_b], axis=0).astype(BF16)
        kv = _dot(kd, v_ref[chunk(c), :].astype(BF16))
        for t in range(2):
            kv_f[t][c] = kv[t * RET_DK:(t + 1) * RET_DK, t * RET_DV:(t + 1) * RET_DV]
            kv_b[t][c] = kv[CHUNK + t * RET_DK:CHUNK + (t + 1) * RET_DK, t * RET_DV:(t + 1) * RET_DV]

    st_f = [[None] * n_chunks for _ in range(2)]
    st_b = [[None] * n_chunks for _ in range(2)]
    for t in range(2):
        if has_state:
            s_f = s0f_ref[0, 0, t]
            s_b = s0b_ref[0, 0, t]
        else:
            s_f = jnp.zeros((RET_DK, RET_DV), F32)
            s_b = s_f
        for c in range(n_chunks):
            st_f[t][c] = s_f
            s_f = s_f * cdec_f[t] + kv_f[t][c]
        for c in reversed(range(n_chunks)):
            st_b[t][c] = s_b
            s_b = s_b * cdec_b[t] + kv_b[t][c]
        if not has_state:
            sf_ref[0, t] = s_f
            sb_ref[0, t] = s_b

    for c in range(n_chunks):
        q_c = q_ref[chunk(c), :]
        qm = jnp.concatenate([jnp.where(first_head, q_c, 0.0), jnp.where(first_head, 0.0, q_c)],
                             axis=0).astype(BF16)
        att = (_dot(qm, k_t[:, chunk(c)].astype(BF16)) * mask2).astype(BF16)
        states = jnp.concatenate(
            [jnp.concatenate([st_f[t][c], st_b[t][c]], axis=1) for t in range(2)], axis=0)
        qs = _dot(qm, states.astype(BF16))
        v_c = v_ref[chunk(c), :].astype(BF16)
        for t in range(2):
            rows = slice(t * CHUNK, (t + 1) * CHUNK)
            vsl = slice(t * RET_DV, (t + 1) * RET_DV)
            o = (_dot(att[rows], v_c[:, vsl]) + qs[rows, :RET_DV] * qdec_f[t]
                 + qs[rows, RET_DV:] * qdec_b[t])
            o = (o * lax.rsqrt(jnp.mean(o * o, axis=-1, keepdims=True) + EPS)) * rn_ref[...]
            g = g_ref[chunk(c), vsl]
            o_ref[chunk(c), vsl] = (o * (g * jax.nn.sigmoid(g))).astype(o_ref.dtype)


def _retention(p_act, seq, dec, ret_norm, states=None, layer=0):
    n_tok = p_act.shape[0]
    n_seq = n_tok // seq
    has_state = states is not None
    pair_qk, pair_v = 2 * RET_DK, 2 * RET_DV
    in_specs = [
        pl.BlockSpec((seq, pair_qk), lambda s, p: (s, OFF_QR // pair_qk + p)),
        pl.BlockSpec((seq, pair_qk), lambda s, p: (s, OFF_KR // pair_qk + p)),
        pl.BlockSpec((seq, pair_v), lambda s, p: (s, OFF_VR // pair_v + p)),
        pl.BlockSpec((seq, pair_v), lambda s, p: (s, OFF_GR // pair_v + p)),
        pl.BlockSpec((8, 128), lambda s, p: (0, 0)),
        pl.BlockSpec((1, RET_DV), lambda s, p: (0, 0)),
    ]
    args = [p_act, p_act, p_act, p_act, dec, ret_norm]
    o_spec = pl.BlockSpec((seq, pair_v), lambda s, p: (s, p))
    o_shape = jax.ShapeDtypeStruct((n_tok, RET_V), BF16)
    if has_state:
        st_spec = pl.BlockSpec((1, 1, 2, RET_DK, RET_DV), lambda s, p: (s, layer, p, 0, 0))
        in_specs += [st_spec, st_spec]
        args += list(states)
        out_specs, out_shape = o_spec, o_shape
    else:
        st_spec = pl.BlockSpec((1, 2, RET_DK, RET_DV), lambda s, p: (s, p, 0, 0))
        st_shape = jax.ShapeDtypeStruct((n_seq, RET_HEADS, RET_DK, RET_DV), F32)
        out_specs, out_shape = [o_spec, st_spec, st_spec], [o_shape, st_shape, st_shape]
    return pl.pallas_call(
        functools.partial(_ret_kernel, seq=seq, has_state=has_state),
        grid=(n_seq, RET_HEADS // 2),
        in_specs=in_specs,
        out_specs=out_specs,
        out_shape=out_shape,
        compiler_params=_cparams(("parallel", "parallel")),
        name="retention_lat" if has_state else "retention_ctx",
    )(*args)


def _four_kernel(u0_ref, u1_ref, wc_ref, csn_ref, o_ref):
    gd = FOURIER_GROUP_DIM
    tc, ts = [], []
    for g in range(FOURIER_GROUPS):
        u_ref = (u0_ref, u1_ref)[g // 2]
        u_g = u_ref[:, (g % 2) * gd:(g % 2 + 1) * gd].astype(BF16)
        t = _dot(u_g, wc_ref[...])
        tc.append(t[:, :gd])
        ts.append(t[:, gd:])
    t_all = jnp.concatenate([jnp.concatenate(tc, axis=1), jnp.concatenate(ts, axis=1)], axis=0)
    o_ref[...] = _dot(csn_ref[...], t_all.astype(BF16)).astype(o_ref.dtype)


def _dft_tables(n):
    k = np.arange(n, dtype=np.int64)
    ang = 2.0 * np.pi * ((k[:, None] * k[None, :]) % n).astype(np.float64) / n
    scale = 1.0 / np.sqrt(n)
    return np.cos(ang) * scale, np.sin(ang) * scale


def _fourier(p_act, seq):
    n_tok = p_act.shape[0]
    cc, sc = _dft_tables(FOURIER_GROUP_DIM)
    wc = jnp.asarray(np.concatenate([cc, -sc], axis=1), F32).astype(BF16)
    cn, sn = _dft_tables(seq)
    csn = jnp.asarray(np.concatenate([cn, sn], axis=1), F32).astype(BF16)
    const = lambda shape: pl.BlockSpec(shape, lambda s: (0, 0))
    half = FOURIER_W // 2
    return pl.pallas_call(
        _four_kernel,
        grid=(n_tok // seq,),
        in_specs=[
            pl.BlockSpec((seq, half), lambda s: (s, OFF_UF // half)),
            pl.BlockSpec((seq, half), lambda s: (s, OFF_UF // half + 1)),
            const((FOURIER_GROUP_DIM, 2 * FOURIER_GROUP_DIM)),
            const((seq, 2 * seq)),
        ],
        out_specs=pl.BlockSpec((seq, FOURIER_W), lambda s: (s, 0)),
        out_shape=jax.ShapeDtypeStruct((n_tok, FOURIER_W), BF16),
        compiler_params=_cparams(("parallel",)),
        name="fourier",
    )(p_act, p_act, wc, csn)


MIX_TM = 1024
MERGE_STEPS = 4
MERGE_TM = MIX_TM // MERGE_STEPS
FF_TILE = 256
FF_STEPS = D_FF // FF_TILE


def _mix_kernel(oa_ref, or_ref, of_ref, ga_ref, gr_ref, gf_ref, x_ref, g1_ref, sh_ref, sc_ref,
                g2_ref, n2_ref, wba_ref, wbr_ref, wbf_ref, wo_ref, wa_ref, wv_ref, cw_ref, cb_ref,
                wd_ref, o_ref, x1_scr, h_scr, acc_scr, *, seq):
    j = pl.program_id(1)

    @pl.when(j < MERGE_STEPS)
    def _():
        rows = pl.ds(pl.multiple_of(j * MERGE_TM, MERGE_TM), MERGE_TM)
        merged = (jax.nn.sigmoid(ga_ref[...]) * _dot(oa_ref[...], wba_ref[...])
                  + jax.nn.sigmoid(gr_ref[...]) * _dot(or_ref[...], wbr_ref[...])
                  + jax.nn.sigmoid(gf_ref[...]) * _dot(of_ref[...], wbf_ref[...]))
        x1 = x_ref[...] + g1_ref[0] * _dot(merged.astype(BF16), wo_ref[...])
        x1_scr[rows, :] = x1
        r = lax.rsqrt(jnp.mean(x1 * x1, axis=-1, keepdims=True) + EPS)
        h = (x1 * r) * n2_ref[...] * (1.0 + sc_ref[0]) + sh_ref[0]
        h_scr[rows, :] = h.astype(BF16)
        acc_scr[rows, :] = jnp.zeros((MERGE_TM, D_MODEL), F32)

    @pl.when(j >= MERGE_STEPS)
    def _():
        h = h_scr[...]
        a = _dot(h, wa_ref[...])
        val = _dot(h, wv_ref[...])
        pos = lax.broadcasted_iota(jnp.int32, a.shape, 0) % seq
        prev = jnp.where(pos == 0, 0.0, pltpu.roll(a, 1, 0))
        nxt = jnp.where(pos == seq - 1, 0.0, pltpu.roll(a, MIX_TM - 1, 0))
        cw = cw_ref[...]
        ac = prev * cw[0:1] + a * cw[1:2] + nxt * cw[2:3] + cb_ref[...]
        act = jax.nn.gelu(ac) * val
        acc_scr[...] += _dot(act.astype(BF16), wd_ref[...])

    @pl.when(j == pl.num_programs(1) - 1)
    def _():
        o_ref[...] = x1_scr[...] + g2_ref[0] * acc_scr[...]


def _mix(o_att, o_ret, o_four, p_act, x2d, seq, mod, rows_per_mod, w_att, w_ret, w_four, w_out,
         norm2, w_up, conv_w, conv_b, w_down):
    n_tok = x2d.shape[0]
    mod_idx = lambda i: (i * MIX_TM) // rows_per_mod
    sub = lambda i, j: i * MERGE_STEPS + jnp.minimum(j, MERGE_STEPS - 1)
    ff = lambda j: jnp.maximum(j - MERGE_STEPS, 0)
    br = lambda: pl.BlockSpec((MERGE_TM, 512), lambda i, j: (sub(i, j), 0))
    gate = lambda k: pl.BlockSpec((MERGE_TM, D_MODEL), lambda i, j: (sub(i, j), OFF_GATE // D_MODEL + k))
    modv = lambda k: pl.BlockSpec((1, 1, D_MODEL), lambda i, j: (mod_idx(i), 0, k))
    const = lambda shape: pl.BlockSpec(shape, lambda i, j: (0, 0))
    return pl.pallas_call(
        functools.partial(_mix_kernel, seq=seq),
        grid=(n_tok // MIX_TM, MERGE_STEPS + FF_STEPS),
        in_specs=[
            br(), br(), br(), gate(0), gate(1), gate(2),
            pl.BlockSpec((MERGE_TM, D_MODEL), lambda i, j: (sub(i, j), 0)),
            modv(2), modv(3), modv(4), modv(5),
            const((1, D_MODEL)),
            const((512, D_MODEL)), const((512, D_MODEL)), const((512, D_MODEL)),
            const((D_MODEL, D_MODEL)),
            pl.BlockSpec((D_MODEL, FF_TILE), lambda i, j: (0, ff(j))),
            pl.BlockSpec((D_MODEL, FF_TILE), lambda i, j: (0, FF_STEPS + ff(j))),
            pl.BlockSpec((3, FF_TILE), lambda i, j: (0, ff(j))),
            pl.BlockSpec((1, FF_TILE), lambda i, j: (0, ff(j))),
            pl.BlockSpec((FF_TILE, D_MODEL), lambda i, j: (ff(j), 0)),
        ],
        out_specs=pl.BlockSpec((MIX_TM, D_MODEL), lambda i, j: (i, 0)),
        out_shape=jax.ShapeDtypeStruct((n_tok, D_MODEL), F32),
        scratch_shapes=[pltpu.VMEM((MIX_TM, D_MODEL), F32), pltpu.VMEM((MIX_TM, D_MODEL), BF16),
                        pltpu.VMEM((MIX_TM, D_MODEL), F32)],
        compiler_params=_cparams(("parallel", "arbitrary")),
        name="mix",
    )(o_att, o_ret, o_four, p_act, p_act, p_act, x2d, mod, mod, mod, mod, norm2,
      w_att, w_ret, w_four, w_out, w_up, w_up, conv_w, conv_b, w_down)


def _pad_w_in(w):
    pad = jnp.zeros((w.shape[0], OFF_GATE - W_IN_SPLIT), BF16)
    return jnp.concatenate([w[:, :W_IN_SPLIT].astype(BF16), pad, w[:, W_IN_SPLIT:].astype(BF16)], axis=1)


def _rope_tables(n_tok):
    rows = n_tok // GRID_W
    row_id = jnp.repeat(jnp.arange(rows), GRID_W).astype(F32)
    col_id = jnp.tile(jnp.arange(GRID_W), rows).astype(F32)
    n_freq = HEAD_DIM // 4
    inv = ROPE_THETA ** (-jnp.arange(n_freq, dtype=F32) / n_freq)
    ang = jnp.concatenate([row_id[None, :] * inv[:, None], col_id[None, :] * inv[:, None]], axis=0)
    return jnp.cos(ang), jnp.sin(ang)


def kernel(x_prompt, x_sample, cache_k, cache_v, state_ret_fwd, state_ret_bwd, c, c_ctx, w_ada, b_ada, norm1, w_in, q_norm, k_norm, ret_decay_f, ret_decay_b, ret_norm, w_br_att, w_br_ret, w_br_four, w_out, norm2, w_up, conv_w, conv_b, w_down):
    batch, seq, _ = x_prompt.shape
    dec_batch, dec_seq, _ = x_sample.shape
    past = cache_k.shape[2]

    cond_all = jnp.concatenate([c_ctx[None, :], c], axis=0)
    mod_all = _ada(cond_all, w_ada, b_ada)
    cos_t, sin_t = _rope_tables(dec_seq)
    ck = cache_k.reshape(dec_batch, DEPTH, past, ATT_KV)
    cv = cache_v.reshape(dec_batch, DEPTH, past, ATT_KV)

    xp = x_prompt.reshape(batch * seq, D_MODEL)
    xs = x_sample.reshape(dec_batch * dec_seq, D_MODEL)
    new_k, new_v, new_sf, new_sb = [], [], [], []
    for l in range(DEPTH):
        w_in_p = _pad_w_in(w_in[l])
        w_att, w_ret, w_four = (w_br_att[l].astype(BF16), w_br_ret[l].astype(BF16),
                                w_br_four[l].astype(BF16))
        w_o, w_u, w_d = w_out[l].astype(BF16), w_up[l].astype(BF16), w_down[l].astype(BF16)
        n1, n2 = norm1[l][None, :], norm2[l][None, :]
        qn, kn = q_norm[l][:, None], k_norm[l][:, None]
        rn = ret_norm[l][None, :]
        dec = jnp.broadcast_to(
            jnp.concatenate([ret_decay_f[l], ret_decay_b[l]])[:, None].astype(F32), (2 * RET_HEADS, 128))
        cw, cb = conv_w[l], conv_b[l][None, :]
        mod_ctx = mod_all[l, 0:1].reshape(1, 1, 6 * D_MODEL)
        mod_lat = mod_all[l, 1:].reshape(dec_batch, 1, 6 * D_MODEL)

        p_act = _in_proj(xp, mod_ctx, batch * seq, n1, w_in_p)
        o_att, k_l = _att_ctx(p_act, seq, qn, kn)
        o_ret, sf_l, sb_l = _retention(p_act, seq, dec, rn)
        o_four = _fourier(p_act, seq)
        xp = _mix(o_att, o_ret, o_four, p_act, xp, seq, mod_ctx, batch * seq,
                  w_att, w_ret, w_four, w_o, n2, w_u, cw, cb, w_d)
        new_k.append(k_l.reshape(batch, seq, N_KV_HEADS, HEAD_DIM))
        new_v.append(p_act[:, OFF_VA:OFF_VA + ATT_KV].reshape(batch, seq, N_KV_HEADS, HEAD_DIM))
        new_sf.append(sf_l)
        new_sb.append(sb_l)

        p_act = _in_proj(xs, mod_lat, dec_seq, n1, w_in_p)
        o_att = _att_lat(p_act, dec_seq, ck, cv, l, qn, kn, cos_t, sin_t)
        o_ret = _retention(p_act, dec_seq, dec, rn, states=(state_ret_fwd, state_ret_bwd), layer=l)
        o_four = _fourier(p_act, dec_seq)
        xs = _mix(o_att, o_ret, o_four, p_act, xs, dec_seq, mod_lat, dec_seq,
                  w_att, w_ret, w_four, w_o, n2, w_u, cw, cb, w_d)

    return (xp.reshape(batch, seq, D_MODEL), xs.reshape(dec_batch, dec_seq, D_MODEL),
            jnp.stack(new_k, axis=1), jnp.stack(new_v, axis=1),
            jnp.stack(new_sf, axis=1), jnp.stack(new_sb, axis=1))
```

```python
import functools

import numpy as np
import jax
import jax.numpy as jnp
from jax import lax
from jax.experimental import pallas as pl
from jax.experimental.pallas import tpu as pltpu

D_MODEL = 1024
DEPTH = 2
GRID_W = 64
HEAD_DIM = 64
N_HEADS = 8
N_KV_HEADS = 2
Q_PER_KV = N_HEADS // N_KV_HEADS
ATT_Q = N_HEADS * HEAD_DIM
ATT_KV = N_KV_HEADS * HEAD_DIM
RET_HEADS = 4
RET_DK = 64
RET_DV = 128
RET_QK = RET_HEADS * RET_DK
RET_V = RET_HEADS * RET_DV
FOURIER_GROUPS = 4
FOURIER_GROUP_DIM = 128
FOURIER_W = FOURIER_GROUPS * FOURIER_GROUP_DIM
D_FF = 2816
CHUNK = 128
Q_BLOCK = 128
ROPE_THETA = 10000.0
EPS = 1e-6

F32 = jnp.float32
BF16 = jnp.bfloat16

OFF_QA, OFF_KA, OFF_VA = 0, 512, 640
OFF_QR, OFF_KR, OFF_VR, OFF_GR, OFF_UF = 768, 1024, 1280, 1792, 2304
W_IN_SPLIT = 2816
OFF_GATE = 3072
P_W = OFF_GATE + 3 * D_MODEL

VMEM_LIMIT = 56 * 1024 * 1024


def _cparams(sem, flags=None):
    return pltpu.CompilerParams(dimension_semantics=sem, vmem_limit_bytes=VMEM_LIMIT, flags=flags)


def _dot(a, b):
    return jnp.dot(a, b, preferred_element_type=F32)


def _ada_kernel(cond_ref, w_ref, b_ref, o_ref):
    cnd = cond_ref[...]
    s = cnd * jax.nn.sigmoid(cnd)
    o_ref[0] = _dot(s.astype(BF16), w_ref[0].astype(BF16)) + b_ref[0]


def _ada(cond_all, w_ada, b_ada):
    n = cond_all.shape[0]
    tn = 1024
    return pl.pallas_call(
        _ada_kernel,
        grid=(DEPTH, 6 * D_MODEL // tn),
        in_specs=[
            pl.BlockSpec((n, D_MODEL), lambda l, j: (0, 0)),
            pl.BlockSpec((1, D_MODEL, tn), lambda l, j: (l, 0, j)),
            pl.BlockSpec((1, 1, tn), lambda l, j: (l, 0, j)),
        ],
        out_specs=pl.BlockSpec((1, n, tn), lambda l, j: (l, 0, j)),
        out_shape=jax.ShapeDtypeStruct((DEPTH, n, 6 * D_MODEL), F32),
        compiler_params=_cparams(("parallel", "parallel")),
        name="ada",
    )(cond_all, w_ada, b_ada.reshape(DEPTH, 1, 6 * D_MODEL))


IN_TN = 1024
MAIN_STEPS = OFF_GATE // IN_TN


def _in_kernel(x_ref, sh_ref, sc_ref, g_ref, w_ref, om_ref, og_ref, h_scr):
    j = pl.program_id(1)

    @pl.when(j == 0)
    def _():
        x = x_ref[...]
        r = lax.rsqrt(jnp.mean(x * x, axis=-1, keepdims=True) + EPS)
        h = (x * r) * g_ref[...] * (1.0 + sc_ref[0]) + sh_ref[0]
        h_scr[...] = h.astype(BF16)

    y = _dot(h_scr[...], w_ref[...])

    @pl.when(j < MAIN_STEPS)
    def _():
        om_ref[...] = y

    @pl.when(j >= MAIN_STEPS)
    def _():
        og_ref[...] = y.astype(og_ref.dtype)


def _in_proj(x2d, mod, rows_per_mod, norm1, w_in_p):
    n_tok = x2d.shape[0]
    tm, tn = 1024, IN_TN
    mod_idx = lambda i: (i * tm) // rows_per_mod
    return pl.pallas_call(
        _in_kernel,
        grid=(n_tok // tm, P_W // tn),
        in_specs=[
            pl.BlockSpec((tm, D_MODEL), lambda i, j: (i, 0)),
            pl.BlockSpec((1, 1, D_MODEL), lambda i, j: (mod_idx(i), 0, 0)),
            pl.BlockSpec((1, 1, D_MODEL), lambda i, j: (mod_idx(i), 0, 1)),
            pl.BlockSpec((1, D_MODEL), lambda i, j: (0, 0)),
            pl.BlockSpec((D_MODEL, tn), lambda i, j: (0, j)),
        ],
        out_specs=[
            pl.BlockSpec((tm, tn), lambda i, j: (i, jnp.minimum(j, MAIN_STEPS - 1))),
            pl.BlockSpec((tm, tn), lambda i, j: (i, jnp.maximum(j - MAIN_STEPS, 0))),
        ],
        out_shape=[jax.ShapeDtypeStruct((n_tok, OFF_GATE), F32),
                   jax.ShapeDtypeStruct((n_tok, P_W - OFF_GATE), BF16)],
        scratch_shapes=[pltpu.VMEM((tm, D_MODEL), BF16)],
        compiler_params=_cparams(("parallel", "arbitrary")),
        name="in_proj",
    )(x2d, mod, mod, norm1, w_in_p)


def _norm_rope_heads_t(x_t, n_heads, g_col, cos_t=None, sin_t=None):
    quarter = HEAD_DIM // 4
    outs = []
    for h in range(n_heads):
        x = x_t[h * HEAD_DIM:(h + 1) * HEAD_DIM, :]
        r = lax.rsqrt(jnp.mean(x * x, axis=0, keepdims=True) + EPS)
        y = (x * r) * g_col
        if cos_t is not None:
            pieces = []
            for a in range(2):
                c = cos_t[a * quarter:(a + 1) * quarter]
                s = sin_t[a * quarter:(a + 1) * quarter]
                x1 = y[2 * a * quarter:(2 * a + 1) * quarter]
                x2 = y[(2 * a + 1) * quarter:(2 * a + 2) * quarter]
                pieces += [x1 * c - x2 * s, x2 * c + x1 * s]
            y = jnp.concatenate(pieces, axis=0)
        outs.append(y)
    return outs


def _attend_t(q_heads_t, k_bf, v_t_bf):
    zeros = jnp.zeros((HEAD_DIM, Q_PER_KV * q_heads_t[0].shape[1]), F32)
    tq = q_heads_t[0].shape[1]
    out_rows = []
    for kv in range(N_KV_HEADS):
        q_kv = jnp.concatenate(q_heads_t[kv * Q_PER_KV:(kv + 1) * Q_PER_KV], axis=1) * (HEAD_DIM ** -0.5)
        rhs = jnp.concatenate([q_kv, zeros] if kv == 0 else [zeros, q_kv], axis=0).astype(BF16)
        s_t = _dot(k_bf, rhs)
        e = jnp.exp(s_t - jnp.max(s_t, axis=0, keepdims=True))
        inv = 1.0 / jnp.sum(e, axis=0, keepdims=True)
        o_t = _dot(v_t_bf[kv * HEAD_DIM:(kv + 1) * HEAD_DIM, :], e.astype(BF16)) * inv
        out_rows += [o_t[:, g * tq:(g + 1) * tq] for g in range(Q_PER_KV)]
    return jnp.concatenate(out_rows, axis=0)


def _att_ctx_kernel(q_ref, k_ref, v_ref, qn_ref, kn_ref, o_ref, ko_ref):
    k_heads = _norm_rope_heads_t(k_ref[...].T, N_KV_HEADS, kn_ref[...])
    k_n = jnp.concatenate(k_heads, axis=0).T
    ko_ref[...] = k_n
    q_heads = _norm_rope_heads_t(q_ref[...].T, N_HEADS, qn_ref[...])
    o_t = _attend_t(q_heads, k_n.astype(BF16), v_ref[...].T.astype(BF16))
    o_ref[...] = o_t.T.astype(o_ref.dtype)


def _att_ctx(p_act, seq, q_norm, k_norm):
    n_tok = p_act.shape[0]
    return pl.pallas_call(
        _att_ctx_kernel,
        grid=(n_tok // seq,),
        in_specs=[
            pl.BlockSpec((seq, ATT_Q), lambda b: (b, OFF_QA // ATT_Q)),
            pl.BlockSpec((seq, ATT_KV), lambda b: (b, OFF_KA // ATT_KV)),
            pl.BlockSpec((seq, ATT_KV), lambda b: (b, OFF_VA // ATT_KV)),
            pl.BlockSpec((HEAD_DIM, 1), lambda b: (0, 0)),
            pl.BlockSpec((HEAD_DIM, 1), lambda b: (0, 0)),
        ],
        out_specs=[
            pl.BlockSpec((seq, ATT_Q), lambda b: (b, 0)),
            pl.BlockSpec((seq, ATT_KV), lambda b: (b, 0)),
        ],
        out_shape=[
            jax.ShapeDtypeStruct((n_tok, ATT_Q), BF16),
            jax.ShapeDtypeStruct((n_tok, ATT_KV), F32),
        ],
        compiler_params=_cparams(("parallel",)),
        name="att_ctx",
    )(p_act, p_act, p_act, q_norm, k_norm)


def _att_lat_kernel(q_ref, k_ref, v_ref, ck_ref, cv_ref, qn_ref, kn_ref,
                    cq_ref, sq_ref, ck_t_ref, sk_t_ref, o_ref, kf_scr, vt_scr, *, seq):
    @pl.when(pl.program_id(1) == 0)
    def _():
        k_heads = _norm_rope_heads_t(k_ref[...].T, N_KV_HEADS, kn_ref[...], ck_t_ref[...], sk_t_ref[...])
        kf_scr[0:seq, :] = jnp.concatenate(k_heads, axis=0).T.astype(BF16)
        kf_scr[seq:, :] = ck_ref[0, 0].astype(BF16)
        vt_scr[:, 0:seq] = v_ref[...].T.astype(BF16)
        vt_scr[:, seq:] = cv_ref[0, 0].T.astype(BF16)

    q_heads = _norm_rope_heads_t(q_ref[...].T, N_HEADS, qn_ref[...], cq_ref[...], sq_ref[...])
    o_ref[...] = _attend_t(q_heads, kf_scr[...], vt_scr[...]).T.astype(o_ref.dtype)


def _att_lat(p_act, seq, cache_k, cache_v, layer, q_norm, k_norm, cos_t, sin_t):
    n_tok = p_act.shape[0]
    nb = seq // Q_BLOCK
    past = cache_k.shape[2]
    return pl.pallas_call(
        functools.partial(_att_lat_kernel, seq=seq),
        grid=(n_tok // seq, nb),
        in_specs=[
            pl.BlockSpec((Q_BLOCK, ATT_Q), lambda b, i: (b * nb + i, OFF_QA // ATT_Q)),
            pl.BlockSpec((seq, ATT_KV), lambda b, i: (b, OFF_KA // ATT_KV)),
            pl.BlockSpec((seq, ATT_KV), lambda b, i: (b, OFF_VA // ATT_KV)),
            pl.BlockSpec((1, 1, past, ATT_KV), lambda b, i: (b, layer, 0, 0)),
            pl.BlockSpec((1, 1, past, ATT_KV), lambda b, i: (b, layer, 0, 0)),
            pl.BlockSpec((HEAD_DIM, 1), lambda b, i: (0, 0)),
            pl.BlockSpec((HEAD_DIM, 1), lambda b, i: (0, 0)),
            pl.BlockSpec((HEAD_DIM // 2, Q_BLOCK), lambda b, i: (0, i)),
            pl.BlockSpec((HEAD_DIM // 2, Q_BLOCK), lambda b, i: (0, i)),
            pl.BlockSpec((HEAD_DIM // 2, seq), lambda b, i: (0, 0)),
            pl.BlockSpec((HEAD_DIM // 2, seq), lambda b, i: (0, 0)),
        ],
        out_specs=pl.BlockSpec((Q_BLOCK, ATT_Q), lambda b, i: (b * nb + i, 0)),
        out_shape=jax.ShapeDtypeStruct((n_tok, ATT_Q), BF16),
        scratch_shapes=[pltpu.VMEM((seq + past, ATT_KV), BF16),
                        pltpu.VMEM((ATT_KV, seq + past), BF16)],
        compiler_params=_cparams(("parallel", "arbitrary")),
        name="att_lat",
    )(p_act, p_act, p_act, cache_k, cache_v, q_norm, k_norm, cos_t, sin_t, cos_t, sin_t)


def _log_sigmoid(d):
    return jnp.minimum(d, 0.0) - jnp.log1p(jnp.exp(-jnp.abs(d)))


def _ret_kernel(q_ref, k_ref, v_ref, g_ref, dec_ref, rn_ref, *rest, seq, has_state):
    if has_state:
        s0f_ref, s0b_ref, o_ref = rest
    else:
        o_ref, sf_ref, sb_ref = rest
    hp = pl.program_id(1)
    n_chunks = seq // CHUNK
    ii = lax.broadcasted_iota(jnp.int32, (CHUNK, CHUNK), 0)
    jj = lax.broadcasted_iota(jnp.int32, (CHUNK, CHUNK), 1)
    rel = (ii - jj).astype(F32)
    row = ii.astype(F32)
    lane = jj.astype(F32)
    lgf = [_log_sigmoid(dec_ref[pl.ds(2 * hp + t, 1), :]) for t in range(2)]
    lgb = [_log_sigmoid(dec_ref[pl.ds(RET_HEADS + 2 * hp + t, 1), :]) for t in range(2)]

    mask2 = jnp.concatenate(
        [jnp.where(rel > 0, jnp.exp(jnp.maximum(rel, 0.0) * lgf[t]),
                   jnp.where(rel < 0, jnp.exp(jnp.maximum(-rel, 0.0) * lgb[t]), 2.0))
         for t in range(2)], axis=0)
    qdec_f = [jnp.exp((row + 1.0) * lgf[t]) for t in range(2)]
    qdec_b = [jnp.exp((CHUNK - row) * lgb[t]) for t in range(2)]
    cdec_f = [jnp.exp(CHUNK * lgf[t]) for t in range(2)]
    cdec_b = [jnp.exp(CHUNK * lgb[t]) for t in range(2)]
    kdec_f = jnp.exp((CHUNK - 1.0 - lane) * jnp.where(ii < RET_DK, lgf[0], lgf[1]))
    kdec_b = jnp.exp(lane * jnp.where(ii < RET_DK, lgb[0], lgb[1]))

    k_t = (k_ref[...] * (RET_DK ** -0.5)).T
    first_head = jj < RET_DK

    def chunk(c):
        return slice(c * CHUNK, (c + 1) * CHUNK)

    kv_f = [[None] * n_chunks for _ in range(2)]
    kv_b = [[None] * n_chunks for _ in range(2)]
    for c in range(n_chunks):
        k_c = k_t[:, chunk(c)]
        kd = jnp.concatenate([k_c * kdec_f, k_c * kdec_b], axis=0).astype(BF16)
        kv = _dot(kd, v_ref[chunk(c), :].astype(BF16))
        for t in range(2):
            kv_f[t][c] = kv[t * RET_DK:(t + 1) * RET_DK, t * RET_DV:(t + 1) * RET_DV]
            kv_b[t][c] = kv[CHUNK + t * RET_DK:CHUNK + (t + 1) * RET_DK, t * RET_DV:(t + 1) * RET_DV]

    st_f = [[None] * n_chunks for _ in range(2)]
    st_b = [[None] * n_chunks for _ in range(2)]
    for t in range(2):
        if has_state:
            s_f = s0f_ref[0, 0, t]
            s_b = s0b_ref[0, 0, t]
        else:
            s_f = jnp.zeros((RET_DK, RET_DV), F32)
            s_b = s_f
        for c in range(n_chunks):
            st_f[t][c] = s_f
            s_f = s_f * cdec_f[t] + kv_f[t][c]
        for c in reversed(range(n_chunks)):
            st_b[t][c] = s_b
            s_b = s_b * cdec_b[t] + kv_b[t][c]
        if not has_state:
            sf_ref[0, t] = s_f
            sb_ref[0, t] = s_b

    for c in range(n_chunks):
        q_c = q_ref[chunk(c), :]
        qm = jnp.concatenate([jnp.where(first_head, q_c, 0.0), jnp.where(first_head, 0.0, q_c)],
                             axis=0).astype(BF16)
        att = (_dot(qm, k_t[:, chunk(c)].astype(BF16)) * mask2).astype(BF16)
        states = jnp.concatenate(
            [jnp.concatenate([st_f[t][c], st_b[t][c]], axis=1) for t in range(2)], axis=0)
        qs = _dot(qm, states.astype(BF16))
        v_c = v_ref[chunk(c), :].astype(BF16)
        for t in range(2):
            rows = slice(t * CHUNK, (t + 1) * CHUNK)
            vsl = slice(t * RET_DV, (t + 1) * RET_DV)
            o = (_dot(att[rows], v_c[:, vsl]) + qs[rows, :RET_DV] * qdec_f[t]
                 + qs[rows, RET_DV:] * qdec_b[t])
            o = (o * lax.rsqrt(jnp.mean(o * o, axis=-1, keepdims=True) + EPS)) * rn_ref[...]
            g = g_ref[chunk(c), vsl]
            o_ref[chunk(c), vsl] = (o * (g * jax.nn.sigmoid(g))).astype(o_ref.dtype)


def _retention(p_act, seq, dec, ret_norm, states=None, layer=0):
    n_tok = p_act.shape[0]
    n_seq = n_tok // seq
    has_state = states is not None
    pair_qk, pair_v = 2 * RET_DK, 2 * RET_DV
    in_specs = [
        pl.BlockSpec((seq, pair_qk), lambda s, p: (s, OFF_QR // pair_qk + p)),
        pl.BlockSpec((seq, pair_qk), lambda s, p: (s, OFF_KR // pair_qk + p)),
        pl.BlockSpec((seq, pair_v), lambda s, p: (s, OFF_VR // pair_v + p)),
        pl.BlockSpec((seq, pair_v), lambda s, p: (s, OFF_GR // pair_v + p)),
        pl.BlockSpec((8, 128), lambda s, p: (0, 0)),
        pl.BlockSpec((1, RET_DV), lambda s, p: (0, 0)),
    ]
    args = [p_act, p_act, p_act, p_act, dec, ret_norm]
    o_spec = pl.BlockSpec((seq, pair_v), lambda s, p: (s, p))
    o_shape = jax.ShapeDtypeStruct((n_tok, RET_V), BF16)
    if has_state:
        st_spec = pl.BlockSpec((1, 1, 2, RET_DK, RET_DV), lambda s, p: (s, layer, p, 0, 0))
        in_specs += [st_spec, st_spec]
        args += list(states)
        out_specs, out_shape = o_spec, o_shape
    else:
        st_spec = pl.BlockSpec((1, 2, RET_DK, RET_DV), lambda s, p: (s, p, 0, 0))
        st_shape = jax.ShapeDtypeStruct((n_seq, RET_HEADS, RET_DK, RET_DV), F32)
        out_specs, out_shape = [o_spec, st_spec, st_spec], [o_shape, st_shape, st_shape]
    return pl.pallas_call(
        functools.partial(_ret_kernel, seq=seq, has_state=has_state),
        grid=(n_seq, RET_HEADS // 2),
        in_specs=in_specs,
        out_specs=out_specs,
        out_shape=out_shape,
        compiler_params=_cparams(("parallel", "parallel")),
        name="retention_lat" if has_state else "retention_ctx",
    )(*args)


def _four_kernel(u0_ref, u1_ref, wc_ref, csn_ref, o_ref):
    gd = FOURIER_GROUP_DIM
    tc, ts = [], []
    for g in range(FOURIER_GROUPS):
        u_ref = (u0_ref, u1_ref)[g // 2]
        u_g = u_ref[:, (g % 2) * gd:(g % 2 + 1) * gd].astype(BF16)
        t = _dot(u_g, wc_ref[...])
        tc.append(t[:, :gd])
        ts.append(t[:, gd:])
    t_all = jnp.concatenate([jnp.concatenate(tc, axis=1), jnp.concatenate(ts, axis=1)], axis=0)
    o_ref[...] = _dot(csn_ref[...], t_all.astype(BF16)).astype(o_ref.dtype)


def _dft_tables(n):
    k = np.arange(n, dtype=np.int64)
    ang = 2.0 * np.pi * ((k[:, None] * k[None, :]) % n).astype(np.float64) / n
    scale = 1.0 / np.sqrt(n)
    return np.cos(ang) * scale, np.sin(ang) * scale


def _fourier(p_act, seq):
    n_tok = p_act.shape[0]
    cc, sc = _dft_tables(FOURIER_GROUP_DIM)
    wc = jnp.asarray(np.concatenate([cc, -sc], axis=1), F32).astype(BF16)
    cn, sn = _dft_tables(seq)
    csn = jnp.asarray(np.concatenate([cn, sn], axis=1), F32).astype(BF16)
    const = lambda shape: pl.BlockSpec(shape, lambda s: (0, 0))
    half = FOURIER_W // 2
    return pl.pallas_call(
        _four_kernel,
        grid=(n_tok // seq,),
        in_specs=[
            pl.BlockSpec((seq, half), lambda s: (s, OFF_UF // half)),
            pl.BlockSpec((seq, half), lambda s: (s, OFF_UF // half + 1)),
            const((FOURIER_GROUP_DIM, 2 * FOURIER_GROUP_DIM)),
            const((seq, 2 * seq)),
        ],
        out_specs=pl.BlockSpec((seq, FOURIER_W), lambda s: (s, 0)),
        out_shape=jax.ShapeDtypeStruct((n_tok, FOURIER_W), BF16),
        compiler_params=_cparams(("parallel",)),
        name="fourier",
    )(p_act, p_act, wc, csn)


MIX_TM = 1024
MERGE_STEPS = 4
MERGE_TM = MIX_TM // MERGE_STEPS
FF_TILE = 256
FF_STEPS = D_FF // FF_TILE
FF_ROWS = 256


def _mix_kernel(oa_ref, or_ref, of_ref, ga_ref, gr_ref, gf_ref, x_ref, g1_ref, sh_ref, sc_ref,
                g2_ref, n2_ref, wba_ref, wbr_ref, wbf_ref, wo_ref, wa_ref, wv_ref, cw_ref, cb_ref,
                wd_ref, o_ref, x1_scr, h_scr, acc_scr, *, seq):
    j = pl.program_id(1)

    @pl.when(j < MERGE_STEPS)
    def _():
        rows = pl.ds(pl.multiple_of(j * MERGE_TM, MERGE_TM), MERGE_TM)
        gate = lambda ref: 0.5 * (jnp.tanh(0.5 * ref[...].astype(F32)) + 1.0)
        merged = (gate(ga_ref) * _dot(oa_ref[...], wba_ref[...])
                  + gate(gr_ref) * _dot(or_ref[...], wbr_ref[...])
                  + gate(gf_ref) * _dot(of_ref[...], wbf_ref[...]))
        x1 = x_ref[...] + g1_ref[0] * _dot(merged.astype(BF16), wo_ref[...])
        x1_scr[rows, :] = x1
        r = lax.rsqrt(jnp.mean(x1 * x1, axis=-1, keepdims=True) + EPS)
        h = (x1 * r) * n2_ref[...] * (1.0 + sc_ref[0]) + sh_ref[0]
        h_scr[rows, :] = h.astype(BF16)
        acc_scr[rows, :] = jnp.zeros((MERGE_TM, D_MODEL), F32)

    @pl.when(j >= MERGE_STEPS)
    def _():
        blocks = [slice(b * FF_ROWS, (b + 1) * FF_ROWS) for b in range(MIX_TM // FF_ROWS)]
        a = jnp.concatenate([_dot(h_scr[rows, :], wa_ref[...]) for rows in blocks], axis=0)
        val = [_dot(h_scr[rows, :], wv_ref[...]) for rows in blocks]
        pos = lax.broadcasted_iota(jnp.int32, a.shape, 0) % seq
        prev = jnp.where(pos == 0, 0.0, pltpu.roll(a, 1, 0))
        nxt = jnp.where(pos == seq - 1, 0.0, pltpu.roll(a, MIX_TM - 1, 0))
        cw = cw_ref[...]
        ac = prev * cw[0:1] + a * cw[1:2] + nxt * cw[2:3] + cb_ref[...]
        for b, rows in enumerate(blocks):
            act = jax.nn.gelu(ac[rows]) * val[b]
            acc_scr[rows, :] += _dot(act.astype(BF16), wd_ref[...])

    @pl.when(j == pl.num_programs(1) - 1)
    def _():
        o_ref[...] = x1_scr[...] + g2_ref[0] * acc_scr[...]


def _mix(o_att, o_ret, o_four, p_gate, x2d, seq, mod, rows_per_mod, w_att, w_ret, w_four, w_out,
         norm2, w_up, conv_w, conv_b, w_down):
    n_tok = x2d.shape[0]
    mod_idx = lambda i: (i * MIX_TM) // rows_per_mod
    sub = lambda i, j: i * MERGE_STEPS + jnp.minimum(j, MERGE_STEPS - 1)
    ff = lambda j: jnp.maximum(j - MERGE_STEPS, 0)
    br = lambda: pl.BlockSpec((MERGE_TM, 512), lambda i, j: (sub(i, j), 0))
    gate = lambda k: pl.BlockSpec((MERGE_TM, D_MODEL), lambda i, j: (sub(i, j), k))
    modv = lambda k: pl.BlockSpec((1, 1, D_MODEL), lambda i, j: (mod_idx(i), 0, k))
    const = lambda shape: pl.BlockSpec(shape, lambda i, j: (0, 0))
    return pl.pallas_call(
        functools.partial(_mix_kernel, seq=seq),
        grid=(n_tok // MIX_TM, MERGE_STEPS + FF_STEPS),
        in_specs=[
            br(), br(), br(), gate(0), gate(1), gate(2),
            pl.BlockSpec((MERGE_TM, D_MODEL), lambda i, j: (sub(i, j), 0)),
            modv(2), modv(3), modv(4), modv(5),
            const((1, D_MODEL)),
            const((512, D_MODEL)), const((512, D_MODEL)), const((512, D_MODEL)),
            const((D_MODEL, D_MODEL)),
            pl.BlockSpec((D_MODEL, FF_TILE), lambda i, j: (0, ff(j))),
            pl.BlockSpec((D_MODEL, FF_TILE), lambda i, j: (0, FF_STEPS + ff(j))),
            pl.BlockSpec((3, FF_TILE), lambda i, j: (0, ff(j))),
            pl.BlockSpec((1, FF_TILE), lambda i, j: (0, ff(j))),
            pl.BlockSpec((FF_TILE, D_MODEL), lambda i, j: (ff(j), 0)),
        ],
        out_specs=pl.BlockSpec((MIX_TM, D_MODEL), lambda i, j: (i, 0)),
        out_shape=jax.ShapeDtypeStruct((n_tok, D_MODEL), F32),
        scratch_shapes=[pltpu.VMEM((MIX_TM, D_MODEL), F32), pltpu.VMEM((MIX_TM, D_MODEL), BF16),
                        pltpu.VMEM((MIX_TM, D_MODEL), F32)],
        compiler_params=_cparams(("parallel", "arbitrary")),
        name="mix",
    )(o_att, o_ret, o_four, p_gate, p_gate, p_gate, x2d, mod, mod, mod, mod, norm2,
      w_att, w_ret, w_four, w_out, w_up, w_up, conv_w, conv_b, w_down)


def _pad_w_in(w):
    pad = jnp.zeros((w.shape[0], OFF_GATE - W_IN_SPLIT), BF16)
    return jnp.concatenate([w[:, :W_IN_SPLIT].astype(BF16), pad, w[:, W_IN_SPLIT:].astype(BF16)], axis=1)


def _rope_tables(n_tok):
    rows = n_tok // GRID_W
    row_id = jnp.repeat(jnp.arange(rows), GRID_W).astype(F32)
    col_id = jnp.tile(jnp.arange(GRID_W), rows).astype(F32)
    n_freq = HEAD_DIM // 4
    inv = ROPE_THETA ** (-jnp.arange(n_freq, dtype=F32) / n_freq)
    ang = jnp.concatenate([row_id[None, :] * inv[:, None], col_id[None, :] * inv[:, None]], axis=0)
    return jnp.cos(ang), jnp.sin(ang)


def kernel(x_prompt, x_sample, cache_k, cache_v, state_ret_fwd, state_ret_bwd, c, c_ctx, w_ada, b_ada, norm1, w_in, q_norm, k_norm, ret_decay_f, ret_decay_b, ret_norm, w_br_att, w_br_ret, w_br_four, w_out, norm2, w_up, conv_w, conv_b, w_down):
    batch, seq, _ = x_prompt.shape
    dec_batch, dec_seq, _ = x_sample.shape
    past = cache_k.shape[2]

    cond_all = jnp.concatenate([c_ctx[None, :], c], axis=0)
    mod_all = _ada(cond_all, w_ada, b_ada)
    cos_t, sin_t = _rope_tables(dec_seq)
    ck = cache_k.reshape(dec_batch, DEPTH, past, ATT_KV)
    cv = cache_v.reshape(dec_batch, DEPTH, past, ATT_KV)

    xp = x_prompt.reshape(batch * seq, D_MODEL)
    xs = x_sample.reshape(dec_batch * dec_seq, D_MODEL)
    new_k, new_v, new_sf, new_sb = [], [], [], []
    for l in range(DEPTH):
        w_in_p = _pad_w_in(w_in[l])
        w_att, w_ret, w_four = (w_br_att[l].astype(BF16), w_br_ret[l].astype(BF16),
                                w_br_four[l].astype(BF16))
        w_o, w_u, w_d = w_out[l].astype(BF16), w_up[l].astype(BF16), w_down[l].astype(BF16)
        n1, n2 = norm1[l][None, :], norm2[l][None, :]
        qn, kn = q_norm[l][:, None], k_norm[l][:, None]
        rn = ret_norm[l][None, :]
        dec = jnp.broadcast_to(
            jnp.concatenate([ret_decay_f[l], ret_decay_b[l]])[:, None].astype(F32), (2 * RET_HEADS, 128))
        cw, cb = conv_w[l], conv_b[l][None, :]
        mod_ctx = mod_all[l, 0:1].reshape(1, 1, 6 * D_MODEL)
        mod_lat = mod_all[l, 1:].reshape(dec_batch, 1, 6 * D_MODEL)

        p_act, p_gate = _in_proj(xp, mod_ctx, batch * seq, n1, w_in_p)
        o_att, k_l = _att_ctx(p_act, seq, qn, kn)
        o_ret, sf_l, sb_l = _retention(p_act, seq, dec, rn)
        o_four = _fourier(p_act, seq)
        xp = _mix(o_att, o_ret, o_four, p_gate, xp, seq, mod_ctx, batch * seq,
                  w_att, w_ret, w_four, w_o, n2, w_u, cw, cb, w_d)
        new_k.append(k_l.reshape(batch, seq, N_KV_HEADS, HEAD_DIM))
        new_v.append(p_act[:, OFF_VA:OFF_VA + ATT_KV].reshape(batch, seq, N_KV_HEADS, HEAD_DIM))
        new_sf.append(sf_l)
        new_sb.append(sb_l)

        p_act, p_gate = _in_proj(xs, mod_lat, dec_seq, n1, w_in_p)
        o_att = _att_lat(p_act, dec_seq, ck, cv, l, qn, kn, cos_t, sin_t)
        o_ret = _retention(p_act, dec_seq, dec, rn, states=(state_ret_fwd, state_ret_bwd), layer=l)
        o_four = _fourier(p_act, dec_seq)
        xs = _mix(o_att, o_ret, o_four, p_gate, xs, dec_seq, mod_lat, dec_seq,
                  w_att, w_ret, w_four, w_o, n2, w_u, cw, cb, w_d)

    return (xp.reshape(batch, seq, D_MODEL), xs.reshape(dec_batch, dec_seq, D_MODEL),
            jnp.stack(new_k, axis=1), jnp.stack(new_v, axis=1),
            jnp.stack(new_sf, axis=1), jnp.stack(new_sb, axis=1))
```

```python
import functools

import numpy as np
import jax
import jax.numpy as jnp
from jax import lax
from jax.experimental import pallas as pl
from jax.experimental.pallas import tpu as pltpu

D_MODEL = 1024
DEPTH = 2
GRID_W = 64
HEAD_DIM = 64
N_HEADS = 8
N_KV_HEADS = 2
Q_PER_KV = N_HEADS // N_KV_HEADS
ATT_Q = N_HEADS * HEAD_DIM
ATT_KV = N_KV_HEADS * HEAD_DIM
RET_HEADS = 4
RET_DK = 64
RET_DV = 128
RET_QK = RET_HEADS * RET_DK
RET_V = RET_HEADS * RET_DV
FOURIER_GROUPS = 4
FOURIER_GROUP_DIM = 128
FOURIER_W = FOURIER_GROUPS * FOURIER_GROUP_DIM
D_FF = 2816
CHUNK = 128
Q_BLOCK = 128
ROPE_THETA = 10000.0
EPS = 1e-6
LOG2_E = 1.4426950408889634

F32 = jnp.float32
BF16 = jnp.bfloat16

OFF_QA, OFF_KA, OFF_VA = 0, 512, 640
OFF_QR, OFF_KR, OFF_VR, OFF_GR, OFF_UF = 768, 1024, 1280, 1792, 2304
W_IN_SPLIT = 2816
OFF_GATE = 3072
P_W = OFF_GATE + 3 * D_MODEL

VMEM_LIMIT = 56 * 1024 * 1024


def _cparams(sem, flags=None):
    return pltpu.CompilerParams(dimension_semantics=sem, vmem_limit_bytes=VMEM_LIMIT, flags=flags)


def _dot(a, b):
    return jnp.dot(a, b, preferred_element_type=F32)


def _ada_kernel(cond_ref, w_ref, b_ref, o_ref):
    cnd = cond_ref[...]
    s = cnd * jax.nn.sigmoid(cnd)
    o_ref[0] = _dot(s.astype(BF16), w_ref[0].astype(BF16)) + b_ref[0]


def _ada(cond_all, w_ada, b_ada):
    n = cond_all.shape[0]
    tn = 1024
    return pl.pallas_call(
        _ada_kernel,
        grid=(DEPTH, 6 * D_MODEL // tn),
        in_specs=[
            pl.BlockSpec((n, D_MODEL), lambda l, j: (0, 0)),
            pl.BlockSpec((1, D_MODEL, tn), lambda l, j: (l, 0, j)),
            pl.BlockSpec((1, 1, tn), lambda l, j: (l, 0, j)),
        ],
        out_specs=pl.BlockSpec((1, n, tn), lambda l, j: (l, 0, j)),
        out_shape=jax.ShapeDtypeStruct((DEPTH, n, 6 * D_MODEL), F32),
        compiler_params=_cparams(("parallel", "parallel")),
        name="ada",
    )(cond_all, w_ada, b_ada.reshape(DEPTH, 1, 6 * D_MODEL))


IN_TN = 1024
MAIN_STEPS = OFF_GATE // IN_TN


def _in_kernel(x_ref, sh_ref, sc_ref, g_ref, w_ref, om_ref, og_ref, h_scr):
    j = pl.program_id(1)

    @pl.when(j == 0)
    def _():
        x = x_ref[...]
        r = lax.rsqrt(jnp.mean(x * x, axis=-1, keepdims=True) + EPS)
        h = (x * r) * g_ref[...] * (1.0 + sc_ref[0]) + sh_ref[0]
        h_scr[...] = h.astype(BF16)

    y = _dot(h_scr[...], w_ref[...])

    @pl.when(j < MAIN_STEPS)
    def _():
        om_ref[...] = y

    @pl.when(j >= MAIN_STEPS)
    def _():
        og_ref[...] = y.astype(og_ref.dtype)


def _in_proj(x2d, mod, rows_per_mod, norm1, w_in_p):
    n_tok = x2d.shape[0]
    tm, tn = 1024, IN_TN
    mod_idx = lambda i: (i * tm) // rows_per_mod
    return pl.pallas_call(
        _in_kernel,
        grid=(n_tok // tm, P_W // tn),
        in_specs=[
            pl.BlockSpec((tm, D_MODEL), lambda i, j: (i, 0)),
            pl.BlockSpec((1, 1, D_MODEL), lambda i, j: (mod_idx(i), 0, 0)),
            pl.BlockSpec((1, 1, D_MODEL), lambda i, j: (mod_idx(i), 0, 1)),
            pl.BlockSpec((1, D_MODEL), lambda i, j: (0, 0)),
            pl.BlockSpec((D_MODEL, tn), lambda i, j: (0, j)),
        ],
        out_specs=[
            pl.BlockSpec((tm, tn), lambda i, j: (i, jnp.minimum(j, MAIN_STEPS - 1))),
            pl.BlockSpec((tm, tn), lambda i, j: (i, jnp.maximum(j - MAIN_STEPS, 0))),
        ],
        out_shape=[jax.ShapeDtypeStruct((n_tok, OFF_GATE), F32),
                   jax.ShapeDtypeStruct((n_tok, P_W - OFF_GATE), BF16)],
        scratch_shapes=[pltpu.VMEM((tm, D_MODEL), BF16)],
        compiler_params=_cparams(("parallel", "arbitrary")),
        name="in_proj",
    )(x2d, mod, mod, norm1, w_in_p)


def _norm_rope_heads_t(x_t, n_heads, g_col, cos_t=None, sin_t=None):
    quarter = HEAD_DIM // 4
    outs = []
    for h in range(n_heads):
        x = x_t[h * HEAD_DIM:(h + 1) * HEAD_DIM, :]
        r = lax.rsqrt(jnp.mean(x * x, axis=0, keepdims=True) + EPS)
        y = (x * r) * g_col
        if cos_t is not None:
            pieces = []
            for a in range(2):
                c = cos_t[a * quarter:(a + 1) * quarter]
                s = sin_t[a * quarter:(a + 1) * quarter]
                x1 = y[2 * a * quarter:(2 * a + 1) * quarter]
                x2 = y[(2 * a + 1) * quarter:(2 * a + 2) * quarter]
                pieces += [x1 * c - x2 * s, x2 * c + x1 * s]
            y = jnp.concatenate(pieces, axis=0)
        outs.append(y)
    return outs


def _attend_t(q_heads_t, k_bf, v_t_bf):
    tq = q_heads_t[0].shape[1]
    cols = Q_PER_KV * tq
    zeros = jnp.zeros((HEAD_DIM, cols), F32)
    q_kv = [jnp.concatenate(q_heads_t[kv * Q_PER_KV:(kv + 1) * Q_PER_KV], axis=1)
            * (HEAD_DIM ** -0.5 * LOG2_E) for kv in range(N_KV_HEADS)]
    rhs = jnp.concatenate([jnp.concatenate([q_kv[0], zeros], axis=1),
                           jnp.concatenate([zeros, q_kv[1]], axis=1)], axis=0).astype(BF16)
    s_all = _dot(k_bf, rhs)
    out_rows = []
    for kv in range(N_KV_HEADS):
        s_t = s_all[:, kv * cols:(kv + 1) * cols]
        e = jnp.exp2(s_t - jnp.max(s_t, axis=0, keepdims=True))
        inv = 1.0 / jnp.sum(e, axis=0, keepdims=True)
        o_t = _dot(v_t_bf[kv * HEAD_DIM:(kv + 1) * HEAD_DIM, :], e.astype(BF16)) * inv
        out_rows += [o_t[:, g * tq:(g + 1) * tq] for g in range(Q_PER_KV)]
    return jnp.concatenate(out_rows, axis=0)


def _att_ctx_kernel(q_ref, k_ref, v_ref, qn_ref, kn_ref, o_ref, ko_ref):
    k_heads = _norm_rope_heads_t(k_ref[...].T, N_KV_HEADS, kn_ref[...])
    k_n = jnp.concatenate(k_heads, axis=0).T
    ko_ref[...] = k_n
    q_heads = _norm_rope_heads_t(q_ref[...].T, N_HEADS, qn_ref[...])
    o_t = _attend_t(q_heads, k_n.astype(BF16), v_ref[...].T.astype(BF16))
    o_ref[...] = o_t.T.astype(o_ref.dtype)


def _att_ctx(p_act, seq, q_norm, k_norm):
    n_tok = p_act.shape[0]
    return pl.pallas_call(
        _att_ctx_kernel,
        grid=(n_tok // seq,),
        in_specs=[
            pl.BlockSpec((seq, ATT_Q), lambda b: (b, OFF_QA // ATT_Q)),
            pl.BlockSpec((seq, ATT_KV), lambda b: (b, OFF_KA // ATT_KV)),
            pl.BlockSpec((seq, ATT_KV), lambda b: (b, OFF_VA // ATT_KV)),
            pl.BlockSpec((HEAD_DIM, 1), lambda b: (0, 0)),
            pl.BlockSpec((HEAD_DIM, 1), lambda b: (0, 0)),
        ],
        out_specs=[
            pl.BlockSpec((seq, ATT_Q), lambda b: (b, 0)),
            pl.BlockSpec((seq, ATT_KV), lambda b: (b, 0)),
        ],
        out_shape=[
            jax.ShapeDtypeStruct((n_tok, ATT_Q), BF16),
            jax.ShapeDtypeStruct((n_tok, ATT_KV), F32),
        ],
        compiler_params=_cparams(("parallel",)),
        name="att_ctx",
    )(p_act, p_act, p_act, q_norm, k_norm)


def _att_lat_kernel(q_ref, k_ref, v_ref, ck_ref, cv_ref, qn_ref, kn_ref,
                    cq_ref, sq_ref, ck_t_ref, sk_t_ref, o_ref, kf_scr, vt_scr, *, seq):
    @pl.when(pl.program_id(1) == 0)
    def _():
        k_heads = _norm_rope_heads_t(k_ref[...].T, N_KV_HEADS, kn_ref[...], ck_t_ref[...], sk_t_ref[...])
        kf_scr[0:seq, :] = jnp.concatenate(k_heads, axis=0).T.astype(BF16)
        kf_scr[seq:, :] = ck_ref[0, 0].astype(BF16)
        vt_scr[:, 0:seq] = v_ref[...].T.astype(BF16)
        vt_scr[:, seq:] = cv_ref[0, 0].T.astype(BF16)

    q_heads = _norm_rope_heads_t(q_ref[...].T, N_HEADS, qn_ref[...], cq_ref[...], sq_ref[...])
    o_ref[...] = _attend_t(q_heads, kf_scr[...], vt_scr[...]).T.astype(o_ref.dtype)


def _att_lat(p_act, seq, cache_k, cache_v, layer, q_norm, k_norm, cos_t, sin_t):
    n_tok = p_act.shape[0]
    nb = seq // Q_BLOCK
    past = cache_k.shape[2]
    return pl.pallas_call(
        functools.partial(_att_lat_kernel, seq=seq),
        grid=(n_tok // seq, nb),
        in_specs=[
            pl.BlockSpec((Q_BLOCK, ATT_Q), lambda b, i: (b * nb + i, OFF_QA // ATT_Q)),
            pl.BlockSpec((seq, ATT_KV), lambda b, i: (b, OFF_KA // ATT_KV)),
            pl.BlockSpec((seq, ATT_KV), lambda b, i: (b, OFF_VA // ATT_KV)),
            pl.BlockSpec((1, 1, past, ATT_KV), lambda b, i: (b, layer, 0, 0)),
            pl.BlockSpec((1, 1, past, ATT_KV), lambda b, i: (b, layer, 0, 0)),
            pl.BlockSpec((HEAD_DIM, 1), lambda b, i: (0, 0)),
            pl.BlockSpec((HEAD_DIM, 1), lambda b, i: (0, 0)),
            pl.BlockSpec((HEAD_DIM // 2, Q_BLOCK), lambda b, i: (0, i)),
            pl.BlockSpec((HEAD_DIM // 2, Q_BLOCK), lambda b, i: (0, i)),
            pl.BlockSpec((HEAD_DIM // 2, seq), lambda b, i: (0, 0)),
            pl.BlockSpec((HEAD_DIM // 2, seq), lambda b, i: (0, 0)),
        ],
        out_specs=pl.BlockSpec((Q_BLOCK, ATT_Q), lambda b, i: (b * nb + i, 0)),
        out_shape=jax.ShapeDtypeStruct((n_tok, ATT_Q), BF16),
        scratch_shapes=[pltpu.VMEM((seq + past, ATT_KV), BF16),
                        pltpu.VMEM((ATT_KV, seq + past), BF16)],
        compiler_params=_cparams(("parallel", "arbitrary")),
        name="att_lat",
    )(p_act, p_act, p_act, cache_k, cache_v, q_norm, k_norm, cos_t, sin_t, cos_t, sin_t)


def _log_sigmoid(d):
    return jnp.minimum(d, 0.0) - jnp.log1p(jnp.exp(-jnp.abs(d)))


def _ret_kernel(q_ref, k_ref, v_ref, g_ref, dec_ref, rn_ref, *rest, seq, has_state):
    if has_state:
        s0f_ref, s0b_ref, o_ref = rest
    else:
        o_ref, sf_ref, sb_ref = rest
    hp = pl.program_id(1)
    n_chunks = seq // CHUNK
    ii = lax.broadcasted_iota(jnp.int32, (CHUNK, CHUNK), 0)
    jj = lax.broadcasted_iota(jnp.int32, (CHUNK, CHUNK), 1)
    rel = (ii - jj).astype(F32)
    row = ii.astype(F32)
    lane = jj.astype(F32)
    lgf = [_log_sigmoid(dec_ref[pl.ds(2 * hp + t, 1), :]) for t in range(2)]
    lgb = [_log_sigmoid(dec_ref[pl.ds(RET_HEADS + 2 * hp + t, 1), :]) for t in range(2)]

    mask2 = jnp.concatenate(
        [jnp.where(rel > 0, jnp.exp(jnp.maximum(rel, 0.0) * lgf[t]),
                   jnp.where(rel < 0, jnp.exp(jnp.maximum(-rel, 0.0) * lgb[t]), 2.0))
         for t in range(2)], axis=0)
    qdec_f = [jnp.exp((row + 1.0) * lgf[t]) for t in range(2)]
    qdec_b = [jnp.exp((CHUNK - row) * lgb[t]) for t in range(2)]
    cdec_f = [jnp.exp(CHUNK * lgf[t]) for t in range(2)]
    cdec_b = [jnp.exp(CHUNK * lgb[t]) for t in range(2)]
    kdec_f = jnp.exp((CHUNK - 1.0 - lane) * jnp.where(ii < RET_DK, lgf[0], lgf[1]))
    kdec_b = jnp.exp(lane * jnp.where(ii < RET_DK, lgb[0], lgb[1]))

    k_t = (k_ref[...] * (RET_DK ** -0.5)).T
    first_head = jj < RET_DK

    def chunk(c):
        return slice(c * CHUNK, (c + 1) * CHUNK)

    kv_f = [[None] * n_chunks for _ in range(2)]
    kv_b = [[None] * n_chunks for _ in range(2)]
    for c in range(n_chunks):
        k_c = k_t[:, chunk(c)]
        kd = jnp.concatenate([k_c * kdec_f, k_c * kdec_b], axis=0).astype(BF16)
        kv = _dot(kd, v_ref[chunk(c), :].astype(BF16))
        for t in range(2):
            kv_f[t][c] = kv[t * RET_DK:(t + 1) * RET_DK, t * RET_DV:(t + 1) * RET_DV]
            kv_b[t][c] = kv[CHUNK + t * RET_DK:CHUNK + (t + 1) * RET_DK, t * RET_DV:(t + 1) * RET_DV]

    st_f = [[None] * n_chunks for _ in range(2)]
    st_b = [[None] * n_chunks for _ in range(2)]
    for t in range(2):
        if has_state:
            s_f = s0f_ref[0, 0, t]
            s_b = s0b_ref[0, 0, t]
        else:
            s_f = jnp.zeros((RET_DK, RET_DV), F32)
            s_b = s_f
        for c in range(n_chunks):
            st_f[t][c] = s_f
            s_f = s_f * cdec_f[t] + kv_f[t][c]
        for c in reversed(range(n_chunks)):
            st_b[t][c] = s_b
            s_b = s_b * cdec_b[t] + kv_b[t][c]
        if not has_state:
            sf_ref[0, t] = s_f
            sb_ref[0, t] = s_b

    for c in range(n_chunks):
        q_c = q_ref[chunk(c), :]
        qm = jnp.concatenate([jnp.where(first_head, q_c, 0.0), jnp.where(first_head, 0.0, q_c)],
                             axis=0).astype(BF16)
        att = (_dot(qm, k_t[:, chunk(c)].astype(BF16)) * mask2).astype(BF16)
        states = jnp.concatenate(
            [jnp.concatenate([st_f[t][c], st_b[t][c]], axis=1) for t in range(2)], axis=0)
        qs = _dot(qm, states.astype(BF16))
        v_c = v_ref[chunk(c), :].astype(BF16)
        for t in range(2):
            rows = slice(t * CHUNK, (t + 1) * CHUNK)
            vsl = slice(t * RET_DV, (t + 1) * RET_DV)
            o = (_dot(att[rows], v_c[:, vsl]) + qs[rows, :RET_DV] * qdec_f[t]
                 + qs[rows, RET_DV:] * qdec_b[t])
            o = (o * lax.rsqrt(jnp.mean(o * o, axis=-1, keepdims=True) + EPS)) * rn_ref[...]
            g = g_ref[chunk(c), vsl]
            o_ref[chunk(c), vsl] = (o * (g * jax.nn.sigmoid(g))).astype(o_ref.dtype)


def _retention(p_act, seq, dec, ret_norm, states=None, layer=0):
    n_tok = p_act.shape[0]
    n_seq = n_tok // seq
    has_state = states is not None
    pair_qk, pair_v = 2 * RET_DK, 2 * RET_DV
    in_specs = [
        pl.BlockSpec((seq, pair_qk), lambda s, p: (s, OFF_QR // pair_qk + p)),
        pl.BlockSpec((seq, pair_qk), lambda s, p: (s, OFF_KR // pair_qk + p)),
        pl.BlockSpec((seq, pair_v), lambda s, p: (s, OFF_VR // pair_v + p)),
        pl.BlockSpec((seq, pair_v), lambda s, p: (s, OFF_GR // pair_v + p)),
        pl.BlockSpec((8, 128), lambda s, p: (0, 0)),
        pl.BlockSpec((1, RET_DV), lambda s, p: (0, 0)),
    ]
    args = [p_act, p_act, p_act, p_act, dec, ret_norm]
    o_spec = pl.BlockSpec((seq, pair_v), lambda s, p: (s, p))
    o_shape = jax.ShapeDtypeStruct((n_tok, RET_V), BF16)
    if has_state:
        st_spec = pl.BlockSpec((1, 1, 2, RET_DK, RET_DV), lambda s, p: (s, layer, p, 0, 0))
        in_specs += [st_spec, st_spec]
        args += list(states)
        out_specs, out_shape = o_spec, o_shape
    else:
        st_spec = pl.BlockSpec((1, 2, RET_DK, RET_DV), lambda s, p: (s, p, 0, 0))
        st_shape = jax.ShapeDtypeStruct((n_seq, RET_HEADS, RET_DK, RET_DV), F32)
        out_specs, out_shape = [o_spec, st_spec, st_spec], [o_shape, st_shape, st_shape]
    return pl.pallas_call(
        functools.partial(_ret_kernel, seq=seq, has_state=has_state),
        grid=(n_seq, RET_HEADS // 2),
        in_specs=in_specs,
        out_specs=out_specs,
        out_shape=out_shape,
        compiler_params=_cparams(("parallel", "parallel")),
        name="retention_lat" if has_state else "retention_ctx",
    )(*args)


def _four_kernel(u0_ref, u1_ref, wc_ref, csn_ref, o_ref):
    gd = FOURIER_GROUP_DIM
    tc, ts = [], []
    for g in range(FOURIER_GROUPS):
        u_ref = (u0_ref, u1_ref)[g // 2]
        u_g = u_ref[:, (g % 2) * gd:(g % 2 + 1) * gd].astype(BF16)
        t = _dot(u_g, wc_ref[...])
        tc.append(t[:, :gd])
        ts.append(t[:, gd:])
    t_all = jnp.concatenate([jnp.concatenate(tc, axis=1), jnp.concatenate(ts, axis=1)], axis=0)
    o_ref[...] = _dot(csn_ref[...], t_all.astype(BF16)).astype(o_ref.dtype)


def _dft_tables(n):
    k = np.arange(n, dtype=np.int64)
    ang = 2.0 * np.pi * ((k[:, None] * k[None, :]) % n).astype(np.float64) / n
    scale = 1.0 / np.sqrt(n)
    return np.cos(ang) * scale, np.sin(ang) * scale


def _fourier(p_act, seq):
    n_tok = p_act.shape[0]
    cc, sc = _dft_tables(FOURIER_GROUP_DIM)
    wc = jnp.asarray(np.concatenate([cc, -sc], axis=1), F32).astype(BF16)
    cn, sn = _dft_tables(seq)
    csn = jnp.asarray(np.concatenate([cn, sn], axis=1), F32).astype(BF16)
    const = lambda shape: pl.BlockSpec(shape, lambda s: (0, 0))
    half = FOURIER_W // 2
    return pl.pallas_call(
        _four_kernel,
        grid=(n_tok // seq,),
        in_specs=[
            pl.BlockSpec((seq, half), lambda s: (s, OFF_UF // half)),
            pl.BlockSpec((seq, half), lambda s: (s, OFF_UF // half + 1)),
            const((FOURIER_GROUP_DIM, 2 * FOURIER_GROUP_DIM)),
            const((seq, 2 * seq)),
        ],
        out_specs=pl.BlockSpec((seq, FOURIER_W), lambda s: (s, 0)),
        out_shape=jax.ShapeDtypeStruct((n_tok, FOURIER_W), BF16),
        compiler_params=_cparams(("parallel",)),
        name="fourier",
    )(p_act, p_act, wc, csn)


MIX_TM = 1024
MERGE_STEPS = 4
MERGE_TM = MIX_TM // MERGE_STEPS
FF_TILE = 256
FF_STEPS = D_FF // FF_TILE
FF_ROWS = 256


def _mix_kernel(oa_ref, or_ref, of_ref, ga_ref, gr_ref, gf_ref, x_ref, g1_ref, sh_ref, sc_ref,
                g2_ref, n2_ref, wba_ref, wbr_ref, wbf_ref, wo_ref, wa_ref, wv_ref, cw_ref, cb_ref,
                wd_ref, o_ref, x1_scr, h_scr, acc_scr, *, seq):
    j = pl.program_id(1)

    @pl.when(j < MERGE_STEPS)
    def _():
        rows = pl.ds(pl.multiple_of(j * MERGE_TM, MERGE_TM), MERGE_TM)
        gate = lambda ref: 0.5 * (jnp.tanh(0.5 * ref[...].astype(F32)) + 1.0)
        merged = (gate(ga_ref) * _dot(oa_ref[...], wba_ref[...])
                  + gate(gr_ref) * _dot(or_ref[...], wbr_ref[...])
                  + gate(gf_ref) * _dot(of_ref[...], wbf_ref[...]))
        x1 = x_ref[...] + g1_ref[0] * _dot(merged.astype(BF16), wo_ref[...])
        x1_scr[rows, :] = x1
        r = lax.rsqrt(jnp.mean(x1 * x1, axis=-1, keepdims=True) + EPS)
        h = (x1 * r) * n2_ref[...] * (1.0 + sc_ref[0]) + sh_ref[0]
        h_scr[rows, :] = h.astype(BF16)
        acc_scr[rows, :] = jnp.zeros((MERGE_TM, D_MODEL), F32)

    @pl.when(j >= MERGE_STEPS)
    def _():
        blocks = [slice(b * FF_ROWS, (b + 1) * FF_ROWS) for b in range(MIX_TM // FF_ROWS)]
        a = jnp.concatenate([_dot(h_scr[rows, :], wa_ref[...]) for rows in blocks], axis=0)
        val = [_dot(h_scr[rows, :], wv_ref[...]) for rows in blocks]
        pos = lax.broadcasted_iota(jnp.int32, a.shape, 0) % seq
        prev = jnp.where(pos == 0, 0.0, pltpu.roll(a, 1, 0))
        nxt = jnp.where(pos == seq - 1, 0.0, pltpu.roll(a, MIX_TM - 1, 0))
        cw = cw_ref[...]
        ac = prev * cw[0:1] + a * cw[1:2] + nxt * cw[2:3] + cb_ref[...]
        for b, rows in enumerate(blocks):
            act = jax.nn.gelu(ac[rows]) * val[b]
            acc_scr[rows, :] += _dot(act.astype(BF16), wd_ref[...])

    @pl.when(j == pl.num_programs(1) - 1)
    def _():
        o_ref[...] = x1_scr[...] + g2_ref[0] * acc_scr[...]


def _mix(o_att, o_ret, o_four, p_gate, x2d, seq, mod, rows_per_mod, w_att, w_ret, w_four, w_out,
         norm2, w_up, conv_w, conv_b, w_down):
    n_tok = x2d.shape[0]
    mod_idx = lambda i: (i * MIX_TM) // rows_per_mod
    sub = lambda i, j: i * MERGE_STEPS + jnp.minimum(j, MERGE_STEPS - 1)
    ff = lambda j: jnp.maximum(j - MERGE_STEPS, 0)
    br = lambda: pl.BlockSpec((MERGE_TM, 512), lambda i, j: (sub(i, j), 0))
    gate = lambda k: pl.BlockSpec((MERGE_TM, D_MODEL), lambda i, j: (sub(i, j), k))
    modv = lambda k: pl.BlockSpec((1, 1, D_MODEL), lambda i, j: (mod_idx(i), 0, k))
    const = lambda shape: pl.BlockSpec(shape, lambda i, j: (0, 0))
    return pl.pallas_call(
        functools.partial(_mix_kernel, seq=seq),
        grid=(n_tok // MIX_TM, MERGE_STEPS + FF_STEPS),
        in_specs=[
            br(), br(), br(), gate(0), gate(1), gate(2),
            pl.BlockSpec((MERGE_TM, D_MODEL), lambda i, j: (sub(i, j), 0)),
            modv(2), modv(3), modv(4), modv(5),
            const((1, D_MODEL)),
            const((512, D_MODEL)), const((512, D_MODEL)), const((512, D_MODEL)),
            const((D_MODEL, D_MODEL)),
            pl.BlockSpec((D_MODEL, FF_TILE), lambda i, j: (0, ff(j))),
            pl.BlockSpec((D_MODEL, FF_TILE), lambda i, j: (0, FF_STEPS + ff(j))),
            pl.BlockSpec((3, FF_TILE), lambda i, j: (0, ff(j))),
            pl.BlockSpec((1, FF_TILE), lambda i, j: (0, ff(j))),
            pl.BlockSpec((FF_TILE, D_MODEL), lambda i, j: (ff(j), 0)),
        ],
        out_specs=pl.BlockSpec((MIX_TM, D_MODEL), lambda i, j: (i, 0)),
        out_shape=jax.ShapeDtypeStruct((n_tok, D_MODEL), F32),
        scratch_shapes=[pltpu.VMEM((MIX_TM, D_MODEL), F32), pltpu.VMEM((MIX_TM, D_MODEL), BF16),
                        pltpu.VMEM((MIX_TM, D_MODEL), F32)],
        compiler_params=_cparams(("parallel", "arbitrary")),
        name="mix",
    )(o_att, o_ret, o_four, p_gate, p_gate, p_gate, x2d, mod, mod, mod, mod, norm2,
      w_att, w_ret, w_four, w_out, w_up, w_up, conv_w, conv_b, w_down)


def _pad_w_in(w):
    pad = jnp.zeros((w.shape[0], OFF_GATE - W_IN_SPLIT), BF16)
    return jnp.concatenate([w[:, :W_IN_SPLIT].astype(BF16), pad, w[:, W_IN_SPLIT:].astype(BF16)], axis=1)


def _rope_tables(n_tok):
    rows = n_tok // GRID_W
    row_id = jnp.repeat(jnp.arange(rows), GRID_W).astype(F32)
    col_id = jnp.tile(jnp.arange(GRID_W), rows).astype(F32)
    n_freq = HEAD_DIM // 4
    inv = ROPE_THETA ** (-jnp.arange(n_freq, dtype=F32) / n_freq)
    ang = jnp.concatenate([row_id[None, :] * inv[:, None], col_id[None, :] * inv[:, None]], axis=0)
    return jnp.cos(ang), jnp.sin(ang)


def kernel(x_prompt, x_sample, cache_k, cache_v, state_ret_fwd, state_ret_bwd, c, c_ctx, w_ada, b_ada, norm1, w_in, q_norm, k_norm, ret_decay_f, ret_decay_b, ret_norm, w_br_att, w_br_ret, w_br_four, w_out, norm2, w_up, conv_w, conv_b, w_down):
    batch, seq, _ = x_prompt.shape
    dec_batch, dec_seq, _ = x_sample.shape
    past = cache_k.shape[2]

    cond_all = jnp.concatenate([c_ctx[None, :], c], axis=0)
    mod_all = _ada(cond_all, w_ada, b_ada)
    cos_t, sin_t = _rope_tables(dec_seq)
    ck = cache_k.reshape(dec_batch, DEPTH, past, ATT_KV)
    cv = cache_v.reshape(dec_batch, DEPTH, past, ATT_KV)

    xp = x_prompt.reshape(batch * seq, D_MODEL)
    xs = x_sample.reshape(dec_batch * dec_seq, D_MODEL)
    new_k, new_v, new_sf, new_sb = [], [], [], []
    for l in range(DEPTH):
        w_in_p = _pad_w_in(w_in[l])
        w_att, w_ret, w_four = (w_br_att[l].astype(BF16), w_br_ret[l].astype(BF16),
                                w_br_four[l].astype(BF16))
        w_o, w_u, w_d = w_out[l].astype(BF16), w_up[l].astype(BF16), w_down[l].astype(BF16)
        n1, n2 = norm1[l][None, :], norm2[l][None, :]
        qn, kn = q_norm[l][:, None], k_norm[l][:, None]
        rn = ret_norm[l][None, :]
        dec = jnp.broadcast_to(
            jnp.concatenate([ret_decay_f[l], ret_decay_b[l]])[:, None].astype(F32), (2 * RET_HEADS, 128))
        cw, cb = conv_w[l], conv_b[l][None, :]
        mod_ctx = mod_all[l, 0:1].reshape(1, 1, 6 * D_MODEL)
        mod_lat = mod_all[l, 1:].reshape(dec_batch, 1, 6 * D_MODEL)

        p_act, p_gate = _in_proj(xp, mod_ctx, batch * seq, n1, w_in_p)
        o_att, k_l = _att_ctx(p_act, seq, qn, kn)
        o_ret, sf_l, sb_l = _retention(p_act, seq, dec, rn)
        o_four = _fourier(p_act, seq)
        xp = _mix(o_att, o_ret, o_four, p_gate, xp, seq, mod_ctx, batch * seq,
                  w_att, w_ret, w_four, w_o, n2, w_u, cw, cb, w_d)
        new_k.append(k_l.reshape(batch, seq, N_KV_HEADS, HEAD_DIM))
        new_v.append(p_act[:, OFF_VA:OFF_VA + ATT_KV].reshape(batch, seq, N_KV_HEADS, HEAD_DIM))
        new_sf.append(sf_l)
        new_sb.append(sb_l)

        p_act, p_gate = _in_proj(xs, mod_lat, dec_seq, n1, w_in_p)
        o_att = _att_lat(p_act, dec_seq, ck, cv, l, qn, kn, cos_t, sin_t)
        o_ret = _retention(p_act, dec_seq, dec, rn, states=(state_ret_fwd, state_ret_bwd), layer=l)
        o_four = _fourier(p_act, dec_seq)
        xs = _mix(o_att, o_ret, o_four, p_gate, xs, dec_seq, mod_lat, dec_seq,
                  w_att, w_ret, w_four, w_o, n2, w_u, cw, cb, w_d)

    return (xp.reshape(batch, seq, D_MODEL), xs.reshape(dec_batch, dec_seq, D_MODEL),
            jnp.stack(new_k, axis=1), jnp.stack(new_v, axis=1),
            jnp.stack(new_sf, axis=1), jnp.stack(new_sb, axis=1))
```

```python
import functools

import numpy as np
import jax
import jax.numpy as jnp
from jax import lax
from jax.experimental import pallas as pl
from jax.experimental.pallas import tpu as pltpu

D_MODEL = 1024
DEPTH = 2
GRID_W = 64
HEAD_DIM = 64
N_HEADS = 8
N_KV_HEADS = 2
Q_PER_KV = N_HEADS // N_KV_HEADS
ATT_Q = N_HEADS * HEAD_DIM
ATT_KV = N_KV_HEADS * HEAD_DIM
RET_HEADS = 4
RET_DK = 64
RET_DV = 128
RET_QK = RET_HEADS * RET_DK
RET_V = RET_HEADS * RET_DV
FOURIER_GROUPS = 4
FOURIER_GROUP_DIM = 128
FOURIER_W = FOURIER_GROUPS * FOURIER_GROUP_DIM
D_FF = 2816
CHUNK = 128
Q_BLOCK = 128
ROPE_THETA = 10000.0
EPS = 1e-6
LOG2_E = 1.4426950408889634

F32 = jnp.float32
BF16 = jnp.bfloat16

OFF_QA, OFF_KA, OFF_VA = 0, 512, 640
OFF_QR, OFF_KR, OFF_VR, OFF_GR, OFF_UF = 768, 1024, 1280, 1792, 2304
W_IN_SPLIT = 2816
OFF_GATE = 3072
P_W = OFF_GATE + 3 * D_MODEL

VMEM_LIMIT = 56 * 1024 * 1024


def _cparams(sem, flags=None):
    return pltpu.CompilerParams(dimension_semantics=sem, vmem_limit_bytes=VMEM_LIMIT, flags=flags)


def _dot(a, b):
    return jnp.dot(a, b, preferred_element_type=F32)


def _ada_kernel(cond_ref, w_ref, b_ref, o_ref):
    cnd = cond_ref[...]
    s = cnd * jax.nn.sigmoid(cnd)
    o_ref[0] = _dot(s.astype(BF16), w_ref[0].astype(BF16)) + b_ref[0]


def _ada(cond_all, w_ada, b_ada):
    n = cond_all.shape[0]
    tn = 1024
    return pl.pallas_call(
        _ada_kernel,
        grid=(DEPTH, 6 * D_MODEL // tn),
        in_specs=[
            pl.BlockSpec((n, D_MODEL), lambda l, j: (0, 0)),
            pl.BlockSpec((1, D_MODEL, tn), lambda l, j: (l, 0, j)),
            pl.BlockSpec((1, 1, tn), lambda l, j: (l, 0, j)),
        ],
        out_specs=pl.BlockSpec((1, n, tn), lambda l, j: (l, 0, j)),
        out_shape=jax.ShapeDtypeStruct((DEPTH, n, 6 * D_MODEL), F32),
        compiler_params=_cparams(("parallel", "parallel")),
        name="ada",
    )(cond_all, w_ada, b_ada.reshape(DEPTH, 1, 6 * D_MODEL))


IN_TN = 1024
IN_ROWS = 256
MAIN_STEPS = OFF_GATE // IN_TN


def _in_kernel(x_ref, sh_ref, sc_ref, g_ref, w_ref, om_ref, og_ref, h_scr):
    j = pl.program_id(1)

    @pl.when(j == 0)
    def _():
        for b in range(x_ref.shape[0] // IN_ROWS):
            rows = slice(b * IN_ROWS, (b + 1) * IN_ROWS)
            x = x_ref[rows, :]
            r = lax.rsqrt(jnp.mean(x * x, axis=-1, keepdims=True) + EPS)
            h = ((x * r) * g_ref[...] * (1.0 + sc_ref[0]) + sh_ref[0]).astype(BF16)
            h_scr[rows, :] = h
            om_ref[rows, :] = _dot(h, w_ref[...])

    @pl.when(jnp.logical_and(j > 0, j < MAIN_STEPS))
    def _():
        om_ref[...] = _dot(h_scr[...], w_ref[...])

    @pl.when(j >= MAIN_STEPS)
    def _():
        og_ref[...] = _dot(h_scr[...], w_ref[...]).astype(og_ref.dtype)


def _in_proj(x2d, mod, rows_per_mod, norm1, w_in_p):
    n_tok = x2d.shape[0]
    tm, tn = 1024, IN_TN
    mod_idx = lambda i: (i * tm) // rows_per_mod
    return pl.pallas_call(
        _in_kernel,
        grid=(n_tok // tm, P_W // tn),
        in_specs=[
            pl.BlockSpec((tm, D_MODEL), lambda i, j: (i, 0)),
            pl.BlockSpec((1, 1, D_MODEL), lambda i, j: (mod_idx(i), 0, 0)),
            pl.BlockSpec((1, 1, D_MODEL), lambda i, j: (mod_idx(i), 0, 1)),
            pl.BlockSpec((1, D_MODEL), lambda i, j: (0, 0)),
            pl.BlockSpec((D_MODEL, tn), lambda i, j: (0, j)),
        ],
        out_specs=[
            pl.BlockSpec((tm, tn), lambda i, j: (i, jnp.minimum(j, MAIN_STEPS - 1))),
            pl.BlockSpec((tm, tn), lambda i, j: (i, jnp.maximum(j - MAIN_STEPS, 0))),
        ],
        out_shape=[jax.ShapeDtypeStruct((n_tok, OFF_GATE), F32),
                   jax.ShapeDtypeStruct((n_tok, P_W - OFF_GATE), BF16)],
        scratch_shapes=[pltpu.VMEM((tm, D_MODEL), BF16)],
        compiler_params=_cparams(("parallel", "arbitrary")),
        name="in_proj",
    )(x2d, mod, mod, norm1, w_in_p)


def _norm_rope_heads_t(x_t, n_heads, g_col, cos_t=None, sin_t=None):
    quarter = HEAD_DIM // 4
    outs = []
    for h in range(n_heads):
        x = x_t[h * HEAD_DIM:(h + 1) * HEAD_DIM, :]
        r = lax.rsqrt(jnp.mean(x * x, axis=0, keepdims=True) + EPS)
        y = (x * r) * g_col
        if cos_t is not None:
            pieces = []
            for a in range(2):
                c = cos_t[a * quarter:(a + 1) * quarter]
                s = sin_t[a * quarter:(a + 1) * quarter]
                x1 = y[2 * a * quarter:(2 * a + 1) * quarter]
                x2 = y[(2 * a + 1) * quarter:(2 * a + 2) * quarter]
                pieces += [x1 * c - x2 * s, x2 * c + x1 * s]
            y = jnp.concatenate(pieces, axis=0)
        outs.append(y)
    return outs


def _attend_t(q_heads_t, k_bf, v_t_bf):
    tq = q_heads_t[0].shape[1]
    cols = Q_PER_KV * tq
    zeros = jnp.zeros((HEAD_DIM, cols), F32)
    q_kv = [jnp.concatenate(q_heads_t[kv * Q_PER_KV:(kv + 1) * Q_PER_KV], axis=1)
            * (HEAD_DIM ** -0.5 * LOG2_E) for kv in range(N_KV_HEADS)]
    rhs = jnp.concatenate([jnp.concatenate([q_kv[0], zeros], axis=1),
                           jnp.concatenate([zeros, q_kv[1]], axis=1)], axis=0).astype(BF16)
    s_all = _dot(k_bf, rhs)
    out_rows = []
    for kv in range(N_KV_HEADS):
        s_t = s_all[:, kv * cols:(kv + 1) * cols]
        e = jnp.exp2(s_t - jnp.max(s_t, axis=0, keepdims=True))
        inv = 1.0 / jnp.sum(e, axis=0, keepdims=True)
        o_t = _dot(v_t_bf[kv * HEAD_DIM:(kv + 1) * HEAD_DIM, :], e.astype(BF16)) * inv
        out_rows += [o_t[:, g * tq:(g + 1) * tq] for g in range(Q_PER_KV)]
    return jnp.concatenate(out_rows, axis=0)


def _att_ctx_kernel(q_ref, k_ref, v_ref, qn_ref, kn_ref, o_ref, ko_ref):
    k_heads = _norm_rope_heads_t(k_ref[...].T, N_KV_HEADS, kn_ref[...])
    k_n = jnp.concatenate(k_heads, axis=0).T
    ko_ref[...] = k_n
    q_heads = _norm_rope_heads_t(q_ref[...].T, N_HEADS, qn_ref[...])
    o_t = _attend_t(q_heads, k_n.astype(BF16), v_ref[...].T.astype(BF16))
    o_ref[...] = o_t.T.astype(o_ref.dtype)


def _att_ctx(p_act, seq, q_norm, k_norm):
    n_tok = p_act.shape[0]
    return pl.pallas_call(
        _att_ctx_kernel,
        grid=(n_tok // seq,),
        in_specs=[
            pl.BlockSpec((seq, ATT_Q), lambda b: (b, OFF_QA // ATT_Q)),
            pl.BlockSpec((seq, ATT_KV), lambda b: (b, OFF_KA // ATT_KV)),
            pl.BlockSpec((seq, ATT_KV), lambda b: (b, OFF_VA // ATT_KV)),
            pl.BlockSpec((HEAD_DIM, 1), lambda b: (0, 0)),
            pl.BlockSpec((HEAD_DIM, 1), lambda b: (0, 0)),
        ],
        out_specs=[
            pl.BlockSpec((seq, ATT_Q), lambda b: (b, 0)),
            pl.BlockSpec((seq, ATT_KV), lambda b: (b, 0)),
        ],
        out_shape=[
            jax.ShapeDtypeStruct((n_tok, ATT_Q), BF16),
            jax.ShapeDtypeStruct((n_tok, ATT_KV), F32),
        ],
        compiler_params=_cparams(("parallel",)),
        name="att_ctx",
    )(p_act, p_act, p_act, q_norm, k_norm)


def _att_lat_kernel(q_ref, k_ref, v_ref, ck_ref, cv_ref, qn_ref, kn_ref,
                    cq_ref, sq_ref, ck_t_ref, sk_t_ref, o_ref, kf_scr, vt_scr, *, seq):
    @pl.when(pl.program_id(1) == 0)
    def _():
        k_heads = _norm_rope_heads_t(k_ref[...].T, N_KV_HEADS, kn_ref[...], ck_t_ref[...], sk_t_ref[...])
        kf_scr[0:seq, :] = jnp.concatenate(k_heads, axis=0).T.astype(BF16)
        kf_scr[seq:, :] = ck_ref[0, 0].astype(BF16)
        vt_scr[:, 0:seq] = v_ref[...].T.astype(BF16)
        vt_scr[:, seq:] = cv_ref[0, 0].T.astype(BF16)

    q_heads = _norm_rope_heads_t(q_ref[...].T, N_HEADS, qn_ref[...], cq_ref[...], sq_ref[...])
    o_ref[...] = _attend_t(q_heads, kf_scr[...], vt_scr[...]).T.astype(o_ref.dtype)


def _att_lat(p_act, seq, cache_k, cache_v, layer, q_norm, k_norm, cos_t, sin_t):
    n_tok = p_act.shape[0]
    nb = seq // Q_BLOCK
    past = cache_k.shape[2]
    return pl.pallas_call(
        functools.partial(_att_lat_kernel, seq=seq),
        grid=(n_tok // seq, nb),
        in_specs=[
            pl.BlockSpec((Q_BLOCK, ATT_Q), lambda b, i: (b * nb + i, OFF_QA // ATT_Q)),
            pl.BlockSpec((seq, ATT_KV), lambda b, i: (b, OFF_KA // ATT_KV)),
            pl.BlockSpec((seq, ATT_KV), lambda b, i: (b, OFF_VA // ATT_KV)),
            pl.BlockSpec((1, 1, past, ATT_KV), lambda b, i: (b, layer, 0, 0)),
            pl.BlockSpec((1, 1, past, ATT_KV), lambda b, i: (b, layer, 0, 0)),
            pl.BlockSpec((HEAD_DIM, 1), lambda b, i: (0, 0)),
            pl.BlockSpec((HEAD_DIM, 1), lambda b, i: (0, 0)),
            pl.BlockSpec((HEAD_DIM // 2, Q_BLOCK), lambda b, i: (0, i)),
            pl.BlockSpec((HEAD_DIM // 2, Q_BLOCK), lambda b, i: (0, i)),
            pl.BlockSpec((HEAD_DIM // 2, seq), lambda b, i: (0, 0)),
            pl.BlockSpec((HEAD_DIM // 2, seq), lambda b, i: (0, 0)),
        ],
        out_specs=pl.BlockSpec((Q_BLOCK, ATT_Q), lambda b, i: (b * nb + i, 0)),
        out_shape=jax.ShapeDtypeStruct((n_tok, ATT_Q), BF16),
        scratch_shapes=[pltpu.VMEM((seq + past, ATT_KV), BF16),
                        pltpu.VMEM((ATT_KV, seq + past), BF16)],
        compiler_params=_cparams(("parallel", "arbitrary")),
        name="att_lat",
    )(p_act, p_act, p_act, cache_k, cache_v, q_norm, k_norm, cos_t, sin_t, cos_t, sin_t)


RET_ROWS = 1024


def _log_sigmoid(d):
    return jnp.minimum(d, 0.0) - jnp.log1p(jnp.exp(-jnp.abs(d)))


def _ret_kernel(q_ref, k_ref, v_ref, g_ref, dec_ref, rn_ref, *rest, seq, has_state):
    if has_state:
        s0f_ref, s0b_ref, o_ref = rest
    else:
        o_ref, sf_ref, sb_ref = rest
    hp = pl.program_id(1)
    n_chunks = q_ref.shape[0] // CHUNK
    ii = lax.broadcasted_iota(jnp.int32, (CHUNK, CHUNK), 0)
    jj = lax.broadcasted_iota(jnp.int32, (CHUNK, CHUNK), 1)
    rel = (ii - jj).astype(F32)
    row = ii.astype(F32)
    lane = jj.astype(F32)
    lgf = [_log_sigmoid(dec_ref[pl.ds(2 * hp + t, 1), :]) for t in range(2)]
    lgb = [_log_sigmoid(dec_ref[pl.ds(RET_HEADS + 2 * hp + t, 1), :]) for t in range(2)]

    mask2 = jnp.concatenate(
        [jnp.where(rel > 0, jnp.exp(jnp.maximum(rel, 0.0) * lgf[t]),
                   jnp.where(rel < 0, jnp.exp(jnp.maximum(-rel, 0.0) * lgb[t]), 2.0))
         for t in range(2)], axis=0)
    qdec_f = [jnp.exp((row + 1.0) * lgf[t]) for t in range(2)]
    qdec_b = [jnp.exp((CHUNK - row) * lgb[t]) for t in range(2)]
    cdec_f = [jnp.exp(CHUNK * lgf[t]) for t in range(2)]
    cdec_b = [jnp.exp(CHUNK * lgb[t]) for t in range(2)]
    kdec_f = jnp.exp((CHUNK - 1.0 - lane) * jnp.where(ii < RET_DK, lgf[0], lgf[1]))
    kdec_b = jnp.exp(lane * jnp.where(ii < RET_DK, lgb[0], lgb[1]))

    k_t = (k_ref[...] * (RET_DK ** -0.5)).T
    first_head = jj < RET_DK

    def chunk(c):
        return slice(c * CHUNK, (c + 1) * CHUNK)

    kv_f = [[None] * n_chunks for _ in range(2)]
    kv_b = [[None] * n_chunks for _ in range(2)]
    for c in range(n_chunks):
        k_c = k_t[:, chunk(c)]
        kd = jnp.concatenate([k_c * kdec_f, k_c * kdec_b], axis=0).astype(BF16)
        kv = _dot(kd, v_ref[chunk(c), :].astype(BF16))
        for t in range(2):
            kv_f[t][c] = kv[t * RET_DK:(t + 1) * RET_DK, t * RET_DV:(t + 1) * RET_DV]
            kv_b[t][c] = kv[CHUNK + t * RET_DK:CHUNK + (t + 1) * RET_DK, t * RET_DV:(t + 1) * RET_DV]

    st_f = [[None] * n_chunks for _ in range(2)]
    st_b = [[None] * n_chunks for _ in range(2)]
    per_seq = seq // CHUNK
    for t in range(2):
        for sq in range(n_chunks // per_seq):
            own = range(sq * per_seq, (sq + 1) * per_seq)
            if has_state:
                s_f = s0f_ref[sq, 0, t]
                s_b = s0b_ref[sq, 0, t]
            else:
                s_f = jnp.zeros((RET_DK, RET_DV), F32)
                s_b = s_f
            for c in own:
                st_f[t][c] = s_f
                s_f = s_f * cdec_f[t] + kv_f[t][c]
            for c in reversed(own):
                st_b[t][c] = s_b
                s_b = s_b * cdec_b[t] + kv_b[t][c]
            if not has_state:
                sf_ref[sq, t] = s_f
                sb_ref[sq, t] = s_b

    for c in range(n_chunks):
        q_c = q_ref[chunk(c), :]
        qm = jnp.concatenate([jnp.where(first_head, q_c, 0.0), jnp.where(first_head, 0.0, q_c)],
                             axis=0).astype(BF16)
        att = (_dot(qm, k_t[:, chunk(c)].astype(BF16)) * mask2).astype(BF16)
        states = jnp.concatenate(
            [jnp.concatenate([st_f[t][c], st_b[t][c]], axis=1) for t in range(2)], axis=0)
        qs = _dot(qm, states.astype(BF16))
        v_c = v_ref[chunk(c), :].astype(BF16)
        for t in range(2):
            rows = slice(t * CHUNK, (t + 1) * CHUNK)
            vsl = slice(t * RET_DV, (t + 1) * RET_DV)
            o = (_dot(att[rows], v_c[:, vsl]) + qs[rows, :RET_DV] * qdec_f[t]
                 + qs[rows, RET_DV:] * qdec_b[t])
            o = (o * lax.rsqrt(jnp.mean(o * o, axis=-1, keepdims=True) + EPS)) * rn_ref[...]
            g = g_ref[chunk(c), vsl]
            o_ref[chunk(c), vsl] = (o * (g * jax.nn.sigmoid(g))).astype(o_ref.dtype)


def _retention(p_act, seq, dec, ret_norm, states=None, layer=0):
    n_tok = p_act.shape[0]
    n_seq = n_tok // seq
    has_state = states is not None
    pair_qk, pair_v = 2 * RET_DK, 2 * RET_DV
    rows = max(seq, RET_ROWS)
    blk_seqs = rows // seq
    in_specs = [
        pl.BlockSpec((rows, pair_qk), lambda s, p: (s, OFF_QR // pair_qk + p)),
        pl.BlockSpec((rows, pair_qk), lambda s, p: (s, OFF_KR // pair_qk + p)),
        pl.BlockSpec((rows, pair_v), lambda s, p: (s, OFF_VR // pair_v + p)),
        pl.BlockSpec((rows, pair_v), lambda s, p: (s, OFF_GR // pair_v + p)),
        pl.BlockSpec((8, 128), lambda s, p: (0, 0)),
        pl.BlockSpec((1, RET_DV), lambda s, p: (0, 0)),
    ]
    args = [p_act, p_act, p_act, p_act, dec, ret_norm]
    o_spec = pl.BlockSpec((rows, pair_v), lambda s, p: (s, p))
    o_shape = jax.ShapeDtypeStruct((n_tok, RET_V), BF16)
    if has_state:
        st_spec = pl.BlockSpec((blk_seqs, 1, 2, RET_DK, RET_DV), lambda s, p: (s, layer, p, 0, 0))
        in_specs += [st_spec, st_spec]
        args += list(states)
        out_specs, out_shape = o_spec, o_shape
    else:
        st_spec = pl.BlockSpec((blk_seqs, 2, RET_DK, RET_DV), lambda s, p: (s, p, 0, 0))
        st_shape = jax.ShapeDtypeStruct((n_seq, RET_HEADS, RET_DK, RET_DV), F32)
        out_specs, out_shape = [o_spec, st_spec, st_spec], [o_shape, st_shape, st_shape]
    return pl.pallas_call(
        functools.partial(_ret_kernel, seq=seq, has_state=has_state),
        grid=(n_tok // rows, RET_HEADS // 2),
        in_specs=in_specs,
        out_specs=out_specs,
        out_shape=out_shape,
        compiler_params=_cparams(("parallel", "parallel")),
        name="retention_lat" if has_state else "retention_ctx",
    )(*args)


def _four_kernel(u0_ref, u1_ref, wc_ref, csn_ref, o_ref):
    gd = FOURIER_GROUP_DIM
    tc, ts = [], []
    for g in range(FOURIER_GROUPS):
        u_ref = (u0_ref, u1_ref)[g // 2]
        u_g = u_ref[:, (g % 2) * gd:(g % 2 + 1) * gd].astype(BF16)
        t = _dot(u_g, wc_ref[...])
        tc.append(t[:, :gd])
        ts.append(t[:, gd:])
    t_all = jnp.concatenate([jnp.concatenate(tc, axis=1), jnp.concatenate(ts, axis=1)], axis=0)
    o_ref[...] = _dot(csn_ref[...], t_all.astype(BF16)).astype(o_ref.dtype)


def _dft_tables(n):
    k = np.arange(n, dtype=np.int64)
    ang = 2.0 * np.pi * ((k[:, None] * k[None, :]) % n).astype(np.float64) / n
    scale = 1.0 / np.sqrt(n)
    return np.cos(ang) * scale, np.sin(ang) * scale


def _fourier(p_act, seq):
    n_tok = p_act.shape[0]
    cc, sc = _dft_tables(FOURIER_GROUP_DIM)
    wc = jnp.asarray(np.concatenate([cc, -sc], axis=1), F32).astype(BF16)
    cn, sn = _dft_tables(seq)
    csn = jnp.asarray(np.concatenate([cn, sn], axis=1), F32).astype(BF16)
    const = lambda shape: pl.BlockSpec(shape, lambda s: (0, 0))
    half = FOURIER_W // 2
    return pl.pallas_call(
        _four_kernel,
        grid=(n_tok // seq,),
        in_specs=[
            pl.BlockSpec((seq, half), lambda s: (s, OFF_UF // half)),
            pl.BlockSpec((seq, half), lambda s: (s, OFF_UF // half + 1)),
            const((FOURIER_GROUP_DIM, 2 * FOURIER_GROUP_DIM)),
            const((seq, 2 * seq)),
        ],
        out_specs=pl.BlockSpec((seq, FOURIER_W), lambda s: (s, 0)),
        out_shape=jax.ShapeDtypeStruct((n_tok, FOURIER_W), BF16),
        compiler_params=_cparams(("parallel",)),
        name="fourier",
    )(p_act, p_act, wc, csn)


MIX_TM = 1024
MERGE_STEPS = 4
MERGE_TM = MIX_TM // MERGE_STEPS
MERGE_ROWS = 256
FF_TILE = 256
FF_STEPS = D_FF // FF_TILE
FF_ROWS = 256


def _mix_kernel(oa_ref, or_ref, of_ref, ga_ref, gr_ref, gf_ref, x_ref, g1_ref, sh_ref, sc_ref,
                g2_ref, n2_ref, wba_ref, wbr_ref, wbf_ref, wo_ref, wa_ref, wv_ref, cw_ref, cb_ref,
                wd_ref, o_ref, x1_scr, h_scr, acc_scr, *, seq):
    j = pl.program_id(1)

    @pl.when(j < MERGE_STEPS)
    def _():
        gate = lambda ref, rows: 0.5 * (jnp.tanh(0.5 * ref[rows, :].astype(F32)) + 1.0)
        for b in range(MERGE_TM // MERGE_ROWS):
            rows = slice(b * MERGE_ROWS, (b + 1) * MERGE_ROWS)
            dst = pl.ds(pl.multiple_of(j * MERGE_TM + b * MERGE_ROWS, MERGE_ROWS), MERGE_ROWS)
            merged = (gate(ga_ref, rows) * _dot(oa_ref[rows, :], wba_ref[...])
                      + gate(gr_ref, rows) * _dot(or_ref[rows, :], wbr_ref[...])
                      + gate(gf_ref, rows) * _dot(of_ref[rows, :], wbf_ref[...]))
            x1 = x_ref[rows, :] + g1_ref[0] * _dot(merged.astype(BF16), wo_ref[...])
            x1_scr[dst, :] = x1
            r = lax.rsqrt(jnp.mean(x1 * x1, axis=-1, keepdims=True) + EPS)
            h = (x1 * r) * n2_ref[...] * (1.0 + sc_ref[0]) + sh_ref[0]
            h_scr[dst, :] = h.astype(BF16)
            acc_scr[dst, :] = jnp.zeros((MERGE_ROWS, D_MODEL), F32)

    @pl.when(j >= MERGE_STEPS)
    def _():
        blocks = [slice(b * FF_ROWS, (b + 1) * FF_ROWS) for b in range(MIX_TM // FF_ROWS)]
        a = jnp.concatenate([_dot(h_scr[rows, :], wa_ref[...]) for rows in blocks], axis=0)
        val = [_dot(h_scr[rows, :], wv_ref[...]) for rows in blocks]
        pos = lax.broadcasted_iota(jnp.int32, a.shape, 0) % seq
        prev = jnp.where(pos == 0, 0.0, pltpu.roll(a, 1, 0))
        nxt = jnp.where(pos == seq - 1, 0.0, pltpu.roll(a, MIX_TM - 1, 0))
        cw = cw_ref[...]
        ac = prev * cw[0:1] + a * cw[1:2] + nxt * cw[2:3] + cb_ref[...]
        for b, rows in enumerate(blocks):
            act = jax.nn.gelu(ac[rows]) * val[b]
            acc_scr[rows, :] += _dot(act.astype(BF16), wd_ref[...])

    @pl.when(j == pl.num_programs(1) - 1)
    def _():
        o_ref[...] = x1_scr[...] + g2_ref[0] * acc_scr[...]


def _mix(o_att, o_ret, o_four, p_gate, x2d, seq, mod, rows_per_mod, w_att, w_ret, w_four, w_out,
         norm2, w_up, conv_w, conv_b, w_down):
    n_tok = x2d.shape[0]
    mod_idx = lambda i: (i * MIX_TM) // rows_per_mod
    sub = lambda i, j: i * MERGE_STEPS + jnp.minimum(j, MERGE_STEPS - 1)
    ff = lambda j: jnp.maximum(j - MERGE_STEPS, 0)
    br = lambda: pl.BlockSpec((MERGE_TM, 512), lambda i, j: (sub(i, j), 0))
    gate = lambda k: pl.BlockSpec((MERGE_TM, D_MODEL), lambda i, j: (sub(i, j), k))
    modv = lambda k: pl.BlockSpec((1, 1, D_MODEL), lambda i, j: (mod_idx(i), 0, k))
    const = lambda shape: pl.BlockSpec(shape, lambda i, j: (0, 0))
    return pl.pallas_call(
        functools.partial(_mix_kernel, seq=seq),
        grid=(n_tok // MIX_TM, MERGE_STEPS + FF_STEPS),
        in_specs=[
            br(), br(), br(), gate(0), gate(1), gate(2),
            pl.BlockSpec((MERGE_TM, D_MODEL), lambda i, j: (sub(i, j), 0)),
            modv(2), modv(3), modv(4), modv(5),
            const((1, D_MODEL)),
            const((512, D_MODEL)), const((512, D_MODEL)), const((512, D_MODEL)),
            const((D_MODEL, D_MODEL)),
            pl.BlockSpec((D_MODEL, FF_TILE), lambda i, j: (0, ff(j))),
            pl.BlockSpec((D_MODEL, FF_TILE), lambda i, j: (0, FF_STEPS + ff(j))),
            pl.BlockSpec((3, FF_TILE), lambda i, j: (0, ff(j))),
            pl.BlockSpec((1, FF_TILE), lambda i, j: (0, ff(j))),
            pl.BlockSpec((FF_TILE, D_MODEL), lambda i, j: (ff(j), 0)),
        ],
        out_specs=pl.BlockSpec((MIX_TM, D_MODEL), lambda i, j: (i, 0)),
        out_shape=jax.ShapeDtypeStruct((n_tok, D_MODEL), F32),
        scratch_shapes=[pltpu.VMEM((MIX_TM, D_MODEL), F32), pltpu.VMEM((MIX_TM, D_MODEL), BF16),
                        pltpu.VMEM((MIX_TM, D_MODEL), F32)],
        compiler_params=_cparams(("parallel", "arbitrary")),
        name="mix",
    )(o_att, o_ret, o_four, p_gate, p_gate, p_gate, x2d, mod, mod, mod, mod, norm2,
      w_att, w_ret, w_four, w_out, w_up, w_up, conv_w, conv_b, w_down)


def _pad_w_in(w):
    pad = jnp.zeros((w.shape[0], OFF_GATE - W_IN_SPLIT), BF16)
    return jnp.concatenate([w[:, :W_IN_SPLIT].astype(BF16), pad, w[:, W_IN_SPLIT:].astype(BF16)], axis=1)


def _rope_tables(n_tok):
    rows = n_tok // GRID_W
    row_id = jnp.repeat(jnp.arange(rows), GRID_W).astype(F32)
    col_id = jnp.tile(jnp.arange(GRID_W), rows).astype(F32)
    n_freq = HEAD_DIM // 4
    inv = ROPE_THETA ** (-jnp.arange(n_freq, dtype=F32) / n_freq)
    ang = jnp.concatenate([row_id[None, :] * inv[:, None], col_id[None, :] * inv[:, None]], axis=0)
    return jnp.cos(ang), jnp.sin(ang)


def kernel(x_prompt, x_sample, cache_k, cache_v, state_ret_fwd, state_ret_bwd, c, c_ctx, w_ada, b_ada, norm1, w_in, q_norm, k_norm, ret_decay_f, ret_decay_b, ret_norm, w_br_att, w_br_ret, w_br_four, w_out, norm2, w_up, conv_w, conv_b, w_down):
    batch, seq, _ = x_prompt.shape
    dec_batch, dec_seq, _ = x_sample.shape
    past = cache_k.shape[2]

    cond_all = jnp.concatenate([c_ctx[None, :], c], axis=0)
    mod_all = _ada(cond_all, w_ada, b_ada)
    cos_t, sin_t = _rope_tables(dec_seq)
    ck = cache_k.reshape(dec_batch, DEPTH, past, ATT_KV)
    cv = cache_v.reshape(dec_batch, DEPTH, past, ATT_KV)

    xp = x_prompt.reshape(batch * seq, D_MODEL)
    xs = x_sample.reshape(dec_batch * dec_seq, D_MODEL)
    new_k, new_v, new_sf, new_sb = [], [], [], []
    for l in range(DEPTH):
        w_in_p = _pad_w_in(w_in[l])
        w_att, w_ret, w_four = (w_br_att[l].astype(BF16), w_br_ret[l].astype(BF16),
                                w_br_four[l].astype(BF16))
        w_o, w_u, w_d = w_out[l].astype(BF16), w_up[l].astype(BF16), w_down[l].astype(BF16)
        n1, n2 = norm1[l][None, :], norm2[l][None, :]
        qn, kn = q_norm[l][:, None], k_norm[l][:, None]
        rn = ret_norm[l][None, :]
        dec = jnp.broadcast_to(
            jnp.concatenate([ret_decay_f[l], ret_decay_b[l]])[:, None].astype(F32), (2 * RET_HEADS, 128))
        cw, cb = conv_w[l], conv_b[l][None, :]
        mod_ctx = mod_all[l, 0:1].reshape(1, 1, 6 * D_MODEL)
        mod_lat = mod_all[l, 1:].reshape(dec_batch, 1, 6 * D_MODEL)

        p_act, p_gate = _in_proj(xp, mod_ctx, batch * seq, n1, w_in_p)
        o_att, k_l = _att_ctx(p_act, seq, qn, kn)
        o_ret, sf_l, sb_l = _retention(p_act, seq, dec, rn)
        o_four = _fourier(p_act, seq)
        xp = _mix(o_att, o_ret, o_four, p_gate, xp, seq, mod_ctx, batch * seq,
                  w_att, w_ret, w_four, w_o, n2, w_u, cw, cb, w_d)
        new_k.append(k_l.reshape(batch, seq, N_KV_HEADS, HEAD_DIM))
        new_v.append(p_act[:, OFF_VA:OFF_VA + ATT_KV].reshape(batch, seq, N_KV_HEADS, HEAD_DIM))
        new_sf.append(sf_l)
        new_sb.append(sb_l)

        p_act, p_gate = _in_proj(xs, mod_lat, dec_seq, n1, w_in_p)
        o_att = _att_lat(p_act, dec_seq, ck, cv, l, qn, kn, cos_t, sin_t)
        o_ret = _retention(p_act, dec_seq, dec, rn, states=(state_ret_fwd, state_ret_bwd), layer=l)
        o_four = _fourier(p_act, dec_seq)
        xs = _mix(o_att, o_ret, o_four, p_gate, xs, dec_seq, mod_lat, dec_seq,
                  w_att, w_ret, w_four, w_o, n2, w_u, cw, cb, w_d)

    return (xp.reshape(batch, seq, D_MODEL), xs.reshape(dec_batch, dec_seq, D_MODEL),
            jnp.stack(new_k, axis=1), jnp.stack(new_v, axis=1),
            jnp.stack(new_sf, axis=1), jnp.stack(new_sb, axis=1))
```

```python
import functools

import numpy as np
import jax
import jax.numpy as jnp
from jax import lax
from jax.experimental import pallas as pl
from jax.experimental.pallas import tpu as pltpu

D_MODEL = 1024
DEPTH = 2
GRID_W = 64
HEAD_DIM = 64
N_HEADS = 8
N_KV_HEADS = 2
Q_PER_KV = N_HEADS // N_KV_HEADS
ATT_Q = N_HEADS * HEAD_DIM
ATT_KV = N_KV_HEADS * HEAD_DIM
RET_HEADS = 4
RET_DK = 64
RET_DV = 128
RET_QK = RET_HEADS * RET_DK
RET_V = RET_HEADS * RET_DV
FOURIER_GROUPS = 4
FOURIER_GROUP_DIM = 128
FOURIER_W = FOURIER_GROUPS * FOURIER_GROUP_DIM
D_FF = 2816
CHUNK = 128
Q_BLOCK = 256
ROPE_THETA = 10000.0
EPS = 1e-6
LOG2_E = 1.4426950408889634

F32 = jnp.float32
BF16 = jnp.bfloat16

OFF_QA, OFF_KA, OFF_VA = 0, 512, 640
OFF_QR, OFF_KR, OFF_VR, OFF_GR, OFF_UF = 768, 1024, 1280, 1792, 2304
W_IN_SPLIT = 2816
OFF_GATE = 3072
P_W = OFF_GATE + 3 * D_MODEL

VMEM_LIMIT = 56 * 1024 * 1024


def _cparams(sem, flags=None):
    return pltpu.CompilerParams(dimension_semantics=sem, vmem_limit_bytes=VMEM_LIMIT, flags=flags)


def _dot(a, b):
    return jnp.dot(a, b, preferred_element_type=F32)


def _ada_kernel(cond_ref, w_ref, b_ref, o_ref):
    cnd = cond_ref[...]
    s = cnd * jax.nn.sigmoid(cnd)
    o_ref[0] = _dot(s.astype(BF16), w_ref[0].astype(BF16)) + b_ref[0]


def _ada(cond_all, w_ada, b_ada):
    n = cond_all.shape[0]
    tn = 1024
    return pl.pallas_call(
        _ada_kernel,
        grid=(DEPTH, 6 * D_MODEL // tn),
        in_specs=[
            pl.BlockSpec((n, D_MODEL), lambda l, j: (0, 0)),
            pl.BlockSpec((1, D_MODEL, tn), lambda l, j: (l, 0, j)),
            pl.BlockSpec((1, 1, tn), lambda l, j: (l, 0, j)),
        ],
        out_specs=pl.BlockSpec((1, n, tn), lambda l, j: (l, 0, j)),
        out_shape=jax.ShapeDtypeStruct((DEPTH, n, 6 * D_MODEL), F32),
        compiler_params=_cparams(("parallel", "parallel")),
        name="ada",
    )(cond_all, w_ada, b_ada.reshape(DEPTH, 1, 6 * D_MODEL))


IN_TN = 1024
IN_ROWS = 256
MAIN_STEPS = OFF_GATE // IN_TN


def _in_kernel(x_ref, sh_ref, sc_ref, g_ref, w_ref, om_ref, og_ref, h_scr):
    j = pl.program_id(1)

    @pl.when(j == 0)
    def _():
        for b in range(x_ref.shape[0] // IN_ROWS):
            rows = slice(b * IN_ROWS, (b + 1) * IN_ROWS)
            x = x_ref[rows, :]
            r = lax.rsqrt(jnp.mean(x * x, axis=-1, keepdims=True) + EPS)
            h = ((x * r) * g_ref[...] * (1.0 + sc_ref[0]) + sh_ref[0]).astype(BF16)
            h_scr[rows, :] = h
            om_ref[rows, :] = _dot(h, w_ref[...])

    @pl.when(jnp.logical_and(j > 0, j < MAIN_STEPS))
    def _():
        om_ref[...] = _dot(h_scr[...], w_ref[...])

    @pl.when(j >= MAIN_STEPS)
    def _():
        og_ref[...] = _dot(h_scr[...], w_ref[...]).astype(og_ref.dtype)


def _in_proj(x2d, mod, rows_per_mod, norm1, w_in_p):
    n_tok = x2d.shape[0]
    tm, tn = 1024, IN_TN
    mod_idx = lambda i: (i * tm) // rows_per_mod
    return pl.pallas_call(
        _in_kernel,
        grid=(n_tok // tm, P_W // tn),
        in_specs=[
            pl.BlockSpec((tm, D_MODEL), lambda i, j: (i, 0)),
            pl.BlockSpec((1, 1, D_MODEL), lambda i, j: (mod_idx(i), 0, 0)),
            pl.BlockSpec((1, 1, D_MODEL), lambda i, j: (mod_idx(i), 0, 1)),
            pl.BlockSpec((1, D_MODEL), lambda i, j: (0, 0)),
            pl.BlockSpec((D_MODEL, tn), lambda i, j: (0, j)),
        ],
        out_specs=[
            pl.BlockSpec((tm, tn), lambda i, j: (i, jnp.minimum(j, MAIN_STEPS - 1))),
            pl.BlockSpec((tm, tn), lambda i, j: (i, jnp.maximum(j - MAIN_STEPS, 0))),
        ],
        out_shape=[jax.ShapeDtypeStruct((n_tok, OFF_GATE), F32),
                   jax.ShapeDtypeStruct((n_tok, P_W - OFF_GATE), BF16)],
        scratch_shapes=[pltpu.VMEM((tm, D_MODEL), BF16)],
        compiler_params=_cparams(("parallel", "arbitrary")),
        name="in_proj",
    )(x2d, mod, mod, norm1, w_in_p)


def _norm_rope_heads_t(x_t, n_heads, g_col, cos_t=None, sin_t=None):
    quarter = HEAD_DIM // 4
    outs = []
    for h in range(n_heads):
        x = x_t[h * HEAD_DIM:(h + 1) * HEAD_DIM, :]
        r = lax.rsqrt(jnp.mean(x * x, axis=0, keepdims=True) + EPS)
        y = (x * r) * g_col
        if cos_t is not None:
            pieces = []
            for a in range(2):
                c = cos_t[a * quarter:(a + 1) * quarter]
                s = sin_t[a * quarter:(a + 1) * quarter]
                x1 = y[2 * a * quarter:(2 * a + 1) * quarter]
                x2 = y[(2 * a + 1) * quarter:(2 * a + 2) * quarter]
                pieces += [x1 * c - x2 * s, x2 * c + x1 * s]
            y = jnp.concatenate(pieces, axis=0)
        outs.append(y)
    return outs


def _attend_t(q_heads_t, k_bf, v_t_bf):
    tq = q_heads_t[0].shape[1]
    cols = Q_PER_KV * tq
    zeros = jnp.zeros((HEAD_DIM, cols), F32)
    q_kv = [jnp.concatenate(q_heads_t[kv * Q_PER_KV:(kv + 1) * Q_PER_KV], axis=1)
            * (HEAD_DIM ** -0.5 * LOG2_E) for kv in range(N_KV_HEADS)]
    rhs = jnp.concatenate([jnp.concatenate([q_kv[0], zeros], axis=1),
                           jnp.concatenate([zeros, q_kv[1]], axis=1)], axis=0).astype(BF16)
    s_all = _dot(k_bf, rhs)
    out_rows = []
    for kv in range(N_KV_HEADS):
        s_t = s_all[:, kv * cols:(kv + 1) * cols]
        e = jnp.exp2(s_t - jnp.max(s_t, axis=0, keepdims=True))
        inv = 1.0 / jnp.sum(e, axis=0, keepdims=True)
        o_t = _dot(v_t_bf[kv * HEAD_DIM:(kv + 1) * HEAD_DIM, :], e.astype(BF16)) * inv
        out_rows += [o_t[:, g * tq:(g + 1) * tq] for g in range(Q_PER_KV)]
    return jnp.concatenate(out_rows, axis=0)


def _att_ctx_kernel(q_ref, k_ref, v_ref, qn_ref, kn_ref, o_ref, ko_ref):
    k_heads = _norm_rope_heads_t(k_ref[...].T, N_KV_HEADS, kn_ref[...])
    k_n = jnp.concatenate(k_heads, axis=0).T
    ko_ref[...] = k_n
    q_heads = _norm_rope_heads_t(q_ref[...].T, N_HEADS, qn_ref[...])
    o_t = _attend_t(q_heads, k_n.astype(BF16), v_ref[...].T.astype(BF16))
    o_ref[...] = o_t.T.astype(o_ref.dtype)


def _att_ctx(p_act, seq, q_norm, k_norm):
    n_tok = p_act.shape[0]
    return pl.pallas_call(
        _att_ctx_kernel,
        grid=(n_tok // seq,),
        in_specs=[
            pl.BlockSpec((seq, ATT_Q), lambda b: (b, OFF_QA // ATT_Q)),
            pl.BlockSpec((seq, ATT_KV), lambda b: (b, OFF_KA // ATT_KV)),
            pl.BlockSpec((seq, ATT_KV), lambda b: (b, OFF_VA // ATT_KV)),
            pl.BlockSpec((HEAD_DIM, 1), lambda b: (0, 0)),
            pl.BlockSpec((HEAD_DIM, 1), lambda b: (0, 0)),
        ],
        out_specs=[
            pl.BlockSpec((seq, ATT_Q), lambda b: (b, 0)),
            pl.BlockSpec((seq, ATT_KV), lambda b: (b, 0)),
        ],
        out_shape=[
            jax.ShapeDtypeStruct((n_tok, ATT_Q), BF16),
            jax.ShapeDtypeStruct((n_tok, ATT_KV), F32),
        ],
        compiler_params=_cparams(("parallel",)),
        name="att_ctx",
    )(p_act, p_act, p_act, q_norm, k_norm)


def _att_lat_kernel(q_ref, k_ref, v_ref, ck_ref, cv_ref, qn_ref, kn_ref,
                    cq_ref, sq_ref, ck_t_ref, sk_t_ref, o_ref, kf_scr, vt_scr, *, seq):
    @pl.when(pl.program_id(1) == 0)
    def _():
        k_heads = _norm_rope_heads_t(k_ref[...].T, N_KV_HEADS, kn_ref[...], ck_t_ref[...], sk_t_ref[...])
        kf_scr[0:seq, :] = jnp.concatenate(k_heads, axis=0).T.astype(BF16)
        kf_scr[seq:, :] = ck_ref[0, 0].astype(BF16)
        vt_scr[:, 0:seq] = v_ref[...].T.astype(BF16)
        vt_scr[:, seq:] = cv_ref[0, 0].T.astype(BF16)

    q_heads = _norm_rope_heads_t(q_ref[...].T, N_HEADS, qn_ref[...], cq_ref[...], sq_ref[...])
    o_ref[...] = _attend_t(q_heads, kf_scr[...], vt_scr[...]).T.astype(o_ref.dtype)


def _att_lat(p_act, seq, cache_k, cache_v, layer, q_norm, k_norm, cos_t, sin_t):
    n_tok = p_act.shape[0]
    nb = seq // Q_BLOCK
    past = cache_k.shape[2]
    return pl.pallas_call(
        functools.partial(_att_lat_kernel, seq=seq),
        grid=(n_tok // seq, nb),
        in_specs=[
            pl.BlockSpec((Q_BLOCK, ATT_Q), lambda b, i: (b * nb + i, OFF_QA // ATT_Q)),
            pl.BlockSpec((seq, ATT_KV), lambda b, i: (b, OFF_KA // ATT_KV)),
            pl.BlockSpec((seq, ATT_KV), lambda b, i: (b, OFF_VA // ATT_KV)),
            pl.BlockSpec((1, 1, past, ATT_KV), lambda b, i: (b, layer, 0, 0)),
            pl.BlockSpec((1, 1, past, ATT_KV), lambda b, i: (b, layer, 0, 0)),
            pl.BlockSpec((HEAD_DIM, 1), lambda b, i: (0, 0)),
            pl.BlockSpec((HEAD_DIM, 1), lambda b, i: (0, 0)),
            pl.BlockSpec((HEAD_DIM // 2, Q_BLOCK), lambda b, i: (0, i)),
            pl.BlockSpec((HEAD_DIM // 2, Q_BLOCK), lambda b, i: (0, i)),
            pl.BlockSpec((HEAD_DIM // 2, seq), lambda b, i: (0, 0)),
            pl.BlockSpec((HEAD_DIM // 2, seq), lambda b, i: (0, 0)),
        ],
        out_specs=pl.BlockSpec((Q_BLOCK, ATT_Q), lambda b, i: (b * nb + i, 0)),
        out_shape=jax.ShapeDtypeStruct((n_tok, ATT_Q), BF16),
        scratch_shapes=[pltpu.VMEM((seq + past, ATT_KV), BF16),
                        pltpu.VMEM((ATT_KV, seq + past), BF16)],
        compiler_params=_cparams(("parallel", "arbitrary")),
        name="att_lat",
    )(p_act, p_act, p_act, cache_k, cache_v, q_norm, k_norm, cos_t, sin_t, cos_t, sin_t)


RET_ROWS = 1024


def _log_sigmoid(d):
    return jnp.minimum(d, 0.0) - jnp.log1p(jnp.exp(-jnp.abs(d)))


def _ret_kernel(q_ref, k_ref, v_ref, g_ref, dec_ref, rn_ref, *rest, seq, has_state):
    if has_state:
        s0f_ref, s0b_ref, o_ref = rest
    else:
        o_ref, sf_ref, sb_ref = rest
    hp = pl.program_id(1)
    n_chunks = q_ref.shape[0] // CHUNK
    ii = lax.broadcasted_iota(jnp.int32, (CHUNK, CHUNK), 0)
    jj = lax.broadcasted_iota(jnp.int32, (CHUNK, CHUNK), 1)
    rel = (ii - jj).astype(F32)
    row = ii.astype(F32)
    lane = jj.astype(F32)
    lgf = [_log_sigmoid(dec_ref[pl.ds(2 * hp + t, 1), :]) for t in range(2)]
    lgb = [_log_sigmoid(dec_ref[pl.ds(RET_HEADS + 2 * hp + t, 1), :]) for t in range(2)]

    mask2 = jnp.concatenate(
        [jnp.where(rel > 0, jnp.exp(jnp.maximum(rel, 0.0) * lgf[t]),
                   jnp.where(rel < 0, jnp.exp(jnp.maximum(-rel, 0.0) * lgb[t]), 2.0))
         for t in range(2)], axis=0)
    qdec_f = [jnp.exp((row + 1.0) * lgf[t]) for t in range(2)]
    qdec_b = [jnp.exp((CHUNK - row) * lgb[t]) for t in range(2)]
    cdec_f = [jnp.exp(CHUNK * lgf[t]) for t in range(2)]
    cdec_b = [jnp.exp(CHUNK * lgb[t]) for t in range(2)]
    kdec_f = jnp.exp((CHUNK - 1.0 - lane) * jnp.where(ii < RET_DK, lgf[0], lgf[1]))
    kdec_b = jnp.exp(lane * jnp.where(ii < RET_DK, lgb[0], lgb[1]))

    k_t = (k_ref[...] * (RET_DK ** -0.5)).T
    first_head = jj < RET_DK

    def chunk(c):
        return slice(c * CHUNK, (c + 1) * CHUNK)

    kv_f = [[None] * n_chunks for _ in range(2)]
    kv_b = [[None] * n_chunks for _ in range(2)]
    for c in range(n_chunks):
        k_c = k_t[:, chunk(c)]
        kd = jnp.concatenate([k_c * kdec_f, k_c * kdec_b], axis=0).astype(BF16)
        kv = _dot(kd, v_ref[chunk(c), :].astype(BF16))
        for t in range(2):
            kv_f[t][c] = kv[t * RET_DK:(t + 1) * RET_DK, t * RET_DV:(t + 1) * RET_DV]
            kv_b[t][c] = kv[CHUNK + t * RET_DK:CHUNK + (t + 1) * RET_DK, t * RET_DV:(t + 1) * RET_DV]

    st_f = [[None] * n_chunks for _ in range(2)]
    st_b = [[None] * n_chunks for _ in range(2)]
    per_seq = seq // CHUNK
    for t in range(2):
        for sq in range(n_chunks // per_seq):
            own = range(sq * per_seq, (sq + 1) * per_seq)
            if has_state:
                s_f = s0f_ref[sq, 0, t]
                s_b = s0b_ref[sq, 0, t]
            else:
                s_f = jnp.zeros((RET_DK, RET_DV), F32)
                s_b = s_f
            for c in own:
                st_f[t][c] = s_f
                s_f = s_f * cdec_f[t] + kv_f[t][c]
            for c in reversed(own):
                st_b[t][c] = s_b
                s_b = s_b * cdec_b[t] + kv_b[t][c]
            if not has_state:
                sf_ref[sq, t] = s_f
                sb_ref[sq, t] = s_b

    for c in range(n_chunks):
        q_c = q_ref[chunk(c), :]
        qm = jnp.concatenate([jnp.where(first_head, q_c, 0.0), jnp.where(first_head, 0.0, q_c)],
                             axis=0).astype(BF16)
        att = (_dot(qm, k_t[:, chunk(c)].astype(BF16)) * mask2).astype(BF16)
        states = jnp.concatenate(
            [jnp.concatenate([st_f[t][c], st_b[t][c]], axis=1) for t in range(2)], axis=0)
        qs = _dot(qm, states.astype(BF16))
        v_c = v_ref[chunk(c), :].astype(BF16)
        for t in range(2):
            rows = slice(t * CHUNK, (t + 1) * CHUNK)
            vsl = slice(t * RET_DV, (t + 1) * RET_DV)
            o = (_dot(att[rows], v_c[:, vsl]) + qs[rows, :RET_DV] * qdec_f[t]
                 + qs[rows, RET_DV:] * qdec_b[t])
            o = (o * lax.rsqrt(jnp.mean(o * o, axis=-1, keepdims=True) + EPS)) * rn_ref[...]
            g = g_ref[chunk(c), vsl]
            o_ref[chunk(c), vsl] = (o * (g * jax.nn.sigmoid(g))).astype(o_ref.dtype)


def _retention(p_act, seq, dec, ret_norm, states=None, layer=0):
    n_tok = p_act.shape[0]
    n_seq = n_tok // seq
    has_state = states is not None
    pair_qk, pair_v = 2 * RET_DK, 2 * RET_DV
    rows = max(seq, RET_ROWS)
    blk_seqs = rows // seq
    in_specs = [
        pl.BlockSpec((rows, pair_qk), lambda s, p: (s, OFF_QR // pair_qk + p)),
        pl.BlockSpec((rows, pair_qk), lambda s, p: (s, OFF_KR // pair_qk + p)),
        pl.BlockSpec((rows, pair_v), lambda s, p: (s, OFF_VR // pair_v + p)),
        pl.BlockSpec((rows, pair_v), lambda s, p: (s, OFF_GR // pair_v + p)),
        pl.BlockSpec((8, 128), lambda s, p: (0, 0)),
        pl.BlockSpec((1, RET_DV), lambda s, p: (0, 0)),
    ]
    args = [p_act, p_act, p_act, p_act, dec, ret_norm]
    o_spec = pl.BlockSpec((rows, pair_v), lambda s, p: (s, p))
    o_shape = jax.ShapeDtypeStruct((n_tok, RET_V), BF16)
    if has_state:
        st_spec = pl.BlockSpec((blk_seqs, 1, 2, RET_DK, RET_DV), lambda s, p: (s, layer, p, 0, 0))
        in_specs += [st_spec, st_spec]
        args += list(states)
        out_specs, out_shape = o_spec, o_shape
    else:
        st_spec = pl.BlockSpec((blk_seqs, 2, RET_DK, RET_DV), lambda s, p: (s, p, 0, 0))
        st_shape = jax.ShapeDtypeStruct((n_seq, RET_HEADS, RET_DK, RET_DV), F32)
        out_specs, out_shape = [o_spec, st_spec, st_spec], [o_shape, st_shape, st_shape]
    return pl.pallas_call(
        functools.partial(_ret_kernel, seq=seq, has_state=has_state),
        grid=(n_tok // rows, RET_HEADS // 2),
        in_specs=in_specs,
        out_specs=out_specs,
        out_shape=out_shape,
        compiler_params=_cparams(("parallel", "parallel")),
        name="retention_lat" if has_state else "retention_ctx",
    )(*args)


FOURIER_ROWS = 1024


def _four_kernel(u0_ref, u1_ref, wc_ref, csn_ref, o_ref):
    gd = FOURIER_GROUP_DIM
    tc, ts = [], []
    for g in range(FOURIER_GROUPS):
        u_ref = (u0_ref, u1_ref)[g // 2]
        u_g = u_ref[:, (g % 2) * gd:(g % 2 + 1) * gd].astype(BF16)
        t = _dot(u_g, wc_ref[...])
        tc.append(t[:, :gd])
        ts.append(t[:, gd:])
    tc = jnp.concatenate(tc, axis=1).astype(BF16)
    ts = jnp.concatenate(ts, axis=1).astype(BF16)
    seq = csn_ref.shape[0]
    for sq in range(o_ref.shape[0] // seq):
        rows = slice(sq * seq, (sq + 1) * seq)
        t_sq = jnp.concatenate([tc[rows], ts[rows]], axis=0)
        o_ref[rows, :] = _dot(csn_ref[...], t_sq).astype(o_ref.dtype)


def _dft_tables(n):
    k = np.arange(n, dtype=np.int64)
    ang = 2.0 * np.pi * ((k[:, None] * k[None, :]) % n).astype(np.float64) / n
    scale = 1.0 / np.sqrt(n)
    return np.cos(ang) * scale, np.sin(ang) * scale


def _fourier(p_act, seq):
    n_tok = p_act.shape[0]
    cc, sc = _dft_tables(FOURIER_GROUP_DIM)
    wc = jnp.asarray(np.concatenate([cc, -sc], axis=1), F32).astype(BF16)
    cn, sn = _dft_tables(seq)
    csn = jnp.asarray(np.concatenate([cn, sn], axis=1), F32).astype(BF16)
    const = lambda shape: pl.BlockSpec(shape, lambda s: (0, 0))
    half = FOURIER_W // 2
    rows = max(seq, FOURIER_ROWS)
    return pl.pallas_call(
        _four_kernel,
        grid=(n_tok // rows,),
        in_specs=[
            pl.BlockSpec((rows, half), lambda s: (s, OFF_UF // half)),
            pl.BlockSpec((rows, half), lambda s: (s, OFF_UF // half + 1)),
            const((FOURIER_GROUP_DIM, 2 * FOURIER_GROUP_DIM)),
            const((seq, 2 * seq)),
        ],
        out_specs=pl.BlockSpec((rows, FOURIER_W), lambda s: (s, 0)),
        out_shape=jax.ShapeDtypeStruct((n_tok, FOURIER_W), BF16),
        compiler_params=_cparams(("parallel",)),
        name="fourier",
    )(p_act, p_act, wc, csn)


MIX_TM = 1024
MERGE_STEPS = 4
MERGE_TM = MIX_TM // MERGE_STEPS
MERGE_ROWS = 256
FF_TILE = 256
FF_STEPS = D_FF // FF_TILE
FF_ROWS = 256


def _mix_kernel(oa_ref, or_ref, of_ref, ga_ref, gr_ref, gf_ref, x_ref, g1_ref, sh_ref, sc_ref,
                g2_ref, n2_ref, wba_ref, wbr_ref, wbf_ref, wo_ref, wa_ref, wv_ref, cw_ref, cb_ref,
                wd_ref, o_ref, x1_scr, h_scr, acc_scr, *, seq):
    j = pl.program_id(1)

    @pl.when(j < MERGE_STEPS)
    def _():
        gate = lambda ref, rows: 0.5 * (jnp.tanh(0.5 * ref[rows, :].astype(F32)) + 1.0)
        for b in range(MERGE_TM // MERGE_ROWS):
            rows = slice(b * MERGE_ROWS, (b + 1) * MERGE_ROWS)
            dst = pl.ds(pl.multiple_of(j * MERGE_TM + b * MERGE_ROWS, MERGE_ROWS), MERGE_ROWS)
            merged = (gate(ga_ref, rows) * _dot(oa_ref[rows, :], wba_ref[...])
                      + gate(gr_ref, rows) * _dot(or_ref[rows, :], wbr_ref[...])
                      + gate(gf_ref, rows) * _dot(of_ref[rows, :], wbf_ref[...]))
            x1 = x_ref[rows, :] + g1_ref[0] * _dot(merged.astype(BF16), wo_ref[...])
            x1_scr[dst, :] = x1
            r = lax.rsqrt(jnp.mean(x1 * x1, axis=-1, keepdims=True) + EPS)
            h = (x1 * r) * n2_ref[...] * (1.0 + sc_ref[0]) + sh_ref[0]
            h_scr[dst, :] = h.astype(BF16)
            acc_scr[dst, :] = jnp.zeros((MERGE_ROWS, D_MODEL), F32)

    @pl.when(j >= MERGE_STEPS)
    def _():
        blocks = [slice(b * FF_ROWS, (b + 1) * FF_ROWS) for b in range(MIX_TM // FF_ROWS)]
        wa, wv, wd = (ref[0].astype(BF16) for ref in (wa_ref, wv_ref, wd_ref))
        a = jnp.concatenate([_dot(h_scr[rows, :], wa) for rows in blocks], axis=0)
        val = [_dot(h_scr[rows, :], wv) for rows in blocks]
        pos = lax.broadcasted_iota(jnp.int32, a.shape, 0) % seq
        prev = jnp.where(pos == 0, 0.0, pltpu.roll(a, 1, 0))
        nxt = jnp.where(pos == seq - 1, 0.0, pltpu.roll(a, MIX_TM - 1, 0))
        cw = cw_ref[...]
        ac = prev * cw[0:1] + a * cw[1:2] + nxt * cw[2:3] + cb_ref[...]
        for b, rows in enumerate(blocks):
            act = jax.nn.gelu(ac[rows]) * val[b]
            acc_scr[rows, :] += _dot(act.astype(BF16), wd)

    @pl.when(j == pl.num_programs(1) - 1)
    def _():
        o_ref[...] = x1_scr[...] + g2_ref[0] * acc_scr[...]


def _mix(o_att, o_ret, o_four, p_gate, x2d, seq, mod, rows_per_mod, w_att, w_ret, w_four, w_out,
         norm2, w_up, conv_w, conv_b, w_down, layer):
    n_tok = x2d.shape[0]
    mod_idx = lambda i: (i * MIX_TM) // rows_per_mod
    sub = lambda i, j: i * MERGE_STEPS + jnp.minimum(j, MERGE_STEPS - 1)
    ff = lambda j: jnp.maximum(j - MERGE_STEPS, 0)
    br = lambda: pl.BlockSpec((MERGE_TM, 512), lambda i, j: (sub(i, j), 0))
    gate = lambda k: pl.BlockSpec((MERGE_TM, D_MODEL), lambda i, j: (sub(i, j), k))
    modv = lambda k: pl.BlockSpec((1, 1, D_MODEL), lambda i, j: (mod_idx(i), 0, k))
    const = lambda shape: pl.BlockSpec(shape, lambda i, j: (0, 0))
    return pl.pallas_call(
        functools.partial(_mix_kernel, seq=seq),
        grid=(n_tok // MIX_TM, MERGE_STEPS + FF_STEPS),
        in_specs=[
            br(), br(), br(), gate(0), gate(1), gate(2),
            pl.BlockSpec((MERGE_TM, D_MODEL), lambda i, j: (sub(i, j), 0)),
            modv(2), modv(3), modv(4), modv(5),
            const((1, D_MODEL)),
            const((512, D_MODEL)), const((512, D_MODEL)), const((512, D_MODEL)),
            const((D_MODEL, D_MODEL)),
            pl.BlockSpec((1, D_MODEL, FF_TILE), lambda i, j: (layer, 0, ff(j))),
            pl.BlockSpec((1, D_MODEL, FF_TILE), lambda i, j: (layer, 0, FF_STEPS + ff(j))),
            pl.BlockSpec((3, FF_TILE), lambda i, j: (0, ff(j))),
            pl.BlockSpec((1, FF_TILE), lambda i, j: (0, ff(j))),
            pl.BlockSpec((1, FF_TILE, D_MODEL), lambda i, j: (layer, ff(j), 0)),
        ],
        out_specs=pl.BlockSpec((MIX_TM, D_MODEL), lambda i, j: (i, 0)),
        out_shape=jax.ShapeDtypeStruct((n_tok, D_MODEL), F32),
        scratch_shapes=[pltpu.VMEM((MIX_TM, D_MODEL), F32), pltpu.VMEM((MIX_TM, D_MODEL), BF16),
                        pltpu.VMEM((MIX_TM, D_MODEL), F32)],
        compiler_params=_cparams(("parallel", "arbitrary")),
        name="mix",
    )(o_att, o_ret, o_four, p_gate, p_gate, p_gate, x2d, mod, mod, mod, mod, norm2,
      w_att, w_ret, w_four, w_out, w_up, w_up, conv_w, conv_b, w_down)


def _pad_w_in(w):
    pad = jnp.zeros((w.shape[0], OFF_GATE - W_IN_SPLIT), BF16)
    return jnp.concatenate([w[:, :W_IN_SPLIT].astype(BF16), pad, w[:, W_IN_SPLIT:].astype(BF16)], axis=1)


def _rope_tables(n_tok):
    rows = n_tok // GRID_W
    row_id = jnp.repeat(jnp.arange(rows), GRID_W).astype(F32)
    col_id = jnp.tile(jnp.arange(GRID_W), rows).astype(F32)
    n_freq = HEAD_DIM // 4
    inv = ROPE_THETA ** (-jnp.arange(n_freq, dtype=F32) / n_freq)
    ang = jnp.concatenate([row_id[None, :] * inv[:, None], col_id[None, :] * inv[:, None]], axis=0)
    return jnp.cos(ang), jnp.sin(ang)


def kernel(x_prompt, x_sample, cache_k, cache_v, state_ret_fwd, state_ret_bwd, c, c_ctx, w_ada, b_ada, norm1, w_in, q_norm, k_norm, ret_decay_f, ret_decay_b, ret_norm, w_br_att, w_br_ret, w_br_four, w_out, norm2, w_up, conv_w, conv_b, w_down):
    batch, seq, _ = x_prompt.shape
    dec_batch, dec_seq, _ = x_sample.shape
    past = cache_k.shape[2]

    cond_all = jnp.concatenate([c_ctx[None, :], c], axis=0)
    mod_all = _ada(cond_all, w_ada, b_ada)
    cos_t, sin_t = _rope_tables(dec_seq)
    ck = cache_k.reshape(dec_batch, DEPTH, past, ATT_KV)
    cv = cache_v.reshape(dec_batch, DEPTH, past, ATT_KV)

    xp = x_prompt.reshape(batch * seq, D_MODEL)
    xs = x_sample.reshape(dec_batch * dec_seq, D_MODEL)
    new_k, new_v, new_sf, new_sb = [], [], [], []
    for l in range(DEPTH):
        w_in_p = _pad_w_in(w_in[l])
        w_att, w_ret, w_four = (w_br_att[l].astype(BF16), w_br_ret[l].astype(BF16),
                                w_br_four[l].astype(BF16))
        w_o = w_out[l].astype(BF16)
        n1, n2 = norm1[l][None, :], norm2[l][None, :]
        qn, kn = q_norm[l][:, None], k_norm[l][:, None]
        rn = ret_norm[l][None, :]
        dec = jnp.broadcast_to(
            jnp.concatenate([ret_decay_f[l], ret_decay_b[l]])[:, None].astype(F32), (2 * RET_HEADS, 128))
        cw, cb = conv_w[l], conv_b[l][None, :]
        mod_ctx = mod_all[l, 0:1].reshape(1, 1, 6 * D_MODEL)
        mod_lat = mod_all[l, 1:].reshape(dec_batch, 1, 6 * D_MODEL)

        p_act, p_gate = _in_proj(xp, mod_ctx, batch * seq, n1, w_in_p)
        o_att, k_l = _att_ctx(p_act, seq, qn, kn)
        o_ret, sf_l, sb_l = _retention(p_act, seq, dec, rn)
        o_four = _fourier(p_act, seq)
        xp = _mix(o_att, o_ret, o_four, p_gate, xp, seq, mod_ctx, batch * seq,
                  w_att, w_ret, w_four, w_o, n2, w_up, cw, cb, w_down, l)
        new_k.append(k_l.reshape(batch, seq, N_KV_HEADS, HEAD_DIM))
        new_v.append(p_act[:, OFF_VA:OFF_VA + ATT_KV].reshape(batch, seq, N_KV_HEADS, HEAD_DIM))
        new_sf.append(sf_l)
        new_sb.append(sb_l)

        p_act, p_gate = _in_proj(xs, mod_lat, dec_seq, n1, w_in_p)
        o_att = _att_lat(p_act, dec_seq, ck, cv, l, qn, kn, cos_t, sin_t)
        o_ret = _retention(p_act, dec_seq, dec, rn, states=(state_ret_fwd, state_ret_bwd), layer=l)
        o_four = _fourier(p_act, dec_seq)
        xs = _mix(o_att, o_ret, o_four, p_gate, xs, dec_seq, mod_lat, dec_seq,
                  w_att, w_ret, w_four, w_o, n2, w_up, cw, cb, w_down, l)

    return (xp.reshape(batch, seq, D_MODEL), xs.reshape(dec_batch, dec_seq, D_MODEL),
            jnp.stack(new_k, axis=1), jnp.stack(new_v, axis=1),
            jnp.stack(new_sf, axis=1), jnp.stack(new_sb, axis=1))
```

```python
import functools

import numpy as np
import jax
import jax.numpy as jnp
from jax import lax
from jax.experimental import pallas as pl
from jax.experimental.pallas import tpu as pltpu

D_MODEL = 1024
DEPTH = 2
GRID_W = 64
HEAD_DIM = 64
N_HEADS = 8
N_KV_HEADS = 2
Q_PER_KV = N_HEADS // N_KV_HEADS
ATT_Q = N_HEADS * HEAD_DIM
ATT_KV = N_KV_HEADS * HEAD_DIM
RET_HEADS = 4
RET_DK = 64
RET_DV = 128
RET_QK = RET_HEADS * RET_DK
RET_V = RET_HEADS * RET_DV
FOURIER_GROUPS = 4
FOURIER_GROUP_DIM = 128
FOURIER_W = FOURIER_GROUPS * FOURIER_GROUP_DIM
D_FF = 2816
CHUNK = 128
Q_BLOCK = 256
ROPE_THETA = 10000.0
EPS = 1e-6
LOG2_E = 1.4426950408889634

F32 = jnp.float32
BF16 = jnp.bfloat16

OFF_QA, OFF_KA, OFF_VA = 0, 512, 640
OFF_QR, OFF_KR, OFF_VR, OFF_GR, OFF_UF = 768, 1024, 1280, 1792, 2304
W_IN_SPLIT = 2816
OFF_GATE = 3072
P_W = OFF_GATE + 3 * D_MODEL

VMEM_LIMIT = 56 * 1024 * 1024


def _cparams(sem, flags=None):
    return pltpu.CompilerParams(dimension_semantics=sem, vmem_limit_bytes=VMEM_LIMIT, flags=flags)


def _dot(a, b):
    return jnp.dot(a, b, preferred_element_type=F32)


def _ada_kernel(cond_ref, w_ref, b_ref, o_ref):
    cnd = cond_ref[...]
    s = cnd * jax.nn.sigmoid(cnd)
    o_ref[0] = _dot(s.astype(BF16), w_ref[0].astype(BF16)) + b_ref[0]


def _ada(cond_all, w_ada, b_ada):
    n = cond_all.shape[0]
    tn = 1024
    return pl.pallas_call(
        _ada_kernel,
        grid=(DEPTH, 6 * D_MODEL // tn),
        in_specs=[
            pl.BlockSpec((n, D_MODEL), lambda l, j: (0, 0)),
            pl.BlockSpec((1, D_MODEL, tn), lambda l, j: (l, 0, j)),
            pl.BlockSpec((1, 1, tn), lambda l, j: (l, 0, j)),
        ],
        out_specs=pl.BlockSpec((1, n, tn), lambda l, j: (l, 0, j)),
        out_shape=jax.ShapeDtypeStruct((DEPTH, n, 6 * D_MODEL), F32),
        compiler_params=_cparams(("parallel", "parallel")),
        name="ada",
    )(cond_all, w_ada, b_ada.reshape(DEPTH, 1, 6 * D_MODEL))


IN_TN = 1024
IN_ROWS = 256
MAIN_STEPS = OFF_GATE // IN_TN


def _in_kernel(x_ref, sh_ref, sc_ref, g_ref, wm_ref, wg_ref, om_ref, og_ref, h_scr):
    j = pl.program_id(1)

    @pl.when(j == 0)
    def _():
        for b in range(x_ref.shape[0] // IN_ROWS):
            rows = slice(b * IN_ROWS, (b + 1) * IN_ROWS)
            x = x_ref[rows, :]
            r = lax.rsqrt(jnp.mean(x * x, axis=-1, keepdims=True) + EPS)
            h = ((x * r) * g_ref[...] * (1.0 + sc_ref[0]) + sh_ref[0]).astype(BF16)
            h_scr[rows, :] = h
            om_ref[rows, :] = _dot(h, wm_ref[0])

    @pl.when(jnp.logical_and(j > 0, j < MAIN_STEPS))
    def _():
        om_ref[...] = _dot(h_scr[...], wm_ref[0])

    @pl.when(j >= MAIN_STEPS)
    def _():
        og_ref[...] = _dot(h_scr[...], wg_ref[0]).astype(og_ref.dtype)


def _in_proj(x2d, mod, rows_per_mod, norm1, w_in_bf, layer):
    n_tok = x2d.shape[0]
    tm, tn = 1024, IN_TN
    mod_idx = lambda i: (i * tm) // rows_per_mod
    gate_col = lambda j: pl.multiple_of(
        W_IN_SPLIT + tn * jnp.maximum(j - MAIN_STEPS, 0), 128)
    return pl.pallas_call(
        _in_kernel,
        grid=(n_tok // tm, P_W // tn),
        in_specs=[
            pl.BlockSpec((tm, D_MODEL), lambda i, j: (i, 0)),
            pl.BlockSpec((1, 1, D_MODEL), lambda i, j: (mod_idx(i), 0, 0)),
            pl.BlockSpec((1, 1, D_MODEL), lambda i, j: (mod_idx(i), 0, 1)),
            pl.BlockSpec((1, D_MODEL), lambda i, j: (0, 0)),
            pl.BlockSpec((1, D_MODEL, tn), lambda i, j: (layer, 0, jnp.minimum(j, MAIN_STEPS - 1))),
            pl.BlockSpec((pl.Element(1), pl.Element(D_MODEL), pl.Element(tn)),
                         lambda i, j: (layer, 0, gate_col(j))),
        ],
        out_specs=[
            pl.BlockSpec((tm, tn), lambda i, j: (i, jnp.minimum(j, MAIN_STEPS - 1))),
            pl.BlockSpec((tm, tn), lambda i, j: (i, jnp.maximum(j - MAIN_STEPS, 0))),
        ],
        out_shape=[jax.ShapeDtypeStruct((n_tok, OFF_GATE), F32),
                   jax.ShapeDtypeStruct((n_tok, P_W - OFF_GATE), BF16)],
        scratch_shapes=[pltpu.VMEM((tm, D_MODEL), BF16)],
        compiler_params=_cparams(("parallel", "arbitrary")),
        name="in_proj",
    )(x2d, mod, mod, norm1, w_in_bf, w_in_bf)


def _norm_rope_heads_t(x_t, n_heads, g_col, cos_t=None, sin_t=None):
    quarter = HEAD_DIM // 4
    outs = []
    for h in range(n_heads):
        x = x_t[h * HEAD_DIM:(h + 1) * HEAD_DIM, :]
        r = lax.rsqrt(jnp.mean(x * x, axis=0, keepdims=True) + EPS)
        y = (x * r) * g_col
        if cos_t is not None:
            pieces = []
            for a in range(2):
                c = cos_t[a * quarter:(a + 1) * quarter]
                s = sin_t[a * quarter:(a + 1) * quarter]
                x1 = y[2 * a * quarter:(2 * a + 1) * quarter]
                x2 = y[(2 * a + 1) * quarter:(2 * a + 2) * quarter]
                pieces += [x1 * c - x2 * s, x2 * c + x1 * s]
            y = jnp.concatenate(pieces, axis=0)
        outs.append(y)
    return outs


def _attend_t(q_heads_t, k_bf, v_t_bf):
    tq = q_heads_t[0].shape[1]
    cols = Q_PER_KV * tq
    zeros = jnp.zeros((HEAD_DIM, cols), F32)
    q_kv = [jnp.concatenate(q_heads_t[kv * Q_PER_KV:(kv + 1) * Q_PER_KV], axis=1)
            * (HEAD_DIM ** -0.5 * LOG2_E) for kv in range(N_KV_HEADS)]
    rhs = jnp.concatenate([jnp.concatenate([q_kv[0], zeros], axis=1),
                           jnp.concatenate([zeros, q_kv[1]], axis=1)], axis=0).astype(BF16)
    s_all = _dot(k_bf, rhs)
    out_rows = []
    for kv in range(N_KV_HEADS):
        s_t = s_all[:, kv * cols:(kv + 1) * cols]
        e = jnp.exp2(s_t - jnp.max(s_t, axis=0, keepdims=True))
        inv = 1.0 / jnp.sum(e, axis=0, keepdims=True)
        o_t = _dot(v_t_bf[kv * HEAD_DIM:(kv + 1) * HEAD_DIM, :], e.astype(BF16)) * inv
        out_rows += [o_t[:, g * tq:(g + 1) * tq] for g in range(Q_PER_KV)]
    return jnp.concatenate(out_rows, axis=0)


def _att_ctx_kernel(q_ref, k_ref, v_ref, qn_ref, kn_ref, ck_any, cv_any, o_ref, ko_ref, vo_ref):
    del ck_any, cv_any
    k_heads = _norm_rope_heads_t(k_ref[...].T, N_KV_HEADS, kn_ref[...])
    k_n = jnp.concatenate(k_heads, axis=0).T
    ko_ref[0, 0] = k_n
    v = v_ref[...]
    vo_ref[0, 0] = v
    q_heads = _norm_rope_heads_t(q_ref[...].T, N_HEADS, qn_ref[...])
    o_t = _attend_t(q_heads, k_n.astype(BF16), v.T.astype(BF16))
    o_ref[...] = o_t.T.astype(o_ref.dtype)


def _att_ctx(p_act, seq, q_norm, k_norm, layer, caches):
    n_tok = p_act.shape[0]
    batch = n_tok // seq
    cache_spec = pl.BlockSpec((1, 1, seq, ATT_KV), lambda b: (b, layer, 0, 0))
    cache_shape = jax.ShapeDtypeStruct((batch, DEPTH, seq, ATT_KV), F32)
    return pl.pallas_call(
        _att_ctx_kernel,
        grid=(batch,),
        in_specs=[
            pl.BlockSpec((seq, ATT_Q), lambda b: (b, OFF_QA // ATT_Q)),
            pl.BlockSpec((seq, ATT_KV), lambda b: (b, OFF_KA // ATT_KV)),
            pl.BlockSpec((seq, ATT_KV), lambda b: (b, OFF_VA // ATT_KV)),
            pl.BlockSpec((HEAD_DIM, 1), lambda b: (0, 0)),
            pl.BlockSpec((HEAD_DIM, 1), lambda b: (0, 0)),
            pl.BlockSpec(memory_space=pl.ANY),
            pl.BlockSpec(memory_space=pl.ANY),
        ],
        out_specs=[pl.BlockSpec((seq, ATT_Q), lambda b: (b, 0)), cache_spec, cache_spec],
        out_shape=[jax.ShapeDtypeStruct((n_tok, ATT_Q), BF16), cache_shape, cache_shape],
        input_output_aliases={5: 1, 6: 2},
        compiler_params=_cparams(("parallel",)),
        name="att_ctx",
    )(p_act, p_act, p_act, q_norm, k_norm, *caches)


def _att_lat_kernel(q_ref, k_ref, v_ref, ck_ref, cv_ref, qn_ref, kn_ref,
                    cq_ref, sq_ref, ck_t_ref, sk_t_ref, o_ref, kf_scr, vt_scr, *, seq):
    @pl.when(pl.program_id(1) == 0)
    def _():
        k_heads = _norm_rope_heads_t(k_ref[...].T, N_KV_HEADS, kn_ref[...], ck_t_ref[...], sk_t_ref[...])
        kf_scr[0:seq, :] = jnp.concatenate(k_heads, axis=0).T.astype(BF16)
        kf_scr[seq:, :] = ck_ref[0, 0].astype(BF16)
        vt_scr[:, 0:seq] = v_ref[...].T.astype(BF16)
        vt_scr[:, seq:] = cv_ref[0, 0].T.astype(BF16)

    q_heads = _norm_rope_heads_t(q_ref[...].T, N_HEADS, qn_ref[...], cq_ref[...], sq_ref[...])
    o_ref[...] = _attend_t(q_heads, kf_scr[...], vt_scr[...]).T.astype(o_ref.dtype)


def _att_lat(p_act, seq, cache_k, cache_v, layer, q_norm, k_norm, cos_t, sin_t):
    n_tok = p_act.shape[0]
    nb = seq // Q_BLOCK
    past = cache_k.shape[2]
    return pl.pallas_call(
        functools.partial(_att_lat_kernel, seq=seq),
        grid=(n_tok // seq, nb),
        in_specs=[
            pl.BlockSpec((Q_BLOCK, ATT_Q), lambda b, i: (b * nb + i, OFF_QA // ATT_Q)),
            pl.BlockSpec((seq, ATT_KV), lambda b, i: (b, OFF_KA // ATT_KV)),
            pl.BlockSpec((seq, ATT_KV), lambda b, i: (b, OFF_VA // ATT_KV)),
            pl.BlockSpec((1, 1, past, ATT_KV), lambda b, i: (b, layer, 0, 0)),
            pl.BlockSpec((1, 1, past, ATT_KV), lambda b, i: (b, layer, 0, 0)),
            pl.BlockSpec((HEAD_DIM, 1), lambda b, i: (0, 0)),
            pl.BlockSpec((HEAD_DIM, 1), lambda b, i: (0, 0)),
            pl.BlockSpec((HEAD_DIM // 2, Q_BLOCK), lambda b, i: (0, i)),
            pl.BlockSpec((HEAD_DIM // 2, Q_BLOCK), lambda b, i: (0, i)),
            pl.BlockSpec((HEAD_DIM // 2, seq), lambda b, i: (0, 0)),
            pl.BlockSpec((HEAD_DIM // 2, seq), lambda b, i: (0, 0)),
        ],
        out_specs=pl.BlockSpec((Q_BLOCK, ATT_Q), lambda b, i: (b * nb + i, 0)),
        out_shape=jax.ShapeDtypeStruct((n_tok, ATT_Q), BF16),
        scratch_shapes=[pltpu.VMEM((seq + past, ATT_KV), BF16),
                        pltpu.VMEM((ATT_KV, seq + past), BF16)],
        compiler_params=_cparams(("parallel", "arbitrary")),
        name="att_lat",
    )(p_act, p_act, p_act, cache_k, cache_v, q_norm, k_norm, cos_t, sin_t, cos_t, sin_t)


RET_ROWS = 1024


def _log_sigmoid(d):
    return jnp.minimum(d, 0.0) - jnp.log1p(jnp.exp(-jnp.abs(d)))


def _ret_kernel(q_ref, k_ref, v_ref, g_ref, dec_ref, rn_ref, *rest, seq, has_state):
    if has_state:
        s0f_ref, s0b_ref, o_ref = rest
    else:
        o_ref, sf_ref, sb_ref = rest[-3:]
    hp = pl.program_id(1)
    n_chunks = q_ref.shape[0] // CHUNK
    ii = lax.broadcasted_iota(jnp.int32, (CHUNK, CHUNK), 0)
    jj = lax.broadcasted_iota(jnp.int32, (CHUNK, CHUNK), 1)
    rel = (ii - jj).astype(F32)
    row = ii.astype(F32)
    lane = jj.astype(F32)
    lgf = [_log_sigmoid(dec_ref[pl.ds(2 * hp + t, 1), :]) for t in range(2)]
    lgb = [_log_sigmoid(dec_ref[pl.ds(RET_HEADS + 2 * hp + t, 1), :]) for t in range(2)]

    mask2 = jnp.concatenate(
        [jnp.where(rel > 0, jnp.exp(jnp.maximum(rel, 0.0) * lgf[t]),
                   jnp.where(rel < 0, jnp.exp(jnp.maximum(-rel, 0.0) * lgb[t]), 2.0))
         for t in range(2)], axis=0)
    qdec_f = [jnp.exp((row + 1.0) * lgf[t]) for t in range(2)]
    qdec_b = [jnp.exp((CHUNK - row) * lgb[t]) for t in range(2)]
    cdec_f = [jnp.exp(CHUNK * lgf[t]) for t in range(2)]
    cdec_b = [jnp.exp(CHUNK * lgb[t]) for t in range(2)]
    kdec_f = jnp.exp((CHUNK - 1.0 - lane) * jnp.where(ii < RET_DK, lgf[0], lgf[1]))
    kdec_b = jnp.exp(lane * jnp.where(ii < RET_DK, lgb[0], lgb[1]))

    k_t = (k_ref[...] * (RET_DK ** -0.5)).T
    first_head = jj < RET_DK

    def chunk(c):
        return slice(c * CHUNK, (c + 1) * CHUNK)

    kv_f = [[None] * n_chunks for _ in range(2)]
    kv_b = [[None] * n_chunks for _ in range(2)]
    for c in range(n_chunks):
        k_c = k_t[:, chunk(c)]
        kd = jnp.concatenate([k_c * kdec_f, k_c * kdec_b], axis=0).astype(BF16)
        kv = _dot(kd, v_ref[chunk(c), :].astype(BF16))
        for t in range(2):
            kv_f[t][c] = kv[t * RET_DK:(t + 1) * RET_DK, t * RET_DV:(t + 1) * RET_DV]
            kv_b[t][c] = kv[CHUNK + t * RET_DK:CHUNK + (t + 1) * RET_DK, t * RET_DV:(t + 1) * RET_DV]

    st_f = [[None] * n_chunks for _ in range(2)]
    st_b = [[None] * n_chunks for _ in range(2)]
    per_seq = seq // CHUNK
    for t in range(2):
        for sq in range(n_chunks // per_seq):
            own = range(sq * per_seq, (sq + 1) * per_seq)
            if has_state:
                s_f = s0f_ref[sq, 0, t]
                s_b = s0b_ref[sq, 0, t]
            else:
                s_f = jnp.zeros((RET_DK, RET_DV), F32)
                s_b = s_f
            for c in own:
                st_f[t][c] = s_f
                s_f = s_f * cdec_f[t] + kv_f[t][c]
            for c in reversed(own):
                st_b[t][c] = s_b
                s_b = s_b * cdec_b[t] + kv_b[t][c]
            if not has_state:
                sf_ref[sq, 0, t] = s_f
                sb_ref[sq, 0, t] = s_b

    for c in range(n_chunks):
        q_c = q_ref[chunk(c), :]
        qm = jnp.concatenate([jnp.where(first_head, q_c, 0.0), jnp.where(first_head, 0.0, q_c)],
                             axis=0).astype(BF16)
        att = (_dot(qm, k_t[:, chunk(c)].astype(BF16)) * mask2).astype(BF16)
        states = jnp.concatenate(
            [jnp.concatenate([st_f[t][c], st_b[t][c]], axis=1) for t in range(2)], axis=0)
        qs = _dot(qm, states.astype(BF16))
        v_c = v_ref[chunk(c), :].astype(BF16)
        for t in range(2):
            rows = slice(t * CHUNK, (t + 1) * CHUNK)
            vsl = slice(t * RET_DV, (t + 1) * RET_DV)
            o = (_dot(att[rows], v_c[:, vsl]) + qs[rows, :RET_DV] * qdec_f[t]
                 + qs[rows, RET_DV:] * qdec_b[t])
            o = (o * lax.rsqrt(jnp.mean(o * o, axis=-1, keepdims=True) + EPS)) * rn_ref[...]
            g = g_ref[chunk(c), vsl]
            o_ref[chunk(c), vsl] = (o * (g * jax.nn.sigmoid(g))).astype(o_ref.dtype)


def _retention(p_act, seq, dec, ret_norm, layer, states=None, new_states=None):
    n_tok = p_act.shape[0]
    n_seq = n_tok // seq
    has_state = states is not None
    pair_qk, pair_v = 2 * RET_DK, 2 * RET_DV
    rows = max(seq, RET_ROWS)
    blk_seqs = rows // seq
    in_specs = [
        pl.BlockSpec((rows, pair_qk), lambda s, p: (s, OFF_QR // pair_qk + p)),
        pl.BlockSpec((rows, pair_qk), lambda s, p: (s, OFF_KR // pair_qk + p)),
        pl.BlockSpec((rows, pair_v), lambda s, p: (s, OFF_VR // pair_v + p)),
        pl.BlockSpec((rows, pair_v), lambda s, p: (s, OFF_GR // pair_v + p)),
        pl.BlockSpec((8, 128), lambda s, p: (0, 0)),
        pl.BlockSpec((1, RET_DV), lambda s, p: (0, 0)),
    ]
    args = [p_act, p_act, p_act, p_act, dec, ret_norm]
    o_spec = pl.BlockSpec((rows, pair_v), lambda s, p: (s, p))
    o_shape = jax.ShapeDtypeStruct((n_tok, RET_V), BF16)
    st_spec = pl.BlockSpec((blk_seqs, 1, 2, RET_DK, RET_DV), lambda s, p: (s, layer, p, 0, 0))
    aliases = {}
    if has_state:
        in_specs += [st_spec, st_spec]
        args += list(states)
        out_specs, out_shape = o_spec, o_shape
    else:
        in_specs += [pl.BlockSpec(memory_space=pl.ANY)] * 2
        aliases = {len(args): 1, len(args) + 1: 2}
        args += list(new_states)
        st_shape = jax.ShapeDtypeStruct((n_seq, DEPTH, RET_HEADS, RET_DK, RET_DV), F32)
        out_specs, out_shape = [o_spec, st_spec, st_spec], [o_shape, st_shape, st_shape]
    return pl.pallas_call(
        functools.partial(_ret_kernel, seq=seq, has_state=has_state),
        grid=(n_tok // rows, RET_HEADS // 2),
        in_specs=in_specs,
        out_specs=out_specs,
        out_shape=out_shape,
        input_output_aliases=aliases,
        compiler_params=_cparams(("parallel", "parallel")),
        name="retention_lat" if has_state else "retention_ctx",
    )(*args)


FOURIER_ROWS = 1024


def _four_kernel(u0_ref, u1_ref, wc_ref, csn_ref, o_ref):
    gd = FOURIER_GROUP_DIM
    tc, ts = [], []
    for g in range(FOURIER_GROUPS):
        u_ref = (u0_ref, u1_ref)[g // 2]
        u_g = u_ref[:, (g % 2) * gd:(g % 2 + 1) * gd].astype(BF16)
        t = _dot(u_g, wc_ref[...])
        tc.append(t[:, :gd])
        ts.append(t[:, gd:])
    tc = jnp.concatenate(tc, axis=1).astype(BF16)
    ts = jnp.concatenate(ts, axis=1).astype(BF16)
    seq = csn_ref.shape[0]
    for sq in range(o_ref.shape[0] // seq):
        rows = slice(sq * seq, (sq + 1) * seq)
        t_sq = jnp.concatenate([tc[rows], ts[rows]], axis=0)
        o_ref[rows, :] = _dot(csn_ref[...], t_sq).astype(o_ref.dtype)


def _dft_tables(n):
    k = np.arange(n, dtype=np.int64)
    ang = 2.0 * np.pi * ((k[:, None] * k[None, :]) % n).astype(np.float64) / n
    scale = 1.0 / np.sqrt(n)
    return np.cos(ang) * scale, np.sin(ang) * scale


def _fourier(p_act, seq):
    n_tok = p_act.shape[0]
    cc, sc = _dft_tables(FOURIER_GROUP_DIM)
    wc = jnp.asarray(np.concatenate([cc, -sc], axis=1), F32).astype(BF16)
    cn, sn = _dft_tables(seq)
    csn = jnp.asarray(np.concatenate([cn, sn], axis=1), F32).astype(BF16)
    const = lambda shape: pl.BlockSpec(shape, lambda s: (0, 0))
    half = FOURIER_W // 2
    rows = max(seq, FOURIER_ROWS)
    return pl.pallas_call(
        _four_kernel,
        grid=(n_tok // rows,),
        in_specs=[
            pl.BlockSpec((rows, half), lambda s: (s, OFF_UF // half)),
            pl.BlockSpec((rows, half), lambda s: (s, OFF_UF // half + 1)),
            const((FOURIER_GROUP_DIM, 2 * FOURIER_GROUP_DIM)),
            const((seq, 2 * seq)),
        ],
        out_specs=pl.BlockSpec((rows, FOURIER_W), lambda s: (s, 0)),
        out_shape=jax.ShapeDtypeStruct((n_tok, FOURIER_W), BF16),
        compiler_params=_cparams(("parallel",)),
        name="fourier",
    )(p_act, p_act, wc, csn)


MIX_TM = 1024
MERGE_STEPS = 4
MERGE_TM = MIX_TM // MERGE_STEPS
MERGE_ROWS = 256
FF_TILE = 256
FF_STEPS = D_FF // FF_TILE
FF_ROWS = 256


def _mix_kernel(oa_ref, or_ref, of_ref, ga_ref, gr_ref, gf_ref, x_ref, g1_ref, sh_ref, sc_ref,
                g2_ref, n2_ref, wba_ref, wbr_ref, wbf_ref, wo_ref, wa_ref, wv_ref, cw_ref, cb_ref,
                wd_ref, o_ref, x1_scr, h_scr, acc_scr, *, seq):
    j = pl.program_id(1)

    @pl.when(j < MERGE_STEPS)
    def _():
        gate = lambda ref, rows: 0.5 * (jnp.tanh(0.5 * ref[rows, :].astype(F32)) + 1.0)
        for b in range(MERGE_TM // MERGE_ROWS):
            rows = slice(b * MERGE_ROWS, (b + 1) * MERGE_ROWS)
            dst = pl.ds(pl.multiple_of(j * MERGE_TM + b * MERGE_ROWS, MERGE_ROWS), MERGE_ROWS)
            merged = (gate(ga_ref, rows) * _dot(oa_ref[rows, :], wba_ref[...])
                      + gate(gr_ref, rows) * _dot(or_ref[rows, :], wbr_ref[...])
                      + gate(gf_ref, rows) * _dot(of_ref[rows, :], wbf_ref[...]))
            x1 = x_ref[rows, :] + g1_ref[0] * _dot(merged.astype(BF16), wo_ref[...])
            x1_scr[dst, :] = x1
            r = lax.rsqrt(jnp.mean(x1 * x1, axis=-1, keepdims=True) + EPS)
            h = (x1 * r) * n2_ref[...] * (1.0 + sc_ref[0]) + sh_ref[0]
            h_scr[dst, :] = h.astype(BF16)
            acc_scr[dst, :] = jnp.zeros((MERGE_ROWS, D_MODEL), F32)

    @pl.when(j >= MERGE_STEPS)
    def _():
        blocks = [slice(b * FF_ROWS, (b + 1) * FF_ROWS) for b in range(MIX_TM // FF_ROWS)]
        wa, wv, wd = (ref[0].astype(BF16) for ref in (wa_ref, wv_ref, wd_ref))
        a = jnp.concatenate([_dot(h_scr[rows, :], wa) for rows in blocks], axis=0)
        val = [_dot(h_scr[rows, :], wv) for rows in blocks]
        pos = lax.broadcasted_iota(jnp.int32, a.shape, 0) % seq
        prev = jnp.where(pos == 0, 0.0, pltpu.roll(a, 1, 0))
        nxt = jnp.where(pos == seq - 1, 0.0, pltpu.roll(a, MIX_TM - 1, 0))
        cw = cw_ref[...]
        ac = prev * cw[0:1] + a * cw[1:2] + nxt * cw[2:3] + cb_ref[...]
        for b, rows in enumerate(blocks):
            act = jax.nn.gelu(ac[rows]) * val[b]
            acc_scr[rows, :] += _dot(act.astype(BF16), wd)

    @pl.when(j == pl.num_programs(1) - 1)
    def _():
        o_ref[...] = x1_scr[...] + g2_ref[0] * acc_scr[...]


def _mix(o_att, o_ret, o_four, p_gate, x2d, seq, mod, rows_per_mod, w_att, w_ret, w_four, w_out,
         norm2, w_up, conv_w, conv_b, w_down, layer):
    n_tok = x2d.shape[0]
    mod_idx = lambda i: (i * MIX_TM) // rows_per_mod
    sub = lambda i, j: i * MERGE_STEPS + jnp.minimum(j, MERGE_STEPS - 1)
    ff = lambda j: jnp.maximum(j - MERGE_STEPS, 0)
    br = lambda: pl.BlockSpec((MERGE_TM, 512), lambda i, j: (sub(i, j), 0))
    gate = lambda k: pl.BlockSpec((MERGE_TM, D_MODEL), lambda i, j: (sub(i, j), k))
    modv = lambda k: pl.BlockSpec((1, 1, D_MODEL), lambda i, j: (mod_idx(i), 0, k))
    const = lambda shape: pl.BlockSpec(shape, lambda i, j: (0, 0))
    return pl.pallas_call(
        functools.partial(_mix_kernel, seq=seq),
        grid=(n_tok // MIX_TM, MERGE_STEPS + FF_STEPS),
        in_specs=[
            br(), br(), br(), gate(0), gate(1), gate(2),
            pl.BlockSpec((MERGE_TM, D_MODEL), lambda i, j: (sub(i, j), 0)),
            modv(2), modv(3), modv(4), modv(5),
            const((1, D_MODEL)),
            const((512, D_MODEL)), const((512, D_MODEL)), const((512, D_MODEL)),
            const((D_MODEL, D_MODEL)),
            pl.BlockSpec((1, D_MODEL, FF_TILE), lambda i, j: (layer, 0, ff(j))),
            pl.BlockSpec((1, D_MODEL, FF_TILE), lambda i, j: (layer, 0, FF_STEPS + ff(j))),
            pl.BlockSpec((3, FF_TILE), lambda i, j: (0, ff(j))),
            pl.BlockSpec((1, FF_TILE), lambda i, j: (0, ff(j))),
            pl.BlockSpec((1, FF_TILE, D_MODEL), lambda i, j: (layer, ff(j), 0)),
        ],
        out_specs=pl.BlockSpec((MIX_TM, D_MODEL), lambda i, j: (i, 0)),
        out_shape=jax.ShapeDtypeStruct((n_tok, D_MODEL), F32),
        scratch_shapes=[pltpu.VMEM((MIX_TM, D_MODEL), F32), pltpu.VMEM((MIX_TM, D_MODEL), BF16),
                        pltpu.VMEM((MIX_TM, D_MODEL), F32)],
        compiler_params=_cparams(("parallel", "arbitrary")),
        name="mix",
    )(o_att, o_ret, o_four, p_gate, p_gate, p_gate, x2d, mod, mod, mod, mod, norm2,
      w_att, w_ret, w_four, w_out, w_up, w_up, conv_w, conv_b, w_down)


def _rope_tables(n_tok):
    rows = n_tok // GRID_W
    row_id = jnp.repeat(jnp.arange(rows), GRID_W).astype(F32)
    col_id = jnp.tile(jnp.arange(GRID_W), rows).astype(F32)
    n_freq = HEAD_DIM // 4
    inv = ROPE_THETA ** (-jnp.arange(n_freq, dtype=F32) / n_freq)
    ang = jnp.concatenate([row_id[None, :] * inv[:, None], col_id[None, :] * inv[:, None]], axis=0)
    return jnp.cos(ang), jnp.sin(ang)


def kernel(x_prompt, x_sample, cache_k, cache_v, state_ret_fwd, state_ret_bwd, c, c_ctx, w_ada, b_ada, norm1, w_in, q_norm, k_norm, ret_decay_f, ret_decay_b, ret_norm, w_br_att, w_br_ret, w_br_four, w_out, norm2, w_up, conv_w, conv_b, w_down):
    batch, seq, _ = x_prompt.shape
    dec_batch, dec_seq, _ = x_sample.shape
    past = cache_k.shape[2]

    cond_all = jnp.concatenate([c_ctx[None, :], c], axis=0)
    mod_all = _ada(cond_all, w_ada, b_ada)
    cos_t, sin_t = _rope_tables(dec_seq)
    ck = cache_k.reshape(dec_batch, DEPTH, past, ATT_KV)
    cv = cache_v.reshape(dec_batch, DEPTH, past, ATT_KV)

    xp = x_prompt.reshape(batch * seq, D_MODEL)
    xs = x_sample.reshape(dec_batch * dec_seq, D_MODEL)
    new_kv = (jnp.zeros((batch, DEPTH, seq, ATT_KV), F32),) * 2
    new_st = (jnp.zeros((batch, DEPTH, RET_HEADS, RET_DK, RET_DV), F32),) * 2
    w_in_bf = w_in.astype(BF16)
    for l in range(DEPTH):
        w_att, w_ret, w_four = (w_br_att[l].astype(BF16), w_br_ret[l].astype(BF16),
                                w_br_four[l].astype(BF16))
        w_o = w_out[l].astype(BF16)
        n1, n2 = norm1[l][None, :], norm2[l][None, :]
        qn, kn = q_norm[l][:, None], k_norm[l][:, None]
        rn = ret_norm[l][None, :]
        dec = jnp.broadcast_to(
            jnp.concatenate([ret_decay_f[l], ret_decay_b[l]])[:, None].astype(F32), (2 * RET_HEADS, 128))
        cw, cb = conv_w[l], conv_b[l][None, :]
        mod_ctx = mod_all[l, 0:1].reshape(1, 1, 6 * D_MODEL)
        mod_lat = mod_all[l, 1:].reshape(dec_batch, 1, 6 * D_MODEL)

        p_act, p_gate = _in_proj(xp, mod_ctx, batch * seq, n1, w_in_bf, l)
        o_att, *new_kv = _att_ctx(p_act, seq, qn, kn, l, new_kv)
        o_ret, *new_st = _retention(p_act, seq, dec, rn, l, new_states=new_st)
        o_four = _fourier(p_act, seq)
        xp = _mix(o_att, o_ret, o_four, p_gate, xp, seq, mod_ctx, batch * seq,
                  w_att, w_ret, w_four, w_o, n2, w_up, cw, cb, w_down, l)

        p_act, p_gate = _in_proj(xs, mod_lat, dec_seq, n1, w_in_bf, l)
        o_att = _att_lat(p_act, dec_seq, ck, cv, l, qn, kn, cos_t, sin_t)
        o_ret = _retention(p_act, dec_seq, dec, rn, l, states=(state_ret_fwd, state_ret_bwd))
        o_four = _fourier(p_act, dec_seq)
        xs = _mix(o_att, o_ret, o_four, p_gate, xs, dec_seq, mod_lat, dec_seq,
                  w_att, w_ret, w_four, w_o, n2, w_up, cw, cb, w_down, l)

    kv_shape = (batch, DEPTH, seq, N_KV_HEADS, HEAD_DIM)
    return (xp.reshape(batch, seq, D_MODEL), xs.reshape(dec_batch, dec_seq, D_MODEL),
            new_kv[0].reshape(kv_shape), new_kv[1].reshape(kv_shape), new_st[0], new_st[1])
```

```python
import functools

import numpy as np
import jax
import jax.numpy as jnp
from jax import lax
from jax.experimental import pallas as pl
from jax.experimental.pallas import tpu as pltpu

D_MODEL = 1024
DEPTH = 2
GRID_W = 64
HEAD_DIM = 64
N_HEADS = 8
N_KV_HEADS = 2
Q_PER_KV = N_HEADS // N_KV_HEADS
ATT_Q = N_HEADS * HEAD_DIM
ATT_KV = N_KV_HEADS * HEAD_DIM
RET_HEADS = 4
RET_DK = 64
RET_DV = 128
RET_QK = RET_HEADS * RET_DK
RET_V = RET_HEADS * RET_DV
FOURIER_GROUPS = 4
FOURIER_GROUP_DIM = 128
FOURIER_W = FOURIER_GROUPS * FOURIER_GROUP_DIM
D_FF = 2816
CHUNK = 128
Q_BLOCK = 256
ROPE_THETA = 10000.0
EPS = 1e-6
LOG2_E = 1.4426950408889634

F32 = jnp.float32
BF16 = jnp.bfloat16

OFF_QA, OFF_KA, OFF_VA = 0, 512, 640
OFF_QR, OFF_KR, OFF_VR, OFF_GR, OFF_UF = 768, 1024, 1280, 1792, 2304
W_IN_SPLIT = 2816
OFF_GATE = 3072
P_W = OFF_GATE + 3 * D_MODEL

VMEM_LIMIT = 56 * 1024 * 1024


def _cparams(sem, flags=None):
    return pltpu.CompilerParams(dimension_semantics=sem, vmem_limit_bytes=VMEM_LIMIT, flags=flags)


def _dot(a, b):
    return jnp.dot(a, b, preferred_element_type=F32)


def _ada_kernel(cond_ref, w_ref, b_ref, o_ref):
    cnd = cond_ref[...]
    s = cnd * jax.nn.sigmoid(cnd)
    o_ref[0] = _dot(s.astype(BF16), w_ref[0].astype(BF16)) + b_ref[0]


def _ada(cond_all, w_ada, b_ada):
    n = cond_all.shape[0]
    tn = 1024
    return pl.pallas_call(
        _ada_kernel,
        grid=(DEPTH, 6 * D_MODEL // tn),
        in_specs=[
            pl.BlockSpec((n, D_MODEL), lambda l, j: (0, 0)),
            pl.BlockSpec((1, D_MODEL, tn), lambda l, j: (l, 0, j)),
            pl.BlockSpec((1, 1, tn), lambda l, j: (l, 0, j)),
        ],
        out_specs=pl.BlockSpec((1, n, tn), lambda l, j: (l, 0, j)),
        out_shape=jax.ShapeDtypeStruct((DEPTH, n, 6 * D_MODEL), F32),
        compiler_params=_cparams(("parallel", "parallel")),
        name="ada",
    )(cond_all, w_ada, b_ada.reshape(DEPTH, 1, 6 * D_MODEL))


IN_TN = 1024
IN_ROWS = 256
MAIN_STEPS = OFF_GATE // IN_TN


def _in_kernel(x_ref, sh_ref, sc_ref, g_ref, wm_ref, wg_ref, om_ref, og_ref, h_scr):
    j = pl.program_id(1)

    @pl.when(j == 0)
    def _():
        for b in range(x_ref.shape[0] // IN_ROWS):
            rows = slice(b * IN_ROWS, (b + 1) * IN_ROWS)
            x = x_ref[rows, :]
            r = lax.rsqrt(jnp.mean(x * x, axis=-1, keepdims=True) + EPS)
            h = ((x * r) * g_ref[...] * (1.0 + sc_ref[0]) + sh_ref[0]).astype(BF16)
            h_scr[rows, :] = h
            om_ref[rows, :] = _dot(h, wm_ref[0])

    @pl.when(jnp.logical_and(j > 0, j < MAIN_STEPS))
    def _():
        om_ref[...] = _dot(h_scr[...], wm_ref[0])

    @pl.when(j >= MAIN_STEPS)
    def _():
        og_ref[...] = (0.5 * _dot(h_scr[...], wg_ref[0])).astype(og_ref.dtype)


def _in_proj(x2d, mod, rows_per_mod, norm1, w_main, w_gate, layer):
    n_tok = x2d.shape[0]
    tm, tn = 1024, IN_TN
    mod_idx = lambda i: (i * tm) // rows_per_mod
    return pl.pallas_call(
        _in_kernel,
        grid=(n_tok // tm, P_W // tn),
        in_specs=[
            pl.BlockSpec((tm, D_MODEL), lambda i, j: (i, 0)),
            pl.BlockSpec((1, 1, D_MODEL), lambda i, j: (mod_idx(i), 0, 0)),
            pl.BlockSpec((1, 1, D_MODEL), lambda i, j: (mod_idx(i), 0, 1)),
            pl.BlockSpec((1, D_MODEL), lambda i, j: (0, 0)),
            pl.BlockSpec((1, D_MODEL, tn), lambda i, j: (layer, 0, jnp.minimum(j, MAIN_STEPS - 1))),
            pl.BlockSpec((1, D_MODEL, tn), lambda i, j: (layer, 0, jnp.maximum(j - MAIN_STEPS, 0))),
        ],
        out_specs=[
            pl.BlockSpec((tm, tn), lambda i, j: (i, jnp.minimum(j, MAIN_STEPS - 1))),
            pl.BlockSpec((tm, tn), lambda i, j: (i, jnp.maximum(j - MAIN_STEPS, 0))),
        ],
        out_shape=[jax.ShapeDtypeStruct((n_tok, OFF_GATE), F32),
                   jax.ShapeDtypeStruct((n_tok, P_W - OFF_GATE), BF16)],
        scratch_shapes=[pltpu.VMEM((tm, D_MODEL), BF16)],
        compiler_params=_cparams(("parallel", "arbitrary")),
        name="in_proj",
    )(x2d, mod, mod, norm1, w_main, w_gate)


def _norm_rope_heads_t(x_t, n_heads, g_col, cos_t=None, sin_t=None):
    quarter = HEAD_DIM // 4
    outs = []
    for h in range(n_heads):
        x = x_t[h * HEAD_DIM:(h + 1) * HEAD_DIM, :]
        r = lax.rsqrt(jnp.mean(x * x, axis=0, keepdims=True) + EPS)
        y = (x * r) * g_col
        if cos_t is not None:
            pieces = []
            for a in range(2):
                c = cos_t[a * quarter:(a + 1) * quarter]
                s = sin_t[a * quarter:(a + 1) * quarter]
                x1 = y[2 * a * quarter:(2 * a + 1) * quarter]
                x2 = y[(2 * a + 1) * quarter:(2 * a + 2) * quarter]
                pieces += [x1 * c - x2 * s, x2 * c + x1 * s]
            y = jnp.concatenate(pieces, axis=0)
        outs.append(y)
    return outs


def _attend_t(q_heads_t, k_bf, v_t_bf):
    tq = q_heads_t[0].shape[1]
    cols = Q_PER_KV * tq
    zeros = jnp.zeros((HEAD_DIM, cols), F32)
    q_kv = [jnp.concatenate(q_heads_t[kv * Q_PER_KV:(kv + 1) * Q_PER_KV], axis=1)
            * (HEAD_DIM ** -0.5 * LOG2_E) for kv in range(N_KV_HEADS)]
    rhs = jnp.concatenate([jnp.concatenate([q_kv[0], zeros], axis=1),
                           jnp.concatenate([zeros, q_kv[1]], axis=1)], axis=0).astype(BF16)
    s_all = _dot(k_bf, rhs)
    out_rows = []
    for kv in range(N_KV_HEADS):
        s_t = s_all[:, kv * cols:(kv + 1) * cols]
        e = jnp.exp2(s_t - jnp.max(s_t, axis=0, keepdims=True))
        inv = 1.0 / jnp.sum(e, axis=0, keepdims=True)
        o_t = _dot(v_t_bf[kv * HEAD_DIM:(kv + 1) * HEAD_DIM, :], e.astype(BF16)) * inv
        out_rows += [o_t[:, g * tq:(g + 1) * tq] for g in range(Q_PER_KV)]
    return jnp.concatenate(out_rows, axis=0)


def _att_ctx_kernel(q_ref, k_ref, v_ref, qn_ref, kn_ref, ck_any, cv_any, o_ref, ko_ref, vo_ref):
    del ck_any, cv_any
    seq = ko_ref.shape[2]
    for sq in range(ko_ref.shape[0]):
        rows = slice(sq * seq, (sq + 1) * seq)
        k_heads = _norm_rope_heads_t(k_ref[rows, :].T, N_KV_HEADS, kn_ref[...])
        k_n = jnp.concatenate(k_heads, axis=0).T
        ko_ref[sq, 0] = k_n
        v = v_ref[rows, :]
        vo_ref[sq, 0] = v
        q_heads = _norm_rope_heads_t(q_ref[rows, :].T, N_HEADS, qn_ref[...])
        o_t = _attend_t(q_heads, k_n.astype(BF16), v.T.astype(BF16))
        o_ref[rows, :] = o_t.T.astype(o_ref.dtype)


ATT_CTX_ROWS = 1024


def _att_ctx(p_act, seq, q_norm, k_norm, layer, caches):
    n_tok = p_act.shape[0]
    batch = n_tok // seq
    rows = max(seq, ATT_CTX_ROWS)
    blk = rows // seq
    cache_spec = pl.BlockSpec((blk, 1, seq, ATT_KV), lambda b: (b, layer, 0, 0))
    cache_shape = jax.ShapeDtypeStruct((batch, DEPTH, seq, ATT_KV), F32)
    return pl.pallas_call(
        _att_ctx_kernel,
        grid=(n_tok // rows,),
        in_specs=[
            pl.BlockSpec((rows, ATT_Q), lambda b: (b, OFF_QA // ATT_Q)),
            pl.BlockSpec((rows, ATT_KV), lambda b: (b, OFF_KA // ATT_KV)),
            pl.BlockSpec((rows, ATT_KV), lambda b: (b, OFF_VA // ATT_KV)),
            pl.BlockSpec((HEAD_DIM, 1), lambda b: (0, 0)),
            pl.BlockSpec((HEAD_DIM, 1), lambda b: (0, 0)),
            pl.BlockSpec(memory_space=pl.ANY),
            pl.BlockSpec(memory_space=pl.ANY),
        ],
        out_specs=[pl.BlockSpec((rows, ATT_Q), lambda b: (b, 0)), cache_spec, cache_spec],
        out_shape=[jax.ShapeDtypeStruct((n_tok, ATT_Q), BF16), cache_shape, cache_shape],
        input_output_aliases={5: 1, 6: 2},
        compiler_params=_cparams(("parallel",)),
        name="att_ctx",
    )(p_act, p_act, p_act, q_norm, k_norm, *caches)


def _att_lat_kernel(q_ref, k_ref, v_ref, ck_ref, cv_ref, qn_ref, kn_ref,
                    cq_ref, sq_ref, ck_t_ref, sk_t_ref, o_ref, kf_scr, vt_scr, *, seq):
    @pl.when(pl.program_id(1) == 0)
    def _():
        k_heads = _norm_rope_heads_t(k_ref[...].T, N_KV_HEADS, kn_ref[...], ck_t_ref[...], sk_t_ref[...])
        kf_scr[0:seq, :] = jnp.concatenate(k_heads, axis=0).T.astype(BF16)
        kf_scr[seq:, :] = ck_ref[0, 0].astype(BF16)
        vt_scr[:, 0:seq] = v_ref[...].T.astype(BF16)
        vt_scr[:, seq:] = cv_ref[0, 0].T.astype(BF16)

    q_heads = _norm_rope_heads_t(q_ref[...].T, N_HEADS, qn_ref[...], cq_ref[...], sq_ref[...])
    o_ref[...] = _attend_t(q_heads, kf_scr[...], vt_scr[...]).T.astype(o_ref.dtype)


def _att_lat(p_act, seq, cache_k, cache_v, layer, q_norm, k_norm, cos_t, sin_t):
    n_tok = p_act.shape[0]
    nb = seq // Q_BLOCK
    past = cache_k.shape[2]
    return pl.pallas_call(
        functools.partial(_att_lat_kernel, seq=seq),
        grid=(n_tok // seq, nb),
        in_specs=[
            pl.BlockSpec((Q_BLOCK, ATT_Q), lambda b, i: (b * nb + i, OFF_QA // ATT_Q)),
            pl.BlockSpec((seq, ATT_KV), lambda b, i: (b, OFF_KA // ATT_KV)),
            pl.BlockSpec((seq, ATT_KV), lambda b, i: (b, OFF_VA // ATT_KV)),
            pl.BlockSpec((1, 1, past, ATT_KV), lambda b, i: (b, layer, 0, 0)),
            pl.BlockSpec((1, 1, past, ATT_KV), lambda b, i: (b, layer, 0, 0)),
            pl.BlockSpec((HEAD_DIM, 1), lambda b, i: (0, 0)),
            pl.BlockSpec((HEAD_DIM, 1), lambda b, i: (0, 0)),
            pl.BlockSpec((HEAD_DIM // 2, Q_BLOCK), lambda b, i: (0, i)),
            pl.BlockSpec((HEAD_DIM // 2, Q_BLOCK), lambda b, i: (0, i)),
            pl.BlockSpec((HEAD_DIM // 2, seq), lambda b, i: (0, 0)),
            pl.BlockSpec((HEAD_DIM // 2, seq), lambda b, i: (0, 0)),
        ],
        out_specs=pl.BlockSpec((Q_BLOCK, ATT_Q), lambda b, i: (b * nb + i, 0)),
        out_shape=jax.ShapeDtypeStruct((n_tok, ATT_Q), BF16),
        scratch_shapes=[pltpu.VMEM((seq + past, ATT_KV), BF16),
                        pltpu.VMEM((ATT_KV, seq + past), BF16)],
        compiler_params=_cparams(("parallel", "arbitrary")),
        name="att_lat",
    )(p_act, p_act, p_act, cache_k, cache_v, q_norm, k_norm, cos_t, sin_t, cos_t, sin_t)


RET_ROWS = 1024


def _log_sigmoid(d):
    return jnp.minimum(d, 0.0) - jnp.log1p(jnp.exp(-jnp.abs(d)))


TAB_MASK, TAB_QF, TAB_QB, TAB_KF, TAB_KB, TAB_C, N_TAB = 0, 2, 4, 6, 7, 8, 9
TAB_CF, TAB_CB = 0, 2


def _ret_tables_kernel(dec_ref, tab_ref):
    hp = pl.program_id(0)
    ii = lax.broadcasted_iota(jnp.int32, (CHUNK, CHUNK), 0)
    jj = lax.broadcasted_iota(jnp.int32, (CHUNK, CHUNK), 1)
    rel = (ii - jj).astype(F32)
    row = ii.astype(F32)
    lane = jj.astype(F32)
    lgf = [_log_sigmoid(dec_ref[pl.ds(2 * hp + t, 1), :]) for t in range(2)]
    lgb = [_log_sigmoid(dec_ref[pl.ds(RET_HEADS + 2 * hp + t, 1), :]) for t in range(2)]
    for t in range(2):
        tab_ref[0, TAB_MASK + t] = jnp.where(
            rel > 0, jnp.exp(jnp.maximum(rel, 0.0) * lgf[t]),
            jnp.where(rel < 0, jnp.exp(jnp.maximum(-rel, 0.0) * lgb[t]), 2.0))
        tab_ref[0, TAB_QF + t] = jnp.exp((row + 1.0) * lgf[t])
        tab_ref[0, TAB_QB + t] = jnp.exp((CHUNK - row) * lgb[t])
    tab_ref[0, TAB_KF] = jnp.exp((CHUNK - 1.0 - lane) * jnp.where(ii < RET_DK, lgf[0], lgf[1]))
    tab_ref[0, TAB_KB] = jnp.exp(lane * jnp.where(ii < RET_DK, lgb[0], lgb[1]))
    c_rows = jnp.where(ii == TAB_CF, lgf[0], jnp.where(ii == TAB_CF + 1, lgf[1],
                       jnp.where(ii == TAB_CB, lgb[0], lgb[1])))
    tab_ref[0, TAB_C] = jnp.exp(CHUNK * c_rows)


def _ret_tables(dec):
    return pl.pallas_call(
        _ret_tables_kernel,
        grid=(RET_HEADS // 2,),
        in_specs=[pl.BlockSpec((8, 128), lambda p: (0, 0))],
        out_specs=pl.BlockSpec((1, N_TAB, CHUNK, CHUNK), lambda p: (p, 0, 0, 0)),
        out_shape=jax.ShapeDtypeStruct((RET_HEADS // 2, N_TAB, CHUNK, CHUNK), F32),
        compiler_params=_cparams(("parallel",)),
        name="ret_tables",
    )(dec)


def _ret_kernel(q_ref, k_ref, v_ref, g_ref, tab_ref, rn_ref, *rest, seq, has_state):
    if has_state:
        s0f_ref, s0b_ref, o_ref = rest
    else:
        o_ref, sf_ref, sb_ref = rest[-3:]
    n_chunks = q_ref.shape[0] // CHUNK
    jj = lax.broadcasted_iota(jnp.int32, (CHUNK, CHUNK), 1)
    mask2 = jnp.concatenate([tab_ref[0, TAB_MASK], tab_ref[0, TAB_MASK + 1]], axis=0)
    qdec_f = [tab_ref[0, TAB_QF + t] for t in range(2)]
    qdec_b = [tab_ref[0, TAB_QB + t] for t in range(2)]
    cdec_f = [tab_ref[0, TAB_C, TAB_CF + t:TAB_CF + t + 1, :] for t in range(2)]
    cdec_b = [tab_ref[0, TAB_C, TAB_CB + t:TAB_CB + t + 1, :] for t in range(2)]
    kdec_f = tab_ref[0, TAB_KF]
    kdec_b = tab_ref[0, TAB_KB]

    k_t = (k_ref[...] * (RET_DK ** -0.5)).T
    first_head = jj < RET_DK

    def chunk(c):
        return slice(c * CHUNK, (c + 1) * CHUNK)

    kv_f = [[None] * n_chunks for _ in range(2)]
    kv_b = [[None] * n_chunks for _ in range(2)]
    for c in range(n_chunks):
        k_c = k_t[:, chunk(c)]
        kd = jnp.concatenate([k_c * kdec_f, k_c * kdec_b], axis=0).astype(BF16)
        kv = _dot(kd, v_ref[chunk(c), :].astype(BF16))
        for t in range(2):
            kv_f[t][c] = kv[t * RET_DK:(t + 1) * RET_DK, t * RET_DV:(t + 1) * RET_DV]
            kv_b[t][c] = kv[CHUNK + t * RET_DK:CHUNK + (t + 1) * RET_DK, t * RET_DV:(t + 1) * RET_DV]

    st_f = [[None] * n_chunks for _ in range(2)]
    st_b = [[None] * n_chunks for _ in range(2)]
    per_seq = seq // CHUNK
    for t in range(2):
        for sq in range(n_chunks // per_seq):
            own = range(sq * per_seq, (sq + 1) * per_seq)
            if has_state:
                s_f = s0f_ref[sq, 0, t]
                s_b = s0b_ref[sq, 0, t]
            else:
                s_f = jnp.zeros((RET_DK, RET_DV), F32)
                s_b = s_f
            for c in own:
                st_f[t][c] = s_f
                s_f = s_f * cdec_f[t] + kv_f[t][c]
            for c in reversed(own):
                st_b[t][c] = s_b
                s_b = s_b * cdec_b[t] + kv_b[t][c]
            if not has_state:
                sf_ref[sq, 0, t] = s_f
                sb_ref[sq, 0, t] = s_b

    for c in range(n_chunks):
        q_c = q_ref[chunk(c), :]
        qm = jnp.concatenate([jnp.where(first_head, q_c, 0.0), jnp.where(first_head, 0.0, q_c)],
                             axis=0).astype(BF16)
        att = (_dot(qm, k_t[:, chunk(c)].astype(BF16)) * mask2).astype(BF16)
        states = jnp.concatenate(
            [jnp.concatenate([st_f[t][c], st_b[t][c]], axis=1) for t in range(2)], axis=0)
        qs = _dot(qm, states.astype(BF16))
        v_c = v_ref[chunk(c), :].astype(BF16)
        for t in range(2):
            rows = slice(t * CHUNK, (t + 1) * CHUNK)
            vsl = slice(t * RET_DV, (t + 1) * RET_DV)
            o = (_dot(att[rows], v_c[:, vsl]) + qs[rows, :RET_DV] * qdec_f[t]
                 + qs[rows, RET_DV:] * qdec_b[t])
            o = (o * lax.rsqrt(jnp.mean(o * o, axis=-1, keepdims=True) + EPS)) * rn_ref[...]
            hg = 0.5 * g_ref[chunk(c), vsl]
            o_ref[chunk(c), vsl] = (o * (hg * (1.0 + jnp.tanh(hg)))).astype(o_ref.dtype)


def _retention(p_act, seq, tables, ret_norm, layer, states=None, new_states=None):
    n_tok = p_act.shape[0]
    n_seq = n_tok // seq
    has_state = states is not None
    pair_qk, pair_v = 2 * RET_DK, 2 * RET_DV
    rows = max(seq, RET_ROWS)
    blk_seqs = rows // seq
    in_specs = [
        pl.BlockSpec((rows, pair_qk), lambda p, s: (s, OFF_QR // pair_qk + p)),
        pl.BlockSpec((rows, pair_qk), lambda p, s: (s, OFF_KR // pair_qk + p)),
        pl.BlockSpec((rows, pair_v), lambda p, s: (s, OFF_VR // pair_v + p)),
        pl.BlockSpec((rows, pair_v), lambda p, s: (s, OFF_GR // pair_v + p)),
        pl.BlockSpec((1, N_TAB, CHUNK, CHUNK), lambda p, s: (p, 0, 0, 0)),
        pl.BlockSpec((1, RET_DV), lambda p, s: (0, 0)),
    ]
    args = [p_act, p_act, p_act, p_act, tables, ret_norm]
    o_spec = pl.BlockSpec((rows, pair_v), lambda p, s: (s, p))
    o_shape = jax.ShapeDtypeStruct((n_tok, RET_V), BF16)
    st_spec = pl.BlockSpec((blk_seqs, 1, 2, RET_DK, RET_DV), lambda p, s: (s, layer, p, 0, 0))
    aliases = {}
    if has_state:
        in_specs += [st_spec, st_spec]
        args += list(states)
        out_specs, out_shape = o_spec, o_shape
    else:
        in_specs += [pl.BlockSpec(memory_space=pl.ANY)] * 2
        aliases = {len(args): 1, len(args) + 1: 2}
        args += list(new_states)
        st_shape = jax.ShapeDtypeStruct((n_seq, DEPTH, RET_HEADS, RET_DK, RET_DV), F32)
        out_specs, out_shape = [o_spec, st_spec, st_spec], [o_shape, st_shape, st_shape]
    return pl.pallas_call(
        functools.partial(_ret_kernel, seq=seq, has_state=has_state),
        grid=(RET_HEADS // 2, n_tok // rows),
        in_specs=in_specs,
        out_specs=out_specs,
        out_shape=out_shape,
        input_output_aliases=aliases,
        compiler_params=_cparams(("parallel", "parallel")),
        name="retention_lat" if has_state else "retention_ctx",
    )(*args)


FOURIER_ROWS = 1024


def _four_kernel(u0_ref, u1_ref, wc_ref, csn_ref, o_ref):
    gd = FOURIER_GROUP_DIM
    tc, ts = [], []
    for g in range(FOURIER_GROUPS):
        u_ref = (u0_ref, u1_ref)[g // 2]
        u_g = u_ref[:, (g % 2) * gd:(g % 2 + 1) * gd].astype(BF16)
        t = _dot(u_g, wc_ref[...])
        tc.append(t[:, :gd])
        ts.append(t[:, gd:])
    tc = jnp.concatenate(tc, axis=1).astype(BF16)
    ts = jnp.concatenate(ts, axis=1).astype(BF16)
    seq = csn_ref.shape[0]
    for sq in range(o_ref.shape[0] // seq):
        rows = slice(sq * seq, (sq + 1) * seq)
        t_sq = jnp.concatenate([tc[rows], ts[rows]], axis=0)
        o_ref[rows, :] = _dot(csn_ref[...], t_sq).astype(o_ref.dtype)


def _dft_tables(n):
    k = np.arange(n, dtype=np.int64)
    ang = 2.0 * np.pi * ((k[:, None] * k[None, :]) % n).astype(np.float64) / n
    scale = 1.0 / np.sqrt(n)
    return np.cos(ang) * scale, np.sin(ang) * scale


def _fourier(p_act, seq):
    n_tok = p_act.shape[0]
    cc, sc = _dft_tables(FOURIER_GROUP_DIM)
    wc = jnp.asarray(np.concatenate([cc, -sc], axis=1), F32).astype(BF16)
    cn, sn = _dft_tables(seq)
    csn = jnp.asarray(np.concatenate([cn, sn], axis=1), F32).astype(BF16)
    const = lambda shape: pl.BlockSpec(shape, lambda s: (0, 0))
    half = FOURIER_W // 2
    rows = max(seq, FOURIER_ROWS)
    return pl.pallas_call(
        _four_kernel,
        grid=(n_tok // rows,),
        in_specs=[
            pl.BlockSpec((rows, half), lambda s: (s, OFF_UF // half)),
            pl.BlockSpec((rows, half), lambda s: (s, OFF_UF // half + 1)),
            const((FOURIER_GROUP_DIM, 2 * FOURIER_GROUP_DIM)),
            const((seq, 2 * seq)),
        ],
        out_specs=pl.BlockSpec((rows, FOURIER_W), lambda s: (s, 0)),
        out_shape=jax.ShapeDtypeStruct((n_tok, FOURIER_W), BF16),
        compiler_params=_cparams(("parallel",)),
        name="fourier",
    )(p_act, p_act, wc, csn)


MIX_TM = 1024
MERGE_STEPS = 4
MERGE_TM = MIX_TM // MERGE_STEPS
MERGE_ROWS = 256
FF_TILE = 256
FF_STEPS = D_FF // FF_TILE
FF_ROWS = 256


def _mix_kernel(oa_ref, or_ref, of_ref, ga_ref, gr_ref, gf_ref, x_ref, g1_ref, sh_ref, sc_ref,
                g2_ref, n2_ref, wba_ref, wbr_ref, wbf_ref, wo_ref, wa_ref, wv_ref, cw_ref, cb_ref,
                wd_ref, o_ref, x1_scr, h_scr, acc_scr, *, seq):
    j = pl.program_id(1)

    @pl.when(j < MERGE_STEPS)
    def _():
        gate = lambda ref, rows: jnp.tanh(ref[rows, :].astype(F32)) + 1.0
        for b in range(MERGE_TM // MERGE_ROWS):
            rows = slice(b * MERGE_ROWS, (b + 1) * MERGE_ROWS)
            dst = pl.ds(pl.multiple_of(j * MERGE_TM + b * MERGE_ROWS, MERGE_ROWS), MERGE_ROWS)
            merged = (gate(ga_ref, rows) * _dot(oa_ref[rows, :], wba_ref[...])
                      + gate(gr_ref, rows) * _dot(or_ref[rows, :], wbr_ref[...])
                      + gate(gf_ref, rows) * _dot(of_ref[rows, :], wbf_ref[...]))
            x1 = x_ref[rows, :] + g1_ref[0] * _dot((0.5 * merged).astype(BF16), wo_ref[...])
            x1_scr[dst, :] = x1
            r = lax.rsqrt(jnp.mean(x1 * x1, axis=-1, keepdims=True) + EPS)
            h = (x1 * r) * n2_ref[...] * (1.0 + sc_ref[0]) + sh_ref[0]
            h_scr[dst, :] = h.astype(BF16)
            acc_scr[dst, :] = jnp.zeros((MERGE_ROWS, D_MODEL), F32)

    @pl.when(j >= MERGE_STEPS)
    def _():
        blocks = [slice(b * FF_ROWS, (b + 1) * FF_ROWS) for b in range(MIX_TM // FF_ROWS)]
        wa, wv, wd = (ref[0].astype(BF16) for ref in (wa_ref, wv_ref, wd_ref))
        a = jnp.concatenate([_dot(h_scr[rows, :], wa) for rows in blocks], axis=0)
        val = [_dot(h_scr[rows, :], wv) for rows in blocks]
        pos = lax.broadcasted_iota(jnp.int32, a.shape, 0) % seq
        prev = jnp.where(pos == 0, 0.0, pltpu.roll(a, 1, 0))
        nxt = jnp.where(pos == seq - 1, 0.0, pltpu.roll(a, MIX_TM - 1, 0))
        cw = cw_ref[...]
        ac = prev * cw[0:1] + a * cw[1:2] + nxt * cw[2:3] + cb_ref[...]
        for b, rows in enumerate(blocks):
            act = jax.nn.gelu(ac[rows]) * val[b]
            acc_scr[rows, :] += _dot(act.astype(BF16), wd)

    @pl.when(j == pl.num_programs(1) - 1)
    def _():
        o_ref[...] = x1_scr[...] + g2_ref[0] * acc_scr[...]


def _mix(o_att, o_ret, o_four, p_gate, x2d, seq, mod, rows_per_mod, w_att, w_ret, w_four, w_out,
         norm2, w_up, conv_w, conv_b, w_down, layer):
    n_tok = x2d.shape[0]
    mod_idx = lambda i: (i * MIX_TM) // rows_per_mod
    sub = lambda i, j: i * MERGE_STEPS + jnp.minimum(j, MERGE_STEPS - 1)
    ff = lambda j: jnp.maximum(j - MERGE_STEPS, 0)
    br = lambda: pl.BlockSpec((MERGE_TM, 512), lambda i, j: (sub(i, j), 0))
    gate = lambda k: pl.BlockSpec((MERGE_TM, D_MODEL), lambda i, j: (sub(i, j), k))
    modv = lambda k: pl.BlockSpec((1, 1, D_MODEL), lambda i, j: (mod_idx(i), 0, k))
    const = lambda shape: pl.BlockSpec(shape, lambda i, j: (0, 0))
    return pl.pallas_call(
        functools.partial(_mix_kernel, seq=seq),
        grid=(n_tok // MIX_TM, MERGE_STEPS + FF_STEPS),
        in_specs=[
            br(), br(), br(), gate(0), gate(1), gate(2),
            pl.BlockSpec((MERGE_TM, D_MODEL), lambda i, j: (sub(i, j), 0)),
            modv(2), modv(3), modv(4), modv(5),
            const((1, D_MODEL)),
            const((512, D_MODEL)), const((512, D_MODEL)), const((512, D_MODEL)),
            const((D_MODEL, D_MODEL)),
            pl.BlockSpec((1, D_MODEL, FF_TILE), lambda i, j: (layer, 0, ff(j))),
            pl.BlockSpec((1, D_MODEL, FF_TILE), lambda i, j: (layer, 0, FF_STEPS + ff(j))),
            pl.BlockSpec((3, FF_TILE), lambda i, j: (0, ff(j))),
            pl.BlockSpec((1, FF_TILE), lambda i, j: (0, ff(j))),
            pl.BlockSpec((1, FF_TILE, D_MODEL), lambda i, j: (layer, ff(j), 0)),
        ],
        out_specs=pl.BlockSpec((MIX_TM, D_MODEL), lambda i, j: (i, 0)),
        out_shape=jax.ShapeDtypeStruct((n_tok, D_MODEL), F32),
        scratch_shapes=[pltpu.VMEM((MIX_TM, D_MODEL), F32), pltpu.VMEM((MIX_TM, D_MODEL), BF16),
                        pltpu.VMEM((MIX_TM, D_MODEL), F32)],
        compiler_params=_cparams(("parallel", "arbitrary")),
        name="mix",
    )(o_att, o_ret, o_four, p_gate, p_gate, p_gate, x2d, mod, mod, mod, mod, norm2,
      w_att, w_ret, w_four, w_out, w_up, w_up, conv_w, conv_b, w_down)


def _rope_tables(n_tok):
    rows = n_tok // GRID_W
    row_id = jnp.repeat(jnp.arange(rows), GRID_W).astype(F32)
    col_id = jnp.tile(jnp.arange(GRID_W), rows).astype(F32)
    n_freq = HEAD_DIM // 4
    inv = ROPE_THETA ** (-jnp.arange(n_freq, dtype=F32) / n_freq)
    ang = jnp.concatenate([row_id[None, :] * inv[:, None], col_id[None, :] * inv[:, None]], axis=0)
    return jnp.cos(ang), jnp.sin(ang)


def kernel(x_prompt, x_sample, cache_k, cache_v, state_ret_fwd, state_ret_bwd, c, c_ctx, w_ada, b_ada, norm1, w_in, q_norm, k_norm, ret_decay_f, ret_decay_b, ret_norm, w_br_att, w_br_ret, w_br_four, w_out, norm2, w_up, conv_w, conv_b, w_down):
    batch, seq, _ = x_prompt.shape
    dec_batch, dec_seq, _ = x_sample.shape
    past = cache_k.shape[2]

    cond_all = jnp.concatenate([c_ctx[None, :], c], axis=0)
    mod_all = _ada(cond_all, w_ada, b_ada)
    cos_t, sin_t = _rope_tables(dec_seq)
    ck = cache_k.reshape(dec_batch, DEPTH, past, ATT_KV)
    cv = cache_v.reshape(dec_batch, DEPTH, past, ATT_KV)

    xp = x_prompt.reshape(batch * seq, D_MODEL)
    xs = x_sample.reshape(dec_batch * dec_seq, D_MODEL)
    new_kv = (jnp.zeros((batch, DEPTH, seq, ATT_KV), F32),) * 2
    new_st = (jnp.zeros((batch, DEPTH, RET_HEADS, RET_DK, RET_DV), F32),) * 2
    w_main = w_in[:, :, :OFF_GATE].astype(BF16)
    w_gate = w_in[:, :, W_IN_SPLIT:].astype(BF16)
    for l in range(DEPTH):
        w_att, w_ret, w_four = (w_br_att[l].astype(BF16), w_br_ret[l].astype(BF16),
                                w_br_four[l].astype(BF16))
        w_o = w_out[l].astype(BF16)
        n1, n2 = norm1[l][None, :], norm2[l][None, :]
        qn, kn = q_norm[l][:, None], k_norm[l][:, None]
        rn = ret_norm[l][None, :]
        dec = _ret_tables(jnp.broadcast_to(
            jnp.concatenate([ret_decay_f[l], ret_decay_b[l]])[:, None].astype(F32), (2 * RET_HEADS, 128)))
        cw, cb = conv_w[l], conv_b[l][None, :]
        mod_ctx = mod_all[l, 0:1].reshape(1, 1, 6 * D_MODEL)
        mod_lat = mod_all[l, 1:].reshape(dec_batch, 1, 6 * D_MODEL)

        p_act, p_gate = _in_proj(xp, mod_ctx, batch * seq, n1, w_main, w_gate, l)
        o_att, *new_kv = _att_ctx(p_act, seq, qn, kn, l, new_kv)
        o_ret, *new_st = _retention(p_act, seq, dec, rn, l, new_states=new_st)
        o_four = _fourier(p_act, seq)
        xp = _mix(o_att, o_ret, o_four, p_gate, xp, seq, mod_ctx, batch * seq,
                  w_att, w_ret, w_four, w_o, n2, w_up, cw, cb, w_down, l)

        p_act, p_gate = _in_proj(xs, mod_lat, dec_seq, n1, w_main, w_gate, l)
        o_att = _att_lat(p_act, dec_seq, ck, cv, l, qn, kn, cos_t, sin_t)
        o_ret = _retention(p_act, dec_seq, dec, rn, l, states=(state_ret_fwd, state_ret_bwd))
        o_four = _fourier(p_act, dec_seq)
        xs = _mix(o_att, o_ret, o_four, p_gate, xs, dec_seq, mod_lat, dec_seq,
                  w_att, w_ret, w_four, w_o, n2, w_up, cw, cb, w_down, l)

    kv_shape = (batch, DEPTH, seq, N_KV_HEADS, HEAD_DIM)
    return (xp.reshape(batch, seq, D_MODEL), xs.reshape(dec_batch, dec_seq, D_MODEL),
            new_kv[0].reshape(kv_shape), new_kv[1].reshape(kv_shape), new_st[0], new_st[1])
```

```python
import functools

import numpy as np
import jax
import jax.numpy as jnp
from jax import lax
from jax.experimental import pallas as pl
from jax.experimental.pallas import tpu as pltpu

D_MODEL = 1024
DEPTH = 2
GRID_W = 64
HEAD_DIM = 64
N_HEADS = 8
N_KV_HEADS = 2
Q_PER_KV = N_HEADS // N_KV_HEADS
ATT_Q = N_HEADS * HEAD_DIM
ATT_KV = N_KV_HEADS * HEAD_DIM
RET_HEADS = 4
RET_DK = 64
RET_DV = 128
RET_QK = RET_HEADS * RET_DK
RET_V = RET_HEADS * RET_DV
FOURIER_GROUPS = 4
FOURIER_GROUP_DIM = 128
FOURIER_W = FOURIER_GROUPS * FOURIER_GROUP_DIM
D_FF = 2816
CHUNK = 128
Q_BLOCK = 256
ROPE_THETA = 10000.0
EPS = 1e-6
LOG2_E = 1.4426950408889634

F32 = jnp.float32
BF16 = jnp.bfloat16

OFF_QA, OFF_KA, OFF_VA = 0, 512, 640
OFF_QR, OFF_KR, OFF_VR, OFF_GR, OFF_UF = 768, 1024, 1280, 1792, 2304
W_IN_SPLIT = 2816
OFF_GATE = 3072
P_W = OFF_GATE + 3 * D_MODEL

VMEM_LIMIT = 56 * 1024 * 1024


def _cparams(sem, flags=None):
    return pltpu.CompilerParams(dimension_semantics=sem, vmem_limit_bytes=VMEM_LIMIT, flags=flags)


def _dot(a, b):
    return jnp.dot(a, b, preferred_element_type=F32)


def _ada_kernel(cond_ref, w_ref, b_ref, o_ref):
    cnd = cond_ref[...]
    s = cnd * jax.nn.sigmoid(cnd)
    o_ref[0] = _dot(s.astype(BF16), w_ref[0].astype(BF16)) + b_ref[0]


def _ada(cond_all, w_ada, b_ada):
    n = cond_all.shape[0]
    tn = 1024
    return pl.pallas_call(
        _ada_kernel,
        grid=(DEPTH, 6 * D_MODEL // tn),
        in_specs=[
            pl.BlockSpec((n, D_MODEL), lambda l, j: (0, 0)),
            pl.BlockSpec((1, D_MODEL, tn), lambda l, j: (l, 0, j)),
            pl.BlockSpec((1, 1, tn), lambda l, j: (l, 0, j)),
        ],
        out_specs=pl.BlockSpec((1, n, tn), lambda l, j: (l, 0, j)),
        out_shape=jax.ShapeDtypeStruct((DEPTH, n, 6 * D_MODEL), F32),
        compiler_params=_cparams(("parallel", "parallel")),
        name="ada",
    )(cond_all, w_ada, b_ada.reshape(DEPTH, 1, 6 * D_MODEL))


IN_TN = 1024
IN_ROWS = 256
MAIN_STEPS = OFF_GATE // IN_TN


def _in_kernel(x_ref, sh_ref, sc_ref, g_ref, wm_ref, wg_ref, om_ref, og_ref, h_scr):
    j = pl.program_id(1)

    @pl.when(j == 0)
    def _():
        for b in range(x_ref.shape[0] // IN_ROWS):
            rows = slice(b * IN_ROWS, (b + 1) * IN_ROWS)
            x = x_ref[rows, :]
            r = lax.rsqrt(jnp.mean(x * x, axis=-1, keepdims=True) + EPS)
            h = ((x * r) * g_ref[...] * (1.0 + sc_ref[0]) + sh_ref[0]).astype(BF16)
            h_scr[rows, :] = h
            om_ref[rows, :] = _dot(h, wm_ref[0])

    @pl.when(jnp.logical_and(j > 0, j < MAIN_STEPS))
    def _():
        om_ref[...] = _dot(h_scr[...], wm_ref[0])

    @pl.when(j >= MAIN_STEPS)
    def _():
        og_ref[...] = (0.5 * _dot(h_scr[...], wg_ref[0])).astype(og_ref.dtype)


def _in_proj(x2d, mod, rows_per_mod, norm1, w_main, w_gate, layer):
    n_tok = x2d.shape[0]
    tm, tn = 1024, IN_TN
    mod_idx = lambda i: (i * tm) // rows_per_mod
    return pl.pallas_call(
        _in_kernel,
        grid=(n_tok // tm, P_W // tn),
        in_specs=[
            pl.BlockSpec((tm, D_MODEL), lambda i, j: (i, 0)),
            pl.BlockSpec((1, 1, D_MODEL), lambda i, j: (mod_idx(i), 0, 0)),
            pl.BlockSpec((1, 1, D_MODEL), lambda i, j: (mod_idx(i), 0, 1)),
            pl.BlockSpec((1, D_MODEL), lambda i, j: (0, 0)),
            pl.BlockSpec((1, D_MODEL, tn), lambda i, j: (layer, 0, jnp.minimum(j, MAIN_STEPS - 1))),
            pl.BlockSpec((1, D_MODEL, tn), lambda i, j: (layer, 0, jnp.maximum(j - MAIN_STEPS, 0))),
        ],
        out_specs=[
            pl.BlockSpec((tm, tn), lambda i, j: (i, jnp.minimum(j, MAIN_STEPS - 1))),
            pl.BlockSpec((tm, tn), lambda i, j: (i, jnp.maximum(j - MAIN_STEPS, 0))),
        ],
        out_shape=[jax.ShapeDtypeStruct((n_tok, OFF_GATE), F32),
                   jax.ShapeDtypeStruct((n_tok, P_W - OFF_GATE), BF16)],
        scratch_shapes=[pltpu.VMEM((tm, D_MODEL), BF16)],
        compiler_params=_cparams(("parallel", "arbitrary")),
        name="in_proj",
    )(x2d, mod, mod, norm1, w_main, w_gate)


def _norm_rope_heads_t(x_t, n_heads, g_col, cos_t=None, sin_t=None):
    quarter = HEAD_DIM // 4
    outs = []
    for h in range(n_heads):
        x = x_t[h * HEAD_DIM:(h + 1) * HEAD_DIM, :]
        r = lax.rsqrt(jnp.mean(x * x, axis=0, keepdims=True) + EPS)
        y = (x * r) * g_col
        if cos_t is not None:
            pieces = []
            for a in range(2):
                c = cos_t[a * quarter:(a + 1) * quarter]
                s = sin_t[a * quarter:(a + 1) * quarter]
                x1 = y[2 * a * quarter:(2 * a + 1) * quarter]
                x2 = y[(2 * a + 1) * quarter:(2 * a + 2) * quarter]
                pieces += [x1 * c - x2 * s, x2 * c + x1 * s]
            y = jnp.concatenate(pieces, axis=0)
        outs.append(y)
    return outs


def _attend_t(q_heads_t, k_bf, v_t_bf):
    tq = q_heads_t[0].shape[1]
    cols = Q_PER_KV * tq
    zeros = jnp.zeros((HEAD_DIM, cols), F32)
    q_kv = [jnp.concatenate(q_heads_t[kv * Q_PER_KV:(kv + 1) * Q_PER_KV], axis=1)
            * (HEAD_DIM ** -0.5 * LOG2_E) for kv in range(N_KV_HEADS)]
    rhs = jnp.concatenate([jnp.concatenate([q_kv[0], zeros], axis=1),
                           jnp.concatenate([zeros, q_kv[1]], axis=1)], axis=0).astype(BF16)
    s_all = _dot(k_bf, rhs)
    out_rows = []
    for kv in range(N_KV_HEADS):
        s_t = s_all[:, kv * cols:(kv + 1) * cols]
        e = jnp.exp2(s_t - jnp.max(s_t, axis=0, keepdims=True))
        inv = 1.0 / jnp.sum(e, axis=0, keepdims=True)
        o_t = _dot(v_t_bf[kv * HEAD_DIM:(kv + 1) * HEAD_DIM, :], e.astype(BF16)) * inv
        out_rows += [o_t[:, g * tq:(g + 1) * tq] for g in range(Q_PER_KV)]
    return jnp.concatenate(out_rows, axis=0)


def _att_ctx_kernel(q_ref, k_ref, v_ref, qn_ref, kn_ref, ck_any, cv_any, o_ref, ko_ref, vo_ref):
    del ck_any, cv_any
    seq = ko_ref.shape[2]
    for sq in range(ko_ref.shape[0]):
        rows = slice(sq * seq, (sq + 1) * seq)
        k_heads = _norm_rope_heads_t(k_ref[rows, :].T, N_KV_HEADS, kn_ref[...])
        k_n = jnp.concatenate(k_heads, axis=0).T
        ko_ref[sq, 0] = k_n
        v = v_ref[rows, :]
        vo_ref[sq, 0] = v
        q_heads = _norm_rope_heads_t(q_ref[rows, :].T, N_HEADS, qn_ref[...])
        o_t = _attend_t(q_heads, k_n.astype(BF16), v.T.astype(BF16))
        o_ref[rows, :] = o_t.T.astype(o_ref.dtype)


ATT_CTX_ROWS = 1024


def _att_ctx(p_act, seq, q_norm, k_norm, layer, caches):
    n_tok = p_act.shape[0]
    batch = n_tok // seq
    rows = max(seq, ATT_CTX_ROWS)
    blk = rows // seq
    cache_spec = pl.BlockSpec((blk, 1, seq, ATT_KV), lambda b: (b, layer, 0, 0))
    cache_shape = jax.ShapeDtypeStruct((batch, DEPTH, seq, ATT_KV), F32)
    return pl.pallas_call(
        _att_ctx_kernel,
        grid=(n_tok // rows,),
        in_specs=[
            pl.BlockSpec((rows, ATT_Q), lambda b: (b, OFF_QA // ATT_Q)),
            pl.BlockSpec((rows, ATT_KV), lambda b: (b, OFF_KA // ATT_KV)),
            pl.BlockSpec((rows, ATT_KV), lambda b: (b, OFF_VA // ATT_KV)),
            pl.BlockSpec((HEAD_DIM, 1), lambda b: (0, 0)),
            pl.BlockSpec((HEAD_DIM, 1), lambda b: (0, 0)),
            pl.BlockSpec(memory_space=pl.ANY),
            pl.BlockSpec(memory_space=pl.ANY),
        ],
        out_specs=[pl.BlockSpec((rows, ATT_Q), lambda b: (b, 0)), cache_spec, cache_spec],
        out_shape=[jax.ShapeDtypeStruct((n_tok, ATT_Q), BF16), cache_shape, cache_shape],
        input_output_aliases={5: 1, 6: 2},
        compiler_params=_cparams(("parallel",)),
        name="att_ctx",
    )(p_act, p_act, p_act, q_norm, k_norm, *caches)


def _att_lat_kernel(q_ref, k_ref, v_ref, ck_ref, cv_ref, qn_ref, kn_ref,
                    cq_ref, sq_ref, ck_t_ref, sk_t_ref, o_ref, kf_scr, vt_scr, *, seq):
    @pl.when(pl.program_id(1) == 0)
    def _():
        k_heads = _norm_rope_heads_t(k_ref[...].T, N_KV_HEADS, kn_ref[...], ck_t_ref[...], sk_t_ref[...])
        kf_scr[0:seq, :] = jnp.concatenate(k_heads, axis=0).T.astype(BF16)
        kf_scr[seq:, :] = ck_ref[0, 0].astype(BF16)
        vt_scr[:, 0:seq] = v_ref[...].T.astype(BF16)
        vt_scr[:, seq:] = cv_ref[0, 0].T.astype(BF16)

    q_heads = _norm_rope_heads_t(q_ref[...].T, N_HEADS, qn_ref[...], cq_ref[...], sq_ref[...])
    o_ref[...] = _attend_t(q_heads, kf_scr[...], vt_scr[...]).T.astype(o_ref.dtype)


def _att_lat(p_act, seq, cache_k, cache_v, layer, q_norm, k_norm, cos_t, sin_t):
    n_tok = p_act.shape[0]
    nb = seq // Q_BLOCK
    past = cache_k.shape[2]
    return pl.pallas_call(
        functools.partial(_att_lat_kernel, seq=seq),
        grid=(n_tok // seq, nb),
        in_specs=[
            pl.BlockSpec((Q_BLOCK, ATT_Q), lambda b, i: (b * nb + i, OFF_QA // ATT_Q)),
            pl.BlockSpec((seq, ATT_KV), lambda b, i: (b, OFF_KA // ATT_KV)),
            pl.BlockSpec((seq, ATT_KV), lambda b, i: (b, OFF_VA // ATT_KV)),
            pl.BlockSpec((1, 1, past, ATT_KV), lambda b, i: (b, layer, 0, 0)),
            pl.BlockSpec((1, 1, past, ATT_KV), lambda b, i: (b, layer, 0, 0)),
            pl.BlockSpec((HEAD_DIM, 1), lambda b, i: (0, 0)),
            pl.BlockSpec((HEAD_DIM, 1), lambda b, i: (0, 0)),
            pl.BlockSpec((HEAD_DIM // 2, Q_BLOCK), lambda b, i: (0, i)),
            pl.BlockSpec((HEAD_DIM // 2, Q_BLOCK), lambda b, i: (0, i)),
            pl.BlockSpec((HEAD_DIM // 2, seq), lambda b, i: (0, 0)),
            pl.BlockSpec((HEAD_DIM // 2, seq), lambda b, i: (0, 0)),
        ],
        out_specs=pl.BlockSpec((Q_BLOCK, ATT_Q), lambda b, i: (b * nb + i, 0)),
        out_shape=jax.ShapeDtypeStruct((n_tok, ATT_Q), BF16),
        scratch_shapes=[pltpu.VMEM((seq + past, ATT_KV), BF16),
                        pltpu.VMEM((ATT_KV, seq + past), BF16)],
        compiler_params=_cparams(("parallel", "arbitrary")),
        name="att_lat",
    )(p_act, p_act, p_act, cache_k, cache_v, q_norm, k_norm, cos_t, sin_t, cos_t, sin_t)


RET_ROWS = 1024


def _log_sigmoid(d):
    return jnp.minimum(d, 0.0) - jnp.log1p(jnp.exp(-jnp.abs(d)))


TAB_MASK, TAB_QF, TAB_QB, TAB_KF, TAB_KB, TAB_C, N_TAB = 0, 2, 4, 6, 7, 8, 9
TAB_CF, TAB_CB = 0, 2


def _ret_tables_kernel(dec_ref, tab_ref):
    hp = pl.program_id(0)
    ii = lax.broadcasted_iota(jnp.int32, (CHUNK, CHUNK), 0)
    jj = lax.broadcasted_iota(jnp.int32, (CHUNK, CHUNK), 1)
    rel = (ii - jj).astype(F32)
    row = ii.astype(F32)
    lane = jj.astype(F32)
    lgf = [_log_sigmoid(dec_ref[pl.ds(2 * hp + t, 1), :]) for t in range(2)]
    lgb = [_log_sigmoid(dec_ref[pl.ds(RET_HEADS + 2 * hp + t, 1), :]) for t in range(2)]
    for t in range(2):
        tab_ref[0, TAB_MASK + t] = jnp.where(
            rel > 0, jnp.exp(jnp.maximum(rel, 0.0) * lgf[t]),
            jnp.where(rel < 0, jnp.exp(jnp.maximum(-rel, 0.0) * lgb[t]), 2.0))
        tab_ref[0, TAB_QF + t] = jnp.exp((row + 1.0) * lgf[t])
        tab_ref[0, TAB_QB + t] = jnp.exp((CHUNK - row) * lgb[t])
    tab_ref[0, TAB_KF] = jnp.exp((CHUNK - 1.0 - lane) * jnp.where(ii < RET_DK, lgf[0], lgf[1]))
    tab_ref[0, TAB_KB] = jnp.exp(lane * jnp.where(ii < RET_DK, lgb[0], lgb[1]))
    c_rows = jnp.where(ii == TAB_CF, lgf[0], jnp.where(ii == TAB_CF + 1, lgf[1],
                       jnp.where(ii == TAB_CB, lgb[0], lgb[1])))
    tab_ref[0, TAB_C] = jnp.exp(CHUNK * c_rows)


def _ret_tables(dec):
    return pl.pallas_call(
        _ret_tables_kernel,
        grid=(RET_HEADS // 2,),
        in_specs=[pl.BlockSpec((8, 128), lambda p: (0, 0))],
        out_specs=pl.BlockSpec((1, N_TAB, CHUNK, CHUNK), lambda p: (p, 0, 0, 0)),
        out_shape=jax.ShapeDtypeStruct((RET_HEADS // 2, N_TAB, CHUNK, CHUNK), F32),
        compiler_params=_cparams(("parallel",)),
        name="ret_tables",
    )(dec)


def _ret_kernel(q_ref, k_ref, v_ref, g_ref, tab_ref, rn_ref, *rest, seq, has_state):
    if has_state:
        s0f_ref, s0b_ref, o_ref = rest
    else:
        o_ref, sf_ref, sb_ref = rest[-3:]
    n_chunks = q_ref.shape[0] // CHUNK
    jj = lax.broadcasted_iota(jnp.int32, (CHUNK, CHUNK), 1)
    mask2 = jnp.concatenate([tab_ref[0, TAB_MASK], tab_ref[0, TAB_MASK + 1]], axis=0)
    qdec_f = [tab_ref[0, TAB_QF + t] for t in range(2)]
    qdec_b = [tab_ref[0, TAB_QB + t] for t in range(2)]
    cdec_f = [tab_ref[0, TAB_C, TAB_CF + t:TAB_CF + t + 1, :] for t in range(2)]
    cdec_b = [tab_ref[0, TAB_C, TAB_CB + t:TAB_CB + t + 1, :] for t in range(2)]
    kdec_f = tab_ref[0, TAB_KF]
    kdec_b = tab_ref[0, TAB_KB]

    k_t = (k_ref[...] * (RET_DK ** -0.5)).T
    first_head = jj < RET_DK

    def chunk(c):
        return slice(c * CHUNK, (c + 1) * CHUNK)

    kv_f = [[None] * n_chunks for _ in range(2)]
    kv_b = [[None] * n_chunks for _ in range(2)]
    for c in range(n_chunks):
        k_c = k_t[:, chunk(c)]
        kd = jnp.concatenate([k_c * kdec_f, k_c * kdec_b], axis=0).astype(BF16)
        kv = _dot(kd, v_ref[chunk(c), :].astype(BF16))
        for t in range(2):
            kv_f[t][c] = kv[t * RET_DK:(t + 1) * RET_DK, t * RET_DV:(t + 1) * RET_DV]
            kv_b[t][c] = kv[CHUNK + t * RET_DK:CHUNK + (t + 1) * RET_DK, t * RET_DV:(t + 1) * RET_DV]

    st_f = [[None] * n_chunks for _ in range(2)]
    st_b = [[None] * n_chunks for _ in range(2)]
    per_seq = seq // CHUNK
    for t in range(2):
        for sq in range(n_chunks // per_seq):
            own = range(sq * per_seq, (sq + 1) * per_seq)
            if has_state:
                s_f = s0f_ref[sq, 0, t]
                s_b = s0b_ref[sq, 0, t]
            else:
                s_f = jnp.zeros((RET_DK, RET_DV), F32)
                s_b = s_f
            for c in own:
                st_f[t][c] = s_f
                s_f = s_f * cdec_f[t] + kv_f[t][c]
            for c in reversed(own):
                st_b[t][c] = s_b
                s_b = s_b * cdec_b[t] + kv_b[t][c]
            if not has_state:
                sf_ref[sq, 0, t] = s_f
                sb_ref[sq, 0, t] = s_b

    for c in range(n_chunks):
        q_c = q_ref[chunk(c), :]
        qm = jnp.concatenate([jnp.where(first_head, q_c, 0.0), jnp.where(first_head, 0.0, q_c)],
                             axis=0).astype(BF16)
        att = (_dot(qm, k_t[:, chunk(c)].astype(BF16)) * mask2).astype(BF16)
        states = jnp.concatenate(
            [jnp.concatenate([st_f[t][c], st_b[t][c]], axis=1) for t in range(2)], axis=0)
        qs = _dot(qm, states.astype(BF16))
        v_c = v_ref[chunk(c), :].astype(BF16)
        for t in range(2):
            rows = slice(t * CHUNK, (t + 1) * CHUNK)
            vsl = slice(t * RET_DV, (t + 1) * RET_DV)
            o = (_dot(att[rows], v_c[:, vsl]) + qs[rows, :RET_DV] * qdec_f[t]
                 + qs[rows, RET_DV:] * qdec_b[t])
            o = (o * lax.rsqrt(jnp.mean(o * o, axis=-1, keepdims=True) + EPS)) * rn_ref[...]
            hg = 0.5 * g_ref[chunk(c), vsl]
            o_ref[chunk(c), vsl] = (o * (hg * (1.0 + jnp.tanh(hg)))).astype(o_ref.dtype)


def _retention(p_act, seq, tables, ret_norm, layer, states=None, new_states=None):
    n_tok = p_act.shape[0]
    n_seq = n_tok // seq
    has_state = states is not None
    pair_qk, pair_v = 2 * RET_DK, 2 * RET_DV
    rows = max(seq, RET_ROWS)
    blk_seqs = rows // seq
    in_specs = [
        pl.BlockSpec((rows, pair_qk), lambda p, s: (s, OFF_QR // pair_qk + p)),
        pl.BlockSpec((rows, pair_qk), lambda p, s: (s, OFF_KR // pair_qk + p)),
        pl.BlockSpec((rows, pair_v), lambda p, s: (s, OFF_VR // pair_v + p)),
        pl.BlockSpec((rows, pair_v), lambda p, s: (s, OFF_GR // pair_v + p)),
        pl.BlockSpec((1, N_TAB, CHUNK, CHUNK), lambda p, s: (p, 0, 0, 0)),
        pl.BlockSpec((1, RET_DV), lambda p, s: (0, 0)),
    ]
    args = [p_act, p_act, p_act, p_act, tables, ret_norm]
    o_spec = pl.BlockSpec((rows, pair_v), lambda p, s: (s, p))
    o_shape = jax.ShapeDtypeStruct((n_tok, RET_V), BF16)
    st_spec = pl.BlockSpec((blk_seqs, 1, 2, RET_DK, RET_DV), lambda p, s: (s, layer, p, 0, 0))
    aliases = {}
    if has_state:
        in_specs += [st_spec, st_spec]
        args += list(states)
        out_specs, out_shape = o_spec, o_shape
    else:
        in_specs += [pl.BlockSpec(memory_space=pl.ANY)] * 2
        aliases = {len(args): 1, len(args) + 1: 2}
        args += list(new_states)
        st_shape = jax.ShapeDtypeStruct((n_seq, DEPTH, RET_HEADS, RET_DK, RET_DV), F32)
        out_specs, out_shape = [o_spec, st_spec, st_spec], [o_shape, st_shape, st_shape]
    return pl.pallas_call(
        functools.partial(_ret_kernel, seq=seq, has_state=has_state),
        grid=(RET_HEADS // 2, n_tok // rows),
        in_specs=in_specs,
        out_specs=out_specs,
        out_shape=out_shape,
        input_output_aliases=aliases,
        compiler_params=_cparams(("parallel", "parallel")),
        name="retention_lat" if has_state else "retention_ctx",
    )(*args)


FOURIER_ROWS = 1024


def _four_kernel(u0_ref, u1_ref, wc_ref, csn_ref, o_ref):
    gd = FOURIER_GROUP_DIM
    tc, ts = [], []
    for g in range(FOURIER_GROUPS):
        u_ref = (u0_ref, u1_ref)[g // 2]
        u_g = u_ref[:, (g % 2) * gd:(g % 2 + 1) * gd].astype(BF16)
        t = _dot(u_g, wc_ref[...])
        tc.append(t[:, :gd])
        ts.append(t[:, gd:])
    tc = jnp.concatenate(tc, axis=1).astype(BF16)
    ts = jnp.concatenate(ts, axis=1).astype(BF16)
    seq = csn_ref.shape[0]
    for sq in range(o_ref.shape[0] // seq):
        rows = slice(sq * seq, (sq + 1) * seq)
        t_sq = jnp.concatenate([tc[rows], ts[rows]], axis=0)
        o_ref[rows, :] = _dot(csn_ref[...], t_sq).astype(o_ref.dtype)


def _dft_tables(n):
    k = np.arange(n, dtype=np.int64)
    ang = 2.0 * np.pi * ((k[:, None] * k[None, :]) % n).astype(np.float64) / n
    scale = 1.0 / np.sqrt(n)
    return np.cos(ang) * scale, np.sin(ang) * scale


def _fourier(p_act, seq):
    n_tok = p_act.shape[0]
    cc, sc = _dft_tables(FOURIER_GROUP_DIM)
    wc = jnp.asarray(np.concatenate([cc, -sc], axis=1), F32).astype(BF16)
    cn, sn = _dft_tables(seq)
    csn = jnp.asarray(np.concatenate([cn, sn], axis=1), F32).astype(BF16)
    const = lambda shape: pl.BlockSpec(shape, lambda s: (0, 0))
    half = FOURIER_W // 2
    rows = max(seq, FOURIER_ROWS)
    return pl.pallas_call(
        _four_kernel,
        grid=(n_tok // rows,),
        in_specs=[
            pl.BlockSpec((rows, half), lambda s: (s, OFF_UF // half)),
            pl.BlockSpec((rows, half), lambda s: (s, OFF_UF // half + 1)),
            const((FOURIER_GROUP_DIM, 2 * FOURIER_GROUP_DIM)),
            const((seq, 2 * seq)),
        ],
        out_specs=pl.BlockSpec((rows, FOURIER_W), lambda s: (s, 0)),
        out_shape=jax.ShapeDtypeStruct((n_tok, FOURIER_W), BF16),
        compiler_params=_cparams(("parallel",)),
        name="fourier",
    )(p_act, p_act, wc, csn)


MIX_TM = 1024
MERGE_STEPS = 4
MERGE_TM = MIX_TM // MERGE_STEPS
FF_TILE = 256
FF_STEPS = D_FF // FF_TILE
FF_ROWS = 256


def _mix_kernel(oa_ref, or_ref, of_ref, ga_ref, gr_ref, gf_ref, x_ref, g1_ref, sh_ref, sc_ref,
                g2_ref, n2_ref, wba_ref, wbr_ref, wbf_ref, wo_ref, wa_ref, wv_ref, cw_ref, cb_ref,
                wd_ref, o_ref, x1_scr, h_scr, acc_scr, *, seq):
    j = pl.program_id(1)

    def merge(step):
        gate = lambda ref: jnp.tanh(ref[...].astype(F32)) + 1.0
        dst = pl.ds(pl.multiple_of(step * MERGE_TM, MERGE_TM), MERGE_TM)
        merged = (gate(ga_ref) * _dot(oa_ref[...], wba_ref[...])
                  + gate(gr_ref) * _dot(or_ref[...], wbr_ref[...])
                  + gate(gf_ref) * _dot(of_ref[...], wbf_ref[...]))
        x1_scr[dst, :] = x_ref[...] + g1_ref[0] * _dot((0.5 * merged).astype(BF16), wo_ref[...])
        acc_scr[dst, :] = jnp.zeros((MERGE_TM, D_MODEL), F32)

    def norm(step):
        dst = pl.ds(pl.multiple_of(step * MERGE_TM, MERGE_TM), MERGE_TM)
        x1 = x1_scr[dst, :]
        r = lax.rsqrt(jnp.mean(x1 * x1, axis=-1, keepdims=True) + EPS)
        h = (x1 * r) * n2_ref[...] * (1.0 + sc_ref[0]) + sh_ref[0]
        h_scr[dst, :] = h.astype(BF16)

    def mlp():
        blocks = [slice(b * FF_ROWS, (b + 1) * FF_ROWS) for b in range(MIX_TM // FF_ROWS)]
        wa, wv, wd = (ref[0].astype(BF16) for ref in (wa_ref, wv_ref, wd_ref))
        a = jnp.concatenate([_dot(h_scr[rows, :], wa) for rows in blocks], axis=0)
        val = [_dot(h_scr[rows, :], wv) for rows in blocks]
        pos = lax.broadcasted_iota(jnp.int32, a.shape, 0) % seq
        prev = jnp.where(pos == 0, 0.0, pltpu.roll(a, 1, 0))
        nxt = jnp.where(pos == seq - 1, 0.0, pltpu.roll(a, MIX_TM - 1, 0))
        cw = cw_ref[...]
        ac = prev * cw[0:1] + a * cw[1:2] + nxt * cw[2:3] + cb_ref[...]
        for b, rows in enumerate(blocks):
            act = jax.nn.gelu(ac[rows]) * val[b]
            acc_scr[rows, :] += _dot(act.astype(BF16), wd)

    @pl.when(j == 0)
    def _():
        merge(0)

    @pl.when(jnp.logical_and(j > 0, j < MERGE_STEPS))
    def _():
        norm(j - 1)
        merge(j)

    @pl.when(j == MERGE_STEPS)
    def _():
        norm(MERGE_STEPS - 1)
        mlp()

    @pl.when(j > MERGE_STEPS)
    def _():
        mlp()

    @pl.when(j == pl.num_programs(1) - 1)
    def _():
        o_ref[...] = x1_scr[...] + g2_ref[0] * acc_scr[...]


def _mix(o_att, o_ret, o_four, p_gate, x2d, seq, mod, rows_per_mod, w_att, w_ret, w_four, w_out,
         norm2, w_up, conv_w, conv_b, w_down, layer):
    n_tok = x2d.shape[0]
    mod_idx = lambda i: (i * MIX_TM) // rows_per_mod
    sub = lambda i, j: i * MERGE_STEPS + jnp.minimum(j, MERGE_STEPS - 1)
    ff = lambda j: jnp.maximum(j - MERGE_STEPS, 0)
    br = lambda: pl.BlockSpec((MERGE_TM, 512), lambda i, j: (sub(i, j), 0))
    gate = lambda k: pl.BlockSpec((MERGE_TM, D_MODEL), lambda i, j: (sub(i, j), k))
    modv = lambda k: pl.BlockSpec((1, 1, D_MODEL), lambda i, j: (mod_idx(i), 0, k))
    const = lambda shape: pl.BlockSpec(shape, lambda i, j: (0, 0))
    return pl.pallas_call(
        functools.partial(_mix_kernel, seq=seq),
        grid=(n_tok // MIX_TM, MERGE_STEPS + FF_STEPS),
        in_specs=[
            br(), br(), br(), gate(0), gate(1), gate(2),
            pl.BlockSpec((MERGE_TM, D_MODEL), lambda i, j: (sub(i, j), 0)),
            modv(2), modv(3), modv(4), modv(5),
            const((1, D_MODEL)),
            const((512, D_MODEL)), const((512, D_MODEL)), const((512, D_MODEL)),
            const((D_MODEL, D_MODEL)),
            pl.BlockSpec((1, D_MODEL, FF_TILE), lambda i, j: (layer, 0, ff(j))),
            pl.BlockSpec((1, D_MODEL, FF_TILE), lambda i, j: (layer, 0, FF_STEPS + ff(j))),
            pl.BlockSpec((3, FF_TILE), lambda i, j: (0, ff(j))),
            pl.BlockSpec((1, FF_TILE), lambda i, j: (0, ff(j))),
            pl.BlockSpec((1, FF_TILE, D_MODEL), lambda i, j: (layer, ff(j), 0)),
        ],
        out_specs=pl.BlockSpec((MIX_TM, D_MODEL), lambda i, j: (i, 0)),
        out_shape=jax.ShapeDtypeStruct((n_tok, D_MODEL), F32),
        scratch_shapes=[pltpu.VMEM((MIX_TM, D_MODEL), F32), pltpu.VMEM((MIX_TM, D_MODEL), BF16),
                        pltpu.VMEM((MIX_TM, D_MODEL), F32)],
        compiler_params=_cparams(("parallel", "arbitrary")),
        name="mix",
    )(o_att, o_ret, o_four, p_gate, p_gate, p_gate, x2d, mod, mod, mod, mod, norm2,
      w_att, w_ret, w_four, w_out, w_up, w_up, conv_w, conv_b, w_down)


def _rope_tables(n_tok):
    rows = n_tok // GRID_W
    row_id = jnp.repeat(jnp.arange(rows), GRID_W).astype(F32)
    col_id = jnp.tile(jnp.arange(GRID_W), rows).astype(F32)
    n_freq = HEAD_DIM // 4
    inv = ROPE_THETA ** (-jnp.arange(n_freq, dtype=F32) / n_freq)
    ang = jnp.concatenate([row_id[None, :] * inv[:, None], col_id[None, :] * inv[:, None]], axis=0)
    return jnp.cos(ang), jnp.sin(ang)


def kernel(x_prompt, x_sample, cache_k, cache_v, state_ret_fwd, state_ret_bwd, c, c_ctx, w_ada, b_ada, norm1, w_in, q_norm, k_norm, ret_decay_f, ret_decay_b, ret_norm, w_br_att, w_br_ret, w_br_four, w_out, norm2, w_up, conv_w, conv_b, w_down):
    batch, seq, _ = x_prompt.shape
    dec_batch, dec_seq, _ = x_sample.shape
    past = cache_k.shape[2]

    cond_all = jnp.concatenate([c_ctx[None, :], c], axis=0)
    mod_all = _ada(cond_all, w_ada, b_ada)
    cos_t, sin_t = _rope_tables(dec_seq)
    ck = cache_k.reshape(dec_batch, DEPTH, past, ATT_KV)
    cv = cache_v.reshape(dec_batch, DEPTH, past, ATT_KV)

    xp = x_prompt.reshape(batch * seq, D_MODEL)
    xs = x_sample.reshape(dec_batch * dec_seq, D_MODEL)
    new_kv = (jnp.zeros((batch, DEPTH, seq, ATT_KV), F32),) * 2
    new_st = (jnp.zeros((batch, DEPTH, RET_HEADS, RET_DK, RET_DV), F32),) * 2
    w_main = w_in[:, :, :OFF_GATE].astype(BF16)
    w_gate = w_in[:, :, W_IN_SPLIT:].astype(BF16)
    for l in range(DEPTH):
        w_att, w_ret, w_four = (w_br_att[l].astype(BF16), w_br_ret[l].astype(BF16),
                                w_br_four[l].astype(BF16))
        w_o = w_out[l].astype(BF16)
        n1, n2 = norm1[l][None, :], norm2[l][None, :]
        qn, kn = q_norm[l][:, None], k_norm[l][:, None]
        rn = ret_norm[l][None, :]
        dec = _ret_tables(jnp.broadcast_to(
            jnp.concatenate([ret_decay_f[l], ret_decay_b[l]])[:, None].astype(F32), (2 * RET_HEADS, 128)))
        cw, cb = conv_w[l], conv_b[l][None, :]
        mod_ctx = mod_all[l, 0:1].reshape(1, 1, 6 * D_MODEL)
        mod_lat = mod_all[l, 1:].reshape(dec_batch, 1, 6 * D_MODEL)

        p_act, p_gate = _in_proj(xp, mod_ctx, batch * seq, n1, w_main, w_gate, l)
        o_att, *new_kv = _att_ctx(p_act, seq, qn, kn, l, new_kv)
        o_ret, *new_st = _retention(p_act, seq, dec, rn, l, new_states=new_st)
        o_four = _fourier(p_act, seq)
        xp = _mix(o_att, o_ret, o_four, p_gate, xp, seq, mod_ctx, batch * seq,
                  w_att, w_ret, w_four, w_o, n2, w_up, cw, cb, w_down, l)

        p_act, p_gate = _in_proj(xs, mod_lat, dec_seq, n1, w_main, w_gate, l)
        o_att = _att_lat(p_act, dec_seq, ck, cv, l, qn, kn, cos_t, sin_t)
        o_ret = _retention(p_act, dec_seq, dec, rn, l, states=(state_ret_fwd, state_ret_bwd))
        o_four = _fourier(p_act, dec_seq)
        xs = _mix(o_att, o_ret, o_four, p_gate, xs, dec_seq, mod_lat, dec_seq,
                  w_att, w_ret, w_four, w_o, n2, w_up, cw, cb, w_down, l)

    kv_shape = (batch, DEPTH, seq, N_KV_HEADS, HEAD_DIM)
    return (xp.reshape(batch, seq, D_MODEL), xs.reshape(dec_batch, dec_seq, D_MODEL),
            new_kv[0].reshape(kv_shape), new_kv[1].reshape(kv_shape), new_st[0], new_st[1])
```

```python
import functools

import numpy as np
import jax
import jax.numpy as jnp
from jax import lax
from jax.experimental import pallas as pl
from jax.experimental.pallas import tpu as pltpu

D_MODEL = 1024
DEPTH = 2
GRID_W = 64
HEAD_DIM = 64
N_HEADS = 8
N_KV_HEADS = 2
Q_PER_KV = N_HEADS // N_KV_HEADS
ATT_Q = N_HEADS * HEAD_DIM
ATT_KV = N_KV_HEADS * HEAD_DIM
RET_HEADS = 4
RET_DK = 64
RET_DV = 128
RET_QK = RET_HEADS * RET_DK
RET_V = RET_HEADS * RET_DV
FOURIER_GROUPS = 4
FOURIER_GROUP_DIM = 128
FOURIER_W = FOURIER_GROUPS * FOURIER_GROUP_DIM
D_FF = 2816
CHUNK = 128
Q_BLOCK = 256
ROPE_THETA = 10000.0
EPS = 1e-6
LOG2_E = 1.4426950408889634

F32 = jnp.float32
BF16 = jnp.bfloat16

OFF_QA, OFF_KA, OFF_VA = 0, 512, 640
OFF_QR, OFF_KR, OFF_VR, OFF_GR, OFF_UF = 768, 1024, 1280, 1792, 2304
W_IN_SPLIT = 2816
OFF_GATE = 3072
P_W = OFF_GATE + 3 * D_MODEL

VMEM_LIMIT = 56 * 1024 * 1024


def _cparams(sem, flags=None):
    return pltpu.CompilerParams(dimension_semantics=sem, vmem_limit_bytes=VMEM_LIMIT, flags=flags)


def _dot(a, b):
    return jnp.dot(a, b, preferred_element_type=F32)


def _ada_kernel(cond_ref, w_ref, b_ref, o_ref):
    cnd = cond_ref[...]
    s = cnd * jax.nn.sigmoid(cnd)
    o_ref[0] = _dot(s.astype(BF16), w_ref[0].astype(BF16)) + b_ref[0]


def _ada(cond_all, w_ada, b_ada):
    n = cond_all.shape[0]
    tn = 1024
    return pl.pallas_call(
        _ada_kernel,
        grid=(DEPTH, 6 * D_MODEL // tn),
        in_specs=[
            pl.BlockSpec((n, D_MODEL), lambda l, j: (0, 0)),
            pl.BlockSpec((1, D_MODEL, tn), lambda l, j: (l, 0, j)),
            pl.BlockSpec((1, 1, tn), lambda l, j: (l, 0, j)),
        ],
        out_specs=pl.BlockSpec((1, n, tn), lambda l, j: (l, 0, j)),
        out_shape=jax.ShapeDtypeStruct((DEPTH, n, 6 * D_MODEL), F32),
        compiler_params=_cparams(("parallel", "parallel")),
        name="ada",
    )(cond_all, w_ada, b_ada.reshape(DEPTH, 1, 6 * D_MODEL))


IN_TN = 1024
IN_ROWS = 256
MAIN_STEPS = OFF_GATE // IN_TN


def _in_kernel(x_ref, sh_ref, sc_ref, g_ref, wm_ref, wg_ref, om_ref, og_ref, h_scr):
    j = pl.program_id(1)

    @pl.when(j == 0)
    def _():
        for b in range(x_ref.shape[0] // IN_ROWS):
            rows = slice(b * IN_ROWS, (b + 1) * IN_ROWS)
            x = x_ref[rows, :]
            r = lax.rsqrt(jnp.mean(x * x, axis=-1, keepdims=True) + EPS)
            h = ((x * r) * g_ref[...] * (1.0 + sc_ref[0]) + sh_ref[0]).astype(BF16)
            h_scr[rows, :] = h
            om_ref[rows, :] = _dot(h, wm_ref[0, :, 0:IN_TN])

    @pl.when(jnp.logical_and(j > 0, j < MAIN_STEPS))
    def _():
        cols = pl.ds(pl.multiple_of(j * IN_TN, IN_TN), IN_TN)
        om_ref[...] = _dot(h_scr[...], wm_ref[0, :, cols])

    @pl.when(j >= MAIN_STEPS)
    def _():
        cols = pl.ds(pl.multiple_of((j - MAIN_STEPS) * IN_TN, IN_TN), IN_TN)
        og_ref[...] = (0.5 * _dot(h_scr[...], wg_ref[0, :, cols])).astype(og_ref.dtype)


def _in_proj(x2d, mod, rows_per_mod, norm1, w_main, w_gate, layer):
    n_tok = x2d.shape[0]
    tm, tn = 1024, IN_TN
    mod_idx = lambda i: (i * tm) // rows_per_mod
    return pl.pallas_call(
        _in_kernel,
        grid=(n_tok // tm, P_W // tn),
        in_specs=[
            pl.BlockSpec((tm, D_MODEL), lambda i, j: (i, 0)),
            pl.BlockSpec((1, 1, D_MODEL), lambda i, j: (mod_idx(i), 0, 0)),
            pl.BlockSpec((1, 1, D_MODEL), lambda i, j: (mod_idx(i), 0, 1)),
            pl.BlockSpec((1, D_MODEL), lambda i, j: (0, 0)),
            pl.BlockSpec((1, D_MODEL, OFF_GATE), lambda i, j: (layer, 0, 0)),
            pl.BlockSpec((1, D_MODEL, P_W - OFF_GATE), lambda i, j: (layer, 0, 0)),
        ],
        out_specs=[
            pl.BlockSpec((tm, tn), lambda i, j: (i, jnp.minimum(j, MAIN_STEPS - 1))),
            pl.BlockSpec((tm, tn), lambda i, j: (i, jnp.maximum(j - MAIN_STEPS, 0))),
        ],
        out_shape=[jax.ShapeDtypeStruct((n_tok, OFF_GATE), F32),
                   jax.ShapeDtypeStruct((n_tok, P_W - OFF_GATE), BF16)],
        scratch_shapes=[pltpu.VMEM((tm, D_MODEL), BF16)],
        compiler_params=_cparams(("parallel", "arbitrary")),
        name="in_proj",
    )(x2d, mod, mod, norm1, w_main, w_gate)


def _norm_rope_heads_t(x_t, n_heads, g_col, cos_t=None, sin_t=None):
    quarter = HEAD_DIM // 4
    outs = []
    for h in range(n_heads):
        x = x_t[h * HEAD_DIM:(h + 1) * HEAD_DIM, :]
        r = lax.rsqrt(jnp.mean(x * x, axis=0, keepdims=True) + EPS)
        y = (x * r) * g_col
        if cos_t is not None:
            pieces = []
            for a in range(2):
                c = cos_t[a * quarter:(a + 1) * quarter]
                s = sin_t[a * quarter:(a + 1) * quarter]
                x1 = y[2 * a * quarter:(2 * a + 1) * quarter]
                x2 = y[(2 * a + 1) * quarter:(2 * a + 2) * quarter]
                pieces += [x1 * c - x2 * s, x2 * c + x1 * s]
            y = jnp.concatenate(pieces, axis=0)
        outs.append(y)
    return outs


def _attend_t(q_heads_t, k_bf, v_t_bf):
    tq = q_heads_t[0].shape[1]
    cols = Q_PER_KV * tq
    zeros = jnp.zeros((HEAD_DIM, cols), F32)
    q_kv = [jnp.concatenate(q_heads_t[kv * Q_PER_KV:(kv + 1) * Q_PER_KV], axis=1)
            * (HEAD_DIM ** -0.5 * LOG2_E) for kv in range(N_KV_HEADS)]
    rhs = jnp.concatenate([jnp.concatenate([q_kv[0], zeros], axis=1),
                           jnp.concatenate([zeros, q_kv[1]], axis=1)], axis=0).astype(BF16)
    s_all = _dot(k_bf, rhs)
    out_rows = []
    for kv in range(N_KV_HEADS):
        s_t = s_all[:, kv * cols:(kv + 1) * cols]
        e = jnp.exp2(s_t - jnp.max(s_t, axis=0, keepdims=True))
        inv = 1.0 / jnp.sum(e, axis=0, keepdims=True)
        o_t = _dot(v_t_bf[kv * HEAD_DIM:(kv + 1) * HEAD_DIM, :], e.astype(BF16)) * inv
        out_rows += [o_t[:, g * tq:(g + 1) * tq] for g in range(Q_PER_KV)]
    return jnp.concatenate(out_rows, axis=0)


def _att_ctx_kernel(q_ref, k_ref, v_ref, qn_ref, kn_ref, ck_any, cv_any, o_ref, ko_ref, vo_ref):
    del ck_any, cv_any
    seq = ko_ref.shape[2]
    for sq in range(ko_ref.shape[0]):
        rows = slice(sq * seq, (sq + 1) * seq)
        k_heads = _norm_rope_heads_t(k_ref[rows, :].T, N_KV_HEADS, kn_ref[...])
        k_n = jnp.concatenate(k_heads, axis=0).T
        ko_ref[sq, 0] = k_n
        v = v_ref[rows, :]
        vo_ref[sq, 0] = v
        q_heads = _norm_rope_heads_t(q_ref[rows, :].T, N_HEADS, qn_ref[...])
        o_t = _attend_t(q_heads, k_n.astype(BF16), v.T.astype(BF16))
        o_ref[rows, :] = o_t.T.astype(o_ref.dtype)


ATT_CTX_ROWS = 1024


def _att_ctx(p_act, seq, q_norm, k_norm, layer, caches):
    n_tok = p_act.shape[0]
    batch = n_tok // seq
    rows = max(seq, ATT_CTX_ROWS)
    blk = rows // seq
    cache_spec = pl.BlockSpec((blk, 1, seq, ATT_KV), lambda b: (b, layer, 0, 0))
    cache_shape = jax.ShapeDtypeStruct((batch, DEPTH, seq, ATT_KV), F32)
    return pl.pallas_call(
        _att_ctx_kernel,
        grid=(n_tok // rows,),
        in_specs=[
            pl.BlockSpec((rows, ATT_Q), lambda b: (b, OFF_QA // ATT_Q)),
            pl.BlockSpec((rows, ATT_KV), lambda b: (b, OFF_KA // ATT_KV)),
            pl.BlockSpec((rows, ATT_KV), lambda b: (b, OFF_VA // ATT_KV)),
            pl.BlockSpec((HEAD_DIM, 1), lambda b: (0, 0)),
            pl.BlockSpec((HEAD_DIM, 1), lambda b: (0, 0)),
            pl.BlockSpec(memory_space=pl.ANY),
            pl.BlockSpec(memory_space=pl.ANY),
        ],
        out_specs=[pl.BlockSpec((rows, ATT_Q), lambda b: (b, 0)), cache_spec, cache_spec],
        out_shape=[jax.ShapeDtypeStruct((n_tok, ATT_Q), BF16), cache_shape, cache_shape],
        input_output_aliases={5: 1, 6: 2},
        compiler_params=_cparams(("parallel",)),
        name="att_ctx",
    )(p_act, p_act, p_act, q_norm, k_norm, *caches)


def _att_lat_kernel(q_ref, k_ref, v_ref, ck_ref, cv_ref, qn_ref, kn_ref,
                    cq_ref, sq_ref, ck_t_ref, sk_t_ref, o_ref, kf_scr, vt_scr, *, seq):
    @pl.when(pl.program_id(1) == 0)
    def _():
        k_heads = _norm_rope_heads_t(k_ref[...].T, N_KV_HEADS, kn_ref[...], ck_t_ref[...], sk_t_ref[...])
        kf_scr[0:seq, :] = jnp.concatenate(k_heads, axis=0).T.astype(BF16)
        kf_scr[seq:, :] = ck_ref[0, 0].astype(BF16)
        vt_scr[:, 0:seq] = v_ref[...].T.astype(BF16)
        vt_scr[:, seq:] = cv_ref[0, 0].T.astype(BF16)

    q_heads = _norm_rope_heads_t(q_ref[...].T, N_HEADS, qn_ref[...], cq_ref[...], sq_ref[...])
    o_ref[...] = _attend_t(q_heads, kf_scr[...], vt_scr[...]).T.astype(o_ref.dtype)


def _att_lat(p_act, seq, cache_k, cache_v, layer, q_norm, k_norm, cos_t, sin_t):
    n_tok = p_act.shape[0]
    nb = seq // Q_BLOCK
    past = cache_k.shape[2]
    return pl.pallas_call(
        functools.partial(_att_lat_kernel, seq=seq),
        grid=(n_tok // seq, nb),
        in_specs=[
            pl.BlockSpec((Q_BLOCK, ATT_Q), lambda b, i: (b * nb + i, OFF_QA // ATT_Q)),
            pl.BlockSpec((seq, ATT_KV), lambda b, i: (b, OFF_KA // ATT_KV)),
            pl.BlockSpec((seq, ATT_KV), lambda b, i: (b, OFF_VA // ATT_KV)),
            pl.BlockSpec((1, 1, past, ATT_KV), lambda b, i: (b, layer, 0, 0)),
            pl.BlockSpec((1, 1, past, ATT_KV), lambda b, i: (b, layer, 0, 0)),
            pl.BlockSpec((HEAD_DIM, 1), lambda b, i: (0, 0)),
            pl.BlockSpec((HEAD_DIM, 1), lambda b, i: (0, 0)),
            pl.BlockSpec((HEAD_DIM // 2, Q_BLOCK), lambda b, i: (0, i)),
            pl.BlockSpec((HEAD_DIM // 2, Q_BLOCK), lambda b, i: (0, i)),
            pl.BlockSpec((HEAD_DIM // 2, seq), lambda b, i: (0, 0)),
            pl.BlockSpec((HEAD_DIM // 2, seq), lambda b, i: (0, 0)),
        ],
        out_specs=pl.BlockSpec((Q_BLOCK, ATT_Q), lambda b, i: (b * nb + i, 0)),
        out_shape=jax.ShapeDtypeStruct((n_tok, ATT_Q), BF16),
        scratch_shapes=[pltpu.VMEM((seq + past, ATT_KV), BF16),
                        pltpu.VMEM((ATT_KV, seq + past), BF16)],
        compiler_params=_cparams(("parallel", "arbitrary")),
        name="att_lat",
    )(p_act, p_act, p_act, cache_k, cache_v, q_norm, k_norm, cos_t, sin_t, cos_t, sin_t)


RET_ROWS = 1024


def _log_sigmoid(d):
    return jnp.minimum(d, 0.0) - jnp.log1p(jnp.exp(-jnp.abs(d)))


TAB_MASK, TAB_QF, TAB_QB, TAB_KF, TAB_KB, TAB_C, N_TAB = 0, 2, 4, 6, 7, 8, 9
TAB_CF, TAB_CB = 0, 2


def _ret_tables_kernel(dec_ref, tab_ref):
    hp = pl.program_id(0)
    ii = lax.broadcasted_iota(jnp.int32, (CHUNK, CHUNK), 0)
    jj = lax.broadcasted_iota(jnp.int32, (CHUNK, CHUNK), 1)
    rel = (ii - jj).astype(F32)
    row = ii.astype(F32)
    lane = jj.astype(F32)
    lgf = [_log_sigmoid(dec_ref[pl.ds(2 * hp + t, 1), :]) for t in range(2)]
    lgb = [_log_sigmoid(dec_ref[pl.ds(RET_HEADS + 2 * hp + t, 1), :]) for t in range(2)]
    for t in range(2):
        tab_ref[0, TAB_MASK + t] = jnp.where(
            rel > 0, jnp.exp(jnp.maximum(rel, 0.0) * lgf[t]),
            jnp.where(rel < 0, jnp.exp(jnp.maximum(-rel, 0.0) * lgb[t]), 2.0))
        tab_ref[0, TAB_QF + t] = jnp.exp((row + 1.0) * lgf[t])
        tab_ref[0, TAB_QB + t] = jnp.exp((CHUNK - row) * lgb[t])
    tab_ref[0, TAB_KF] = jnp.exp((CHUNK - 1.0 - lane) * jnp.where(ii < RET_DK, lgf[0], lgf[1]))
    tab_ref[0, TAB_KB] = jnp.exp(lane * jnp.where(ii < RET_DK, lgb[0], lgb[1]))
    c_rows = jnp.where(ii == TAB_CF, lgf[0], jnp.where(ii == TAB_CF + 1, lgf[1],
                       jnp.where(ii == TAB_CB, lgb[0], lgb[1])))
    tab_ref[0, TAB_C] = jnp.exp(CHUNK * c_rows)


def _ret_tables(dec):
    return pl.pallas_call(
        _ret_tables_kernel,
        grid=(RET_HEADS // 2,),
        in_specs=[pl.BlockSpec((8, 128), lambda p: (0, 0))],
        out_specs=pl.BlockSpec((1, N_TAB, CHUNK, CHUNK), lambda p: (p, 0, 0, 0)),
        out_shape=jax.ShapeDtypeStruct((RET_HEADS // 2, N_TAB, CHUNK, CHUNK), F32),
        compiler_params=_cparams(("parallel",)),
        name="ret_tables",
    )(dec)


def _ret_kernel(q_ref, k_ref, v_ref, g_ref, tab_ref, rn_ref, *rest, seq, has_state):
    if has_state:
        s0f_ref, s0b_ref, o_ref = rest
    else:
        o_ref, sf_ref, sb_ref = rest[-3:]
    n_chunks = q_ref.shape[0] // CHUNK
    jj = lax.broadcasted_iota(jnp.int32, (CHUNK, CHUNK), 1)
    mask2 = jnp.concatenate([tab_ref[0, TAB_MASK], tab_ref[0, TAB_MASK + 1]], axis=0)
    qdec_f = [tab_ref[0, TAB_QF + t] for t in range(2)]
    qdec_b = [tab_ref[0, TAB_QB + t] for t in range(2)]
    cdec_f = [tab_ref[0, TAB_C, TAB_CF + t:TAB_CF + t + 1, :] for t in range(2)]
    cdec_b = [tab_ref[0, TAB_C, TAB_CB + t:TAB_CB + t + 1, :] for t in range(2)]
    kdec_f = tab_ref[0, TAB_KF]
    kdec_b = tab_ref[0, TAB_KB]

    k_t = (k_ref[...] * (RET_DK ** -0.5)).T
    first_head = jj < RET_DK

    def chunk(c):
        return slice(c * CHUNK, (c + 1) * CHUNK)

    kv_f = [[None] * n_chunks for _ in range(2)]
    kv_b = [[None] * n_chunks for _ in range(2)]
    for c in range(n_chunks):
        k_c = k_t[:, chunk(c)]
        kd = jnp.concatenate([k_c * kdec_f, k_c * kdec_b], axis=0).astype(BF16)
        kv = _dot(kd, v_ref[chunk(c), :].astype(BF16))
        for t in range(2):
            kv_f[t][c] = kv[t * RET_DK:(t + 1) * RET_DK, t * RET_DV:(t + 1) * RET_DV]
            kv_b[t][c] = kv[CHUNK + t * RET_DK:CHUNK + (t + 1) * RET_DK, t * RET_DV:(t + 1) * RET_DV]

    st_f = [[None] * n_chunks for _ in range(2)]
    st_b = [[None] * n_chunks for _ in range(2)]
    per_seq = seq // CHUNK
    for t in range(2):
        for sq in range(n_chunks // per_seq):
            own = range(sq * per_seq, (sq + 1) * per_seq)
            if has_state:
                s_f = s0f_ref[sq, 0, t]
                s_b = s0b_ref[sq, 0, t]
            else:
                s_f = jnp.zeros((RET_DK, RET_DV), F32)
                s_b = s_f
            for c in own:
                st_f[t][c] = s_f
                s_f = s_f * cdec_f[t] + kv_f[t][c]
            for c in reversed(own):
                st_b[t][c] = s_b
                s_b = s_b * cdec_b[t] + kv_b[t][c]
            if not has_state:
                sf_ref[sq, 0, t] = s_f
                sb_ref[sq, 0, t] = s_b

    for c in range(n_chunks):
        q_c = q_ref[chunk(c), :]
        qm = jnp.concatenate([jnp.where(first_head, q_c, 0.0), jnp.where(first_head, 0.0, q_c)],
                             axis=0).astype(BF16)
        att = (_dot(qm, k_t[:, chunk(c)].astype(BF16)) * mask2).astype(BF16)
        states = jnp.concatenate(
            [jnp.concatenate([st_f[t][c], st_b[t][c]], axis=1) for t in range(2)], axis=0)
        qs = _dot(qm, states.astype(BF16))
        v_c = v_ref[chunk(c), :].astype(BF16)
        for t in range(2):
            rows = slice(t * CHUNK, (t + 1) * CHUNK)
            vsl = slice(t * RET_DV, (t + 1) * RET_DV)
            o = (_dot(att[rows], v_c[:, vsl]) + qs[rows, :RET_DV] * qdec_f[t]
                 + qs[rows, RET_DV:] * qdec_b[t])
            o = (o * lax.rsqrt(jnp.mean(o * o, axis=-1, keepdims=True) + EPS)) * rn_ref[...]
            hg = 0.5 * g_ref[chunk(c), vsl]
            o_ref[chunk(c), vsl] = (o * (hg * (1.0 + jnp.tanh(hg)))).astype(o_ref.dtype)


def _retention(p_act, seq, tables, ret_norm, layer, states=None, new_states=None):
    n_tok = p_act.shape[0]
    n_seq = n_tok // seq
    has_state = states is not None
    pair_qk, pair_v = 2 * RET_DK, 2 * RET_DV
    rows = max(seq, RET_ROWS)
    blk_seqs = rows // seq
    in_specs = [
        pl.BlockSpec((rows, pair_qk), lambda p, s: (s, OFF_QR // pair_qk + p)),
        pl.BlockSpec((rows, pair_qk), lambda p, s: (s, OFF_KR // pair_qk + p)),
        pl.BlockSpec((rows, pair_v), lambda p, s: (s, OFF_VR // pair_v + p)),
        pl.BlockSpec((rows, pair_v), lambda p, s: (s, OFF_GR // pair_v + p)),
        pl.BlockSpec((1, N_TAB, CHUNK, CHUNK), lambda p, s: (p, 0, 0, 0)),
        pl.BlockSpec((1, RET_DV), lambda p, s: (0, 0)),
    ]
    args = [p_act, p_act, p_act, p_act, tables, ret_norm]
    o_spec = pl.BlockSpec((rows, pair_v), lambda p, s: (s, p))
    o_shape = jax.ShapeDtypeStruct((n_tok, RET_V), BF16)
    st_spec = pl.BlockSpec((blk_seqs, 1, 2, RET_DK, RET_DV), lambda p, s: (s, layer, p, 0, 0))
    aliases = {}
    if has_state:
        in_specs += [st_spec, st_spec]
        args += list(states)
        out_specs, out_shape = o_spec, o_shape
    else:
        in_specs += [pl.BlockSpec(memory_space=pl.ANY)] * 2
        aliases = {len(args): 1, len(args) + 1: 2}
        args += list(new_states)
        st_shape = jax.ShapeDtypeStruct((n_seq, DEPTH, RET_HEADS, RET_DK, RET_DV), F32)
        out_specs, out_shape = [o_spec, st_spec, st_spec], [o_shape, st_shape, st_shape]
    return pl.pallas_call(
        functools.partial(_ret_kernel, seq=seq, has_state=has_state),
        grid=(RET_HEADS // 2, n_tok // rows),
        in_specs=in_specs,
        out_specs=out_specs,
        out_shape=out_shape,
        input_output_aliases=aliases,
        compiler_params=_cparams(("parallel", "parallel")),
        name="retention_lat" if has_state else "retention_ctx",
    )(*args)


FOURIER_ROWS = 1024


def _four_kernel(u0_ref, u1_ref, wc_ref, csn_ref, o_ref):
    gd = FOURIER_GROUP_DIM
    tc, ts = [], []
    for g in range(FOURIER_GROUPS):
        u_ref = (u0_ref, u1_ref)[g // 2]
        u_g = u_ref[:, (g % 2) * gd:(g % 2 + 1) * gd].astype(BF16)
        t = _dot(u_g, wc_ref[...])
        tc.append(t[:, :gd])
        ts.append(t[:, gd:])
    tc = jnp.concatenate(tc, axis=1).astype(BF16)
    ts = jnp.concatenate(ts, axis=1).astype(BF16)
    seq = csn_ref.shape[0]
    for sq in range(o_ref.shape[0] // seq):
        rows = slice(sq * seq, (sq + 1) * seq)
        t_sq = jnp.concatenate([tc[rows], ts[rows]], axis=0)
        o_ref[rows, :] = _dot(csn_ref[...], t_sq).astype(o_ref.dtype)


def _dft_tables(n):
    k = np.arange(n, dtype=np.int64)
    ang = 2.0 * np.pi * ((k[:, None] * k[None, :]) % n).astype(np.float64) / n
    scale = 1.0 / np.sqrt(n)
    return np.cos(ang) * scale, np.sin(ang) * scale


def _fourier(p_act, seq):
    n_tok = p_act.shape[0]
    cc, sc = _dft_tables(FOURIER_GROUP_DIM)
    wc = jnp.asarray(np.concatenate([cc, -sc], axis=1), F32).astype(BF16)
    cn, sn = _dft_tables(seq)
    csn = jnp.asarray(np.concatenate([cn, sn], axis=1), F32).astype(BF16)
    const = lambda shape: pl.BlockSpec(shape, lambda s: (0, 0))
    half = FOURIER_W // 2
    rows = max(seq, FOURIER_ROWS)
    return pl.pallas_call(
        _four_kernel,
        grid=(n_tok // rows,),
        in_specs=[
            pl.BlockSpec((rows, half), lambda s: (s, OFF_UF // half)),
            pl.BlockSpec((rows, half), lambda s: (s, OFF_UF // half + 1)),
            const((FOURIER_GROUP_DIM, 2 * FOURIER_GROUP_DIM)),
            const((seq, 2 * seq)),
        ],
        out_specs=pl.BlockSpec((rows, FOURIER_W), lambda s: (s, 0)),
        out_shape=jax.ShapeDtypeStruct((n_tok, FOURIER_W), BF16),
        compiler_params=_cparams(("parallel",)),
        name="fourier",
    )(p_act, p_act, wc, csn)


MIX_TM = 1024
MERGE_STEPS = 4
MERGE_TM = MIX_TM // MERGE_STEPS
FF_TILE = 256
FF_STEPS = D_FF // FF_TILE
FF_ROWS = 256


def _mix_kernel(oa_ref, or_ref, of_ref, ga_ref, gr_ref, gf_ref, x_ref, g1_ref, sh_ref, sc_ref,
                g2_ref, n2_ref, wba_ref, wbr_ref, wbf_ref, wo_ref, wa_ref, wv_ref, cw_ref, cb_ref,
                wd_ref, o_ref, x1_scr, h_scr, acc_scr, *, seq):
    j = pl.program_id(1)

    def merge(step):
        gate = lambda ref: jnp.tanh(ref[...].astype(F32)) + 1.0
        dst = pl.ds(pl.multiple_of(step * MERGE_TM, MERGE_TM), MERGE_TM)
        merged = (gate(ga_ref) * _dot(oa_ref[...], wba_ref[...])
                  + gate(gr_ref) * _dot(or_ref[...], wbr_ref[...])
                  + gate(gf_ref) * _dot(of_ref[...], wbf_ref[...]))
        x1_scr[dst, :] = x_ref[...] + g1_ref[0] * _dot((0.5 * merged).astype(BF16), wo_ref[...])
        acc_scr[dst, :] = jnp.zeros((MERGE_TM, D_MODEL), F32)

    def norm(step):
        dst = pl.ds(pl.multiple_of(step * MERGE_TM, MERGE_TM), MERGE_TM)
        x1 = x1_scr[dst, :]
        r = lax.rsqrt(jnp.mean(x1 * x1, axis=-1, keepdims=True) + EPS)
        h = (x1 * r) * n2_ref[...] * (1.0 + sc_ref[0]) + sh_ref[0]
        h_scr[dst, :] = h.astype(BF16)

    def mlp():
        blocks = [slice(b * FF_ROWS, (b + 1) * FF_ROWS) for b in range(MIX_TM // FF_ROWS)]
        wa, wv, wd = (ref[0].astype(BF16) for ref in (wa_ref, wv_ref, wd_ref))
        a = jnp.concatenate([_dot(h_scr[rows, :], wa) for rows in blocks], axis=0)
        val = [_dot(h_scr[rows, :], wv) for rows in blocks]
        pos = lax.broadcasted_iota(jnp.int32, a.shape, 0) % seq
        prev = jnp.where(pos == 0, 0.0, pltpu.roll(a, 1, 0))
        nxt = jnp.where(pos == seq - 1, 0.0, pltpu.roll(a, MIX_TM - 1, 0))
        cw = cw_ref[...]
        ac = prev * cw[0:1] + a * cw[1:2] + nxt * cw[2:3] + cb_ref[...]
        for b, rows in enumerate(blocks):
            act = jax.nn.gelu(ac[rows]) * val[b]
            acc_scr[rows, :] += _dot(act.astype(BF16), wd)

    @pl.when(j == 0)
    def _():
        merge(0)

    @pl.when(jnp.logical_and(j > 0, j < MERGE_STEPS))
    def _():
        norm(j - 1)
        merge(j)

    @pl.when(j == MERGE_STEPS)
    def _():
        norm(MERGE_STEPS - 1)
        mlp()

    @pl.when(j > MERGE_STEPS)
    def _():
        mlp()

    @pl.when(j == pl.num_programs(1) - 1)
    def _():
        o_ref[...] = x1_scr[...] + g2_ref[0] * acc_scr[...]


def _mix(o_att, o_ret, o_four, p_gate, x2d, seq, mod, rows_per_mod, w_att, w_ret, w_four, w_out,
         norm2, w_up, conv_w, conv_b, w_down, layer):
    n_tok = x2d.shape[0]
    mod_idx = lambda i: (i * MIX_TM) // rows_per_mod
    sub = lambda i, j: i * MERGE_STEPS + jnp.minimum(j, MERGE_STEPS - 1)
    ff = lambda j: jnp.maximum(j - MERGE_STEPS, 0)
    br = lambda: pl.BlockSpec((MERGE_TM, 512), lambda i, j: (sub(i, j), 0))
    gate = lambda k: pl.BlockSpec((MERGE_TM, D_MODEL), lambda i, j: (sub(i, j), k))
    modv = lambda k: pl.BlockSpec((1, 1, D_MODEL), lambda i, j: (mod_idx(i), 0, k))
    const = lambda shape: pl.BlockSpec(shape, lambda i, j: (0, 0))
    return pl.pallas_call(
        functools.partial(_mix_kernel, seq=seq),
        grid=(n_tok // MIX_TM, MERGE_STEPS + FF_STEPS),
        in_specs=[
            br(), br(), br(), gate(0), gate(1), gate(2),
            pl.BlockSpec((MERGE_TM, D_MODEL), lambda i, j: (sub(i, j), 0)),
            modv(2), modv(3), modv(4), modv(5),
            const((1, D_MODEL)),
            const((512, D_MODEL)), const((512, D_MODEL)), const((512, D_MODEL)),
            const((D_MODEL, D_MODEL)),
            pl.BlockSpec((1, D_MODEL, FF_TILE), lambda i, j: (layer, 0, ff(j))),
            pl.BlockSpec((1, D_MODEL, FF_TILE), lambda i, j: (layer, 0, FF_STEPS + ff(j))),
            pl.BlockSpec((3, FF_TILE), lambda i, j: (0, ff(j))),
            pl.BlockSpec((1, FF_TILE), lambda i, j: (0, ff(j))),
            pl.BlockSpec((1, FF_TILE, D_MODEL), lambda i, j: (layer, ff(j), 0)),
        ],
        out_specs=pl.BlockSpec((MIX_TM, D_MODEL), lambda i, j: (i, 0)),
        out_shape=jax.ShapeDtypeStruct((n_tok, D_MODEL), F32),
        scratch_shapes=[pltpu.VMEM((MIX_TM, D_MODEL), F32), pltpu.VMEM((MIX_TM, D_MODEL), BF16),
                        pltpu.VMEM((MIX_TM, D_MODEL), F32)],
        compiler_params=_cparams(("parallel", "arbitrary")),
        name="mix",
    )(o_att, o_ret, o_four, p_gate, p_gate, p_gate, x2d, mod, mod, mod, mod, norm2,
      w_att, w_ret, w_four, w_out, w_up, w_up, conv_w, conv_b, w_down)


def _rope_tables(n_tok):
    rows = n_tok // GRID_W
    row_id = jnp.repeat(jnp.arange(rows), GRID_W).astype(F32)
    col_id = jnp.tile(jnp.arange(GRID_W), rows).astype(F32)
    n_freq = HEAD_DIM // 4
    inv = ROPE_THETA ** (-jnp.arange(n_freq, dtype=F32) / n_freq)
    ang = jnp.concatenate([row_id[None, :] * inv[:, None], col_id[None, :] * inv[:, None]], axis=0)
    return jnp.cos(ang), jnp.sin(ang)


def kernel(x_prompt, x_sample, cache_k, cache_v, state_ret_fwd, state_ret_bwd, c, c_ctx, w_ada, b_ada, norm1, w_in, q_norm, k_norm, ret_decay_f, ret_decay_b, ret_norm, w_br_att, w_br_ret, w_br_four, w_out, norm2, w_up, conv_w, conv_b, w_down):
    batch, seq, _ = x_prompt.shape
    dec_batch, dec_seq, _ = x_sample.shape
    past = cache_k.shape[2]

    cond_all = jnp.concatenate([c_ctx[None, :], c], axis=0)
    mod_all = _ada(cond_all, w_ada, b_ada)
    cos_t, sin_t = _rope_tables(dec_seq)
    ck = cache_k.reshape(dec_batch, DEPTH, past, ATT_KV)
    cv = cache_v.reshape(dec_batch, DEPTH, past, ATT_KV)

    xp = x_prompt.reshape(batch * seq, D_MODEL)
    xs = x_sample.reshape(dec_batch * dec_seq, D_MODEL)
    new_kv = (jnp.zeros((batch, DEPTH, seq, ATT_KV), F32),) * 2
    new_st = (jnp.zeros((batch, DEPTH, RET_HEADS, RET_DK, RET_DV), F32),) * 2
    w_main = w_in[:, :, :OFF_GATE].astype(BF16)
    w_gate = w_in[:, :, W_IN_SPLIT:].astype(BF16)
    for l in range(DEPTH):
        w_att, w_ret, w_four = (w_br_att[l].astype(BF16), w_br_ret[l].astype(BF16),
                                w_br_four[l].astype(BF16))
        w_o = w_out[l].astype(BF16)
        n1, n2 = norm1[l][None, :], norm2[l][None, :]
        qn, kn = q_norm[l][:, None], k_norm[l][:, None]
        rn = ret_norm[l][None, :]
        dec = _ret_tables(jnp.broadcast_to(
            jnp.concatenate([ret_decay_f[l], ret_decay_b[l]])[:, None].astype(F32), (2 * RET_HEADS, 128)))
        cw, cb = conv_w[l], conv_b[l][None, :]
        mod_ctx = mod_all[l, 0:1].reshape(1, 1, 6 * D_MODEL)
        mod_lat = mod_all[l, 1:].reshape(dec_batch, 1, 6 * D_MODEL)

        p_act, p_gate = _in_proj(xp, mod_ctx, batch * seq, n1, w_main, w_gate, l)
        o_att, *new_kv = _att_ctx(p_act, seq, qn, kn, l, new_kv)
        o_ret, *new_st = _retention(p_act, seq, dec, rn, l, new_states=new_st)
        o_four = _fourier(p_act, seq)
        xp = _mix(o_att, o_ret, o_four, p_gate, xp, seq, mod_ctx, batch * seq,
                  w_att, w_ret, w_four, w_o, n2, w_up, cw, cb, w_down, l)

        p_act, p_gate = _in_proj(xs, mod_lat, dec_seq, n1, w_main, w_gate, l)
        o_att = _att_lat(p_act, dec_seq, ck, cv, l, qn, kn, cos_t, sin_t)
        o_ret = _retention(p_act, dec_seq, dec, rn, l, states=(state_ret_fwd, state_ret_bwd))
        o_four = _fourier(p_act, dec_seq)
        xs = _mix(o_att, o_ret, o_four, p_gate, xs, dec_seq, mod_lat, dec_seq,
                  w_att, w_ret, w_four, w_o, n2, w_up, cw, cb, w_down, l)

    kv_shape = (batch, DEPTH, seq, N_KV_HEADS, HEAD_DIM)
    return (xp.reshape(batch, seq, D_MODEL), xs.reshape(dec_batch, dec_seq, D_MODEL),
            new_kv[0].reshape(kv_shape), new_kv[1].reshape(kv_shape), new_st[0], new_st[1])
```

```python
import functools

import numpy as np
import jax
import jax.numpy as jnp
from jax import lax
from jax.experimental import pallas as pl
from jax.experimental.pallas import tpu as pltpu

D_MODEL = 1024
DEPTH = 2
GRID_W = 64
HEAD_DIM = 64
N_HEADS = 8
N_KV_HEADS = 2
Q_PER_KV = N_HEADS // N_KV_HEADS
ATT_Q = N_HEADS * HEAD_DIM
ATT_KV = N_KV_HEADS * HEAD_DIM
RET_HEADS = 4
RET_DK = 64
RET_DV = 128
RET_QK = RET_HEADS * RET_DK
RET_V = RET_HEADS * RET_DV
FOURIER_GROUPS = 4
FOURIER_GROUP_DIM = 128
FOURIER_W = FOURIER_GROUPS * FOURIER_GROUP_DIM
D_FF = 2816
CHUNK = 128
Q_BLOCK = 256
ROPE_THETA = 10000.0
EPS = 1e-6
LOG2_E = 1.4426950408889634

F32 = jnp.float32
BF16 = jnp.bfloat16

OFF_QA, OFF_KA, OFF_VA = 0, 512, 640
OFF_QR, OFF_KR, OFF_VR, OFF_GR, OFF_UF = 768, 1024, 1280, 1792, 2304
W_IN_SPLIT = 2816
OFF_GATE = 3072
P_W = OFF_GATE + 3 * D_MODEL

VMEM_LIMIT = 56 * 1024 * 1024


def _cparams(sem, flags=None):
    return pltpu.CompilerParams(dimension_semantics=sem, vmem_limit_bytes=VMEM_LIMIT, flags=flags)


def _dot(a, b):
    return jnp.dot(a, b, preferred_element_type=F32)


def _ada_kernel(cond_ref, w_ref, b_ref, o_ref):
    cnd = cond_ref[...]
    s = cnd * jax.nn.sigmoid(cnd)
    o_ref[0] = _dot(s.astype(BF16), w_ref[0].astype(BF16)) + b_ref[0]


def _ada(cond_all, w_ada, b_ada):
    n = cond_all.shape[0]
    tn = 1024
    return pl.pallas_call(
        _ada_kernel,
        grid=(DEPTH, 6 * D_MODEL // tn),
        in_specs=[
            pl.BlockSpec((n, D_MODEL), lambda l, j: (0, 0)),
            pl.BlockSpec((1, D_MODEL, tn), lambda l, j: (l, 0, j)),
            pl.BlockSpec((1, 1, tn), lambda l, j: (l, 0, j)),
        ],
        out_specs=pl.BlockSpec((1, n, tn), lambda l, j: (l, 0, j)),
        out_shape=jax.ShapeDtypeStruct((DEPTH, n, 6 * D_MODEL), F32),
        compiler_params=_cparams(("parallel", "parallel")),
        name="ada",
    )(cond_all, w_ada, b_ada.reshape(DEPTH, 1, 6 * D_MODEL))


IN_TN = 1024
IN_ROWS = 256
MAIN_STEPS = OFF_GATE // IN_TN


def _in_kernel(x_ref, sh_ref, sc_ref, g_ref, w_ref, om_ref, og_ref, h_scr, w_scr):
    i = pl.program_id(0)
    j = pl.program_id(1)
    cols = pl.ds(pl.multiple_of(j * IN_TN, IN_TN), IN_TN)

    @pl.when(i == 0)
    def _():
        w_scr[:, cols] = w_ref[0].astype(BF16)

    @pl.when(j == 0)
    def _():
        for b in range(x_ref.shape[0] // IN_ROWS):
            rows = slice(b * IN_ROWS, (b + 1) * IN_ROWS)
            x = x_ref[rows, :]
            r = lax.rsqrt(jnp.mean(x * x, axis=-1, keepdims=True) + EPS)
            h = ((x * r) * g_ref[...] * (1.0 + sc_ref[0]) + sh_ref[0]).astype(BF16)
            h_scr[rows, :] = h
            om_ref[rows, :] = _dot(h, w_scr[:, 0:IN_TN])

    @pl.when(jnp.logical_and(j > 0, j < MAIN_STEPS))
    def _():
        om_ref[...] = _dot(h_scr[...], w_scr[:, cols])

    @pl.when(j >= MAIN_STEPS)
    def _():
        og_ref[...] = (0.5 * _dot(h_scr[...], w_scr[:, cols])).astype(og_ref.dtype)


def _in_proj(x2d, mod, rows_per_mod, norm1, w_in, layer):
    n_tok = x2d.shape[0]
    tm, tn = 1024, IN_TN
    n_col = P_W // tn
    mod_idx = lambda i: (i * tm) // rows_per_mod

    def w_col(i, j):
        jj = jnp.where(i == 0, j, n_col - 1)
        col = jnp.where(jj < MAIN_STEPS, jj * tn, W_IN_SPLIT + (jj - MAIN_STEPS) * tn)
        return pl.multiple_of(col, 128)

    return pl.pallas_call(
        _in_kernel,
        grid=(n_tok // tm, n_col),
        in_specs=[
            pl.BlockSpec((tm, D_MODEL), lambda i, j: (i, 0)),
            pl.BlockSpec((1, 1, D_MODEL), lambda i, j: (mod_idx(i), 0, 0)),
            pl.BlockSpec((1, 1, D_MODEL), lambda i, j: (mod_idx(i), 0, 1)),
            pl.BlockSpec((1, D_MODEL), lambda i, j: (0, 0)),
            pl.BlockSpec((pl.Element(1), pl.Element(D_MODEL), pl.Element(tn)),
                         lambda i, j: (layer, 0, w_col(i, j))),
        ],
        out_specs=[
            pl.BlockSpec((tm, tn), lambda i, j: (i, jnp.minimum(j, MAIN_STEPS - 1))),
            pl.BlockSpec((tm, tn), lambda i, j: (i, jnp.maximum(j - MAIN_STEPS, 0))),
        ],
        out_shape=[jax.ShapeDtypeStruct((n_tok, OFF_GATE), F32),
                   jax.ShapeDtypeStruct((n_tok, P_W - OFF_GATE), BF16)],
        scratch_shapes=[pltpu.VMEM((tm, D_MODEL), BF16), pltpu.VMEM((D_MODEL, P_W), BF16)],
        compiler_params=_cparams(("arbitrary", "arbitrary")),
        name="in_proj",
    )(x2d, mod, mod, norm1, w_in)


def _norm_rope_heads_t(x_t, n_heads, g_col, cos_t=None, sin_t=None):
    quarter = HEAD_DIM // 4
    outs = []
    for h in range(n_heads):
        x = x_t[h * HEAD_DIM:(h + 1) * HEAD_DIM, :]
        r = lax.rsqrt(jnp.mean(x * x, axis=0, keepdims=True) + EPS)
        y = (x * r) * g_col
        if cos_t is not None:
            pieces = []
            for a in range(2):
                c = cos_t[a * quarter:(a + 1) * quarter]
                s = sin_t[a * quarter:(a + 1) * quarter]
                x1 = y[2 * a * quarter:(2 * a + 1) * quarter]
                x2 = y[(2 * a + 1) * quarter:(2 * a + 2) * quarter]
                pieces += [x1 * c - x2 * s, x2 * c + x1 * s]
            y = jnp.concatenate(pieces, axis=0)
        outs.append(y)
    return outs


def _attend_t(q_heads_t, k_bf, v_t_bf):
    tq = q_heads_t[0].shape[1]
    cols = Q_PER_KV * tq
    zeros = jnp.zeros((HEAD_DIM, cols), F32)
    q_kv = [jnp.concatenate(q_heads_t[kv * Q_PER_KV:(kv + 1) * Q_PER_KV], axis=1)
            * (HEAD_DIM ** -0.5 * LOG2_E) for kv in range(N_KV_HEADS)]
    rhs = jnp.concatenate([jnp.concatenate([q_kv[0], zeros], axis=1),
                           jnp.concatenate([zeros, q_kv[1]], axis=1)], axis=0).astype(BF16)
    s_all = _dot(k_bf, rhs)
    out_rows = []
    for kv in range(N_KV_HEADS):
        s_t = s_all[:, kv * cols:(kv + 1) * cols]
        e = jnp.exp2(s_t - jnp.max(s_t, axis=0, keepdims=True))
        inv = 1.0 / jnp.sum(e, axis=0, keepdims=True)
        o_t = _dot(v_t_bf[kv * HEAD_DIM:(kv + 1) * HEAD_DIM, :], e.astype(BF16)) * inv
        out_rows += [o_t[:, g * tq:(g + 1) * tq] for g in range(Q_PER_KV)]
    return jnp.concatenate(out_rows, axis=0)


def _att_ctx_kernel(q_ref, k_ref, v_ref, qn_ref, kn_ref, ck_any, cv_any, o_ref, ko_ref, vo_ref):
    del ck_any, cv_any
    seq = ko_ref.shape[2]
    for sq in range(ko_ref.shape[0]):
        rows = slice(sq * seq, (sq + 1) * seq)
        k_heads = _norm_rope_heads_t(k_ref[rows, :].T, N_KV_HEADS, kn_ref[...])
        k_n = jnp.concatenate(k_heads, axis=0).T
        ko_ref[sq, 0] = k_n
        v = v_ref[rows, :]
        vo_ref[sq, 0] = v
        q_heads = _norm_rope_heads_t(q_ref[rows, :].T, N_HEADS, qn_ref[...])
        o_t = _attend_t(q_heads, k_n.astype(BF16), v.T.astype(BF16))
        o_ref[rows, :] = o_t.T.astype(o_ref.dtype)


ATT_CTX_ROWS = 1024


def _att_ctx(p_act, seq, q_norm, k_norm, layer, caches):
    n_tok = p_act.shape[0]
    batch = n_tok // seq
    rows = max(seq, ATT_CTX_ROWS)
    blk = rows // seq
    cache_spec = pl.BlockSpec((blk, 1, seq, ATT_KV), lambda b: (b, layer, 0, 0))
    cache_shape = jax.ShapeDtypeStruct((batch, DEPTH, seq, ATT_KV), F32)
    return pl.pallas_call(
        _att_ctx_kernel,
        grid=(n_tok // rows,),
        in_specs=[
            pl.BlockSpec((rows, ATT_Q), lambda b: (b, OFF_QA // ATT_Q)),
            pl.BlockSpec((rows, ATT_KV), lambda b: (b, OFF_KA // ATT_KV)),
            pl.BlockSpec((rows, ATT_KV), lambda b: (b, OFF_VA // ATT_KV)),
            pl.BlockSpec((HEAD_DIM, 1), lambda b: (0, 0)),
            pl.BlockSpec((HEAD_DIM, 1), lambda b: (0, 0)),
            pl.BlockSpec(memory_space=pl.ANY),
            pl.BlockSpec(memory_space=pl.ANY),
        ],
        out_specs=[pl.BlockSpec((rows, ATT_Q), lambda b: (b, 0)), cache_spec, cache_spec],
        out_shape=[jax.ShapeDtypeStruct((n_tok, ATT_Q), BF16), cache_shape, cache_shape],
        input_output_aliases={5: 1, 6: 2},
        compiler_params=_cparams(("parallel",)),
        name="att_ctx",
    )(p_act, p_act, p_act, q_norm, k_norm, *caches)


def _att_lat_kernel(q_ref, k_ref, v_ref, ck_ref, cv_ref, qn_ref, kn_ref,
                    cq_ref, sq_ref, ck_t_ref, sk_t_ref, o_ref, kf_scr, vt_scr, *, seq):
    @pl.when(pl.program_id(1) == 0)
    def _():
        k_heads = _norm_rope_heads_t(k_ref[...].T, N_KV_HEADS, kn_ref[...], ck_t_ref[...], sk_t_ref[...])
        kf_scr[0:seq, :] = jnp.concatenate(k_heads, axis=0).T.astype(BF16)
        kf_scr[seq:, :] = ck_ref[0, 0].astype(BF16)
        vt_scr[:, 0:seq] = v_ref[...].T.astype(BF16)
        vt_scr[:, seq:] = cv_ref[0, 0].T.astype(BF16)

    q_heads = _norm_rope_heads_t(q_ref[...].T, N_HEADS, qn_ref[...], cq_ref[...], sq_ref[...])
    o_ref[...] = _attend_t(q_heads, kf_scr[...], vt_scr[...]).T.astype(o_ref.dtype)


def _att_lat(p_act, seq, cache_k, cache_v, layer, q_norm, k_norm, cos_t, sin_t):
    n_tok = p_act.shape[0]
    nb = seq // Q_BLOCK
    past = cache_k.shape[2]
    return pl.pallas_call(
        functools.partial(_att_lat_kernel, seq=seq),
        grid=(n_tok // seq, nb),
        in_specs=[
            pl.BlockSpec((Q_BLOCK, ATT_Q), lambda b, i: (b * nb + i, OFF_QA // ATT_Q)),
            pl.BlockSpec((seq, ATT_KV), lambda b, i: (b, OFF_KA // ATT_KV)),
            pl.BlockSpec((seq, ATT_KV), lambda b, i: (b, OFF_VA // ATT_KV)),
            pl.BlockSpec((1, 1, past, ATT_KV), lambda b, i: (b, layer, 0, 0)),
            pl.BlockSpec((1, 1, past, ATT_KV), lambda b, i: (b, layer, 0, 0)),
            pl.BlockSpec((HEAD_DIM, 1), lambda b, i: (0, 0)),
            pl.BlockSpec((HEAD_DIM, 1), lambda b, i: (0, 0)),
            pl.BlockSpec((HEAD_DIM // 2, Q_BLOCK), lambda b, i: (0, i)),
            pl.BlockSpec((HEAD_DIM // 2, Q_BLOCK), lambda b, i: (0, i)),
            pl.BlockSpec((HEAD_DIM // 2, seq), lambda b, i: (0, 0)),
            pl.BlockSpec((HEAD_DIM // 2, seq), lambda b, i: (0, 0)),
        ],
        out_specs=pl.BlockSpec((Q_BLOCK, ATT_Q), lambda b, i: (b * nb + i, 0)),
        out_shape=jax.ShapeDtypeStruct((n_tok, ATT_Q), BF16),
        scratch_shapes=[pltpu.VMEM((seq + past, ATT_KV), BF16),
                        pltpu.VMEM((ATT_KV, seq + past), BF16)],
        compiler_params=_cparams(("parallel", "arbitrary")),
        name="att_lat",
    )(p_act, p_act, p_act, cache_k, cache_v, q_norm, k_norm, cos_t, sin_t, cos_t, sin_t)


RET_ROWS = 1024


def _log_sigmoid(d):
    return jnp.minimum(d, 0.0) - jnp.log1p(jnp.exp(-jnp.abs(d)))


TAB_MASK, TAB_QF, TAB_QB, TAB_KF, TAB_KB, TAB_C, N_TAB = 0, 2, 4, 6, 7, 8, 9
TAB_CF, TAB_CB = 0, 2


def _ret_tables_kernel(dec_ref, tab_ref):
    hp = pl.program_id(0)
    ii = lax.broadcasted_iota(jnp.int32, (CHUNK, CHUNK), 0)
    jj = lax.broadcasted_iota(jnp.int32, (CHUNK, CHUNK), 1)
    rel = (ii - jj).astype(F32)
    row = ii.astype(F32)
    lane = jj.astype(F32)
    lgf = [_log_sigmoid(dec_ref[pl.ds(2 * hp + t, 1), :]) for t in range(2)]
    lgb = [_log_sigmoid(dec_ref[pl.ds(RET_HEADS + 2 * hp + t, 1), :]) for t in range(2)]
    for t in range(2):
        tab_ref[0, TAB_MASK + t] = jnp.where(
            rel > 0, jnp.exp(jnp.maximum(rel, 0.0) * lgf[t]),
            jnp.where(rel < 0, jnp.exp(jnp.maximum(-rel, 0.0) * lgb[t]), 2.0))
        tab_ref[0, TAB_QF + t] = jnp.exp((row + 1.0) * lgf[t])
        tab_ref[0, TAB_QB + t] = jnp.exp((CHUNK - row) * lgb[t])
    tab_ref[0, TAB_KF] = jnp.exp((CHUNK - 1.0 - lane) * jnp.where(ii < RET_DK, lgf[0], lgf[1]))
    tab_ref[0, TAB_KB] = jnp.exp(lane * jnp.where(ii < RET_DK, lgb[0], lgb[1]))
    c_rows = jnp.where(ii == TAB_CF, lgf[0], jnp.where(ii == TAB_CF + 1, lgf[1],
                       jnp.where(ii == TAB_CB, lgb[0], lgb[1])))
    tab_ref[0, TAB_C] = jnp.exp(CHUNK * c_rows)


def _ret_tables(dec):
    return pl.pallas_call(
        _ret_tables_kernel,
        grid=(RET_HEADS // 2,),
        in_specs=[pl.BlockSpec((8, 128), lambda p: (0, 0))],
        out_specs=pl.BlockSpec((1, N_TAB, CHUNK, CHUNK), lambda p: (p, 0, 0, 0)),
        out_shape=jax.ShapeDtypeStruct((RET_HEADS // 2, N_TAB, CHUNK, CHUNK), F32),
        compiler_params=_cparams(("parallel",)),
        name="ret_tables",
    )(dec)


def _ret_kernel(q_ref, k_ref, v_ref, g_ref, tab_ref, rn_ref, *rest, seq, has_state):
    if has_state:
        s0f_ref, s0b_ref, o_ref = rest
    else:
        o_ref, sf_ref, sb_ref = rest[-3:]
    n_chunks = q_ref.shape[0] // CHUNK
    jj = lax.broadcasted_iota(jnp.int32, (CHUNK, CHUNK), 1)
    mask2 = jnp.concatenate([tab_ref[0, TAB_MASK], tab_ref[0, TAB_MASK + 1]], axis=0)
    qdec_f = [tab_ref[0, TAB_QF + t] for t in range(2)]
    qdec_b = [tab_ref[0, TAB_QB + t] for t in range(2)]
    cdec_f = [tab_ref[0, TAB_C, TAB_CF + t:TAB_CF + t + 1, :] for t in range(2)]
    cdec_b = [tab_ref[0, TAB_C, TAB_CB + t:TAB_CB + t + 1, :] for t in range(2)]
    kdec_f = tab_ref[0, TAB_KF]
    kdec_b = tab_ref[0, TAB_KB]

    k_t = (k_ref[...] * (RET_DK ** -0.5)).T
    first_head = jj < RET_DK

    def chunk(c):
        return slice(c * CHUNK, (c + 1) * CHUNK)

    kv_f = [[None] * n_chunks for _ in range(2)]
    kv_b = [[None] * n_chunks for _ in range(2)]
    for c in range(n_chunks):
        k_c = k_t[:, chunk(c)]
        kd = jnp.concatenate([k_c * kdec_f, k_c * kdec_b], axis=0).astype(BF16)
        kv = _dot(kd, v_ref[chunk(c), :].astype(BF16))
        for t in range(2):
            kv_f[t][c] = kv[t * RET_DK:(t + 1) * RET_DK, t * RET_DV:(t + 1) * RET_DV]
            kv_b[t][c] = kv[CHUNK + t * RET_DK:CHUNK + (t + 1) * RET_DK, t * RET_DV:(t + 1) * RET_DV]

    st_f = [[None] * n_chunks for _ in range(2)]
    st_b = [[None] * n_chunks for _ in range(2)]
    per_seq = seq // CHUNK
    for t in range(2):
        for sq in range(n_chunks // per_seq):
            own = range(sq * per_seq, (sq + 1) * per_seq)
            if has_state:
                s_f = s0f_ref[sq, 0, t]
                s_b = s0b_ref[sq, 0, t]
            else:
                s_f = jnp.zeros((RET_DK, RET_DV), F32)
                s_b = s_f
            for c in own:
                st_f[t][c] = s_f
                s_f = s_f * cdec_f[t] + kv_f[t][c]
            for c in reversed(own):
                st_b[t][c] = s_b
                s_b = s_b * cdec_b[t] + kv_b[t][c]
            if not has_state:
                sf_ref[sq, 0, t] = s_f
                sb_ref[sq, 0, t] = s_b

    for c in range(n_chunks):
        q_c = q_ref[chunk(c), :]
        qm = jnp.concatenate([jnp.where(first_head, q_c, 0.0), jnp.where(first_head, 0.0, q_c)],
                             axis=0).astype(BF16)
        att = (_dot(qm, k_t[:, chunk(c)].astype(BF16)) * mask2).astype(BF16)
        states = jnp.concatenate(
            [jnp.concatenate([st_f[t][c], st_b[t][c]], axis=1) for t in range(2)], axis=0)
        qs = _dot(qm, states.astype(BF16))
        v_c = v_ref[chunk(c), :].astype(BF16)
        for t in range(2):
            rows = slice(t * CHUNK, (t + 1) * CHUNK)
            vsl = slice(t * RET_DV, (t + 1) * RET_DV)
            o = (_dot(att[rows], v_c[:, vsl]) + qs[rows, :RET_DV] * qdec_f[t]
                 + qs[rows, RET_DV:] * qdec_b[t])
            o = (o * lax.rsqrt(jnp.mean(o * o, axis=-1, keepdims=True) + EPS)) * rn_ref[...]
            hg = 0.5 * g_ref[chunk(c), vsl]
            o_ref[chunk(c), vsl] = (o * (hg * (1.0 + jnp.tanh(hg)))).astype(o_ref.dtype)


def _retention(p_act, seq, tables, ret_norm, layer, states=None, new_states=None):
    n_tok = p_act.shape[0]
    n_seq = n_tok // seq
    has_state = states is not None
    pair_qk, pair_v = 2 * RET_DK, 2 * RET_DV
    rows = max(seq, RET_ROWS)
    blk_seqs = rows // seq
    in_specs = [
        pl.BlockSpec((rows, pair_qk), lambda p, s: (s, OFF_QR // pair_qk + p)),
        pl.BlockSpec((rows, pair_qk), lambda p, s: (s, OFF_KR // pair_qk + p)),
        pl.BlockSpec((rows, pair_v), lambda p, s: (s, OFF_VR // pair_v + p)),
        pl.BlockSpec((rows, pair_v), lambda p, s: (s, OFF_GR // pair_v + p)),
        pl.BlockSpec((1, N_TAB, CHUNK, CHUNK), lambda p, s: (p, 0, 0, 0)),
        pl.BlockSpec((1, RET_DV), lambda p, s: (0, 0)),
    ]
    args = [p_act, p_act, p_act, p_act, tables, ret_norm]
    o_spec = pl.BlockSpec((rows, pair_v), lambda p, s: (s, p))
    o_shape = jax.ShapeDtypeStruct((n_tok, RET_V), BF16)
    st_spec = pl.BlockSpec((blk_seqs, 1, 2, RET_DK, RET_DV), lambda p, s: (s, layer, p, 0, 0))
    aliases = {}
    if has_state:
        in_specs += [st_spec, st_spec]
        args += list(states)
        out_specs, out_shape = o_spec, o_shape
    else:
        in_specs += [pl.BlockSpec(memory_space=pl.ANY)] * 2
        aliases = {len(args): 1, len(args) + 1: 2}
        args += list(new_states)
        st_shape = jax.ShapeDtypeStruct((n_seq, DEPTH, RET_HEADS, RET_DK, RET_DV), F32)
        out_specs, out_shape = [o_spec, st_spec, st_spec], [o_shape, st_shape, st_shape]
    return pl.pallas_call(
        functools.partial(_ret_kernel, seq=seq, has_state=has_state),
        grid=(RET_HEADS // 2, n_tok // rows),
        in_specs=in_specs,
        out_specs=out_specs,
        out_shape=out_shape,
        input_output_aliases=aliases,
        compiler_params=_cparams(("parallel", "parallel")),
        name="retention_lat" if has_state else "retention_ctx",
    )(*args)


FOURIER_ROWS = 1024


def _four_kernel(u0_ref, u1_ref, wc_ref, csn_ref, o_ref):
    gd = FOURIER_GROUP_DIM
    tc, ts = [], []
    for g in range(FOURIER_GROUPS):
        u_ref = (u0_ref, u1_ref)[g // 2]
        u_g = u_ref[:, (g % 2) * gd:(g % 2 + 1) * gd].astype(BF16)
        t = _dot(u_g, wc_ref[...])
        tc.append(t[:, :gd])
        ts.append(t[:, gd:])
    tc = jnp.concatenate(tc, axis=1).astype(BF16)
    ts = jnp.concatenate(ts, axis=1).astype(BF16)
    seq = csn_ref.shape[0]
    for sq in range(o_ref.shape[0] // seq):
        rows = slice(sq * seq, (sq + 1) * seq)
        t_sq = jnp.concatenate([tc[rows], ts[rows]], axis=0)
        o_ref[rows, :] = _dot(csn_ref[...], t_sq).astype(o_ref.dtype)


def _dft_tables(n):
    k = np.arange(n, dtype=np.int64)
    ang = 2.0 * np.pi * ((k[:, None] * k[None, :]) % n).astype(np.float64) / n
    scale = 1.0 / np.sqrt(n)
    return np.cos(ang) * scale, np.sin(ang) * scale


def _fourier(p_act, seq):
    n_tok = p_act.shape[0]
    cc, sc = _dft_tables(FOURIER_GROUP_DIM)
    wc = jnp.asarray(np.concatenate([cc, -sc], axis=1), F32).astype(BF16)
    cn, sn = _dft_tables(seq)
    csn = jnp.asarray(np.concatenate([cn, sn], axis=1), F32).astype(BF16)
    const = lambda shape: pl.BlockSpec(shape, lambda s: (0, 0))
    half = FOURIER_W // 2
    rows = max(seq, FOURIER_ROWS)
    return pl.pallas_call(
        _four_kernel,
        grid=(n_tok // rows,),
        in_specs=[
            pl.BlockSpec((rows, half), lambda s: (s, OFF_UF // half)),
            pl.BlockSpec((rows, half), lambda s: (s, OFF_UF // half + 1)),
            const((FOURIER_GROUP_DIM, 2 * FOURIER_GROUP_DIM)),
            const((seq, 2 * seq)),
        ],
        out_specs=pl.BlockSpec((rows, FOURIER_W), lambda s: (s, 0)),
        out_shape=jax.ShapeDtypeStruct((n_tok, FOURIER_W), BF16),
        compiler_params=_cparams(("parallel",)),
        name="fourier",
    )(p_act, p_act, wc, csn)


MIX_TM = 1024
MERGE_STEPS = 4
MERGE_TM = MIX_TM // MERGE_STEPS
FF_TILE = 256
FF_STEPS = D_FF // FF_TILE
FF_ROWS = 256


def _mix_kernel(oa_ref, or_ref, of_ref, ga_ref, gr_ref, gf_ref, x_ref, g1_ref, sh_ref, sc_ref,
                g2_ref, n2_ref, wba_ref, wbr_ref, wbf_ref, wo_ref, wa_ref, wv_ref, cw_ref, cb_ref,
                wd_ref, o_ref, x1_scr, h_scr, acc_scr, *, seq):
    j = pl.program_id(1)

    def merge(step):
        gate = lambda ref: jnp.tanh(ref[...].astype(F32)) + 1.0
        dst = pl.ds(pl.multiple_of(step * MERGE_TM, MERGE_TM), MERGE_TM)
        merged = (gate(ga_ref) * _dot(oa_ref[...], wba_ref[...])
                  + gate(gr_ref) * _dot(or_ref[...], wbr_ref[...])
                  + gate(gf_ref) * _dot(of_ref[...], wbf_ref[...]))
        x1_scr[dst, :] = x_ref[...] + g1_ref[0] * _dot((0.5 * merged).astype(BF16), wo_ref[...])
        acc_scr[dst, :] = jnp.zeros((MERGE_TM, D_MODEL), F32)

    def norm(step):
        dst = pl.ds(pl.multiple_of(step * MERGE_TM, MERGE_TM), MERGE_TM)
        x1 = x1_scr[dst, :]
        r = lax.rsqrt(jnp.mean(x1 * x1, axis=-1, keepdims=True) + EPS)
        h = (x1 * r) * n2_ref[...] * (1.0 + sc_ref[0]) + sh_ref[0]
        h_scr[dst, :] = h.astype(BF16)

    def mlp():
        blocks = [slice(b * FF_ROWS, (b + 1) * FF_ROWS) for b in range(MIX_TM // FF_ROWS)]
        wa, wv, wd = (ref[0].astype(BF16) for ref in (wa_ref, wv_ref, wd_ref))
        a = jnp.concatenate([_dot(h_scr[rows, :], wa) for rows in blocks], axis=0)
        val = [_dot(h_scr[rows, :], wv) for rows in blocks]
        pos = lax.broadcasted_iota(jnp.int32, a.shape, 0) % seq
        prev = jnp.where(pos == 0, 0.0, pltpu.roll(a, 1, 0))
        nxt = jnp.where(pos == seq - 1, 0.0, pltpu.roll(a, MIX_TM - 1, 0))
        cw = cw_ref[...]
        ac = prev * cw[0:1] + a * cw[1:2] + nxt * cw[2:3] + cb_ref[...]
        for b, rows in enumerate(blocks):
            act = jax.nn.gelu(ac[rows]) * val[b]
            acc_scr[rows, :] += _dot(act.astype(BF16), wd)

    @pl.when(j == 0)
    def _():
        merge(0)

    @pl.when(jnp.logical_and(j > 0, j < MERGE_STEPS))
    def _():
        norm(j - 1)
        merge(j)

    @pl.when(j == MERGE_STEPS)
    def _():
        norm(MERGE_STEPS - 1)
        mlp()

    @pl.when(j > MERGE_STEPS)
    def _():
        mlp()

    @pl.when(j == pl.num_programs(1) - 1)
    def _():
        o_ref[...] = x1_scr[...] + g2_ref[0] * acc_scr[...]


def _mix(o_att, o_ret, o_four, p_gate, x2d, seq, mod, rows_per_mod, w_att, w_ret, w_four, w_out,
         norm2, w_up, conv_w, conv_b, w_down, layer):
    n_tok = x2d.shape[0]
    mod_idx = lambda i: (i * MIX_TM) // rows_per_mod
    sub = lambda i, j: i * MERGE_STEPS + jnp.minimum(j, MERGE_STEPS - 1)
    ff = lambda j: jnp.maximum(j - MERGE_STEPS, 0)
    br = lambda: pl.BlockSpec((MERGE_TM, 512), lambda i, j: (sub(i, j), 0))
    gate = lambda k: pl.BlockSpec((MERGE_TM, D_MODEL), lambda i, j: (sub(i, j), k))
    modv = lambda k: pl.BlockSpec((1, 1, D_MODEL), lambda i, j: (mod_idx(i), 0, k))
    const = lambda shape: pl.BlockSpec(shape, lambda i, j: (0, 0))
    return pl.pallas_call(
        functools.partial(_mix_kernel, seq=seq),
        grid=(n_tok // MIX_TM, MERGE_STEPS + FF_STEPS),
        in_specs=[
            br(), br(), br(), gate(0), gate(1), gate(2),
            pl.BlockSpec((MERGE_TM, D_MODEL), lambda i, j: (sub(i, j), 0)),
            modv(2), modv(3), modv(4), modv(5),
            const((1, D_MODEL)),
            const((512, D_MODEL)), const((512, D_MODEL)), const((512, D_MODEL)),
            const((D_MODEL, D_MODEL)),
            pl.BlockSpec((1, D_MODEL, FF_TILE), lambda i, j: (layer, 0, ff(j))),
            pl.BlockSpec((1, D_MODEL, FF_TILE), lambda i, j: (layer, 0, FF_STEPS + ff(j))),
            pl.BlockSpec((3, FF_TILE), lambda i, j: (0, ff(j))),
            pl.BlockSpec((1, FF_TILE), lambda i, j: (0, ff(j))),
            pl.BlockSpec((1, FF_TILE, D_MODEL), lambda i, j: (layer, ff(j), 0)),
        ],
        out_specs=pl.BlockSpec((MIX_TM, D_MODEL), lambda i, j: (i, 0)),
        out_shape=jax.ShapeDtypeStruct((n_tok, D_MODEL), F32),
        scratch_shapes=[pltpu.VMEM((MIX_TM, D_MODEL), F32), pltpu.VMEM((MIX_TM, D_MODEL), BF16),
                        pltpu.VMEM((MIX_TM, D_MODEL), F32)],
        compiler_params=_cparams(("parallel", "arbitrary")),
        name="mix",
    )(o_att, o_ret, o_four, p_gate, p_gate, p_gate, x2d, mod, mod, mod, mod, norm2,
      w_att, w_ret, w_four, w_out, w_up, w_up, conv_w, conv_b, w_down)


def _rope_tables(n_tok):
    rows = n_tok // GRID_W
    row_id = jnp.repeat(jnp.arange(rows), GRID_W).astype(F32)
    col_id = jnp.tile(jnp.arange(GRID_W), rows).astype(F32)
    n_freq = HEAD_DIM // 4
    inv = ROPE_THETA ** (-jnp.arange(n_freq, dtype=F32) / n_freq)
    ang = jnp.concatenate([row_id[None, :] * inv[:, None], col_id[None, :] * inv[:, None]], axis=0)
    return jnp.cos(ang), jnp.sin(ang)


def kernel(x_prompt, x_sample, cache_k, cache_v, state_ret_fwd, state_ret_bwd, c, c_ctx, w_ada, b_ada, norm1, w_in, q_norm, k_norm, ret_decay_f, ret_decay_b, ret_norm, w_br_att, w_br_ret, w_br_four, w_out, norm2, w_up, conv_w, conv_b, w_down):
    batch, seq, _ = x_prompt.shape
    dec_batch, dec_seq, _ = x_sample.shape
    past = cache_k.shape[2]

    cond_all = jnp.concatenate([c_ctx[None, :], c], axis=0)
    mod_all = _ada(cond_all, w_ada, b_ada)
    cos_t, sin_t = _rope_tables(dec_seq)
    ck = cache_k.reshape(dec_batch, DEPTH, past, ATT_KV)
    cv = cache_v.reshape(dec_batch, DEPTH, past, ATT_KV)

    xp = x_prompt.reshape(batch * seq, D_MODEL)
    xs = x_sample.reshape(dec_batch * dec_seq, D_MODEL)
    new_kv = (jnp.zeros((batch, DEPTH, seq, ATT_KV), F32),) * 2
    new_st = (jnp.zeros((batch, DEPTH, RET_HEADS, RET_DK, RET_DV), F32),) * 2
    for l in range(DEPTH):
        w_att, w_ret, w_four = (w_br_att[l].astype(BF16), w_br_ret[l].astype(BF16),
                                w_br_four[l].astype(BF16))
        w_o = w_out[l].astype(BF16)
        n1, n2 = norm1[l][None, :], norm2[l][None, :]
        qn, kn = q_norm[l][:, None], k_norm[l][:, None]
        rn = ret_norm[l][None, :]
        dec = _ret_tables(jnp.broadcast_to(
            jnp.concatenate([ret_decay_f[l], ret_decay_b[l]])[:, None].astype(F32), (2 * RET_HEADS, 128)))
        cw, cb = conv_w[l], conv_b[l][None, :]
        mod_ctx = mod_all[l, 0:1].reshape(1, 1, 6 * D_MODEL)
        mod_lat = mod_all[l, 1:].reshape(dec_batch, 1, 6 * D_MODEL)

        p_act, p_gate = _in_proj(xp, mod_ctx, batch * seq, n1, w_in, l)
        o_att, *new_kv = _att_ctx(p_act, seq, qn, kn, l, new_kv)
        o_ret, *new_st = _retention(p_act, seq, dec, rn, l, new_states=new_st)
        o_four = _fourier(p_act, seq)
        xp = _mix(o_att, o_ret, o_four, p_gate, xp, seq, mod_ctx, batch * seq,
                  w_att, w_ret, w_four, w_o, n2, w_up, cw, cb, w_down, l)

        p_act, p_gate = _in_proj(xs, mod_lat, dec_seq, n1, w_in, l)
        o_att = _att_lat(p_act, dec_seq, ck, cv, l, qn, kn, cos_t, sin_t)
        o_ret = _retention(p_act, dec_seq, dec, rn, l, states=(state_ret_fwd, state_ret_bwd))
        o_four = _fourier(p_act, dec_seq)
        xs = _mix(o_att, o_ret, o_four, p_gate, xs, dec_seq, mod_lat, dec_seq,
                  w_att, w_ret, w_four, w_o, n2, w_up, cw, cb, w_down, l)

    kv_shape = (batch, DEPTH, seq, N_KV_HEADS, HEAD_DIM)
    return (xp.reshape(batch, seq, D_MODEL), xs.reshape(dec_batch, dec_seq, D_MODEL),
            new_kv[0].reshape(kv_shape), new_kv[1].reshape(kv_shape), new_st[0], new_st[1])
```

```python
import functools

import numpy as np
import jax
import jax.numpy as jnp
from jax import lax
from jax.experimental import pallas as pl
from jax.experimental.pallas import tpu as pltpu

D_MODEL = 1024
DEPTH = 2
GRID_W = 64
HEAD_DIM = 64
N_HEADS = 8
N_KV_HEADS = 2
Q_PER_KV = N_HEADS // N_KV_HEADS
ATT_Q = N_HEADS * HEAD_DIM
ATT_KV = N_KV_HEADS * HEAD_DIM
RET_HEADS = 4
RET_DK = 64
RET_DV = 128
RET_QK = RET_HEADS * RET_DK
RET_V = RET_HEADS * RET_DV
FOURIER_GROUPS = 4
FOURIER_GROUP_DIM = 128
FOURIER_W = FOURIER_GROUPS * FOURIER_GROUP_DIM
D_FF = 2816
CHUNK = 128
Q_BLOCK = 256
ROPE_THETA = 10000.0
EPS = 1e-6
LOG2_E = 1.4426950408889634

F32 = jnp.float32
BF16 = jnp.bfloat16

OFF_QA, OFF_KA, OFF_VA = 0, 512, 640
OFF_QR, OFF_KR, OFF_VR, OFF_GR, OFF_UF = 768, 1024, 1280, 1792, 2304
W_IN_SPLIT = 2816
OFF_GATE = 3072
P_W = OFF_GATE + 3 * D_MODEL

VMEM_LIMIT = 56 * 1024 * 1024


def _cparams(sem, flags=None):
    return pltpu.CompilerParams(dimension_semantics=sem, vmem_limit_bytes=VMEM_LIMIT, flags=flags)


def _dot(a, b):
    return jnp.dot(a, b, preferred_element_type=F32)


def _ada_kernel(cond_ref, w_ref, b_ref, o_ref):
    cnd = cond_ref[...]
    s = cnd * jax.nn.sigmoid(cnd)
    o_ref[0] = _dot(s.astype(BF16), w_ref[0].astype(BF16)) + b_ref[0]


def _ada(cond_all, w_ada, b_ada):
    n = cond_all.shape[0]
    tn = 1024
    return pl.pallas_call(
        _ada_kernel,
        grid=(DEPTH, 6 * D_MODEL // tn),
        in_specs=[
            pl.BlockSpec((n, D_MODEL), lambda l, j: (0, 0)),
            pl.BlockSpec((1, D_MODEL, tn), lambda l, j: (l, 0, j)),
            pl.BlockSpec((1, 1, tn), lambda l, j: (l, 0, j)),
        ],
        out_specs=pl.BlockSpec((1, n, tn), lambda l, j: (l, 0, j)),
        out_shape=jax.ShapeDtypeStruct((DEPTH, n, 6 * D_MODEL), F32),
        compiler_params=_cparams(("parallel", "parallel")),
        name="ada",
    )(cond_all, w_ada, b_ada.reshape(DEPTH, 1, 6 * D_MODEL))


IN_TN = 1024
IN_ROWS = 256
MAIN_STEPS = OFF_GATE // IN_TN


def _in_kernel(x_ref, sh_ref, sc_ref, g_ref, w_ref, om_ref, og_ref, h_scr, w_scr):
    i = pl.program_id(0)
    j = pl.program_id(1)
    cols = pl.ds(pl.multiple_of(j * IN_TN, IN_TN), IN_TN)

    @pl.when(i == 0)
    def _():
        w_scr[:, cols] = w_ref[0].astype(BF16)

    @pl.when(j == 0)
    def _():
        for b in range(x_ref.shape[0] // IN_ROWS):
            rows = slice(b * IN_ROWS, (b + 1) * IN_ROWS)
            x = x_ref[rows, :]
            r = lax.rsqrt(jnp.mean(x * x, axis=-1, keepdims=True) + EPS)
            h = ((x * r) * g_ref[...] * (1.0 + sc_ref[0]) + sh_ref[0]).astype(BF16)
            h_scr[rows, :] = h
            om_ref[rows, :] = _dot(h, w_scr[:, 0:IN_TN])

    @pl.when(jnp.logical_and(j > 0, j < MAIN_STEPS))
    def _():
        om_ref[...] = _dot(h_scr[...], w_scr[:, cols])

    @pl.when(j >= MAIN_STEPS)
    def _():
        og_ref[...] = (0.5 * _dot(h_scr[...], w_scr[:, cols])).astype(og_ref.dtype)


def _in_proj(x2d, mod, rows_per_mod, norm1, w_in, layer):
    n_tok = x2d.shape[0]
    tm, tn = 1024, IN_TN
    n_col = P_W // tn
    mod_idx = lambda i: (i * tm) // rows_per_mod

    def w_col(i, j):
        jj = jnp.where(i == 0, j, n_col - 1)
        col = jnp.where(jj < MAIN_STEPS, jj * tn, W_IN_SPLIT + (jj - MAIN_STEPS) * tn)
        return pl.multiple_of(col, 128)

    return pl.pallas_call(
        _in_kernel,
        grid=(n_tok // tm, n_col),
        in_specs=[
            pl.BlockSpec((tm, D_MODEL), lambda i, j: (i, 0)),
            pl.BlockSpec((1, 1, D_MODEL), lambda i, j: (mod_idx(i), 0, 0)),
            pl.BlockSpec((1, 1, D_MODEL), lambda i, j: (mod_idx(i), 0, 1)),
            pl.BlockSpec((1, D_MODEL), lambda i, j: (0, 0)),
            pl.BlockSpec((pl.Element(1), pl.Element(D_MODEL), pl.Element(tn)),
                         lambda i, j: (layer, 0, w_col(i, j))),
        ],
        out_specs=[
            pl.BlockSpec((tm, tn), lambda i, j: (i, jnp.minimum(j, MAIN_STEPS - 1))),
            pl.BlockSpec((tm, tn), lambda i, j: (i, jnp.maximum(j - MAIN_STEPS, 0))),
        ],
        out_shape=[jax.ShapeDtypeStruct((n_tok, OFF_GATE), F32),
                   jax.ShapeDtypeStruct((n_tok, P_W - OFF_GATE), BF16)],
        scratch_shapes=[pltpu.VMEM((tm, D_MODEL), BF16), pltpu.VMEM((D_MODEL, P_W), BF16)],
        compiler_params=_cparams(("arbitrary", "arbitrary")),
        name="in_proj",
    )(x2d, mod, mod, norm1, w_in)


def _norm_rope_heads_t(x_t, n_heads, g_col, cos_t=None, sin_t=None):
    quarter = HEAD_DIM // 4
    outs = []
    for h in range(n_heads):
        x = x_t[h * HEAD_DIM:(h + 1) * HEAD_DIM, :]
        r = lax.rsqrt(jnp.mean(x * x, axis=0, keepdims=True) + EPS)
        y = (x * r) * g_col
        if cos_t is not None:
            pieces = []
            for a in range(2):
                c = cos_t[a * quarter:(a + 1) * quarter]
                s = sin_t[a * quarter:(a + 1) * quarter]
                x1 = y[2 * a * quarter:(2 * a + 1) * quarter]
                x2 = y[(2 * a + 1) * quarter:(2 * a + 2) * quarter]
                pieces += [x1 * c - x2 * s, x2 * c + x1 * s]
            y = jnp.concatenate(pieces, axis=0)
        outs.append(y)
    return outs


def _attend_t(q_heads_t, k_bf, v_t_bf):
    tq = q_heads_t[0].shape[1]
    cols = Q_PER_KV * tq
    zeros = jnp.zeros((HEAD_DIM, cols), F32)
    q_kv = [jnp.concatenate(q_heads_t[kv * Q_PER_KV:(kv + 1) * Q_PER_KV], axis=1)
            * (HEAD_DIM ** -0.5 * LOG2_E) for kv in range(N_KV_HEADS)]
    rhs = jnp.concatenate([jnp.concatenate([q_kv[0], zeros], axis=1),
                           jnp.concatenate([zeros, q_kv[1]], axis=1)], axis=0).astype(BF16)
    s_all = _dot(k_bf, rhs)
    out_rows = []
    for kv in range(N_KV_HEADS):
        s_t = s_all[:, kv * cols:(kv + 1) * cols]
        e = jnp.exp2(s_t - jnp.max(s_t, axis=0, keepdims=True))
        inv = 1.0 / jnp.sum(e, axis=0, keepdims=True)
        o_t = _dot(v_t_bf[kv * HEAD_DIM:(kv + 1) * HEAD_DIM, :], e.astype(BF16)) * inv
        out_rows += [o_t[:, g * tq:(g + 1) * tq] for g in range(Q_PER_KV)]
    return jnp.concatenate(out_rows, axis=0)


def _att_ctx_kernel(q_ref, k_ref, v_ref, qn_ref, kn_ref, ck_any, cv_any, o_ref, ko_ref, vo_ref):
    del ck_any, cv_any
    seq = ko_ref.shape[2]
    for sq in range(ko_ref.shape[0]):
        rows = slice(sq * seq, (sq + 1) * seq)
        k_heads = _norm_rope_heads_t(k_ref[rows, :].T, N_KV_HEADS, kn_ref[...])
        k_n = jnp.concatenate(k_heads, axis=0).T
        ko_ref[sq, 0] = k_n
        v = v_ref[rows, :]
        vo_ref[sq, 0] = v
        q_heads = _norm_rope_heads_t(q_ref[rows, :].T, N_HEADS, qn_ref[...])
        o_t = _attend_t(q_heads, k_n.astype(BF16), v.T.astype(BF16))
        o_ref[rows, :] = o_t.T.astype(o_ref.dtype)


ATT_CTX_ROWS = 1024


def _att_ctx(p_act, seq, q_norm, k_norm, layer, caches):
    n_tok = p_act.shape[0]
    batch = n_tok // seq
    rows = max(seq, ATT_CTX_ROWS)
    blk = rows // seq
    cache_spec = pl.BlockSpec((blk, 1, seq, ATT_KV), lambda b: (b, layer, 0, 0))
    cache_shape = jax.ShapeDtypeStruct((batch, DEPTH, seq, ATT_KV), F32)
    return pl.pallas_call(
        _att_ctx_kernel,
        grid=(n_tok // rows,),
        in_specs=[
            pl.BlockSpec((rows, ATT_Q), lambda b: (b, OFF_QA // ATT_Q)),
            pl.BlockSpec((rows, ATT_KV), lambda b: (b, OFF_KA // ATT_KV)),
            pl.BlockSpec((rows, ATT_KV), lambda b: (b, OFF_VA // ATT_KV)),
            pl.BlockSpec((HEAD_DIM, 1), lambda b: (0, 0)),
            pl.BlockSpec((HEAD_DIM, 1), lambda b: (0, 0)),
            pl.BlockSpec(memory_space=pl.ANY),
            pl.BlockSpec(memory_space=pl.ANY),
        ],
        out_specs=[pl.BlockSpec((rows, ATT_Q), lambda b: (b, 0)), cache_spec, cache_spec],
        out_shape=[jax.ShapeDtypeStruct((n_tok, ATT_Q), BF16), cache_shape, cache_shape],
        input_output_aliases={5: 1, 6: 2},
        compiler_params=_cparams(("parallel",)),
        name="att_ctx",
    )(p_act, p_act, p_act, q_norm, k_norm, *caches)


def _att_lat_kernel(q_ref, k_ref, v_ref, ck_ref, cv_ref, qn_ref, kn_ref,
                    cq_ref, sq_ref, ck_t_ref, sk_t_ref, o_ref, kf_scr, vt_scr, *, seq):
    @pl.when(pl.program_id(1) == 0)
    def _():
        k_heads = _norm_rope_heads_t(k_ref[...].T, N_KV_HEADS, kn_ref[...], ck_t_ref[...], sk_t_ref[...])
        kf_scr[0:seq, :] = jnp.concatenate(k_heads, axis=0).T.astype(BF16)
        kf_scr[seq:, :] = ck_ref[0, 0].astype(BF16)
        vt_scr[:, 0:seq] = v_ref[...].T.astype(BF16)
        vt_scr[:, seq:] = cv_ref[0, 0].T.astype(BF16)

    q_heads = _norm_rope_heads_t(q_ref[...].T, N_HEADS, qn_ref[...], cq_ref[...], sq_ref[...])
    o_ref[...] = _attend_t(q_heads, kf_scr[...], vt_scr[...]).T.astype(o_ref.dtype)


def _att_lat(p_act, seq, cache_k, cache_v, layer, q_norm, k_norm, cos_t, sin_t):
    n_tok = p_act.shape[0]
    nb = seq // Q_BLOCK
    past = cache_k.shape[2]
    return pl.pallas_call(
        functools.partial(_att_lat_kernel, seq=seq),
        grid=(n_tok // seq, nb),
        in_specs=[
            pl.BlockSpec((Q_BLOCK, ATT_Q), lambda b, i: (b * nb + i, OFF_QA // ATT_Q)),
            pl.BlockSpec((seq, ATT_KV), lambda b, i: (b, OFF_KA // ATT_KV)),
            pl.BlockSpec((seq, ATT_KV), lambda b, i: (b, OFF_VA // ATT_KV)),
            pl.BlockSpec((1, 1, past, ATT_KV), lambda b, i: (b, layer, 0, 0)),
            pl.BlockSpec((1, 1, past, ATT_KV), lambda b, i: (b, layer, 0, 0)),
            pl.BlockSpec((HEAD_DIM, 1), lambda b, i: (0, 0)),
            pl.BlockSpec((HEAD_DIM, 1), lambda b, i: (0, 0)),
            pl.BlockSpec((HEAD_DIM // 2, Q_BLOCK), lambda b, i: (0, i)),
            pl.BlockSpec((HEAD_DIM // 2, Q_BLOCK), lambda b, i: (0, i)),
            pl.BlockSpec((HEAD_DIM // 2, seq), lambda b, i: (0, 0)),
            pl.BlockSpec((HEAD_DIM // 2, seq), lambda b, i: (0, 0)),
        ],
        out_specs=pl.BlockSpec((Q_BLOCK, ATT_Q), lambda b, i: (b * nb + i, 0)),
        out_shape=jax.ShapeDtypeStruct((n_tok, ATT_Q), BF16),
        scratch_shapes=[pltpu.VMEM((seq + past, ATT_KV), BF16),
                        pltpu.VMEM((ATT_KV, seq + past), BF16)],
        compiler_params=_cparams(("parallel", "arbitrary")),
        name="att_lat",
    )(p_act, p_act, p_act, cache_k, cache_v, q_norm, k_norm, cos_t, sin_t, cos_t, sin_t)


RET_ROWS = 1024


def _log_sigmoid(d):
    return jnp.minimum(d, 0.0) - jnp.log1p(jnp.exp(-jnp.abs(d)))


TAB_MASK, TAB_QF, TAB_QB, TAB_KF, TAB_KB, TAB_C, N_TAB = 0, 2, 4, 6, 7, 8, 9
TAB_CF, TAB_CB = 0, 2


def _ret_tables_kernel(dec_ref, tab_ref):
    hp = pl.program_id(0)
    ii = lax.broadcasted_iota(jnp.int32, (CHUNK, CHUNK), 0)
    jj = lax.broadcasted_iota(jnp.int32, (CHUNK, CHUNK), 1)
    rel = (ii - jj).astype(F32)
    row = ii.astype(F32)
    lane = jj.astype(F32)
    lgf = [_log_sigmoid(dec_ref[pl.ds(2 * hp + t, 1), :]) for t in range(2)]
    lgb = [_log_sigmoid(dec_ref[pl.ds(RET_HEADS + 2 * hp + t, 1), :]) for t in range(2)]
    for t in range(2):
        tab_ref[0, TAB_MASK + t] = jnp.where(
            rel > 0, jnp.exp(jnp.maximum(rel, 0.0) * lgf[t]),
            jnp.where(rel < 0, jnp.exp(jnp.maximum(-rel, 0.0) * lgb[t]), 2.0))
        tab_ref[0, TAB_QF + t] = jnp.exp((row + 1.0) * lgf[t])
        tab_ref[0, TAB_QB + t] = jnp.exp((CHUNK - row) * lgb[t])
    tab_ref[0, TAB_KF] = jnp.exp((CHUNK - 1.0 - lane) * jnp.where(ii < RET_DK, lgf[0], lgf[1]))
    tab_ref[0, TAB_KB] = jnp.exp(lane * jnp.where(ii < RET_DK, lgb[0], lgb[1]))
    c_rows = jnp.where(ii == TAB_CF, lgf[0], jnp.where(ii == TAB_CF + 1, lgf[1],
                       jnp.where(ii == TAB_CB, lgb[0], lgb[1])))
    tab_ref[0, TAB_C] = jnp.exp(CHUNK * c_rows)


def _ret_tables(dec):
    return pl.pallas_call(
        _ret_tables_kernel,
        grid=(RET_HEADS // 2,),
        in_specs=[pl.BlockSpec((8, 128), lambda p: (0, 0))],
        out_specs=pl.BlockSpec((1, N_TAB, CHUNK, CHUNK), lambda p: (p, 0, 0, 0)),
        out_shape=jax.ShapeDtypeStruct((RET_HEADS // 2, N_TAB, CHUNK, CHUNK), F32),
        compiler_params=_cparams(("parallel",)),
        name="ret_tables",
    )(dec)


def _ret_kernel(q_ref, k_ref, v_ref, g_ref, tab_ref, rn_ref, *rest, seq, has_state):
    if has_state:
        s0f_ref, s0b_ref, o_ref = rest
    else:
        o_ref, sf_ref, sb_ref = rest[-3:]
    n_chunks = q_ref.shape[0] // CHUNK
    jj = lax.broadcasted_iota(jnp.int32, (CHUNK, CHUNK), 1)
    mask2 = jnp.concatenate([tab_ref[0, TAB_MASK], tab_ref[0, TAB_MASK + 1]], axis=0)
    qdec_f = [tab_ref[0, TAB_QF + t] for t in range(2)]
    qdec_b = [tab_ref[0, TAB_QB + t] for t in range(2)]
    cdec_f = [tab_ref[0, TAB_C, TAB_CF + t:TAB_CF + t + 1, :] for t in range(2)]
    cdec_b = [tab_ref[0, TAB_C, TAB_CB + t:TAB_CB + t + 1, :] for t in range(2)]
    kdec_f = tab_ref[0, TAB_KF]
    kdec_b = tab_ref[0, TAB_KB]

    k_t = (k_ref[...] * (RET_DK ** -0.5)).T
    first_head = jj < RET_DK

    def chunk(c):
        return slice(c * CHUNK, (c + 1) * CHUNK)

    kv_f = [[None] * n_chunks for _ in range(2)]
    kv_b = [[None] * n_chunks for _ in range(2)]
    for c in range(n_chunks):
        k_c = k_t[:, chunk(c)]
        kd = jnp.concatenate([k_c * kdec_f, k_c * kdec_b], axis=0).astype(BF16)
        kv = _dot(kd, v_ref[chunk(c), :].astype(BF16))
        for t in range(2):
            kv_f[t][c] = kv[t * RET_DK:(t + 1) * RET_DK, t * RET_DV:(t + 1) * RET_DV]
            kv_b[t][c] = kv[CHUNK + t * RET_DK:CHUNK + (t + 1) * RET_DK, t * RET_DV:(t + 1) * RET_DV]

    st_f = [[None] * n_chunks for _ in range(2)]
    st_b = [[None] * n_chunks for _ in range(2)]
    per_seq = seq // CHUNK
    for t in range(2):
        for sq in range(n_chunks // per_seq):
            own = range(sq * per_seq, (sq + 1) * per_seq)
            if has_state:
                s_f = s0f_ref[sq, 0, t]
                s_b = s0b_ref[sq, 0, t]
            else:
                s_f = jnp.zeros((RET_DK, RET_DV), F32)
                s_b = s_f
            for c in own:
                st_f[t][c] = s_f
                s_f = s_f * cdec_f[t] + kv_f[t][c]
            for c in reversed(own):
                st_b[t][c] = s_b
                s_b = s_b * cdec_b[t] + kv_b[t][c]
            if not has_state:
                sf_ref[sq, 0, t] = s_f
                sb_ref[sq, 0, t] = s_b

    for c in range(n_chunks):
        q_c = q_ref[chunk(c), :]
        qm = jnp.concatenate([jnp.where(first_head, q_c, 0.0), jnp.where(first_head, 0.0, q_c)],
                             axis=0).astype(BF16)
        att = (_dot(qm, k_t[:, chunk(c)].astype(BF16)) * mask2).astype(BF16)
        states = jnp.concatenate(
            [jnp.concatenate([st_f[t][c], st_b[t][c]], axis=1) for t in range(2)], axis=0)
        qs = _dot(qm, states.astype(BF16))
        v_c = v_ref[chunk(c), :].astype(BF16)
        for t in range(2):
            rows = slice(t * CHUNK, (t + 1) * CHUNK)
            vsl = slice(t * RET_DV, (t + 1) * RET_DV)
            o = (_dot(att[rows], v_c[:, vsl]) + qs[rows, :RET_DV] * qdec_f[t]
                 + qs[rows, RET_DV:] * qdec_b[t])
            o = (o * lax.rsqrt(jnp.mean(o * o, axis=-1, keepdims=True) + EPS)) * rn_ref[...]
            hg = 0.5 * g_ref[chunk(c), vsl]
            o_ref[chunk(c), vsl] = (o * (hg * (1.0 + jnp.tanh(hg)))).astype(o_ref.dtype)


def _retention(p_act, seq, tables, ret_norm, layer, states=None, new_states=None):
    n_tok = p_act.shape[0]
    n_seq = n_tok // seq
    has_state = states is not None
    pair_qk, pair_v = 2 * RET_DK, 2 * RET_DV
    rows = max(seq, RET_ROWS)
    blk_seqs = rows // seq
    in_specs = [
        pl.BlockSpec((rows, pair_qk), lambda p, s: (s, OFF_QR // pair_qk + p)),
        pl.BlockSpec((rows, pair_qk), lambda p, s: (s, OFF_KR // pair_qk + p)),
        pl.BlockSpec((rows, pair_v), lambda p, s: (s, OFF_VR // pair_v + p)),
        pl.BlockSpec((rows, pair_v), lambda p, s: (s, OFF_GR // pair_v + p)),
        pl.BlockSpec((1, N_TAB, CHUNK, CHUNK), lambda p, s: (p, 0, 0, 0)),
        pl.BlockSpec((1, RET_DV), lambda p, s: (0, 0)),
    ]
    args = [p_act, p_act, p_act, p_act, tables, ret_norm]
    o_spec = pl.BlockSpec((rows, pair_v), lambda p, s: (s, p))
    o_shape = jax.ShapeDtypeStruct((n_tok, RET_V), BF16)
    st_spec = pl.BlockSpec((blk_seqs, 1, 2, RET_DK, RET_DV), lambda p, s: (s, layer, p, 0, 0))
    aliases = {}
    if has_state:
        in_specs += [st_spec, st_spec]
        args += list(states)
        out_specs, out_shape = o_spec, o_shape
    else:
        in_specs += [pl.BlockSpec(memory_space=pl.ANY)] * 2
        aliases = {len(args): 1, len(args) + 1: 2}
        args += list(new_states)
        st_shape = jax.ShapeDtypeStruct((n_seq, DEPTH, RET_HEADS, RET_DK, RET_DV), F32)
        out_specs, out_shape = [o_spec, st_spec, st_spec], [o_shape, st_shape, st_shape]
    return pl.pallas_call(
        functools.partial(_ret_kernel, seq=seq, has_state=has_state),
        grid=(RET_HEADS // 2, n_tok // rows),
        in_specs=in_specs,
        out_specs=out_specs,
        out_shape=out_shape,
        input_output_aliases=aliases,
        compiler_params=_cparams(("parallel", "parallel")),
        name="retention_lat" if has_state else "retention_ctx",
    )(*args)


FOURIER_ROWS = 1024


def _four_kernel(u0_ref, u1_ref, wc_ref, csn_ref, o_ref):
    gd = FOURIER_GROUP_DIM
    tc, ts = [], []
    for g in range(FOURIER_GROUPS):
        u_ref = (u0_ref, u1_ref)[g // 2]
        u_g = u_ref[:, (g % 2) * gd:(g % 2 + 1) * gd].astype(BF16)
        t = _dot(u_g, wc_ref[...])
        tc.append(t[:, :gd])
        ts.append(t[:, gd:])
    tc = jnp.concatenate(tc, axis=1).astype(BF16)
    ts = jnp.concatenate(ts, axis=1).astype(BF16)
    seq = csn_ref.shape[0]
    for sq in range(o_ref.shape[0] // seq):
        rows = slice(sq * seq, (sq + 1) * seq)
        t_sq = jnp.concatenate([tc[rows], ts[rows]], axis=0)
        o_ref[rows, :] = _dot(csn_ref[...], t_sq).astype(o_ref.dtype)


def _dft_tables(n):
    k = np.arange(n, dtype=np.int64)
    ang = 2.0 * np.pi * ((k[:, None] * k[None, :]) % n).astype(np.float64) / n
    scale = 1.0 / np.sqrt(n)
    return np.cos(ang) * scale, np.sin(ang) * scale


def _fourier(p_act, seq):
    n_tok = p_act.shape[0]
    cc, sc = _dft_tables(FOURIER_GROUP_DIM)
    wc = jnp.asarray(np.concatenate([cc, -sc], axis=1), F32).astype(BF16)
    cn, sn = _dft_tables(seq)
    csn = jnp.asarray(np.concatenate([cn, sn], axis=1), F32).astype(BF16)
    const = lambda shape: pl.BlockSpec(shape, lambda s: (0, 0))
    half = FOURIER_W // 2
    rows = max(seq, FOURIER_ROWS)
    return pl.pallas_call(
        _four_kernel,
        grid=(n_tok // rows,),
        in_specs=[
            pl.BlockSpec((rows, half), lambda s: (s, OFF_UF // half)),
            pl.BlockSpec((rows, half), lambda s: (s, OFF_UF // half + 1)),
            const((FOURIER_GROUP_DIM, 2 * FOURIER_GROUP_DIM)),
            const((seq, 2 * seq)),
        ],
        out_specs=pl.BlockSpec((rows, FOURIER_W), lambda s: (s, 0)),
        out_shape=jax.ShapeDtypeStruct((n_tok, FOURIER_W), BF16),
        compiler_params=_cparams(("parallel",)),
        name="fourier",
    )(p_act, p_act, wc, csn)


MIX_TM = 1024
MERGE_STEPS = 4
MERGE_TM = MIX_TM // MERGE_STEPS
FF_TILE = 256
FF_STEPS = D_FF // FF_TILE
FF_ROWS = 256


def _mix_kernel(oa_ref, or_ref, of_ref, ga_ref, gr_ref, gf_ref, x_ref, g1_ref, sh_ref, sc_ref,
                g2_ref, n2_ref, wba_ref, wbr_ref, wbf_ref, wo_ref, wa_ref, wv_ref, cw_ref, cb_ref,
                wd_ref, o_ref, h_scr, acc_scr, wu_scr, wd_scr, *, seq):
    i = pl.program_id(0)
    j = pl.program_id(1)
    chunk = jnp.maximum(j - MERGE_STEPS, 0)
    a_cols = pl.ds(pl.multiple_of(chunk * FF_TILE, FF_TILE), FF_TILE)
    v_cols = pl.ds(pl.multiple_of((FF_STEPS + chunk) * FF_TILE, FF_TILE), FF_TILE)

    @pl.when(j < MERGE_STEPS)
    def _():
        gate = lambda ref: jnp.tanh(ref[...].astype(F32)) + 1.0
        dst = pl.ds(pl.multiple_of(j * MERGE_TM, MERGE_TM), MERGE_TM)
        merged = (gate(ga_ref) * _dot(oa_ref[...], wba_ref[...])
                  + gate(gr_ref) * _dot(or_ref[...], wbr_ref[...])
                  + gate(gf_ref) * _dot(of_ref[...], wbf_ref[...]))
        x1 = x_ref[...] + g1_ref[0] * _dot((0.5 * merged).astype(BF16), wo_ref[...])
        o_ref[dst, :] = x1
        r = lax.rsqrt(jnp.mean(x1 * x1, axis=-1, keepdims=True) + EPS)
        h = (x1 * r) * n2_ref[...] * (1.0 + sc_ref[0]) + sh_ref[0]
        h_scr[dst, :] = h.astype(BF16)
        acc_scr[dst, :] = jnp.zeros((MERGE_TM, D_MODEL), F32)

    @pl.when(jnp.logical_and(i == 0, j >= MERGE_STEPS))
    def _():
        wu_scr[:, a_cols] = wa_ref[0].astype(BF16)
        wu_scr[:, v_cols] = wv_ref[0].astype(BF16)
        wd_scr[a_cols, :] = wd_ref[0].astype(BF16)

    @pl.when(j >= MERGE_STEPS)
    def _():
        blocks = [slice(b * FF_ROWS, (b + 1) * FF_ROWS) for b in range(MIX_TM // FF_ROWS)]
        wa, wv, wd = wu_scr[:, a_cols], wu_scr[:, v_cols], wd_scr[a_cols, :]
        a = jnp.concatenate([_dot(h_scr[rows, :], wa) for rows in blocks], axis=0)
        val = [_dot(h_scr[rows, :], wv) for rows in blocks]
        pos = lax.broadcasted_iota(jnp.int32, a.shape, 0) % seq
        prev = jnp.where(pos == 0, 0.0, pltpu.roll(a, 1, 0))
        nxt = jnp.where(pos == seq - 1, 0.0, pltpu.roll(a, MIX_TM - 1, 0))
        cw = cw_ref[...]
        ac = prev * cw[0:1] + a * cw[1:2] + nxt * cw[2:3] + cb_ref[...]
        for b, rows in enumerate(blocks):
            act = jax.nn.gelu(ac[rows]) * val[b]
            acc_scr[rows, :] += _dot(act.astype(BF16), wd)

    @pl.when(j == pl.num_programs(1) - 1)
    def _():
        o_ref[...] = o_ref[...] + g2_ref[0] * acc_scr[...]


def _mix(o_att, o_ret, o_four, p_gate, x2d, seq, mod, rows_per_mod, w_att, w_ret, w_four, w_out,
         norm2, w_up, conv_w, conv_b, w_down, layer):
    n_tok = x2d.shape[0]
    mod_idx = lambda i: (i * MIX_TM) // rows_per_mod
    sub = lambda i, j: i * MERGE_STEPS + jnp.minimum(j, MERGE_STEPS - 1)
    ff = lambda j: jnp.maximum(j - MERGE_STEPS, 0)
    ff_w = lambda i, j: jnp.where(i == 0, ff(j), FF_STEPS - 1)
    br = lambda: pl.BlockSpec((MERGE_TM, 512), lambda i, j: (sub(i, j), 0))
    gate = lambda k: pl.BlockSpec((MERGE_TM, D_MODEL), lambda i, j: (sub(i, j), k))
    modv = lambda k: pl.BlockSpec((1, 1, D_MODEL), lambda i, j: (mod_idx(i), 0, k))
    const = lambda shape: pl.BlockSpec(shape, lambda i, j: (0, 0))
    return pl.pallas_call(
        functools.partial(_mix_kernel, seq=seq),
        grid=(n_tok // MIX_TM, MERGE_STEPS + FF_STEPS),
        in_specs=[
            br(), br(), br(), gate(0), gate(1), gate(2),
            pl.BlockSpec((MERGE_TM, D_MODEL), lambda i, j: (sub(i, j), 0)),
            modv(2), modv(3), modv(4), modv(5),
            const((1, D_MODEL)),
            const((512, D_MODEL)), const((512, D_MODEL)), const((512, D_MODEL)),
            const((D_MODEL, D_MODEL)),
            pl.BlockSpec((1, D_MODEL, FF_TILE), lambda i, j: (layer, 0, ff_w(i, j))),
            pl.BlockSpec((1, D_MODEL, FF_TILE), lambda i, j: (layer, 0, FF_STEPS + ff_w(i, j))),
            pl.BlockSpec((3, FF_TILE), lambda i, j: (0, ff(j))),
            pl.BlockSpec((1, FF_TILE), lambda i, j: (0, ff(j))),
            pl.BlockSpec((1, FF_TILE, D_MODEL), lambda i, j: (layer, ff_w(i, j), 0)),
        ],
        out_specs=pl.BlockSpec((MIX_TM, D_MODEL), lambda i, j: (i, 0)),
        out_shape=jax.ShapeDtypeStruct((n_tok, D_MODEL), F32),
        scratch_shapes=[pltpu.VMEM((MIX_TM, D_MODEL), BF16), pltpu.VMEM((MIX_TM, D_MODEL), F32),
                        pltpu.VMEM((D_MODEL, 2 * D_FF), BF16), pltpu.VMEM((D_FF, D_MODEL), BF16)],
        compiler_params=_cparams(("arbitrary", "arbitrary")),
        name="mix",
    )(o_att, o_ret, o_four, p_gate, p_gate, p_gate, x2d, mod, mod, mod, mod, norm2,
      w_att, w_ret, w_four, w_out, w_up, w_up, conv_w, conv_b, w_down)


def _rope_tables(n_tok):
    rows = n_tok // GRID_W
    row_id = jnp.repeat(jnp.arange(rows), GRID_W).astype(F32)
    col_id = jnp.tile(jnp.arange(GRID_W), rows).astype(F32)
    n_freq = HEAD_DIM // 4
    inv = ROPE_THETA ** (-jnp.arange(n_freq, dtype=F32) / n_freq)
    ang = jnp.concatenate([row_id[None, :] * inv[:, None], col_id[None, :] * inv[:, None]], axis=0)
    return jnp.cos(ang), jnp.sin(ang)


def kernel(x_prompt, x_sample, cache_k, cache_v, state_ret_fwd, state_ret_bwd, c, c_ctx, w_ada, b_ada, norm1, w_in, q_norm, k_norm, ret_decay_f, ret_decay_b, ret_norm, w_br_att, w_br_ret, w_br_four, w_out, norm2, w_up, conv_w, conv_b, w_down):
    batch, seq, _ = x_prompt.shape
    dec_batch, dec_seq, _ = x_sample.shape
    past = cache_k.shape[2]

    cond_all = jnp.concatenate([c_ctx[None, :], c], axis=0)
    mod_all = _ada(cond_all, w_ada, b_ada)
    cos_t, sin_t = _rope_tables(dec_seq)
    ck = cache_k.reshape(dec_batch, DEPTH, past, ATT_KV)
    cv = cache_v.reshape(dec_batch, DEPTH, past, ATT_KV)

    xp = x_prompt.reshape(batch * seq, D_MODEL)
    xs = x_sample.reshape(dec_batch * dec_seq, D_MODEL)
    new_kv = (jnp.zeros((batch, DEPTH, seq, ATT_KV), F32),) * 2
    new_st = (jnp.zeros((batch, DEPTH, RET_HEADS, RET_DK, RET_DV), F32),) * 2
    for l in range(DEPTH):
        w_att, w_ret, w_four = (w_br_att[l].astype(BF16), w_br_ret[l].astype(BF16),
                                w_br_four[l].astype(BF16))
        w_o = w_out[l].astype(BF16)
        n1, n2 = norm1[l][None, :], norm2[l][None, :]
        qn, kn = q_norm[l][:, None], k_norm[l][:, None]
        rn = ret_norm[l][None, :]
        dec = _ret_tables(jnp.broadcast_to(
            jnp.concatenate([ret_decay_f[l], ret_decay_b[l]])[:, None].astype(F32), (2 * RET_HEADS, 128)))
        cw, cb = conv_w[l], conv_b[l][None, :]
        mod_ctx = mod_all[l, 0:1].reshape(1, 1, 6 * D_MODEL)
        mod_lat = mod_all[l, 1:].reshape(dec_batch, 1, 6 * D_MODEL)

        p_act, p_gate = _in_proj(xp, mod_ctx, batch * seq, n1, w_in, l)
        o_att, *new_kv = _att_ctx(p_act, seq, qn, kn, l, new_kv)
        o_ret, *new_st = _retention(p_act, seq, dec, rn, l, new_states=new_st)
        o_four = _fourier(p_act, seq)
        xp = _mix(o_att, o_ret, o_four, p_gate, xp, seq, mod_ctx, batch * seq,
                  w_att, w_ret, w_four, w_o, n2, w_up, cw, cb, w_down, l)

        p_act, p_gate = _in_proj(xs, mod_lat, dec_seq, n1, w_in, l)
        o_att = _att_lat(p_act, dec_seq, ck, cv, l, qn, kn, cos_t, sin_t)
        o_ret = _retention(p_act, dec_seq, dec, rn, l, states=(state_ret_fwd, state_ret_bwd))
        o_four = _fourier(p_act, dec_seq)
        xs = _mix(o_att, o_ret, o_four, p_gate, xs, dec_seq, mod_lat, dec_seq,
                  w_att, w_ret, w_four, w_o, n2, w_up, cw, cb, w_down, l)

    kv_shape = (batch, DEPTH, seq, N_KV_HEADS, HEAD_DIM)
    return (xp.reshape(batch, seq, D_MODEL), xs.reshape(dec_batch, dec_seq, D_MODEL),
            new_kv[0].reshape(kv_shape), new_kv[1].reshape(kv_shape), new_st[0], new_st[1])
```

```python
import functools

import numpy as np
import jax
import jax.numpy as jnp
from jax import lax
from jax.experimental import pallas as pl
from jax.experimental.pallas import tpu as pltpu

D_MODEL = 1024
DEPTH = 2
GRID_W = 64
HEAD_DIM = 64
N_HEADS = 8
N_KV_HEADS = 2
Q_PER_KV = N_HEADS // N_KV_HEADS
ATT_Q = N_HEADS * HEAD_DIM
ATT_KV = N_KV_HEADS * HEAD_DIM
RET_HEADS = 4
RET_DK = 64
RET_DV = 128
RET_QK = RET_HEADS * RET_DK
RET_V = RET_HEADS * RET_DV
FOURIER_GROUPS = 4
FOURIER_GROUP_DIM = 128
FOURIER_W = FOURIER_GROUPS * FOURIER_GROUP_DIM
D_FF = 2816
CHUNK = 128
Q_BLOCK = 256
ROPE_THETA = 10000.0
EPS = 1e-6
LOG2_E = 1.4426950408889634

F32 = jnp.float32
BF16 = jnp.bfloat16

OFF_QA, OFF_KA, OFF_VA = 0, 512, 640
OFF_QR, OFF_KR, OFF_VR, OFF_GR, OFF_UF = 768, 1024, 1280, 1792, 2304
W_IN_SPLIT = 2816
OFF_GATE = 3072
P_W = OFF_GATE + 3 * D_MODEL

VMEM_LIMIT = 56 * 1024 * 1024


def _cparams(sem, flags=None):
    return pltpu.CompilerParams(dimension_semantics=sem, vmem_limit_bytes=VMEM_LIMIT, flags=flags)


def _dot(a, b):
    return jnp.dot(a, b, preferred_element_type=F32)


def _ada_kernel(cond_ref, w_ref, b_ref, o_ref):
    cnd = cond_ref[...]
    s = cnd * jax.nn.sigmoid(cnd)
    o_ref[0] = _dot(s.astype(BF16), w_ref[0].astype(BF16)) + b_ref[0]


def _ada(cond_all, w_ada, b_ada):
    n = cond_all.shape[0]
    tn = 1024
    return pl.pallas_call(
        _ada_kernel,
        grid=(DEPTH, 6 * D_MODEL // tn),
        in_specs=[
            pl.BlockSpec((n, D_MODEL), lambda l, j: (0, 0)),
            pl.BlockSpec((1, D_MODEL, tn), lambda l, j: (l, 0, j)),
            pl.BlockSpec((1, 1, tn), lambda l, j: (l, 0, j)),
        ],
        out_specs=pl.BlockSpec((1, n, tn), lambda l, j: (l, 0, j)),
        out_shape=jax.ShapeDtypeStruct((DEPTH, n, 6 * D_MODEL), F32),
        compiler_params=_cparams(("parallel", "parallel")),
        name="ada",
    )(cond_all, w_ada, b_ada.reshape(DEPTH, 1, 6 * D_MODEL))


IN_TN = 1024
IN_ROWS = 256
MAIN_STEPS = OFF_GATE // IN_TN


def _in_kernel(x_ref, sh_ref, sc_ref, g_ref, w_ref, om_ref, og_ref, h_scr, w_scr):
    i = pl.program_id(0)
    j = pl.program_id(1)
    cols = pl.ds(pl.multiple_of(j * IN_TN, IN_TN), IN_TN)

    @pl.when(i == 0)
    def _():
        w_scr[:, cols] = w_ref[0].astype(BF16)

    @pl.when(j == 0)
    def _():
        for b in range(x_ref.shape[0] // IN_ROWS):
            rows = slice(b * IN_ROWS, (b + 1) * IN_ROWS)
            x = x_ref[rows, :]
            r = lax.rsqrt(jnp.mean(x * x, axis=-1, keepdims=True) + EPS)
            h = ((x * r) * g_ref[...] * (1.0 + sc_ref[0]) + sh_ref[0]).astype(BF16)
            h_scr[rows, :] = h
            om_ref[rows, :] = _dot(h, w_scr[:, 0:IN_TN])

    @pl.when(jnp.logical_and(j > 0, j < MAIN_STEPS))
    def _():
        om_ref[...] = _dot(h_scr[...], w_scr[:, cols])

    @pl.when(j >= MAIN_STEPS)
    def _():
        og_ref[...] = (0.5 * _dot(h_scr[...], w_scr[:, cols])).astype(og_ref.dtype)


def _in_proj(x2d, mod, rows_per_mod, norm1, w_in, layer):
    n_tok = x2d.shape[0]
    tm, tn = 1024, IN_TN
    n_col = P_W // tn
    mod_idx = lambda i: (i * tm) // rows_per_mod

    def w_col(i, j):
        jj = jnp.where(i == 0, j, n_col - 1)
        col = jnp.where(jj < MAIN_STEPS, jj * tn, W_IN_SPLIT + (jj - MAIN_STEPS) * tn)
        return pl.multiple_of(col, 128)

    return pl.pallas_call(
        _in_kernel,
        grid=(n_tok // tm, n_col),
        in_specs=[
            pl.BlockSpec((tm, D_MODEL), lambda i, j: (i, 0)),
            pl.BlockSpec((1, 1, D_MODEL), lambda i, j: (mod_idx(i), 0, 0)),
            pl.BlockSpec((1, 1, D_MODEL), lambda i, j: (mod_idx(i), 0, 1)),
            pl.BlockSpec((1, D_MODEL), lambda i, j: (0, 0)),
            pl.BlockSpec((pl.Element(1), pl.Element(D_MODEL), pl.Element(tn)),
                         lambda i, j: (layer, 0, w_col(i, j))),
        ],
        out_specs=[
            pl.BlockSpec((tm, tn), lambda i, j: (i, jnp.minimum(j, MAIN_STEPS - 1))),
            pl.BlockSpec((tm, tn), lambda i, j: (i, jnp.maximum(j - MAIN_STEPS, 0))),
        ],
        out_shape=[jax.ShapeDtypeStruct((n_tok, OFF_GATE), F32),
                   jax.ShapeDtypeStruct((n_tok, P_W - OFF_GATE), BF16)],
        scratch_shapes=[pltpu.VMEM((tm, D_MODEL), BF16), pltpu.VMEM((D_MODEL, P_W), BF16)],
        compiler_params=_cparams(("arbitrary", "arbitrary")),
        name="in_proj",
    )(x2d, mod, mod, norm1, w_in)


def _norm_rope_heads_t(x_t, n_heads, g_col, cos_t=None, sin_t=None):
    quarter = HEAD_DIM // 4
    outs = []
    for h in range(n_heads):
        x = x_t[h * HEAD_DIM:(h + 1) * HEAD_DIM, :]
        r = lax.rsqrt(jnp.mean(x * x, axis=0, keepdims=True) + EPS)
        y = (x * r) * g_col
        if cos_t is not None:
            pieces = []
            for a in range(2):
                c = cos_t[a * quarter:(a + 1) * quarter]
                s = sin_t[a * quarter:(a + 1) * quarter]
                x1 = y[2 * a * quarter:(2 * a + 1) * quarter]
                x2 = y[(2 * a + 1) * quarter:(2 * a + 2) * quarter]
                pieces += [x1 * c - x2 * s, x2 * c + x1 * s]
            y = jnp.concatenate(pieces, axis=0)
        outs.append(y)
    return outs


SHIFT_LIMIT = 40.0


def _key_norm_max(k_heads_t):
    return [jnp.sqrt(jnp.max(jnp.sum(k * k, axis=0, keepdims=True), axis=1, keepdims=True))
            for k in k_heads_t]


def _attend_t(q_heads_t, k_bf, v_t_bf, k_norm_max, store):
    tq = q_heads_t[0].shape[1]
    cols = Q_PER_KV * tq
    zeros = jnp.zeros((HEAD_DIM, cols), F32)
    q_kv = [jnp.concatenate(q_heads_t[kv * Q_PER_KV:(kv + 1) * Q_PER_KV], axis=1)
            * (HEAD_DIM ** -0.5 * LOG2_E) for kv in range(N_KV_HEADS)]
    rhs = jnp.concatenate([jnp.concatenate([q_kv[0], zeros], axis=1),
                           jnp.concatenate([zeros, q_kv[1]], axis=1)], axis=0).astype(BF16)
    bound = [jnp.sqrt(jnp.sum(q_kv[kv] * q_kv[kv], axis=0, keepdims=True)) * (1.01 * k_norm_max[kv])
             for kv in range(N_KV_HEADS)]
    safe = jnp.maximum(jnp.max(bound[0]), jnp.max(bound[1])) <= SHIFT_LIMIT

    def finish(shift):
        s_all = _dot(k_bf, rhs)
        out_rows = []
        for kv in range(N_KV_HEADS):
            s_t = s_all[:, kv * cols:(kv + 1) * cols]
            m = jnp.max(s_t, axis=0, keepdims=True) if shift is None else shift[kv]
            e = jnp.exp2(s_t - m)
            inv = 1.0 / jnp.sum(e, axis=0, keepdims=True)
            o_t = _dot(v_t_bf[kv * HEAD_DIM:(kv + 1) * HEAD_DIM, :], e.astype(BF16)) * inv
            out_rows += [o_t[:, g * tq:(g + 1) * tq] for g in range(Q_PER_KV)]
        store(jnp.concatenate(out_rows, axis=0))

    @pl.when(safe)
    def _():
        finish(bound)

    @pl.when(jnp.logical_not(safe))
    def _():
        finish(None)


def _att_ctx_kernel(q_ref, k_ref, v_ref, qn_ref, kn_ref, ck_any, cv_any, o_ref, ko_ref, vo_ref):
    del ck_any, cv_any
    seq = ko_ref.shape[2]
    for sq in range(ko_ref.shape[0]):
        rows = slice(sq * seq, (sq + 1) * seq)
        k_heads = _norm_rope_heads_t(k_ref[rows, :].T, N_KV_HEADS, kn_ref[...])
        k_n = jnp.concatenate(k_heads, axis=0).T
        ko_ref[sq, 0] = k_n
        v = v_ref[rows, :]
        vo_ref[sq, 0] = v
        q_heads = _norm_rope_heads_t(q_ref[rows, :].T, N_HEADS, qn_ref[...])

        def store(o_t, rows=rows):
            o_ref[rows, :] = o_t.T.astype(o_ref.dtype)

        _attend_t(q_heads, k_n.astype(BF16), v.T.astype(BF16), _key_norm_max(k_heads), store)


ATT_CTX_ROWS = 1024


def _att_ctx(p_act, seq, q_norm, k_norm, layer, caches):
    n_tok = p_act.shape[0]
    batch = n_tok // seq
    rows = max(seq, ATT_CTX_ROWS)
    blk = rows // seq
    cache_spec = pl.BlockSpec((blk, 1, seq, ATT_KV), lambda b: (b, layer, 0, 0))
    cache_shape = jax.ShapeDtypeStruct((batch, DEPTH, seq, ATT_KV), F32)
    return pl.pallas_call(
        _att_ctx_kernel,
        grid=(n_tok // rows,),
        in_specs=[
            pl.BlockSpec((rows, ATT_Q), lambda b: (b, OFF_QA // ATT_Q)),
            pl.BlockSpec((rows, ATT_KV), lambda b: (b, OFF_KA // ATT_KV)),
            pl.BlockSpec((rows, ATT_KV), lambda b: (b, OFF_VA // ATT_KV)),
            pl.BlockSpec((HEAD_DIM, 1), lambda b: (0, 0)),
            pl.BlockSpec((HEAD_DIM, 1), lambda b: (0, 0)),
            pl.BlockSpec(memory_space=pl.ANY),
            pl.BlockSpec(memory_space=pl.ANY),
        ],
        out_specs=[pl.BlockSpec((rows, ATT_Q), lambda b: (b, 0)), cache_spec, cache_spec],
        out_shape=[jax.ShapeDtypeStruct((n_tok, ATT_Q), BF16), cache_shape, cache_shape],
        input_output_aliases={5: 1, 6: 2},
        compiler_params=_cparams(("parallel",)),
        name="att_ctx",
    )(p_act, p_act, p_act, q_norm, k_norm, *caches)


def _att_lat_kernel(q_ref, k_ref, v_ref, ck_ref, cv_ref, qn_ref, kn_ref,
                    cq_ref, sq_ref, ck_t_ref, sk_t_ref, o_ref, kf_scr, vt_scr, kmax_scr, *, seq):
    @pl.when(pl.program_id(1) == 0)
    def _():
        k_heads = _norm_rope_heads_t(k_ref[...].T, N_KV_HEADS, kn_ref[...], ck_t_ref[...], sk_t_ref[...])
        kf_scr[0:seq, :] = jnp.concatenate(k_heads, axis=0).T.astype(BF16)
        past_k = ck_ref[0, 0]
        kf_scr[seq:, :] = past_k.astype(BF16)
        vt_scr[:, 0:seq] = v_ref[...].T.astype(BF16)
        vt_scr[:, seq:] = cv_ref[0, 0].T.astype(BF16)
        past_t = past_k.T
        own_max = _key_norm_max(k_heads)
        past_max = _key_norm_max([past_t[kv * HEAD_DIM:(kv + 1) * HEAD_DIM] for kv in range(N_KV_HEADS)])
        for kv in range(N_KV_HEADS):
            kmax_scr[kv:kv + 1, :] = jnp.broadcast_to(jnp.maximum(own_max[kv], past_max[kv]), (1, 128))

    q_heads = _norm_rope_heads_t(q_ref[...].T, N_HEADS, qn_ref[...], cq_ref[...], sq_ref[...])

    def store(o_t):
        o_ref[...] = o_t.T.astype(o_ref.dtype)

    k_norm_max = [kmax_scr[kv:kv + 1, 0:1] for kv in range(N_KV_HEADS)]
    _attend_t(q_heads, kf_scr[...], vt_scr[...], k_norm_max, store)


def _att_lat(p_act, seq, cache_k, cache_v, layer, q_norm, k_norm, cos_t, sin_t):
    n_tok = p_act.shape[0]
    nb = seq // Q_BLOCK
    past = cache_k.shape[2]
    return pl.pallas_call(
        functools.partial(_att_lat_kernel, seq=seq),
        grid=(n_tok // seq, nb),
        in_specs=[
            pl.BlockSpec((Q_BLOCK, ATT_Q), lambda b, i: (b * nb + i, OFF_QA // ATT_Q)),
            pl.BlockSpec((seq, ATT_KV), lambda b, i: (b, OFF_KA // ATT_KV)),
            pl.BlockSpec((seq, ATT_KV), lambda b, i: (b, OFF_VA // ATT_KV)),
            pl.BlockSpec((1, 1, past, ATT_KV), lambda b, i: (b, layer, 0, 0)),
            pl.BlockSpec((1, 1, past, ATT_KV), lambda b, i: (b, layer, 0, 0)),
            pl.BlockSpec((HEAD_DIM, 1), lambda b, i: (0, 0)),
            pl.BlockSpec((HEAD_DIM, 1), lambda b, i: (0, 0)),
            pl.BlockSpec((HEAD_DIM // 2, Q_BLOCK), lambda b, i: (0, i)),
            pl.BlockSpec((HEAD_DIM // 2, Q_BLOCK), lambda b, i: (0, i)),
            pl.BlockSpec((HEAD_DIM // 2, seq), lambda b, i: (0, 0)),
            pl.BlockSpec((HEAD_DIM // 2, seq), lambda b, i: (0, 0)),
        ],
        out_specs=pl.BlockSpec((Q_BLOCK, ATT_Q), lambda b, i: (b * nb + i, 0)),
        out_shape=jax.ShapeDtypeStruct((n_tok, ATT_Q), BF16),
        scratch_shapes=[pltpu.VMEM((seq + past, ATT_KV), BF16),
                        pltpu.VMEM((ATT_KV, seq + past), BF16),
                        pltpu.VMEM((8, 128), F32)],
        compiler_params=_cparams(("parallel", "arbitrary")),
        name="att_lat",
    )(p_act, p_act, p_act, cache_k, cache_v, q_norm, k_norm, cos_t, sin_t, cos_t, sin_t)


RET_ROWS = 1024


def _log_sigmoid(d):
    return jnp.minimum(d, 0.0) - jnp.log1p(jnp.exp(-jnp.abs(d)))


TAB_MASK, TAB_QF, TAB_QB, TAB_KF, TAB_KB, TAB_C, N_TAB = 0, 2, 4, 6, 7, 8, 9
TAB_CF, TAB_CB = 0, 2


def _ret_tables_kernel(dec_ref, tab_ref):
    hp = pl.program_id(0)
    ii = lax.broadcasted_iota(jnp.int32, (CHUNK, CHUNK), 0)
    jj = lax.broadcasted_iota(jnp.int32, (CHUNK, CHUNK), 1)
    rel = (ii - jj).astype(F32)
    row = ii.astype(F32)
    lane = jj.astype(F32)
    lgf = [_log_sigmoid(dec_ref[pl.ds(2 * hp + t, 1), :]) for t in range(2)]
    lgb = [_log_sigmoid(dec_ref[pl.ds(RET_HEADS + 2 * hp + t, 1), :]) for t in range(2)]
    for t in range(2):
        tab_ref[0, TAB_MASK + t] = jnp.where(
            rel > 0, jnp.exp(jnp.maximum(rel, 0.0) * lgf[t]),
            jnp.where(rel < 0, jnp.exp(jnp.maximum(-rel, 0.0) * lgb[t]), 2.0))
        tab_ref[0, TAB_QF + t] = jnp.exp((row + 1.0) * lgf[t])
        tab_ref[0, TAB_QB + t] = jnp.exp((CHUNK - row) * lgb[t])
    tab_ref[0, TAB_KF] = jnp.exp((CHUNK - 1.0 - lane) * jnp.where(ii < RET_DK, lgf[0], lgf[1]))
    tab_ref[0, TAB_KB] = jnp.exp(lane * jnp.where(ii < RET_DK, lgb[0], lgb[1]))
    c_rows = jnp.where(ii == TAB_CF, lgf[0], jnp.where(ii == TAB_CF + 1, lgf[1],
                       jnp.where(ii == TAB_CB, lgb[0], lgb[1])))
    tab_ref[0, TAB_C] = jnp.exp(CHUNK * c_rows)


def _ret_tables(dec):
    return pl.pallas_call(
        _ret_tables_kernel,
        grid=(RET_HEADS // 2,),
        in_specs=[pl.BlockSpec((8, 128), lambda p: (0, 0))],
        out_specs=pl.BlockSpec((1, N_TAB, CHUNK, CHUNK), lambda p: (p, 0, 0, 0)),
        out_shape=jax.ShapeDtypeStruct((RET_HEADS // 2, N_TAB, CHUNK, CHUNK), F32),
        compiler_params=_cparams(("parallel",)),
        name="ret_tables",
    )(dec)


def _ret_kernel(q_ref, k_ref, v_ref, g_ref, tab_ref, rn_ref, *rest, seq, has_state):
    if has_state:
        s0f_ref, s0b_ref, o_ref = rest
    else:
        o_ref, sf_ref, sb_ref = rest[-3:]
    n_chunks = q_ref.shape[0] // CHUNK
    jj = lax.broadcasted_iota(jnp.int32, (CHUNK, CHUNK), 1)
    mask2 = jnp.concatenate([tab_ref[0, TAB_MASK], tab_ref[0, TAB_MASK + 1]], axis=0)
    qdec_f = [tab_ref[0, TAB_QF + t] for t in range(2)]
    qdec_b = [tab_ref[0, TAB_QB + t] for t in range(2)]
    cdec_f = [tab_ref[0, TAB_C, TAB_CF + t:TAB_CF + t + 1, :] for t in range(2)]
    cdec_b = [tab_ref[0, TAB_C, TAB_CB + t:TAB_CB + t + 1, :] for t in range(2)]
    kdec_f = tab_ref[0, TAB_KF]
    kdec_b = tab_ref[0, TAB_KB]

    k_t = (k_ref[...] * (RET_DK ** -0.5)).T
    first_head = jj < RET_DK

    def chunk(c):
        return slice(c * CHUNK, (c + 1) * CHUNK)

    kv_f = [[None] * n_chunks for _ in range(2)]
    kv_b = [[None] * n_chunks for _ in range(2)]
    for c in range(n_chunks):
        k_c = k_t[:, chunk(c)]
        kd = jnp.concatenate([k_c * kdec_f, k_c * kdec_b], axis=0).astype(BF16)
        kv = _dot(kd, v_ref[chunk(c), :].astype(BF16))
        for t in range(2):
            kv_f[t][c] = kv[t * RET_DK:(t + 1) * RET_DK, t * RET_DV:(t + 1) * RET_DV]
            kv_b[t][c] = kv[CHUNK + t * RET_DK:CHUNK + (t + 1) * RET_DK, t * RET_DV:(t + 1) * RET_DV]

    st_f = [[None] * n_chunks for _ in range(2)]
    st_b = [[None] * n_chunks for _ in range(2)]
    per_seq = seq // CHUNK
    for t in range(2):
        for sq in range(n_chunks // per_seq):
            own = range(sq * per_seq, (sq + 1) * per_seq)
            if has_state:
                s_f = s0f_ref[sq, 0, t]
                s_b = s0b_ref[sq, 0, t]
            else:
                s_f = jnp.zeros((RET_DK, RET_DV), F32)
                s_b = s_f
            for c in own:
                st_f[t][c] = s_f
                s_f = s_f * cdec_f[t] + kv_f[t][c]
            for c in reversed(own):
                st_b[t][c] = s_b
                s_b = s_b * cdec_b[t] + kv_b[t][c]
            if not has_state:
                sf_ref[sq, 0, t] = s_f
                sb_ref[sq, 0, t] = s_b

    for c in range(n_chunks):
        q_c = q_ref[chunk(c), :]
        qm = jnp.concatenate([jnp.where(first_head, q_c, 0.0), jnp.where(first_head, 0.0, q_c)],
                             axis=0).astype(BF16)
        att = (_dot(qm, k_t[:, chunk(c)].astype(BF16)) * mask2).astype(BF16)
        states = jnp.concatenate(
            [jnp.concatenate([st_f[t][c], st_b[t][c]], axis=1) for t in range(2)], axis=0)
        qs = _dot(qm, states.astype(BF16))
        v_c = v_ref[chunk(c), :].astype(BF16)
        for t in range(2):
            rows = slice(t * CHUNK, (t + 1) * CHUNK)
            vsl = slice(t * RET_DV, (t + 1) * RET_DV)
            o = (_dot(att[rows], v_c[:, vsl]) + qs[rows, :RET_DV] * qdec_f[t]
                 + qs[rows, RET_DV:] * qdec_b[t])
            o = (o * lax.rsqrt(jnp.mean(o * o, axis=-1, keepdims=True) + EPS)) * rn_ref[...]
            hg = 0.5 * g_ref[chunk(c), vsl]
            o_ref[chunk(c), vsl] = (o * (hg * (1.0 + jnp.tanh(hg)))).astype(o_ref.dtype)


def _retention(p_act, seq, tables, ret_norm, layer, states=None, new_states=None):
    n_tok = p_act.shape[0]
    n_seq = n_tok // seq
    has_state = states is not None
    pair_qk, pair_v = 2 * RET_DK, 2 * RET_DV
    rows = max(seq, RET_ROWS)
    blk_seqs = rows // seq
    in_specs = [
        pl.BlockSpec((rows, pair_qk), lambda p, s: (s, OFF_QR // pair_qk + p)),
        pl.BlockSpec((rows, pair_qk), lambda p, s: (s, OFF_KR // pair_qk + p)),
        pl.BlockSpec((rows, pair_v), lambda p, s: (s, OFF_VR // pair_v + p)),
        pl.BlockSpec((rows, pair_v), lambda p, s: (s, OFF_GR // pair_v + p)),
        pl.BlockSpec((1, N_TAB, CHUNK, CHUNK), lambda p, s: (p, 0, 0, 0)),
        pl.BlockSpec((1, RET_DV), lambda p, s: (0, 0)),
    ]
    args = [p_act, p_act, p_act, p_act, tables, ret_norm]
    o_spec = pl.BlockSpec((rows, pair_v), lambda p, s: (s, p))
    o_shape = jax.ShapeDtypeStruct((n_tok, RET_V), BF16)
    st_spec = pl.BlockSpec((blk_seqs, 1, 2, RET_DK, RET_DV), lambda p, s: (s, layer, p, 0, 0))
    aliases = {}
    if has_state:
        in_specs += [st_spec, st_spec]
        args += list(states)
        out_specs, out_shape = o_spec, o_shape
    else:
        in_specs += [pl.BlockSpec(memory_space=pl.ANY)] * 2
        aliases = {len(args): 1, len(args) + 1: 2}
        args += list(new_states)
        st_shape = jax.ShapeDtypeStruct((n_seq, DEPTH, RET_HEADS, RET_DK, RET_DV), F32)
        out_specs, out_shape = [o_spec, st_spec, st_spec], [o_shape, st_shape, st_shape]
    return pl.pallas_call(
        functools.partial(_ret_kernel, seq=seq, has_state=has_state),
        grid=(RET_HEADS // 2, n_tok // rows),
        in_specs=in_specs,
        out_specs=out_specs,
        out_shape=out_shape,
        input_output_aliases=aliases,
        compiler_params=_cparams(("parallel", "parallel")),
        name="retention_lat" if has_state else "retention_ctx",
    )(*args)


FOURIER_ROWS = 1024


def _four_kernel(u0_ref, u1_ref, wc_ref, csn_ref, o_ref):
    gd = FOURIER_GROUP_DIM
    tc, ts = [], []
    for g in range(FOURIER_GROUPS):
        u_ref = (u0_ref, u1_ref)[g // 2]
        u_g = u_ref[:, (g % 2) * gd:(g % 2 + 1) * gd].astype(BF16)
        t = _dot(u_g, wc_ref[...])
        tc.append(t[:, :gd])
        ts.append(t[:, gd:])
    tc = jnp.concatenate(tc, axis=1).astype(BF16)
    ts = jnp.concatenate(ts, axis=1).astype(BF16)
    seq = csn_ref.shape[0]
    for sq in range(o_ref.shape[0] // seq):
        rows = slice(sq * seq, (sq + 1) * seq)
        t_sq = jnp.concatenate([tc[rows], ts[rows]], axis=0)
        o_ref[rows, :] = _dot(csn_ref[...], t_sq).astype(o_ref.dtype)


def _dft_tables(n):
    k = np.arange(n, dtype=np.int64)
    ang = 2.0 * np.pi * ((k[:, None] * k[None, :]) % n).astype(np.float64) / n
    scale = 1.0 / np.sqrt(n)
    return np.cos(ang) * scale, np.sin(ang) * scale


def _fourier(p_act, seq):
    n_tok = p_act.shape[0]
    cc, sc = _dft_tables(FOURIER_GROUP_DIM)
    wc = jnp.asarray(np.concatenate([cc, -sc], axis=1), F32).astype(BF16)
    cn, sn = _dft_tables(seq)
    csn = jnp.asarray(np.concatenate([cn, sn], axis=1), F32).astype(BF16)
    const = lambda shape: pl.BlockSpec(shape, lambda s: (0, 0))
    half = FOURIER_W // 2
    rows = max(seq, FOURIER_ROWS)
    return pl.pallas_call(
        _four_kernel,
        grid=(n_tok // rows,),
        in_specs=[
            pl.BlockSpec((rows, half), lambda s: (s, OFF_UF // half)),
            pl.BlockSpec((rows, half), lambda s: (s, OFF_UF // half + 1)),
            const((FOURIER_GROUP_DIM, 2 * FOURIER_GROUP_DIM)),
            const((seq, 2 * seq)),
        ],
        out_specs=pl.BlockSpec((rows, FOURIER_W), lambda s: (s, 0)),
        out_shape=jax.ShapeDtypeStruct((n_tok, FOURIER_W), BF16),
        compiler_params=_cparams(("parallel",)),
        name="fourier",
    )(p_act, p_act, wc, csn)


MIX_TM = 1024
MERGE_STEPS = 4
MERGE_TM = MIX_TM // MERGE_STEPS
FF_TILE = 256
FF_STEPS = D_FF // FF_TILE
FF_ROWS = 256


def _mix_kernel(oa_ref, or_ref, of_ref, ga_ref, gr_ref, gf_ref, x_ref, g1_ref, sh_ref, sc_ref,
                g2_ref, n2_ref, wba_ref, wbr_ref, wbf_ref, wo_ref, wa_ref, wv_ref, cw_ref, cb_ref,
                wd_ref, o_ref, h_scr, acc_scr, wu_scr, wd_scr, *, seq):
    i = pl.program_id(0)
    j = pl.program_id(1)
    chunk = jnp.maximum(j - MERGE_STEPS, 0)
    a_cols = pl.ds(pl.multiple_of(chunk * FF_TILE, FF_TILE), FF_TILE)
    v_cols = pl.ds(pl.multiple_of((FF_STEPS + chunk) * FF_TILE, FF_TILE), FF_TILE)

    @pl.when(j < MERGE_STEPS)
    def _():
        gate = lambda ref: jnp.tanh(ref[...].astype(F32)) + 1.0
        dst = pl.ds(pl.multiple_of(j * MERGE_TM, MERGE_TM), MERGE_TM)
        merged = (gate(ga_ref) * _dot(oa_ref[...], wba_ref[...])
                  + gate(gr_ref) * _dot(or_ref[...], wbr_ref[...])
                  + gate(gf_ref) * _dot(of_ref[...], wbf_ref[...]))
        x1 = x_ref[...] + g1_ref[0] * _dot((0.5 * merged).astype(BF16), wo_ref[...])
        o_ref[dst, :] = x1
        r = lax.rsqrt(jnp.mean(x1 * x1, axis=-1, keepdims=True) + EPS)
        h = (x1 * r) * n2_ref[...] * (1.0 + sc_ref[0]) + sh_ref[0]
        h_scr[dst, :] = h.astype(BF16)
        acc_scr[dst, :] = jnp.zeros((MERGE_TM, D_MODEL), F32)

    @pl.when(jnp.logical_and(i == 0, j >= MERGE_STEPS))
    def _():
        wu_scr[:, a_cols] = wa_ref[0].astype(BF16)
        wu_scr[:, v_cols] = wv_ref[0].astype(BF16)
        wd_scr[a_cols, :] = wd_ref[0].astype(BF16)

    @pl.when(j >= MERGE_STEPS)
    def _():
        blocks = [slice(b * FF_ROWS, (b + 1) * FF_ROWS) for b in range(MIX_TM // FF_ROWS)]
        wa, wv, wd = wu_scr[:, a_cols], wu_scr[:, v_cols], wd_scr[a_cols, :]
        a = jnp.concatenate([_dot(h_scr[rows, :], wa) for rows in blocks], axis=0)
        val = [_dot(h_scr[rows, :], wv) for rows in blocks]
        pos = lax.broadcasted_iota(jnp.int32, a.shape, 0) % seq
        prev = jnp.where(pos == 0, 0.0, pltpu.roll(a, 1, 0))
        nxt = jnp.where(pos == seq - 1, 0.0, pltpu.roll(a, MIX_TM - 1, 0))
        cw = cw_ref[...]
        ac = prev * cw[0:1] + a * cw[1:2] + nxt * cw[2:3] + cb_ref[...]
        for b, rows in enumerate(blocks):
            act = jax.nn.gelu(ac[rows]) * val[b]
            acc_scr[rows, :] += _dot(act.astype(BF16), wd)

    @pl.when(j == pl.num_programs(1) - 1)
    def _():
        o_ref[...] = o_ref[...] + g2_ref[0] * acc_scr[...]


def _mix(o_att, o_ret, o_four, p_gate, x2d, seq, mod, rows_per_mod, w_att, w_ret, w_four, w_out,
         norm2, w_up, conv_w, conv_b, w_down, layer):
    n_tok = x2d.shape[0]
    mod_idx = lambda i: (i * MIX_TM) // rows_per_mod
    sub = lambda i, j: i * MERGE_STEPS + jnp.minimum(j, MERGE_STEPS - 1)
    ff = lambda j: jnp.maximum(j - MERGE_STEPS, 0)
    ff_w = lambda i, j: jnp.where(i == 0, ff(j), FF_STEPS - 1)
    br = lambda: pl.BlockSpec((MERGE_TM, 512), lambda i, j: (sub(i, j), 0))
    gate = lambda k: pl.BlockSpec((MERGE_TM, D_MODEL), lambda i, j: (sub(i, j), k))
    modv = lambda k: pl.BlockSpec((1, 1, D_MODEL), lambda i, j: (mod_idx(i), 0, k))
    const = lambda shape: pl.BlockSpec(shape, lambda i, j: (0, 0))
    return pl.pallas_call(
        functools.partial(_mix_kernel, seq=seq),
        grid=(n_tok // MIX_TM, MERGE_STEPS + FF_STEPS),
        in_specs=[
            br(), br(), br(), gate(0), gate(1), gate(2),
            pl.BlockSpec((MERGE_TM, D_MODEL), lambda i, j: (sub(i, j), 0)),
            modv(2), modv(3), modv(4), modv(5),
            const((1, D_MODEL)),
            const((512, D_MODEL)), const((512, D_MODEL)), const((512, D_MODEL)),
            const((D_MODEL, D_MODEL)),
            pl.BlockSpec((1, D_MODEL, FF_TILE), lambda i, j: (layer, 0, ff_w(i, j))),
            pl.BlockSpec((1, D_MODEL, FF_TILE), lambda i, j: (layer, 0, FF_STEPS + ff_w(i, j))),
            pl.BlockSpec((3, FF_TILE), lambda i, j: (0, ff(j))),
            pl.BlockSpec((1, FF_TILE), lambda i, j: (0, ff(j))),
            pl.BlockSpec((1, FF_TILE, D_MODEL), lambda i, j: (layer, ff_w(i, j), 0)),
        ],
        out_specs=pl.BlockSpec((MIX_TM, D_MODEL), lambda i, j: (i, 0)),
        out_shape=jax.ShapeDtypeStruct((n_tok, D_MODEL), F32),
        scratch_shapes=[pltpu.VMEM((MIX_TM, D_MODEL), BF16), pltpu.VMEM((MIX_TM, D_MODEL), F32),
                        pltpu.VMEM((D_MODEL, 2 * D_FF), BF16), pltpu.VMEM((D_FF, D_MODEL), BF16)],
        compiler_params=_cparams(("arbitrary", "arbitrary")),
        name="mix",
    )(o_att, o_ret, o_four, p_gate, p_gate, p_gate, x2d, mod, mod, mod, mod, norm2,
      w_att, w_ret, w_four, w_out, w_up, w_up, conv_w, conv_b, w_down)


def _rope_tables(n_tok):
    rows = n_tok // GRID_W
    row_id = jnp.repeat(jnp.arange(rows), GRID_W).astype(F32)
    col_id = jnp.tile(jnp.arange(GRID_W), rows).astype(F32)
    n_freq = HEAD_DIM // 4
    inv = ROPE_THETA ** (-jnp.arange(n_freq, dtype=F32) / n_freq)
    ang = jnp.concatenate([row_id[None, :] * inv[:, None], col_id[None, :] * inv[:, None]], axis=0)
    return jnp.cos(ang), jnp.sin(ang)


def kernel(x_prompt, x_sample, cache_k, cache_v, state_ret_fwd, state_ret_bwd, c, c_ctx, w_ada, b_ada, norm1, w_in, q_norm, k_norm, ret_decay_f, ret_decay_b, ret_norm, w_br_att, w_br_ret, w_br_four, w_out, norm2, w_up, conv_w, conv_b, w_down):
    batch, seq, _ = x_prompt.shape
    dec_batch, dec_seq, _ = x_sample.shape
    past = cache_k.shape[2]

    cond_all = jnp.concatenate([c_ctx[None, :], c], axis=0)
    mod_all = _ada(cond_all, w_ada, b_ada)
    cos_t, sin_t = _rope_tables(dec_seq)
    ck = cache_k.reshape(dec_batch, DEPTH, past, ATT_KV)
    cv = cache_v.reshape(dec_batch, DEPTH, past, ATT_KV)

    xp = x_prompt.reshape(batch * seq, D_MODEL)
    xs = x_sample.reshape(dec_batch * dec_seq, D_MODEL)
    new_kv = (jnp.zeros((batch, DEPTH, seq, ATT_KV), F32),) * 2
    new_st = (jnp.zeros((batch, DEPTH, RET_HEADS, RET_DK, RET_DV), F32),) * 2
    for l in range(DEPTH):
        w_att, w_ret, w_four = (w_br_att[l].astype(BF16), w_br_ret[l].astype(BF16),
                                w_br_four[l].astype(BF16))
        w_o = w_out[l].astype(BF16)
        n1, n2 = norm1[l][None, :], norm2[l][None, :]
        qn, kn = q_norm[l][:, None], k_norm[l][:, None]
        rn = ret_norm[l][None, :]
        dec = _ret_tables(jnp.broadcast_to(
            jnp.concatenate([ret_decay_f[l], ret_decay_b[l]])[:, None].astype(F32), (2 * RET_HEADS, 128)))
        cw, cb = conv_w[l], conv_b[l][None, :]
        mod_ctx = mod_all[l, 0:1].reshape(1, 1, 6 * D_MODEL)
        mod_lat = mod_all[l, 1:].reshape(dec_batch, 1, 6 * D_MODEL)

        p_act, p_gate = _in_proj(xp, mod_ctx, batch * seq, n1, w_in, l)
        o_att, *new_kv = _att_ctx(p_act, seq, qn, kn, l, new_kv)
        o_ret, *new_st = _retention(p_act, seq, dec, rn, l, new_states=new_st)
        o_four = _fourier(p_act, seq)
        xp = _mix(o_att, o_ret, o_four, p_gate, xp, seq, mod_ctx, batch * seq,
                  w_att, w_ret, w_four, w_o, n2, w_up, cw, cb, w_down, l)

        p_act, p_gate = _in_proj(xs, mod_lat, dec_seq, n1, w_in, l)
        o_att = _att_lat(p_act, dec_seq, ck, cv, l, qn, kn, cos_t, sin_t)
        o_ret = _retention(p_act, dec_seq, dec, rn, l, states=(state_ret_fwd, state_ret_bwd))
        o_four = _fourier(p_act, dec_seq)
        xs = _mix(o_att, o_ret, o_four, p_gate, xs, dec_seq, mod_lat, dec_seq,
                  w_att, w_ret, w_four, w_o, n2, w_up, cw, cb, w_down, l)

    kv_shape = (batch, DEPTH, seq, N_KV_HEADS, HEAD_DIM)
    return (xp.reshape(batch, seq, D_MODEL), xs.reshape(dec_batch, dec_seq, D_MODEL),
            new_kv[0].reshape(kv_shape), new_kv[1].reshape(kv_shape), new_st[0], new_st[1])
```

```python
import functools

import numpy as np
import jax
import jax.numpy as jnp
from jax import lax
from jax.experimental import pallas as pl
from jax.experimental.pallas import tpu as pltpu

D_MODEL = 1024
DEPTH = 2
GRID_W = 64
HEAD_DIM = 64
N_HEADS = 8
N_KV_HEADS = 2
Q_PER_KV = N_HEADS // N_KV_HEADS
ATT_Q = N_HEADS * HEAD_DIM
ATT_KV = N_KV_HEADS * HEAD_DIM
RET_HEADS = 4
RET_DK = 64
RET_DV = 128
RET_QK = RET_HEADS * RET_DK
RET_V = RET_HEADS * RET_DV
FOURIER_GROUPS = 4
FOURIER_GROUP_DIM = 128
FOURIER_W = FOURIER_GROUPS * FOURIER_GROUP_DIM
D_FF = 2816
CHUNK = 128
Q_BLOCK = 256
ROPE_THETA = 10000.0
EPS = 1e-6
LOG2_E = 1.4426950408889634

F32 = jnp.float32
BF16 = jnp.bfloat16

OFF_QA, OFF_KA, OFF_VA = 0, 512, 640
OFF_QR, OFF_KR, OFF_VR, OFF_GR, OFF_UF = 768, 1024, 1280, 1792, 2304
W_IN_SPLIT = 2816
OFF_GATE = 3072
P_W = OFF_GATE + 3 * D_MODEL

VMEM_LIMIT = 56 * 1024 * 1024


def _cparams(sem, flags=None):
    return pltpu.CompilerParams(dimension_semantics=sem, vmem_limit_bytes=VMEM_LIMIT, flags=flags)


def _dot(a, b):
    return jnp.dot(a, b, preferred_element_type=F32)


def _ada_kernel(cond_ref, w_ref, b_ref, o_ref):
    cnd = cond_ref[...]
    s = cnd * jax.nn.sigmoid(cnd)
    o_ref[0] = _dot(s.astype(BF16), w_ref[0].astype(BF16)) + b_ref[0]


def _ada(cond_all, w_ada, b_ada):
    n = cond_all.shape[0]
    tn = 1024
    return pl.pallas_call(
        _ada_kernel,
        grid=(DEPTH, 6 * D_MODEL // tn),
        in_specs=[
            pl.BlockSpec((n, D_MODEL), lambda l, j: (0, 0)),
            pl.BlockSpec((1, D_MODEL, tn), lambda l, j: (l, 0, j)),
            pl.BlockSpec((1, 1, tn), lambda l, j: (l, 0, j)),
        ],
        out_specs=pl.BlockSpec((1, n, tn), lambda l, j: (l, 0, j)),
        out_shape=jax.ShapeDtypeStruct((DEPTH, n, 6 * D_MODEL), F32),
        compiler_params=_cparams(("parallel", "parallel")),
        name="ada",
    )(cond_all, w_ada, b_ada.reshape(DEPTH, 1, 6 * D_MODEL))


IN_TN = 1024
IN_ROWS = 256
MAIN_STEPS = OFF_GATE // IN_TN


def _in_kernel(x_ref, sh_ref, sc_ref, g_ref, w_ref, om_ref, og_ref, h_scr, w_scr):
    i = pl.program_id(0)
    j = pl.program_id(1)
    cols = pl.ds(pl.multiple_of(j * IN_TN, IN_TN), IN_TN)

    @pl.when(i == 0)
    def _():
        w_scr[:, cols] = w_ref[0].astype(BF16)

    @pl.when(j == 0)
    def _():
        for b in range(x_ref.shape[0] // IN_ROWS):
            rows = slice(b * IN_ROWS, (b + 1) * IN_ROWS)
            x = x_ref[rows, :]
            r = lax.rsqrt(jnp.mean(x * x, axis=-1, keepdims=True) + EPS)
            h = ((x * r) * g_ref[...] * (1.0 + sc_ref[0]) + sh_ref[0]).astype(BF16)
            h_scr[rows, :] = h
            om_ref[rows, :] = _dot(h, w_scr[:, 0:IN_TN])

    @pl.when(jnp.logical_and(j > 0, j < MAIN_STEPS))
    def _():
        om_ref[...] = _dot(h_scr[...], w_scr[:, cols])

    @pl.when(j >= MAIN_STEPS)
    def _():
        og_ref[...] = (0.5 * _dot(h_scr[...], w_scr[:, cols])).astype(og_ref.dtype)


def _in_proj(x2d, mod, rows_per_mod, norm1, w_in, layer):
    n_tok = x2d.shape[0]
    tm, tn = 1024, IN_TN
    n_col = P_W // tn
    mod_idx = lambda i: (i * tm) // rows_per_mod

    def w_col(i, j):
        jj = jnp.where(i == 0, j, n_col - 1)
        col = jnp.where(jj < MAIN_STEPS, jj * tn, W_IN_SPLIT + (jj - MAIN_STEPS) * tn)
        return pl.multiple_of(col, 128)

    return pl.pallas_call(
        _in_kernel,
        grid=(n_tok // tm, n_col),
        in_specs=[
            pl.BlockSpec((tm, D_MODEL), lambda i, j: (i, 0)),
            pl.BlockSpec((1, 1, D_MODEL), lambda i, j: (mod_idx(i), 0, 0)),
            pl.BlockSpec((1, 1, D_MODEL), lambda i, j: (mod_idx(i), 0, 1)),
            pl.BlockSpec((1, D_MODEL), lambda i, j: (0, 0)),
            pl.BlockSpec((pl.Element(1), pl.Element(D_MODEL), pl.Element(tn)),
                         lambda i, j: (layer, 0, w_col(i, j))),
        ],
        out_specs=[
            pl.BlockSpec((tm, tn), lambda i, j: (i, jnp.minimum(j, MAIN_STEPS - 1))),
            pl.BlockSpec((tm, tn), lambda i, j: (i, jnp.maximum(j - MAIN_STEPS, 0))),
        ],
        out_shape=[jax.ShapeDtypeStruct((n_tok, OFF_GATE), F32),
                   jax.ShapeDtypeStruct((n_tok, P_W - OFF_GATE), BF16)],
        scratch_shapes=[pltpu.VMEM((tm, D_MODEL), BF16), pltpu.VMEM((D_MODEL, P_W), BF16)],
        compiler_params=_cparams(("arbitrary", "arbitrary")),
        name="in_proj",
    )(x2d, mod, mod, norm1, w_in)


def _norm_rope_heads_t(x_t, n_heads, g_col, cos_t=None, sin_t=None):
    quarter = HEAD_DIM // 4
    outs = []
    for h in range(n_heads):
        x = x_t[h * HEAD_DIM:(h + 1) * HEAD_DIM, :]
        r = lax.rsqrt(jnp.mean(x * x, axis=0, keepdims=True) + EPS)
        y = (x * r) * g_col
        if cos_t is not None:
            pieces = []
            for a in range(2):
                c = cos_t[a * quarter:(a + 1) * quarter]
                s = sin_t[a * quarter:(a + 1) * quarter]
                x1 = y[2 * a * quarter:(2 * a + 1) * quarter]
                x2 = y[(2 * a + 1) * quarter:(2 * a + 2) * quarter]
                pieces += [x1 * c - x2 * s, x2 * c + x1 * s]
            y = jnp.concatenate(pieces, axis=0)
        outs.append(y)
    return outs


SHIFT_LIMIT = 40.0


def _score_bound(k_heads_t, q_gain):
    k_sq = [jnp.max(jnp.sum(k * k, axis=0, keepdims=True)) for k in k_heads_t]
    k_norm = jnp.sqrt(functools.reduce(jnp.maximum, k_sq))
    q_norm = HEAD_DIM ** 0.5 * jnp.max(jnp.abs(q_gain))
    return 1.01 * (HEAD_DIM ** -0.5 * LOG2_E) * q_norm * k_norm


def _attend_t(q_heads_t, k_bf, v_t_bf, score_bound, store):
    tq = q_heads_t[0].shape[1]
    cols = Q_PER_KV * tq
    zeros = jnp.zeros((HEAD_DIM, cols), F32)
    q_kv = [jnp.concatenate(q_heads_t[kv * Q_PER_KV:(kv + 1) * Q_PER_KV], axis=1)
            * (HEAD_DIM ** -0.5 * LOG2_E) for kv in range(N_KV_HEADS)]
    rhs = jnp.concatenate([jnp.concatenate([q_kv[0], zeros], axis=1),
                           jnp.concatenate([zeros, q_kv[1]], axis=1)], axis=0).astype(BF16)
    def finish(shift):
        s_all = _dot(k_bf, rhs)
        out_rows = []
        for kv in range(N_KV_HEADS):
            s_t = s_all[:, kv * cols:(kv + 1) * cols]
            m = jnp.max(s_t, axis=0, keepdims=True) if shift is None else shift
            e = jnp.exp2(s_t - m)
            inv = 1.0 / jnp.sum(e, axis=0, keepdims=True)
            o_t = _dot(v_t_bf[kv * HEAD_DIM:(kv + 1) * HEAD_DIM, :], e.astype(BF16)) * inv
            out_rows += [o_t[:, g * tq:(g + 1) * tq] for g in range(Q_PER_KV)]
        store(jnp.concatenate(out_rows, axis=0))

    if score_bound is None:
        finish(None)
        return

    @pl.when(score_bound <= SHIFT_LIMIT)
    def _():
        finish(score_bound)

    @pl.when(jnp.logical_not(score_bound <= SHIFT_LIMIT))
    def _():
        finish(None)


def _att_ctx_kernel(q_ref, k_ref, v_ref, qn_ref, kn_ref, ck_any, cv_any, o_ref, ko_ref, vo_ref):
    del ck_any, cv_any
    seq = ko_ref.shape[2]
    for sq in range(ko_ref.shape[0]):
        rows = slice(sq * seq, (sq + 1) * seq)
        k_heads = _norm_rope_heads_t(k_ref[rows, :].T, N_KV_HEADS, kn_ref[...])
        k_n = jnp.concatenate(k_heads, axis=0).T
        ko_ref[sq, 0] = k_n
        v = v_ref[rows, :]
        vo_ref[sq, 0] = v
        q_heads = _norm_rope_heads_t(q_ref[rows, :].T, N_HEADS, qn_ref[...])

        def store(o_t, rows=rows):
            o_ref[rows, :] = o_t.T.astype(o_ref.dtype)

        _attend_t(q_heads, k_n.astype(BF16), v.T.astype(BF16), None, store)


ATT_CTX_ROWS = 1024


def _att_ctx(p_act, seq, q_norm, k_norm, layer, caches):
    n_tok = p_act.shape[0]
    batch = n_tok // seq
    rows = max(seq, ATT_CTX_ROWS)
    blk = rows // seq
    cache_spec = pl.BlockSpec((blk, 1, seq, ATT_KV), lambda b: (b, layer, 0, 0))
    cache_shape = jax.ShapeDtypeStruct((batch, DEPTH, seq, ATT_KV), F32)
    return pl.pallas_call(
        _att_ctx_kernel,
        grid=(n_tok // rows,),
        in_specs=[
            pl.BlockSpec((rows, ATT_Q), lambda b: (b, OFF_QA // ATT_Q)),
            pl.BlockSpec((rows, ATT_KV), lambda b: (b, OFF_KA // ATT_KV)),
            pl.BlockSpec((rows, ATT_KV), lambda b: (b, OFF_VA // ATT_KV)),
            pl.BlockSpec((HEAD_DIM, 1), lambda b: (0, 0)),
            pl.BlockSpec((HEAD_DIM, 1), lambda b: (0, 0)),
            pl.BlockSpec(memory_space=pl.ANY),
            pl.BlockSpec(memory_space=pl.ANY),
        ],
        out_specs=[pl.BlockSpec((rows, ATT_Q), lambda b: (b, 0)), cache_spec, cache_spec],
        out_shape=[jax.ShapeDtypeStruct((n_tok, ATT_Q), BF16), cache_shape, cache_shape],
        input_output_aliases={5: 1, 6: 2},
        compiler_params=_cparams(("parallel",)),
        name="att_ctx",
    )(p_act, p_act, p_act, q_norm, k_norm, *caches)


def _att_lat_kernel(q_ref, k_ref, v_ref, ck_ref, cv_ref, qn_ref, kn_ref,
                    cq_ref, sq_ref, ck_t_ref, sk_t_ref, o_ref, kf_scr, vt_scr, bound_scr, *, seq):
    @pl.when(pl.program_id(1) == 0)
    def _():
        k_heads = _norm_rope_heads_t(k_ref[...].T, N_KV_HEADS, kn_ref[...], ck_t_ref[...], sk_t_ref[...])
        kf_scr[0:seq, :] = jnp.concatenate(k_heads, axis=0).T.astype(BF16)
        past_k = ck_ref[0, 0]
        kf_scr[seq:, :] = past_k.astype(BF16)
        vt_scr[:, 0:seq] = v_ref[...].T.astype(BF16)
        vt_scr[:, seq:] = cv_ref[0, 0].T.astype(BF16)
        past_t = past_k.T
        past_heads = [past_t[kv * HEAD_DIM:(kv + 1) * HEAD_DIM] for kv in range(N_KV_HEADS)]
        bound_scr[0] = _score_bound(k_heads + past_heads, qn_ref[...])

    q_heads = _norm_rope_heads_t(q_ref[...].T, N_HEADS, qn_ref[...], cq_ref[...], sq_ref[...])

    def store(o_t):
        o_ref[...] = o_t.T.astype(o_ref.dtype)

    _attend_t(q_heads, kf_scr[...], vt_scr[...], bound_scr[0], store)


def _att_lat(p_act, seq, cache_k, cache_v, layer, q_norm, k_norm, cos_t, sin_t):
    n_tok = p_act.shape[0]
    nb = seq // Q_BLOCK
    past = cache_k.shape[2]
    return pl.pallas_call(
        functools.partial(_att_lat_kernel, seq=seq),
        grid=(n_tok // seq, nb),
        in_specs=[
            pl.BlockSpec((Q_BLOCK, ATT_Q), lambda b, i: (b * nb + i, OFF_QA // ATT_Q)),
            pl.BlockSpec((seq, ATT_KV), lambda b, i: (b, OFF_KA // ATT_KV)),
            pl.BlockSpec((seq, ATT_KV), lambda b, i: (b, OFF_VA // ATT_KV)),
            pl.BlockSpec((1, 1, past, ATT_KV), lambda b, i: (b, layer, 0, 0)),
            pl.BlockSpec((1, 1, past, ATT_KV), lambda b, i: (b, layer, 0, 0)),
            pl.BlockSpec((HEAD_DIM, 1), lambda b, i: (0, 0)),
            pl.BlockSpec((HEAD_DIM, 1), lambda b, i: (0, 0)),
            pl.BlockSpec((HEAD_DIM // 2, Q_BLOCK), lambda b, i: (0, i)),
            pl.BlockSpec((HEAD_DIM // 2, Q_BLOCK), lambda b, i: (0, i)),
            pl.BlockSpec((HEAD_DIM // 2, seq), lambda b, i: (0, 0)),
            pl.BlockSpec((HEAD_DIM // 2, seq), lambda b, i: (0, 0)),
        ],
        out_specs=pl.BlockSpec((Q_BLOCK, ATT_Q), lambda b, i: (b * nb + i, 0)),
        out_shape=jax.ShapeDtypeStruct((n_tok, ATT_Q), BF16),
        scratch_shapes=[pltpu.VMEM((seq + past, ATT_KV), BF16),
                        pltpu.VMEM((ATT_KV, seq + past), BF16),
                        pltpu.SMEM((1,), F32)],
        compiler_params=_cparams(("parallel", "arbitrary")),
        name="att_lat",
    )(p_act, p_act, p_act, cache_k, cache_v, q_norm, k_norm, cos_t, sin_t, cos_t, sin_t)


RET_ROWS = 1024


def _log_sigmoid(d):
    return jnp.minimum(d, 0.0) - jnp.log1p(jnp.exp(-jnp.abs(d)))


TAB_MASK, TAB_QF, TAB_QB, TAB_KF, TAB_KB, TAB_C, N_TAB = 0, 2, 4, 6, 7, 8, 9
TAB_CF, TAB_CB = 0, 2


def _ret_tables_kernel(dec_ref, tab_ref):
    hp = pl.program_id(0)
    ii = lax.broadcasted_iota(jnp.int32, (CHUNK, CHUNK), 0)
    jj = lax.broadcasted_iota(jnp.int32, (CHUNK, CHUNK), 1)
    rel = (ii - jj).astype(F32)
    row = ii.astype(F32)
    lane = jj.astype(F32)
    lgf = [_log_sigmoid(dec_ref[pl.ds(2 * hp + t, 1), :]) for t in range(2)]
    lgb = [_log_sigmoid(dec_ref[pl.ds(RET_HEADS + 2 * hp + t, 1), :]) for t in range(2)]
    for t in range(2):
        tab_ref[0, TAB_MASK + t] = jnp.where(
            rel > 0, jnp.exp(jnp.maximum(rel, 0.0) * lgf[t]),
            jnp.where(rel < 0, jnp.exp(jnp.maximum(-rel, 0.0) * lgb[t]), 2.0))
        tab_ref[0, TAB_QF + t] = jnp.exp((row + 1.0) * lgf[t])
        tab_ref[0, TAB_QB + t] = jnp.exp((CHUNK - row) * lgb[t])
    tab_ref[0, TAB_KF] = jnp.exp((CHUNK - 1.0 - lane) * jnp.where(ii < RET_DK, lgf[0], lgf[1]))
    tab_ref[0, TAB_KB] = jnp.exp(lane * jnp.where(ii < RET_DK, lgb[0], lgb[1]))
    c_rows = jnp.where(ii == TAB_CF, lgf[0], jnp.where(ii == TAB_CF + 1, lgf[1],
                       jnp.where(ii == TAB_CB, lgb[0], lgb[1])))
    tab_ref[0, TAB_C] = jnp.exp(CHUNK * c_rows)


def _ret_tables(dec):
    return pl.pallas_call(
        _ret_tables_kernel,
        grid=(RET_HEADS // 2,),
        in_specs=[pl.BlockSpec((8, 128), lambda p: (0, 0))],
        out_specs=pl.BlockSpec((1, N_TAB, CHUNK, CHUNK), lambda p: (p, 0, 0, 0)),
        out_shape=jax.ShapeDtypeStruct((RET_HEADS // 2, N_TAB, CHUNK, CHUNK), F32),
        compiler_params=_cparams(("parallel",)),
        name="ret_tables",
    )(dec)


def _ret_kernel(q_ref, k_ref, v_ref, g_ref, tab_ref, rn_ref, *rest, seq, has_state):
    if has_state:
        s0f_ref, s0b_ref, o_ref = rest
    else:
        o_ref, sf_ref, sb_ref = rest[-3:]
    n_chunks = q_ref.shape[0] // CHUNK
    jj = lax.broadcasted_iota(jnp.int32, (CHUNK, CHUNK), 1)
    mask2 = jnp.concatenate([tab_ref[0, TAB_MASK], tab_ref[0, TAB_MASK + 1]], axis=0)
    qdec_f = [tab_ref[0, TAB_QF + t] for t in range(2)]
    qdec_b = [tab_ref[0, TAB_QB + t] for t in range(2)]
    cdec_f = [tab_ref[0, TAB_C, TAB_CF + t:TAB_CF + t + 1, :] for t in range(2)]
    cdec_b = [tab_ref[0, TAB_C, TAB_CB + t:TAB_CB + t + 1, :] for t in range(2)]
    kdec_f = tab_ref[0, TAB_KF]
    kdec_b = tab_ref[0, TAB_KB]

    k_t = (k_ref[...] * (RET_DK ** -0.5)).T
    first_head = jj < RET_DK

    def chunk(c):
        return slice(c * CHUNK, (c + 1) * CHUNK)

    kv_f = [[None] * n_chunks for _ in range(2)]
    kv_b = [[None] * n_chunks for _ in range(2)]
    for c in range(n_chunks):
        k_c = k_t[:, chunk(c)]
        kd = jnp.concatenate([k_c * kdec_f, k_c * kdec_b], axis=0).astype(BF16)
        kv = _dot(kd, v_ref[chunk(c), :].astype(BF16))
        for t in range(2):
            kv_f[t][c] = kv[t * RET_DK:(t + 1) * RET_DK, t * RET_DV:(t + 1) * RET_DV]
            kv_b[t][c] = kv[CHUNK + t * RET_DK:CHUNK + (t + 1) * RET_DK, t * RET_DV:(t + 1) * RET_DV]

    st_f = [[None] * n_chunks for _ in range(2)]
    st_b = [[None] * n_chunks for _ in range(2)]
    per_seq = seq // CHUNK
    for t in range(2):
        for sq in range(n_chunks // per_seq):
            own = range(sq * per_seq, (sq + 1) * per_seq)
            if has_state:
                s_f = s0f_ref[sq, 0, t]
                s_b = s0b_ref[sq, 0, t]
            else:
                s_f = jnp.zeros((RET_DK, RET_DV), F32)
                s_b = s_f
            for c in own:
                st_f[t][c] = s_f
                s_f = s_f * cdec_f[t] + kv_f[t][c]
            for c in reversed(own):
                st_b[t][c] = s_b
                s_b = s_b * cdec_b[t] + kv_b[t][c]
            if not has_state:
                sf_ref[sq, 0, t] = s_f
                sb_ref[sq, 0, t] = s_b

    for c in range(n_chunks):
        q_c = q_ref[chunk(c), :]
        qm = jnp.concatenate([jnp.where(first_head, q_c, 0.0), jnp.where(first_head, 0.0, q_c)],
                             axis=0).astype(BF16)
        att = (_dot(qm, k_t[:, chunk(c)].astype(BF16)) * mask2).astype(BF16)
        states = jnp.concatenate(
            [jnp.concatenate([st_f[t][c], st_b[t][c]], axis=1) for t in range(2)], axis=0)
        qs = _dot(qm, states.astype(BF16))
        v_c = v_ref[chunk(c), :].astype(BF16)
        for t in range(2):
            rows = slice(t * CHUNK, (t + 1) * CHUNK)
            vsl = slice(t * RET_DV, (t + 1) * RET_DV)
            o = (_dot(att[rows], v_c[:, vsl]) + qs[rows, :RET_DV] * qdec_f[t]
                 + qs[rows, RET_DV:] * qdec_b[t])
            o = (o * lax.rsqrt(jnp.mean(o * o, axis=-1, keepdims=True) + EPS)) * rn_ref[...]
            hg = 0.5 * g_ref[chunk(c), vsl]
            o_ref[chunk(c), vsl] = (o * (hg * (1.0 + jnp.tanh(hg)))).astype(o_ref.dtype)


def _retention(p_act, seq, tables, ret_norm, layer, states=None, new_states=None):
    n_tok = p_act.shape[0]
    n_seq = n_tok // seq
    has_state = states is not None
    pair_qk, pair_v = 2 * RET_DK, 2 * RET_DV
    rows = max(seq, RET_ROWS)
    blk_seqs = rows // seq
    in_specs = [
        pl.BlockSpec((rows, pair_qk), lambda p, s: (s, OFF_QR // pair_qk + p)),
        pl.BlockSpec((rows, pair_qk), lambda p, s: (s, OFF_KR // pair_qk + p)),
        pl.BlockSpec((rows, pair_v), lambda p, s: (s, OFF_VR // pair_v + p)),
        pl.BlockSpec((rows, pair_v), lambda p, s: (s, OFF_GR // pair_v + p)),
        pl.BlockSpec((1, N_TAB, CHUNK, CHUNK), lambda p, s: (p, 0, 0, 0)),
        pl.BlockSpec((1, RET_DV), lambda p, s: (0, 0)),
    ]
    args = [p_act, p_act, p_act, p_act, tables, ret_norm]
    o_spec = pl.BlockSpec((rows, pair_v), lambda p, s: (s, p))
    o_shape = jax.ShapeDtypeStruct((n_tok, RET_V), BF16)
    st_spec = pl.BlockSpec((blk_seqs, 1, 2, RET_DK, RET_DV), lambda p, s: (s, layer, p, 0, 0))
    aliases = {}
    if has_state:
        in_specs += [st_spec, st_spec]
        args += list(states)
        out_specs, out_shape = o_spec, o_shape
    else:
        in_specs += [pl.BlockSpec(memory_space=pl.ANY)] * 2
        aliases = {len(args): 1, len(args) + 1: 2}
        args += list(new_states)
        st_shape = jax.ShapeDtypeStruct((n_seq, DEPTH, RET_HEADS, RET_DK, RET_DV), F32)
        out_specs, out_shape = [o_spec, st_spec, st_spec], [o_shape, st_shape, st_shape]
    return pl.pallas_call(
        functools.partial(_ret_kernel, seq=seq, has_state=has_state),
        grid=(RET_HEADS // 2, n_tok // rows),
        in_specs=in_specs,
        out_specs=out_specs,
        out_shape=out_shape,
        input_output_aliases=aliases,
        compiler_params=_cparams(("parallel", "parallel")),
        name="retention_lat" if has_state else "retention_ctx",
    )(*args)


FOURIER_ROWS = 1024


def _four_kernel(u0_ref, u1_ref, wc_ref, csn_ref, o_ref):
    gd = FOURIER_GROUP_DIM
    tc, ts = [], []
    for g in range(FOURIER_GROUPS):
        u_ref = (u0_ref, u1_ref)[g // 2]
        u_g = u_ref[:, (g % 2) * gd:(g % 2 + 1) * gd].astype(BF16)
        t = _dot(u_g, wc_ref[...])
        tc.append(t[:, :gd])
        ts.append(t[:, gd:])
    tc = jnp.concatenate(tc, axis=1).astype(BF16)
    ts = jnp.concatenate(ts, axis=1).astype(BF16)
    seq = csn_ref.shape[0]
    for sq in range(o_ref.shape[0] // seq):
        rows = slice(sq * seq, (sq + 1) * seq)
        t_sq = jnp.concatenate([tc[rows], ts[rows]], axis=0)
        o_ref[rows, :] = _dot(csn_ref[...], t_sq).astype(o_ref.dtype)


def _dft_tables(n):
    k = np.arange(n, dtype=np.int64)
    ang = 2.0 * np.pi * ((k[:, None] * k[None, :]) % n).astype(np.float64) / n
    scale = 1.0 / np.sqrt(n)
    return np.cos(ang) * scale, np.sin(ang) * scale


def _fourier(p_act, seq):
    n_tok = p_act.shape[0]
    cc, sc = _dft_tables(FOURIER_GROUP_DIM)
    wc = jnp.asarray(np.concatenate([cc, -sc], axis=1), F32).astype(BF16)
    cn, sn = _dft_tables(seq)
    csn = jnp.asarray(np.concatenate([cn, sn], axis=1), F32).astype(BF16)
    const = lambda shape: pl.BlockSpec(shape, lambda s: (0, 0))
    half = FOURIER_W // 2
    rows = max(seq, FOURIER_ROWS)
    return pl.pallas_call(
        _four_kernel,
        grid=(n_tok // rows,),
        in_specs=[
            pl.BlockSpec((rows, half), lambda s: (s, OFF_UF // half)),
            pl.BlockSpec((rows, half), lambda s: (s, OFF_UF // half + 1)),
            const((FOURIER_GROUP_DIM, 2 * FOURIER_GROUP_DIM)),
            const((seq, 2 * seq)),
        ],
        out_specs=pl.BlockSpec((rows, FOURIER_W), lambda s: (s, 0)),
        out_shape=jax.ShapeDtypeStruct((n_tok, FOURIER_W), BF16),
        compiler_params=_cparams(("parallel",)),
        name="fourier",
    )(p_act, p_act, wc, csn)


MIX_TM = 1024
MERGE_STEPS = 4
MERGE_TM = MIX_TM // MERGE_STEPS
FF_TILE = 256
FF_STEPS = D_FF // FF_TILE
FF_ROWS = 256


def _mix_kernel(oa_ref, or_ref, of_ref, ga_ref, gr_ref, gf_ref, x_ref, g1_ref, sh_ref, sc_ref,
                g2_ref, n2_ref, wba_ref, wbr_ref, wbf_ref, wo_ref, wa_ref, wv_ref, cw_ref, cb_ref,
                wd_ref, o_ref, h_scr, acc_scr, wu_scr, wd_scr, *, seq):
    i = pl.program_id(0)
    j = pl.program_id(1)
    chunk = jnp.maximum(j - MERGE_STEPS, 0)
    a_cols = pl.ds(pl.multiple_of(chunk * FF_TILE, FF_TILE), FF_TILE)
    v_cols = pl.ds(pl.multiple_of((FF_STEPS + chunk) * FF_TILE, FF_TILE), FF_TILE)

    @pl.when(j < MERGE_STEPS)
    def _():
        gate = lambda ref: jnp.tanh(ref[...].astype(F32)) + 1.0
        dst = pl.ds(pl.multiple_of(j * MERGE_TM, MERGE_TM), MERGE_TM)
        merged = (gate(ga_ref) * _dot(oa_ref[...], wba_ref[...])
                  + gate(gr_ref) * _dot(or_ref[...], wbr_ref[...])
                  + gate(gf_ref) * _dot(of_ref[...], wbf_ref[...]))
        x1 = x_ref[...] + g1_ref[0] * _dot((0.5 * merged).astype(BF16), wo_ref[...])
        o_ref[dst, :] = x1
        r = lax.rsqrt(jnp.mean(x1 * x1, axis=-1, keepdims=True) + EPS)
        h = (x1 * r) * n2_ref[...] * (1.0 + sc_ref[0]) + sh_ref[0]
        h_scr[dst, :] = h.astype(BF16)
        acc_scr[dst, :] = jnp.zeros((MERGE_TM, D_MODEL), F32)

    @pl.when(jnp.logical_and(i == 0, j >= MERGE_STEPS))
    def _():
        wu_scr[:, a_cols] = wa_ref[0].astype(BF16)
        wu_scr[:, v_cols] = wv_ref[0].astype(BF16)
        wd_scr[a_cols, :] = wd_ref[0].astype(BF16)

    @pl.when(j >= MERGE_STEPS)
    def _():
        blocks = [slice(b * FF_ROWS, (b + 1) * FF_ROWS) for b in range(MIX_TM // FF_ROWS)]
        wa, wv, wd = wu_scr[:, a_cols], wu_scr[:, v_cols], wd_scr[a_cols, :]
        a = jnp.concatenate([_dot(h_scr[rows, :], wa) for rows in blocks], axis=0)
        val = [_dot(h_scr[rows, :], wv) for rows in blocks]
        pos = lax.broadcasted_iota(jnp.int32, a.shape, 0) % seq
        prev = jnp.where(pos == 0, 0.0, pltpu.roll(a, 1, 0))
        nxt = jnp.where(pos == seq - 1, 0.0, pltpu.roll(a, MIX_TM - 1, 0))
        cw = cw_ref[...]
        ac = prev * cw[0:1] + a * cw[1:2] + nxt * cw[2:3] + cb_ref[...]
        for b, rows in enumerate(blocks):
            act = jax.nn.gelu(ac[rows]) * val[b]
            acc_scr[rows, :] += _dot(act.astype(BF16), wd)

    @pl.when(j == pl.num_programs(1) - 1)
    def _():
        o_ref[...] = o_ref[...] + g2_ref[0] * acc_scr[...]


def _mix(o_att, o_ret, o_four, p_gate, x2d, seq, mod, rows_per_mod, w_att, w_ret, w_four, w_out,
         norm2, w_up, conv_w, conv_b, w_down, layer):
    n_tok = x2d.shape[0]
    mod_idx = lambda i: (i * MIX_TM) // rows_per_mod
    sub = lambda i, j: i * MERGE_STEPS + jnp.minimum(j, MERGE_STEPS - 1)
    ff = lambda j: jnp.maximum(j - MERGE_STEPS, 0)
    ff_w = lambda i, j: jnp.where(i == 0, ff(j), FF_STEPS - 1)
    br = lambda: pl.BlockSpec((MERGE_TM, 512), lambda i, j: (sub(i, j), 0))
    gate = lambda k: pl.BlockSpec((MERGE_TM, D_MODEL), lambda i, j: (sub(i, j), k))
    modv = lambda k: pl.BlockSpec((1, 1, D_MODEL), lambda i, j: (mod_idx(i), 0, k))
    const = lambda shape: pl.BlockSpec(shape, lambda i, j: (0, 0))
    return pl.pallas_call(
        functools.partial(_mix_kernel, seq=seq),
        grid=(n_tok // MIX_TM, MERGE_STEPS + FF_STEPS),
        in_specs=[
            br(), br(), br(), gate(0), gate(1), gate(2),
            pl.BlockSpec((MERGE_TM, D_MODEL), lambda i, j: (sub(i, j), 0)),
            modv(2), modv(3), modv(4), modv(5),
            const((1, D_MODEL)),
            const((512, D_MODEL)), const((512, D_MODEL)), const((512, D_MODEL)),
            const((D_MODEL, D_MODEL)),
            pl.BlockSpec((1, D_MODEL, FF_TILE), lambda i, j: (layer, 0, ff_w(i, j))),
            pl.BlockSpec((1, D_MODEL, FF_TILE), lambda i, j: (layer, 0, FF_STEPS + ff_w(i, j))),
            pl.BlockSpec((3, FF_TILE), lambda i, j: (0, ff(j))),
            pl.BlockSpec((1, FF_TILE), lambda i, j: (0, ff(j))),
            pl.BlockSpec((1, FF_TILE, D_MODEL), lambda i, j: (layer, ff_w(i, j), 0)),
        ],
        out_specs=pl.BlockSpec((MIX_TM, D_MODEL), lambda i, j: (i, 0)),
        out_shape=jax.ShapeDtypeStruct((n_tok, D_MODEL), F32),
        scratch_shapes=[pltpu.VMEM((MIX_TM, D_MODEL), BF16), pltpu.VMEM((MIX_TM, D_MODEL), F32),
                        pltpu.VMEM((D_MODEL, 2 * D_FF), BF16), pltpu.VMEM((D_FF, D_MODEL), BF16)],
        compiler_params=_cparams(("arbitrary", "arbitrary")),
        name="mix",
    )(o_att, o_ret, o_four, p_gate, p_gate, p_gate, x2d, mod, mod, mod, mod, norm2,
      w_att, w_ret, w_four, w_out, w_up, w_up, conv_w, conv_b, w_down)


def _rope_tables(n_tok):
    rows = n_tok // GRID_W
    row_id = jnp.repeat(jnp.arange(rows), GRID_W).astype(F32)
    col_id = jnp.tile(jnp.arange(GRID_W), rows).astype(F32)
    n_freq = HEAD_DIM // 4
    inv = ROPE_THETA ** (-jnp.arange(n_freq, dtype=F32) / n_freq)
    ang = jnp.concatenate([row_id[None, :] * inv[:, None], col_id[None, :] * inv[:, None]], axis=0)
    return jnp.cos(ang), jnp.sin(ang)


def kernel(x_prompt, x_sample, cache_k, cache_v, state_ret_fwd, state_ret_bwd, c, c_ctx, w_ada, b_ada, norm1, w_in, q_norm, k_norm, ret_decay_f, ret_decay_b, ret_norm, w_br_att, w_br_ret, w_br_four, w_out, norm2, w_up, conv_w, conv_b, w_down):
    batch, seq, _ = x_prompt.shape
    dec_batch, dec_seq, _ = x_sample.shape
    past = cache_k.shape[2]

    cond_all = jnp.concatenate([c_ctx[None, :], c], axis=0)
    mod_all = _ada(cond_all, w_ada, b_ada)
    cos_t, sin_t = _rope_tables(dec_seq)
    ck = cache_k.reshape(dec_batch, DEPTH, past, ATT_KV)
    cv = cache_v.reshape(dec_batch, DEPTH, past, ATT_KV)

    xp = x_prompt.reshape(batch * seq, D_MODEL)
    xs = x_sample.reshape(dec_batch * dec_seq, D_MODEL)
    new_kv = (jnp.zeros((batch, DEPTH, seq, ATT_KV), F32),) * 2
    new_st = (jnp.zeros((batch, DEPTH, RET_HEADS, RET_DK, RET_DV), F32),) * 2
    for l in range(DEPTH):
        w_att, w_ret, w_four = (w_br_att[l].astype(BF16), w_br_ret[l].astype(BF16),
                                w_br_four[l].astype(BF16))
        w_o = w_out[l].astype(BF16)
        n1, n2 = norm1[l][None, :], norm2[l][None, :]
        qn, kn = q_norm[l][:, None], k_norm[l][:, None]
        rn = ret_norm[l][None, :]
        dec = _ret_tables(jnp.broadcast_to(
            jnp.concatenate([ret_decay_f[l], ret_decay_b[l]])[:, None].astype(F32), (2 * RET_HEADS, 128)))
        cw, cb = conv_w[l], conv_b[l][None, :]
        mod_ctx = mod_all[l, 0:1].reshape(1, 1, 6 * D_MODEL)
        mod_lat = mod_all[l, 1:].reshape(dec_batch, 1, 6 * D_MODEL)

        p_act, p_gate = _in_proj(xp, mod_ctx, batch * seq, n1, w_in, l)
        o_att, *new_kv = _att_ctx(p_act, seq, qn, kn, l, new_kv)
        o_ret, *new_st = _retention(p_act, seq, dec, rn, l, new_states=new_st)
        o_four = _fourier(p_act, seq)
        xp = _mix(o_att, o_ret, o_four, p_gate, xp, seq, mod_ctx, batch * seq,
                  w_att, w_ret, w_four, w_o, n2, w_up, cw, cb, w_down, l)

        p_act, p_gate = _in_proj(xs, mod_lat, dec_seq, n1, w_in, l)
        o_att = _att_lat(p_act, dec_seq, ck, cv, l, qn, kn, cos_t, sin_t)
        o_ret = _retention(p_act, dec_seq, dec, rn, l, states=(state_ret_fwd, state_ret_bwd))
        o_four = _fourier(p_act, dec_seq)
        xs = _mix(o_att, o_ret, o_four, p_gate, xs, dec_seq, mod_lat, dec_seq,
                  w_att, w_ret, w_four, w_o, n2, w_up, cw, cb, w_down, l)

    kv_shape = (batch, DEPTH, seq, N_KV_HEADS, HEAD_DIM)
    return (xp.reshape(batch, seq, D_MODEL), xs.reshape(dec_batch, dec_seq, D_MODEL),
            new_kv[0].reshape(kv_shape), new_kv[1].reshape(kv_shape), new_st[0], new_st[1])
```

```python
import functools

import numpy as np
import jax
import jax.numpy as jnp
from jax import lax
from jax.experimental import pallas as pl
from jax.experimental.pallas import tpu as pltpu

D_MODEL = 1024
DEPTH = 2
GRID_W = 64
HEAD_DIM = 64
N_HEADS = 8
N_KV_HEADS = 2
Q_PER_KV = N_HEADS // N_KV_HEADS
ATT_Q = N_HEADS * HEAD_DIM
ATT_KV = N_KV_HEADS * HEAD_DIM
RET_HEADS = 4
RET_DK = 64
RET_DV = 128
RET_V = RET_HEADS * RET_DV
FOURIER_GROUPS = 4
FOURIER_GROUP_DIM = 128
FOURIER_W = FOURIER_GROUPS * FOURIER_GROUP_DIM
D_FF = 2816
CHUNK = 128
Q_BLOCK = 256
ROPE_THETA = 10000.0
EPS = 1e-6
LOG2_E = 1.4426950408889634

F32 = jnp.float32
BF16 = jnp.bfloat16

OFF_QA, OFF_KA, OFF_VA = 0, 512, 640
OFF_QR, OFF_KR, OFF_VR, OFF_GR, OFF_UF = 768, 1024, 1280, 1792, 2304
W_IN_SPLIT = 2816
OFF_GATE = 3072
P_W = OFF_GATE + 3 * D_MODEL

VMEM_LIMIT = 56 * 1024 * 1024


def _cparams(sem):
    return pltpu.CompilerParams(dimension_semantics=sem, vmem_limit_bytes=VMEM_LIMIT)


def _dot(a, b):
    return jnp.dot(a, b, preferred_element_type=F32)


def _ada_kernel(cond_ref, w_ref, b_ref, o_ref):
    cnd = cond_ref[...]
    s = cnd * jax.nn.sigmoid(cnd)
    o_ref[0] = _dot(s.astype(BF16), w_ref[0].astype(BF16)) + b_ref[0]


def _ada(cond_all, w_ada, b_ada):
    n = cond_all.shape[0]
    tn = 1024
    return pl.pallas_call(
        _ada_kernel,
        grid=(DEPTH, 6 * D_MODEL // tn),
        in_specs=[
            pl.BlockSpec((n, D_MODEL), lambda l, j: (0, 0)),
            pl.BlockSpec((1, D_MODEL, tn), lambda l, j: (l, 0, j)),
            pl.BlockSpec((1, 1, tn), lambda l, j: (l, 0, j)),
        ],
        out_specs=pl.BlockSpec((1, n, tn), lambda l, j: (l, 0, j)),
        out_shape=jax.ShapeDtypeStruct((DEPTH, n, 6 * D_MODEL), F32),
        compiler_params=_cparams(("parallel", "parallel")),
        name="ada",
    )(cond_all, w_ada, b_ada.reshape(DEPTH, 1, 6 * D_MODEL))


IN_TN = 1024
IN_ROWS = 256
MAIN_STEPS = OFF_GATE // IN_TN


def _in_kernel(x_ref, sh_ref, sc_ref, g_ref, w_ref, om_ref, og_ref, h_scr, w_scr):
    i = pl.program_id(0)
    j = pl.program_id(1)
    cols = pl.ds(pl.multiple_of(j * IN_TN, IN_TN), IN_TN)

    @pl.when(i == 0)
    def _():
        w_scr[:, cols] = w_ref[0].astype(BF16)

    @pl.when(j == 0)
    def _():
        for b in range(x_ref.shape[0] // IN_ROWS):
            rows = slice(b * IN_ROWS, (b + 1) * IN_ROWS)
            x = x_ref[rows, :]
            r = lax.rsqrt(jnp.mean(x * x, axis=-1, keepdims=True) + EPS)
            h = ((x * r) * g_ref[...] * (1.0 + sc_ref[0]) + sh_ref[0]).astype(BF16)
            h_scr[rows, :] = h
            om_ref[rows, :] = _dot(h, w_scr[:, 0:IN_TN])

    @pl.when(jnp.logical_and(j > 0, j < MAIN_STEPS))
    def _():
        om_ref[...] = _dot(h_scr[...], w_scr[:, cols])

    @pl.when(j >= MAIN_STEPS)
    def _():
        og_ref[...] = (0.5 * _dot(h_scr[...], w_scr[:, cols])).astype(og_ref.dtype)


def _in_proj(x2d, mod, rows_per_mod, norm1, w_in, layer):
    n_tok = x2d.shape[0]
    tm, tn = 1024, IN_TN
    n_col = P_W // tn
    mod_idx = lambda i: (i * tm) // rows_per_mod

    def w_col(i, j):
        jj = jnp.where(i == 0, j, n_col - 1)
        col = jnp.where(jj < MAIN_STEPS, jj * tn, W_IN_SPLIT + (jj - MAIN_STEPS) * tn)
        return pl.multiple_of(col, 128)

    return pl.pallas_call(
        _in_kernel,
        grid=(n_tok // tm, n_col),
        in_specs=[
            pl.BlockSpec((tm, D_MODEL), lambda i, j: (i, 0)),
            pl.BlockSpec((1, 1, D_MODEL), lambda i, j: (mod_idx(i), 0, 0)),
            pl.BlockSpec((1, 1, D_MODEL), lambda i, j: (mod_idx(i), 0, 1)),
            pl.BlockSpec((1, D_MODEL), lambda i, j: (0, 0)),
            pl.BlockSpec((pl.Element(1), pl.Element(D_MODEL), pl.Element(tn)),
                         lambda i, j: (layer, 0, w_col(i, j))),
        ],
        out_specs=[
            pl.BlockSpec((tm, tn), lambda i, j: (i, jnp.minimum(j, MAIN_STEPS - 1))),
            pl.BlockSpec((tm, tn), lambda i, j: (i, jnp.maximum(j - MAIN_STEPS, 0))),
        ],
        out_shape=[jax.ShapeDtypeStruct((n_tok, OFF_GATE), F32),
                   jax.ShapeDtypeStruct((n_tok, P_W - OFF_GATE), BF16)],
        scratch_shapes=[pltpu.VMEM((tm, D_MODEL), BF16), pltpu.VMEM((D_MODEL, P_W), BF16)],
        compiler_params=_cparams(("arbitrary", "arbitrary")),
        name="in_proj",
    )(x2d, mod, mod, norm1, w_in)


def _norm_rope_heads_t(x_t, n_heads, g_col, cos_t=None, sin_t=None):
    quarter = HEAD_DIM // 4
    outs = []
    for h in range(n_heads):
        x = x_t[h * HEAD_DIM:(h + 1) * HEAD_DIM, :]
        r = lax.rsqrt(jnp.mean(x * x, axis=0, keepdims=True) + EPS)
        y = (x * r) * g_col
        if cos_t is not None:
            pieces = []
            for a in range(2):
                c = cos_t[a * quarter:(a + 1) * quarter]
                s = sin_t[a * quarter:(a + 1) * quarter]
                x1 = y[2 * a * quarter:(2 * a + 1) * quarter]
                x2 = y[(2 * a + 1) * quarter:(2 * a + 2) * quarter]
                pieces += [x1 * c - x2 * s, x2 * c + x1 * s]
            y = jnp.concatenate(pieces, axis=0)
        outs.append(y)
    return outs


SHIFT_LIMIT = 40.0


def _score_bound(k_heads_t, q_gain):
    k_sq = [jnp.max(jnp.sum(k * k, axis=0, keepdims=True)) for k in k_heads_t]
    k_norm = jnp.sqrt(functools.reduce(jnp.maximum, k_sq))
    q_norm = HEAD_DIM ** 0.5 * jnp.max(jnp.abs(q_gain))
    return 1.01 * (HEAD_DIM ** -0.5 * LOG2_E) * q_norm * k_norm


def _attend_t(q_heads_t, k_bf, v_t_bf, score_bound, store):
    tq = q_heads_t[0].shape[1]
    cols = Q_PER_KV * tq
    zeros = jnp.zeros((HEAD_DIM, cols), F32)
    q_kv = [jnp.concatenate(q_heads_t[kv * Q_PER_KV:(kv + 1) * Q_PER_KV], axis=1)
            * (HEAD_DIM ** -0.5 * LOG2_E) for kv in range(N_KV_HEADS)]
    rhs = jnp.concatenate([jnp.concatenate([q_kv[0], zeros], axis=1),
                           jnp.concatenate([zeros, q_kv[1]], axis=1)], axis=0).astype(BF16)
    def finish(shift):
        s_all = _dot(k_bf, rhs)
        out_rows = []
        for kv in range(N_KV_HEADS):
            s_t = s_all[:, kv * cols:(kv + 1) * cols]
            m = jnp.max(s_t, axis=0, keepdims=True) if shift is None else shift
            e = jnp.exp2(s_t - m)
            inv = 1.0 / jnp.sum(e, axis=0, keepdims=True)
            o_t = _dot(v_t_bf[kv * HEAD_DIM:(kv + 1) * HEAD_DIM, :], e.astype(BF16)) * inv
            out_rows += [o_t[:, g * tq:(g + 1) * tq] for g in range(Q_PER_KV)]
        store(jnp.concatenate(out_rows, axis=0))

    if score_bound is None:
        finish(None)
        return

    @pl.when(score_bound <= SHIFT_LIMIT)
    def _():
        finish(score_bound)

    @pl.when(jnp.logical_not(score_bound <= SHIFT_LIMIT))
    def _():
        finish(None)


def _att_ctx_kernel(q_ref, k_ref, v_ref, qn_ref, kn_ref, *rest, slab):
    o_ref, ko_ref, vo_ref = rest[-3:]
    seq = ko_ref.shape[2]
    if len(rest) == 3:
        for other in range(ko_ref.shape[1]):
            if other != slab:
                ko_ref[:, other] = jnp.zeros_like(ko_ref[:, other])
                vo_ref[:, other] = jnp.zeros_like(vo_ref[:, other])
    for sq in range(ko_ref.shape[0]):
        rows = slice(sq * seq, (sq + 1) * seq)
        k_heads = _norm_rope_heads_t(k_ref[rows, :].T, N_KV_HEADS, kn_ref[...])
        k_n = jnp.concatenate(k_heads, axis=0).T
        ko_ref[sq, slab] = k_n
        v = v_ref[rows, :]
        vo_ref[sq, slab] = v
        q_heads = _norm_rope_heads_t(q_ref[rows, :].T, N_HEADS, qn_ref[...])

        def store(o_t, rows=rows):
            o_ref[rows, :] = o_t.T.astype(o_ref.dtype)

        _attend_t(q_heads, k_n.astype(BF16), v.T.astype(BF16), None, store)


ATT_CTX_ROWS = 1024


def _att_ctx(p_act, seq, q_norm, k_norm, layer, caches=None):
    n_tok = p_act.shape[0]
    batch = n_tok // seq
    rows = max(seq, ATT_CTX_ROWS)
    blk = rows // seq
    in_specs = [
        pl.BlockSpec((rows, ATT_Q), lambda b: (b, OFF_QA // ATT_Q)),
        pl.BlockSpec((rows, ATT_KV), lambda b: (b, OFF_KA // ATT_KV)),
        pl.BlockSpec((rows, ATT_KV), lambda b: (b, OFF_VA // ATT_KV)),
        pl.BlockSpec((HEAD_DIM, 1), lambda b: (0, 0)),
        pl.BlockSpec((HEAD_DIM, 1), lambda b: (0, 0)),
    ]
    args = [p_act, p_act, p_act, q_norm, k_norm]
    if caches is None:
        cache_spec = pl.BlockSpec((blk, DEPTH, seq, ATT_KV), lambda b: (b, 0, 0, 0))
        slab, aliases = layer, {}
    else:
        cache_spec = pl.BlockSpec((blk, 1, seq, ATT_KV), lambda b: (b, layer, 0, 0))
        slab, aliases = 0, {len(args): 1, len(args) + 1: 2}
        in_specs += [pl.BlockSpec(memory_space=pl.ANY)] * 2
        args += list(caches)
    cache_shape = jax.ShapeDtypeStruct((batch, DEPTH, seq, ATT_KV), F32)
    return pl.pallas_call(
        functools.partial(_att_ctx_kernel, slab=slab),
        grid=(n_tok // rows,),
        in_specs=in_specs,
        out_specs=[pl.BlockSpec((rows, ATT_Q), lambda b: (b, 0)), cache_spec, cache_spec],
        out_shape=[jax.ShapeDtypeStruct((n_tok, ATT_Q), BF16), cache_shape, cache_shape],
        input_output_aliases=aliases,
        compiler_params=_cparams(("parallel",)),
        name="att_ctx",
    )(*args)


def _att_lat_kernel(q_ref, k_ref, v_ref, ck_ref, cv_ref, qn_ref, kn_ref,
                    cq_ref, sq_ref, ck_t_ref, sk_t_ref, o_ref, kf_scr, vt_scr, bound_scr, *, seq):
    @pl.when(pl.program_id(1) == 0)
    def _():
        k_heads = _norm_rope_heads_t(k_ref[...].T, N_KV_HEADS, kn_ref[...], ck_t_ref[...], sk_t_ref[...])
        kf_scr[0:seq, :] = jnp.concatenate(k_heads, axis=0).T.astype(BF16)
        past_k = ck_ref[0, 0]
        kf_scr[seq:, :] = past_k.astype(BF16)
        vt_scr[:, 0:seq] = v_ref[...].T.astype(BF16)
        vt_scr[:, seq:] = cv_ref[0, 0].T.astype(BF16)
        past_t = past_k.T
        past_heads = [past_t[kv * HEAD_DIM:(kv + 1) * HEAD_DIM] for kv in range(N_KV_HEADS)]
        bound_scr[0] = _score_bound(k_heads + past_heads, qn_ref[...])

    q_heads = _norm_rope_heads_t(q_ref[...].T, N_HEADS, qn_ref[...], cq_ref[...], sq_ref[...])

    def store(o_t):
        o_ref[...] = o_t.T.astype(o_ref.dtype)

    _attend_t(q_heads, kf_scr[...], vt_scr[...], bound_scr[0], store)


def _att_lat(p_act, seq, cache_k, cache_v, layer, q_norm, k_norm, cos_t, sin_t):
    n_tok = p_act.shape[0]
    nb = seq // Q_BLOCK
    past = cache_k.shape[2]
    return pl.pallas_call(
        functools.partial(_att_lat_kernel, seq=seq),
        grid=(n_tok // seq, nb),
        in_specs=[
            pl.BlockSpec((Q_BLOCK, ATT_Q), lambda b, i: (b * nb + i, OFF_QA // ATT_Q)),
            pl.BlockSpec((seq, ATT_KV), lambda b, i: (b, OFF_KA // ATT_KV)),
            pl.BlockSpec((seq, ATT_KV), lambda b, i: (b, OFF_VA // ATT_KV)),
            pl.BlockSpec((1, 1, past, ATT_KV), lambda b, i: (b, layer, 0, 0)),
            pl.BlockSpec((1, 1, past, ATT_KV), lambda b, i: (b, layer, 0, 0)),
            pl.BlockSpec((HEAD_DIM, 1), lambda b, i: (0, 0)),
            pl.BlockSpec((HEAD_DIM, 1), lambda b, i: (0, 0)),
            pl.BlockSpec((HEAD_DIM // 2, Q_BLOCK), lambda b, i: (0, i)),
            pl.BlockSpec((HEAD_DIM // 2, Q_BLOCK), lambda b, i: (0, i)),
            pl.BlockSpec((HEAD_DIM // 2, seq), lambda b, i: (0, 0)),
            pl.BlockSpec((HEAD_DIM // 2, seq), lambda b, i: (0, 0)),
        ],
        out_specs=pl.BlockSpec((Q_BLOCK, ATT_Q), lambda b, i: (b * nb + i, 0)),
        out_shape=jax.ShapeDtypeStruct((n_tok, ATT_Q), BF16),
        scratch_shapes=[pltpu.VMEM((seq + past, ATT_KV), BF16),
                        pltpu.VMEM((ATT_KV, seq + past), BF16),
                        pltpu.SMEM((1,), F32)],
        compiler_params=_cparams(("parallel", "arbitrary")),
        name="att_lat",
    )(p_act, p_act, p_act, cache_k, cache_v, q_norm, k_norm, cos_t, sin_t, cos_t, sin_t)


RET_ROWS = 1024


def _log_sigmoid(d):
    return jnp.minimum(d, 0.0) - jnp.log1p(jnp.exp(-jnp.abs(d)))


TAB_MASK, TAB_QF, TAB_QB, TAB_KF, TAB_KB, TAB_C, N_TAB = 0, 2, 4, 6, 7, 8, 9
TAB_CF, TAB_CB = 0, 2


def _ret_tables_kernel(dec_ref, tab_ref):
    hp = pl.program_id(0)
    ii = lax.broadcasted_iota(jnp.int32, (CHUNK, CHUNK), 0)
    jj = lax.broadcasted_iota(jnp.int32, (CHUNK, CHUNK), 1)
    rel = (ii - jj).astype(F32)
    row = ii.astype(F32)
    lane = jj.astype(F32)
    lgf = [_log_sigmoid(dec_ref[pl.ds(2 * hp + t, 1), :]) for t in range(2)]
    lgb = [_log_sigmoid(dec_ref[pl.ds(RET_HEADS + 2 * hp + t, 1), :]) for t in range(2)]
    for t in range(2):
        tab_ref[0, TAB_MASK + t] = jnp.where(
            rel > 0, jnp.exp(jnp.maximum(rel, 0.0) * lgf[t]),
            jnp.where(rel < 0, jnp.exp(jnp.maximum(-rel, 0.0) * lgb[t]), 2.0))
        tab_ref[0, TAB_QF + t] = jnp.exp((row + 1.0) * lgf[t])
        tab_ref[0, TAB_QB + t] = jnp.exp((CHUNK - row) * lgb[t])
    tab_ref[0, TAB_KF] = jnp.exp((CHUNK - 1.0 - lane) * jnp.where(ii < RET_DK, lgf[0], lgf[1]))
    tab_ref[0, TAB_KB] = jnp.exp(lane * jnp.where(ii < RET_DK, lgb[0], lgb[1]))
    c_rows = jnp.where(ii == TAB_CF, lgf[0], jnp.where(ii == TAB_CF + 1, lgf[1],
                       jnp.where(ii == TAB_CB, lgb[0], lgb[1])))
    tab_ref[0, TAB_C] = jnp.exp(CHUNK * c_rows)


def _ret_tables(dec):
    return pl.pallas_call(
        _ret_tables_kernel,
        grid=(RET_HEADS // 2,),
        in_specs=[pl.BlockSpec((8, 128), lambda p: (0, 0))],
        out_specs=pl.BlockSpec((1, N_TAB, CHUNK, CHUNK), lambda p: (p, 0, 0, 0)),
        out_shape=jax.ShapeDtypeStruct((RET_HEADS // 2, N_TAB, CHUNK, CHUNK), F32),
        compiler_params=_cparams(("parallel",)),
        name="ret_tables",
    )(dec)


def _ret_kernel(q_ref, k_ref, v_ref, g_ref, tab_ref, rn_ref, *rest, seq, has_state, slab):
    if has_state:
        s0f_ref, s0b_ref, o_ref = rest
    else:
        o_ref, sf_ref, sb_ref = rest[-3:]
        if len(rest) == 3:
            for other in range(sf_ref.shape[1]):
                if other != slab:
                    sf_ref[:, other] = jnp.zeros_like(sf_ref[:, other])
                    sb_ref[:, other] = jnp.zeros_like(sb_ref[:, other])
    n_chunks = q_ref.shape[0] // CHUNK
    jj = lax.broadcasted_iota(jnp.int32, (CHUNK, CHUNK), 1)
    mask2 = jnp.concatenate([tab_ref[0, TAB_MASK], tab_ref[0, TAB_MASK + 1]], axis=0)
    qdec_f = [tab_ref[0, TAB_QF + t] for t in range(2)]
    qdec_b = [tab_ref[0, TAB_QB + t] for t in range(2)]
    cdec_f = [tab_ref[0, TAB_C, TAB_CF + t:TAB_CF + t + 1, :] for t in range(2)]
    cdec_b = [tab_ref[0, TAB_C, TAB_CB + t:TAB_CB + t + 1, :] for t in range(2)]
    kdec_f = tab_ref[0, TAB_KF]
    kdec_b = tab_ref[0, TAB_KB]

    k_t = (k_ref[...] * (RET_DK ** -0.5)).T
    first_head = jj < RET_DK

    def chunk(c):
        return slice(c * CHUNK, (c + 1) * CHUNK)

    kv_f = [[None] * n_chunks for _ in range(2)]
    kv_b = [[None] * n_chunks for _ in range(2)]
    for c in range(n_chunks):
        k_c = k_t[:, chunk(c)]
        kd = jnp.concatenate([k_c * kdec_f, k_c * kdec_b], axis=0).astype(BF16)
        kv = _dot(kd, v_ref[chunk(c), :].astype(BF16))
        for t in range(2):
            kv_f[t][c] = kv[t * RET_DK:(t + 1) * RET_DK, t * RET_DV:(t + 1) * RET_DV]
            kv_b[t][c] = kv[CHUNK + t * RET_DK:CHUNK + (t + 1) * RET_DK, t * RET_DV:(t + 1) * RET_DV]

    st_f = [[None] * n_chunks for _ in range(2)]
    st_b = [[None] * n_chunks for _ in range(2)]
    per_seq = seq // CHUNK
    for t in range(2):
        for sq in range(n_chunks // per_seq):
            own = range(sq * per_seq, (sq + 1) * per_seq)
            if has_state:
                s_f = s0f_ref[sq, 0, t]
                s_b = s0b_ref[sq, 0, t]
            else:
                s_f = jnp.zeros((RET_DK, RET_DV), F32)
                s_b = s_f
            for c in own:
                st_f[t][c] = s_f
                s_f = s_f * cdec_f[t] + kv_f[t][c]
            for c in reversed(own):
                st_b[t][c] = s_b
                s_b = s_b * cdec_b[t] + kv_b[t][c]
            if not has_state:
                sf_ref[sq, slab, t] = s_f
                sb_ref[sq, slab, t] = s_b

    for c in range(n_chunks):
        q_c = q_ref[chunk(c), :]
        qm = jnp.concatenate([jnp.where(first_head, q_c, 0.0), jnp.where(first_head, 0.0, q_c)],
                             axis=0).astype(BF16)
        att = (_dot(qm, k_t[:, chunk(c)].astype(BF16)) * mask2).astype(BF16)
        states = jnp.concatenate(
            [jnp.concatenate([st_f[t][c], st_b[t][c]], axis=1) for t in range(2)], axis=0)
        qs = _dot(qm, states.astype(BF16))
        v_c = v_ref[chunk(c), :].astype(BF16)
        for t in range(2):
            rows = slice(t * CHUNK, (t + 1) * CHUNK)
            vsl = slice(t * RET_DV, (t + 1) * RET_DV)
            o = (_dot(att[rows], v_c[:, vsl]) + qs[rows, :RET_DV] * qdec_f[t]
                 + qs[rows, RET_DV:] * qdec_b[t])
            o = (o * lax.rsqrt(jnp.mean(o * o, axis=-1, keepdims=True) + EPS)) * rn_ref[...]
            hg = 0.5 * g_ref[chunk(c), vsl]
            o_ref[chunk(c), vsl] = (o * (hg * (1.0 + jnp.tanh(hg)))).astype(o_ref.dtype)


def _retention(p_act, seq, tables, ret_norm, layer, states=None, new_states=None):
    n_tok = p_act.shape[0]
    n_seq = n_tok // seq
    has_state = states is not None
    pair_qk, pair_v = 2 * RET_DK, 2 * RET_DV
    rows = max(seq, RET_ROWS)
    blk_seqs = rows // seq
    in_specs = [
        pl.BlockSpec((rows, pair_qk), lambda p, s: (s, OFF_QR // pair_qk + p)),
        pl.BlockSpec((rows, pair_qk), lambda p, s: (s, OFF_KR // pair_qk + p)),
        pl.BlockSpec((rows, pair_v), lambda p, s: (s, OFF_VR // pair_v + p)),
        pl.BlockSpec((rows, pair_v), lambda p, s: (s, OFF_GR // pair_v + p)),
        pl.BlockSpec((1, N_TAB, CHUNK, CHUNK), lambda p, s: (p, 0, 0, 0)),
        pl.BlockSpec((1, RET_DV), lambda p, s: (0, 0)),
    ]
    args = [p_act, p_act, p_act, p_act, tables, ret_norm]
    o_spec = pl.BlockSpec((rows, pair_v), lambda p, s: (s, p))
    o_shape = jax.ShapeDtypeStruct((n_tok, RET_V), BF16)
    st_spec = pl.BlockSpec((blk_seqs, 1, 2, RET_DK, RET_DV), lambda p, s: (s, layer, p, 0, 0))
    aliases, slab = {}, 0
    if has_state:
        in_specs += [st_spec, st_spec]
        args += list(states)
        out_specs, out_shape = o_spec, o_shape
    else:
        if new_states is None:
            st_spec = pl.BlockSpec((blk_seqs, DEPTH, 2, RET_DK, RET_DV), lambda p, s: (s, 0, p, 0, 0))
            slab = layer
        else:
            in_specs += [pl.BlockSpec(memory_space=pl.ANY)] * 2
            aliases = {len(args): 1, len(args) + 1: 2}
            args += list(new_states)
        st_shape = jax.ShapeDtypeStruct((n_seq, DEPTH, RET_HEADS, RET_DK, RET_DV), F32)
        out_specs, out_shape = [o_spec, st_spec, st_spec], [o_shape, st_shape, st_shape]
    return pl.pallas_call(
        functools.partial(_ret_kernel, seq=seq, has_state=has_state, slab=slab),
        grid=(RET_HEADS // 2, n_tok // rows),
        in_specs=in_specs,
        out_specs=out_specs,
        out_shape=out_shape,
        input_output_aliases=aliases,
        compiler_params=_cparams(("parallel", "parallel")),
        name="retention_lat" if has_state else "retention_ctx",
    )(*args)


FOURIER_ROWS = 1024


def _four_kernel(u0_ref, u1_ref, wc_ref, csn_ref, o_ref):
    gd = FOURIER_GROUP_DIM
    tc, ts = [], []
    for g in range(FOURIER_GROUPS):
        u_ref = (u0_ref, u1_ref)[g // 2]
        u_g = u_ref[:, (g % 2) * gd:(g % 2 + 1) * gd].astype(BF16)
        t = _dot(u_g, wc_ref[...])
        tc.append(t[:, :gd])
        ts.append(t[:, gd:])
    tc = jnp.concatenate(tc, axis=1).astype(BF16)
    ts = jnp.concatenate(ts, axis=1).astype(BF16)
    seq = csn_ref.shape[0]
    for sq in range(o_ref.shape[0] // seq):
        rows = slice(sq * seq, (sq + 1) * seq)
        t_sq = jnp.concatenate([tc[rows], ts[rows]], axis=0)
        o_ref[rows, :] = _dot(csn_ref[...], t_sq).astype(o_ref.dtype)


def _dft_tables(n):
    k = np.arange(n, dtype=np.int64)
    ang = 2.0 * np.pi * ((k[:, None] * k[None, :]) % n).astype(np.float64) / n
    scale = 1.0 / np.sqrt(n)
    return np.cos(ang) * scale, np.sin(ang) * scale


def _fourier(p_act, seq):
    n_tok = p_act.shape[0]
    cc, sc = _dft_tables(FOURIER_GROUP_DIM)
    wc = jnp.asarray(np.concatenate([cc, -sc], axis=1), F32).astype(BF16)
    cn, sn = _dft_tables(seq)
    csn = jnp.asarray(np.concatenate([cn, sn], axis=1), F32).astype(BF16)
    const = lambda shape: pl.BlockSpec(shape, lambda s: (0, 0))
    half = FOURIER_W // 2
    rows = max(seq, FOURIER_ROWS)
    return pl.pallas_call(
        _four_kernel,
        grid=(n_tok // rows,),
        in_specs=[
            pl.BlockSpec((rows, half), lambda s: (s, OFF_UF // half)),
            pl.BlockSpec((rows, half), lambda s: (s, OFF_UF // half + 1)),
            const((FOURIER_GROUP_DIM, 2 * FOURIER_GROUP_DIM)),
            const((seq, 2 * seq)),
        ],
        out_specs=pl.BlockSpec((rows, FOURIER_W), lambda s: (s, 0)),
        out_shape=jax.ShapeDtypeStruct((n_tok, FOURIER_W), BF16),
        compiler_params=_cparams(("parallel",)),
        name="fourier",
    )(p_act, p_act, wc, csn)


MIX_TM = 1024
MERGE_STEPS = 4
MERGE_TM = MIX_TM // MERGE_STEPS
FF_TILE = 256
FF_STEPS = D_FF // FF_TILE
FF_ROWS = 256


def _mix_kernel(oa_ref, or_ref, of_ref, ga_ref, gr_ref, gf_ref, x_ref, g1_ref, sh_ref, sc_ref,
                g2_ref, n2_ref, wba_ref, wbr_ref, wbf_ref, wo_ref, wa_ref, wv_ref, cw_ref, cb_ref,
                wd_ref, o_ref, h_scr, acc_scr, wu_scr, wd_scr, *, seq):
    i = pl.program_id(0)
    j = pl.program_id(1)
    chunk = jnp.maximum(j - MERGE_STEPS, 0)
    a_cols = pl.ds(pl.multiple_of(chunk * FF_TILE, FF_TILE), FF_TILE)
    v_cols = pl.ds(pl.multiple_of((FF_STEPS + chunk) * FF_TILE, FF_TILE), FF_TILE)

    @pl.when(j < MERGE_STEPS)
    def _():
        gate = lambda ref: jnp.tanh(ref[...].astype(F32)) + 1.0
        dst = pl.ds(pl.multiple_of(j * MERGE_TM, MERGE_TM), MERGE_TM)
        merged = (gate(ga_ref) * _dot(oa_ref[...], wba_ref[...])
                  + gate(gr_ref) * _dot(or_ref[...], wbr_ref[...])
                  + gate(gf_ref) * _dot(of_ref[...], wbf_ref[...]))
        x1 = x_ref[...] + g1_ref[0] * _dot((0.5 * merged).astype(BF16), wo_ref[...])
        o_ref[dst, :] = x1
        r = lax.rsqrt(jnp.mean(x1 * x1, axis=-1, keepdims=True) + EPS)
        h = (x1 * r) * n2_ref[...] * (1.0 + sc_ref[0]) + sh_ref[0]
        h_scr[dst, :] = h.astype(BF16)
        acc_scr[dst, :] = jnp.zeros((MERGE_TM, D_MODEL), F32)

    @pl.when(jnp.logical_and(i == 0, j >= MERGE_STEPS))
    def _():
        wu_scr[:, a_cols] = wa_ref[0].astype(BF16)
        wu_scr[:, v_cols] = wv_ref[0].astype(BF16)
        wd_scr[a_cols, :] = wd_ref[0].astype(BF16)

    @pl.when(j >= MERGE_STEPS)
    def _():
        blocks = [slice(b * FF_ROWS, (b + 1) * FF_ROWS) for b in range(MIX_TM // FF_ROWS)]
        wa, wv, wd = wu_scr[:, a_cols], wu_scr[:, v_cols], wd_scr[a_cols, :]
        a = jnp.concatenate([_dot(h_scr[rows, :], wa) for rows in blocks], axis=0)
        val = [_dot(h_scr[rows, :], wv) for rows in blocks]
        pos = lax.broadcasted_iota(jnp.int32, a.shape, 0) % seq
        prev = jnp.where(pos == 0, 0.0, pltpu.roll(a, 1, 0))
        nxt = jnp.where(pos == seq - 1, 0.0, pltpu.roll(a, MIX_TM - 1, 0))
        cw = cw_ref[...]
        ac = prev * cw[0:1] + a * cw[1:2] + nxt * cw[2:3] + cb_ref[...]
        for b, rows in enumerate(blocks):
            act = jax.nn.gelu(ac[rows]) * val[b]
            acc_scr[rows, :] += _dot(act.astype(BF16), wd)

    @pl.when(j == pl.num_programs(1) - 1)
    def _():
        o_ref[...] = o_ref[...] + g2_ref[0] * acc_scr[...]


def _mix(o_att, o_ret, o_four, p_gate, x2d, seq, mod, rows_per_mod, w_att, w_ret, w_four, w_out,
         norm2, w_up, conv_w, conv_b, w_down, layer):
    n_tok = x2d.shape[0]
    mod_idx = lambda i: (i * MIX_TM) // rows_per_mod
    sub = lambda i, j: i * MERGE_STEPS + jnp.minimum(j, MERGE_STEPS - 1)
    ff = lambda j: jnp.maximum(j - MERGE_STEPS, 0)
    ff_w = lambda i, j: jnp.where(i == 0, ff(j), FF_STEPS - 1)
    br = lambda: pl.BlockSpec((MERGE_TM, 512), lambda i, j: (sub(i, j), 0))
    gate = lambda k: pl.BlockSpec((MERGE_TM, D_MODEL), lambda i, j: (sub(i, j), k))
    modv = lambda k: pl.BlockSpec((1, 1, D_MODEL), lambda i, j: (mod_idx(i), 0, k))
    const = lambda shape: pl.BlockSpec(shape, lambda i, j: (0, 0))
    return pl.pallas_call(
        functools.partial(_mix_kernel, seq=seq),
        grid=(n_tok // MIX_TM, MERGE_STEPS + FF_STEPS),
        in_specs=[
            br(), br(), br(), gate(0), gate(1), gate(2),
            pl.BlockSpec((MERGE_TM, D_MODEL), lambda i, j: (sub(i, j), 0)),
            modv(2), modv(3), modv(4), modv(5),
            const((1, D_MODEL)),
            const((512, D_MODEL)), const((512, D_MODEL)), const((512, D_MODEL)),
            const((D_MODEL, D_MODEL)),
            pl.BlockSpec((1, D_MODEL, FF_TILE), lambda i, j: (layer, 0, ff_w(i, j))),
            pl.BlockSpec((1, D_MODEL, FF_TILE), lambda i, j: (layer, 0, FF_STEPS + ff_w(i, j))),
            pl.BlockSpec((3, FF_TILE), lambda i, j: (0, ff(j))),
            pl.BlockSpec((1, FF_TILE), lambda i, j: (0, ff(j))),
            pl.BlockSpec((1, FF_TILE, D_MODEL), lambda i, j: (layer, ff_w(i, j), 0)),
        ],
        out_specs=pl.BlockSpec((MIX_TM, D_MODEL), lambda i, j: (i, 0)),
        out_shape=jax.ShapeDtypeStruct((n_tok, D_MODEL), F32),
        scratch_shapes=[pltpu.VMEM((MIX_TM, D_MODEL), BF16), pltpu.VMEM((MIX_TM, D_MODEL), F32),
                        pltpu.VMEM((D_MODEL, 2 * D_FF), BF16), pltpu.VMEM((D_FF, D_MODEL), BF16)],
        compiler_params=_cparams(("arbitrary", "arbitrary")),
        name="mix",
    )(o_att, o_ret, o_four, p_gate, p_gate, p_gate, x2d, mod, mod, mod, mod, norm2,
      w_att, w_ret, w_four, w_out, w_up, w_up, conv_w, conv_b, w_down)


def _rope_tables(n_tok):
    rows = n_tok // GRID_W
    row_id = jnp.repeat(jnp.arange(rows), GRID_W).astype(F32)
    col_id = jnp.tile(jnp.arange(GRID_W), rows).astype(F32)
    n_freq = HEAD_DIM // 4
    inv = ROPE_THETA ** (-jnp.arange(n_freq, dtype=F32) / n_freq)
    ang = jnp.concatenate([row_id[None, :] * inv[:, None], col_id[None, :] * inv[:, None]], axis=0)
    return jnp.cos(ang), jnp.sin(ang)


def kernel(x_prompt, x_sample, cache_k, cache_v, state_ret_fwd, state_ret_bwd, c, c_ctx, w_ada, b_ada, norm1, w_in, q_norm, k_norm, ret_decay_f, ret_decay_b, ret_norm, w_br_att, w_br_ret, w_br_four, w_out, norm2, w_up, conv_w, conv_b, w_down):
    batch, seq, _ = x_prompt.shape
    dec_batch, dec_seq, _ = x_sample.shape
    past = cache_k.shape[2]

    cond_all = jnp.concatenate([c_ctx[None, :], c], axis=0)
    mod_all = _ada(cond_all, w_ada, b_ada)
    cos_t, sin_t = _rope_tables(dec_seq)
    ck = cache_k.reshape(dec_batch, DEPTH, past, ATT_KV)
    cv = cache_v.reshape(dec_batch, DEPTH, past, ATT_KV)

    xp = x_prompt.reshape(batch * seq, D_MODEL)
    xs = x_sample.reshape(dec_batch * dec_seq, D_MODEL)
    new_kv = new_st = None
    for l in range(DEPTH):
        w_att, w_ret, w_four = (w_br_att[l].astype(BF16), w_br_ret[l].astype(BF16),
                                w_br_four[l].astype(BF16))
        w_o = w_out[l].astype(BF16)
        n1, n2 = norm1[l][None, :], norm2[l][None, :]
        qn, kn = q_norm[l][:, None], k_norm[l][:, None]
        rn = ret_norm[l][None, :]
        dec = _ret_tables(jnp.broadcast_to(
            jnp.concatenate([ret_decay_f[l], ret_decay_b[l]])[:, None].astype(F32), (2 * RET_HEADS, 128)))
        cw, cb = conv_w[l], conv_b[l][None, :]
        mod_ctx = mod_all[l, 0:1].reshape(1, 1, 6 * D_MODEL)
        mod_lat = mod_all[l, 1:].reshape(dec_batch, 1, 6 * D_MODEL)

        p_act, p_gate = _in_proj(xp, mod_ctx, batch * seq, n1, w_in, l)
        o_att, *new_kv = _att_ctx(p_act, seq, qn, kn, l, new_kv)
        o_ret, *new_st = _retention(p_act, seq, dec, rn, l, new_states=new_st)
        o_four = _fourier(p_act, seq)
        xp = _mix(o_att, o_ret, o_four, p_gate, xp, seq, mod_ctx, batch * seq,
                  w_att, w_ret, w_four, w_o, n2, w_up, cw, cb, w_down, l)

        p_act, p_gate = _in_proj(xs, mod_lat, dec_seq, n1, w_in, l)
        o_att = _att_lat(p_act, dec_seq, ck, cv, l, qn, kn, cos_t, sin_t)
        o_ret = _retention(p_act, dec_seq, dec, rn, l, states=(state_ret_fwd, state_ret_bwd))
        o_four = _fourier(p_act, dec_seq)
        xs = _mix(o_att, o_ret, o_four, p_gate, xs, dec_seq, mod_lat, dec_seq,
                  w_att, w_ret, w_four, w_o, n2, w_up, cw, cb, w_down, l)

    kv_shape = (batch, DEPTH, seq, N_KV_HEADS, HEAD_DIM)
    return (xp.reshape(batch, seq, D_MODEL), xs.reshape(dec_batch, dec_seq, D_MODEL),
            new_kv[0].reshape(kv_shape), new_kv[1].reshape(kv_shape), new_st[0], new_st[1])
```

```python
import functools

import numpy as np
import jax
import jax.numpy as jnp
from jax import lax
from jax.experimental import pallas as pl
from jax.experimental.pallas import tpu as pltpu

D_MODEL = 1024
DEPTH = 2
GRID_W = 64
HEAD_DIM = 64
N_HEADS = 8
N_KV_HEADS = 2
Q_PER_KV = N_HEADS // N_KV_HEADS
ATT_Q = N_HEADS * HEAD_DIM
ATT_KV = N_KV_HEADS * HEAD_DIM
RET_HEADS = 4
RET_DK = 64
RET_DV = 128
RET_V = RET_HEADS * RET_DV
FOURIER_GROUPS = 4
FOURIER_GROUP_DIM = 128
FOURIER_W = FOURIER_GROUPS * FOURIER_GROUP_DIM
D_FF = 2816
CHUNK = 128
Q_BLOCK = 256
ROPE_THETA = 10000.0
EPS = 1e-6
LOG2_E = 1.4426950408889634

F32 = jnp.float32
BF16 = jnp.bfloat16

OFF_QA, OFF_KA, OFF_VA = 0, 512, 640
OFF_QR, OFF_KR, OFF_VR, OFF_GR, OFF_UF = 768, 1024, 1280, 1792, 2304
W_IN_SPLIT = 2816
OFF_GATE = 3072
P_W = OFF_GATE + 3 * D_MODEL

VMEM_LIMIT = 56 * 1024 * 1024


def _cparams(sem):
    return pltpu.CompilerParams(dimension_semantics=sem, vmem_limit_bytes=VMEM_LIMIT)


def _dot(a, b):
    return jnp.dot(a, b, preferred_element_type=F32)


def _ada_kernel(cond_ref, w_ref, b_ref, o_ref):
    cnd = cond_ref[...]
    s = cnd * jax.nn.sigmoid(cnd)
    o_ref[0] = _dot(s.astype(BF16), w_ref[0].astype(BF16)) + b_ref[0]


def _ada(cond_all, w_ada, b_ada):
    n = cond_all.shape[0]
    tn = 1024
    return pl.pallas_call(
        _ada_kernel,
        grid=(DEPTH, 6 * D_MODEL // tn),
        in_specs=[
            pl.BlockSpec((n, D_MODEL), lambda l, j: (0, 0)),
            pl.BlockSpec((1, D_MODEL, tn), lambda l, j: (l, 0, j)),
            pl.BlockSpec((1, 1, tn), lambda l, j: (l, 0, j)),
        ],
        out_specs=pl.BlockSpec((1, n, tn), lambda l, j: (l, 0, j)),
        out_shape=jax.ShapeDtypeStruct((DEPTH, n, 6 * D_MODEL), F32),
        compiler_params=_cparams(("parallel", "parallel")),
        name="ada",
    )(cond_all, w_ada, b_ada.reshape(DEPTH, 1, 6 * D_MODEL))


IN_TN = 1024
IN_ROWS = 256
MAIN_STEPS = OFF_GATE // IN_TN


def _in_kernel(x_ref, sh_ref, sc_ref, g_ref, w_ref, om_ref, og_ref, h_scr, w_scr):
    i = pl.program_id(0)
    j = pl.program_id(1)
    cols = pl.ds(pl.multiple_of(j * IN_TN, IN_TN), IN_TN)

    @pl.when(i == 0)
    def _():
        w_scr[:, cols] = w_ref[0].astype(BF16)

    @pl.when(j == 0)
    def _():
        for b in range(x_ref.shape[0] // IN_ROWS):
            rows = slice(b * IN_ROWS, (b + 1) * IN_ROWS)
            x = x_ref[rows, :]
            r = lax.rsqrt(jnp.mean(x * x, axis=-1, keepdims=True) + EPS)
            h = ((x * r) * g_ref[...] * (1.0 + sc_ref[0]) + sh_ref[0]).astype(BF16)
            h_scr[rows, :] = h
            om_ref[rows, :] = _dot(h, w_scr[:, 0:IN_TN])

    @pl.when(jnp.logical_and(j > 0, j < MAIN_STEPS))
    def _():
        om_ref[...] = _dot(h_scr[...], w_scr[:, cols])

    @pl.when(j >= MAIN_STEPS)
    def _():
        og_ref[...] = (0.5 * _dot(h_scr[...], w_scr[:, cols])).astype(og_ref.dtype)


def _in_proj(x2d, mod, rows_per_mod, norm1, w_in, layer):
    n_tok = x2d.shape[0]
    tm, tn = 1024, IN_TN
    n_col = P_W // tn
    mod_idx = lambda i: (i * tm) // rows_per_mod

    def w_col(i, j):
        jj = jnp.where(i == 0, j, n_col - 1)
        col = jnp.where(jj < MAIN_STEPS, jj * tn, W_IN_SPLIT + (jj - MAIN_STEPS) * tn)
        return pl.multiple_of(col, 128)

    return pl.pallas_call(
        _in_kernel,
        grid=(n_tok // tm, n_col),
        in_specs=[
            pl.BlockSpec((tm, D_MODEL), lambda i, j: (i, 0)),
            pl.BlockSpec((1, 1, D_MODEL), lambda i, j: (mod_idx(i), 0, 0)),
            pl.BlockSpec((1, 1, D_MODEL), lambda i, j: (mod_idx(i), 0, 1)),
            pl.BlockSpec((1, D_MODEL), lambda i, j: (0, 0)),
            pl.BlockSpec((pl.Element(1), pl.Element(D_MODEL), pl.Element(tn)),
                         lambda i, j: (layer, 0, w_col(i, j))),
        ],
        out_specs=[
            pl.BlockSpec((tm, tn), lambda i, j: (i, jnp.minimum(j, MAIN_STEPS - 1))),
            pl.BlockSpec((tm, tn), lambda i, j: (i, jnp.maximum(j - MAIN_STEPS, 0))),
        ],
        out_shape=[jax.ShapeDtypeStruct((n_tok, OFF_GATE), F32),
                   jax.ShapeDtypeStruct((n_tok, P_W - OFF_GATE), BF16)],
        scratch_shapes=[pltpu.VMEM((tm, D_MODEL), BF16), pltpu.VMEM((D_MODEL, P_W), BF16)],
        compiler_params=_cparams(("arbitrary", "arbitrary")),
        name="in_proj",
    )(x2d, mod, mod, norm1, w_in)


def _norm_rope_heads_t(x_t, n_heads, g_col, cos_t=None, sin_t=None):
    quarter = HEAD_DIM // 4
    outs = []
    for h in range(n_heads):
        x = x_t[h * HEAD_DIM:(h + 1) * HEAD_DIM, :]
        r = lax.rsqrt(jnp.mean(x * x, axis=0, keepdims=True) + EPS)
        y = (x * r) * g_col
        if cos_t is not None:
            pieces = []
            for a in range(2):
                c = cos_t[a * quarter:(a + 1) * quarter]
                s = sin_t[a * quarter:(a + 1) * quarter]
                x1 = y[2 * a * quarter:(2 * a + 1) * quarter]
                x2 = y[(2 * a + 1) * quarter:(2 * a + 2) * quarter]
                pieces += [x1 * c - x2 * s, x2 * c + x1 * s]
            y = jnp.concatenate(pieces, axis=0)
        outs.append(y)
    return outs


SHIFT_LIMIT = 40.0


def _score_bound(k_heads_t, q_gain):
    k_sq = [jnp.max(jnp.sum(k * k, axis=0, keepdims=True)) for k in k_heads_t]
    k_norm = jnp.sqrt(functools.reduce(jnp.maximum, k_sq))
    q_norm = HEAD_DIM ** 0.5 * jnp.max(jnp.abs(q_gain))
    return 1.01 * (HEAD_DIM ** -0.5 * LOG2_E) * q_norm * k_norm


def _attend_t(q_heads_t, k_bf, v_t_bf, score_bound, store):
    tq = q_heads_t[0].shape[1]
    cols = Q_PER_KV * tq
    zeros = jnp.zeros((HEAD_DIM, cols), F32)
    q_kv = [jnp.concatenate(q_heads_t[kv * Q_PER_KV:(kv + 1) * Q_PER_KV], axis=1)
            * (HEAD_DIM ** -0.5 * LOG2_E) for kv in range(N_KV_HEADS)]
    rhs = jnp.concatenate([jnp.concatenate([q_kv[0], zeros], axis=1),
                           jnp.concatenate([zeros, q_kv[1]], axis=1)], axis=0).astype(BF16)
    def finish(shift):
        s_all = _dot(k_bf, rhs)
        out_rows = []
        for kv in range(N_KV_HEADS):
            s_t = s_all[:, kv * cols:(kv + 1) * cols]
            m = jnp.max(s_t, axis=0, keepdims=True) if shift is None else shift
            e = jnp.exp2(s_t - m)
            inv = 1.0 / jnp.sum(e, axis=0, keepdims=True)
            o_t = _dot(v_t_bf[kv * HEAD_DIM:(kv + 1) * HEAD_DIM, :], e.astype(BF16)) * inv
            out_rows += [o_t[:, g * tq:(g + 1) * tq] for g in range(Q_PER_KV)]
        store(jnp.concatenate(out_rows, axis=0))

    if score_bound is None:
        finish(None)
        return

    @pl.when(score_bound <= SHIFT_LIMIT)
    def _():
        finish(score_bound)

    @pl.when(jnp.logical_not(score_bound <= SHIFT_LIMIT))
    def _():
        finish(None)


def _att_ctx_kernel(q_ref, k_ref, v_ref, qn_ref, kn_ref, *rest, slab):
    o_ref, ko_ref, vo_ref = rest[-3:]
    seq = ko_ref.shape[2]
    if len(rest) == 3:
        for other in range(ko_ref.shape[1]):
            if other != slab:
                ko_ref[:, other] = jnp.zeros_like(ko_ref[:, other])
                vo_ref[:, other] = jnp.zeros_like(vo_ref[:, other])
    for sq in range(ko_ref.shape[0]):
        rows = slice(sq * seq, (sq + 1) * seq)
        k_heads = _norm_rope_heads_t(k_ref[rows, :].T, N_KV_HEADS, kn_ref[...])
        k_n = jnp.concatenate(k_heads, axis=0).T
        ko_ref[sq, slab] = k_n
        v = v_ref[rows, :]
        vo_ref[sq, slab] = v
        q_heads = _norm_rope_heads_t(q_ref[rows, :].T, N_HEADS, qn_ref[...])

        def store(o_t, rows=rows):
            o_ref[rows, :] = o_t.T.astype(o_ref.dtype)

        _attend_t(q_heads, k_n.astype(BF16), v.T.astype(BF16), None, store)


ATT_CTX_ROWS = 1024


def _att_ctx(p_act, seq, q_norm, k_norm, layer, caches=None):
    n_tok = p_act.shape[0]
    batch = n_tok // seq
    rows = max(seq, ATT_CTX_ROWS)
    blk = rows // seq
    in_specs = [
        pl.BlockSpec((rows, ATT_Q), lambda b: (b, OFF_QA // ATT_Q)),
        pl.BlockSpec((rows, ATT_KV), lambda b: (b, OFF_KA // ATT_KV)),
        pl.BlockSpec((rows, ATT_KV), lambda b: (b, OFF_VA // ATT_KV)),
        pl.BlockSpec((HEAD_DIM, 1), lambda b: (0, 0)),
        pl.BlockSpec((HEAD_DIM, 1), lambda b: (0, 0)),
    ]
    args = [p_act, p_act, p_act, q_norm, k_norm]
    if caches is None:
        cache_spec = pl.BlockSpec((blk, DEPTH, seq, ATT_KV), lambda b: (b, 0, 0, 0))
        slab, aliases = layer, {}
    else:
        cache_spec = pl.BlockSpec((blk, 1, seq, ATT_KV), lambda b: (b, layer, 0, 0))
        slab, aliases = 0, {len(args): 1, len(args) + 1: 2}
        in_specs += [pl.BlockSpec(memory_space=pl.ANY)] * 2
        args += list(caches)
    cache_shape = jax.ShapeDtypeStruct((batch, DEPTH, seq, ATT_KV), F32)
    return pl.pallas_call(
        functools.partial(_att_ctx_kernel, slab=slab),
        grid=(n_tok // rows,),
        in_specs=in_specs,
        out_specs=[pl.BlockSpec((rows, ATT_Q), lambda b: (b, 0)), cache_spec, cache_spec],
        out_shape=[jax.ShapeDtypeStruct((n_tok, ATT_Q), BF16), cache_shape, cache_shape],
        input_output_aliases=aliases,
        compiler_params=_cparams(("parallel",)),
        name="att_ctx",
    )(*args)


def _att_lat_kernel(q_ref, k_ref, v_ref, ck_ref, cv_ref, qn_ref, kn_ref,
                    cq_ref, sq_ref, ck_t_ref, sk_t_ref, o_ref, kf_scr, vt_scr, bound_scr, *, seq):
    @pl.when(pl.program_id(1) == 0)
    def _():
        k_heads = _norm_rope_heads_t(k_ref[...].T, N_KV_HEADS, kn_ref[...], ck_t_ref[...], sk_t_ref[...])
        kf_scr[0:seq, :] = jnp.concatenate(k_heads, axis=0).T.astype(BF16)
        past_k = ck_ref[0, 0]
        kf_scr[seq:, :] = past_k.astype(BF16)
        vt_scr[:, 0:seq] = v_ref[...].T.astype(BF16)
        vt_scr[:, seq:] = cv_ref[0, 0].T.astype(BF16)
        past_t = past_k.T
        past_heads = [past_t[kv * HEAD_DIM:(kv + 1) * HEAD_DIM] for kv in range(N_KV_HEADS)]
        bound_scr[0] = _score_bound(k_heads + past_heads, qn_ref[...])

    q_heads = _norm_rope_heads_t(q_ref[...].T, N_HEADS, qn_ref[...], cq_ref[...], sq_ref[...])

    def store(o_t):
        o_ref[...] = o_t.T.astype(o_ref.dtype)

    _attend_t(q_heads, kf_scr[...], vt_scr[...], bound_scr[0], store)


def _att_lat(p_act, seq, cache_k, cache_v, layer, q_norm, k_norm, cos_t, sin_t):
    n_tok = p_act.shape[0]
    nb = seq // Q_BLOCK
    past = cache_k.shape[2]
    return pl.pallas_call(
        functools.partial(_att_lat_kernel, seq=seq),
        grid=(n_tok // seq, nb),
        in_specs=[
            pl.BlockSpec((Q_BLOCK, ATT_Q), lambda b, i: (b * nb + i, OFF_QA // ATT_Q)),
            pl.BlockSpec((seq, ATT_KV), lambda b, i: (b, OFF_KA // ATT_KV)),
            pl.BlockSpec((seq, ATT_KV), lambda b, i: (b, OFF_VA // ATT_KV)),
            pl.BlockSpec((1, 1, past, ATT_KV), lambda b, i: (b, layer, 0, 0)),
            pl.BlockSpec((1, 1, past, ATT_KV), lambda b, i: (b, layer, 0, 0)),
            pl.BlockSpec((HEAD_DIM, 1), lambda b, i: (0, 0)),
            pl.BlockSpec((HEAD_DIM, 1), lambda b, i: (0, 0)),
            pl.BlockSpec((HEAD_DIM // 2, Q_BLOCK), lambda b, i: (0, i)),
            pl.BlockSpec((HEAD_DIM // 2, Q_BLOCK), lambda b, i: (0, i)),
            pl.BlockSpec((HEAD_DIM // 2, seq), lambda b, i: (0, 0)),
            pl.BlockSpec((HEAD_DIM // 2, seq), lambda b, i: (0, 0)),
        ],
        out_specs=pl.BlockSpec((Q_BLOCK, ATT_Q), lambda b, i: (b * nb + i, 0)),
        out_shape=jax.ShapeDtypeStruct((n_tok, ATT_Q), BF16),
        scratch_shapes=[pltpu.VMEM((seq + past, ATT_KV), BF16),
                        pltpu.VMEM((ATT_KV, seq + past), BF16),
                        pltpu.SMEM((1,), F32)],
        compiler_params=_cparams(("parallel", "arbitrary")),
        name="att_lat",
    )(p_act, p_act, p_act, cache_k, cache_v, q_norm, k_norm, cos_t, sin_t, cos_t, sin_t)


RET_ROWS = 1024


def _log_sigmoid(d):
    return jnp.minimum(d, 0.0) - jnp.log1p(jnp.exp(-jnp.abs(d)))


TAB_MASK, TAB_QF, TAB_QB, TAB_KF, TAB_KB, TAB_C, N_TAB = 0, 2, 4, 6, 7, 8, 9
TAB_CF, TAB_CB = 0, 2


def _ret_tables_kernel(dec_ref, tab_ref):
    hp = pl.program_id(0)
    ii = lax.broadcasted_iota(jnp.int32, (CHUNK, CHUNK), 0)
    jj = lax.broadcasted_iota(jnp.int32, (CHUNK, CHUNK), 1)
    rel = (ii - jj).astype(F32)
    row = ii.astype(F32)
    lane = jj.astype(F32)
    lgf = [_log_sigmoid(dec_ref[pl.ds(2 * hp + t, 1), :]) for t in range(2)]
    lgb = [_log_sigmoid(dec_ref[pl.ds(RET_HEADS + 2 * hp + t, 1), :]) for t in range(2)]
    for t in range(2):
        tab_ref[0, TAB_MASK + t] = jnp.where(
            rel > 0, jnp.exp(jnp.maximum(rel, 0.0) * lgf[t]),
            jnp.where(rel < 0, jnp.exp(jnp.maximum(-rel, 0.0) * lgb[t]), 2.0))
        tab_ref[0, TAB_QF + t] = jnp.exp((row + 1.0) * lgf[t])
        tab_ref[0, TAB_QB + t] = jnp.exp((CHUNK - row) * lgb[t])
    tab_ref[0, TAB_KF] = jnp.exp((CHUNK - 1.0 - lane) * jnp.where(ii < RET_DK, lgf[0], lgf[1]))
    tab_ref[0, TAB_KB] = jnp.exp(lane * jnp.where(ii < RET_DK, lgb[0], lgb[1]))
    c_rows = jnp.where(ii == TAB_CF, lgf[0], jnp.where(ii == TAB_CF + 1, lgf[1],
                       jnp.where(ii == TAB_CB, lgb[0], lgb[1])))
    tab_ref[0, TAB_C] = jnp.exp(CHUNK * c_rows)


def _ret_tables(dec):
    return pl.pallas_call(
        _ret_tables_kernel,
        grid=(RET_HEADS // 2,),
        in_specs=[pl.BlockSpec((8, 128), lambda p: (0, 0))],
        out_specs=pl.BlockSpec((1, N_TAB, CHUNK, CHUNK), lambda p: (p, 0, 0, 0)),
        out_shape=jax.ShapeDtypeStruct((RET_HEADS // 2, N_TAB, CHUNK, CHUNK), F32),
        compiler_params=_cparams(("parallel",)),
        name="ret_tables",
    )(dec)


def _ret_kernel(q_ref, k_ref, v_ref, g_ref, tab_ref, rn_ref, *rest, seq, has_state, slab):
    if has_state:
        s0f_ref, s0b_ref, o_ref = rest
    else:
        o_ref, sf_ref, sb_ref = rest[-3:]
        if len(rest) == 3:
            for other in range(sf_ref.shape[1]):
                if other != slab:
                    sf_ref[:, other] = jnp.zeros_like(sf_ref[:, other])
                    sb_ref[:, other] = jnp.zeros_like(sb_ref[:, other])
    n_chunks = q_ref.shape[0] // CHUNK
    jj = lax.broadcasted_iota(jnp.int32, (CHUNK, CHUNK), 1)
    mask2 = jnp.concatenate([tab_ref[0, TAB_MASK], tab_ref[0, TAB_MASK + 1]], axis=0)
    qdec_f = [tab_ref[0, TAB_QF + t] for t in range(2)]
    qdec_b = [tab_ref[0, TAB_QB + t] for t in range(2)]
    cdec_f = [tab_ref[0, TAB_C, TAB_CF + t:TAB_CF + t + 1, :] for t in range(2)]
    cdec_b = [tab_ref[0, TAB_C, TAB_CB + t:TAB_CB + t + 1, :] for t in range(2)]
    kdec_f = tab_ref[0, TAB_KF]
    kdec_b = tab_ref[0, TAB_KB]

    k_t = (k_ref[...] * (RET_DK ** -0.5)).T
    first_head = jj < RET_DK

    def chunk(c):
        return slice(c * CHUNK, (c + 1) * CHUNK)

    kv_f = [[None] * n_chunks for _ in range(2)]
    kv_b = [[None] * n_chunks for _ in range(2)]
    for c in range(n_chunks):
        k_c = k_t[:, chunk(c)]
        kd = jnp.concatenate([k_c * kdec_f, k_c * kdec_b], axis=0).astype(BF16)
        kv = _dot(kd, v_ref[chunk(c), :].astype(BF16))
        for t in range(2):
            kv_f[t][c] = kv[t * RET_DK:(t + 1) * RET_DK, t * RET_DV:(t + 1) * RET_DV]
            kv_b[t][c] = kv[CHUNK + t * RET_DK:CHUNK + (t + 1) * RET_DK, t * RET_DV:(t + 1) * RET_DV]

    st_f = [[None] * n_chunks for _ in range(2)]
    st_b = [[None] * n_chunks for _ in range(2)]
    per_seq = seq // CHUNK
    for t in range(2):
        for sq in range(n_chunks // per_seq):
            own = range(sq * per_seq, (sq + 1) * per_seq)
            if has_state:
                s_f = s0f_ref[sq, 0, t]
                s_b = s0b_ref[sq, 0, t]
            else:
                s_f = jnp.zeros((RET_DK, RET_DV), F32)
                s_b = s_f
            for c in own:
                st_f[t][c] = s_f
                s_f = s_f * cdec_f[t] + kv_f[t][c]
            for c in reversed(own):
                st_b[t][c] = s_b
                s_b = s_b * cdec_b[t] + kv_b[t][c]
            if not has_state:
                sf_ref[sq, slab, t] = s_f
                sb_ref[sq, slab, t] = s_b

    for c in range(n_chunks):
        q_c = q_ref[chunk(c), :]
        qm = jnp.concatenate([jnp.where(first_head, q_c, 0.0), jnp.where(first_head, 0.0, q_c)],
                             axis=0).astype(BF16)
        att = (_dot(qm, k_t[:, chunk(c)].astype(BF16)) * mask2).astype(BF16)
        states = jnp.concatenate(
            [jnp.concatenate([st_f[t][c], st_b[t][c]], axis=1) for t in range(2)], axis=0)
        qs = _dot(qm, states.astype(BF16))
        v_c = v_ref[chunk(c), :].astype(BF16)
        for t in range(2):
            rows = slice(t * CHUNK, (t + 1) * CHUNK)
            vsl = slice(t * RET_DV, (t + 1) * RET_DV)
            o = (_dot(att[rows], v_c[:, vsl]) + qs[rows, :RET_DV] * qdec_f[t]
                 + qs[rows, RET_DV:] * qdec_b[t])
            o = (o * lax.rsqrt(jnp.mean(o * o, axis=-1, keepdims=True) + EPS)) * rn_ref[...]
            hg = 0.5 * g_ref[chunk(c), vsl]
            o_ref[chunk(c), vsl] = (o * (hg * (1.0 + jnp.tanh(hg)))).astype(o_ref.dtype)


def _retention(p_act, seq, tables, ret_norm, layer, states=None, new_states=None):
    n_tok = p_act.shape[0]
    n_seq = n_tok // seq
    has_state = states is not None
    pair_qk, pair_v = 2 * RET_DK, 2 * RET_DV
    rows = max(seq, RET_ROWS)
    blk_seqs = rows // seq
    in_specs = [
        pl.BlockSpec((rows, pair_qk), lambda p, s: (s, OFF_QR // pair_qk + p)),
        pl.BlockSpec((rows, pair_qk), lambda p, s: (s, OFF_KR // pair_qk + p)),
        pl.BlockSpec((rows, pair_v), lambda p, s: (s, OFF_VR // pair_v + p)),
        pl.BlockSpec((rows, pair_v), lambda p, s: (s, OFF_GR // pair_v + p)),
        pl.BlockSpec((1, N_TAB, CHUNK, CHUNK), lambda p, s: (p, 0, 0, 0)),
        pl.BlockSpec((1, RET_DV), lambda p, s: (0, 0)),
    ]
    args = [p_act, p_act, p_act, p_act, tables, ret_norm]
    o_spec = pl.BlockSpec((rows, pair_v), lambda p, s: (s, p))
    o_shape = jax.ShapeDtypeStruct((n_tok, RET_V), BF16)
    st_spec = pl.BlockSpec((blk_seqs, 1, 2, RET_DK, RET_DV), lambda p, s: (s, layer, p, 0, 0))
    aliases, slab = {}, 0
    if has_state:
        in_specs += [st_spec, st_spec]
        args += list(states)
        out_specs, out_shape = o_spec, o_shape
    else:
        if new_states is None:
            st_spec = pl.BlockSpec((blk_seqs, DEPTH, 2, RET_DK, RET_DV), lambda p, s: (s, 0, p, 0, 0))
            slab = layer
        else:
            in_specs += [pl.BlockSpec(memory_space=pl.ANY)] * 2
            aliases = {len(args): 1, len(args) + 1: 2}
            args += list(new_states)
        st_shape = jax.ShapeDtypeStruct((n_seq, DEPTH, RET_HEADS, RET_DK, RET_DV), F32)
        out_specs, out_shape = [o_spec, st_spec, st_spec], [o_shape, st_shape, st_shape]
    return pl.pallas_call(
        functools.partial(_ret_kernel, seq=seq, has_state=has_state, slab=slab),
        grid=(RET_HEADS // 2, n_tok // rows),
        in_specs=in_specs,
        out_specs=out_specs,
        out_shape=out_shape,
        input_output_aliases=aliases,
        compiler_params=_cparams(("parallel", "parallel")),
        name="retention_lat" if has_state else "retention_ctx",
    )(*args)


FOURIER_ROWS = 1024


def _four_kernel(u0_ref, u1_ref, wc_ref, csn_ref, o_ref):
    gd = FOURIER_GROUP_DIM
    tc, ts = [], []
    for g in range(FOURIER_GROUPS):
        u_ref = (u0_ref, u1_ref)[g // 2]
        u_g = u_ref[:, (g % 2) * gd:(g % 2 + 1) * gd].astype(BF16)
        t = _dot(u_g, wc_ref[...])
        tc.append(t[:, :gd])
        ts.append(t[:, gd:])
    tc = jnp.concatenate(tc, axis=1).astype(BF16)
    ts = jnp.concatenate(ts, axis=1).astype(BF16)
    seq = csn_ref.shape[0]
    for sq in range(o_ref.shape[0] // seq):
        rows = slice(sq * seq, (sq + 1) * seq)
        t_sq = jnp.concatenate([tc[rows], ts[rows]], axis=0)
        o_ref[rows, :] = _dot(csn_ref[...], t_sq).astype(o_ref.dtype)


def _dft_tables(n):
    k = np.arange(n, dtype=np.int64)
    ang = 2.0 * np.pi * ((k[:, None] * k[None, :]) % n).astype(np.float64) / n
    scale = 1.0 / np.sqrt(n)
    return np.cos(ang) * scale, np.sin(ang) * scale


def _fourier(p_act, seq):
    n_tok = p_act.shape[0]
    cc, sc = _dft_tables(FOURIER_GROUP_DIM)
    wc = jnp.asarray(np.concatenate([cc, -sc], axis=1), F32).astype(BF16)
    cn, sn = _dft_tables(seq)
    csn = jnp.asarray(np.concatenate([cn, sn], axis=1), F32).astype(BF16)
    const = lambda shape: pl.BlockSpec(shape, lambda s: (0, 0))
    half = FOURIER_W // 2
    rows = max(seq, FOURIER_ROWS)
    return pl.pallas_call(
        _four_kernel,
        grid=(n_tok // rows,),
        in_specs=[
            pl.BlockSpec((rows, half), lambda s: (s, OFF_UF // half)),
            pl.BlockSpec((rows, half), lambda s: (s, OFF_UF // half + 1)),
            const((FOURIER_GROUP_DIM, 2 * FOURIER_GROUP_DIM)),
            const((seq, 2 * seq)),
        ],
        out_specs=pl.BlockSpec((rows, FOURIER_W), lambda s: (s, 0)),
        out_shape=jax.ShapeDtypeStruct((n_tok, FOURIER_W), BF16),
        compiler_params=_cparams(("parallel",)),
        name="fourier",
    )(p_act, p_act, wc, csn)


MIX_TM = 1024
MERGE_STEPS = 4
MERGE_TM = MIX_TM // MERGE_STEPS
FF_TILE = 256
FF_STEPS = D_FF // FF_TILE
FF_ROWS = 256


def _mix_kernel(oa_ref, or_ref, of_ref, ga_ref, gr_ref, gf_ref, x_ref, g1_ref, sh_ref, sc_ref,
                g2_ref, n2_ref, wba_ref, wbr_ref, wbf_ref, wo_ref, wa_ref, wv_ref, cw_ref, cb_ref,
                wd_ref, o_ref, h_scr, acc_scr, wu_scr, wd_scr, *, seq):
    i = pl.program_id(0)
    j = pl.program_id(1)
    chunk = jnp.maximum(j - MERGE_STEPS, 0)
    a_cols = pl.ds(pl.multiple_of(chunk * FF_TILE, FF_TILE), FF_TILE)
    v_cols = pl.ds(pl.multiple_of((FF_STEPS + chunk) * FF_TILE, FF_TILE), FF_TILE)

    @pl.when(j < MERGE_STEPS)
    def _():
        gate = lambda ref: jnp.tanh(ref[...].astype(F32)) + 1.0
        dst = pl.ds(pl.multiple_of(j * MERGE_TM, MERGE_TM), MERGE_TM)
        merged = (gate(ga_ref) * _dot(oa_ref[...], wba_ref[...])
                  + gate(gr_ref) * _dot(or_ref[...], wbr_ref[...])
                  + gate(gf_ref) * _dot(of_ref[...], wbf_ref[...]))
        x1 = x_ref[...] + g1_ref[0] * _dot((0.5 * merged).astype(BF16), wo_ref[...])
        o_ref[dst, :] = x1
        r = lax.rsqrt(jnp.mean(x1 * x1, axis=-1, keepdims=True) + EPS)
        h = (x1 * r) * n2_ref[...] * (1.0 + sc_ref[0]) + sh_ref[0]
        h_scr[dst, :] = h.astype(BF16)
        acc_scr[dst, :] = jnp.zeros((MERGE_TM, D_MODEL), F32)

    @pl.when(jnp.logical_and(i == 0, j >= MERGE_STEPS))
    def _():
        wu_scr[:, a_cols] = wa_ref[0].astype(BF16)
        wu_scr[:, v_cols] = wv_ref[0].astype(BF16)
        wd_scr[a_cols, :] = wd_ref[0].astype(BF16)

    def mlp(last):
        blocks = [slice(b * FF_ROWS, (b + 1) * FF_ROWS) for b in range(MIX_TM // FF_ROWS)]
        wa, wv, wd = wu_scr[:, a_cols], wu_scr[:, v_cols], wd_scr[a_cols, :]
        a = jnp.concatenate([_dot(h_scr[rows, :], wa) for rows in blocks], axis=0)
        val = [_dot(h_scr[rows, :], wv) for rows in blocks]
        pos = lax.broadcasted_iota(jnp.int32, a.shape, 0) % seq
        prev = jnp.where(pos == 0, 0.0, pltpu.roll(a, 1, 0))
        nxt = jnp.where(pos == seq - 1, 0.0, pltpu.roll(a, MIX_TM - 1, 0))
        cw = cw_ref[...]
        ac = prev * cw[0:1] + a * cw[1:2] + nxt * cw[2:3] + cb_ref[...]
        for b, rows in enumerate(blocks):
            act = jax.nn.gelu(ac[rows]) * val[b]
            down = _dot(act.astype(BF16), wd)
            if last:
                o_ref[rows, :] = o_ref[rows, :] + g2_ref[0] * (acc_scr[rows, :] + down)
            else:
                acc_scr[rows, :] += down

    last_step = pl.num_programs(1) - 1

    @pl.when(jnp.logical_and(j >= MERGE_STEPS, j < last_step))
    def _():
        mlp(last=False)

    @pl.when(j == last_step)
    def _():
        mlp(last=True)


def _mix(o_att, o_ret, o_four, p_gate, x2d, seq, mod, rows_per_mod, w_att, w_ret, w_four, w_out,
         norm2, w_up, conv_w, conv_b, w_down, layer):
    n_tok = x2d.shape[0]
    mod_idx = lambda i: (i * MIX_TM) // rows_per_mod
    sub = lambda i, j: i * MERGE_STEPS + jnp.minimum(j, MERGE_STEPS - 1)
    ff = lambda j: jnp.maximum(j - MERGE_STEPS, 0)
    ff_w = lambda i, j: jnp.where(i == 0, ff(j), FF_STEPS - 1)
    br = lambda: pl.BlockSpec((MERGE_TM, 512), lambda i, j: (sub(i, j), 0))
    gate = lambda k: pl.BlockSpec((MERGE_TM, D_MODEL), lambda i, j: (sub(i, j), k))
    modv = lambda k: pl.BlockSpec((1, 1, D_MODEL), lambda i, j: (mod_idx(i), 0, k))
    const = lambda shape: pl.BlockSpec(shape, lambda i, j: (0, 0))
    return pl.pallas_call(
        functools.partial(_mix_kernel, seq=seq),
        grid=(n_tok // MIX_TM, MERGE_STEPS + FF_STEPS),
        in_specs=[
            br(), br(), br(), gate(0), gate(1), gate(2),
            pl.BlockSpec((MERGE_TM, D_MODEL), lambda i, j: (sub(i, j), 0)),
            modv(2), modv(3), modv(4), modv(5),
            const((1, D_MODEL)),
            const((512, D_MODEL)), const((512, D_MODEL)), const((512, D_MODEL)),
            const((D_MODEL, D_MODEL)),
            pl.BlockSpec((1, D_MODEL, FF_TILE), lambda i, j: (layer, 0, ff_w(i, j))),
            pl.BlockSpec((1, D_MODEL, FF_TILE), lambda i, j: (layer, 0, FF_STEPS + ff_w(i, j))),
            pl.BlockSpec((3, FF_TILE), lambda i, j: (0, ff(j))),
            pl.BlockSpec((1, FF_TILE), lambda i, j: (0, ff(j))),
            pl.BlockSpec((1, FF_TILE, D_MODEL), lambda i, j: (layer, ff_w(i, j), 0)),
        ],
        out_specs=pl.BlockSpec((MIX_TM, D_MODEL), lambda i, j: (i, 0)),
        out_shape=jax.ShapeDtypeStruct((n_tok, D_MODEL), F32),
        scratch_shapes=[pltpu.VMEM((MIX_TM, D_MODEL), BF16), pltpu.VMEM((MIX_TM, D_MODEL), F32),
                        pltpu.VMEM((D_MODEL, 2 * D_FF), BF16), pltpu.VMEM((D_FF, D_MODEL), BF16)],
        compiler_params=_cparams(("arbitrary", "arbitrary")),
        name="mix",
    )(o_att, o_ret, o_four, p_gate, p_gate, p_gate, x2d, mod, mod, mod, mod, norm2,
      w_att, w_ret, w_four, w_out, w_up, w_up, conv_w, conv_b, w_down)


def _rope_tables(n_tok):
    rows = n_tok // GRID_W
    row_id = jnp.repeat(jnp.arange(rows), GRID_W).astype(F32)
    col_id = jnp.tile(jnp.arange(GRID_W), rows).astype(F32)
    n_freq = HEAD_DIM // 4
    inv = ROPE_THETA ** (-jnp.arange(n_freq, dtype=F32) / n_freq)
    ang = jnp.concatenate([row_id[None, :] * inv[:, None], col_id[None, :] * inv[:, None]], axis=0)
    return jnp.cos(ang), jnp.sin(ang)


def kernel(x_prompt, x_sample, cache_k, cache_v, state_ret_fwd, state_ret_bwd, c, c_ctx, w_ada, b_ada, norm1, w_in, q_norm, k_norm, ret_decay_f, ret_decay_b, ret_norm, w_br_att, w_br_ret, w_br_four, w_out, norm2, w_up, conv_w, conv_b, w_down):
    batch, seq, _ = x_prompt.shape
    dec_batch, dec_seq, _ = x_sample.shape
    past = cache_k.shape[2]

    cond_all = jnp.concatenate([c_ctx[None, :], c], axis=0)
    mod_all = _ada(cond_all, w_ada, b_ada)
    cos_t, sin_t = _rope_tables(dec_seq)
    ck = cache_k.reshape(dec_batch, DEPTH, past, ATT_KV)
    cv = cache_v.reshape(dec_batch, DEPTH, past, ATT_KV)

    xp = x_prompt.reshape(batch * seq, D_MODEL)
    xs = x_sample.reshape(dec_batch * dec_seq, D_MODEL)
    new_kv = new_st = None
    for l in range(DEPTH):
        w_att, w_ret, w_four = (w_br_att[l].astype(BF16), w_br_ret[l].astype(BF16),
                                w_br_four[l].astype(BF16))
        w_o = w_out[l].astype(BF16)
        n1, n2 = norm1[l][None, :], norm2[l][None, :]
        qn, kn = q_norm[l][:, None], k_norm[l][:, None]
        rn = ret_norm[l][None, :]
        dec = _ret_tables(jnp.broadcast_to(
            jnp.concatenate([ret_decay_f[l], ret_decay_b[l]])[:, None].astype(F32), (2 * RET_HEADS, 128)))
        cw, cb = conv_w[l], conv_b[l][None, :]
        mod_ctx = mod_all[l, 0:1].reshape(1, 1, 6 * D_MODEL)
        mod_lat = mod_all[l, 1:].reshape(dec_batch, 1, 6 * D_MODEL)

        p_act, p_gate = _in_proj(xp, mod_ctx, batch * seq, n1, w_in, l)
        o_att, *new_kv = _att_ctx(p_act, seq, qn, kn, l, new_kv)
        o_ret, *new_st = _retention(p_act, seq, dec, rn, l, new_states=new_st)
        o_four = _fourier(p_act, seq)
        xp = _mix(o_att, o_ret, o_four, p_gate, xp, seq, mod_ctx, batch * seq,
                  w_att, w_ret, w_four, w_o, n2, w_up, cw, cb, w_down, l)

        p_act, p_gate = _in_proj(xs, mod_lat, dec_seq, n1, w_in, l)
        o_att = _att_lat(p_act, dec_seq, ck, cv, l, qn, kn, cos_t, sin_t)
        o_ret = _retention(p_act, dec_seq, dec, rn, l, states=(state_ret_fwd, state_ret_bwd))
        o_four = _fourier(p_act, dec_seq)
        xs = _mix(o_att, o_ret, o_four, p_gate, xs, dec_seq, mod_lat, dec_seq,
                  w_att, w_ret, w_four, w_o, n2, w_up, cw, cb, w_down, l)

    kv_shape = (batch, DEPTH, seq, N_KV_HEADS, HEAD_DIM)
    return (xp.reshape(batch, seq, D_MODEL), xs.reshape(dec_batch, dec_seq, D_MODEL),
            new_kv[0].reshape(kv_shape), new_kv[1].reshape(kv_shape), new_st[0], new_st[1])
```

```python
import functools

import numpy as np
import jax
import jax.numpy as jnp
from jax import lax
from jax.experimental import pallas as pl
from jax.experimental.pallas import tpu as pltpu

D_MODEL = 1024
DEPTH = 2
GRID_W = 64
HEAD_DIM = 64
N_HEADS = 8
N_KV_HEADS = 2
Q_PER_KV = N_HEADS // N_KV_HEADS
ATT_Q = N_HEADS * HEAD_DIM
ATT_KV = N_KV_HEADS * HEAD_DIM
RET_HEADS = 4
RET_DK = 64
RET_DV = 128
RET_V = RET_HEADS * RET_DV
FOURIER_GROUPS = 4
FOURIER_GROUP_DIM = 128
FOURIER_W = FOURIER_GROUPS * FOURIER_GROUP_DIM
D_FF = 2816
CHUNK = 128
Q_BLOCK = 512
ROPE_THETA = 10000.0
EPS = 1e-6
LOG2_E = 1.4426950408889634

F32 = jnp.float32
BF16 = jnp.bfloat16

OFF_QA, OFF_KA, OFF_VA = 0, 512, 640
OFF_QR, OFF_KR, OFF_VR, OFF_GR, OFF_UF = 768, 1024, 1280, 1792, 2304
W_IN_SPLIT = 2816
OFF_GATE = 3072
P_W = OFF_GATE + 3 * D_MODEL

VMEM_LIMIT = 56 * 1024 * 1024


def _cparams(sem):
    return pltpu.CompilerParams(dimension_semantics=sem, vmem_limit_bytes=VMEM_LIMIT)


def _dot(a, b):
    return jnp.dot(a, b, preferred_element_type=F32)


def _ada_kernel(cond_ref, w_ref, b_ref, o_ref):
    cnd = cond_ref[...]
    s = cnd * jax.nn.sigmoid(cnd)
    o_ref[0] = _dot(s.astype(BF16), w_ref[0].astype(BF16)) + b_ref[0]


def _ada(cond_all, w_ada, b_ada):
    n = cond_all.shape[0]
    tn = 1024
    return pl.pallas_call(
        _ada_kernel,
        grid=(DEPTH, 6 * D_MODEL // tn),
        in_specs=[
            pl.BlockSpec((n, D_MODEL), lambda l, j: (0, 0)),
            pl.BlockSpec((1, D_MODEL, tn), lambda l, j: (l, 0, j)),
            pl.BlockSpec((1, 1, tn), lambda l, j: (l, 0, j)),
        ],
        out_specs=pl.BlockSpec((1, n, tn), lambda l, j: (l, 0, j)),
        out_shape=jax.ShapeDtypeStruct((DEPTH, n, 6 * D_MODEL), F32),
        compiler_params=_cparams(("parallel", "parallel")),
        name="ada",
    )(cond_all, w_ada, b_ada.reshape(DEPTH, 1, 6 * D_MODEL))


IN_TN = 1024
IN_ROWS = 256
MAIN_STEPS = OFF_GATE // IN_TN


def _in_kernel(x_ref, sh_ref, sc_ref, g_ref, w_ref, om_ref, og_ref, h_scr, w_scr):
    i = pl.program_id(0)
    j = pl.program_id(1)
    cols = pl.ds(pl.multiple_of(j * IN_TN, IN_TN), IN_TN)

    @pl.when(i == 0)
    def _():
        w_scr[:, cols] = w_ref[0].astype(BF16)

    @pl.when(j == 0)
    def _():
        for b in range(x_ref.shape[0] // IN_ROWS):
            rows = slice(b * IN_ROWS, (b + 1) * IN_ROWS)
            x = x_ref[rows, :]
            r = lax.rsqrt(jnp.mean(x * x, axis=-1, keepdims=True) + EPS)
            h = ((x * r) * g_ref[...] * (1.0 + sc_ref[0]) + sh_ref[0]).astype(BF16)
            h_scr[rows, :] = h
            om_ref[rows, :] = _dot(h, w_scr[:, 0:IN_TN])

    @pl.when(jnp.logical_and(j > 0, j < MAIN_STEPS))
    def _():
        om_ref[...] = _dot(h_scr[...], w_scr[:, cols])

    @pl.when(j >= MAIN_STEPS)
    def _():
        og_ref[...] = (0.5 * _dot(h_scr[...], w_scr[:, cols])).astype(og_ref.dtype)


def _in_proj(x2d, mod, rows_per_mod, norm1, w_in, layer):
    n_tok = x2d.shape[0]
    tm, tn = 1024, IN_TN
    n_col = P_W // tn
    mod_idx = lambda i: (i * tm) // rows_per_mod

    def w_col(i, j):
        jj = jnp.where(i == 0, j, n_col - 1)
        col = jnp.where(jj < MAIN_STEPS, jj * tn, W_IN_SPLIT + (jj - MAIN_STEPS) * tn)
        return pl.multiple_of(col, 128)

    return pl.pallas_call(
        _in_kernel,
        grid=(n_tok // tm, n_col),
        in_specs=[
            pl.BlockSpec((tm, D_MODEL), lambda i, j: (i, 0)),
            pl.BlockSpec((1, 1, D_MODEL), lambda i, j: (mod_idx(i), 0, 0)),
            pl.BlockSpec((1, 1, D_MODEL), lambda i, j: (mod_idx(i), 0, 1)),
            pl.BlockSpec((1, D_MODEL), lambda i, j: (0, 0)),
            pl.BlockSpec((pl.Element(1), pl.Element(D_MODEL), pl.Element(tn)),
                         lambda i, j: (layer, 0, w_col(i, j))),
        ],
        out_specs=[
            pl.BlockSpec((tm, tn), lambda i, j: (i, jnp.minimum(j, MAIN_STEPS - 1))),
            pl.BlockSpec((tm, tn), lambda i, j: (i, jnp.maximum(j - MAIN_STEPS, 0))),
        ],
        out_shape=[jax.ShapeDtypeStruct((n_tok, OFF_GATE), F32),
                   jax.ShapeDtypeStruct((n_tok, P_W - OFF_GATE), BF16)],
        scratch_shapes=[pltpu.VMEM((tm, D_MODEL), BF16), pltpu.VMEM((D_MODEL, P_W), BF16)],
        compiler_params=_cparams(("arbitrary", "arbitrary")),
        name="in_proj",
    )(x2d, mod, mod, norm1, w_in)


def _norm_rope_heads_t(x_t, n_heads, g_col, cos_t=None, sin_t=None):
    quarter = HEAD_DIM // 4
    outs = []
    for h in range(n_heads):
        x = x_t[h * HEAD_DIM:(h + 1) * HEAD_DIM, :]
        r = lax.rsqrt(jnp.mean(x * x, axis=0, keepdims=True) + EPS)
        y = (x * r) * g_col
        if cos_t is not None:
            pieces = []
            for a in range(2):
                c = cos_t[a * quarter:(a + 1) * quarter]
                s = sin_t[a * quarter:(a + 1) * quarter]
                x1 = y[2 * a * quarter:(2 * a + 1) * quarter]
                x2 = y[(2 * a + 1) * quarter:(2 * a + 2) * quarter]
                pieces += [x1 * c - x2 * s, x2 * c + x1 * s]
            y = jnp.concatenate(pieces, axis=0)
        outs.append(y)
    return outs


SHIFT_LIMIT = 40.0


def _score_bound(k_heads_t, q_gain):
    k_sq = [jnp.max(jnp.sum(k * k, axis=0, keepdims=True)) for k in k_heads_t]
    k_norm = jnp.sqrt(functools.reduce(jnp.maximum, k_sq))
    q_norm = HEAD_DIM ** 0.5 * jnp.max(jnp.abs(q_gain))
    return 1.01 * (HEAD_DIM ** -0.5 * LOG2_E) * q_norm * k_norm


def _attend_t(q_heads_t, k_bf, v_t_bf, score_bound, store):
    tq = q_heads_t[0].shape[1]
    cols = Q_PER_KV * tq
    zeros = jnp.zeros((HEAD_DIM, cols), F32)
    q_kv = [jnp.concatenate(q_heads_t[kv * Q_PER_KV:(kv + 1) * Q_PER_KV], axis=1)
            * (HEAD_DIM ** -0.5 * LOG2_E) for kv in range(N_KV_HEADS)]
    rhs = jnp.concatenate([jnp.concatenate([q_kv[0], zeros], axis=1),
                           jnp.concatenate([zeros, q_kv[1]], axis=1)], axis=0).astype(BF16)
    def finish(shift):
        s_all = _dot(k_bf, rhs)
        out_rows = []
        for kv in range(N_KV_HEADS):
            s_t = s_all[:, kv * cols:(kv + 1) * cols]
            m = jnp.max(s_t, axis=0, keepdims=True) if shift is None else shift
            e = jnp.exp2(s_t - m)
            inv = 1.0 / jnp.sum(e, axis=0, keepdims=True)
            o_t = _dot(v_t_bf[kv * HEAD_DIM:(kv + 1) * HEAD_DIM, :], e.astype(BF16)) * inv
            out_rows += [o_t[:, g * tq:(g + 1) * tq] for g in range(Q_PER_KV)]
        store(jnp.concatenate(out_rows, axis=0))

    if score_bound is None:
        finish(None)
        return

    @pl.when(score_bound <= SHIFT_LIMIT)
    def _():
        finish(score_bound)

    @pl.when(jnp.logical_not(score_bound <= SHIFT_LIMIT))
    def _():
        finish(None)


def _att_ctx_kernel(q_ref, k_ref, v_ref, qn_ref, kn_ref, *rest, slab):
    o_ref, ko_ref, vo_ref = rest[-3:]
    seq = ko_ref.shape[2]
    if len(rest) == 3:
        for other in range(ko_ref.shape[1]):
            if other != slab:
                ko_ref[:, other] = jnp.zeros_like(ko_ref[:, other])
                vo_ref[:, other] = jnp.zeros_like(vo_ref[:, other])
    for sq in range(ko_ref.shape[0]):
        rows = slice(sq * seq, (sq + 1) * seq)
        k_heads = _norm_rope_heads_t(k_ref[rows, :].T, N_KV_HEADS, kn_ref[...])
        k_n = jnp.concatenate(k_heads, axis=0).T
        ko_ref[sq, slab] = k_n
        v = v_ref[rows, :]
        vo_ref[sq, slab] = v
        q_heads = _norm_rope_heads_t(q_ref[rows, :].T, N_HEADS, qn_ref[...])

        def store(o_t, rows=rows):
            o_ref[rows, :] = o_t.T.astype(o_ref.dtype)

        _attend_t(q_heads, k_n.astype(BF16), v.T.astype(BF16), None, store)


ATT_CTX_ROWS = 1024


def _att_ctx(p_act, seq, q_norm, k_norm, layer, caches=None):
    n_tok = p_act.shape[0]
    batch = n_tok // seq
    rows = max(seq, ATT_CTX_ROWS)
    blk = rows // seq
    in_specs = [
        pl.BlockSpec((rows, ATT_Q), lambda b: (b, OFF_QA // ATT_Q)),
        pl.BlockSpec((rows, ATT_KV), lambda b: (b, OFF_KA // ATT_KV)),
        pl.BlockSpec((rows, ATT_KV), lambda b: (b, OFF_VA // ATT_KV)),
        pl.BlockSpec((HEAD_DIM, 1), lambda b: (0, 0)),
        pl.BlockSpec((HEAD_DIM, 1), lambda b: (0, 0)),
    ]
    args = [p_act, p_act, p_act, q_norm, k_norm]
    if caches is None:
        cache_spec = pl.BlockSpec((blk, DEPTH, seq, ATT_KV), lambda b: (b, 0, 0, 0))
        slab, aliases = layer, {}
    else:
        cache_spec = pl.BlockSpec((blk, 1, seq, ATT_KV), lambda b: (b, layer, 0, 0))
        slab, aliases = 0, {len(args): 1, len(args) + 1: 2}
        in_specs += [pl.BlockSpec(memory_space=pl.ANY)] * 2
        args += list(caches)
    cache_shape = jax.ShapeDtypeStruct((batch, DEPTH, seq, ATT_KV), F32)
    return pl.pallas_call(
        functools.partial(_att_ctx_kernel, slab=slab),
        grid=(n_tok // rows,),
        in_specs=in_specs,
        out_specs=[pl.BlockSpec((rows, ATT_Q), lambda b: (b, 0)), cache_spec, cache_spec],
        out_shape=[jax.ShapeDtypeStruct((n_tok, ATT_Q), BF16), cache_shape, cache_shape],
        input_output_aliases=aliases,
        compiler_params=_cparams(("parallel",)),
        name="att_ctx",
    )(*args)


def _att_lat_kernel(q_ref, k_ref, v_ref, ck_ref, cv_ref, qn_ref, kn_ref,
                    cq_ref, sq_ref, ck_t_ref, sk_t_ref, o_ref, kf_scr, vt_scr, bound_scr, *, seq):
    @pl.when(pl.program_id(1) == 0)
    def _():
        k_heads = _norm_rope_heads_t(k_ref[...].T, N_KV_HEADS, kn_ref[...], ck_t_ref[...], sk_t_ref[...])
        kf_scr[0:seq, :] = jnp.concatenate(k_heads, axis=0).T.astype(BF16)
        past_k = ck_ref[0, 0]
        kf_scr[seq:, :] = past_k.astype(BF16)
        vt_scr[:, 0:seq] = v_ref[...].T.astype(BF16)
        vt_scr[:, seq:] = cv_ref[0, 0].T.astype(BF16)
        past_t = past_k.T
        past_heads = [past_t[kv * HEAD_DIM:(kv + 1) * HEAD_DIM] for kv in range(N_KV_HEADS)]
        bound_scr[0] = _score_bound(k_heads + past_heads, qn_ref[...])

    q_heads = _norm_rope_heads_t(q_ref[...].T, N_HEADS, qn_ref[...], cq_ref[...], sq_ref[...])

    def store(o_t):
        o_ref[...] = o_t.T.astype(o_ref.dtype)

    _attend_t(q_heads, kf_scr[...], vt_scr[...], bound_scr[0], store)


def _att_lat(p_act, seq, cache_k, cache_v, layer, q_norm, k_norm, cos_t, sin_t):
    n_tok = p_act.shape[0]
    nb = seq // Q_BLOCK
    past = cache_k.shape[2]
    return pl.pallas_call(
        functools.partial(_att_lat_kernel, seq=seq),
        grid=(n_tok // seq, nb),
        in_specs=[
            pl.BlockSpec((Q_BLOCK, ATT_Q), lambda b, i: (b * nb + i, OFF_QA // ATT_Q)),
            pl.BlockSpec((seq, ATT_KV), lambda b, i: (b, OFF_KA // ATT_KV)),
            pl.BlockSpec((seq, ATT_KV), lambda b, i: (b, OFF_VA // ATT_KV)),
            pl.BlockSpec((1, 1, past, ATT_KV), lambda b, i: (b, layer, 0, 0)),
            pl.BlockSpec((1, 1, past, ATT_KV), lambda b, i: (b, layer, 0, 0)),
            pl.BlockSpec((HEAD_DIM, 1), lambda b, i: (0, 0)),
            pl.BlockSpec((HEAD_DIM, 1), lambda b, i: (0, 0)),
            pl.BlockSpec((HEAD_DIM // 2, Q_BLOCK), lambda b, i: (0, i)),
            pl.BlockSpec((HEAD_DIM // 2, Q_BLOCK), lambda b, i: (0, i)),
            pl.BlockSpec((HEAD_DIM // 2, seq), lambda b, i: (0, 0)),
            pl.BlockSpec((HEAD_DIM // 2, seq), lambda b, i: (0, 0)),
        ],
        out_specs=pl.BlockSpec((Q_BLOCK, ATT_Q), lambda b, i: (b * nb + i, 0)),
        out_shape=jax.ShapeDtypeStruct((n_tok, ATT_Q), BF16),
        scratch_shapes=[pltpu.VMEM((seq + past, ATT_KV), BF16),
                        pltpu.VMEM((ATT_KV, seq + past), BF16),
                        pltpu.SMEM((1,), F32)],
        compiler_params=_cparams(("parallel", "arbitrary")),
        name="att_lat",
    )(p_act, p_act, p_act, cache_k, cache_v, q_norm, k_norm, cos_t, sin_t, cos_t, sin_t)


RET_ROWS = 1024


def _log_sigmoid(d):
    return jnp.minimum(d, 0.0) - jnp.log1p(jnp.exp(-jnp.abs(d)))


TAB_MASK, TAB_QF, TAB_QB, TAB_KF, TAB_KB, TAB_C, N_TAB = 0, 2, 4, 6, 7, 8, 9
TAB_CF, TAB_CB = 0, 2


def _ret_tables_kernel(dec_ref, tab_ref):
    hp = pl.program_id(0)
    ii = lax.broadcasted_iota(jnp.int32, (CHUNK, CHUNK), 0)
    jj = lax.broadcasted_iota(jnp.int32, (CHUNK, CHUNK), 1)
    rel = (ii - jj).astype(F32)
    row = ii.astype(F32)
    lane = jj.astype(F32)
    lgf = [_log_sigmoid(dec_ref[pl.ds(2 * hp + t, 1), :]) for t in range(2)]
    lgb = [_log_sigmoid(dec_ref[pl.ds(RET_HEADS + 2 * hp + t, 1), :]) for t in range(2)]
    for t in range(2):
        tab_ref[0, TAB_MASK + t] = jnp.where(
            rel > 0, jnp.exp(jnp.maximum(rel, 0.0) * lgf[t]),
            jnp.where(rel < 0, jnp.exp(jnp.maximum(-rel, 0.0) * lgb[t]), 2.0))
        tab_ref[0, TAB_QF + t] = jnp.exp((row + 1.0) * lgf[t])
        tab_ref[0, TAB_QB + t] = jnp.exp((CHUNK - row) * lgb[t])
    tab_ref[0, TAB_KF] = jnp.exp((CHUNK - 1.0 - lane) * jnp.where(ii < RET_DK, lgf[0], lgf[1]))
    tab_ref[0, TAB_KB] = jnp.exp(lane * jnp.where(ii < RET_DK, lgb[0], lgb[1]))
    c_rows = jnp.where(ii == TAB_CF, lgf[0], jnp.where(ii == TAB_CF + 1, lgf[1],
                       jnp.where(ii == TAB_CB, lgb[0], lgb[1])))
    tab_ref[0, TAB_C] = jnp.exp(CHUNK * c_rows)


def _ret_tables(dec):
    return pl.pallas_call(
        _ret_tables_kernel,
        grid=(RET_HEADS // 2,),
        in_specs=[pl.BlockSpec((8, 128), lambda p: (0, 0))],
        out_specs=pl.BlockSpec((1, N_TAB, CHUNK, CHUNK), lambda p: (p, 0, 0, 0)),
        out_shape=jax.ShapeDtypeStruct((RET_HEADS // 2, N_TAB, CHUNK, CHUNK), F32),
        compiler_params=_cparams(("parallel",)),
        name="ret_tables",
    )(dec)


def _ret_kernel(q_ref, k_ref, v_ref, g_ref, tab_ref, rn_ref, *rest, seq, has_state, slab):
    if has_state:
        s0f_ref, s0b_ref, o_ref = rest
    else:
        o_ref, sf_ref, sb_ref = rest[-3:]
        if len(rest) == 3:
            for other in range(sf_ref.shape[1]):
                if other != slab:
                    sf_ref[:, other] = jnp.zeros_like(sf_ref[:, other])
                    sb_ref[:, other] = jnp.zeros_like(sb_ref[:, other])
    n_chunks = q_ref.shape[0] // CHUNK
    jj = lax.broadcasted_iota(jnp.int32, (CHUNK, CHUNK), 1)
    mask2 = jnp.concatenate([tab_ref[0, TAB_MASK], tab_ref[0, TAB_MASK + 1]], axis=0)
    qdec_f = [tab_ref[0, TAB_QF + t] for t in range(2)]
    qdec_b = [tab_ref[0, TAB_QB + t] for t in range(2)]
    cdec_f = [tab_ref[0, TAB_C, TAB_CF + t:TAB_CF + t + 1, :] for t in range(2)]
    cdec_b = [tab_ref[0, TAB_C, TAB_CB + t:TAB_CB + t + 1, :] for t in range(2)]
    kdec_f = tab_ref[0, TAB_KF]
    kdec_b = tab_ref[0, TAB_KB]

    k_t = (k_ref[...] * (RET_DK ** -0.5)).T
    first_head = jj < RET_DK

    def chunk(c):
        return slice(c * CHUNK, (c + 1) * CHUNK)

    kv_f = [[None] * n_chunks for _ in range(2)]
    kv_b = [[None] * n_chunks for _ in range(2)]
    for c in range(n_chunks):
        k_c = k_t[:, chunk(c)]
        kd = jnp.concatenate([k_c * kdec_f, k_c * kdec_b], axis=0).astype(BF16)
        kv = _dot(kd, v_ref[chunk(c), :].astype(BF16))
        for t in range(2):
            kv_f[t][c] = kv[t * RET_DK:(t + 1) * RET_DK, t * RET_DV:(t + 1) * RET_DV]
            kv_b[t][c] = kv[CHUNK + t * RET_DK:CHUNK + (t + 1) * RET_DK, t * RET_DV:(t + 1) * RET_DV]

    st_f = [[None] * n_chunks for _ in range(2)]
    st_b = [[None] * n_chunks for _ in range(2)]
    per_seq = seq // CHUNK
    for t in range(2):
        for sq in range(n_chunks // per_seq):
            own = range(sq * per_seq, (sq + 1) * per_seq)
            if has_state:
                s_f = s0f_ref[sq, 0, t]
                s_b = s0b_ref[sq, 0, t]
            else:
                s_f = jnp.zeros((RET_DK, RET_DV), F32)
                s_b = s_f
            for c in own:
                st_f[t][c] = s_f
                s_f = s_f * cdec_f[t] + kv_f[t][c]
            for c in reversed(own):
                st_b[t][c] = s_b
                s_b = s_b * cdec_b[t] + kv_b[t][c]
            if not has_state:
                sf_ref[sq, slab, t] = s_f
                sb_ref[sq, slab, t] = s_b

    for c in range(n_chunks):
        q_c = q_ref[chunk(c), :]
        qm = jnp.concatenate([jnp.where(first_head, q_c, 0.0), jnp.where(first_head, 0.0, q_c)],
                             axis=0).astype(BF16)
        att = (_dot(qm, k_t[:, chunk(c)].astype(BF16)) * mask2).astype(BF16)
        states = jnp.concatenate(
            [jnp.concatenate([st_f[t][c], st_b[t][c]], axis=1) for t in range(2)], axis=0)
        qs = _dot(qm, states.astype(BF16))
        v_c = v_ref[chunk(c), :].astype(BF16)
        for t in range(2):
            rows = slice(t * CHUNK, (t + 1) * CHUNK)
            vsl = slice(t * RET_DV, (t + 1) * RET_DV)
            o = (_dot(att[rows], v_c[:, vsl]) + qs[rows, :RET_DV] * qdec_f[t]
                 + qs[rows, RET_DV:] * qdec_b[t])
            o = (o * lax.rsqrt(jnp.mean(o * o, axis=-1, keepdims=True) + EPS)) * rn_ref[...]
            hg = 0.5 * g_ref[chunk(c), vsl]
            o_ref[chunk(c), vsl] = (o * (hg * (1.0 + jnp.tanh(hg)))).astype(o_ref.dtype)


def _retention(p_act, seq, tables, ret_norm, layer, states=None, new_states=None):
    n_tok = p_act.shape[0]
    n_seq = n_tok // seq
    has_state = states is not None
    pair_qk, pair_v = 2 * RET_DK, 2 * RET_DV
    rows = max(seq, RET_ROWS)
    blk_seqs = rows // seq
    in_specs = [
        pl.BlockSpec((rows, pair_qk), lambda p, s: (s, OFF_QR // pair_qk + p)),
        pl.BlockSpec((rows, pair_qk), lambda p, s: (s, OFF_KR // pair_qk + p)),
        pl.BlockSpec((rows, pair_v), lambda p, s: (s, OFF_VR // pair_v + p)),
        pl.BlockSpec((rows, pair_v), lambda p, s: (s, OFF_GR // pair_v + p)),
        pl.BlockSpec((1, N_TAB, CHUNK, CHUNK), lambda p, s: (p, 0, 0, 0)),
        pl.BlockSpec((1, RET_DV), lambda p, s: (0, 0)),
    ]
    args = [p_act, p_act, p_act, p_act, tables, ret_norm]
    o_spec = pl.BlockSpec((rows, pair_v), lambda p, s: (s, p))
    o_shape = jax.ShapeDtypeStruct((n_tok, RET_V), BF16)
    st_spec = pl.BlockSpec((blk_seqs, 1, 2, RET_DK, RET_DV), lambda p, s: (s, layer, p, 0, 0))
    aliases, slab = {}, 0
    if has_state:
        in_specs += [st_spec, st_spec]
        args += list(states)
        out_specs, out_shape = o_spec, o_shape
    else:
        if new_states is None:
            st_spec = pl.BlockSpec((blk_seqs, DEPTH, 2, RET_DK, RET_DV), lambda p, s: (s, 0, p, 0, 0))
            slab = layer
        else:
            in_specs += [pl.BlockSpec(memory_space=pl.ANY)] * 2
            aliases = {len(args): 1, len(args) + 1: 2}
            args += list(new_states)
        st_shape = jax.ShapeDtypeStruct((n_seq, DEPTH, RET_HEADS, RET_DK, RET_DV), F32)
        out_specs, out_shape = [o_spec, st_spec, st_spec], [o_shape, st_shape, st_shape]
    return pl.pallas_call(
        functools.partial(_ret_kernel, seq=seq, has_state=has_state, slab=slab),
        grid=(RET_HEADS // 2, n_tok // rows),
        in_specs=in_specs,
        out_specs=out_specs,
        out_shape=out_shape,
        input_output_aliases=aliases,
        compiler_params=_cparams(("parallel", "parallel")),
        name="retention_lat" if has_state else "retention_ctx",
    )(*args)


FOURIER_ROWS = 1024


def _four_kernel(u0_ref, u1_ref, wc_ref, csn_ref, o_ref):
    gd = FOURIER_GROUP_DIM
    tc, ts = [], []
    for g in range(FOURIER_GROUPS):
        u_ref = (u0_ref, u1_ref)[g // 2]
        u_g = u_ref[:, (g % 2) * gd:(g % 2 + 1) * gd].astype(BF16)
        t = _dot(u_g, wc_ref[...])
        tc.append(t[:, :gd])
        ts.append(t[:, gd:])
    tc = jnp.concatenate(tc, axis=1).astype(BF16)
    ts = jnp.concatenate(ts, axis=1).astype(BF16)
    seq = csn_ref.shape[0]
    for sq in range(o_ref.shape[0] // seq):
        rows = slice(sq * seq, (sq + 1) * seq)
        t_sq = jnp.concatenate([tc[rows], ts[rows]], axis=0)
        o_ref[rows, :] = _dot(csn_ref[...], t_sq).astype(o_ref.dtype)


def _dft_tables(n):
    k = np.arange(n, dtype=np.int64)
    ang = 2.0 * np.pi * ((k[:, None] * k[None, :]) % n).astype(np.float64) / n
    scale = 1.0 / np.sqrt(n)
    return np.cos(ang) * scale, np.sin(ang) * scale


def _fourier(p_act, seq):
    n_tok = p_act.shape[0]
    cc, sc = _dft_tables(FOURIER_GROUP_DIM)
    wc = jnp.asarray(np.concatenate([cc, -sc], axis=1), F32).astype(BF16)
    cn, sn = _dft_tables(seq)
    csn = jnp.asarray(np.concatenate([cn, sn], axis=1), F32).astype(BF16)
    const = lambda shape: pl.BlockSpec(shape, lambda s: (0, 0))
    half = FOURIER_W // 2
    rows = max(seq, FOURIER_ROWS)
    return pl.pallas_call(
        _four_kernel,
        grid=(n_tok // rows,),
        in_specs=[
            pl.BlockSpec((rows, half), lambda s: (s, OFF_UF // half)),
            pl.BlockSpec((rows, half), lambda s: (s, OFF_UF // half + 1)),
            const((FOURIER_GROUP_DIM, 2 * FOURIER_GROUP_DIM)),
            const((seq, 2 * seq)),
        ],
        out_specs=pl.BlockSpec((rows, FOURIER_W), lambda s: (s, 0)),
        out_shape=jax.ShapeDtypeStruct((n_tok, FOURIER_W), BF16),
        compiler_params=_cparams(("parallel",)),
        name="fourier",
    )(p_act, p_act, wc, csn)


MIX_TM = 1024
MERGE_STEPS = 4
MERGE_TM = MIX_TM // MERGE_STEPS
FF_TILE = 256
FF_STEPS = D_FF // FF_TILE
FF_ROWS = 256


def _mix_kernel(oa_ref, or_ref, of_ref, ga_ref, gr_ref, gf_ref, x_ref, g1_ref, sh_ref, sc_ref,
                g2_ref, n2_ref, wba_ref, wbr_ref, wbf_ref, wo_ref, wa_ref, wv_ref, cw_ref, cb_ref,
                wd_ref, o_ref, h_scr, acc_scr, wu_scr, wd_scr, *, seq):
    i = pl.program_id(0)
    j = pl.program_id(1)
    chunk = jnp.maximum(j - MERGE_STEPS, 0)
    a_cols = pl.ds(pl.multiple_of(chunk * FF_TILE, FF_TILE), FF_TILE)
    v_cols = pl.ds(pl.multiple_of((FF_STEPS + chunk) * FF_TILE, FF_TILE), FF_TILE)

    @pl.when(j < MERGE_STEPS)
    def _():
        gate = lambda ref: jnp.tanh(ref[...].astype(F32)) + 1.0
        dst = pl.ds(pl.multiple_of(j * MERGE_TM, MERGE_TM), MERGE_TM)
        merged = (gate(ga_ref) * _dot(oa_ref[...], wba_ref[...])
                  + gate(gr_ref) * _dot(or_ref[...], wbr_ref[...])
                  + gate(gf_ref) * _dot(of_ref[...], wbf_ref[...]))
        x1 = x_ref[...] + g1_ref[0] * _dot((0.5 * merged).astype(BF16), wo_ref[...])
        o_ref[dst, :] = x1
        r = lax.rsqrt(jnp.mean(x1 * x1, axis=-1, keepdims=True) + EPS)
        h = (x1 * r) * n2_ref[...] * (1.0 + sc_ref[0]) + sh_ref[0]
        h_scr[dst, :] = h.astype(BF16)
        acc_scr[dst, :] = jnp.zeros((MERGE_TM, D_MODEL), F32)

    @pl.when(jnp.logical_and(i == 0, j >= MERGE_STEPS))
    def _():
        wu_scr[:, a_cols] = wa_ref[0].astype(BF16)
        wu_scr[:, v_cols] = wv_ref[0].astype(BF16)
        wd_scr[a_cols, :] = wd_ref[0].astype(BF16)

    def mlp(last):
        blocks = [slice(b * FF_ROWS, (b + 1) * FF_ROWS) for b in range(MIX_TM // FF_ROWS)]
        wa, wv, wd = wu_scr[:, a_cols], wu_scr[:, v_cols], wd_scr[a_cols, :]
        a = jnp.concatenate([_dot(h_scr[rows, :], wa) for rows in blocks], axis=0)
        val = [_dot(h_scr[rows, :], wv) for rows in blocks]
        pos = lax.broadcasted_iota(jnp.int32, a.shape, 0) % seq
        prev = jnp.where(pos == 0, 0.0, pltpu.roll(a, 1, 0))
        nxt = jnp.where(pos == seq - 1, 0.0, pltpu.roll(a, MIX_TM - 1, 0))
        cw = cw_ref[...]
        ac = prev * cw[0:1] + a * cw[1:2] + nxt * cw[2:3] + cb_ref[...]
        for b, rows in enumerate(blocks):
            act = jax.nn.gelu(ac[rows]) * val[b]
            down = _dot(act.astype(BF16), wd)
            if last:
                o_ref[rows, :] = o_ref[rows, :] + g2_ref[0] * (acc_scr[rows, :] + down)
            else:
                acc_scr[rows, :] += down

    last_step = pl.num_programs(1) - 1

    @pl.when(jnp.logical_and(j >= MERGE_STEPS, j < last_step))
    def _():
        mlp(last=False)

    @pl.when(j == last_step)
    def _():
        mlp(last=True)


def _mix(o_att, o_ret, o_four, p_gate, x2d, seq, mod, rows_per_mod, w_att, w_ret, w_four, w_out,
         norm2, w_up, conv_w, conv_b, w_down, layer):
    n_tok = x2d.shape[0]
    mod_idx = lambda i: (i * MIX_TM) // rows_per_mod
    sub = lambda i, j: i * MERGE_STEPS + jnp.minimum(j, MERGE_STEPS - 1)
    ff = lambda j: jnp.maximum(j - MERGE_STEPS, 0)
    ff_w = lambda i, j: jnp.where(i == 0, ff(j), FF_STEPS - 1)
    br = lambda: pl.BlockSpec((MERGE_TM, 512), lambda i, j: (sub(i, j), 0))
    gate = lambda k: pl.BlockSpec((MERGE_TM, D_MODEL), lambda i, j: (sub(i, j), k))
    modv = lambda k: pl.BlockSpec((1, 1, D_MODEL), lambda i, j: (mod_idx(i), 0, k))
    const = lambda shape: pl.BlockSpec(shape, lambda i, j: (0, 0))
    return pl.pallas_call(
        functools.partial(_mix_kernel, seq=seq),
        grid=(n_tok // MIX_TM, MERGE_STEPS + FF_STEPS),
        in_specs=[
            br(), br(), br(), gate(0), gate(1), gate(2),
            pl.BlockSpec((MERGE_TM, D_MODEL), lambda i, j: (sub(i, j), 0)),
            modv(2), modv(3), modv(4), modv(5),
            const((1, D_MODEL)),
            const((512, D_MODEL)), const((512, D_MODEL)), const((512, D_MODEL)),
            const((D_MODEL, D_MODEL)),
            pl.BlockSpec((1, D_MODEL, FF_TILE), lambda i, j: (layer, 0, ff_w(i, j))),
            pl.BlockSpec((1, D_MODEL, FF_TILE), lambda i, j: (layer, 0, FF_STEPS + ff_w(i, j))),
            pl.BlockSpec((3, FF_TILE), lambda i, j: (0, ff(j))),
            pl.BlockSpec((1, FF_TILE), lambda i, j: (0, ff(j))),
            pl.BlockSpec((1, FF_TILE, D_MODEL), lambda i, j: (layer, ff_w(i, j), 0)),
        ],
        out_specs=pl.BlockSpec((MIX_TM, D_MODEL), lambda i, j: (i, 0)),
        out_shape=jax.ShapeDtypeStruct((n_tok, D_MODEL), F32),
        scratch_shapes=[pltpu.VMEM((MIX_TM, D_MODEL), BF16), pltpu.VMEM((MIX_TM, D_MODEL), F32),
                        pltpu.VMEM((D_MODEL, 2 * D_FF), BF16), pltpu.VMEM((D_FF, D_MODEL), BF16)],
        compiler_params=_cparams(("arbitrary", "arbitrary")),
        name="mix",
    )(o_att, o_ret, o_four, p_gate, p_gate, p_gate, x2d, mod, mod, mod, mod, norm2,
      w_att, w_ret, w_four, w_out, w_up, w_up, conv_w, conv_b, w_down)


def _rope_tables(n_tok):
    rows = n_tok // GRID_W
    row_id = jnp.repeat(jnp.arange(rows), GRID_W).astype(F32)
    col_id = jnp.tile(jnp.arange(GRID_W), rows).astype(F32)
    n_freq = HEAD_DIM // 4
    inv = ROPE_THETA ** (-jnp.arange(n_freq, dtype=F32) / n_freq)
    ang = jnp.concatenate([row_id[None, :] * inv[:, None], col_id[None, :] * inv[:, None]], axis=0)
    return jnp.cos(ang), jnp.sin(ang)


def kernel(x_prompt, x_sample, cache_k, cache_v, state_ret_fwd, state_ret_bwd, c, c_ctx, w_ada, b_ada, norm1, w_in, q_norm, k_norm, ret_decay_f, ret_decay_b, ret_norm, w_br_att, w_br_ret, w_br_four, w_out, norm2, w_up, conv_w, conv_b, w_down):
    batch, seq, _ = x_prompt.shape
    dec_batch, dec_seq, _ = x_sample.shape
    past = cache_k.shape[2]

    cond_all = jnp.concatenate([c_ctx[None, :], c], axis=0)
    mod_all = _ada(cond_all, w_ada, b_ada)
    cos_t, sin_t = _rope_tables(dec_seq)
    ck = cache_k.reshape(dec_batch, DEPTH, past, ATT_KV)
    cv = cache_v.reshape(dec_batch, DEPTH, past, ATT_KV)

    xp = x_prompt.reshape(batch * seq, D_MODEL)
    xs = x_sample.reshape(dec_batch * dec_seq, D_MODEL)
    new_kv = new_st = None
    for l in range(DEPTH):
        w_att, w_ret, w_four = (w_br_att[l].astype(BF16), w_br_ret[l].astype(BF16),
                                w_br_four[l].astype(BF16))
        w_o = w_out[l].astype(BF16)
        n1, n2 = norm1[l][None, :], norm2[l][None, :]
        qn, kn = q_norm[l][:, None], k_norm[l][:, None]
        rn = ret_norm[l][None, :]
        dec = _ret_tables(jnp.broadcast_to(
            jnp.concatenate([ret_decay_f[l], ret_decay_b[l]])[:, None].astype(F32), (2 * RET_HEADS, 128)))
        cw, cb = conv_w[l], conv_b[l][None, :]
        mod_ctx = mod_all[l, 0:1].reshape(1, 1, 6 * D_MODEL)
        mod_lat = mod_all[l, 1:].reshape(dec_batch, 1, 6 * D_MODEL)

        p_act, p_gate = _in_proj(xp, mod_ctx, batch * seq, n1, w_in, l)
        o_att, *new_kv = _att_ctx(p_act, seq, qn, kn, l, new_kv)
        o_ret, *new_st = _retention(p_act, seq, dec, rn, l, new_states=new_st)
        o_four = _fourier(p_act, seq)
        xp = _mix(o_att, o_ret, o_four, p_gate, xp, seq, mod_ctx, batch * seq,
                  w_att, w_ret, w_four, w_o, n2, w_up, cw, cb, w_down, l)

        p_act, p_gate = _in_proj(xs, mod_lat, dec_seq, n1, w_in, l)
        o_att = _att_lat(p_act, dec_seq, ck, cv, l, qn, kn, cos_t, sin_t)
        o_ret = _retention(p_act, dec_seq, dec, rn, l, states=(state_ret_fwd, state_ret_bwd))
        o_four = _fourier(p_act, dec_seq)
        xs = _mix(o_att, o_ret, o_four, p_gate, xs, dec_seq, mod_lat, dec_seq,
                  w_att, w_ret, w_four, w_o, n2, w_up, cw, cb, w_down, l)

    kv_shape = (batch, DEPTH, seq, N_KV_HEADS, HEAD_DIM)
    return (xp.reshape(batch, seq, D_MODEL), xs.reshape(dec_batch, dec_seq, D_MODEL),
            new_kv[0].reshape(kv_shape), new_kv[1].reshape(kv_shape), new_st[0], new_st[1])
```

```python
import functools

import numpy as np
import jax
import jax.numpy as jnp
from jax import lax
from jax.experimental import pallas as pl
from jax.experimental.pallas import tpu as pltpu

D_MODEL = 1024
DEPTH = 2
GRID_W = 64
HEAD_DIM = 64
N_HEADS = 8
N_KV_HEADS = 2
Q_PER_KV = N_HEADS // N_KV_HEADS
ATT_Q = N_HEADS * HEAD_DIM
ATT_KV = N_KV_HEADS * HEAD_DIM
RET_HEADS = 4
RET_DK = 64
RET_DV = 128
RET_V = RET_HEADS * RET_DV
FOURIER_GROUPS = 4
FOURIER_GROUP_DIM = 128
FOURIER_W = FOURIER_GROUPS * FOURIER_GROUP_DIM
D_FF = 2816
CHUNK = 128
Q_BLOCK = 256
ROPE_THETA = 10000.0
EPS = 1e-6
LOG2_E = 1.4426950408889634

F32 = jnp.float32
BF16 = jnp.bfloat16

OFF_QA, OFF_KA, OFF_VA = 0, 512, 640
OFF_QR, OFF_KR, OFF_VR, OFF_GR, OFF_UF = 768, 1024, 1280, 1792, 2304
W_IN_SPLIT = 2816
OFF_GATE = 3072
P_W = OFF_GATE + 3 * D_MODEL

VMEM_LIMIT = 56 * 1024 * 1024


def _cparams(sem):
    return pltpu.CompilerParams(dimension_semantics=sem, vmem_limit_bytes=VMEM_LIMIT)


def _dot(a, b):
    return jnp.dot(a, b, preferred_element_type=F32)


def _ada_kernel(cond_ref, w_ref, b_ref, o_ref):
    cnd = cond_ref[...]
    s = cnd * jax.nn.sigmoid(cnd)
    o_ref[0] = _dot(s.astype(BF16), w_ref[0].astype(BF16)) + b_ref[0]


def _ada(cond_all, w_ada, b_ada):
    n = cond_all.shape[0]
    tn = 1024
    return pl.pallas_call(
        _ada_kernel,
        grid=(DEPTH, 6 * D_MODEL // tn),
        in_specs=[
            pl.BlockSpec((n, D_MODEL), lambda l, j: (0, 0)),
            pl.BlockSpec((1, D_MODEL, tn), lambda l, j: (l, 0, j)),
            pl.BlockSpec((1, 1, tn), lambda l, j: (l, 0, j)),
        ],
        out_specs=pl.BlockSpec((1, n, tn), lambda l, j: (l, 0, j)),
        out_shape=jax.ShapeDtypeStruct((DEPTH, n, 6 * D_MODEL), F32),
        compiler_params=_cparams(("parallel", "parallel")),
        name="ada",
    )(cond_all, w_ada, b_ada.reshape(DEPTH, 1, 6 * D_MODEL))


IN_TN = 1024
IN_ROWS = 256
MAIN_STEPS = OFF_GATE // IN_TN


def _in_kernel(x_ref, sh_ref, sc_ref, g_ref, w_ref, om_ref, og_ref, h_scr, w_scr):
    i = pl.program_id(0)
    j = pl.program_id(1)
    cols = pl.ds(pl.multiple_of(j * IN_TN, IN_TN), IN_TN)

    @pl.when(i == 0)
    def _():
        w_scr[:, cols] = w_ref[0].astype(BF16)

    @pl.when(j == 0)
    def _():
        for b in range(x_ref.shape[0] // IN_ROWS):
            rows = slice(b * IN_ROWS, (b + 1) * IN_ROWS)
            x = x_ref[rows, :]
            r = lax.rsqrt(jnp.mean(x * x, axis=-1, keepdims=True) + EPS)
            h = ((x * r) * g_ref[...] * (1.0 + sc_ref[0]) + sh_ref[0]).astype(BF16)
            h_scr[rows, :] = h
            om_ref[rows, :] = _dot(h, w_scr[:, 0:IN_TN])

    @pl.when(jnp.logical_and(j > 0, j < MAIN_STEPS))
    def _():
        om_ref[...] = _dot(h_scr[...], w_scr[:, cols])

    @pl.when(j >= MAIN_STEPS)
    def _():
        og_ref[...] = (0.5 * _dot(h_scr[...], w_scr[:, cols])).astype(og_ref.dtype)


def _in_proj(x2d, mod, rows_per_mod, norm1, w_in, layer):
    n_tok = x2d.shape[0]
    tm, tn = 1024, IN_TN
    n_col = P_W // tn
    mod_idx = lambda i: (i * tm) // rows_per_mod

    def w_col(i, j):
        jj = jnp.where(i == 0, j, n_col - 1)
        col = jnp.where(jj < MAIN_STEPS, jj * tn, W_IN_SPLIT + (jj - MAIN_STEPS) * tn)
        return pl.multiple_of(col, 128)

    return pl.pallas_call(
        _in_kernel,
        grid=(n_tok // tm, n_col),
        in_specs=[
            pl.BlockSpec((tm, D_MODEL), lambda i, j: (i, 0)),
            pl.BlockSpec((1, 1, D_MODEL), lambda i, j: (mod_idx(i), 0, 0)),
            pl.BlockSpec((1, 1, D_MODEL), lambda i, j: (mod_idx(i), 0, 1)),
            pl.BlockSpec((1, D_MODEL), lambda i, j: (0, 0)),
            pl.BlockSpec((pl.Element(1), pl.Element(D_MODEL), pl.Element(tn)),
                         lambda i, j: (layer, 0, w_col(i, j))),
        ],
        out_specs=[
            pl.BlockSpec((tm, tn), lambda i, j: (i, jnp.minimum(j, MAIN_STEPS - 1))),
            pl.BlockSpec((tm, tn), lambda i, j: (i, jnp.maximum(j - MAIN_STEPS, 0))),
        ],
        out_shape=[jax.ShapeDtypeStruct((n_tok, OFF_GATE), F32),
                   jax.ShapeDtypeStruct((n_tok, P_W - OFF_GATE), BF16)],
        scratch_shapes=[pltpu.VMEM((tm, D_MODEL), BF16), pltpu.VMEM((D_MODEL, P_W), BF16)],
        compiler_params=_cparams(("arbitrary", "arbitrary")),
        name="in_proj",
    )(x2d, mod, mod, norm1, w_in)


def _norm_rope_heads_t(x_t, n_heads, g_col, cos_t=None, sin_t=None):
    quarter = HEAD_DIM // 4
    outs = []
    for h in range(n_heads):
        x = x_t[h * HEAD_DIM:(h + 1) * HEAD_DIM, :]
        r = lax.rsqrt(jnp.mean(x * x, axis=0, keepdims=True) + EPS)
        y = (x * r) * g_col
        if cos_t is not None:
            pieces = []
            for a in range(2):
                c = cos_t[a * quarter:(a + 1) * quarter]
                s = sin_t[a * quarter:(a + 1) * quarter]
                x1 = y[2 * a * quarter:(2 * a + 1) * quarter]
                x2 = y[(2 * a + 1) * quarter:(2 * a + 2) * quarter]
                pieces += [x1 * c - x2 * s, x2 * c + x1 * s]
            y = jnp.concatenate(pieces, axis=0)
        outs.append(y)
    return outs


SHIFT_LIMIT = 40.0


def _score_bound(k_heads_t, q_gain):
    k_sq = [jnp.max(jnp.sum(k * k, axis=0, keepdims=True)) for k in k_heads_t]
    k_norm = jnp.sqrt(functools.reduce(jnp.maximum, k_sq))
    q_norm = HEAD_DIM ** 0.5 * jnp.max(jnp.abs(q_gain))
    return 1.01 * (HEAD_DIM ** -0.5 * LOG2_E) * q_norm * k_norm


def _attend_t(q_heads_t, k_bf, v_t_bf, score_bound, store):
    tq = q_heads_t[0].shape[1]
    cols = Q_PER_KV * tq
    zeros = jnp.zeros((HEAD_DIM, cols), F32)
    q_kv = [jnp.concatenate(q_heads_t[kv * Q_PER_KV:(kv + 1) * Q_PER_KV], axis=1)
            * (HEAD_DIM ** -0.5 * LOG2_E) for kv in range(N_KV_HEADS)]
    rhs = jnp.concatenate([jnp.concatenate([q_kv[0], zeros], axis=1),
                           jnp.concatenate([zeros, q_kv[1]], axis=1)], axis=0).astype(BF16)
    def finish(shift):
        s_all = _dot(k_bf, rhs)
        out_rows = []
        for kv in range(N_KV_HEADS):
            s_t = s_all[:, kv * cols:(kv + 1) * cols]
            m = jnp.max(s_t, axis=0, keepdims=True) if shift is None else shift
            e = jnp.exp2(s_t - m)
            inv = 1.0 / jnp.sum(e, axis=0, keepdims=True)
            o_t = _dot(v_t_bf[kv * HEAD_DIM:(kv + 1) * HEAD_DIM, :], e.astype(BF16)) * inv
            out_rows += [o_t[:, g * tq:(g + 1) * tq] for g in range(Q_PER_KV)]
        store(jnp.concatenate(out_rows, axis=0))

    if score_bound is None:
        finish(None)
        return

    @pl.when(score_bound <= SHIFT_LIMIT)
    def _():
        finish(score_bound)

    @pl.when(jnp.logical_not(score_bound <= SHIFT_LIMIT))
    def _():
        finish(None)


def _att_ctx_kernel(q_ref, k_ref, v_ref, qn_ref, kn_ref, *rest, slab):
    o_ref, ko_ref, vo_ref = rest[-3:]
    seq = ko_ref.shape[2]
    if len(rest) == 3:
        for other in range(ko_ref.shape[1]):
            if other != slab:
                ko_ref[:, other] = jnp.zeros_like(ko_ref[:, other])
                vo_ref[:, other] = jnp.zeros_like(vo_ref[:, other])
    for sq in range(ko_ref.shape[0]):
        rows = slice(sq * seq, (sq + 1) * seq)
        k_heads = _norm_rope_heads_t(k_ref[rows, :].T, N_KV_HEADS, kn_ref[...])
        k_n = jnp.concatenate(k_heads, axis=0).T
        ko_ref[sq, slab] = k_n
        v = v_ref[rows, :]
        vo_ref[sq, slab] = v
        q_heads = _norm_rope_heads_t(q_ref[rows, :].T, N_HEADS, qn_ref[...])

        def store(o_t, rows=rows):
            o_ref[rows, :] = o_t.T.astype(o_ref.dtype)

        _attend_t(q_heads, k_n.astype(BF16), v.T.astype(BF16), None, store)


ATT_CTX_ROWS = 1024


def _att_ctx(p_act, seq, q_norm, k_norm, layer, caches=None):
    n_tok = p_act.shape[0]
    batch = n_tok // seq
    rows = max(seq, ATT_CTX_ROWS)
    blk = rows // seq
    in_specs = [
        pl.BlockSpec((rows, ATT_Q), lambda b: (b, OFF_QA // ATT_Q)),
        pl.BlockSpec((rows, ATT_KV), lambda b: (b, OFF_KA // ATT_KV)),
        pl.BlockSpec((rows, ATT_KV), lambda b: (b, OFF_VA // ATT_KV)),
        pl.BlockSpec((HEAD_DIM, 1), lambda b: (0, 0)),
        pl.BlockSpec((HEAD_DIM, 1), lambda b: (0, 0)),
    ]
    args = [p_act, p_act, p_act, q_norm, k_norm]
    if caches is None:
        cache_spec = pl.BlockSpec((blk, DEPTH, seq, ATT_KV), lambda b: (b, 0, 0, 0))
        slab, aliases = layer, {}
    else:
        cache_spec = pl.BlockSpec((blk, 1, seq, ATT_KV), lambda b: (b, layer, 0, 0))
        slab, aliases = 0, {len(args): 1, len(args) + 1: 2}
        in_specs += [pl.BlockSpec(memory_space=pl.ANY)] * 2
        args += list(caches)
    cache_shape = jax.ShapeDtypeStruct((batch, DEPTH, seq, ATT_KV), F32)
    return pl.pallas_call(
        functools.partial(_att_ctx_kernel, slab=slab),
        grid=(n_tok // rows,),
        in_specs=in_specs,
        out_specs=[pl.BlockSpec((rows, ATT_Q), lambda b: (b, 0)), cache_spec, cache_spec],
        out_shape=[jax.ShapeDtypeStruct((n_tok, ATT_Q), BF16), cache_shape, cache_shape],
        input_output_aliases=aliases,
        compiler_params=_cparams(("parallel",)),
        name="att_ctx",
    )(*args)


def _att_lat_kernel(q_ref, k_ref, v_ref, ck_ref, cv_ref, qn_ref, kn_ref,
                    cq_ref, sq_ref, ck_t_ref, sk_t_ref, o_ref, kf_scr, vt_scr, bound_scr, *, seq):
    @pl.when(pl.program_id(1) == 0)
    def _():
        k_heads = _norm_rope_heads_t(k_ref[...].T, N_KV_HEADS, kn_ref[...], ck_t_ref[...], sk_t_ref[...])
        kf_scr[0:seq, :] = jnp.concatenate(k_heads, axis=0).T.astype(BF16)
        past_k = ck_ref[0, 0]
        kf_scr[seq:, :] = past_k.astype(BF16)
        vt_scr[:, 0:seq] = v_ref[...].T.astype(BF16)
        vt_scr[:, seq:] = cv_ref[0, 0].T.astype(BF16)
        past_t = past_k.T
        past_heads = [past_t[kv * HEAD_DIM:(kv + 1) * HEAD_DIM] for kv in range(N_KV_HEADS)]
        bound_scr[0] = _score_bound(k_heads + past_heads, qn_ref[...])

    q_heads = _norm_rope_heads_t(q_ref[...].T, N_HEADS, qn_ref[...], cq_ref[...], sq_ref[...])

    def store(o_t):
        o_ref[...] = o_t.T.astype(o_ref.dtype)

    _attend_t(q_heads, kf_scr[...], vt_scr[...], bound_scr[0], store)


def _att_lat(p_act, seq, cache_k, cache_v, layer, q_norm, k_norm, cos_t, sin_t):
    n_tok = p_act.shape[0]
    nb = seq // Q_BLOCK
    past = cache_k.shape[2]
    return pl.pallas_call(
        functools.partial(_att_lat_kernel, seq=seq),
        grid=(n_tok // seq, nb),
        in_specs=[
            pl.BlockSpec((Q_BLOCK, ATT_Q), lambda b, i: (b * nb + i, OFF_QA // ATT_Q)),
            pl.BlockSpec((seq, ATT_KV), lambda b, i: (b, OFF_KA // ATT_KV)),
            pl.BlockSpec((seq, ATT_KV), lambda b, i: (b, OFF_VA // ATT_KV)),
            pl.BlockSpec((1, 1, past, ATT_KV), lambda b, i: (b, layer, 0, 0)),
            pl.BlockSpec((1, 1, past, ATT_KV), lambda b, i: (b, layer, 0, 0)),
            pl.BlockSpec((HEAD_DIM, 1), lambda b, i: (0, 0)),
            pl.BlockSpec((HEAD_DIM, 1), lambda b, i: (0, 0)),
            pl.BlockSpec((HEAD_DIM // 2, Q_BLOCK), lambda b, i: (0, i)),
            pl.BlockSpec((HEAD_DIM // 2, Q_BLOCK), lambda b, i: (0, i)),
            pl.BlockSpec((HEAD_DIM // 2, seq), lambda b, i: (0, 0)),
            pl.BlockSpec((HEAD_DIM // 2, seq), lambda b, i: (0, 0)),
        ],
        out_specs=pl.BlockSpec((Q_BLOCK, ATT_Q), lambda b, i: (b * nb + i, 0)),
        out_shape=jax.ShapeDtypeStruct((n_tok, ATT_Q), BF16),
        scratch_shapes=[pltpu.VMEM((seq + past, ATT_KV), BF16),
                        pltpu.VMEM((ATT_KV, seq + past), BF16),
                        pltpu.SMEM((1,), F32)],
        compiler_params=_cparams(("parallel", "arbitrary")),
        name="att_lat",
    )(p_act, p_act, p_act, cache_k, cache_v, q_norm, k_norm, cos_t, sin_t, cos_t, sin_t)


RET_ROWS = 1024


def _log_sigmoid(d):
    return jnp.minimum(d, 0.0) - jnp.log1p(jnp.exp(-jnp.abs(d)))


TAB_MASK, TAB_QF, TAB_QB, TAB_KF, TAB_KB, TAB_C, N_TAB = 0, 2, 4, 6, 7, 8, 9
TAB_CF, TAB_CB = 0, 2


def _ret_tables_kernel(dec_ref, tab_ref):
    hp = pl.program_id(0)
    ii = lax.broadcasted_iota(jnp.int32, (CHUNK, CHUNK), 0)
    jj = lax.broadcasted_iota(jnp.int32, (CHUNK, CHUNK), 1)
    rel = (ii - jj).astype(F32)
    row = ii.astype(F32)
    lane = jj.astype(F32)
    lgf = [_log_sigmoid(dec_ref[pl.ds(2 * hp + t, 1), :]) for t in range(2)]
    lgb = [_log_sigmoid(dec_ref[pl.ds(RET_HEADS + 2 * hp + t, 1), :]) for t in range(2)]
    for t in range(2):
        tab_ref[0, TAB_MASK + t] = jnp.where(
            rel > 0, jnp.exp(jnp.maximum(rel, 0.0) * lgf[t]),
            jnp.where(rel < 0, jnp.exp(jnp.maximum(-rel, 0.0) * lgb[t]), 2.0))
        tab_ref[0, TAB_QF + t] = jnp.exp((row + 1.0) * lgf[t])
        tab_ref[0, TAB_QB + t] = jnp.exp((CHUNK - row) * lgb[t])
    tab_ref[0, TAB_KF] = jnp.exp((CHUNK - 1.0 - lane) * jnp.where(ii < RET_DK, lgf[0], lgf[1]))
    tab_ref[0, TAB_KB] = jnp.exp(lane * jnp.where(ii < RET_DK, lgb[0], lgb[1]))
    c_rows = jnp.where(ii == TAB_CF, lgf[0], jnp.where(ii == TAB_CF + 1, lgf[1],
                       jnp.where(ii == TAB_CB, lgb[0], lgb[1])))
    tab_ref[0, TAB_C] = jnp.exp(CHUNK * c_rows)


def _ret_tables(dec):
    return pl.pallas_call(
        _ret_tables_kernel,
        grid=(RET_HEADS // 2,),
        in_specs=[pl.BlockSpec((8, 128), lambda p: (0, 0))],
        out_specs=pl.BlockSpec((1, N_TAB, CHUNK, CHUNK), lambda p: (p, 0, 0, 0)),
        out_shape=jax.ShapeDtypeStruct((RET_HEADS // 2, N_TAB, CHUNK, CHUNK), F32),
        compiler_params=_cparams(("parallel",)),
        name="ret_tables",
    )(dec)


def _ret_kernel(q_ref, k_ref, v_ref, g_ref, tab_ref, rn_ref, *rest, seq, has_state, slab):
    if has_state:
        s0f_ref, s0b_ref, o_ref = rest
    else:
        o_ref, sf_ref, sb_ref = rest[-3:]
        if len(rest) == 3:
            for other in range(sf_ref.shape[1]):
                if other != slab:
                    sf_ref[:, other] = jnp.zeros_like(sf_ref[:, other])
                    sb_ref[:, other] = jnp.zeros_like(sb_ref[:, other])
    n_chunks = q_ref.shape[0] // CHUNK
    jj = lax.broadcasted_iota(jnp.int32, (CHUNK, CHUNK), 1)
    mask2 = jnp.concatenate([tab_ref[0, TAB_MASK], tab_ref[0, TAB_MASK + 1]], axis=0)
    qdec_f = [tab_ref[0, TAB_QF + t] for t in range(2)]
    qdec_b = [tab_ref[0, TAB_QB + t] for t in range(2)]
    cdec_f = [tab_ref[0, TAB_C, TAB_CF + t:TAB_CF + t + 1, :] for t in range(2)]
    cdec_b = [tab_ref[0, TAB_C, TAB_CB + t:TAB_CB + t + 1, :] for t in range(2)]
    kdec_f = tab_ref[0, TAB_KF]
    kdec_b = tab_ref[0, TAB_KB]

    k_t = (k_ref[...] * (RET_DK ** -0.5)).T
    first_head = jj < RET_DK

    def chunk(c):
        return slice(c * CHUNK, (c + 1) * CHUNK)

    kv_f = [[None] * n_chunks for _ in range(2)]
    kv_b = [[None] * n_chunks for _ in range(2)]
    for c in range(n_chunks):
        k_c = k_t[:, chunk(c)]
        kd = jnp.concatenate([k_c * kdec_f, k_c * kdec_b], axis=0).astype(BF16)
        kv = _dot(kd, v_ref[chunk(c), :].astype(BF16))
        for t in range(2):
            kv_f[t][c] = kv[t * RET_DK:(t + 1) * RET_DK, t * RET_DV:(t + 1) * RET_DV]
            kv_b[t][c] = kv[CHUNK + t * RET_DK:CHUNK + (t + 1) * RET_DK, t * RET_DV:(t + 1) * RET_DV]

    st_f = [[None] * n_chunks for _ in range(2)]
    st_b = [[None] * n_chunks for _ in range(2)]
    per_seq = seq // CHUNK
    for t in range(2):
        for sq in range(n_chunks // per_seq):
            own = range(sq * per_seq, (sq + 1) * per_seq)
            if has_state:
                s_f = s0f_ref[sq, 0, t]
                s_b = s0b_ref[sq, 0, t]
            else:
                s_f = jnp.zeros((RET_DK, RET_DV), F32)
                s_b = s_f
            for c in own:
                st_f[t][c] = s_f
                s_f = s_f * cdec_f[t] + kv_f[t][c]
            for c in reversed(own):
                st_b[t][c] = s_b
                s_b = s_b * cdec_b[t] + kv_b[t][c]
            if not has_state:
                sf_ref[sq, slab, t] = s_f
                sb_ref[sq, slab, t] = s_b

    for c in range(n_chunks):
        q_c = q_ref[chunk(c), :]
        qm = jnp.concatenate([jnp.where(first_head, q_c, 0.0), jnp.where(first_head, 0.0, q_c)],
                             axis=0).astype(BF16)
        att = (_dot(qm, k_t[:, chunk(c)].astype(BF16)) * mask2).astype(BF16)
        states = jnp.concatenate(
            [jnp.concatenate([st_f[t][c], st_b[t][c]], axis=1) for t in range(2)], axis=0)
        qs = _dot(qm, states.astype(BF16))
        v_c = v_ref[chunk(c), :].astype(BF16)
        for t in range(2):
            rows = slice(t * CHUNK, (t + 1) * CHUNK)
            vsl = slice(t * RET_DV, (t + 1) * RET_DV)
            o = (_dot(att[rows], v_c[:, vsl]) + qs[rows, :RET_DV] * qdec_f[t]
                 + qs[rows, RET_DV:] * qdec_b[t])
            o = (o * lax.rsqrt(jnp.mean(o * o, axis=-1, keepdims=True) + EPS)) * rn_ref[...]
            hg = 0.5 * g_ref[chunk(c), vsl]
            o_ref[chunk(c), vsl] = (o * (hg * (1.0 + jnp.tanh(hg)))).astype(o_ref.dtype)


def _retention(p_act, seq, tables, ret_norm, layer, states=None, new_states=None):
    n_tok = p_act.shape[0]
    n_seq = n_tok // seq
    has_state = states is not None
    pair_qk, pair_v = 2 * RET_DK, 2 * RET_DV
    rows = max(seq, RET_ROWS)
    blk_seqs = rows // seq
    in_specs = [
        pl.BlockSpec((rows, pair_qk), lambda p, s: (s, OFF_QR // pair_qk + p)),
        pl.BlockSpec((rows, pair_qk), lambda p, s: (s, OFF_KR // pair_qk + p)),
        pl.BlockSpec((rows, pair_v), lambda p, s: (s, OFF_VR // pair_v + p)),
        pl.BlockSpec((rows, pair_v), lambda p, s: (s, OFF_GR // pair_v + p)),
        pl.BlockSpec((1, N_TAB, CHUNK, CHUNK), lambda p, s: (p, 0, 0, 0)),
        pl.BlockSpec((1, RET_DV), lambda p, s: (0, 0)),
    ]
    args = [p_act, p_act, p_act, p_act, tables, ret_norm]
    o_spec = pl.BlockSpec((rows, pair_v), lambda p, s: (s, p))
    o_shape = jax.ShapeDtypeStruct((n_tok, RET_V), BF16)
    st_spec = pl.BlockSpec((blk_seqs, 1, 2, RET_DK, RET_DV), lambda p, s: (s, layer, p, 0, 0))
    aliases, slab = {}, 0
    if has_state:
        in_specs += [st_spec, st_spec]
        args += list(states)
        out_specs, out_shape = o_spec, o_shape
    else:
        if new_states is None:
            st_spec = pl.BlockSpec((blk_seqs, DEPTH, 2, RET_DK, RET_DV), lambda p, s: (s, 0, p, 0, 0))
            slab = layer
        else:
            in_specs += [pl.BlockSpec(memory_space=pl.ANY)] * 2
            aliases = {len(args): 1, len(args) + 1: 2}
            args += list(new_states)
        st_shape = jax.ShapeDtypeStruct((n_seq, DEPTH, RET_HEADS, RET_DK, RET_DV), F32)
        out_specs, out_shape = [o_spec, st_spec, st_spec], [o_shape, st_shape, st_shape]
    return pl.pallas_call(
        functools.partial(_ret_kernel, seq=seq, has_state=has_state, slab=slab),
        grid=(RET_HEADS // 2, n_tok // rows),
        in_specs=in_specs,
        out_specs=out_specs,
        out_shape=out_shape,
        input_output_aliases=aliases,
        compiler_params=_cparams(("parallel", "parallel")),
        name="retention_lat" if has_state else "retention_ctx",
    )(*args)


FOURIER_ROWS = 1024


def _four_kernel(u0_ref, u1_ref, wc_ref, csn_ref, o_ref):
    gd = FOURIER_GROUP_DIM
    tc, ts = [], []
    for g in range(FOURIER_GROUPS):
        u_ref = (u0_ref, u1_ref)[g // 2]
        u_g = u_ref[:, (g % 2) * gd:(g % 2 + 1) * gd].astype(BF16)
        t = _dot(u_g, wc_ref[...])
        tc.append(t[:, :gd])
        ts.append(t[:, gd:])
    tc = jnp.concatenate(tc, axis=1).astype(BF16)
    ts = jnp.concatenate(ts, axis=1).astype(BF16)
    seq = csn_ref.shape[0]
    for sq in range(o_ref.shape[0] // seq):
        rows = slice(sq * seq, (sq + 1) * seq)
        t_sq = jnp.concatenate([tc[rows], ts[rows]], axis=0)
        o_ref[rows, :] = _dot(csn_ref[...], t_sq).astype(o_ref.dtype)


def _dft_tables(n):
    k = np.arange(n, dtype=np.int64)
    ang = 2.0 * np.pi * ((k[:, None] * k[None, :]) % n).astype(np.float64) / n
    scale = 1.0 / np.sqrt(n)
    return np.cos(ang) * scale, np.sin(ang) * scale


def _fourier(p_act, seq):
    n_tok = p_act.shape[0]
    cc, sc = _dft_tables(FOURIER_GROUP_DIM)
    wc = jnp.asarray(np.concatenate([cc, -sc], axis=1), F32).astype(BF16)
    cn, sn = _dft_tables(seq)
    csn = jnp.asarray(np.concatenate([cn, sn], axis=1), F32).astype(BF16)
    const = lambda shape: pl.BlockSpec(shape, lambda s: (0, 0))
    half = FOURIER_W // 2
    rows = max(seq, FOURIER_ROWS)
    return pl.pallas_call(
        _four_kernel,
        grid=(n_tok // rows,),
        in_specs=[
            pl.BlockSpec((rows, half), lambda s: (s, OFF_UF // half)),
            pl.BlockSpec((rows, half), lambda s: (s, OFF_UF // half + 1)),
            const((FOURIER_GROUP_DIM, 2 * FOURIER_GROUP_DIM)),
            const((seq, 2 * seq)),
        ],
        out_specs=pl.BlockSpec((rows, FOURIER_W), lambda s: (s, 0)),
        out_shape=jax.ShapeDtypeStruct((n_tok, FOURIER_W), BF16),
        compiler_params=_cparams(("parallel",)),
        name="fourier",
    )(p_act, p_act, wc, csn)


MIX_TM = 1024
MERGE_STEPS = 4
MERGE_TM = MIX_TM // MERGE_STEPS
FF_TILE = 256
FF_STEPS = D_FF // FF_TILE
FF_ROWS = 256


def _mix_kernel(oa_ref, or_ref, of_ref, ga_ref, gr_ref, gf_ref, x_ref, g1_ref, sh_ref, sc_ref,
                g2_ref, n2_ref, wba_ref, wbr_ref, wbf_ref, wo_ref, wa_ref, wv_ref, cw_ref, cb_ref,
                wd_ref, o_ref, h_scr, acc_scr, wu_scr, wd_scr, *, seq):
    i = pl.program_id(0)
    j = pl.program_id(1)
    chunk = jnp.maximum(j - MERGE_STEPS, 0)
    a_cols = pl.ds(pl.multiple_of(chunk * FF_TILE, FF_TILE), FF_TILE)
    v_cols = pl.ds(pl.multiple_of((FF_STEPS + chunk) * FF_TILE, FF_TILE), FF_TILE)

    @pl.when(j < MERGE_STEPS)
    def _():
        gate = lambda ref: jnp.tanh(ref[...].astype(F32)) + 1.0
        dst = pl.ds(pl.multiple_of(j * MERGE_TM, MERGE_TM), MERGE_TM)
        merged = (gate(ga_ref) * _dot(oa_ref[...], wba_ref[...])
                  + gate(gr_ref) * _dot(or_ref[...], wbr_ref[...])
                  + gate(gf_ref) * _dot(of_ref[...], wbf_ref[...]))
        x1 = x_ref[...] + g1_ref[0] * _dot((0.5 * merged).astype(BF16), wo_ref[...])
        o_ref[dst, :] = x1
        r = lax.rsqrt(jnp.mean(x1 * x1, axis=-1, keepdims=True) + EPS)
        h = (x1 * r) * n2_ref[...] * (1.0 + sc_ref[0]) + sh_ref[0]
        h_scr[dst, :] = h.astype(BF16)
        acc_scr[dst, :] = jnp.zeros((MERGE_TM, D_MODEL), F32)

    @pl.when(jnp.logical_and(i == 0, j >= MERGE_STEPS))
    def _():
        wu_scr[:, a_cols] = wa_ref[0].astype(BF16)
        wu_scr[:, v_cols] = wv_ref[0].astype(BF16)
        wd_scr[a_cols, :] = wd_ref[0].astype(BF16)

    def mlp(last):
        blocks = [slice(b * FF_ROWS, (b + 1) * FF_ROWS) for b in range(MIX_TM // FF_ROWS)]
        wa, wv, wd = wu_scr[:, a_cols], wu_scr[:, v_cols], wd_scr[a_cols, :]
        a = jnp.concatenate([_dot(h_scr[rows, :], wa) for rows in blocks], axis=0)
        val = [_dot(h_scr[rows, :], wv) for rows in blocks]
        pos = lax.broadcasted_iota(jnp.int32, a.shape, 0) % seq
        prev = jnp.where(pos == 0, 0.0, pltpu.roll(a, 1, 0))
        nxt = jnp.where(pos == seq - 1, 0.0, pltpu.roll(a, MIX_TM - 1, 0))
        cw = cw_ref[...]
        ac = prev * cw[0:1] + a * cw[1:2] + nxt * cw[2:3] + cb_ref[...]
        for b, rows in enumerate(blocks):
            act = jax.nn.gelu(ac[rows]) * val[b]
            down = _dot(act.astype(BF16), wd)
            if last:
                o_ref[rows, :] = o_ref[rows, :] + g2_ref[0] * (acc_scr[rows, :] + down)
            else:
                acc_scr[rows, :] += down

    last_step = pl.num_programs(1) - 1

    @pl.when(jnp.logical_and(j >= MERGE_STEPS, j < last_step))
    def _():
        mlp(last=False)

    @pl.when(j == last_step)
    def _():
        mlp(last=True)


def _mix(o_att, o_ret, o_four, p_gate, x2d, seq, mod, rows_per_mod, w_att, w_ret, w_four, w_out,
         norm2, w_up, conv_w, conv_b, w_down, layer):
    n_tok = x2d.shape[0]
    mod_idx = lambda i: (i * MIX_TM) // rows_per_mod
    sub = lambda i, j: i * MERGE_STEPS + jnp.minimum(j, MERGE_STEPS - 1)
    ff = lambda j: jnp.maximum(j - MERGE_STEPS, 0)
    ff_w = lambda i, j: jnp.where(i == 0, ff(j), FF_STEPS - 1)
    br = lambda: pl.BlockSpec((MERGE_TM, 512), lambda i, j: (sub(i, j), 0))
    gate = lambda k: pl.BlockSpec((MERGE_TM, D_MODEL), lambda i, j: (sub(i, j), k))
    modv = lambda k: pl.BlockSpec((1, 1, D_MODEL), lambda i, j: (mod_idx(i), 0, k))
    const = lambda shape: pl.BlockSpec(shape, lambda i, j: (0, 0))
    return pl.pallas_call(
        functools.partial(_mix_kernel, seq=seq),
        grid=(n_tok // MIX_TM, MERGE_STEPS + FF_STEPS),
        in_specs=[
            br(), br(), br(), gate(0), gate(1), gate(2),
            pl.BlockSpec((MERGE_TM, D_MODEL), lambda i, j: (sub(i, j), 0)),
            modv(2), modv(3), modv(4), modv(5),
            const((1, D_MODEL)),
            const((512, D_MODEL)), const((512, D_MODEL)), const((512, D_MODEL)),
            const((D_MODEL, D_MODEL)),
            pl.BlockSpec((1, D_MODEL, FF_TILE), lambda i, j: (layer, 0, ff_w(i, j))),
            pl.BlockSpec((1, D_MODEL, FF_TILE), lambda i, j: (layer, 0, FF_STEPS + ff_w(i, j))),
            pl.BlockSpec((3, FF_TILE), lambda i, j: (0, ff(j))),
            pl.BlockSpec((1, FF_TILE), lambda i, j: (0, ff(j))),
            pl.BlockSpec((1, FF_TILE, D_MODEL), lambda i, j: (layer, ff_w(i, j), 0)),
        ],
        out_specs=pl.BlockSpec((MIX_TM, D_MODEL), lambda i, j: (i, 0)),
        out_shape=jax.ShapeDtypeStruct((n_tok, D_MODEL), F32),
        scratch_shapes=[pltpu.VMEM((MIX_TM, D_MODEL), BF16), pltpu.VMEM((MIX_TM, D_MODEL), F32),
                        pltpu.VMEM((D_MODEL, 2 * D_FF), BF16), pltpu.VMEM((D_FF, D_MODEL), BF16)],
        compiler_params=_cparams(("arbitrary", "arbitrary")),
        name="mix",
    )(o_att, o_ret, o_four, p_gate, p_gate, p_gate, x2d, mod, mod, mod, mod, norm2,
      w_att, w_ret, w_four, w_out, w_up, w_up, conv_w, conv_b, w_down)


def _rope_tables(n_tok):
    rows = n_tok // GRID_W
    row_id = jnp.repeat(jnp.arange(rows), GRID_W).astype(F32)
    col_id = jnp.tile(jnp.arange(GRID_W), rows).astype(F32)
    n_freq = HEAD_DIM // 4
    inv = ROPE_THETA ** (-jnp.arange(n_freq, dtype=F32) / n_freq)
    ang = jnp.concatenate([row_id[None, :] * inv[:, None], col_id[None, :] * inv[:, None]], axis=0)
    return jnp.cos(ang), jnp.sin(ang)


def kernel(x_prompt, x_sample, cache_k, cache_v, state_ret_fwd, state_ret_bwd, c, c_ctx, w_ada, b_ada, norm1, w_in, q_norm, k_norm, ret_decay_f, ret_decay_b, ret_norm, w_br_att, w_br_ret, w_br_four, w_out, norm2, w_up, conv_w, conv_b, w_down):
    batch, seq, _ = x_prompt.shape
    dec_batch, dec_seq, _ = x_sample.shape
    past = cache_k.shape[2]
    for n_seq, length in ((batch, seq), (dec_batch, dec_seq)):
        assert length % CHUNK == 0 and MIX_TM % length == 0 and (n_seq * length) % MIX_TM == 0
    assert dec_seq % Q_BLOCK == 0 and dec_seq % GRID_W == 0 and past % CHUNK == 0

    cond_all = jnp.concatenate([c_ctx[None, :], c], axis=0)
    mod_all = _ada(cond_all, w_ada, b_ada)
    cos_t, sin_t = _rope_tables(dec_seq)
    ck = cache_k.reshape(dec_batch, DEPTH, past, ATT_KV)
    cv = cache_v.reshape(dec_batch, DEPTH, past, ATT_KV)

    xp = x_prompt.reshape(batch * seq, D_MODEL)
    xs = x_sample.reshape(dec_batch * dec_seq, D_MODEL)
    new_kv = new_st = None
    for l in range(DEPTH):
        w_att, w_ret, w_four = (w_br_att[l].astype(BF16), w_br_ret[l].astype(BF16),
                                w_br_four[l].astype(BF16))
        w_o = w_out[l].astype(BF16)
        n1, n2 = norm1[l][None, :], norm2[l][None, :]
        qn, kn = q_norm[l][:, None], k_norm[l][:, None]
        rn = ret_norm[l][None, :]
        dec = _ret_tables(jnp.broadcast_to(
            jnp.concatenate([ret_decay_f[l], ret_decay_b[l]])[:, None].astype(F32), (2 * RET_HEADS, 128)))
        cw, cb = conv_w[l], conv_b[l][None, :]
        mod_ctx = mod_all[l, 0:1].reshape(1, 1, 6 * D_MODEL)
        mod_lat = mod_all[l, 1:].reshape(dec_batch, 1, 6 * D_MODEL)

        p_act, p_gate = _in_proj(xp, mod_ctx, batch * seq, n1, w_in, l)
        o_att, *new_kv = _att_ctx(p_act, seq, qn, kn, l, new_kv)
        o_ret, *new_st = _retention(p_act, seq, dec, rn, l, new_states=new_st)
        o_four = _fourier(p_act, seq)
        xp = _mix(o_att, o_ret, o_four, p_gate, xp, seq, mod_ctx, batch * seq,
                  w_att, w_ret, w_four, w_o, n2, w_up, cw, cb, w_down, l)

        p_act, p_gate = _in_proj(xs, mod_lat, dec_seq, n1, w_in, l)
        o_att = _att_lat(p_act, dec_seq, ck, cv, l, qn, kn, cos_t, sin_t)
        o_ret = _retention(p_act, dec_seq, dec, rn, l, states=(state_ret_fwd, state_ret_bwd))
        o_four = _fourier(p_act, dec_seq)
        xs = _mix(o_att, o_ret, o_four, p_gate, xs, dec_seq, mod_lat, dec_seq,
                  w_att, w_ret, w_four, w_o, n2, w_up, cw, cb, w_down, l)

    kv_shape = (batch, DEPTH, seq, N_KV_HEADS, HEAD_DIM)
    return (xp.reshape(batch, seq, D_MODEL), xs.reshape(dec_batch, dec_seq, D_MODEL),
            new_kv[0].reshape(kv_shape), new_kv[1].reshape(kv_shape), new_st[0], new_st[1])
```

```python
import functools

import numpy as np
import jax
import jax.numpy as jnp
from jax import lax
from jax.experimental import pallas as pl
from jax.experimental.pallas import tpu as pltpu

D_MODEL = 1024
DEPTH = 2
GRID_W = 64
HEAD_DIM = 64
N_HEADS = 8
N_KV_HEADS = 2
Q_PER_KV = N_HEADS // N_KV_HEADS
ATT_Q = N_HEADS * HEAD_DIM
ATT_KV = N_KV_HEADS * HEAD_DIM
RET_HEADS = 4
RET_DK = 64
RET_DV = 128
RET_V = RET_HEADS * RET_DV
FOURIER_GROUPS = 4
FOURIER_GROUP_DIM = 128
FOURIER_W = FOURIER_GROUPS * FOURIER_GROUP_DIM
D_FF = 2816
CHUNK = 128
Q_BLOCK = 256
ROPE_THETA = 10000.0
EPS = 1e-6
LOG2_E = 1.4426950408889634

F32 = jnp.float32
BF16 = jnp.bfloat16

OFF_QA, OFF_KA, OFF_VA = 0, 512, 640
OFF_QR, OFF_KR, OFF_VR, OFF_GR, OFF_UF = 768, 1024, 1280, 1792, 2304
W_IN_SPLIT = 2816
OFF_GATE = 3072
P_W = OFF_GATE + 3 * D_MODEL

VMEM_LIMIT = 56 * 1024 * 1024


def _cparams(sem):
    return pltpu.CompilerParams(dimension_semantics=sem, vmem_limit_bytes=VMEM_LIMIT)


def _dot(a, b):
    return jnp.dot(a, b, preferred_element_type=F32)


def _ada_kernel(cond_ref, w_ref, b_ref, o_ref):
    cnd = cond_ref[...]
    s = cnd * jax.nn.sigmoid(cnd)
    o_ref[0] = _dot(s.astype(BF16), w_ref[0].astype(BF16)) + b_ref[0]


def _ada(cond_all, w_ada, b_ada):
    n = cond_all.shape[0]
    tn = 1024
    return pl.pallas_call(
        _ada_kernel,
        grid=(DEPTH, 6 * D_MODEL // tn),
        in_specs=[
            pl.BlockSpec((n, D_MODEL), lambda l, j: (0, 0)),
            pl.BlockSpec((1, D_MODEL, tn), lambda l, j: (l, 0, j)),
            pl.BlockSpec((1, 1, tn), lambda l, j: (l, 0, j)),
        ],
        out_specs=pl.BlockSpec((1, n, tn), lambda l, j: (l, 0, j)),
        out_shape=jax.ShapeDtypeStruct((DEPTH, n, 6 * D_MODEL), F32),
        compiler_params=_cparams(("parallel", "parallel")),
        name="ada",
    )(cond_all, w_ada, b_ada.reshape(DEPTH, 1, 6 * D_MODEL))


IN_TN = 1024
IN_ROWS = 256
MAIN_STEPS = OFF_GATE // IN_TN


def _in_kernel(x_ref, sh_ref, sc_ref, g_ref, w_ref, om_ref, og_ref, h_scr, w_scr):
    i = pl.program_id(0)
    j = pl.program_id(1)
    cols = pl.ds(pl.multiple_of(j * IN_TN, IN_TN), IN_TN)

    @pl.when(i == 0)
    def _():
        w_scr[:, cols] = w_ref[0].astype(BF16)

    @pl.when(j == 0)
    def _():
        for b in range(x_ref.shape[0] // IN_ROWS):
            rows = slice(b * IN_ROWS, (b + 1) * IN_ROWS)
            x = x_ref[rows, :]
            r = lax.rsqrt(jnp.mean(x * x, axis=-1, keepdims=True) + EPS)
            h = ((x * r) * g_ref[...] * (1.0 + sc_ref[0]) + sh_ref[0]).astype(BF16)
            h_scr[rows, :] = h
            om_ref[rows, :] = _dot(h, w_scr[:, 0:IN_TN])

    @pl.when(jnp.logical_and(j > 0, j < MAIN_STEPS))
    def _():
        om_ref[...] = _dot(h_scr[...], w_scr[:, cols])

    @pl.when(j >= MAIN_STEPS)
    def _():
        og_ref[...] = (0.5 * _dot(h_scr[...], w_scr[:, cols])).astype(og_ref.dtype)


def _in_proj(x2d, mod, rows_per_mod, norm1, w_in, layer):
    n_tok = x2d.shape[0]
    tm, tn = 1024, IN_TN
    n_col = P_W // tn
    mod_idx = lambda i: (i * tm) // rows_per_mod

    def w_col(i, j):
        jj = jnp.where(i == 0, j, n_col - 1)
        col = jnp.where(jj < MAIN_STEPS, jj * tn, W_IN_SPLIT + (jj - MAIN_STEPS) * tn)
        return pl.multiple_of(col, 128)

    return pl.pallas_call(
        _in_kernel,
        grid=(n_tok // tm, n_col),
        in_specs=[
            pl.BlockSpec((tm, D_MODEL), lambda i, j: (i, 0)),
            pl.BlockSpec((1, 1, D_MODEL), lambda i, j: (mod_idx(i), 0, 0)),
            pl.BlockSpec((1, 1, D_MODEL), lambda i, j: (mod_idx(i), 0, 1)),
            pl.BlockSpec((1, D_MODEL), lambda i, j: (0, 0)),
            pl.BlockSpec((pl.Element(1), pl.Element(D_MODEL), pl.Element(tn)),
                         lambda i, j: (layer, 0, w_col(i, j))),
        ],
        out_specs=[
            pl.BlockSpec((tm, tn), lambda i, j: (i, jnp.minimum(j, MAIN_STEPS - 1))),
            pl.BlockSpec((tm, tn), lambda i, j: (i, jnp.maximum(j - MAIN_STEPS, 0))),
        ],
        out_shape=[jax.ShapeDtypeStruct((n_tok, OFF_GATE), F32),
                   jax.ShapeDtypeStruct((n_tok, P_W - OFF_GATE), BF16)],
        scratch_shapes=[pltpu.VMEM((tm, D_MODEL), BF16), pltpu.VMEM((D_MODEL, P_W), BF16)],
        compiler_params=_cparams(("arbitrary", "arbitrary")),
        name="in_proj",
    )(x2d, mod, mod, norm1, w_in)


def _norm_rope_heads_t(x_t, n_heads, g_col, cos_t=None, sin_t=None):
    quarter = HEAD_DIM // 4
    outs = []
    for h in range(n_heads):
        x = x_t[h * HEAD_DIM:(h + 1) * HEAD_DIM, :]
        r = lax.rsqrt(jnp.mean(x * x, axis=0, keepdims=True) + EPS)
        y = (x * r) * g_col
        if cos_t is not None:
            pieces = []
            for a in range(2):
                c = cos_t[a * quarter:(a + 1) * quarter]
                s = sin_t[a * quarter:(a + 1) * quarter]
                x1 = y[2 * a * quarter:(2 * a + 1) * quarter]
                x2 = y[(2 * a + 1) * quarter:(2 * a + 2) * quarter]
                pieces += [x1 * c - x2 * s, x2 * c + x1 * s]
            y = jnp.concatenate(pieces, axis=0)
        outs.append(y)
    return outs


SHIFT_LIMIT = 40.0


def _score_bound(k_heads_t, q_gain):
    k_sq = [jnp.max(jnp.sum(k * k, axis=0, keepdims=True)) for k in k_heads_t]
    k_norm = jnp.sqrt(functools.reduce(jnp.maximum, k_sq))
    q_norm = HEAD_DIM ** 0.5 * jnp.max(jnp.abs(q_gain))
    return 1.01 * (HEAD_DIM ** -0.5 * LOG2_E) * q_norm * k_norm


def _attend_t(q_heads_t, k_bf, v_t_bf, score_bound, store):
    tq = q_heads_t[0].shape[1]
    cols = Q_PER_KV * tq
    zeros = jnp.zeros((HEAD_DIM, cols), F32)
    q_kv = [jnp.concatenate(q_heads_t[kv * Q_PER_KV:(kv + 1) * Q_PER_KV], axis=1)
            * (HEAD_DIM ** -0.5 * LOG2_E) for kv in range(N_KV_HEADS)]
    rhs = jnp.concatenate([jnp.concatenate([q_kv[0], zeros], axis=1),
                           jnp.concatenate([zeros, q_kv[1]], axis=1)], axis=0).astype(BF16)
    def finish(shift):
        s_all = _dot(k_bf, rhs)
        out_rows = []
        for kv in range(N_KV_HEADS):
            s_t = s_all[:, kv * cols:(kv + 1) * cols]
            m = jnp.max(s_t, axis=0, keepdims=True) if shift is None else shift
            e = jnp.exp2(s_t - m)
            inv = 1.0 / jnp.sum(e, axis=0, keepdims=True)
            o_t = _dot(v_t_bf[kv * HEAD_DIM:(kv + 1) * HEAD_DIM, :], e.astype(BF16)) * inv
            out_rows += [o_t[:, g * tq:(g + 1) * tq] for g in range(Q_PER_KV)]
        store(jnp.concatenate(out_rows, axis=0))

    if score_bound is None:
        finish(None)
        return

    @pl.when(score_bound <= SHIFT_LIMIT)
    def _():
        finish(score_bound)

    @pl.when(jnp.logical_not(score_bound <= SHIFT_LIMIT))
    def _():
        finish(None)


def _att_ctx_kernel(q_ref, k_ref, v_ref, qn_ref, kn_ref, *rest, slab):
    o_ref, ko_ref, vo_ref = rest[-3:]
    seq = ko_ref.shape[2]
    if len(rest) == 3:
        for other in range(ko_ref.shape[1]):
            if other != slab:
                ko_ref[:, other] = jnp.zeros_like(ko_ref[:, other])
                vo_ref[:, other] = jnp.zeros_like(vo_ref[:, other])
    for sq in range(ko_ref.shape[0]):
        rows = slice(sq * seq, (sq + 1) * seq)
        k_heads = _norm_rope_heads_t(k_ref[rows, :].T, N_KV_HEADS, kn_ref[...])
        k_n = jnp.concatenate(k_heads, axis=0).T
        ko_ref[sq, slab] = k_n
        v = v_ref[rows, :]
        vo_ref[sq, slab] = v
        q_heads = _norm_rope_heads_t(q_ref[rows, :].T, N_HEADS, qn_ref[...])

        def store(o_t, rows=rows):
            o_ref[rows, :] = o_t.T.astype(o_ref.dtype)

        _attend_t(q_heads, k_n.astype(BF16), v.T.astype(BF16), None, store)


ATT_CTX_ROWS = 2048


def _att_ctx(p_act, seq, q_norm, k_norm, layer, caches=None):
    n_tok = p_act.shape[0]
    batch = n_tok // seq
    rows = max(seq, ATT_CTX_ROWS)
    blk = rows // seq
    in_specs = [
        pl.BlockSpec((rows, ATT_Q), lambda b: (b, OFF_QA // ATT_Q)),
        pl.BlockSpec((rows, ATT_KV), lambda b: (b, OFF_KA // ATT_KV)),
        pl.BlockSpec((rows, ATT_KV), lambda b: (b, OFF_VA // ATT_KV)),
        pl.BlockSpec((HEAD_DIM, 1), lambda b: (0, 0)),
        pl.BlockSpec((HEAD_DIM, 1), lambda b: (0, 0)),
    ]
    args = [p_act, p_act, p_act, q_norm, k_norm]
    if caches is None:
        cache_spec = pl.BlockSpec((blk, DEPTH, seq, ATT_KV), lambda b: (b, 0, 0, 0))
        slab, aliases = layer, {}
    else:
        cache_spec = pl.BlockSpec((blk, 1, seq, ATT_KV), lambda b: (b, layer, 0, 0))
        slab, aliases = 0, {len(args): 1, len(args) + 1: 2}
        in_specs += [pl.BlockSpec(memory_space=pl.ANY)] * 2
        args += list(caches)
    cache_shape = jax.ShapeDtypeStruct((batch, DEPTH, seq, ATT_KV), F32)
    return pl.pallas_call(
        functools.partial(_att_ctx_kernel, slab=slab),
        grid=(n_tok // rows,),
        in_specs=in_specs,
        out_specs=[pl.BlockSpec((rows, ATT_Q), lambda b: (b, 0)), cache_spec, cache_spec],
        out_shape=[jax.ShapeDtypeStruct((n_tok, ATT_Q), BF16), cache_shape, cache_shape],
        input_output_aliases=aliases,
        compiler_params=_cparams(("parallel",)),
        name="att_ctx",
    )(*args)


def _att_lat_kernel(q_ref, k_ref, v_ref, ck_ref, cv_ref, qn_ref, kn_ref,
                    cq_ref, sq_ref, ck_t_ref, sk_t_ref, o_ref, kf_scr, vt_scr, bound_scr, *, seq):
    @pl.when(pl.program_id(1) == 0)
    def _():
        k_heads = _norm_rope_heads_t(k_ref[...].T, N_KV_HEADS, kn_ref[...], ck_t_ref[...], sk_t_ref[...])
        kf_scr[0:seq, :] = jnp.concatenate(k_heads, axis=0).T.astype(BF16)
        past_k = ck_ref[0, 0]
        kf_scr[seq:, :] = past_k.astype(BF16)
        vt_scr[:, 0:seq] = v_ref[...].T.astype(BF16)
        vt_scr[:, seq:] = cv_ref[0, 0].T.astype(BF16)
        past_t = past_k.T
        past_heads = [past_t[kv * HEAD_DIM:(kv + 1) * HEAD_DIM] for kv in range(N_KV_HEADS)]
        bound_scr[0] = _score_bound(k_heads + past_heads, qn_ref[...])

    q_heads = _norm_rope_heads_t(q_ref[...].T, N_HEADS, qn_ref[...], cq_ref[...], sq_ref[...])

    def store(o_t):
        o_ref[...] = o_t.T.astype(o_ref.dtype)

    _attend_t(q_heads, kf_scr[...], vt_scr[...], bound_scr[0], store)


def _att_lat(p_act, seq, cache_k, cache_v, layer, q_norm, k_norm, cos_t, sin_t):
    n_tok = p_act.shape[0]
    nb = seq // Q_BLOCK
    past = cache_k.shape[2]
    return pl.pallas_call(
        functools.partial(_att_lat_kernel, seq=seq),
        grid=(n_tok // seq, nb),
        in_specs=[
            pl.BlockSpec((Q_BLOCK, ATT_Q), lambda b, i: (b * nb + i, OFF_QA // ATT_Q)),
            pl.BlockSpec((seq, ATT_KV), lambda b, i: (b, OFF_KA // ATT_KV)),
            pl.BlockSpec((seq, ATT_KV), lambda b, i: (b, OFF_VA // ATT_KV)),
            pl.BlockSpec((1, 1, past, ATT_KV), lambda b, i: (b, layer, 0, 0)),
            pl.BlockSpec((1, 1, past, ATT_KV), lambda b, i: (b, layer, 0, 0)),
            pl.BlockSpec((HEAD_DIM, 1), lambda b, i: (0, 0)),
            pl.BlockSpec((HEAD_DIM, 1), lambda b, i: (0, 0)),
            pl.BlockSpec((HEAD_DIM // 2, Q_BLOCK), lambda b, i: (0, i)),
            pl.BlockSpec((HEAD_DIM // 2, Q_BLOCK), lambda b, i: (0, i)),
            pl.BlockSpec((HEAD_DIM // 2, seq), lambda b, i: (0, 0)),
            pl.BlockSpec((HEAD_DIM // 2, seq), lambda b, i: (0, 0)),
        ],
        out_specs=pl.BlockSpec((Q_BLOCK, ATT_Q), lambda b, i: (b * nb + i, 0)),
        out_shape=jax.ShapeDtypeStruct((n_tok, ATT_Q), BF16),
        scratch_shapes=[pltpu.VMEM((seq + past, ATT_KV), BF16),
                        pltpu.VMEM((ATT_KV, seq + past), BF16),
                        pltpu.SMEM((1,), F32)],
        compiler_params=_cparams(("parallel", "arbitrary")),
        name="att_lat",
    )(p_act, p_act, p_act, cache_k, cache_v, q_norm, k_norm, cos_t, sin_t, cos_t, sin_t)


RET_ROWS = 2048


def _log_sigmoid(d):
    return jnp.minimum(d, 0.0) - jnp.log1p(jnp.exp(-jnp.abs(d)))


TAB_MASK, TAB_QF, TAB_QB, TAB_KF, TAB_KB, TAB_C, N_TAB = 0, 2, 4, 6, 7, 8, 9
TAB_CF, TAB_CB = 0, 2


def _ret_tables_kernel(dec_ref, tab_ref):
    hp = pl.program_id(0)
    ii = lax.broadcasted_iota(jnp.int32, (CHUNK, CHUNK), 0)
    jj = lax.broadcasted_iota(jnp.int32, (CHUNK, CHUNK), 1)
    rel = (ii - jj).astype(F32)
    row = ii.astype(F32)
    lane = jj.astype(F32)
    lgf = [_log_sigmoid(dec_ref[pl.ds(2 * hp + t, 1), :]) for t in range(2)]
    lgb = [_log_sigmoid(dec_ref[pl.ds(RET_HEADS + 2 * hp + t, 1), :]) for t in range(2)]
    for t in range(2):
        tab_ref[0, TAB_MASK + t] = jnp.where(
            rel > 0, jnp.exp(jnp.maximum(rel, 0.0) * lgf[t]),
            jnp.where(rel < 0, jnp.exp(jnp.maximum(-rel, 0.0) * lgb[t]), 2.0))
        tab_ref[0, TAB_QF + t] = jnp.exp((row + 1.0) * lgf[t])
        tab_ref[0, TAB_QB + t] = jnp.exp((CHUNK - row) * lgb[t])
    tab_ref[0, TAB_KF] = jnp.exp((CHUNK - 1.0 - lane) * jnp.where(ii < RET_DK, lgf[0], lgf[1]))
    tab_ref[0, TAB_KB] = jnp.exp(lane * jnp.where(ii < RET_DK, lgb[0], lgb[1]))
    c_rows = jnp.where(ii == TAB_CF, lgf[0], jnp.where(ii == TAB_CF + 1, lgf[1],
                       jnp.where(ii == TAB_CB, lgb[0], lgb[1])))
    tab_ref[0, TAB_C] = jnp.exp(CHUNK * c_rows)


def _ret_tables(dec):
    return pl.pallas_call(
        _ret_tables_kernel,
        grid=(RET_HEADS // 2,),
        in_specs=[pl.BlockSpec((8, 128), lambda p: (0, 0))],
        out_specs=pl.BlockSpec((1, N_TAB, CHUNK, CHUNK), lambda p: (p, 0, 0, 0)),
        out_shape=jax.ShapeDtypeStruct((RET_HEADS // 2, N_TAB, CHUNK, CHUNK), F32),
        compiler_params=_cparams(("parallel",)),
        name="ret_tables",
    )(dec)


def _ret_kernel(q_ref, k_ref, v_ref, g_ref, tab_ref, rn_ref, *rest, seq, has_state, slab):
    if has_state:
        s0f_ref, s0b_ref, o_ref = rest
    else:
        o_ref, sf_ref, sb_ref = rest[-3:]
        if len(rest) == 3:
            for other in range(sf_ref.shape[1]):
                if other != slab:
                    sf_ref[:, other] = jnp.zeros_like(sf_ref[:, other])
                    sb_ref[:, other] = jnp.zeros_like(sb_ref[:, other])
    n_chunks = q_ref.shape[0] // CHUNK
    jj = lax.broadcasted_iota(jnp.int32, (CHUNK, CHUNK), 1)
    mask2 = jnp.concatenate([tab_ref[0, TAB_MASK], tab_ref[0, TAB_MASK + 1]], axis=0)
    qdec_f = [tab_ref[0, TAB_QF + t] for t in range(2)]
    qdec_b = [tab_ref[0, TAB_QB + t] for t in range(2)]
    cdec_f = [tab_ref[0, TAB_C, TAB_CF + t:TAB_CF + t + 1, :] for t in range(2)]
    cdec_b = [tab_ref[0, TAB_C, TAB_CB + t:TAB_CB + t + 1, :] for t in range(2)]
    kdec_f = tab_ref[0, TAB_KF]
    kdec_b = tab_ref[0, TAB_KB]

    k_t = (k_ref[...] * (RET_DK ** -0.5)).T
    first_head = jj < RET_DK

    def chunk(c):
        return slice(c * CHUNK, (c + 1) * CHUNK)

    kv_f = [[None] * n_chunks for _ in range(2)]
    kv_b = [[None] * n_chunks for _ in range(2)]
    for c in range(n_chunks):
        k_c = k_t[:, chunk(c)]
        kd = jnp.concatenate([k_c * kdec_f, k_c * kdec_b], axis=0).astype(BF16)
        kv = _dot(kd, v_ref[chunk(c), :].astype(BF16))
        for t in range(2):
            kv_f[t][c] = kv[t * RET_DK:(t + 1) * RET_DK, t * RET_DV:(t + 1) * RET_DV]
            kv_b[t][c] = kv[CHUNK + t * RET_DK:CHUNK + (t + 1) * RET_DK, t * RET_DV:(t + 1) * RET_DV]

    st_f = [[None] * n_chunks for _ in range(2)]
    st_b = [[None] * n_chunks for _ in range(2)]
    per_seq = seq // CHUNK
    for t in range(2):
        for sq in range(n_chunks // per_seq):
            own = range(sq * per_seq, (sq + 1) * per_seq)
            if has_state:
                s_f = s0f_ref[sq, 0, t]
                s_b = s0b_ref[sq, 0, t]
            else:
                s_f = jnp.zeros((RET_DK, RET_DV), F32)
                s_b = s_f
            for c in own:
                st_f[t][c] = s_f
                s_f = s_f * cdec_f[t] + kv_f[t][c]
            for c in reversed(own):
                st_b[t][c] = s_b
                s_b = s_b * cdec_b[t] + kv_b[t][c]
            if not has_state:
                sf_ref[sq, slab, t] = s_f
                sb_ref[sq, slab, t] = s_b

    for c in range(n_chunks):
        q_c = q_ref[chunk(c), :]
        qm = jnp.concatenate([jnp.where(first_head, q_c, 0.0), jnp.where(first_head, 0.0, q_c)],
                             axis=0).astype(BF16)
        att = (_dot(qm, k_t[:, chunk(c)].astype(BF16)) * mask2).astype(BF16)
        states = jnp.concatenate(
            [jnp.concatenate([st_f[t][c], st_b[t][c]], axis=1) for t in range(2)], axis=0)
        qs = _dot(qm, states.astype(BF16))
        v_c = v_ref[chunk(c), :].astype(BF16)
        for t in range(2):
            rows = slice(t * CHUNK, (t + 1) * CHUNK)
            vsl = slice(t * RET_DV, (t + 1) * RET_DV)
            o = (_dot(att[rows], v_c[:, vsl]) + qs[rows, :RET_DV] * qdec_f[t]
                 + qs[rows, RET_DV:] * qdec_b[t])
            o = (o * lax.rsqrt(jnp.mean(o * o, axis=-1, keepdims=True) + EPS)) * rn_ref[...]
            hg = 0.5 * g_ref[chunk(c), vsl]
            o_ref[chunk(c), vsl] = (o * (hg * (1.0 + jnp.tanh(hg)))).astype(o_ref.dtype)


def _retention(p_act, seq, tables, ret_norm, layer, states=None, new_states=None):
    n_tok = p_act.shape[0]
    n_seq = n_tok // seq
    has_state = states is not None
    pair_qk, pair_v = 2 * RET_DK, 2 * RET_DV
    rows = max(seq, RET_ROWS)
    blk_seqs = rows // seq
    in_specs = [
        pl.BlockSpec((rows, pair_qk), lambda p, s: (s, OFF_QR // pair_qk + p)),
        pl.BlockSpec((rows, pair_qk), lambda p, s: (s, OFF_KR // pair_qk + p)),
        pl.BlockSpec((rows, pair_v), lambda p, s: (s, OFF_VR // pair_v + p)),
        pl.BlockSpec((rows, pair_v), lambda p, s: (s, OFF_GR // pair_v + p)),
        pl.BlockSpec((1, N_TAB, CHUNK, CHUNK), lambda p, s: (p, 0, 0, 0)),
        pl.BlockSpec((1, RET_DV), lambda p, s: (0, 0)),
    ]
    args = [p_act, p_act, p_act, p_act, tables, ret_norm]
    o_spec = pl.BlockSpec((rows, pair_v), lambda p, s: (s, p))
    o_shape = jax.ShapeDtypeStruct((n_tok, RET_V), BF16)
    st_spec = pl.BlockSpec((blk_seqs, 1, 2, RET_DK, RET_DV), lambda p, s: (s, layer, p, 0, 0))
    aliases, slab = {}, 0
    if has_state:
        in_specs += [st_spec, st_spec]
        args += list(states)
        out_specs, out_shape = o_spec, o_shape
    else:
        if new_states is None:
            st_spec = pl.BlockSpec((blk_seqs, DEPTH, 2, RET_DK, RET_DV), lambda p, s: (s, 0, p, 0, 0))
            slab = layer
        else:
            in_specs += [pl.BlockSpec(memory_space=pl.ANY)] * 2
            aliases = {len(args): 1, len(args) + 1: 2}
            args += list(new_states)
        st_shape = jax.ShapeDtypeStruct((n_seq, DEPTH, RET_HEADS, RET_DK, RET_DV), F32)
        out_specs, out_shape = [o_spec, st_spec, st_spec], [o_shape, st_shape, st_shape]
    return pl.pallas_call(
        functools.partial(_ret_kernel, seq=seq, has_state=has_state, slab=slab),
        grid=(RET_HEADS // 2, n_tok // rows),
        in_specs=in_specs,
        out_specs=out_specs,
        out_shape=out_shape,
        input_output_aliases=aliases,
        compiler_params=_cparams(("parallel", "parallel")),
        name="retention_lat" if has_state else "retention_ctx",
    )(*args)


FOURIER_ROWS = 2048


def _four_kernel(u0_ref, u1_ref, wc_ref, csn_ref, o_ref):
    gd = FOURIER_GROUP_DIM
    tc, ts = [], []
    for g in range(FOURIER_GROUPS):
        u_ref = (u0_ref, u1_ref)[g // 2]
        u_g = u_ref[:, (g % 2) * gd:(g % 2 + 1) * gd].astype(BF16)
        t = _dot(u_g, wc_ref[...])
        tc.append(t[:, :gd])
        ts.append(t[:, gd:])
    tc = jnp.concatenate(tc, axis=1).astype(BF16)
    ts = jnp.concatenate(ts, axis=1).astype(BF16)
    seq = csn_ref.shape[0]
    for sq in range(o_ref.shape[0] // seq):
        rows = slice(sq * seq, (sq + 1) * seq)
        t_sq = jnp.concatenate([tc[rows], ts[rows]], axis=0)
        o_ref[rows, :] = _dot(csn_ref[...], t_sq).astype(o_ref.dtype)


def _dft_tables(n):
    k = np.arange(n, dtype=np.int64)
    ang = 2.0 * np.pi * ((k[:, None] * k[None, :]) % n).astype(np.float64) / n
    scale = 1.0 / np.sqrt(n)
    return np.cos(ang) * scale, np.sin(ang) * scale


def _fourier(p_act, seq):
    n_tok = p_act.shape[0]
    cc, sc = _dft_tables(FOURIER_GROUP_DIM)
    wc = jnp.asarray(np.concatenate([cc, -sc], axis=1), F32).astype(BF16)
    cn, sn = _dft_tables(seq)
    csn = jnp.asarray(np.concatenate([cn, sn], axis=1), F32).astype(BF16)
    const = lambda shape: pl.BlockSpec(shape, lambda s: (0, 0))
    half = FOURIER_W // 2
    rows = max(seq, FOURIER_ROWS)
    return pl.pallas_call(
        _four_kernel,
        grid=(n_tok // rows,),
        in_specs=[
            pl.BlockSpec((rows, half), lambda s: (s, OFF_UF // half)),
            pl.BlockSpec((rows, half), lambda s: (s, OFF_UF // half + 1)),
            const((FOURIER_GROUP_DIM, 2 * FOURIER_GROUP_DIM)),
            const((seq, 2 * seq)),
        ],
        out_specs=pl.BlockSpec((rows, FOURIER_W), lambda s: (s, 0)),
        out_shape=jax.ShapeDtypeStruct((n_tok, FOURIER_W), BF16),
        compiler_params=_cparams(("parallel",)),
        name="fourier",
    )(p_act, p_act, wc, csn)


MIX_TM = 1024
MERGE_STEPS = 4
MERGE_TM = MIX_TM // MERGE_STEPS
FF_TILE = 256
FF_STEPS = D_FF // FF_TILE
FF_ROWS = 256


def _mix_kernel(oa_ref, or_ref, of_ref, ga_ref, gr_ref, gf_ref, x_ref, g1_ref, sh_ref, sc_ref,
                g2_ref, n2_ref, wba_ref, wbr_ref, wbf_ref, wo_ref, wa_ref, wv_ref, cw_ref, cb_ref,
                wd_ref, o_ref, h_scr, acc_scr, wu_scr, wd_scr, *, seq):
    i = pl.program_id(0)
    j = pl.program_id(1)
    chunk = jnp.maximum(j - MERGE_STEPS, 0)
    a_cols = pl.ds(pl.multiple_of(chunk * FF_TILE, FF_TILE), FF_TILE)
    v_cols = pl.ds(pl.multiple_of((FF_STEPS + chunk) * FF_TILE, FF_TILE), FF_TILE)

    @pl.when(j < MERGE_STEPS)
    def _():
        gate = lambda ref: jnp.tanh(ref[...].astype(F32)) + 1.0
        dst = pl.ds(pl.multiple_of(j * MERGE_TM, MERGE_TM), MERGE_TM)
        merged = (gate(ga_ref) * _dot(oa_ref[...], wba_ref[...])
                  + gate(gr_ref) * _dot(or_ref[...], wbr_ref[...])
                  + gate(gf_ref) * _dot(of_ref[...], wbf_ref[...]))
        x1 = x_ref[...] + g1_ref[0] * _dot((0.5 * merged).astype(BF16), wo_ref[...])
        o_ref[dst, :] = x1
        r = lax.rsqrt(jnp.mean(x1 * x1, axis=-1, keepdims=True) + EPS)
        h = (x1 * r) * n2_ref[...] * (1.0 + sc_ref[0]) + sh_ref[0]
        h_scr[dst, :] = h.astype(BF16)
        acc_scr[dst, :] = jnp.zeros((MERGE_TM, D_MODEL), F32)

    @pl.when(jnp.logical_and(i == 0, j >= MERGE_STEPS))
    def _():
        wu_scr[:, a_cols] = wa_ref[0].astype(BF16)
        wu_scr[:, v_cols] = wv_ref[0].astype(BF16)
        wd_scr[a_cols, :] = wd_ref[0].astype(BF16)

    def mlp(last):
        blocks = [slice(b * FF_ROWS, (b + 1) * FF_ROWS) for b in range(MIX_TM // FF_ROWS)]
        wa, wv, wd = wu_scr[:, a_cols], wu_scr[:, v_cols], wd_scr[a_cols, :]
        a = jnp.concatenate([_dot(h_scr[rows, :], wa) for rows in blocks], axis=0)
        val = [_dot(h_scr[rows, :], wv) for rows in blocks]
        pos = lax.broadcasted_iota(jnp.int32, a.shape, 0) % seq
        prev = jnp.where(pos == 0, 0.0, pltpu.roll(a, 1, 0))
        nxt = jnp.where(pos == seq - 1, 0.0, pltpu.roll(a, MIX_TM - 1, 0))
        cw = cw_ref[...]
        ac = prev * cw[0:1] + a * cw[1:2] + nxt * cw[2:3] + cb_ref[...]
        for b, rows in enumerate(blocks):
            act = jax.nn.gelu(ac[rows]) * val[b]
            down = _dot(act.astype(BF16), wd)
            if last:
                o_ref[rows, :] = o_ref[rows, :] + g2_ref[0] * (acc_scr[rows, :] + down)
            else:
                acc_scr[rows, :] += down

    last_step = pl.num_programs(1) - 1

    @pl.when(jnp.logical_and(j >= MERGE_STEPS, j < last_step))
    def _():
        mlp(last=False)

    @pl.when(j == last_step)
    def _():
        mlp(last=True)


def _mix(o_att, o_ret, o_four, p_gate, x2d, seq, mod, rows_per_mod, w_att, w_ret, w_four, w_out,
         norm2, w_up, conv_w, conv_b, w_down, layer):
    n_tok = x2d.shape[0]
    mod_idx = lambda i: (i * MIX_TM) // rows_per_mod
    sub = lambda i, j: i * MERGE_STEPS + jnp.minimum(j, MERGE_STEPS - 1)
    ff = lambda j: jnp.maximum(j - MERGE_STEPS, 0)
    ff_w = lambda i, j: jnp.where(i == 0, ff(j), FF_STEPS - 1)
    br = lambda: pl.BlockSpec((MERGE_TM, 512), lambda i, j: (sub(i, j), 0))
    gate = lambda k: pl.BlockSpec((MERGE_TM, D_MODEL), lambda i, j: (sub(i, j), k))
    modv = lambda k: pl.BlockSpec((1, 1, D_MODEL), lambda i, j: (mod_idx(i), 0, k))
    const = lambda shape: pl.BlockSpec(shape, lambda i, j: (0, 0))
    return pl.pallas_call(
        functools.partial(_mix_kernel, seq=seq),
        grid=(n_tok // MIX_TM, MERGE_STEPS + FF_STEPS),
        in_specs=[
            br(), br(), br(), gate(0), gate(1), gate(2),
            pl.BlockSpec((MERGE_TM, D_MODEL), lambda i, j: (sub(i, j), 0)),
            modv(2), modv(3), modv(4), modv(5),
            const((1, D_MODEL)),
            const((512, D_MODEL)), const((512, D_MODEL)), const((512, D_MODEL)),
            const((D_MODEL, D_MODEL)),
            pl.BlockSpec((1, D_MODEL, FF_TILE), lambda i, j: (layer, 0, ff_w(i, j))),
            pl.BlockSpec((1, D_MODEL, FF_TILE), lambda i, j: (layer, 0, FF_STEPS + ff_w(i, j))),
            pl.BlockSpec((3, FF_TILE), lambda i, j: (0, ff(j))),
            pl.BlockSpec((1, FF_TILE), lambda i, j: (0, ff(j))),
            pl.BlockSpec((1, FF_TILE, D_MODEL), lambda i, j: (layer, ff_w(i, j), 0)),
        ],
        out_specs=pl.BlockSpec((MIX_TM, D_MODEL), lambda i, j: (i, 0)),
        out_shape=jax.ShapeDtypeStruct((n_tok, D_MODEL), F32),
        scratch_shapes=[pltpu.VMEM((MIX_TM, D_MODEL), BF16), pltpu.VMEM((MIX_TM, D_MODEL), F32),
                        pltpu.VMEM((D_MODEL, 2 * D_FF), BF16), pltpu.VMEM((D_FF, D_MODEL), BF16)],
        compiler_params=_cparams(("arbitrary", "arbitrary")),
        name="mix",
    )(o_att, o_ret, o_four, p_gate, p_gate, p_gate, x2d, mod, mod, mod, mod, norm2,
      w_att, w_ret, w_four, w_out, w_up, w_up, conv_w, conv_b, w_down)


def _rope_tables(n_tok):
    rows = n_tok // GRID_W
    row_id = jnp.repeat(jnp.arange(rows), GRID_W).astype(F32)
    col_id = jnp.tile(jnp.arange(GRID_W), rows).astype(F32)
    n_freq = HEAD_DIM // 4
    inv = ROPE_THETA ** (-jnp.arange(n_freq, dtype=F32) / n_freq)
    ang = jnp.concatenate([row_id[None, :] * inv[:, None], col_id[None, :] * inv[:, None]], axis=0)
    return jnp.cos(ang), jnp.sin(ang)


def kernel(x_prompt, x_sample, cache_k, cache_v, state_ret_fwd, state_ret_bwd, c, c_ctx, w_ada, b_ada, norm1, w_in, q_norm, k_norm, ret_decay_f, ret_decay_b, ret_norm, w_br_att, w_br_ret, w_br_four, w_out, norm2, w_up, conv_w, conv_b, w_down):
    batch, seq, _ = x_prompt.shape
    dec_batch, dec_seq, _ = x_sample.shape
    past = cache_k.shape[2]
    for n_seq, length in ((batch, seq), (dec_batch, dec_seq)):
        assert length % CHUNK == 0 and MIX_TM % length == 0 and (n_seq * length) % MIX_TM == 0
    assert dec_seq % Q_BLOCK == 0 and dec_seq % GRID_W == 0 and past % CHUNK == 0

    cond_all = jnp.concatenate([c_ctx[None, :], c], axis=0)
    mod_all = _ada(cond_all, w_ada, b_ada)
    cos_t, sin_t = _rope_tables(dec_seq)
    ck = cache_k.reshape(dec_batch, DEPTH, past, ATT_KV)
    cv = cache_v.reshape(dec_batch, DEPTH, past, ATT_KV)

    xp = x_prompt.reshape(batch * seq, D_MODEL)
    xs = x_sample.reshape(dec_batch * dec_seq, D_MODEL)
    new_kv = new_st = None
    for l in range(DEPTH):
        w_att, w_ret, w_four = (w_br_att[l].astype(BF16), w_br_ret[l].astype(BF16),
                                w_br_four[l].astype(BF16))
        w_o = w_out[l].astype(BF16)
        n1, n2 = norm1[l][None, :], norm2[l][None, :]
        qn, kn = q_norm[l][:, None], k_norm[l][:, None]
        rn = ret_norm[l][None, :]
        dec = _ret_tables(jnp.broadcast_to(
            jnp.concatenate([ret_decay_f[l], ret_decay_b[l]])[:, None].astype(F32), (2 * RET_HEADS, 128)))
        cw, cb = conv_w[l], conv_b[l][None, :]
        mod_ctx = mod_all[l, 0:1].reshape(1, 1, 6 * D_MODEL)
        mod_lat = mod_all[l, 1:].reshape(dec_batch, 1, 6 * D_MODEL)

        p_act, p_gate = _in_proj(xp, mod_ctx, batch * seq, n1, w_in, l)
        o_att, *new_kv = _att_ctx(p_act, seq, qn, kn, l, new_kv)
        o_ret, *new_st = _retention(p_act, seq, dec, rn, l, new_states=new_st)
        o_four = _fourier(p_act, seq)
        xp = _mix(o_att, o_ret, o_four, p_gate, xp, seq, mod_ctx, batch * seq,
                  w_att, w_ret, w_four, w_o, n2, w_up, cw, cb, w_down, l)

        p_act, p_gate = _in_proj(xs, mod_lat, dec_seq, n1, w_in, l)
        o_att = _att_lat(p_act, dec_seq, ck, cv, l, qn, kn, cos_t, sin_t)
        o_ret = _retention(p_act, dec_seq, dec, rn, l, states=(state_ret_fwd, state_ret_bwd))
        o_four = _fourier(p_act, dec_seq)
        xs = _mix(o_att, o_ret, o_four, p_gate, xs, dec_seq, mod_lat, dec_seq,
                  w_att, w_ret, w_four, w_o, n2, w_up, cw, cb, w_down, l)

    kv_shape = (batch, DEPTH, seq, N_KV_HEADS, HEAD_DIM)
    return (xp.reshape(batch, seq, D_MODEL), xs.reshape(dec_batch, dec_seq, D_MODEL),
            new_kv[0].reshape(kv_shape), new_kv[1].reshape(kv_shape), new_st[0], new_st[1])
```

```python
import functools

import numpy as np
import jax
import jax.numpy as jnp
from jax import lax
from jax.experimental import pallas as pl
from jax.experimental.pallas import tpu as pltpu

D_MODEL = 1024
DEPTH = 2
GRID_W = 64
HEAD_DIM = 64
N_HEADS = 8
N_KV_HEADS = 2
Q_PER_KV = N_HEADS // N_KV_HEADS
ATT_Q = N_HEADS * HEAD_DIM
ATT_KV = N_KV_HEADS * HEAD_DIM
RET_HEADS = 4
RET_DK = 64
RET_DV = 128
RET_V = RET_HEADS * RET_DV
FOURIER_GROUPS = 4
FOURIER_GROUP_DIM = 128
FOURIER_W = FOURIER_GROUPS * FOURIER_GROUP_DIM
D_FF = 2816
CHUNK = 128
Q_BLOCK = 256
ROPE_THETA = 10000.0
EPS = 1e-6
LOG2_E = 1.4426950408889634

F32 = jnp.float32
BF16 = jnp.bfloat16

OFF_QA, OFF_KA, OFF_VA = 0, 512, 640
OFF_QR, OFF_KR, OFF_VR, OFF_GR, OFF_UF = 768, 1024, 1280, 1792, 2304
W_IN_SPLIT = 2816
OFF_GATE = 3072
P_W = OFF_GATE + 3 * D_MODEL

VMEM_LIMIT = 56 * 1024 * 1024


def _cparams(sem):
    return pltpu.CompilerParams(dimension_semantics=sem, vmem_limit_bytes=VMEM_LIMIT)


def _dot(a, b):
    return jnp.dot(a, b, preferred_element_type=F32)


def _ada_kernel(cond_ref, w_ref, b_ref, o_ref):
    cnd = cond_ref[...]
    s = cnd * jax.nn.sigmoid(cnd)
    o_ref[0] = _dot(s.astype(BF16), w_ref[0].astype(BF16)) + b_ref[0]


def _ada(cond_all, w_ada, b_ada):
    n = cond_all.shape[0]
    tn = 1024
    return pl.pallas_call(
        _ada_kernel,
        grid=(DEPTH, 6 * D_MODEL // tn),
        in_specs=[
            pl.BlockSpec((n, D_MODEL), lambda l, j: (0, 0)),
            pl.BlockSpec((1, D_MODEL, tn), lambda l, j: (l, 0, j)),
            pl.BlockSpec((1, 1, tn), lambda l, j: (l, 0, j)),
        ],
        out_specs=pl.BlockSpec((1, n, tn), lambda l, j: (l, 0, j)),
        out_shape=jax.ShapeDtypeStruct((DEPTH, n, 6 * D_MODEL), F32),
        compiler_params=_cparams(("parallel", "parallel")),
        name="ada",
    )(cond_all, w_ada, b_ada.reshape(DEPTH, 1, 6 * D_MODEL))


IN_TN = 1024
IN_ROWS = 256
MAIN_STEPS = OFF_GATE // IN_TN


def _in_kernel(x_ref, sh_ref, sc_ref, g_ref, w_ref, om_ref, og_ref, h_scr, w_scr):
    i = pl.program_id(0)
    j = pl.program_id(1)
    cols = pl.ds(pl.multiple_of(j * IN_TN, IN_TN), IN_TN)

    @pl.when(i == 0)
    def _():
        w_scr[:, cols] = w_ref[0].astype(BF16)

    @pl.when(j == 0)
    def _():
        for b in range(x_ref.shape[0] // IN_ROWS):
            rows = slice(b * IN_ROWS, (b + 1) * IN_ROWS)
            x = x_ref[rows, :]
            r = lax.rsqrt(jnp.mean(x * x, axis=-1, keepdims=True) + EPS)
            h = ((x * r) * g_ref[...] * (1.0 + sc_ref[0]) + sh_ref[0]).astype(BF16)
            h_scr[rows, :] = h
            om_ref[rows, :] = _dot(h, w_scr[:, 0:IN_TN])

    @pl.when(jnp.logical_and(j > 0, j < MAIN_STEPS))
    def _():
        om_ref[...] = _dot(h_scr[...], w_scr[:, cols])

    @pl.when(j >= MAIN_STEPS)
    def _():
        og_ref[...] = (0.5 * _dot(h_scr[...], w_scr[:, cols])).astype(og_ref.dtype)


def _in_proj(x2d, mod, rows_per_mod, norm1, w_in, layer):
    n_tok = x2d.shape[0]
    tm, tn = 1024, IN_TN
    n_col = P_W // tn
    mod_idx = lambda i: (i * tm) // rows_per_mod

    def w_col(i, j):
        jj = jnp.where(i == 0, j, n_col - 1)
        col = jnp.where(jj < MAIN_STEPS, jj * tn, W_IN_SPLIT + (jj - MAIN_STEPS) * tn)
        return pl.multiple_of(col, 128)

    return pl.pallas_call(
        _in_kernel,
        grid=(n_tok // tm, n_col),
        in_specs=[
            pl.BlockSpec((tm, D_MODEL), lambda i, j: (i, 0)),
            pl.BlockSpec((1, 1, D_MODEL), lambda i, j: (mod_idx(i), 0, 0)),
            pl.BlockSpec((1, 1, D_MODEL), lambda i, j: (mod_idx(i), 0, 1)),
            pl.BlockSpec((1, D_MODEL), lambda i, j: (0, 0)),
            pl.BlockSpec((pl.Element(1), pl.Element(D_MODEL), pl.Element(tn)),
                         lambda i, j: (layer, 0, w_col(i, j))),
        ],
        out_specs=[
            pl.BlockSpec((tm, tn), lambda i, j: (i, jnp.minimum(j, MAIN_STEPS - 1))),
            pl.BlockSpec((tm, tn), lambda i, j: (i, jnp.maximum(j - MAIN_STEPS, 0))),
        ],
        out_shape=[jax.ShapeDtypeStruct((n_tok, OFF_GATE), F32),
                   jax.ShapeDtypeStruct((n_tok, P_W - OFF_GATE), BF16)],
        scratch_shapes=[pltpu.VMEM((tm, D_MODEL), BF16), pltpu.VMEM((D_MODEL, P_W), BF16)],
        compiler_params=_cparams(("arbitrary", "arbitrary")),
        name="in_proj",
    )(x2d, mod, mod, norm1, w_in)


def _norm_rope_heads_t(x_t, n_heads, g_col, cos_t=None, sin_t=None):
    quarter = HEAD_DIM // 4
    outs = []
    for h in range(n_heads):
        x = x_t[h * HEAD_DIM:(h + 1) * HEAD_DIM, :]
        r = lax.rsqrt(jnp.mean(x * x, axis=0, keepdims=True) + EPS)
        y = (x * r) * g_col
        if cos_t is not None:
            pieces = []
            for a in range(2):
                c = cos_t[a * quarter:(a + 1) * quarter]
                s = sin_t[a * quarter:(a + 1) * quarter]
                x1 = y[2 * a * quarter:(2 * a + 1) * quarter]
                x2 = y[(2 * a + 1) * quarter:(2 * a + 2) * quarter]
                pieces += [x1 * c - x2 * s, x2 * c + x1 * s]
            y = jnp.concatenate(pieces, axis=0)
        outs.append(y)
    return outs


SHIFT_LIMIT = 40.0


def _score_bound(k_heads_t, q_gain):
    k_sq = [jnp.max(jnp.sum(k * k, axis=0, keepdims=True)) for k in k_heads_t]
    k_norm = jnp.sqrt(functools.reduce(jnp.maximum, k_sq))
    q_norm = HEAD_DIM ** 0.5 * jnp.max(jnp.abs(q_gain))
    return 1.01 * (HEAD_DIM ** -0.5 * LOG2_E) * q_norm * k_norm


def _attend_t(q_heads_t, k_bf, v_t_bf, score_bound, store):
    tq = q_heads_t[0].shape[1]
    cols = Q_PER_KV * tq
    zeros = jnp.zeros((HEAD_DIM, cols), F32)
    q_kv = [jnp.concatenate(q_heads_t[kv * Q_PER_KV:(kv + 1) * Q_PER_KV], axis=1)
            * (HEAD_DIM ** -0.5 * LOG2_E) for kv in range(N_KV_HEADS)]
    rhs = jnp.concatenate([jnp.concatenate([q_kv[0], zeros], axis=1),
                           jnp.concatenate([zeros, q_kv[1]], axis=1)], axis=0).astype(BF16)
    def finish(shift):
        s_all = _dot(k_bf, rhs)
        out_rows = []
        for kv in range(N_KV_HEADS):
            s_t = s_all[:, kv * cols:(kv + 1) * cols]
            m = jnp.max(s_t, axis=0, keepdims=True) if shift is None else shift
            e = jnp.exp2(s_t - m)
            inv = 1.0 / jnp.sum(e, axis=0, keepdims=True)
            o_t = _dot(v_t_bf[kv * HEAD_DIM:(kv + 1) * HEAD_DIM, :], e.astype(BF16)) * inv
            out_rows += [o_t[:, g * tq:(g + 1) * tq] for g in range(Q_PER_KV)]
        store(jnp.concatenate(out_rows, axis=0))

    if score_bound is None:
        finish(None)
        return

    @pl.when(score_bound <= SHIFT_LIMIT)
    def _():
        finish(score_bound)

    @pl.when(jnp.logical_not(score_bound <= SHIFT_LIMIT))
    def _():
        finish(None)


def _att_ctx_kernel(q_ref, k_ref, v_ref, qn_ref, kn_ref, *rest, slab):
    o_ref, ko_ref, vo_ref = rest[-3:]
    seq = ko_ref.shape[2]
    if len(rest) == 3:
        for other in range(ko_ref.shape[1]):
            if other != slab:
                ko_ref[:, other] = jnp.zeros_like(ko_ref[:, other])
                vo_ref[:, other] = jnp.zeros_like(vo_ref[:, other])
    for sq in range(ko_ref.shape[0]):
        rows = slice(sq * seq, (sq + 1) * seq)
        k_heads = _norm_rope_heads_t(k_ref[rows, :].T, N_KV_HEADS, kn_ref[...])
        k_n = jnp.concatenate(k_heads, axis=0).T
        ko_ref[sq, slab] = k_n
        v = v_ref[rows, :]
        vo_ref[sq, slab] = v
        q_heads = _norm_rope_heads_t(q_ref[rows, :].T, N_HEADS, qn_ref[...])

        def store(o_t, rows=rows):
            o_ref[rows, :] = o_t.T.astype(o_ref.dtype)

        _attend_t(q_heads, k_n.astype(BF16), v.T.astype(BF16), None, store)


ATT_CTX_ROWS = 1024


def _att_ctx(p_act, seq, q_norm, k_norm, layer, caches=None):
    n_tok = p_act.shape[0]
    batch = n_tok // seq
    rows = max(seq, ATT_CTX_ROWS)
    blk = rows // seq
    in_specs = [
        pl.BlockSpec((rows, ATT_Q), lambda b: (b, OFF_QA // ATT_Q)),
        pl.BlockSpec((rows, ATT_KV), lambda b: (b, OFF_KA // ATT_KV)),
        pl.BlockSpec((rows, ATT_KV), lambda b: (b, OFF_VA // ATT_KV)),
        pl.BlockSpec((HEAD_DIM, 1), lambda b: (0, 0)),
        pl.BlockSpec((HEAD_DIM, 1), lambda b: (0, 0)),
    ]
    args = [p_act, p_act, p_act, q_norm, k_norm]
    if caches is None:
        cache_spec = pl.BlockSpec((blk, DEPTH, seq, ATT_KV), lambda b: (b, 0, 0, 0))
        slab, aliases = layer, {}
    else:
        cache_spec = pl.BlockSpec((blk, 1, seq, ATT_KV), lambda b: (b, layer, 0, 0))
        slab, aliases = 0, {len(args): 1, len(args) + 1: 2}
        in_specs += [pl.BlockSpec(memory_space=pl.ANY)] * 2
        args += list(caches)
    cache_shape = jax.ShapeDtypeStruct((batch, DEPTH, seq, ATT_KV), F32)
    return pl.pallas_call(
        functools.partial(_att_ctx_kernel, slab=slab),
        grid=(n_tok // rows,),
        in_specs=in_specs,
        out_specs=[pl.BlockSpec((rows, ATT_Q), lambda b: (b, 0)), cache_spec, cache_spec],
        out_shape=[jax.ShapeDtypeStruct((n_tok, ATT_Q), BF16), cache_shape, cache_shape],
        input_output_aliases=aliases,
        compiler_params=_cparams(("parallel",)),
        name="att_ctx",
    )(*args)


def _att_lat_kernel(q_ref, k_ref, v_ref, ck_ref, cv_ref, qn_ref, kn_ref,
                    cq_ref, sq_ref, ck_t_ref, sk_t_ref, o_ref, kf_scr, vt_scr, bound_scr, *, seq):
    @pl.when(pl.program_id(1) == 0)
    def _():
        k_heads = _norm_rope_heads_t(k_ref[...].T, N_KV_HEADS, kn_ref[...], ck_t_ref[...], sk_t_ref[...])
        kf_scr[0:seq, :] = jnp.concatenate(k_heads, axis=0).T.astype(BF16)
        past_k = ck_ref[0, 0]
        kf_scr[seq:, :] = past_k.astype(BF16)
        vt_scr[:, 0:seq] = v_ref[...].T.astype(BF16)
        vt_scr[:, seq:] = cv_ref[0, 0].T.astype(BF16)
        past_t = past_k.T
        past_heads = [past_t[kv * HEAD_DIM:(kv + 1) * HEAD_DIM] for kv in range(N_KV_HEADS)]
        bound_scr[0] = _score_bound(k_heads + past_heads, qn_ref[...])

    q_heads = _norm_rope_heads_t(q_ref[...].T, N_HEADS, qn_ref[...], cq_ref[...], sq_ref[...])

    def store(o_t):
        o_ref[...] = o_t.T.astype(o_ref.dtype)

    _attend_t(q_heads, kf_scr[...], vt_scr[...], bound_scr[0], store)


def _att_lat(p_act, seq, cache_k, cache_v, layer, q_norm, k_norm, cos_t, sin_t):
    n_tok = p_act.shape[0]
    nb = seq // Q_BLOCK
    past = cache_k.shape[2]
    return pl.pallas_call(
        functools.partial(_att_lat_kernel, seq=seq),
        grid=(n_tok // seq, nb),
        in_specs=[
            pl.BlockSpec((Q_BLOCK, ATT_Q), lambda b, i: (b * nb + i, OFF_QA // ATT_Q)),
            pl.BlockSpec((seq, ATT_KV), lambda b, i: (b, OFF_KA // ATT_KV)),
            pl.BlockSpec((seq, ATT_KV), lambda b, i: (b, OFF_VA // ATT_KV)),
            pl.BlockSpec((1, 1, past, ATT_KV), lambda b, i: (b, layer, 0, 0)),
            pl.BlockSpec((1, 1, past, ATT_KV), lambda b, i: (b, layer, 0, 0)),
            pl.BlockSpec((HEAD_DIM, 1), lambda b, i: (0, 0)),
            pl.BlockSpec((HEAD_DIM, 1), lambda b, i: (0, 0)),
            pl.BlockSpec((HEAD_DIM // 2, Q_BLOCK), lambda b, i: (0, i)),
            pl.BlockSpec((HEAD_DIM // 2, Q_BLOCK), lambda b, i: (0, i)),
            pl.BlockSpec((HEAD_DIM // 2, seq), lambda b, i: (0, 0)),
            pl.BlockSpec((HEAD_DIM // 2, seq), lambda b, i: (0, 0)),
        ],
        out_specs=pl.BlockSpec((Q_BLOCK, ATT_Q), lambda b, i: (b * nb + i, 0)),
        out_shape=jax.ShapeDtypeStruct((n_tok, ATT_Q), BF16),
        scratch_shapes=[pltpu.VMEM((seq + past, ATT_KV), BF16),
                        pltpu.VMEM((ATT_KV, seq + past), BF16),
                        pltpu.SMEM((1,), F32)],
        compiler_params=_cparams(("parallel", "arbitrary")),
        name="att_lat",
    )(p_act, p_act, p_act, cache_k, cache_v, q_norm, k_norm, cos_t, sin_t, cos_t, sin_t)


RET_ROWS = 2048


def _log_sigmoid(d):
    return jnp.minimum(d, 0.0) - jnp.log1p(jnp.exp(-jnp.abs(d)))


TAB_MASK, TAB_QF, TAB_QB, TAB_KF, TAB_KB, TAB_C, N_TAB = 0, 2, 4, 6, 7, 8, 9
TAB_CF, TAB_CB = 0, 2


def _ret_tables_kernel(dec_ref, tab_ref):
    hp = pl.program_id(0)
    ii = lax.broadcasted_iota(jnp.int32, (CHUNK, CHUNK), 0)
    jj = lax.broadcasted_iota(jnp.int32, (CHUNK, CHUNK), 1)
    rel = (ii - jj).astype(F32)
    row = ii.astype(F32)
    lane = jj.astype(F32)
    lgf = [_log_sigmoid(dec_ref[pl.ds(2 * hp + t, 1), :]) for t in range(2)]
    lgb = [_log_sigmoid(dec_ref[pl.ds(RET_HEADS + 2 * hp + t, 1), :]) for t in range(2)]
    for t in range(2):
        tab_ref[0, TAB_MASK + t] = jnp.where(
            rel > 0, jnp.exp(jnp.maximum(rel, 0.0) * lgf[t]),
            jnp.where(rel < 0, jnp.exp(jnp.maximum(-rel, 0.0) * lgb[t]), 2.0))
        tab_ref[0, TAB_QF + t] = jnp.exp((row + 1.0) * lgf[t])
        tab_ref[0, TAB_QB + t] = jnp.exp((CHUNK - row) * lgb[t])
    tab_ref[0, TAB_KF] = jnp.exp((CHUNK - 1.0 - lane) * jnp.where(ii < RET_DK, lgf[0], lgf[1]))
    tab_ref[0, TAB_KB] = jnp.exp(lane * jnp.where(ii < RET_DK, lgb[0], lgb[1]))
    c_rows = jnp.where(ii == TAB_CF, lgf[0], jnp.where(ii == TAB_CF + 1, lgf[1],
                       jnp.where(ii == TAB_CB, lgb[0], lgb[1])))
    tab_ref[0, TAB_C] = jnp.exp(CHUNK * c_rows)


def _ret_tables(dec):
    return pl.pallas_call(
        _ret_tables_kernel,
        grid=(RET_HEADS // 2,),
        in_specs=[pl.BlockSpec((8, 128), lambda p: (0, 0))],
        out_specs=pl.BlockSpec((1, N_TAB, CHUNK, CHUNK), lambda p: (p, 0, 0, 0)),
        out_shape=jax.ShapeDtypeStruct((RET_HEADS // 2, N_TAB, CHUNK, CHUNK), F32),
        compiler_params=_cparams(("parallel",)),
        name="ret_tables",
    )(dec)


def _ret_kernel(q_ref, k_ref, v_ref, g_ref, tab_ref, rn_ref, *rest, seq, has_state, slab):
    if has_state:
        s0f_ref, s0b_ref, o_ref = rest
    else:
        o_ref, sf_ref, sb_ref = rest[-3:]
        if len(rest) == 3:
            for other in range(sf_ref.shape[1]):
                if other != slab:
                    sf_ref[:, other] = jnp.zeros_like(sf_ref[:, other])
                    sb_ref[:, other] = jnp.zeros_like(sb_ref[:, other])
    n_chunks = q_ref.shape[0] // CHUNK
    jj = lax.broadcasted_iota(jnp.int32, (CHUNK, CHUNK), 1)
    mask2 = jnp.concatenate([tab_ref[0, TAB_MASK], tab_ref[0, TAB_MASK + 1]], axis=0)
    qdec_f = [tab_ref[0, TAB_QF + t] for t in range(2)]
    qdec_b = [tab_ref[0, TAB_QB + t] for t in range(2)]
    cdec_f = [tab_ref[0, TAB_C, TAB_CF + t:TAB_CF + t + 1, :] for t in range(2)]
    cdec_b = [tab_ref[0, TAB_C, TAB_CB + t:TAB_CB + t + 1, :] for t in range(2)]
    kdec_f = tab_ref[0, TAB_KF]
    kdec_b = tab_ref[0, TAB_KB]

    k_t = (k_ref[...] * (RET_DK ** -0.5)).T
    first_head = jj < RET_DK

    def chunk(c):
        return slice(c * CHUNK, (c + 1) * CHUNK)

    kv_f = [[None] * n_chunks for _ in range(2)]
    kv_b = [[None] * n_chunks for _ in range(2)]
    for c in range(n_chunks):
        k_c = k_t[:, chunk(c)]
        kd = jnp.concatenate([k_c * kdec_f, k_c * kdec_b], axis=0).astype(BF16)
        kv = _dot(kd, v_ref[chunk(c), :].astype(BF16))
        for t in range(2):
            kv_f[t][c] = kv[t * RET_DK:(t + 1) * RET_DK, t * RET_DV:(t + 1) * RET_DV]
            kv_b[t][c] = kv[CHUNK + t * RET_DK:CHUNK + (t + 1) * RET_DK, t * RET_DV:(t + 1) * RET_DV]

    st_f = [[None] * n_chunks for _ in range(2)]
    st_b = [[None] * n_chunks for _ in range(2)]
    per_seq = seq // CHUNK
    for t in range(2):
        for sq in range(n_chunks // per_seq):
            own = range(sq * per_seq, (sq + 1) * per_seq)
            if has_state:
                s_f = s0f_ref[sq, 0, t]
                s_b = s0b_ref[sq, 0, t]
            else:
                s_f = jnp.zeros((RET_DK, RET_DV), F32)
                s_b = s_f
            for c in own:
                st_f[t][c] = s_f
                s_f = s_f * cdec_f[t] + kv_f[t][c]
            for c in reversed(own):
                st_b[t][c] = s_b
                s_b = s_b * cdec_b[t] + kv_b[t][c]
            if not has_state:
                sf_ref[sq, slab, t] = s_f
                sb_ref[sq, slab, t] = s_b

    for c in range(n_chunks):
        q_c = q_ref[chunk(c), :]
        qm = jnp.concatenate([jnp.where(first_head, q_c, 0.0), jnp.where(first_head, 0.0, q_c)],
                             axis=0).astype(BF16)
        att = (_dot(qm, k_t[:, chunk(c)].astype(BF16)) * mask2).astype(BF16)
        states = jnp.concatenate(
            [jnp.concatenate([st_f[t][c], st_b[t][c]], axis=1) for t in range(2)], axis=0)
        qs = _dot(qm, states.astype(BF16))
        v_c = v_ref[chunk(c), :].astype(BF16)
        for t in range(2):
            rows = slice(t * CHUNK, (t + 1) * CHUNK)
            vsl = slice(t * RET_DV, (t + 1) * RET_DV)
            o = (_dot(att[rows], v_c[:, vsl]) + qs[rows, :RET_DV] * qdec_f[t]
                 + qs[rows, RET_DV:] * qdec_b[t])
            o = (o * lax.rsqrt(jnp.mean(o * o, axis=-1, keepdims=True) + EPS)) * rn_ref[...]
            hg = 0.5 * g_ref[chunk(c), vsl]
            o_ref[chunk(c), vsl] = (o * (hg * (1.0 + jnp.tanh(hg)))).astype(o_ref.dtype)


def _retention(p_act, seq, tables, ret_norm, layer, states=None, new_states=None):
    n_tok = p_act.shape[0]
    n_seq = n_tok // seq
    has_state = states is not None
    pair_qk, pair_v = 2 * RET_DK, 2 * RET_DV
    rows = max(seq, RET_ROWS)
    blk_seqs = rows // seq
    in_specs = [
        pl.BlockSpec((rows, pair_qk), lambda p, s: (s, OFF_QR // pair_qk + p)),
        pl.BlockSpec((rows, pair_qk), lambda p, s: (s, OFF_KR // pair_qk + p)),
        pl.BlockSpec((rows, pair_v), lambda p, s: (s, OFF_VR // pair_v + p)),
        pl.BlockSpec((rows, pair_v), lambda p, s: (s, OFF_GR // pair_v + p)),
        pl.BlockSpec((1, N_TAB, CHUNK, CHUNK), lambda p, s: (p, 0, 0, 0)),
        pl.BlockSpec((1, RET_DV), lambda p, s: (0, 0)),
    ]
    args = [p_act, p_act, p_act, p_act, tables, ret_norm]
    o_spec = pl.BlockSpec((rows, pair_v), lambda p, s: (s, p))
    o_shape = jax.ShapeDtypeStruct((n_tok, RET_V), BF16)
    st_spec = pl.BlockSpec((blk_seqs, 1, 2, RET_DK, RET_DV), lambda p, s: (s, layer, p, 0, 0))
    aliases, slab = {}, 0
    if has_state:
        in_specs += [st_spec, st_spec]
        args += list(states)
        out_specs, out_shape = o_spec, o_shape
    else:
        if new_states is None:
            st_spec = pl.BlockSpec((blk_seqs, DEPTH, 2, RET_DK, RET_DV), lambda p, s: (s, 0, p, 0, 0))
            slab = layer
        else:
            in_specs += [pl.BlockSpec(memory_space=pl.ANY)] * 2
            aliases = {len(args): 1, len(args) + 1: 2}
            args += list(new_states)
        st_shape = jax.ShapeDtypeStruct((n_seq, DEPTH, RET_HEADS, RET_DK, RET_DV), F32)
        out_specs, out_shape = [o_spec, st_spec, st_spec], [o_shape, st_shape, st_shape]
    return pl.pallas_call(
        functools.partial(_ret_kernel, seq=seq, has_state=has_state, slab=slab),
        grid=(RET_HEADS // 2, n_tok // rows),
        in_specs=in_specs,
        out_specs=out_specs,
        out_shape=out_shape,
        input_output_aliases=aliases,
        compiler_params=_cparams(("parallel", "parallel")),
        name="retention_lat" if has_state else "retention_ctx",
    )(*args)


FOURIER_ROWS = 2048


def _four_kernel(u0_ref, u1_ref, wc_ref, csn_ref, o_ref):
    gd = FOURIER_GROUP_DIM
    tc, ts = [], []
    for g in range(FOURIER_GROUPS):
        u_ref = (u0_ref, u1_ref)[g // 2]
        u_g = u_ref[:, (g % 2) * gd:(g % 2 + 1) * gd].astype(BF16)
        t = _dot(u_g, wc_ref[...])
        tc.append(t[:, :gd])
        ts.append(t[:, gd:])
    tc = jnp.concatenate(tc, axis=1).astype(BF16)
    ts = jnp.concatenate(ts, axis=1).astype(BF16)
    seq = csn_ref.shape[0]
    for sq in range(o_ref.shape[0] // seq):
        rows = slice(sq * seq, (sq + 1) * seq)
        t_sq = jnp.concatenate([tc[rows], ts[rows]], axis=0)
        o_ref[rows, :] = _dot(csn_ref[...], t_sq).astype(o_ref.dtype)


def _dft_tables(n):
    k = np.arange(n, dtype=np.int64)
    ang = 2.0 * np.pi * ((k[:, None] * k[None, :]) % n).astype(np.float64) / n
    scale = 1.0 / np.sqrt(n)
    return np.cos(ang) * scale, np.sin(ang) * scale


def _fourier(p_act, seq):
    n_tok = p_act.shape[0]
    cc, sc = _dft_tables(FOURIER_GROUP_DIM)
    wc = jnp.asarray(np.concatenate([cc, -sc], axis=1), F32).astype(BF16)
    cn, sn = _dft_tables(seq)
    csn = jnp.asarray(np.concatenate([cn, sn], axis=1), F32).astype(BF16)
    const = lambda shape: pl.BlockSpec(shape, lambda s: (0, 0))
    half = FOURIER_W // 2
    rows = max(seq, FOURIER_ROWS)
    return pl.pallas_call(
        _four_kernel,
        grid=(n_tok // rows,),
        in_specs=[
            pl.BlockSpec((rows, half), lambda s: (s, OFF_UF // half)),
            pl.BlockSpec((rows, half), lambda s: (s, OFF_UF // half + 1)),
            const((FOURIER_GROUP_DIM, 2 * FOURIER_GROUP_DIM)),
            const((seq, 2 * seq)),
        ],
        out_specs=pl.BlockSpec((rows, FOURIER_W), lambda s: (s, 0)),
        out_shape=jax.ShapeDtypeStruct((n_tok, FOURIER_W), BF16),
        compiler_params=_cparams(("parallel",)),
        name="fourier",
    )(p_act, p_act, wc, csn)


MIX_TM = 1024
MERGE_STEPS = 4
MERGE_TM = MIX_TM // MERGE_STEPS
FF_TILE = 256
FF_STEPS = D_FF // FF_TILE
FF_ROWS = 256


def _mix_kernel(oa_ref, or_ref, of_ref, ga_ref, gr_ref, gf_ref, x_ref, g1_ref, sh_ref, sc_ref,
                g2_ref, n2_ref, wba_ref, wbr_ref, wbf_ref, wo_ref, wa_ref, wv_ref, cw_ref, cb_ref,
                wd_ref, o_ref, h_scr, acc_scr, wu_scr, wd_scr, *, seq):
    i = pl.program_id(0)
    j = pl.program_id(1)
    chunk = jnp.maximum(j - MERGE_STEPS, 0)
    a_cols = pl.ds(pl.multiple_of(chunk * FF_TILE, FF_TILE), FF_TILE)
    v_cols = pl.ds(pl.multiple_of((FF_STEPS + chunk) * FF_TILE, FF_TILE), FF_TILE)

    @pl.when(j < MERGE_STEPS)
    def _():
        gate = lambda ref: jnp.tanh(ref[...].astype(F32)) + 1.0
        dst = pl.ds(pl.multiple_of(j * MERGE_TM, MERGE_TM), MERGE_TM)
        merged = (gate(ga_ref) * _dot(oa_ref[...], wba_ref[...])
                  + gate(gr_ref) * _dot(or_ref[...], wbr_ref[...])
                  + gate(gf_ref) * _dot(of_ref[...], wbf_ref[...]))
        x1 = x_ref[...] + g1_ref[0] * _dot((0.5 * merged).astype(BF16), wo_ref[...])
        o_ref[dst, :] = x1
        r = lax.rsqrt(jnp.mean(x1 * x1, axis=-1, keepdims=True) + EPS)
        h = (x1 * r) * n2_ref[...] * (1.0 + sc_ref[0]) + sh_ref[0]
        h_scr[dst, :] = h.astype(BF16)
        acc_scr[dst, :] = jnp.zeros((MERGE_TM, D_MODEL), F32)

    @pl.when(jnp.logical_and(i == 0, j >= MERGE_STEPS))
    def _():
        wu_scr[:, a_cols] = wa_ref[0].astype(BF16)
        wu_scr[:, v_cols] = wv_ref[0].astype(BF16)
        wd_scr[a_cols, :] = wd_ref[0].astype(BF16)

    def mlp(last):
        blocks = [slice(b * FF_ROWS, (b + 1) * FF_ROWS) for b in range(MIX_TM // FF_ROWS)]
        wa, wv, wd = wu_scr[:, a_cols], wu_scr[:, v_cols], wd_scr[a_cols, :]
        a = jnp.concatenate([_dot(h_scr[rows, :], wa) for rows in blocks], axis=0)
        val = [_dot(h_scr[rows, :], wv) for rows in blocks]
        pos = lax.broadcasted_iota(jnp.int32, a.shape, 0) % seq
        prev = jnp.where(pos == 0, 0.0, pltpu.roll(a, 1, 0))
        nxt = jnp.where(pos == seq - 1, 0.0, pltpu.roll(a, MIX_TM - 1, 0))
        cw = cw_ref[...]
        ac = prev * cw[0:1] + a * cw[1:2] + nxt * cw[2:3] + cb_ref[...]
        for b, rows in enumerate(blocks):
            act = jax.nn.gelu(ac[rows]) * val[b]
            down = _dot(act.astype(BF16), wd)
            if last:
                o_ref[rows, :] = o_ref[rows, :] + g2_ref[0] * (acc_scr[rows, :] + down)
            else:
                acc_scr[rows, :] += down

    last_step = pl.num_programs(1) - 1

    @pl.when(jnp.logical_and(j >= MERGE_STEPS, j < last_step))
    def _():
        mlp(last=False)

    @pl.when(j == last_step)
    def _():
        mlp(last=True)


def _mix(o_att, o_ret, o_four, p_gate, x2d, seq, mod, rows_per_mod, w_att, w_ret, w_four, w_out,
         norm2, w_up, conv_w, conv_b, w_down, layer):
    n_tok = x2d.shape[0]
    mod_idx = lambda i: (i * MIX_TM) // rows_per_mod
    sub = lambda i, j: i * MERGE_STEPS + jnp.minimum(j, MERGE_STEPS - 1)
    ff = lambda j: jnp.maximum(j - MERGE_STEPS, 0)
    ff_w = lambda i, j: jnp.where(i == 0, ff(j), FF_STEPS - 1)
    br = lambda: pl.BlockSpec((MERGE_TM, 512), lambda i, j: (sub(i, j), 0))
    gate = lambda k: pl.BlockSpec((MERGE_TM, D_MODEL), lambda i, j: (sub(i, j), k))
    modv = lambda k: pl.BlockSpec((1, 1, D_MODEL), lambda i, j: (mod_idx(i), 0, k))
    const = lambda shape: pl.BlockSpec(shape, lambda i, j: (0, 0))
    return pl.pallas_call(
        functools.partial(_mix_kernel, seq=seq),
        grid=(n_tok // MIX_TM, MERGE_STEPS + FF_STEPS),
        in_specs=[
            br(), br(), br(), gate(0), gate(1), gate(2),
            pl.BlockSpec((MERGE_TM, D_MODEL), lambda i, j: (sub(i, j), 0)),
            modv(2), modv(3), modv(4), modv(5),
            const((1, D_MODEL)),
            const((512, D_MODEL)), const((512, D_MODEL)), const((512, D_MODEL)),
            const((D_MODEL, D_MODEL)),
            pl.BlockSpec((1, D_MODEL, FF_TILE), lambda i, j: (layer, 0, ff_w(i, j))),
            pl.BlockSpec((1, D_MODEL, FF_TILE), lambda i, j: (layer, 0, FF_STEPS + ff_w(i, j))),
            pl.BlockSpec((3, FF_TILE), lambda i, j: (0, ff(j))),
            pl.BlockSpec((1, FF_TILE), lambda i, j: (0, ff(j))),
            pl.BlockSpec((1, FF_TILE, D_MODEL), lambda i, j: (layer, ff_w(i, j), 0)),
        ],
        out_specs=pl.BlockSpec((MIX_TM, D_MODEL), lambda i, j: (i, 0)),
        out_shape=jax.ShapeDtypeStruct((n_tok, D_MODEL), F32),
        scratch_shapes=[pltpu.VMEM((MIX_TM, D_MODEL), BF16), pltpu.VMEM((MIX_TM, D_MODEL), F32),
                        pltpu.VMEM((D_MODEL, 2 * D_FF), BF16), pltpu.VMEM((D_FF, D_MODEL), BF16)],
        compiler_params=_cparams(("arbitrary", "arbitrary")),
        name="mix",
    )(o_att, o_ret, o_four, p_gate, p_gate, p_gate, x2d, mod, mod, mod, mod, norm2,
      w_att, w_ret, w_four, w_out, w_up, w_up, conv_w, conv_b, w_down)


def _rope_tables(n_tok):
    rows = n_tok // GRID_W
    row_id = jnp.repeat(jnp.arange(rows), GRID_W).astype(F32)
    col_id = jnp.tile(jnp.arange(GRID_W), rows).astype(F32)
    n_freq = HEAD_DIM // 4
    inv = ROPE_THETA ** (-jnp.arange(n_freq, dtype=F32) / n_freq)
    ang = jnp.concatenate([row_id[None, :] * inv[:, None], col_id[None, :] * inv[:, None]], axis=0)
    return jnp.cos(ang), jnp.sin(ang)


def kernel(x_prompt, x_sample, cache_k, cache_v, state_ret_fwd, state_ret_bwd, c, c_ctx, w_ada, b_ada, norm1, w_in, q_norm, k_norm, ret_decay_f, ret_decay_b, ret_norm, w_br_att, w_br_ret, w_br_four, w_out, norm2, w_up, conv_w, conv_b, w_down):
    batch, seq, _ = x_prompt.shape
    dec_batch, dec_seq, _ = x_sample.shape
    past = cache_k.shape[2]
    for n_seq, length in ((batch, seq), (dec_batch, dec_seq)):
        assert length % CHUNK == 0 and MIX_TM % length == 0 and (n_seq * length) % MIX_TM == 0
    assert dec_seq % Q_BLOCK == 0 and dec_seq % GRID_W == 0 and past % CHUNK == 0

    cond_all = jnp.concatenate([c_ctx[None, :], c], axis=0)
    mod_all = _ada(cond_all, w_ada, b_ada)
    cos_t, sin_t = _rope_tables(dec_seq)
    ck = cache_k.reshape(dec_batch, DEPTH, past, ATT_KV)
    cv = cache_v.reshape(dec_batch, DEPTH, past, ATT_KV)

    xp = x_prompt.reshape(batch * seq, D_MODEL)
    xs = x_sample.reshape(dec_batch * dec_seq, D_MODEL)
    new_kv = new_st = None
    for l in range(DEPTH):
        w_att, w_ret, w_four = (w_br_att[l].astype(BF16), w_br_ret[l].astype(BF16),
                                w_br_four[l].astype(BF16))
        w_o = w_out[l].astype(BF16)
        n1, n2 = norm1[l][None, :], norm2[l][None, :]
        qn, kn = q_norm[l][:, None], k_norm[l][:, None]
        rn = ret_norm[l][None, :]
        dec = _ret_tables(jnp.broadcast_to(
            jnp.concatenate([ret_decay_f[l], ret_decay_b[l]])[:, None].astype(F32), (2 * RET_HEADS, 128)))
        cw, cb = conv_w[l], conv_b[l][None, :]
        mod_ctx = mod_all[l, 0:1].reshape(1, 1, 6 * D_MODEL)
        mod_lat = mod_all[l, 1:].reshape(dec_batch, 1, 6 * D_MODEL)

        p_act, p_gate = _in_proj(xp, mod_ctx, batch * seq, n1, w_in, l)
        o_att, *new_kv = _att_ctx(p_act, seq, qn, kn, l, new_kv)
        o_ret, *new_st = _retention(p_act, seq, dec, rn, l, new_states=new_st)
        o_four = _fourier(p_act, seq)
        xp = _mix(o_att, o_ret, o_four, p_gate, xp, seq, mod_ctx, batch * seq,
                  w_att, w_ret, w_four, w_o, n2, w_up, cw, cb, w_down, l)

        p_act, p_gate = _in_proj(xs, mod_lat, dec_seq, n1, w_in, l)
        o_att = _att_lat(p_act, dec_seq, ck, cv, l, qn, kn, cos_t, sin_t)
        o_ret = _retention(p_act, dec_seq, dec, rn, l, states=(state_ret_fwd, state_ret_bwd))
        o_four = _fourier(p_act, dec_seq)
        xs = _mix(o_att, o_ret, o_four, p_gate, xs, dec_seq, mod_lat, dec_seq,
                  w_att, w_ret, w_four, w_o, n2, w_up, cw, cb, w_down, l)

    kv_shape = (batch, DEPTH, seq, N_KV_HEADS, HEAD_DIM)
    return (xp.reshape(batch, seq, D_MODEL), xs.reshape(dec_batch, dec_seq, D_MODEL),
            new_kv[0].reshape(kv_shape), new_kv[1].reshape(kv_shape), new_st[0], new_st[1])
```

```python
import functools

import numpy as np
import jax
import jax.numpy as jnp
from jax import lax
from jax.experimental import pallas as pl
from jax.experimental.pallas import tpu as pltpu

D_MODEL = 1024
DEPTH = 2
GRID_W = 64
HEAD_DIM = 64
N_HEADS = 8
N_KV_HEADS = 2
Q_PER_KV = N_HEADS // N_KV_HEADS
ATT_Q = N_HEADS * HEAD_DIM
ATT_KV = N_KV_HEADS * HEAD_DIM
RET_HEADS = 4
RET_DK = 64
RET_DV = 128
RET_V = RET_HEADS * RET_DV
FOURIER_GROUPS = 4
FOURIER_GROUP_DIM = 128
FOURIER_W = FOURIER_GROUPS * FOURIER_GROUP_DIM
D_FF = 2816
CHUNK = 128
Q_BLOCK = 256
ROPE_THETA = 10000.0
EPS = 1e-6
LOG2_E = 1.4426950408889634

F32 = jnp.float32
BF16 = jnp.bfloat16

OFF_QA, OFF_KA, OFF_VA = 0, 512, 640
OFF_QR, OFF_KR, OFF_VR, OFF_GR, OFF_UF = 768, 1024, 1280, 1792, 2304
W_IN_SPLIT = 2816
OFF_GATE = 3072
P_W = OFF_GATE + 3 * D_MODEL

VMEM_LIMIT = 56 * 1024 * 1024


def _cparams(sem):
    return pltpu.CompilerParams(dimension_semantics=sem, vmem_limit_bytes=VMEM_LIMIT)


def _dot(a, b):
    return jnp.dot(a, b, preferred_element_type=F32)


def _ada_kernel(cond_ref, w_ref, b_ref, o_ref):
    cnd = cond_ref[...]
    s = cnd * jax.nn.sigmoid(cnd)
    o_ref[0] = _dot(s.astype(BF16), w_ref[0].astype(BF16)) + b_ref[0]


def _ada(cond_all, w_ada, b_ada):
    n = cond_all.shape[0]
    tn = 1024
    return pl.pallas_call(
        _ada_kernel,
        grid=(DEPTH, 6 * D_MODEL // tn),
        in_specs=[
            pl.BlockSpec((n, D_MODEL), lambda l, j: (0, 0)),
            pl.BlockSpec((1, D_MODEL, tn), lambda l, j: (l, 0, j)),
            pl.BlockSpec((1, 1, tn), lambda l, j: (l, 0, j)),
        ],
        out_specs=pl.BlockSpec((1, n, tn), lambda l, j: (l, 0, j)),
        out_shape=jax.ShapeDtypeStruct((DEPTH, n, 6 * D_MODEL), F32),
        compiler_params=_cparams(("parallel", "parallel")),
        name="ada",
    )(cond_all, w_ada, b_ada.reshape(DEPTH, 1, 6 * D_MODEL))


IN_TN = 1024
IN_ROWS = 256
MAIN_STEPS = OFF_GATE // IN_TN


def _in_kernel(x_ref, sh_ref, sc_ref, g_ref, w_ref, om_ref, og_ref, h_scr, w_scr):
    i = pl.program_id(0)
    j = pl.program_id(1)
    cols = pl.ds(pl.multiple_of(j * IN_TN, IN_TN), IN_TN)

    @pl.when(i == 0)
    def _():
        w_scr[:, cols] = w_ref[0].astype(BF16)

    @pl.when(j == 0)
    def _():
        for b in range(x_ref.shape[0] // IN_ROWS):
            rows = slice(b * IN_ROWS, (b + 1) * IN_ROWS)
            x = x_ref[rows, :]
            r = lax.rsqrt(jnp.mean(x * x, axis=-1, keepdims=True) + EPS)
            h = ((x * r) * g_ref[...] * (1.0 + sc_ref[0]) + sh_ref[0]).astype(BF16)
            h_scr[rows, :] = h
            om_ref[rows, :] = _dot(h, w_scr[:, 0:IN_TN])

    @pl.when(jnp.logical_and(j > 0, j < MAIN_STEPS))
    def _():
        om_ref[...] = _dot(h_scr[...], w_scr[:, cols])

    @pl.when(j >= MAIN_STEPS)
    def _():
        og_ref[...] = (0.5 * _dot(h_scr[...], w_scr[:, cols])).astype(og_ref.dtype)


def _in_proj(x2d, mod, rows_per_mod, norm1, w_in, layer):
    n_tok = x2d.shape[0]
    tm, tn = 1024, IN_TN
    n_col = P_W // tn
    mod_idx = lambda i: (i * tm) // rows_per_mod

    def w_col(i, j):
        jj = jnp.where(i == 0, j, n_col - 1)
        col = jnp.where(jj < MAIN_STEPS, jj * tn, W_IN_SPLIT + (jj - MAIN_STEPS) * tn)
        return pl.multiple_of(col, 128)

    return pl.pallas_call(
        _in_kernel,
        grid=(n_tok // tm, n_col),
        in_specs=[
            pl.BlockSpec((tm, D_MODEL), lambda i, j: (i, 0)),
            pl.BlockSpec((1, 1, D_MODEL), lambda i, j: (mod_idx(i), 0, 0)),
            pl.BlockSpec((1, 1, D_MODEL), lambda i, j: (mod_idx(i), 0, 1)),
            pl.BlockSpec((1, D_MODEL), lambda i, j: (0, 0)),
            pl.BlockSpec((pl.Element(1), pl.Element(D_MODEL), pl.Element(tn)),
                         lambda i, j: (layer, 0, w_col(i, j))),
        ],
        out_specs=[
            pl.BlockSpec((tm, tn), lambda i, j: (i, jnp.minimum(j, MAIN_STEPS - 1))),
            pl.BlockSpec((tm, tn), lambda i, j: (i, jnp.maximum(j - MAIN_STEPS, 0))),
        ],
        out_shape=[jax.ShapeDtypeStruct((n_tok, OFF_GATE), F32),
                   jax.ShapeDtypeStruct((n_tok, P_W - OFF_GATE), BF16)],
        scratch_shapes=[pltpu.VMEM((tm, D_MODEL), BF16), pltpu.VMEM((D_MODEL, P_W), BF16)],
        compiler_params=_cparams(("arbitrary", "arbitrary")),
        name="in_proj",
    )(x2d, mod, mod, norm1, w_in)


def _norm_rope_heads_t(x_t, n_heads, g_col, cos_t=None, sin_t=None):
    quarter = HEAD_DIM // 4
    outs = []
    for h in range(n_heads):
        x = x_t[h * HEAD_DIM:(h + 1) * HEAD_DIM, :]
        r = lax.rsqrt(jnp.mean(x * x, axis=0, keepdims=True) + EPS)
        y = (x * r) * g_col
        if cos_t is not None:
            pieces = []
            for a in range(2):
                c = cos_t[a * quarter:(a + 1) * quarter]
                s = sin_t[a * quarter:(a + 1) * quarter]
                x1 = y[2 * a * quarter:(2 * a + 1) * quarter]
                x2 = y[(2 * a + 1) * quarter:(2 * a + 2) * quarter]
                pieces += [x1 * c - x2 * s, x2 * c + x1 * s]
            y = jnp.concatenate(pieces, axis=0)
        outs.append(y)
    return outs


SHIFT_LIMIT = 40.0


def _score_bound(k_heads_t, q_gain):
    k_sq = [jnp.max(jnp.sum(k * k, axis=0, keepdims=True)) for k in k_heads_t]
    k_norm = jnp.sqrt(functools.reduce(jnp.maximum, k_sq))
    q_norm = HEAD_DIM ** 0.5 * jnp.max(jnp.abs(q_gain))
    return 1.01 * (HEAD_DIM ** -0.5 * LOG2_E) * q_norm * k_norm


def _attend_t(q_heads_t, k_bf, v_t_bf, score_bound, store):
    tq = q_heads_t[0].shape[1]
    cols = Q_PER_KV * tq
    zeros = jnp.zeros((HEAD_DIM, cols), F32)
    q_kv = [jnp.concatenate(q_heads_t[kv * Q_PER_KV:(kv + 1) * Q_PER_KV], axis=1)
            * (HEAD_DIM ** -0.5 * LOG2_E) for kv in range(N_KV_HEADS)]
    rhs = jnp.concatenate([jnp.concatenate([q_kv[0], zeros], axis=1),
                           jnp.concatenate([zeros, q_kv[1]], axis=1)], axis=0).astype(BF16)
    def finish(shift):
        s_all = _dot(k_bf, rhs)
        out_rows = []
        for kv in range(N_KV_HEADS):
            s_t = s_all[:, kv * cols:(kv + 1) * cols]
            m = jnp.max(s_t, axis=0, keepdims=True) if shift is None else shift
            e = jnp.exp2(s_t - m)
            inv = 1.0 / jnp.sum(e, axis=0, keepdims=True)
            o_t = _dot(v_t_bf[kv * HEAD_DIM:(kv + 1) * HEAD_DIM, :], e.astype(BF16)) * inv
            out_rows += [o_t[:, g * tq:(g + 1) * tq] for g in range(Q_PER_KV)]
        store(jnp.concatenate(out_rows, axis=0))

    if score_bound is None:
        finish(None)
        return

    @pl.when(score_bound <= SHIFT_LIMIT)
    def _():
        finish(score_bound)

    @pl.when(jnp.logical_not(score_bound <= SHIFT_LIMIT))
    def _():
        finish(None)


def _att_ctx_kernel(q_ref, k_ref, v_ref, qn_ref, kn_ref, *rest, slab):
    o_ref, ko_ref, vo_ref = rest[-3:]
    seq = ko_ref.shape[2]
    if len(rest) == 3:
        for other in range(ko_ref.shape[1]):
            if other != slab:
                ko_ref[:, other] = jnp.zeros_like(ko_ref[:, other])
                vo_ref[:, other] = jnp.zeros_like(vo_ref[:, other])
    for sq in range(ko_ref.shape[0]):
        rows = slice(sq * seq, (sq + 1) * seq)
        k_heads = _norm_rope_heads_t(k_ref[rows, :].T, N_KV_HEADS, kn_ref[...])
        k_n = jnp.concatenate(k_heads, axis=0).T
        ko_ref[sq, slab] = k_n
        v = v_ref[rows, :]
        vo_ref[sq, slab] = v
        q_heads = _norm_rope_heads_t(q_ref[rows, :].T, N_HEADS, qn_ref[...])

        def store(o_t, rows=rows):
            o_ref[rows, :] = o_t.T.astype(o_ref.dtype)

        _attend_t(q_heads, k_n.astype(BF16), v.T.astype(BF16), None, store)


ATT_CTX_ROWS = 1024


def _att_ctx(p_act, seq, q_norm, k_norm, layer, caches=None):
    n_tok = p_act.shape[0]
    batch = n_tok // seq
    rows = max(seq, ATT_CTX_ROWS)
    blk = rows // seq
    in_specs = [
        pl.BlockSpec((rows, ATT_Q), lambda b: (b, OFF_QA // ATT_Q)),
        pl.BlockSpec((rows, ATT_KV), lambda b: (b, OFF_KA // ATT_KV)),
        pl.BlockSpec((rows, ATT_KV), lambda b: (b, OFF_VA // ATT_KV)),
        pl.BlockSpec((HEAD_DIM, 1), lambda b: (0, 0)),
        pl.BlockSpec((HEAD_DIM, 1), lambda b: (0, 0)),
    ]
    args = [p_act, p_act, p_act, q_norm, k_norm]
    if caches is None:
        cache_spec = pl.BlockSpec((blk, DEPTH, seq, ATT_KV), lambda b: (b, 0, 0, 0))
        slab, aliases = layer, {}
    else:
        cache_spec = pl.BlockSpec((blk, 1, seq, ATT_KV), lambda b: (b, layer, 0, 0))
        slab, aliases = 0, {len(args): 1, len(args) + 1: 2}
        in_specs += [pl.BlockSpec(memory_space=pl.ANY)] * 2
        args += list(caches)
    cache_shape = jax.ShapeDtypeStruct((batch, DEPTH, seq, ATT_KV), F32)
    return pl.pallas_call(
        functools.partial(_att_ctx_kernel, slab=slab),
        grid=(n_tok // rows,),
        in_specs=in_specs,
        out_specs=[pl.BlockSpec((rows, ATT_Q), lambda b: (b, 0)), cache_spec, cache_spec],
        out_shape=[jax.ShapeDtypeStruct((n_tok, ATT_Q), BF16), cache_shape, cache_shape],
        input_output_aliases=aliases,
        compiler_params=_cparams(("parallel",)),
        name="att_ctx",
    )(*args)


def _att_lat_kernel(q_ref, k_ref, v_ref, ck_ref, cv_ref, qn_ref, kn_ref,
                    cq_ref, sq_ref, ck_t_ref, sk_t_ref, o_ref, kf_scr, vt_scr, bound_scr, *, seq):
    @pl.when(pl.program_id(1) == 0)
    def _():
        k_heads = _norm_rope_heads_t(k_ref[...].T, N_KV_HEADS, kn_ref[...], ck_t_ref[...], sk_t_ref[...])
        kf_scr[0:seq, :] = jnp.concatenate(k_heads, axis=0).T.astype(BF16)
        past_k = ck_ref[0, 0]
        kf_scr[seq:, :] = past_k.astype(BF16)
        vt_scr[:, 0:seq] = v_ref[...].T.astype(BF16)
        vt_scr[:, seq:] = cv_ref[0, 0].T.astype(BF16)
        past_t = past_k.T
        past_heads = [past_t[kv * HEAD_DIM:(kv + 1) * HEAD_DIM] for kv in range(N_KV_HEADS)]
        bound_scr[0] = _score_bound(k_heads + past_heads, qn_ref[...])

    q_heads = _norm_rope_heads_t(q_ref[...].T, N_HEADS, qn_ref[...], cq_ref[...], sq_ref[...])

    def store(o_t):
        o_ref[...] = o_t.T.astype(o_ref.dtype)

    _attend_t(q_heads, kf_scr[...], vt_scr[...], bound_scr[0], store)


def _att_lat(p_act, seq, cache_k, cache_v, layer, q_norm, k_norm, cos_t, sin_t):
    n_tok = p_act.shape[0]
    nb = seq // Q_BLOCK
    past = cache_k.shape[2]
    return pl.pallas_call(
        functools.partial(_att_lat_kernel, seq=seq),
        grid=(n_tok // seq, nb),
        in_specs=[
            pl.BlockSpec((Q_BLOCK, ATT_Q), lambda b, i: (b * nb + i, OFF_QA // ATT_Q)),
            pl.BlockSpec((seq, ATT_KV), lambda b, i: (b, OFF_KA // ATT_KV)),
            pl.BlockSpec((seq, ATT_KV), lambda b, i: (b, OFF_VA // ATT_KV)),
            pl.BlockSpec((1, 1, past, ATT_KV), lambda b, i: (b, layer, 0, 0)),
            pl.BlockSpec((1, 1, past, ATT_KV), lambda b, i: (b, layer, 0, 0)),
            pl.BlockSpec((HEAD_DIM, 1), lambda b, i: (0, 0)),
            pl.BlockSpec((HEAD_DIM, 1), lambda b, i: (0, 0)),
            pl.BlockSpec((HEAD_DIM // 2, Q_BLOCK), lambda b, i: (0, i)),
            pl.BlockSpec((HEAD_DIM // 2, Q_BLOCK), lambda b, i: (0, i)),
            pl.BlockSpec((HEAD_DIM // 2, seq), lambda b, i: (0, 0)),
            pl.BlockSpec((HEAD_DIM // 2, seq), lambda b, i: (0, 0)),
        ],
        out_specs=pl.BlockSpec((Q_BLOCK, ATT_Q), lambda b, i: (b * nb + i, 0)),
        out_shape=jax.ShapeDtypeStruct((n_tok, ATT_Q), BF16),
        scratch_shapes=[pltpu.VMEM((seq + past, ATT_KV), BF16),
                        pltpu.VMEM((ATT_KV, seq + past), BF16),
                        pltpu.SMEM((1,), F32)],
        compiler_params=_cparams(("parallel", "arbitrary")),
        name="att_lat",
    )(p_act, p_act, p_act, cache_k, cache_v, q_norm, k_norm, cos_t, sin_t, cos_t, sin_t)


RET_ROWS = 4096


def _log_sigmoid(d):
    return jnp.minimum(d, 0.0) - jnp.log1p(jnp.exp(-jnp.abs(d)))


TAB_MASK, TAB_QF, TAB_QB, TAB_KF, TAB_KB, TAB_C, N_TAB = 0, 2, 4, 6, 7, 8, 9
TAB_CF, TAB_CB = 0, 2


def _ret_tables_kernel(dec_ref, tab_ref):
    hp = pl.program_id(0)
    ii = lax.broadcasted_iota(jnp.int32, (CHUNK, CHUNK), 0)
    jj = lax.broadcasted_iota(jnp.int32, (CHUNK, CHUNK), 1)
    rel = (ii - jj).astype(F32)
    row = ii.astype(F32)
    lane = jj.astype(F32)
    lgf = [_log_sigmoid(dec_ref[pl.ds(2 * hp + t, 1), :]) for t in range(2)]
    lgb = [_log_sigmoid(dec_ref[pl.ds(RET_HEADS + 2 * hp + t, 1), :]) for t in range(2)]
    for t in range(2):
        tab_ref[0, TAB_MASK + t] = jnp.where(
            rel > 0, jnp.exp(jnp.maximum(rel, 0.0) * lgf[t]),
            jnp.where(rel < 0, jnp.exp(jnp.maximum(-rel, 0.0) * lgb[t]), 2.0))
        tab_ref[0, TAB_QF + t] = jnp.exp((row + 1.0) * lgf[t])
        tab_ref[0, TAB_QB + t] = jnp.exp((CHUNK - row) * lgb[t])
    tab_ref[0, TAB_KF] = jnp.exp((CHUNK - 1.0 - lane) * jnp.where(ii < RET_DK, lgf[0], lgf[1]))
    tab_ref[0, TAB_KB] = jnp.exp(lane * jnp.where(ii < RET_DK, lgb[0], lgb[1]))
    c_rows = jnp.where(ii == TAB_CF, lgf[0], jnp.where(ii == TAB_CF + 1, lgf[1],
                       jnp.where(ii == TAB_CB, lgb[0], lgb[1])))
    tab_ref[0, TAB_C] = jnp.exp(CHUNK * c_rows)


def _ret_tables(dec):
    return pl.pallas_call(
        _ret_tables_kernel,
        grid=(RET_HEADS // 2,),
        in_specs=[pl.BlockSpec((8, 128), lambda p: (0, 0))],
        out_specs=pl.BlockSpec((1, N_TAB, CHUNK, CHUNK), lambda p: (p, 0, 0, 0)),
        out_shape=jax.ShapeDtypeStruct((RET_HEADS // 2, N_TAB, CHUNK, CHUNK), F32),
        compiler_params=_cparams(("parallel",)),
        name="ret_tables",
    )(dec)


def _ret_kernel(q_ref, k_ref, v_ref, g_ref, tab_ref, rn_ref, *rest, seq, has_state, slab):
    if has_state:
        s0f_ref, s0b_ref, o_ref = rest
    else:
        o_ref, sf_ref, sb_ref = rest[-3:]
        if len(rest) == 3:
            for other in range(sf_ref.shape[1]):
                if other != slab:
                    sf_ref[:, other] = jnp.zeros_like(sf_ref[:, other])
                    sb_ref[:, other] = jnp.zeros_like(sb_ref[:, other])
    n_chunks = q_ref.shape[0] // CHUNK
    jj = lax.broadcasted_iota(jnp.int32, (CHUNK, CHUNK), 1)
    mask2 = jnp.concatenate([tab_ref[0, TAB_MASK], tab_ref[0, TAB_MASK + 1]], axis=0)
    qdec_f = [tab_ref[0, TAB_QF + t] for t in range(2)]
    qdec_b = [tab_ref[0, TAB_QB + t] for t in range(2)]
    cdec_f = [tab_ref[0, TAB_C, TAB_CF + t:TAB_CF + t + 1, :] for t in range(2)]
    cdec_b = [tab_ref[0, TAB_C, TAB_CB + t:TAB_CB + t + 1, :] for t in range(2)]
    kdec_f = tab_ref[0, TAB_KF]
    kdec_b = tab_ref[0, TAB_KB]

    k_t = (k_ref[...] * (RET_DK ** -0.5)).T
    first_head = jj < RET_DK

    def chunk(c):
        return slice(c * CHUNK, (c + 1) * CHUNK)

    kv_f = [[None] * n_chunks for _ in range(2)]
    kv_b = [[None] * n_chunks for _ in range(2)]
    for c in range(n_chunks):
        k_c = k_t[:, chunk(c)]
        kd = jnp.concatenate([k_c * kdec_f, k_c * kdec_b], axis=0).astype(BF16)
        kv = _dot(kd, v_ref[chunk(c), :].astype(BF16))
        for t in range(2):
            kv_f[t][c] = kv[t * RET_DK:(t + 1) * RET_DK, t * RET_DV:(t + 1) * RET_DV]
            kv_b[t][c] = kv[CHUNK + t * RET_DK:CHUNK + (t + 1) * RET_DK, t * RET_DV:(t + 1) * RET_DV]

    st_f = [[None] * n_chunks for _ in range(2)]
    st_b = [[None] * n_chunks for _ in range(2)]
    per_seq = seq // CHUNK
    for t in range(2):
        for sq in range(n_chunks // per_seq):
            own = range(sq * per_seq, (sq + 1) * per_seq)
            if has_state:
                s_f = s0f_ref[sq, 0, t]
                s_b = s0b_ref[sq, 0, t]
            else:
                s_f = jnp.zeros((RET_DK, RET_DV), F32)
                s_b = s_f
            for c in own:
                st_f[t][c] = s_f
                s_f = s_f * cdec_f[t] + kv_f[t][c]
            for c in reversed(own):
                st_b[t][c] = s_b
                s_b = s_b * cdec_b[t] + kv_b[t][c]
            if not has_state:
                sf_ref[sq, slab, t] = s_f
                sb_ref[sq, slab, t] = s_b

    for c in range(n_chunks):
        q_c = q_ref[chunk(c), :]
        qm = jnp.concatenate([jnp.where(first_head, q_c, 0.0), jnp.where(first_head, 0.0, q_c)],
                             axis=0).astype(BF16)
        att = (_dot(qm, k_t[:, chunk(c)].astype(BF16)) * mask2).astype(BF16)
        states = jnp.concatenate(
            [jnp.concatenate([st_f[t][c], st_b[t][c]], axis=1) for t in range(2)], axis=0)
        qs = _dot(qm, states.astype(BF16))
        v_c = v_ref[chunk(c), :].astype(BF16)
        for t in range(2):
            rows = slice(t * CHUNK, (t + 1) * CHUNK)
            vsl = slice(t * RET_DV, (t + 1) * RET_DV)
            o = (_dot(att[rows], v_c[:, vsl]) + qs[rows, :RET_DV] * qdec_f[t]
                 + qs[rows, RET_DV:] * qdec_b[t])
            o = (o * lax.rsqrt(jnp.mean(o * o, axis=-1, keepdims=True) + EPS)) * rn_ref[...]
            hg = 0.5 * g_ref[chunk(c), vsl]
            o_ref[chunk(c), vsl] = (o * (hg * (1.0 + jnp.tanh(hg)))).astype(o_ref.dtype)


def _retention(p_act, seq, tables, ret_norm, layer, states=None, new_states=None):
    n_tok = p_act.shape[0]
    n_seq = n_tok // seq
    has_state = states is not None
    pair_qk, pair_v = 2 * RET_DK, 2 * RET_DV
    rows = max(seq, RET_ROWS)
    blk_seqs = rows // seq
    in_specs = [
        pl.BlockSpec((rows, pair_qk), lambda p, s: (s, OFF_QR // pair_qk + p)),
        pl.BlockSpec((rows, pair_qk), lambda p, s: (s, OFF_KR // pair_qk + p)),
        pl.BlockSpec((rows, pair_v), lambda p, s: (s, OFF_VR // pair_v + p)),
        pl.BlockSpec((rows, pair_v), lambda p, s: (s, OFF_GR // pair_v + p)),
        pl.BlockSpec((1, N_TAB, CHUNK, CHUNK), lambda p, s: (p, 0, 0, 0)),
        pl.BlockSpec((1, RET_DV), lambda p, s: (0, 0)),
    ]
    args = [p_act, p_act, p_act, p_act, tables, ret_norm]
    o_spec = pl.BlockSpec((rows, pair_v), lambda p, s: (s, p))
    o_shape = jax.ShapeDtypeStruct((n_tok, RET_V), BF16)
    st_spec = pl.BlockSpec((blk_seqs, 1, 2, RET_DK, RET_DV), lambda p, s: (s, layer, p, 0, 0))
    aliases, slab = {}, 0
    if has_state:
        in_specs += [st_spec, st_spec]
        args += list(states)
        out_specs, out_shape = o_spec, o_shape
    else:
        if new_states is None:
            st_spec = pl.BlockSpec((blk_seqs, DEPTH, 2, RET_DK, RET_DV), lambda p, s: (s, 0, p, 0, 0))
            slab = layer
        else:
            in_specs += [pl.BlockSpec(memory_space=pl.ANY)] * 2
            aliases = {len(args): 1, len(args) + 1: 2}
            args += list(new_states)
        st_shape = jax.ShapeDtypeStruct((n_seq, DEPTH, RET_HEADS, RET_DK, RET_DV), F32)
        out_specs, out_shape = [o_spec, st_spec, st_spec], [o_shape, st_shape, st_shape]
    return pl.pallas_call(
        functools.partial(_ret_kernel, seq=seq, has_state=has_state, slab=slab),
        grid=(RET_HEADS // 2, n_tok // rows),
        in_specs=in_specs,
        out_specs=out_specs,
        out_shape=out_shape,
        input_output_aliases=aliases,
        compiler_params=_cparams(("parallel", "parallel")),
        name="retention_lat" if has_state else "retention_ctx",
    )(*args)


FOURIER_ROWS = 2048


def _four_kernel(u0_ref, u1_ref, wc_ref, csn_ref, o_ref):
    gd = FOURIER_GROUP_DIM
    tc, ts = [], []
    for g in range(FOURIER_GROUPS):
        u_ref = (u0_ref, u1_ref)[g // 2]
        u_g = u_ref[:, (g % 2) * gd:(g % 2 + 1) * gd].astype(BF16)
        t = _dot(u_g, wc_ref[...])
        tc.append(t[:, :gd])
        ts.append(t[:, gd:])
    tc = jnp.concatenate(tc, axis=1).astype(BF16)
    ts = jnp.concatenate(ts, axis=1).astype(BF16)
    seq = csn_ref.shape[0]
    for sq in range(o_ref.shape[0] // seq):
        rows = slice(sq * seq, (sq + 1) * seq)
        t_sq = jnp.concatenate([tc[rows], ts[rows]], axis=0)
        o_ref[rows, :] = _dot(csn_ref[...], t_sq).astype(o_ref.dtype)


def _dft_tables(n):
    k = np.arange(n, dtype=np.int64)
    ang = 2.0 * np.pi * ((k[:, None] * k[None, :]) % n).astype(np.float64) / n
    scale = 1.0 / np.sqrt(n)
    return np.cos(ang) * scale, np.sin(ang) * scale


def _fourier(p_act, seq):
    n_tok = p_act.shape[0]
    cc, sc = _dft_tables(FOURIER_GROUP_DIM)
    wc = jnp.asarray(np.concatenate([cc, -sc], axis=1), F32).astype(BF16)
    cn, sn = _dft_tables(seq)
    csn = jnp.asarray(np.concatenate([cn, sn], axis=1), F32).astype(BF16)
    const = lambda shape: pl.BlockSpec(shape, lambda s: (0, 0))
    half = FOURIER_W // 2
    rows = max(seq, FOURIER_ROWS)
    return pl.pallas_call(
        _four_kernel,
        grid=(n_tok // rows,),
        in_specs=[
            pl.BlockSpec((rows, half), lambda s: (s, OFF_UF // half)),
            pl.BlockSpec((rows, half), lambda s: (s, OFF_UF // half + 1)),
            const((FOURIER_GROUP_DIM, 2 * FOURIER_GROUP_DIM)),
            const((seq, 2 * seq)),
        ],
        out_specs=pl.BlockSpec((rows, FOURIER_W), lambda s: (s, 0)),
        out_shape=jax.ShapeDtypeStruct((n_tok, FOURIER_W), BF16),
        compiler_params=_cparams(("parallel",)),
        name="fourier",
    )(p_act, p_act, wc, csn)


MIX_TM = 1024
MERGE_STEPS = 4
MERGE_TM = MIX_TM // MERGE_STEPS
FF_TILE = 256
FF_STEPS = D_FF // FF_TILE
FF_ROWS = 256


def _mix_kernel(oa_ref, or_ref, of_ref, ga_ref, gr_ref, gf_ref, x_ref, g1_ref, sh_ref, sc_ref,
                g2_ref, n2_ref, wba_ref, wbr_ref, wbf_ref, wo_ref, wa_ref, wv_ref, cw_ref, cb_ref,
                wd_ref, o_ref, h_scr, acc_scr, wu_scr, wd_scr, *, seq):
    i = pl.program_id(0)
    j = pl.program_id(1)
    chunk = jnp.maximum(j - MERGE_STEPS, 0)
    a_cols = pl.ds(pl.multiple_of(chunk * FF_TILE, FF_TILE), FF_TILE)
    v_cols = pl.ds(pl.multiple_of((FF_STEPS + chunk) * FF_TILE, FF_TILE), FF_TILE)

    @pl.when(j < MERGE_STEPS)
    def _():
        gate = lambda ref: jnp.tanh(ref[...].astype(F32)) + 1.0
        dst = pl.ds(pl.multiple_of(j * MERGE_TM, MERGE_TM), MERGE_TM)
        merged = (gate(ga_ref) * _dot(oa_ref[...], wba_ref[...])
                  + gate(gr_ref) * _dot(or_ref[...], wbr_ref[...])
                  + gate(gf_ref) * _dot(of_ref[...], wbf_ref[...]))
        x1 = x_ref[...] + g1_ref[0] * _dot((0.5 * merged).astype(BF16), wo_ref[...])
        o_ref[dst, :] = x1
        r = lax.rsqrt(jnp.mean(x1 * x1, axis=-1, keepdims=True) + EPS)
        h = (x1 * r) * n2_ref[...] * (1.0 + sc_ref[0]) + sh_ref[0]
        h_scr[dst, :] = h.astype(BF16)
        acc_scr[dst, :] = jnp.zeros((MERGE_TM, D_MODEL), F32)

    @pl.when(jnp.logical_and(i == 0, j >= MERGE_STEPS))
    def _():
        wu_scr[:, a_cols] = wa_ref[0].astype(BF16)
        wu_scr[:, v_cols] = wv_ref[0].astype(BF16)
        wd_scr[a_cols, :] = wd_ref[0].astype(BF16)

    def mlp(last):
        blocks = [slice(b * FF_ROWS, (b + 1) * FF_ROWS) for b in range(MIX_TM // FF_ROWS)]
        wa, wv, wd = wu_scr[:, a_cols], wu_scr[:, v_cols], wd_scr[a_cols, :]
        a = jnp.concatenate([_dot(h_scr[rows, :], wa) for rows in blocks], axis=0)
        val = [_dot(h_scr[rows, :], wv) for rows in blocks]
        pos = lax.broadcasted_iota(jnp.int32, a.shape, 0) % seq
        prev = jnp.where(pos == 0, 0.0, pltpu.roll(a, 1, 0))
        nxt = jnp.where(pos == seq - 1, 0.0, pltpu.roll(a, MIX_TM - 1, 0))
        cw = cw_ref[...]
        ac = prev * cw[0:1] + a * cw[1:2] + nxt * cw[2:3] + cb_ref[...]
        for b, rows in enumerate(blocks):
            act = jax.nn.gelu(ac[rows]) * val[b]
            down = _dot(act.astype(BF16), wd)
            if last:
                o_ref[rows, :] = o_ref[rows, :] + g2_ref[0] * (acc_scr[rows, :] + down)
            else:
                acc_scr[rows, :] += down

    last_step = pl.num_programs(1) - 1

    @pl.when(jnp.logical_and(j >= MERGE_STEPS, j < last_step))
    def _():
        mlp(last=False)

    @pl.when(j == last_step)
    def _():
        mlp(last=True)


def _mix(o_att, o_ret, o_four, p_gate, x2d, seq, mod, rows_per_mod, w_att, w_ret, w_four, w_out,
         norm2, w_up, conv_w, conv_b, w_down, layer):
    n_tok = x2d.shape[0]
    mod_idx = lambda i: (i * MIX_TM) // rows_per_mod
    sub = lambda i, j: i * MERGE_STEPS + jnp.minimum(j, MERGE_STEPS - 1)
    ff = lambda j: jnp.maximum(j - MERGE_STEPS, 0)
    ff_w = lambda i, j: jnp.where(i == 0, ff(j), FF_STEPS - 1)
    br = lambda: pl.BlockSpec((MERGE_TM, 512), lambda i, j: (sub(i, j), 0))
    gate = lambda k: pl.BlockSpec((MERGE_TM, D_MODEL), lambda i, j: (sub(i, j), k))
    modv = lambda k: pl.BlockSpec((1, 1, D_MODEL), lambda i, j: (mod_idx(i), 0, k))
    const = lambda shape: pl.BlockSpec(shape, lambda i, j: (0, 0))
    return pl.pallas_call(
        functools.partial(_mix_kernel, seq=seq),
        grid=(n_tok // MIX_TM, MERGE_STEPS + FF_STEPS),
        in_specs=[
            br(), br(), br(), gate(0), gate(1), gate(2),
            pl.BlockSpec((MERGE_TM, D_MODEL), lambda i, j: (sub(i, j), 0)),
            modv(2), modv(3), modv(4), modv(5),
            const((1, D_MODEL)),
            const((512, D_MODEL)), const((512, D_MODEL)), const((512, D_MODEL)),
            const((D_MODEL, D_MODEL)),
            pl.BlockSpec((1, D_MODEL, FF_TILE), lambda i, j: (layer, 0, ff_w(i, j))),
            pl.BlockSpec((1, D_MODEL, FF_TILE), lambda i, j: (layer, 0, FF_STEPS + ff_w(i, j))),
            pl.BlockSpec((3, FF_TILE), lambda i, j: (0, ff(j))),
            pl.BlockSpec((1, FF_TILE), lambda i, j: (0, ff(j))),
            pl.BlockSpec((1, FF_TILE, D_MODEL), lambda i, j: (layer, ff_w(i, j), 0)),
        ],
        out_specs=pl.BlockSpec((MIX_TM, D_MODEL), lambda i, j: (i, 0)),
        out_shape=jax.ShapeDtypeStruct((n_tok, D_MODEL), F32),
        scratch_shapes=[pltpu.VMEM((MIX_TM, D_MODEL), BF16), pltpu.VMEM((MIX_TM, D_MODEL), F32),
                        pltpu.VMEM((D_MODEL, 2 * D_FF), BF16), pltpu.VMEM((D_FF, D_MODEL), BF16)],
        compiler_params=_cparams(("arbitrary", "arbitrary")),
        name="mix",
    )(o_att, o_ret, o_four, p_gate, p_gate, p_gate, x2d, mod, mod, mod, mod, norm2,
      w_att, w_ret, w_four, w_out, w_up, w_up, conv_w, conv_b, w_down)


def _rope_tables(n_tok):
    rows = n_tok // GRID_W
    row_id = jnp.repeat(jnp.arange(rows), GRID_W).astype(F32)
    col_id = jnp.tile(jnp.arange(GRID_W), rows).astype(F32)
    n_freq = HEAD_DIM // 4
    inv = ROPE_THETA ** (-jnp.arange(n_freq, dtype=F32) / n_freq)
    ang = jnp.concatenate([row_id[None, :] * inv[:, None], col_id[None, :] * inv[:, None]], axis=0)
    return jnp.cos(ang), jnp.sin(ang)


def kernel(x_prompt, x_sample, cache_k, cache_v, state_ret_fwd, state_ret_bwd, c, c_ctx, w_ada, b_ada, norm1, w_in, q_norm, k_norm, ret_decay_f, ret_decay_b, ret_norm, w_br_att, w_br_ret, w_br_four, w_out, norm2, w_up, conv_w, conv_b, w_down):
    batch, seq, _ = x_prompt.shape
    dec_batch, dec_seq, _ = x_sample.shape
    past = cache_k.shape[2]
    for n_seq, length in ((batch, seq), (dec_batch, dec_seq)):
        assert length % CHUNK == 0 and MIX_TM % length == 0 and (n_seq * length) % MIX_TM == 0
    assert dec_seq % Q_BLOCK == 0 and dec_seq % GRID_W == 0 and past % CHUNK == 0

    cond_all = jnp.concatenate([c_ctx[None, :], c], axis=0)
    mod_all = _ada(cond_all, w_ada, b_ada)
    cos_t, sin_t = _rope_tables(dec_seq)
    ck = cache_k.reshape(dec_batch, DEPTH, past, ATT_KV)
    cv = cache_v.reshape(dec_batch, DEPTH, past, ATT_KV)

    xp = x_prompt.reshape(batch * seq, D_MODEL)
    xs = x_sample.reshape(dec_batch * dec_seq, D_MODEL)
    new_kv = new_st = None
    for l in range(DEPTH):
        w_att, w_ret, w_four = (w_br_att[l].astype(BF16), w_br_ret[l].astype(BF16),
                                w_br_four[l].astype(BF16))
        w_o = w_out[l].astype(BF16)
        n1, n2 = norm1[l][None, :], norm2[l][None, :]
        qn, kn = q_norm[l][:, None], k_norm[l][:, None]
        rn = ret_norm[l][None, :]
        dec = _ret_tables(jnp.broadcast_to(
            jnp.concatenate([ret_decay_f[l], ret_decay_b[l]])[:, None].astype(F32), (2 * RET_HEADS, 128)))
        cw, cb = conv_w[l], conv_b[l][None, :]
        mod_ctx = mod_all[l, 0:1].reshape(1, 1, 6 * D_MODEL)
        mod_lat = mod_all[l, 1:].reshape(dec_batch, 1, 6 * D_MODEL)

        p_act, p_gate = _in_proj(xp, mod_ctx, batch * seq, n1, w_in, l)
        o_att, *new_kv = _att_ctx(p_act, seq, qn, kn, l, new_kv)
        o_ret, *new_st = _retention(p_act, seq, dec, rn, l, new_states=new_st)
        o_four = _fourier(p_act, seq)
        xp = _mix(o_att, o_ret, o_four, p_gate, xp, seq, mod_ctx, batch * seq,
                  w_att, w_ret, w_four, w_o, n2, w_up, cw, cb, w_down, l)

        p_act, p_gate = _in_proj(xs, mod_lat, dec_seq, n1, w_in, l)
        o_att = _att_lat(p_act, dec_seq, ck, cv, l, qn, kn, cos_t, sin_t)
        o_ret = _retention(p_act, dec_seq, dec, rn, l, states=(state_ret_fwd, state_ret_bwd))
        o_four = _fourier(p_act, dec_seq)
        xs = _mix(o_att, o_ret, o_four, p_gate, xs, dec_seq, mod_lat, dec_seq,
                  w_att, w_ret, w_four, w_o, n2, w_up, cw, cb, w_down, l)

    kv_shape = (batch, DEPTH, seq, N_KV_HEADS, HEAD_DIM)
    return (xp.reshape(batch, seq, D_MODEL), xs.reshape(dec_batch, dec_seq, D_MODEL),
            new_kv[0].reshape(kv_shape), new_kv[1].reshape(kv_shape), new_st[0], new_st[1])
```

```python
import functools

import numpy as np
import jax
import jax.numpy as jnp
from jax import lax
from jax.experimental import pallas as pl
from jax.experimental.pallas import tpu as pltpu

D_MODEL = 1024
DEPTH = 2
GRID_W = 64
HEAD_DIM = 64
N_HEADS = 8
N_KV_HEADS = 2
Q_PER_KV = N_HEADS // N_KV_HEADS
ATT_Q = N_HEADS * HEAD_DIM
ATT_KV = N_KV_HEADS * HEAD_DIM
RET_HEADS = 4
RET_DK = 64
RET_DV = 128
RET_V = RET_HEADS * RET_DV
FOURIER_GROUPS = 4
FOURIER_GROUP_DIM = 128
FOURIER_W = FOURIER_GROUPS * FOURIER_GROUP_DIM
D_FF = 2816
CHUNK = 128
Q_BLOCK = 256
ROPE_THETA = 10000.0
EPS = 1e-6
LOG2_E = 1.4426950408889634

F32 = jnp.float32
BF16 = jnp.bfloat16

OFF_QA, OFF_KA, OFF_VA = 0, 512, 640
OFF_QR, OFF_KR, OFF_VR, OFF_GR, OFF_UF = 768, 1024, 1280, 1792, 2304
W_IN_SPLIT = 2816
OFF_GATE = 3072
P_W = OFF_GATE + 3 * D_MODEL

VMEM_LIMIT = 56 * 1024 * 1024


def _cparams(sem):
    return pltpu.CompilerParams(dimension_semantics=sem, vmem_limit_bytes=VMEM_LIMIT)


def _dot(a, b):
    return jnp.dot(a, b, preferred_element_type=F32)


def _ada_kernel(cond_ref, w_ref, b_ref, o_ref):
    cnd = cond_ref[...]
    s = cnd * jax.nn.sigmoid(cnd)
    o_ref[0] = _dot(s.astype(BF16), w_ref[0].astype(BF16)) + b_ref[0]


def _ada(cond_all, w_ada, b_ada):
    n = cond_all.shape[0]
    tn = 1024
    return pl.pallas_call(
        _ada_kernel,
        grid=(DEPTH, 6 * D_MODEL // tn),
        in_specs=[
            pl.BlockSpec((n, D_MODEL), lambda l, j: (0, 0)),
            pl.BlockSpec((1, D_MODEL, tn), lambda l, j: (l, 0, j)),
            pl.BlockSpec((1, 1, tn), lambda l, j: (l, 0, j)),
        ],
        out_specs=pl.BlockSpec((1, n, tn), lambda l, j: (l, 0, j)),
        out_shape=jax.ShapeDtypeStruct((DEPTH, n, 6 * D_MODEL), F32),
        compiler_params=_cparams(("parallel", "parallel")),
        name="ada",
    )(cond_all, w_ada, b_ada.reshape(DEPTH, 1, 6 * D_MODEL))


IN_TN = 1536
IN_ROWS = 256
MAIN_STEPS = OFF_GATE // IN_TN


def _in_kernel(x_ref, sh_ref, sc_ref, g_ref, w_ref, om_ref, og_ref, h_scr, w_scr):
    i = pl.program_id(0)
    j = pl.program_id(1)
    cols = pl.ds(pl.multiple_of(j * IN_TN, IN_TN), IN_TN)

    @pl.when(i == 0)
    def _():
        w_scr[:, cols] = w_ref[0].astype(BF16)

    @pl.when(j == 0)
    def _():
        for b in range(x_ref.shape[0] // IN_ROWS):
            rows = slice(b * IN_ROWS, (b + 1) * IN_ROWS)
            x = x_ref[rows, :]
            r = lax.rsqrt(jnp.mean(x * x, axis=-1, keepdims=True) + EPS)
            h = ((x * r) * g_ref[...] * (1.0 + sc_ref[0]) + sh_ref[0]).astype(BF16)
            h_scr[rows, :] = h
            om_ref[rows, :] = _dot(h, w_scr[:, 0:IN_TN])

    @pl.when(jnp.logical_and(j > 0, j < MAIN_STEPS))
    def _():
        om_ref[...] = _dot(h_scr[...], w_scr[:, cols])

    @pl.when(j >= MAIN_STEPS)
    def _():
        og_ref[...] = (0.5 * _dot(h_scr[...], w_scr[:, cols])).astype(og_ref.dtype)


def _in_proj(x2d, mod, rows_per_mod, norm1, w_in, layer):
    n_tok = x2d.shape[0]
    tm, tn = 1024, IN_TN
    n_col = P_W // tn
    mod_idx = lambda i: (i * tm) // rows_per_mod

    def w_col(i, j):
        jj = jnp.where(i == 0, j, n_col - 1)
        col = jnp.where(jj < MAIN_STEPS, jj * tn, W_IN_SPLIT + (jj - MAIN_STEPS) * tn)
        return pl.multiple_of(col, 128)

    return pl.pallas_call(
        _in_kernel,
        grid=(n_tok // tm, n_col),
        in_specs=[
            pl.BlockSpec((tm, D_MODEL), lambda i, j: (i, 0)),
            pl.BlockSpec((1, 1, D_MODEL), lambda i, j: (mod_idx(i), 0, 0)),
            pl.BlockSpec((1, 1, D_MODEL), lambda i, j: (mod_idx(i), 0, 1)),
            pl.BlockSpec((1, D_MODEL), lambda i, j: (0, 0)),
            pl.BlockSpec((pl.Element(1), pl.Element(D_MODEL), pl.Element(tn)),
                         lambda i, j: (layer, 0, w_col(i, j))),
        ],
        out_specs=[
            pl.BlockSpec((tm, tn), lambda i, j: (i, jnp.minimum(j, MAIN_STEPS - 1))),
            pl.BlockSpec((tm, tn), lambda i, j: (i, jnp.maximum(j - MAIN_STEPS, 0))),
        ],
        out_shape=[jax.ShapeDtypeStruct((n_tok, OFF_GATE), F32),
                   jax.ShapeDtypeStruct((n_tok, P_W - OFF_GATE), BF16)],
        scratch_shapes=[pltpu.VMEM((tm, D_MODEL), BF16), pltpu.VMEM((D_MODEL, P_W), BF16)],
        compiler_params=_cparams(("arbitrary", "arbitrary")),
        name="in_proj",
    )(x2d, mod, mod, norm1, w_in)


def _norm_rope_heads_t(x_t, n_heads, g_col, cos_t=None, sin_t=None):
    quarter = HEAD_DIM // 4
    outs = []
    for h in range(n_heads):
        x = x_t[h * HEAD_DIM:(h + 1) * HEAD_DIM, :]
        r = lax.rsqrt(jnp.mean(x * x, axis=0, keepdims=True) + EPS)
        y = (x * r) * g_col
        if cos_t is not None:
            pieces = []
            for a in range(2):
                c = cos_t[a * quarter:(a + 1) * quarter]
                s = sin_t[a * quarter:(a + 1) * quarter]
                x1 = y[2 * a * quarter:(2 * a + 1) * quarter]
                x2 = y[(2 * a + 1) * quarter:(2 * a + 2) * quarter]
                pieces += [x1 * c - x2 * s, x2 * c + x1 * s]
            y = jnp.concatenate(pieces, axis=0)
        outs.append(y)
    return outs


SHIFT_LIMIT = 40.0


def _score_bound(k_heads_t, q_gain):
    k_sq = [jnp.max(jnp.sum(k * k, axis=0, keepdims=True)) for k in k_heads_t]
    k_norm = jnp.sqrt(functools.reduce(jnp.maximum, k_sq))
    q_norm = HEAD_DIM ** 0.5 * jnp.max(jnp.abs(q_gain))
    return 1.01 * (HEAD_DIM ** -0.5 * LOG2_E) * q_norm * k_norm


def _attend_t(q_heads_t, k_bf, v_t_bf, score_bound, store):
    tq = q_heads_t[0].shape[1]
    cols = Q_PER_KV * tq
    zeros = jnp.zeros((HEAD_DIM, cols), F32)
    q_kv = [jnp.concatenate(q_heads_t[kv * Q_PER_KV:(kv + 1) * Q_PER_KV], axis=1)
            * (HEAD_DIM ** -0.5 * LOG2_E) for kv in range(N_KV_HEADS)]
    rhs = jnp.concatenate([jnp.concatenate([q_kv[0], zeros], axis=1),
                           jnp.concatenate([zeros, q_kv[1]], axis=1)], axis=0).astype(BF16)
    def finish(shift):
        s_all = _dot(k_bf, rhs)
        out_rows = []
        for kv in range(N_KV_HEADS):
            s_t = s_all[:, kv * cols:(kv + 1) * cols]
            m = jnp.max(s_t, axis=0, keepdims=True) if shift is None else shift
            e = jnp.exp2(s_t - m)
            inv = 1.0 / jnp.sum(e, axis=0, keepdims=True)
            o_t = _dot(v_t_bf[kv * HEAD_DIM:(kv + 1) * HEAD_DIM, :], e.astype(BF16)) * inv
            out_rows += [o_t[:, g * tq:(g + 1) * tq] for g in range(Q_PER_KV)]
        store(jnp.concatenate(out_rows, axis=0))

    if score_bound is None:
        finish(None)
        return

    @pl.when(score_bound <= SHIFT_LIMIT)
    def _():
        finish(score_bound)

    @pl.when(jnp.logical_not(score_bound <= SHIFT_LIMIT))
    def _():
        finish(None)


def _att_ctx_kernel(q_ref, k_ref, v_ref, qn_ref, kn_ref, *rest, slab):
    o_ref, ko_ref, vo_ref = rest[-3:]
    seq = ko_ref.shape[2]
    if len(rest) == 3:
        for other in range(ko_ref.shape[1]):
            if other != slab:
                ko_ref[:, other] = jnp.zeros_like(ko_ref[:, other])
                vo_ref[:, other] = jnp.zeros_like(vo_ref[:, other])
    for sq in range(ko_ref.shape[0]):
        rows = slice(sq * seq, (sq + 1) * seq)
        k_heads = _norm_rope_heads_t(k_ref[rows, :].T, N_KV_HEADS, kn_ref[...])
        k_n = jnp.concatenate(k_heads, axis=0).T
        ko_ref[sq, slab] = k_n
        v = v_ref[rows, :]
        vo_ref[sq, slab] = v
        q_heads = _norm_rope_heads_t(q_ref[rows, :].T, N_HEADS, qn_ref[...])

        def store(o_t, rows=rows):
            o_ref[rows, :] = o_t.T.astype(o_ref.dtype)

        _attend_t(q_heads, k_n.astype(BF16), v.T.astype(BF16), None, store)


ATT_CTX_ROWS = 1024


def _att_ctx(p_act, seq, q_norm, k_norm, layer, caches=None):
    n_tok = p_act.shape[0]
    batch = n_tok // seq
    rows = max(seq, ATT_CTX_ROWS)
    blk = rows // seq
    in_specs = [
        pl.BlockSpec((rows, ATT_Q), lambda b: (b, OFF_QA // ATT_Q)),
        pl.BlockSpec((rows, ATT_KV), lambda b: (b, OFF_KA // ATT_KV)),
        pl.BlockSpec((rows, ATT_KV), lambda b: (b, OFF_VA // ATT_KV)),
        pl.BlockSpec((HEAD_DIM, 1), lambda b: (0, 0)),
        pl.BlockSpec((HEAD_DIM, 1), lambda b: (0, 0)),
    ]
    args = [p_act, p_act, p_act, q_norm, k_norm]
    if caches is None:
        cache_spec = pl.BlockSpec((blk, DEPTH, seq, ATT_KV), lambda b: (b, 0, 0, 0))
        slab, aliases = layer, {}
    else:
        cache_spec = pl.BlockSpec((blk, 1, seq, ATT_KV), lambda b: (b, layer, 0, 0))
        slab, aliases = 0, {len(args): 1, len(args) + 1: 2}
        in_specs += [pl.BlockSpec(memory_space=pl.ANY)] * 2
        args += list(caches)
    cache_shape = jax.ShapeDtypeStruct((batch, DEPTH, seq, ATT_KV), F32)
    return pl.pallas_call(
        functools.partial(_att_ctx_kernel, slab=slab),
        grid=(n_tok // rows,),
        in_specs=in_specs,
        out_specs=[pl.BlockSpec((rows, ATT_Q), lambda b: (b, 0)), cache_spec, cache_spec],
        out_shape=[jax.ShapeDtypeStruct((n_tok, ATT_Q), BF16), cache_shape, cache_shape],
        input_output_aliases=aliases,
        compiler_params=_cparams(("parallel",)),
        name="att_ctx",
    )(*args)


def _att_lat_kernel(q_ref, k_ref, v_ref, ck_ref, cv_ref, qn_ref, kn_ref,
                    cq_ref, sq_ref, ck_t_ref, sk_t_ref, o_ref, kf_scr, vt_scr, bound_scr, *, seq):
    @pl.when(pl.program_id(1) == 0)
    def _():
        k_heads = _norm_rope_heads_t(k_ref[...].T, N_KV_HEADS, kn_ref[...], ck_t_ref[...], sk_t_ref[...])
        kf_scr[0:seq, :] = jnp.concatenate(k_heads, axis=0).T.astype(BF16)
        past_k = ck_ref[0, 0]
        kf_scr[seq:, :] = past_k.astype(BF16)
        vt_scr[:, 0:seq] = v_ref[...].T.astype(BF16)
        vt_scr[:, seq:] = cv_ref[0, 0].T.astype(BF16)
        past_t = past_k.T
        past_heads = [past_t[kv * HEAD_DIM:(kv + 1) * HEAD_DIM] for kv in range(N_KV_HEADS)]
        bound_scr[0] = _score_bound(k_heads + past_heads, qn_ref[...])

    q_heads = _norm_rope_heads_t(q_ref[...].T, N_HEADS, qn_ref[...], cq_ref[...], sq_ref[...])

    def store(o_t):
        o_ref[...] = o_t.T.astype(o_ref.dtype)

    _attend_t(q_heads, kf_scr[...], vt_scr[...], bound_scr[0], store)


def _att_lat(p_act, seq, cache_k, cache_v, layer, q_norm, k_norm, cos_t, sin_t):
    n_tok = p_act.shape[0]
    nb = seq // Q_BLOCK
    past = cache_k.shape[2]
    return pl.pallas_call(
        functools.partial(_att_lat_kernel, seq=seq),
        grid=(n_tok // seq, nb),
        in_specs=[
            pl.BlockSpec((Q_BLOCK, ATT_Q), lambda b, i: (b * nb + i, OFF_QA // ATT_Q)),
            pl.BlockSpec((seq, ATT_KV), lambda b, i: (b, OFF_KA // ATT_KV)),
            pl.BlockSpec((seq, ATT_KV), lambda b, i: (b, OFF_VA // ATT_KV)),
            pl.BlockSpec((1, 1, past, ATT_KV), lambda b, i: (b, layer, 0, 0)),
            pl.BlockSpec((1, 1, past, ATT_KV), lambda b, i: (b, layer, 0, 0)),
            pl.BlockSpec((HEAD_DIM, 1), lambda b, i: (0, 0)),
            pl.BlockSpec((HEAD_DIM, 1), lambda b, i: (0, 0)),
            pl.BlockSpec((HEAD_DIM // 2, Q_BLOCK), lambda b, i: (0, i)),
            pl.BlockSpec((HEAD_DIM // 2, Q_BLOCK), lambda b, i: (0, i)),
            pl.BlockSpec((HEAD_DIM // 2, seq), lambda b, i: (0, 0)),
            pl.BlockSpec((HEAD_DIM // 2, seq), lambda b, i: (0, 0)),
        ],
        out_specs=pl.BlockSpec((Q_BLOCK, ATT_Q), lambda b, i: (b * nb + i, 0)),
        out_shape=jax.ShapeDtypeStruct((n_tok, ATT_Q), BF16),
        scratch_shapes=[pltpu.VMEM((seq + past, ATT_KV), BF16),
                        pltpu.VMEM((ATT_KV, seq + past), BF16),
                        pltpu.SMEM((1,), F32)],
        compiler_params=_cparams(("parallel", "arbitrary")),
        name="att_lat",
    )(p_act, p_act, p_act, cache_k, cache_v, q_norm, k_norm, cos_t, sin_t, cos_t, sin_t)


RET_ROWS = 2048


def _log_sigmoid(d):
    return jnp.minimum(d, 0.0) - jnp.log1p(jnp.exp(-jnp.abs(d)))


TAB_MASK, TAB_QF, TAB_QB, TAB_KF, TAB_KB, TAB_C, N_TAB = 0, 2, 4, 6, 7, 8, 9
TAB_CF, TAB_CB = 0, 2


def _ret_tables_kernel(dec_ref, tab_ref):
    hp = pl.program_id(0)
    ii = lax.broadcasted_iota(jnp.int32, (CHUNK, CHUNK), 0)
    jj = lax.broadcasted_iota(jnp.int32, (CHUNK, CHUNK), 1)
    rel = (ii - jj).astype(F32)
    row = ii.astype(F32)
    lane = jj.astype(F32)
    lgf = [_log_sigmoid(dec_ref[pl.ds(2 * hp + t, 1), :]) for t in range(2)]
    lgb = [_log_sigmoid(dec_ref[pl.ds(RET_HEADS + 2 * hp + t, 1), :]) for t in range(2)]
    for t in range(2):
        tab_ref[0, TAB_MASK + t] = jnp.where(
            rel > 0, jnp.exp(jnp.maximum(rel, 0.0) * lgf[t]),
            jnp.where(rel < 0, jnp.exp(jnp.maximum(-rel, 0.0) * lgb[t]), 2.0))
        tab_ref[0, TAB_QF + t] = jnp.exp((row + 1.0) * lgf[t])
        tab_ref[0, TAB_QB + t] = jnp.exp((CHUNK - row) * lgb[t])
    tab_ref[0, TAB_KF] = jnp.exp((CHUNK - 1.0 - lane) * jnp.where(ii < RET_DK, lgf[0], lgf[1]))
    tab_ref[0, TAB_KB] = jnp.exp(lane * jnp.where(ii < RET_DK, lgb[0], lgb[1]))
    c_rows = jnp.where(ii == TAB_CF, lgf[0], jnp.where(ii == TAB_CF + 1, lgf[1],
                       jnp.where(ii == TAB_CB, lgb[0], lgb[1])))
    tab_ref[0, TAB_C] = jnp.exp(CHUNK * c_rows)


def _ret_tables(dec):
    return pl.pallas_call(
        _ret_tables_kernel,
        grid=(RET_HEADS // 2,),
        in_specs=[pl.BlockSpec((8, 128), lambda p: (0, 0))],
        out_specs=pl.BlockSpec((1, N_TAB, CHUNK, CHUNK), lambda p: (p, 0, 0, 0)),
        out_shape=jax.ShapeDtypeStruct((RET_HEADS // 2, N_TAB, CHUNK, CHUNK), F32),
        compiler_params=_cparams(("parallel",)),
        name="ret_tables",
    )(dec)


def _ret_kernel(q_ref, k_ref, v_ref, g_ref, tab_ref, rn_ref, *rest, seq, has_state, slab):
    if has_state:
        s0f_ref, s0b_ref, o_ref = rest
    else:
        o_ref, sf_ref, sb_ref = rest[-3:]
        if len(rest) == 3:
            for other in range(sf_ref.shape[1]):
                if other != slab:
                    sf_ref[:, other] = jnp.zeros_like(sf_ref[:, other])
                    sb_ref[:, other] = jnp.zeros_like(sb_ref[:, other])
    n_chunks = q_ref.shape[0] // CHUNK
    jj = lax.broadcasted_iota(jnp.int32, (CHUNK, CHUNK), 1)
    mask2 = jnp.concatenate([tab_ref[0, TAB_MASK], tab_ref[0, TAB_MASK + 1]], axis=0)
    qdec_f = [tab_ref[0, TAB_QF + t] for t in range(2)]
    qdec_b = [tab_ref[0, TAB_QB + t] for t in range(2)]
    cdec_f = [tab_ref[0, TAB_C, TAB_CF + t:TAB_CF + t + 1, :] for t in range(2)]
    cdec_b = [tab_ref[0, TAB_C, TAB_CB + t:TAB_CB + t + 1, :] for t in range(2)]
    kdec_f = tab_ref[0, TAB_KF]
    kdec_b = tab_ref[0, TAB_KB]

    k_t = (k_ref[...] * (RET_DK ** -0.5)).T
    first_head = jj < RET_DK

    def chunk(c):
        return slice(c * CHUNK, (c + 1) * CHUNK)

    kv_f = [[None] * n_chunks for _ in range(2)]
    kv_b = [[None] * n_chunks for _ in range(2)]
    for c in range(n_chunks):
        k_c = k_t[:, chunk(c)]
        kd = jnp.concatenate([k_c * kdec_f, k_c * kdec_b], axis=0).astype(BF16)
        kv = _dot(kd, v_ref[chunk(c), :].astype(BF16))
        for t in range(2):
            kv_f[t][c] = kv[t * RET_DK:(t + 1) * RET_DK, t * RET_DV:(t + 1) * RET_DV]
            kv_b[t][c] = kv[CHUNK + t * RET_DK:CHUNK + (t + 1) * RET_DK, t * RET_DV:(t + 1) * RET_DV]

    st_f = [[None] * n_chunks for _ in range(2)]
    st_b = [[None] * n_chunks for _ in range(2)]
    per_seq = seq // CHUNK
    for t in range(2):
        for sq in range(n_chunks // per_seq):
            own = range(sq * per_seq, (sq + 1) * per_seq)
            if has_state:
                s_f = s0f_ref[sq, 0, t]
                s_b = s0b_ref[sq, 0, t]
            else:
                s_f = jnp.zeros((RET_DK, RET_DV), F32)
                s_b = s_f
            for c in own:
                st_f[t][c] = s_f
                s_f = s_f * cdec_f[t] + kv_f[t][c]
            for c in reversed(own):
                st_b[t][c] = s_b
                s_b = s_b * cdec_b[t] + kv_b[t][c]
            if not has_state:
                sf_ref[sq, slab, t] = s_f
                sb_ref[sq, slab, t] = s_b

    for c in range(n_chunks):
        q_c = q_ref[chunk(c), :]
        qm = jnp.concatenate([jnp.where(first_head, q_c, 0.0), jnp.where(first_head, 0.0, q_c)],
                             axis=0).astype(BF16)
        att = (_dot(qm, k_t[:, chunk(c)].astype(BF16)) * mask2).astype(BF16)
        states = jnp.concatenate(
            [jnp.concatenate([st_f[t][c], st_b[t][c]], axis=1) for t in range(2)], axis=0)
        qs = _dot(qm, states.astype(BF16))
        v_c = v_ref[chunk(c), :].astype(BF16)
        for t in range(2):
            rows = slice(t * CHUNK, (t + 1) * CHUNK)
            vsl = slice(t * RET_DV, (t + 1) * RET_DV)
            o = (_dot(att[rows], v_c[:, vsl]) + qs[rows, :RET_DV] * qdec_f[t]
                 + qs[rows, RET_DV:] * qdec_b[t])
            o = (o * lax.rsqrt(jnp.mean(o * o, axis=-1, keepdims=True) + EPS)) * rn_ref[...]
            hg = 0.5 * g_ref[chunk(c), vsl]
            o_ref[chunk(c), vsl] = (o * (hg * (1.0 + jnp.tanh(hg)))).astype(o_ref.dtype)


def _retention(p_act, seq, tables, ret_norm, layer, states=None, new_states=None):
    n_tok = p_act.shape[0]
    n_seq = n_tok // seq
    has_state = states is not None
    pair_qk, pair_v = 2 * RET_DK, 2 * RET_DV
    rows = max(seq, RET_ROWS)
    blk_seqs = rows // seq
    in_specs = [
        pl.BlockSpec((rows, pair_qk), lambda p, s: (s, OFF_QR // pair_qk + p)),
        pl.BlockSpec((rows, pair_qk), lambda p, s: (s, OFF_KR // pair_qk + p)),
        pl.BlockSpec((rows, pair_v), lambda p, s: (s, OFF_VR // pair_v + p)),
        pl.BlockSpec((rows, pair_v), lambda p, s: (s, OFF_GR // pair_v + p)),
        pl.BlockSpec((1, N_TAB, CHUNK, CHUNK), lambda p, s: (p, 0, 0, 0)),
        pl.BlockSpec((1, RET_DV), lambda p, s: (0, 0)),
    ]
    args = [p_act, p_act, p_act, p_act, tables, ret_norm]
    o_spec = pl.BlockSpec((rows, pair_v), lambda p, s: (s, p))
    o_shape = jax.ShapeDtypeStruct((n_tok, RET_V), BF16)
    st_spec = pl.BlockSpec((blk_seqs, 1, 2, RET_DK, RET_DV), lambda p, s: (s, layer, p, 0, 0))
    aliases, slab = {}, 0
    if has_state:
        in_specs += [st_spec, st_spec]
        args += list(states)
        out_specs, out_shape = o_spec, o_shape
    else:
        if new_states is None:
            st_spec = pl.BlockSpec((blk_seqs, DEPTH, 2, RET_DK, RET_DV), lambda p, s: (s, 0, p, 0, 0))
            slab = layer
        else:
            in_specs += [pl.BlockSpec(memory_space=pl.ANY)] * 2
            aliases = {len(args): 1, len(args) + 1: 2}
            args += list(new_states)
        st_shape = jax.ShapeDtypeStruct((n_seq, DEPTH, RET_HEADS, RET_DK, RET_DV), F32)
        out_specs, out_shape = [o_spec, st_spec, st_spec], [o_shape, st_shape, st_shape]
    return pl.pallas_call(
        functools.partial(_ret_kernel, seq=seq, has_state=has_state, slab=slab),
        grid=(RET_HEADS // 2, n_tok // rows),
        in_specs=in_specs,
        out_specs=out_specs,
        out_shape=out_shape,
        input_output_aliases=aliases,
        compiler_params=_cparams(("parallel", "parallel")),
        name="retention_lat" if has_state else "retention_ctx",
    )(*args)


FOURIER_ROWS = 2048


def _four_kernel(u0_ref, u1_ref, wc_ref, csn_ref, o_ref):
    gd = FOURIER_GROUP_DIM
    tc, ts = [], []
    for g in range(FOURIER_GROUPS):
        u_ref = (u0_ref, u1_ref)[g // 2]
        u_g = u_ref[:, (g % 2) * gd:(g % 2 + 1) * gd].astype(BF16)
        t = _dot(u_g, wc_ref[...])
        tc.append(t[:, :gd])
        ts.append(t[:, gd:])
    tc = jnp.concatenate(tc, axis=1).astype(BF16)
    ts = jnp.concatenate(ts, axis=1).astype(BF16)
    seq = csn_ref.shape[0]
    for sq in range(o_ref.shape[0] // seq):
        rows = slice(sq * seq, (sq + 1) * seq)
        t_sq = jnp.concatenate([tc[rows], ts[rows]], axis=0)
        o_ref[rows, :] = _dot(csn_ref[...], t_sq).astype(o_ref.dtype)


def _dft_tables(n):
    k = np.arange(n, dtype=np.int64)
    ang = 2.0 * np.pi * ((k[:, None] * k[None, :]) % n).astype(np.float64) / n
    scale = 1.0 / np.sqrt(n)
    return np.cos(ang) * scale, np.sin(ang) * scale


def _fourier(p_act, seq):
    n_tok = p_act.shape[0]
    cc, sc = _dft_tables(FOURIER_GROUP_DIM)
    wc = jnp.asarray(np.concatenate([cc, -sc], axis=1), F32).astype(BF16)
    cn, sn = _dft_tables(seq)
    csn = jnp.asarray(np.concatenate([cn, sn], axis=1), F32).astype(BF16)
    const = lambda shape: pl.BlockSpec(shape, lambda s: (0, 0))
    half = FOURIER_W // 2
    rows = max(seq, FOURIER_ROWS)
    return pl.pallas_call(
        _four_kernel,
        grid=(n_tok // rows,),
        in_specs=[
            pl.BlockSpec((rows, half), lambda s: (s, OFF_UF // half)),
            pl.BlockSpec((rows, half), lambda s: (s, OFF_UF // half + 1)),
            const((FOURIER_GROUP_DIM, 2 * FOURIER_GROUP_DIM)),
            const((seq, 2 * seq)),
        ],
        out_specs=pl.BlockSpec((rows, FOURIER_W), lambda s: (s, 0)),
        out_shape=jax.ShapeDtypeStruct((n_tok, FOURIER_W), BF16),
        compiler_params=_cparams(("parallel",)),
        name="fourier",
    )(p_act, p_act, wc, csn)


MIX_TM = 1024
MERGE_STEPS = 4
MERGE_TM = MIX_TM // MERGE_STEPS
FF_TILE = 256
FF_STEPS = D_FF // FF_TILE
FF_ROWS = 256


def _mix_kernel(oa_ref, or_ref, of_ref, ga_ref, gr_ref, gf_ref, x_ref, g1_ref, sh_ref, sc_ref,
                g2_ref, n2_ref, wba_ref, wbr_ref, wbf_ref, wo_ref, wa_ref, wv_ref, cw_ref, cb_ref,
                wd_ref, o_ref, h_scr, acc_scr, wu_scr, wd_scr, *, seq):
    i = pl.program_id(0)
    j = pl.program_id(1)
    chunk = jnp.maximum(j - MERGE_STEPS, 0)
    a_cols = pl.ds(pl.multiple_of(chunk * FF_TILE, FF_TILE), FF_TILE)
    v_cols = pl.ds(pl.multiple_of((FF_STEPS + chunk) * FF_TILE, FF_TILE), FF_TILE)

    @pl.when(j < MERGE_STEPS)
    def _():
        gate = lambda ref: jnp.tanh(ref[...].astype(F32)) + 1.0
        dst = pl.ds(pl.multiple_of(j * MERGE_TM, MERGE_TM), MERGE_TM)
        merged = (gate(ga_ref) * _dot(oa_ref[...], wba_ref[...])
                  + gate(gr_ref) * _dot(or_ref[...], wbr_ref[...])
                  + gate(gf_ref) * _dot(of_ref[...], wbf_ref[...]))
        x1 = x_ref[...] + g1_ref[0] * _dot((0.5 * merged).astype(BF16), wo_ref[...])
        o_ref[dst, :] = x1
        r = lax.rsqrt(jnp.mean(x1 * x1, axis=-1, keepdims=True) + EPS)
        h = (x1 * r) * n2_ref[...] * (1.0 + sc_ref[0]) + sh_ref[0]
        h_scr[dst, :] = h.astype(BF16)
        acc_scr[dst, :] = jnp.zeros((MERGE_TM, D_MODEL), F32)

    @pl.when(jnp.logical_and(i == 0, j >= MERGE_STEPS))
    def _():
        wu_scr[:, a_cols] = wa_ref[0].astype(BF16)
        wu_scr[:, v_cols] = wv_ref[0].astype(BF16)
        wd_scr[a_cols, :] = wd_ref[0].astype(BF16)

    def mlp(last):
        blocks = [slice(b * FF_ROWS, (b + 1) * FF_ROWS) for b in range(MIX_TM // FF_ROWS)]
        wa, wv, wd = wu_scr[:, a_cols], wu_scr[:, v_cols], wd_scr[a_cols, :]
        a = jnp.concatenate([_dot(h_scr[rows, :], wa) for rows in blocks], axis=0)
        val = [_dot(h_scr[rows, :], wv) for rows in blocks]
        pos = lax.broadcasted_iota(jnp.int32, a.shape, 0) % seq
        prev = jnp.where(pos == 0, 0.0, pltpu.roll(a, 1, 0))
        nxt = jnp.where(pos == seq - 1, 0.0, pltpu.roll(a, MIX_TM - 1, 0))
        cw = cw_ref[...]
        ac = prev * cw[0:1] + a * cw[1:2] + nxt * cw[2:3] + cb_ref[...]
        for b, rows in enumerate(blocks):
            act = jax.nn.gelu(ac[rows]) * val[b]
            down = _dot(act.astype(BF16), wd)
            if last:
                o_ref[rows, :] = o_ref[rows, :] + g2_ref[0] * (acc_scr[rows, :] + down)
            else:
                acc_scr[rows, :] += down

    last_step = pl.num_programs(1) - 1

    @pl.when(jnp.logical_and(j >= MERGE_STEPS, j < last_step))
    def _():
        mlp(last=False)

    @pl.when(j == last_step)
    def _():
        mlp(last=True)


def _mix(o_att, o_ret, o_four, p_gate, x2d, seq, mod, rows_per_mod, w_att, w_ret, w_four, w_out,
         norm2, w_up, conv_w, conv_b, w_down, layer):
    n_tok = x2d.shape[0]
    mod_idx = lambda i: (i * MIX_TM) // rows_per_mod
    sub = lambda i, j: i * MERGE_STEPS + jnp.minimum(j, MERGE_STEPS - 1)
    ff = lambda j: jnp.maximum(j - MERGE_STEPS, 0)
    ff_w = lambda i, j: jnp.where(i == 0, ff(j), FF_STEPS - 1)
    br = lambda: pl.BlockSpec((MERGE_TM, 512), lambda i, j: (sub(i, j), 0))
    gate = lambda k: pl.BlockSpec((MERGE_TM, D_MODEL), lambda i, j: (sub(i, j), k))
    modv = lambda k: pl.BlockSpec((1, 1, D_MODEL), lambda i, j: (mod_idx(i), 0, k))
    const = lambda shape: pl.BlockSpec(shape, lambda i, j: (0, 0))
    return pl.pallas_call(
        functools.partial(_mix_kernel, seq=seq),
        grid=(n_tok // MIX_TM, MERGE_STEPS + FF_STEPS),
        in_specs=[
            br(), br(), br(), gate(0), gate(1), gate(2),
            pl.BlockSpec((MERGE_TM, D_MODEL), lambda i, j: (sub(i, j), 0)),
            modv(2), modv(3), modv(4), modv(5),
            const((1, D_MODEL)),
            const((512, D_MODEL)), const((512, D_MODEL)), const((512, D_MODEL)),
            const((D_MODEL, D_MODEL)),
            pl.BlockSpec((1, D_MODEL, FF_TILE), lambda i, j: (layer, 0, ff_w(i, j))),
            pl.BlockSpec((1, D_MODEL, FF_TILE), lambda i, j: (layer, 0, FF_STEPS + ff_w(i, j))),
            pl.BlockSpec((3, FF_TILE), lambda i, j: (0, ff(j))),
            pl.BlockSpec((1, FF_TILE), lambda i, j: (0, ff(j))),
            pl.BlockSpec((1, FF_TILE, D_MODEL), lambda i, j: (layer, ff_w(i, j), 0)),
        ],
        out_specs=pl.BlockSpec((MIX_TM, D_MODEL), lambda i, j: (i, 0)),
        out_shape=jax.ShapeDtypeStruct((n_tok, D_MODEL), F32),
        scratch_shapes=[pltpu.VMEM((MIX_TM, D_MODEL), BF16), pltpu.VMEM((MIX_TM, D_MODEL), F32),
                        pltpu.VMEM((D_MODEL, 2 * D_FF), BF16), pltpu.VMEM((D_FF, D_MODEL), BF16)],
        compiler_params=_cparams(("arbitrary", "arbitrary")),
        name="mix",
    )(o_att, o_ret, o_four, p_gate, p_gate, p_gate, x2d, mod, mod, mod, mod, norm2,
      w_att, w_ret, w_four, w_out, w_up, w_up, conv_w, conv_b, w_down)


def _rope_tables(n_tok):
    rows = n_tok // GRID_W
    row_id = jnp.repeat(jnp.arange(rows), GRID_W).astype(F32)
    col_id = jnp.tile(jnp.arange(GRID_W), rows).astype(F32)
    n_freq = HEAD_DIM // 4
    inv = ROPE_THETA ** (-jnp.arange(n_freq, dtype=F32) / n_freq)
    ang = jnp.concatenate([row_id[None, :] * inv[:, None], col_id[None, :] * inv[:, None]], axis=0)
    return jnp.cos(ang), jnp.sin(ang)


def kernel(x_prompt, x_sample, cache_k, cache_v, state_ret_fwd, state_ret_bwd, c, c_ctx, w_ada, b_ada, norm1, w_in, q_norm, k_norm, ret_decay_f, ret_decay_b, ret_norm, w_br_att, w_br_ret, w_br_four, w_out, norm2, w_up, conv_w, conv_b, w_down):
    batch, seq, _ = x_prompt.shape
    dec_batch, dec_seq, _ = x_sample.shape
    past = cache_k.shape[2]
    for n_seq, length in ((batch, seq), (dec_batch, dec_seq)):
        assert length % CHUNK == 0 and MIX_TM % length == 0 and (n_seq * length) % MIX_TM == 0
    assert dec_seq % Q_BLOCK == 0 and dec_seq % GRID_W == 0 and past % CHUNK == 0

    cond_all = jnp.concatenate([c_ctx[None, :], c], axis=0)
    mod_all = _ada(cond_all, w_ada, b_ada)
    cos_t, sin_t = _rope_tables(dec_seq)
    ck = cache_k.reshape(dec_batch, DEPTH, past, ATT_KV)
    cv = cache_v.reshape(dec_batch, DEPTH, past, ATT_KV)

    xp = x_prompt.reshape(batch * seq, D_MODEL)
    xs = x_sample.reshape(dec_batch * dec_seq, D_MODEL)
    new_kv = new_st = None
    for l in range(DEPTH):
        w_att, w_ret, w_four = (w_br_att[l].astype(BF16), w_br_ret[l].astype(BF16),
                                w_br_four[l].astype(BF16))
        w_o = w_out[l].astype(BF16)
        n1, n2 = norm1[l][None, :], norm2[l][None, :]
        qn, kn = q_norm[l][:, None], k_norm[l][:, None]
        rn = ret_norm[l][None, :]
        dec = _ret_tables(jnp.broadcast_to(
            jnp.concatenate([ret_decay_f[l], ret_decay_b[l]])[:, None].astype(F32), (2 * RET_HEADS, 128)))
        cw, cb = conv_w[l], conv_b[l][None, :]
        mod_ctx = mod_all[l, 0:1].reshape(1, 1, 6 * D_MODEL)
        mod_lat = mod_all[l, 1:].reshape(dec_batch, 1, 6 * D_MODEL)

        p_act, p_gate = _in_proj(xp, mod_ctx, batch * seq, n1, w_in, l)
        o_att, *new_kv = _att_ctx(p_act, seq, qn, kn, l, new_kv)
        o_ret, *new_st = _retention(p_act, seq, dec, rn, l, new_states=new_st)
        o_four = _fourier(p_act, seq)
        xp = _mix(o_att, o_ret, o_four, p_gate, xp, seq, mod_ctx, batch * seq,
                  w_att, w_ret, w_four, w_o, n2, w_up, cw, cb, w_down, l)

        p_act, p_gate = _in_proj(xs, mod_lat, dec_seq, n1, w_in, l)
        o_att = _att_lat(p_act, dec_seq, ck, cv, l, qn, kn, cos_t, sin_t)
        o_ret = _retention(p_act, dec_seq, dec, rn, l, states=(state_ret_fwd, state_ret_bwd))
        o_four = _fourier(p_act, dec_seq)
        xs = _mix(o_att, o_ret, o_four, p_gate, xs, dec_seq, mod_lat, dec_seq,
                  w_att, w_ret, w_four, w_o, n2, w_up, cw, cb, w_down, l)

    kv_shape = (batch, DEPTH, seq, N_KV_HEADS, HEAD_DIM)
    return (xp.reshape(batch, seq, D_MODEL), xs.reshape(dec_batch, dec_seq, D_MODEL),
            new_kv[0].reshape(kv_shape), new_kv[1].reshape(kv_shape), new_st[0], new_st[1])
```

```python
import functools

import numpy as np
import jax
import jax.numpy as jnp
from jax import lax
from jax.experimental import pallas as pl
from jax.experimental.pallas import tpu as pltpu

D_MODEL = 1024
DEPTH = 2
GRID_W = 64
HEAD_DIM = 64
N_HEADS = 8
N_KV_HEADS = 2
Q_PER_KV = N_HEADS // N_KV_HEADS
ATT_Q = N_HEADS * HEAD_DIM
ATT_KV = N_KV_HEADS * HEAD_DIM
RET_HEADS = 4
RET_DK = 64
RET_DV = 128
RET_V = RET_HEADS * RET_DV
FOURIER_GROUPS = 4
FOURIER_GROUP_DIM = 128
FOURIER_W = FOURIER_GROUPS * FOURIER_GROUP_DIM
D_FF = 2816
CHUNK = 128
Q_BLOCK = 256
ROPE_THETA = 10000.0
EPS = 1e-6
LOG2_E = 1.4426950408889634

F32 = jnp.float32
BF16 = jnp.bfloat16

OFF_QA, OFF_KA, OFF_VA = 0, 512, 640
OFF_QR, OFF_KR, OFF_VR, OFF_GR, OFF_UF = 768, 1024, 1280, 1792, 2304
W_IN_SPLIT = 2816
OFF_GATE = 3072
P_W = OFF_GATE + 3 * D_MODEL

VMEM_LIMIT = 56 * 1024 * 1024


def _cparams(sem):
    return pltpu.CompilerParams(dimension_semantics=sem, vmem_limit_bytes=VMEM_LIMIT)


def _dot(a, b):
    return jnp.dot(a, b, preferred_element_type=F32)


def _ada_kernel(cond_ref, w_ref, b_ref, o_ref):
    cnd = cond_ref[...]
    s = cnd * jax.nn.sigmoid(cnd)
    o_ref[0] = _dot(s.astype(BF16), w_ref[0].astype(BF16)) + b_ref[0]


def _ada(cond_all, w_ada, b_ada):
    n = cond_all.shape[0]
    tn = 1024
    return pl.pallas_call(
        _ada_kernel,
        grid=(DEPTH, 6 * D_MODEL // tn),
        in_specs=[
            pl.BlockSpec((n, D_MODEL), lambda l, j: (0, 0)),
            pl.BlockSpec((1, D_MODEL, tn), lambda l, j: (l, 0, j)),
            pl.BlockSpec((1, 1, tn), lambda l, j: (l, 0, j)),
        ],
        out_specs=pl.BlockSpec((1, n, tn), lambda l, j: (l, 0, j)),
        out_shape=jax.ShapeDtypeStruct((DEPTH, n, 6 * D_MODEL), F32),
        compiler_params=_cparams(("parallel", "parallel")),
        name="ada",
    )(cond_all, w_ada, b_ada.reshape(DEPTH, 1, 6 * D_MODEL))


IN_TN = 1536
IN_ROWS = 256
MAIN_STEPS = OFF_GATE // IN_TN


def _in_kernel(x_ref, sh_ref, sc_ref, g_ref, w_ref, om_ref, og_ref, h_scr, w_scr):
    i = pl.program_id(0)
    j = pl.program_id(1)
    cols = pl.ds(pl.multiple_of(j * IN_TN, IN_TN), IN_TN)

    @pl.when(i == 0)
    def _():
        w_scr[:, cols] = w_ref[0].astype(BF16)

    @pl.when(j == 0)
    def _():
        for b in range(x_ref.shape[0] // IN_ROWS):
            rows = slice(b * IN_ROWS, (b + 1) * IN_ROWS)
            x = x_ref[rows, :]
            r = lax.rsqrt(jnp.mean(x * x, axis=-1, keepdims=True) + EPS)
            h = ((x * r) * g_ref[...] * (1.0 + sc_ref[0]) + sh_ref[0]).astype(BF16)
            h_scr[rows, :] = h
            om_ref[rows, :] = _dot(h, w_scr[:, 0:IN_TN])

    @pl.when(jnp.logical_and(j > 0, j < MAIN_STEPS))
    def _():
        om_ref[...] = _dot(h_scr[...], w_scr[:, cols])

    @pl.when(j >= MAIN_STEPS)
    def _():
        og_ref[...] = (0.5 * _dot(h_scr[...], w_scr[:, cols])).astype(og_ref.dtype)


def _in_proj(x2d, mod, rows_per_mod, norm1, w_in, layer):
    n_tok = x2d.shape[0]
    tm, tn = 1024, IN_TN
    n_col = P_W // tn
    mod_idx = lambda i: (i * tm) // rows_per_mod

    def w_col(i, j):
        jj = jnp.where(i == 0, j, n_col - 1)
        col = jnp.where(jj < MAIN_STEPS, jj * tn, W_IN_SPLIT + (jj - MAIN_STEPS) * tn)
        return pl.multiple_of(col, 128)

    return pl.pallas_call(
        _in_kernel,
        grid=(n_tok // tm, n_col),
        in_specs=[
            pl.BlockSpec((tm, D_MODEL), lambda i, j: (i, 0)),
            pl.BlockSpec((1, 1, D_MODEL), lambda i, j: (mod_idx(i), 0, 0)),
            pl.BlockSpec((1, 1, D_MODEL), lambda i, j: (mod_idx(i), 0, 1)),
            pl.BlockSpec((1, D_MODEL), lambda i, j: (0, 0)),
            pl.BlockSpec((pl.Element(1), pl.Element(D_MODEL), pl.Element(tn)),
                         lambda i, j: (layer, 0, w_col(i, j))),
        ],
        out_specs=[
            pl.BlockSpec((tm, tn), lambda i, j: (i, jnp.minimum(j, MAIN_STEPS - 1))),
            pl.BlockSpec((tm, tn), lambda i, j: (i, jnp.maximum(j - MAIN_STEPS, 0))),
        ],
        out_shape=[jax.ShapeDtypeStruct((n_tok, OFF_GATE), F32),
                   jax.ShapeDtypeStruct((n_tok, P_W - OFF_GATE), BF16)],
        scratch_shapes=[pltpu.VMEM((tm, D_MODEL), BF16), pltpu.VMEM((D_MODEL, P_W), BF16)],
        compiler_params=_cparams(("arbitrary", "arbitrary")),
        name="in_proj",
    )(x2d, mod, mod, norm1, w_in)


def _norm_rope_heads_t(x_t, n_heads, g_col, cos_t=None, sin_t=None):
    quarter = HEAD_DIM // 4
    outs = []
    for h in range(n_heads):
        x = x_t[h * HEAD_DIM:(h + 1) * HEAD_DIM, :]
        r = lax.rsqrt(jnp.mean(x * x, axis=0, keepdims=True) + EPS)
        y = (x * r) * g_col
        if cos_t is not None:
            pieces = []
            for a in range(2):
                c = cos_t[a * quarter:(a + 1) * quarter]
                s = sin_t[a * quarter:(a + 1) * quarter]
                x1 = y[2 * a * quarter:(2 * a + 1) * quarter]
                x2 = y[(2 * a + 1) * quarter:(2 * a + 2) * quarter]
                pieces += [x1 * c - x2 * s, x2 * c + x1 * s]
            y = jnp.concatenate(pieces, axis=0)
        outs.append(y)
    return outs


SHIFT_LIMIT = 40.0


def _score_bound(k_heads_t, q_gain):
    k_sq = [jnp.max(jnp.sum(k * k, axis=0, keepdims=True)) for k in k_heads_t]
    k_norm = jnp.sqrt(functools.reduce(jnp.maximum, k_sq))
    q_norm = HEAD_DIM ** 0.5 * jnp.max(jnp.abs(q_gain))
    return 1.01 * (HEAD_DIM ** -0.5 * LOG2_E) * q_norm * k_norm


def _attend_t(q_heads_t, k_bf, v_t_bf, score_bound, store):
    tq = q_heads_t[0].shape[1]
    cols = Q_PER_KV * tq
    zeros = jnp.zeros((HEAD_DIM, cols), F32)
    q_kv = [jnp.concatenate(q_heads_t[kv * Q_PER_KV:(kv + 1) * Q_PER_KV], axis=1)
            * (HEAD_DIM ** -0.5 * LOG2_E) for kv in range(N_KV_HEADS)]
    rhs = jnp.concatenate([jnp.concatenate([q_kv[0], zeros], axis=1),
                           jnp.concatenate([zeros, q_kv[1]], axis=1)], axis=0).astype(BF16)
    def finish(shift):
        s_all = _dot(k_bf, rhs)
        out_rows = []
        for kv in range(N_KV_HEADS):
            s_t = s_all[:, kv * cols:(kv + 1) * cols]
            m = jnp.max(s_t, axis=0, keepdims=True) if shift is None else shift
            e = jnp.exp2(s_t - m)
            inv = 1.0 / jnp.sum(e, axis=0, keepdims=True)
            o_t = _dot(v_t_bf[kv * HEAD_DIM:(kv + 1) * HEAD_DIM, :], e.astype(BF16)) * inv
            out_rows += [o_t[:, g * tq:(g + 1) * tq] for g in range(Q_PER_KV)]
        store(jnp.concatenate(out_rows, axis=0))

    if score_bound is None:
        finish(None)
        return

    @pl.when(score_bound <= SHIFT_LIMIT)
    def _():
        finish(score_bound)

    @pl.when(jnp.logical_not(score_bound <= SHIFT_LIMIT))
    def _():
        finish(None)


def _att_ctx_kernel(q_ref, k_ref, v_ref, qn_ref, kn_ref, *rest, slab):
    o_ref, ko_ref, vo_ref = rest[-3:]
    seq = ko_ref.shape[2]
    if len(rest) == 3:
        for other in range(ko_ref.shape[1]):
            if other != slab:
                ko_ref[:, other] = jnp.zeros_like(ko_ref[:, other])
                vo_ref[:, other] = jnp.zeros_like(vo_ref[:, other])
    for sq in range(ko_ref.shape[0]):
        rows = slice(sq * seq, (sq + 1) * seq)
        k_heads = _norm_rope_heads_t(k_ref[rows, :].T, N_KV_HEADS, kn_ref[...])
        k_n = jnp.concatenate(k_heads, axis=0).T
        ko_ref[sq, slab] = k_n
        v = v_ref[rows, :]
        vo_ref[sq, slab] = v
        q_heads = _norm_rope_heads_t(q_ref[rows, :].T, N_HEADS, qn_ref[...])

        def store(o_t, rows=rows):
            o_ref[rows, :] = o_t.T.astype(o_ref.dtype)

        _attend_t(q_heads, k_n.astype(BF16), v.T.astype(BF16), None, store)


ATT_CTX_ROWS = 1024


def _att_ctx(p_act, seq, q_norm, k_norm, layer, caches=None):
    n_tok = p_act.shape[0]
    batch = n_tok // seq
    rows = max(seq, ATT_CTX_ROWS)
    blk = rows // seq
    in_specs = [
        pl.BlockSpec((rows, ATT_Q), lambda b: (b, OFF_QA // ATT_Q)),
        pl.BlockSpec((rows, ATT_KV), lambda b: (b, OFF_KA // ATT_KV)),
        pl.BlockSpec((rows, ATT_KV), lambda b: (b, OFF_VA // ATT_KV)),
        pl.BlockSpec((HEAD_DIM, 1), lambda b: (0, 0)),
        pl.BlockSpec((HEAD_DIM, 1), lambda b: (0, 0)),
    ]
    args = [p_act, p_act, p_act, q_norm, k_norm]
    if caches is None:
        cache_spec = pl.BlockSpec((blk, DEPTH, seq, ATT_KV), lambda b: (b, 0, 0, 0))
        slab, aliases = layer, {}
    else:
        cache_spec = pl.BlockSpec((blk, 1, seq, ATT_KV), lambda b: (b, layer, 0, 0))
        slab, aliases = 0, {len(args): 1, len(args) + 1: 2}
        in_specs += [pl.BlockSpec(memory_space=pl.ANY)] * 2
        args += list(caches)
    cache_shape = jax.ShapeDtypeStruct((batch, DEPTH, seq, ATT_KV), F32)
    return pl.pallas_call(
        functools.partial(_att_ctx_kernel, slab=slab),
        grid=(n_tok // rows,),
        in_specs=in_specs,
        out_specs=[pl.BlockSpec((rows, ATT_Q), lambda b: (b, 0)), cache_spec, cache_spec],
        out_shape=[jax.ShapeDtypeStruct((n_tok, ATT_Q), BF16), cache_shape, cache_shape],
        input_output_aliases=aliases,
        compiler_params=_cparams(("parallel",)),
        name="att_ctx",
    )(*args)


def _att_lat_kernel(q_ref, k_ref, v_ref, ck_ref, cv_ref, qn_ref, kn_ref,
                    cq_ref, sq_ref, ck_t_ref, sk_t_ref, o_ref, kf_scr, vt_scr, bound_scr, *, seq):
    @pl.when(pl.program_id(1) == 0)
    def _():
        k_heads = _norm_rope_heads_t(k_ref[...].T, N_KV_HEADS, kn_ref[...], ck_t_ref[...], sk_t_ref[...])
        kf_scr[0:seq, :] = jnp.concatenate(k_heads, axis=0).T.astype(BF16)
        past_k = ck_ref[0, 0]
        kf_scr[seq:, :] = past_k.astype(BF16)
        vt_scr[:, 0:seq] = v_ref[...].T.astype(BF16)
        vt_scr[:, seq:] = cv_ref[0, 0].T.astype(BF16)
        past_t = past_k.T
        past_heads = [past_t[kv * HEAD_DIM:(kv + 1) * HEAD_DIM] for kv in range(N_KV_HEADS)]
        bound_scr[0] = _score_bound(k_heads + past_heads, qn_ref[...])

    q_heads = _norm_rope_heads_t(q_ref[...].T, N_HEADS, qn_ref[...], cq_ref[...], sq_ref[...])

    def store(o_t):
        o_ref[...] = o_t.T.astype(o_ref.dtype)

    _attend_t(q_heads, kf_scr[...], vt_scr[...], bound_scr[0], store)


def _att_lat(p_act, seq, cache_k, cache_v, layer, q_norm, k_norm, cos_t, sin_t):
    n_tok = p_act.shape[0]
    nb = seq // Q_BLOCK
    past = cache_k.shape[2]
    return pl.pallas_call(
        functools.partial(_att_lat_kernel, seq=seq),
        grid=(n_tok // seq, nb),
        in_specs=[
            pl.BlockSpec((Q_BLOCK, ATT_Q), lambda b, i: (b * nb + i, OFF_QA // ATT_Q)),
            pl.BlockSpec((seq, ATT_KV), lambda b, i: (b, OFF_KA // ATT_KV)),
            pl.BlockSpec((seq, ATT_KV), lambda b, i: (b, OFF_VA // ATT_KV)),
            pl.BlockSpec((1, 1, past, ATT_KV), lambda b, i: (b, layer, 0, 0)),
            pl.BlockSpec((1, 1, past, ATT_KV), lambda b, i: (b, layer, 0, 0)),
            pl.BlockSpec((HEAD_DIM, 1), lambda b, i: (0, 0)),
            pl.BlockSpec((HEAD_DIM, 1), lambda b, i: (0, 0)),
            pl.BlockSpec((HEAD_DIM // 2, Q_BLOCK), lambda b, i: (0, i)),
            pl.BlockSpec((HEAD_DIM // 2, Q_BLOCK), lambda b, i: (0, i)),
            pl.BlockSpec((HEAD_DIM // 2, seq), lambda b, i: (0, 0)),
            pl.BlockSpec((HEAD_DIM // 2, seq), lambda b, i: (0, 0)),
        ],
        out_specs=pl.BlockSpec((Q_BLOCK, ATT_Q), lambda b, i: (b * nb + i, 0)),
        out_shape=jax.ShapeDtypeStruct((n_tok, ATT_Q), BF16),
        scratch_shapes=[pltpu.VMEM((seq + past, ATT_KV), BF16),
                        pltpu.VMEM((ATT_KV, seq + past), BF16),
                        pltpu.SMEM((1,), F32)],
        compiler_params=_cparams(("parallel", "arbitrary")),
        name="att_lat",
    )(p_act, p_act, p_act, cache_k, cache_v, q_norm, k_norm, cos_t, sin_t, cos_t, sin_t)


RET_ROWS = 2048


def _log_sigmoid(d):
    return jnp.minimum(d, 0.0) - jnp.log1p(jnp.exp(-jnp.abs(d)))


TAB_MASK, TAB_QF, TAB_QB, TAB_KF, TAB_KB, TAB_C, N_TAB = 0, 2, 4, 6, 7, 8, 9
TAB_CF, TAB_CB = 0, 2


def _ret_tables_kernel(dec_ref, tab_ref):
    hp = pl.program_id(0)
    ii = lax.broadcasted_iota(jnp.int32, (CHUNK, CHUNK), 0)
    jj = lax.broadcasted_iota(jnp.int32, (CHUNK, CHUNK), 1)
    rel = (ii - jj).astype(F32)
    row = ii.astype(F32)
    lane = jj.astype(F32)
    lgf = [_log_sigmoid(dec_ref[pl.ds(2 * hp + t, 1), :]) for t in range(2)]
    lgb = [_log_sigmoid(dec_ref[pl.ds(RET_HEADS + 2 * hp + t, 1), :]) for t in range(2)]
    for t in range(2):
        tab_ref[0, TAB_MASK + t] = jnp.where(
            rel > 0, jnp.exp(jnp.maximum(rel, 0.0) * lgf[t]),
            jnp.where(rel < 0, jnp.exp(jnp.maximum(-rel, 0.0) * lgb[t]), 2.0))
        tab_ref[0, TAB_QF + t] = jnp.exp((row + 1.0) * lgf[t])
        tab_ref[0, TAB_QB + t] = jnp.exp((CHUNK - row) * lgb[t])
    tab_ref[0, TAB_KF] = jnp.exp((CHUNK - 1.0 - lane) * jnp.where(ii < RET_DK, lgf[0], lgf[1]))
    tab_ref[0, TAB_KB] = jnp.exp(lane * jnp.where(ii < RET_DK, lgb[0], lgb[1]))
    c_rows = jnp.where(ii == TAB_CF, lgf[0], jnp.where(ii == TAB_CF + 1, lgf[1],
                       jnp.where(ii == TAB_CB, lgb[0], lgb[1])))
    tab_ref[0, TAB_C] = jnp.exp(CHUNK * c_rows)


def _ret_tables(dec):
    return pl.pallas_call(
        _ret_tables_kernel,
        grid=(RET_HEADS // 2,),
        in_specs=[pl.BlockSpec((8, 128), lambda p: (0, 0))],
        out_specs=pl.BlockSpec((1, N_TAB, CHUNK, CHUNK), lambda p: (p, 0, 0, 0)),
        out_shape=jax.ShapeDtypeStruct((RET_HEADS // 2, N_TAB, CHUNK, CHUNK), F32),
        compiler_params=_cparams(("parallel",)),
        name="ret_tables",
    )(dec)


def _ret_kernel(q_ref, k_ref, v_ref, g_ref, tab_ref, rn_ref, *rest, seq, has_state, slab):
    if has_state:
        s0f_ref, s0b_ref, o_ref = rest
    else:
        o_ref, sf_ref, sb_ref = rest[-3:]
        if len(rest) == 3:
            for other in range(sf_ref.shape[1]):
                if other != slab:
                    sf_ref[:, other] = jnp.zeros_like(sf_ref[:, other])
                    sb_ref[:, other] = jnp.zeros_like(sb_ref[:, other])
    n_chunks = q_ref.shape[0] // CHUNK
    jj = lax.broadcasted_iota(jnp.int32, (CHUNK, CHUNK), 1)
    mask2 = jnp.concatenate([tab_ref[0, TAB_MASK], tab_ref[0, TAB_MASK + 1]], axis=0)
    qdec_f = [tab_ref[0, TAB_QF + t] for t in range(2)]
    qdec_b = [tab_ref[0, TAB_QB + t] for t in range(2)]
    cdec_f = [tab_ref[0, TAB_C, TAB_CF + t:TAB_CF + t + 1, :] for t in range(2)]
    cdec_b = [tab_ref[0, TAB_C, TAB_CB + t:TAB_CB + t + 1, :] for t in range(2)]
    kdec_f = tab_ref[0, TAB_KF]
    kdec_b = tab_ref[0, TAB_KB]

    k_t = (k_ref[...] * (RET_DK ** -0.5)).T
    first_head = jj < RET_DK

    def chunk(c):
        return slice(c * CHUNK, (c + 1) * CHUNK)

    kv_f = [[None] * n_chunks for _ in range(2)]
    kv_b = [[None] * n_chunks for _ in range(2)]
    for c in range(n_chunks):
        k_c = k_t[:, chunk(c)]
        kd = jnp.concatenate([k_c * kdec_f, k_c * kdec_b], axis=0).astype(BF16)
        kv = _dot(kd, v_ref[chunk(c), :].astype(BF16))
        for t in range(2):
            kv_f[t][c] = kv[t * RET_DK:(t + 1) * RET_DK, t * RET_DV:(t + 1) * RET_DV]
            kv_b[t][c] = kv[CHUNK + t * RET_DK:CHUNK + (t + 1) * RET_DK, t * RET_DV:(t + 1) * RET_DV]

    st_f = [[None] * n_chunks for _ in range(2)]
    st_b = [[None] * n_chunks for _ in range(2)]
    per_seq = seq // CHUNK
    for t in range(2):
        for sq in range(n_chunks // per_seq):
            own = range(sq * per_seq, (sq + 1) * per_seq)
            if has_state:
                s_f = s0f_ref[sq, 0, t]
                s_b = s0b_ref[sq, 0, t]
            else:
                s_f = jnp.zeros((RET_DK, RET_DV), F32)
                s_b = s_f
            for c in own:
                st_f[t][c] = s_f
                s_f = s_f * cdec_f[t] + kv_f[t][c]
            for c in reversed(own):
                st_b[t][c] = s_b
                s_b = s_b * cdec_b[t] + kv_b[t][c]
            if not has_state:
                sf_ref[sq, slab, t] = s_f
                sb_ref[sq, slab, t] = s_b

    for c in range(n_chunks):
        q_c = q_ref[chunk(c), :]
        qm = jnp.concatenate([jnp.where(first_head, q_c, 0.0), jnp.where(first_head, 0.0, q_c)],
                             axis=0).astype(BF16)
        att = (_dot(qm, k_t[:, chunk(c)].astype(BF16)) * mask2).astype(BF16)
        states = jnp.concatenate(
            [jnp.concatenate([st_f[t][c], st_b[t][c]], axis=1) for t in range(2)], axis=0)
        qs = _dot(qm, states.astype(BF16))
        v_c = v_ref[chunk(c), :].astype(BF16)
        for t in range(2):
            rows = slice(t * CHUNK, (t + 1) * CHUNK)
            vsl = slice(t * RET_DV, (t + 1) * RET_DV)
            o = (_dot(att[rows], v_c[:, vsl]) + qs[rows, :RET_DV] * qdec_f[t]
                 + qs[rows, RET_DV:] * qdec_b[t])
            o = (o * lax.rsqrt(jnp.mean(o * o, axis=-1, keepdims=True) + EPS)) * rn_ref[...]
            hg = 0.5 * g_ref[chunk(c), vsl]
            o_ref[chunk(c), vsl] = (o * (hg * (1.0 + jnp.tanh(hg)))).astype(o_ref.dtype)


def _retention(p_act, seq, tables, ret_norm, layer, states=None, new_states=None):
    n_tok = p_act.shape[0]
    n_seq = n_tok // seq
    has_state = states is not None
    pair_qk, pair_v = 2 * RET_DK, 2 * RET_DV
    rows = max(seq, RET_ROWS)
    blk_seqs = rows // seq
    in_specs = [
        pl.BlockSpec((rows, pair_qk), lambda p, s: (s, OFF_QR // pair_qk + p)),
        pl.BlockSpec((rows, pair_qk), lambda p, s: (s, OFF_KR // pair_qk + p)),
        pl.BlockSpec((rows, pair_v), lambda p, s: (s, OFF_VR // pair_v + p)),
        pl.BlockSpec((rows, pair_v), lambda p, s: (s, OFF_GR // pair_v + p)),
        pl.BlockSpec((1, N_TAB, CHUNK, CHUNK), lambda p, s: (p, 0, 0, 0)),
        pl.BlockSpec((1, RET_DV), lambda p, s: (0, 0)),
    ]
    args = [p_act, p_act, p_act, p_act, tables, ret_norm]
    o_spec = pl.BlockSpec((rows, pair_v), lambda p, s: (s, p))
    o_shape = jax.ShapeDtypeStruct((n_tok, RET_V), BF16)
    st_spec = pl.BlockSpec((blk_seqs, 1, 2, RET_DK, RET_DV), lambda p, s: (s, layer, p, 0, 0))
    aliases, slab = {}, 0
    if has_state:
        in_specs += [st_spec, st_spec]
        args += list(states)
        out_specs, out_shape = o_spec, o_shape
    else:
        if new_states is None:
            st_spec = pl.BlockSpec((blk_seqs, DEPTH, 2, RET_DK, RET_DV), lambda p, s: (s, 0, p, 0, 0))
            slab = layer
        else:
            in_specs += [pl.BlockSpec(memory_space=pl.ANY)] * 2
            aliases = {len(args): 1, len(args) + 1: 2}
            args += list(new_states)
        st_shape = jax.ShapeDtypeStruct((n_seq, DEPTH, RET_HEADS, RET_DK, RET_DV), F32)
        out_specs, out_shape = [o_spec, st_spec, st_spec], [o_shape, st_shape, st_shape]
    return pl.pallas_call(
        functools.partial(_ret_kernel, seq=seq, has_state=has_state, slab=slab),
        grid=(RET_HEADS // 2, n_tok // rows),
        in_specs=in_specs,
        out_specs=out_specs,
        out_shape=out_shape,
        input_output_aliases=aliases,
        compiler_params=_cparams(("parallel", "parallel")),
        name="retention_lat" if has_state else "retention_ctx",
    )(*args)


FOURIER_ROWS = 2048


def _four_kernel(u0_ref, u1_ref, wc_ref, csn_ref, o_ref):
    gd = FOURIER_GROUP_DIM
    tc, ts = [], []
    for g in range(FOURIER_GROUPS):
        u_ref = (u0_ref, u1_ref)[g // 2]
        u_g = u_ref[:, (g % 2) * gd:(g % 2 + 1) * gd].astype(BF16)
        t = _dot(u_g, wc_ref[...])
        tc.append(t[:, :gd])
        ts.append(t[:, gd:])
    tc = jnp.concatenate(tc, axis=1).astype(BF16)
    ts = jnp.concatenate(ts, axis=1).astype(BF16)
    seq = csn_ref.shape[0]
    for sq in range(o_ref.shape[0] // seq):
        rows = slice(sq * seq, (sq + 1) * seq)
        t_sq = jnp.concatenate([tc[rows], ts[rows]], axis=0)
        o_ref[rows, :] = _dot(csn_ref[...], t_sq).astype(o_ref.dtype)


def _dft_tables(n):
    k = np.arange(n, dtype=np.int64)
    ang = 2.0 * np.pi * ((k[:, None] * k[None, :]) % n).astype(np.float64) / n
    scale = 1.0 / np.sqrt(n)
    return np.cos(ang) * scale, np.sin(ang) * scale


def _fourier(p_act, seq):
    n_tok = p_act.shape[0]
    cc, sc = _dft_tables(FOURIER_GROUP_DIM)
    wc = jnp.asarray(np.concatenate([cc, -sc], axis=1), F32).astype(BF16)
    cn, sn = _dft_tables(seq)
    csn = jnp.asarray(np.concatenate([cn, sn], axis=1), F32).astype(BF16)
    const = lambda shape: pl.BlockSpec(shape, lambda s: (0, 0))
    half = FOURIER_W // 2
    rows = max(seq, FOURIER_ROWS)
    return pl.pallas_call(
        _four_kernel,
        grid=(n_tok // rows,),
        in_specs=[
            pl.BlockSpec((rows, half), lambda s: (s, OFF_UF // half)),
            pl.BlockSpec((rows, half), lambda s: (s, OFF_UF // half + 1)),
            const((FOURIER_GROUP_DIM, 2 * FOURIER_GROUP_DIM)),
            const((seq, 2 * seq)),
        ],
        out_specs=pl.BlockSpec((rows, FOURIER_W), lambda s: (s, 0)),
        out_shape=jax.ShapeDtypeStruct((n_tok, FOURIER_W), BF16),
        compiler_params=_cparams(("parallel",)),
        name="fourier",
    )(p_act, p_act, wc, csn)


MIX_TM = 1024
MERGE_STEPS = 4
MERGE_TM = MIX_TM // MERGE_STEPS
MERGE_COLS = 256
FF_TILE = 256
FF_STEPS = D_FF // FF_TILE
FF_ROWS = 256


def _mix_kernel(oa_ref, or_ref, of_ref, ga_ref, gr_ref, gf_ref, x_ref, g1_ref, sh_ref, sc_ref,
                g2_ref, n2_ref, wba_ref, wbr_ref, wbf_ref, wo_ref, wa_ref, wv_ref, cw_ref, cb_ref,
                wd_ref, o_ref, h_scr, acc_scr, wu_scr, wd_scr, *, seq):
    i = pl.program_id(0)
    j = pl.program_id(1)
    chunk = jnp.maximum(j - MERGE_STEPS, 0)
    a_cols = pl.ds(pl.multiple_of(chunk * FF_TILE, FF_TILE), FF_TILE)
    v_cols = pl.ds(pl.multiple_of((FF_STEPS + chunk) * FF_TILE, FF_TILE), FF_TILE)

    @pl.when(j < MERGE_STEPS)
    def _():
        gate = lambda ref, cols: jnp.tanh(ref[:, cols].astype(F32)) + 1.0
        dst = pl.ds(pl.multiple_of(j * MERGE_TM, MERGE_TM), MERGE_TM)
        merged = []
        for c in range(D_MODEL // MERGE_COLS):
            cols = slice(c * MERGE_COLS, (c + 1) * MERGE_COLS)
            m = (gate(ga_ref, cols) * _dot(oa_ref[...], wba_ref[:, cols])
                 + gate(gr_ref, cols) * _dot(or_ref[...], wbr_ref[:, cols])
                 + gate(gf_ref, cols) * _dot(of_ref[...], wbf_ref[:, cols]))
            merged.append((0.5 * m).astype(BF16))
        x1 = x_ref[...] + g1_ref[0] * _dot(jnp.concatenate(merged, axis=1), wo_ref[...])
        o_ref[dst, :] = x1
        r = lax.rsqrt(jnp.mean(x1 * x1, axis=-1, keepdims=True) + EPS)
        h = (x1 * r) * n2_ref[...] * (1.0 + sc_ref[0]) + sh_ref[0]
        h_scr[dst, :] = h.astype(BF16)
        acc_scr[dst, :] = jnp.zeros((MERGE_TM, D_MODEL), F32)

    @pl.when(jnp.logical_and(i == 0, j >= MERGE_STEPS))
    def _():
        wu_scr[:, a_cols] = wa_ref[0].astype(BF16)
        wu_scr[:, v_cols] = wv_ref[0].astype(BF16)
        wd_scr[a_cols, :] = wd_ref[0].astype(BF16)

    def mlp(last):
        blocks = [slice(b * FF_ROWS, (b + 1) * FF_ROWS) for b in range(MIX_TM // FF_ROWS)]
        wa, wv, wd = wu_scr[:, a_cols], wu_scr[:, v_cols], wd_scr[a_cols, :]
        a = jnp.concatenate([_dot(h_scr[rows, :], wa) for rows in blocks], axis=0)
        val = [_dot(h_scr[rows, :], wv) for rows in blocks]
        pos = lax.broadcasted_iota(jnp.int32, a.shape, 0) % seq
        prev = jnp.where(pos == 0, 0.0, pltpu.roll(a, 1, 0))
        nxt = jnp.where(pos == seq - 1, 0.0, pltpu.roll(a, MIX_TM - 1, 0))
        cw = cw_ref[...]
        ac = prev * cw[0:1] + a * cw[1:2] + nxt * cw[2:3] + cb_ref[...]
        for b, rows in enumerate(blocks):
            act = jax.nn.gelu(ac[rows]) * val[b]
            down = _dot(act.astype(BF16), wd)
            if last:
                o_ref[rows, :] = o_ref[rows, :] + g2_ref[0] * (acc_scr[rows, :] + down)
            else:
                acc_scr[rows, :] += down

    last_step = pl.num_programs(1) - 1

    @pl.when(jnp.logical_and(j >= MERGE_STEPS, j < last_step))
    def _():
        mlp(last=False)

    @pl.when(j == last_step)
    def _():
        mlp(last=True)


def _mix(o_att, o_ret, o_four, p_gate, x2d, seq, mod, rows_per_mod, w_att, w_ret, w_four, w_out,
         norm2, w_up, conv_w, conv_b, w_down, layer):
    n_tok = x2d.shape[0]
    mod_idx = lambda i: (i * MIX_TM) // rows_per_mod
    sub = lambda i, j: i * MERGE_STEPS + jnp.minimum(j, MERGE_STEPS - 1)
    ff = lambda j: jnp.maximum(j - MERGE_STEPS, 0)
    ff_w = lambda i, j: jnp.where(i == 0, ff(j), FF_STEPS - 1)
    br = lambda: pl.BlockSpec((MERGE_TM, 512), lambda i, j: (sub(i, j), 0))
    gate = lambda k: pl.BlockSpec((MERGE_TM, D_MODEL), lambda i, j: (sub(i, j), k))
    modv = lambda k: pl.BlockSpec((1, 1, D_MODEL), lambda i, j: (mod_idx(i), 0, k))
    const = lambda shape: pl.BlockSpec(shape, lambda i, j: (0, 0))
    return pl.pallas_call(
        functools.partial(_mix_kernel, seq=seq),
        grid=(n_tok // MIX_TM, MERGE_STEPS + FF_STEPS),
        in_specs=[
            br(), br(), br(), gate(0), gate(1), gate(2),
            pl.BlockSpec((MERGE_TM, D_MODEL), lambda i, j: (sub(i, j), 0)),
            modv(2), modv(3), modv(4), modv(5),
            const((1, D_MODEL)),
            const((512, D_MODEL)), const((512, D_MODEL)), const((512, D_MODEL)),
            const((D_MODEL, D_MODEL)),
            pl.BlockSpec((1, D_MODEL, FF_TILE), lambda i, j: (layer, 0, ff_w(i, j))),
            pl.BlockSpec((1, D_MODEL, FF_TILE), lambda i, j: (layer, 0, FF_STEPS + ff_w(i, j))),
            pl.BlockSpec((3, FF_TILE), lambda i, j: (0, ff(j))),
            pl.BlockSpec((1, FF_TILE), lambda i, j: (0, ff(j))),
            pl.BlockSpec((1, FF_TILE, D_MODEL), lambda i, j: (layer, ff_w(i, j), 0)),
        ],
        out_specs=pl.BlockSpec((MIX_TM, D_MODEL), lambda i, j: (i, 0)),
        out_shape=jax.ShapeDtypeStruct((n_tok, D_MODEL), F32),
        scratch_shapes=[pltpu.VMEM((MIX_TM, D_MODEL), BF16), pltpu.VMEM((MIX_TM, D_MODEL), F32),
                        pltpu.VMEM((D_MODEL, 2 * D_FF), BF16), pltpu.VMEM((D_FF, D_MODEL), BF16)],
        compiler_params=_cparams(("arbitrary", "arbitrary")),
        name="mix",
    )(o_att, o_ret, o_four, p_gate, p_gate, p_gate, x2d, mod, mod, mod, mod, norm2,
      w_att, w_ret, w_four, w_out, w_up, w_up, conv_w, conv_b, w_down)


def _rope_tables(n_tok):
    rows = n_tok // GRID_W
    row_id = jnp.repeat(jnp.arange(rows), GRID_W).astype(F32)
    col_id = jnp.tile(jnp.arange(GRID_W), rows).astype(F32)
    n_freq = HEAD_DIM // 4
    inv = ROPE_THETA ** (-jnp.arange(n_freq, dtype=F32) / n_freq)
    ang = jnp.concatenate([row_id[None, :] * inv[:, None], col_id[None, :] * inv[:, None]], axis=0)
    return jnp.cos(ang), jnp.sin(ang)


def kernel(x_prompt, x_sample, cache_k, cache_v, state_ret_fwd, state_ret_bwd, c, c_ctx, w_ada, b_ada, norm1, w_in, q_norm, k_norm, ret_decay_f, ret_decay_b, ret_norm, w_br_att, w_br_ret, w_br_four, w_out, norm2, w_up, conv_w, conv_b, w_down):
    batch, seq, _ = x_prompt.shape
    dec_batch, dec_seq, _ = x_sample.shape
    past = cache_k.shape[2]
    for n_seq, length in ((batch, seq), (dec_batch, dec_seq)):
        assert length % CHUNK == 0 and MIX_TM % length == 0 and (n_seq * length) % MIX_TM == 0
    assert dec_seq % Q_BLOCK == 0 and dec_seq % GRID_W == 0 and past % CHUNK == 0

    cond_all = jnp.concatenate([c_ctx[None, :], c], axis=0)
    mod_all = _ada(cond_all, w_ada, b_ada)
    cos_t, sin_t = _rope_tables(dec_seq)
    ck = cache_k.reshape(dec_batch, DEPTH, past, ATT_KV)
    cv = cache_v.reshape(dec_batch, DEPTH, past, ATT_KV)

    xp = x_prompt.reshape(batch * seq, D_MODEL)
    xs = x_sample.reshape(dec_batch * dec_seq, D_MODEL)
    new_kv = new_st = None
    for l in range(DEPTH):
        w_att, w_ret, w_four = (w_br_att[l].astype(BF16), w_br_ret[l].astype(BF16),
                                w_br_four[l].astype(BF16))
        w_o = w_out[l].astype(BF16)
        n1, n2 = norm1[l][None, :], norm2[l][None, :]
        qn, kn = q_norm[l][:, None], k_norm[l][:, None]
        rn = ret_norm[l][None, :]
        dec = _ret_tables(jnp.broadcast_to(
            jnp.concatenate([ret_decay_f[l], ret_decay_b[l]])[:, None].astype(F32), (2 * RET_HEADS, 128)))
        cw, cb = conv_w[l], conv_b[l][None, :]
        mod_ctx = mod_all[l, 0:1].reshape(1, 1, 6 * D_MODEL)
        mod_lat = mod_all[l, 1:].reshape(dec_batch, 1, 6 * D_MODEL)

        p_act, p_gate = _in_proj(xp, mod_ctx, batch * seq, n1, w_in, l)
        o_att, *new_kv = _att_ctx(p_act, seq, qn, kn, l, new_kv)
        o_ret, *new_st = _retention(p_act, seq, dec, rn, l, new_states=new_st)
        o_four = _fourier(p_act, seq)
        xp = _mix(o_att, o_ret, o_four, p_gate, xp, seq, mod_ctx, batch * seq,
                  w_att, w_ret, w_four, w_o, n2, w_up, cw, cb, w_down, l)

        p_act, p_gate = _in_proj(xs, mod_lat, dec_seq, n1, w_in, l)
        o_att = _att_lat(p_act, dec_seq, ck, cv, l, qn, kn, cos_t, sin_t)
        o_ret = _retention(p_act, dec_seq, dec, rn, l, states=(state_ret_fwd, state_ret_bwd))
        o_four = _fourier(p_act, dec_seq)
        xs = _mix(o_att, o_ret, o_four, p_gate, xs, dec_seq, mod_lat, dec_seq,
                  w_att, w_ret, w_four, w_o, n2, w_up, cw, cb, w_down, l)

    kv_shape = (batch, DEPTH, seq, N_KV_HEADS, HEAD_DIM)
    return (xp.reshape(batch, seq, D_MODEL), xs.reshape(dec_batch, dec_seq, D_MODEL),
            new_kv[0].reshape(kv_shape), new_kv[1].reshape(kv_shape), new_st[0], new_st[1])
```

```python
import functools

import numpy as np
import jax
import jax.numpy as jnp
from jax import lax
from jax.experimental import pallas as pl
from jax.experimental.pallas import tpu as pltpu

D_MODEL = 1024
DEPTH = 2
GRID_W = 64
HEAD_DIM = 64
N_HEADS = 8
N_KV_HEADS = 2
Q_PER_KV = N_HEADS // N_KV_HEADS
ATT_Q = N_HEADS * HEAD_DIM
ATT_KV = N_KV_HEADS * HEAD_DIM
RET_HEADS = 4
RET_DK = 64
RET_DV = 128
RET_V = RET_HEADS * RET_DV
FOURIER_GROUPS = 4
FOURIER_GROUP_DIM = 128
FOURIER_W = FOURIER_GROUPS * FOURIER_GROUP_DIM
D_FF = 2816
CHUNK = 128
Q_BLOCK = 256
ROPE_THETA = 10000.0
EPS = 1e-6
LOG2_E = 1.4426950408889634

F32 = jnp.float32
BF16 = jnp.bfloat16

OFF_QA, OFF_KA, OFF_VA = 0, 512, 640
OFF_QR, OFF_KR, OFF_VR, OFF_GR, OFF_UF = 768, 1024, 1280, 1792, 2304
W_IN_SPLIT = 2816
OFF_GATE = 3072
P_W = OFF_GATE + 3 * D_MODEL

VMEM_LIMIT = 56 * 1024 * 1024


def _cparams(sem):
    return pltpu.CompilerParams(dimension_semantics=sem, vmem_limit_bytes=VMEM_LIMIT)


def _dot(a, b):
    return jnp.dot(a, b, preferred_element_type=F32)


def _ada_kernel(cond_ref, w_ref, b_ref, o_ref):
    cnd = cond_ref[...]
    s = cnd * jax.nn.sigmoid(cnd)
    o_ref[0] = _dot(s.astype(BF16), w_ref[0].astype(BF16)) + b_ref[0]


def _ada(cond_all, w_ada, b_ada):
    n = cond_all.shape[0]
    tn = 1024
    return pl.pallas_call(
        _ada_kernel,
        grid=(DEPTH, 6 * D_MODEL // tn),
        in_specs=[
            pl.BlockSpec((n, D_MODEL), lambda l, j: (0, 0)),
            pl.BlockSpec((1, D_MODEL, tn), lambda l, j: (l, 0, j)),
            pl.BlockSpec((1, 1, tn), lambda l, j: (l, 0, j)),
        ],
        out_specs=pl.BlockSpec((1, n, tn), lambda l, j: (l, 0, j)),
        out_shape=jax.ShapeDtypeStruct((DEPTH, n, 6 * D_MODEL), F32),
        compiler_params=_cparams(("parallel", "parallel")),
        name="ada",
    )(cond_all, w_ada, b_ada.reshape(DEPTH, 1, 6 * D_MODEL))


IN_TN = 1536
IN_ROWS = 256
MAIN_STEPS = OFF_GATE // IN_TN


def _in_kernel(x_ref, sh_ref, sc_ref, g_ref, w_ref, om_ref, og_ref, h_scr, w_scr):
    i = pl.program_id(0)
    j = pl.program_id(1)
    cols = pl.ds(pl.multiple_of(j * IN_TN, IN_TN), IN_TN)

    @pl.when(i == 0)
    def _():
        w_scr[:, cols] = w_ref[0].astype(BF16)

    @pl.when(j == 0)
    def _():
        for b in range(x_ref.shape[0] // IN_ROWS):
            rows = slice(b * IN_ROWS, (b + 1) * IN_ROWS)
            x = x_ref[rows, :]
            r = lax.rsqrt(jnp.mean(x * x, axis=-1, keepdims=True) + EPS)
            h = ((x * r) * g_ref[...] * (1.0 + sc_ref[0]) + sh_ref[0]).astype(BF16)
            h_scr[rows, :] = h
            om_ref[rows, :] = _dot(h, w_scr[:, 0:IN_TN])

    @pl.when(jnp.logical_and(j > 0, j < MAIN_STEPS))
    def _():
        om_ref[...] = _dot(h_scr[...], w_scr[:, cols])

    @pl.when(j >= MAIN_STEPS)
    def _():
        og_ref[...] = (0.5 * _dot(h_scr[...], w_scr[:, cols])).astype(og_ref.dtype)


def _in_proj(x2d, mod, rows_per_mod, norm1, w_in, layer):
    n_tok = x2d.shape[0]
    tm, tn = 1024, IN_TN
    n_col = P_W // tn
    mod_idx = lambda i: (i * tm) // rows_per_mod

    def w_col(i, j):
        jj = jnp.where(i == 0, j, n_col - 1)
        col = jnp.where(jj < MAIN_STEPS, jj * tn, W_IN_SPLIT + (jj - MAIN_STEPS) * tn)
        return pl.multiple_of(col, 128)

    return pl.pallas_call(
        _in_kernel,
        grid=(n_tok // tm, n_col),
        in_specs=[
            pl.BlockSpec((tm, D_MODEL), lambda i, j: (i, 0)),
            pl.BlockSpec((1, 1, D_MODEL), lambda i, j: (mod_idx(i), 0, 0)),
            pl.BlockSpec((1, 1, D_MODEL), lambda i, j: (mod_idx(i), 0, 1)),
            pl.BlockSpec((1, D_MODEL), lambda i, j: (0, 0)),
            pl.BlockSpec((pl.Element(1), pl.Element(D_MODEL), pl.Element(tn)),
                         lambda i, j: (layer, 0, w_col(i, j))),
        ],
        out_specs=[
            pl.BlockSpec((tm, tn), lambda i, j: (i, jnp.minimum(j, MAIN_STEPS - 1))),
            pl.BlockSpec((tm, tn), lambda i, j: (i, jnp.maximum(j - MAIN_STEPS, 0))),
        ],
        out_shape=[jax.ShapeDtypeStruct((n_tok, OFF_GATE), F32),
                   jax.ShapeDtypeStruct((n_tok, P_W - OFF_GATE), BF16)],
        scratch_shapes=[pltpu.VMEM((tm, D_MODEL), BF16), pltpu.VMEM((D_MODEL, P_W), BF16)],
        compiler_params=_cparams(("arbitrary", "arbitrary")),
        name="in_proj",
    )(x2d, mod, mod, norm1, w_in)


def _norm_rope_heads_t(x_t, n_heads, g_col, cos_t=None, sin_t=None):
    quarter = HEAD_DIM // 4
    outs = []
    for h in range(n_heads):
        x = x_t[h * HEAD_DIM:(h + 1) * HEAD_DIM, :]
        r = lax.rsqrt(jnp.mean(x * x, axis=0, keepdims=True) + EPS)
        y = (x * r) * g_col
        if cos_t is not None:
            pieces = []
            for a in range(2):
                c = cos_t[a * quarter:(a + 1) * quarter]
                s = sin_t[a * quarter:(a + 1) * quarter]
                x1 = y[2 * a * quarter:(2 * a + 1) * quarter]
                x2 = y[(2 * a + 1) * quarter:(2 * a + 2) * quarter]
                pieces += [x1 * c - x2 * s, x2 * c + x1 * s]
            y = jnp.concatenate(pieces, axis=0)
        outs.append(y)
    return outs


SHIFT_LIMIT = 40.0


def _score_bound(k_heads_t, q_gain):
    k_sq = [jnp.max(jnp.sum(k * k, axis=0, keepdims=True)) for k in k_heads_t]
    k_norm = jnp.sqrt(functools.reduce(jnp.maximum, k_sq))
    q_norm = HEAD_DIM ** 0.5 * jnp.max(jnp.abs(q_gain))
    return 1.01 * (HEAD_DIM ** -0.5 * LOG2_E) * q_norm * k_norm


def _attend_t(q_heads_t, k_bf, v_t_bf, score_bound, store):
    tq = q_heads_t[0].shape[1]
    cols = Q_PER_KV * tq
    zeros = jnp.zeros((HEAD_DIM, cols), F32)
    q_kv = [jnp.concatenate(q_heads_t[kv * Q_PER_KV:(kv + 1) * Q_PER_KV], axis=1)
            * (HEAD_DIM ** -0.5 * LOG2_E) for kv in range(N_KV_HEADS)]
    rhs = jnp.concatenate([jnp.concatenate([q_kv[0], zeros], axis=1),
                           jnp.concatenate([zeros, q_kv[1]], axis=1)], axis=0).astype(BF16)
    def finish(shift):
        s_all = _dot(k_bf, rhs)
        out_rows = []
        for kv in range(N_KV_HEADS):
            s_t = s_all[:, kv * cols:(kv + 1) * cols]
            m = jnp.max(s_t, axis=0, keepdims=True) if shift is None else shift
            e = jnp.exp2(s_t - m)
            inv = 1.0 / jnp.sum(e, axis=0, keepdims=True)
            o_t = _dot(v_t_bf[kv * HEAD_DIM:(kv + 1) * HEAD_DIM, :], e.astype(BF16)) * inv
            out_rows += [o_t[:, g * tq:(g + 1) * tq] for g in range(Q_PER_KV)]
        store(jnp.concatenate(out_rows, axis=0))

    if score_bound is None:
        finish(None)
        return

    @pl.when(score_bound <= SHIFT_LIMIT)
    def _():
        finish(score_bound)

    @pl.when(jnp.logical_not(score_bound <= SHIFT_LIMIT))
    def _():
        finish(None)


def _att_ctx_kernel(q_ref, k_ref, v_ref, qn_ref, kn_ref, *rest, slab):
    o_ref, ko_ref, vo_ref = rest[-3:]
    seq = ko_ref.shape[2]
    if len(rest) == 3:
        for other in range(ko_ref.shape[1]):
            if other != slab:
                ko_ref[:, other] = jnp.zeros_like(ko_ref[:, other])
                vo_ref[:, other] = jnp.zeros_like(vo_ref[:, other])
    for sq in range(ko_ref.shape[0]):
        rows = slice(sq * seq, (sq + 1) * seq)
        k_heads = _norm_rope_heads_t(k_ref[rows, :].T, N_KV_HEADS, kn_ref[...])
        k_n = jnp.concatenate(k_heads, axis=0).T
        ko_ref[sq, slab] = k_n
        v = v_ref[rows, :]
        vo_ref[sq, slab] = v
        q_heads = _norm_rope_heads_t(q_ref[rows, :].T, N_HEADS, qn_ref[...])

        def store(o_t, rows=rows):
            o_ref[rows, :] = o_t.T.astype(o_ref.dtype)

        _attend_t(q_heads, k_n.astype(BF16), v.T.astype(BF16), None, store)


ATT_CTX_ROWS = 1024


def _att_ctx(p_act, seq, q_norm, k_norm, layer, caches=None):
    n_tok = p_act.shape[0]
    batch = n_tok // seq
    rows = max(seq, ATT_CTX_ROWS)
    blk = rows // seq
    in_specs = [
        pl.BlockSpec((rows, ATT_Q), lambda b: (b, OFF_QA // ATT_Q)),
        pl.BlockSpec((rows, ATT_KV), lambda b: (b, OFF_KA // ATT_KV)),
        pl.BlockSpec((rows, ATT_KV), lambda b: (b, OFF_VA // ATT_KV)),
        pl.BlockSpec((HEAD_DIM, 1), lambda b: (0, 0)),
        pl.BlockSpec((HEAD_DIM, 1), lambda b: (0, 0)),
    ]
    args = [p_act, p_act, p_act, q_norm, k_norm]
    if caches is None:
        cache_spec = pl.BlockSpec((blk, DEPTH, seq, ATT_KV), lambda b: (b, 0, 0, 0))
        slab, aliases = layer, {}
    else:
        cache_spec = pl.BlockSpec((blk, 1, seq, ATT_KV), lambda b: (b, layer, 0, 0))
        slab, aliases = 0, {len(args): 1, len(args) + 1: 2}
        in_specs += [pl.BlockSpec(memory_space=pl.ANY)] * 2
        args += list(caches)
    cache_shape = jax.ShapeDtypeStruct((batch, DEPTH, seq, ATT_KV), F32)
    return pl.pallas_call(
        functools.partial(_att_ctx_kernel, slab=slab),
        grid=(n_tok // rows,),
        in_specs=in_specs,
        out_specs=[pl.BlockSpec((rows, ATT_Q), lambda b: (b, 0)), cache_spec, cache_spec],
        out_shape=[jax.ShapeDtypeStruct((n_tok, ATT_Q), BF16), cache_shape, cache_shape],
        input_output_aliases=aliases,
        compiler_params=_cparams(("parallel",)),
        name="att_ctx",
    )(*args)


def _att_lat_kernel(q_ref, k_ref, v_ref, ck_ref, cv_ref, qn_ref, kn_ref,
                    cq_ref, sq_ref, ck_t_ref, sk_t_ref, o_ref, kf_scr, vt_scr, bound_scr, *, seq):
    @pl.when(pl.program_id(1) == 0)
    def _():
        k_heads = _norm_rope_heads_t(k_ref[...].T, N_KV_HEADS, kn_ref[...], ck_t_ref[...], sk_t_ref[...])
        kf_scr[0:seq, :] = jnp.concatenate(k_heads, axis=0).T.astype(BF16)
        past_k = ck_ref[0, 0]
        kf_scr[seq:, :] = past_k.astype(BF16)
        vt_scr[:, 0:seq] = v_ref[...].T.astype(BF16)
        vt_scr[:, seq:] = cv_ref[0, 0].T.astype(BF16)
        past_t = past_k.T
        past_heads = [past_t[kv * HEAD_DIM:(kv + 1) * HEAD_DIM] for kv in range(N_KV_HEADS)]
        bound_scr[0] = _score_bound(k_heads + past_heads, qn_ref[...])

    q_heads = _norm_rope_heads_t(q_ref[...].T, N_HEADS, qn_ref[...], cq_ref[...], sq_ref[...])

    def store(o_t):
        o_ref[...] = o_t.T.astype(o_ref.dtype)

    _attend_t(q_heads, kf_scr[...], vt_scr[...], bound_scr[0], store)


def _att_lat(p_act, seq, cache_k, cache_v, layer, q_norm, k_norm, cos_t, sin_t):
    n_tok = p_act.shape[0]
    nb = seq // Q_BLOCK
    past = cache_k.shape[2]
    return pl.pallas_call(
        functools.partial(_att_lat_kernel, seq=seq),
        grid=(n_tok // seq, nb),
        in_specs=[
            pl.BlockSpec((Q_BLOCK, ATT_Q), lambda b, i: (b * nb + i, OFF_QA // ATT_Q)),
            pl.BlockSpec((seq, ATT_KV), lambda b, i: (b, OFF_KA // ATT_KV)),
            pl.BlockSpec((seq, ATT_KV), lambda b, i: (b, OFF_VA // ATT_KV)),
            pl.BlockSpec((1, 1, past, ATT_KV), lambda b, i: (b, layer, 0, 0)),
            pl.BlockSpec((1, 1, past, ATT_KV), lambda b, i: (b, layer, 0, 0)),
            pl.BlockSpec((HEAD_DIM, 1), lambda b, i: (0, 0)),
            pl.BlockSpec((HEAD_DIM, 1), lambda b, i: (0, 0)),
            pl.BlockSpec((HEAD_DIM // 2, Q_BLOCK), lambda b, i: (0, i)),
            pl.BlockSpec((HEAD_DIM // 2, Q_BLOCK), lambda b, i: (0, i)),
            pl.BlockSpec((HEAD_DIM // 2, seq), lambda b, i: (0, 0)),
            pl.BlockSpec((HEAD_DIM // 2, seq), lambda b, i: (0, 0)),
        ],
        out_specs=pl.BlockSpec((Q_BLOCK, ATT_Q), lambda b, i: (b * nb + i, 0)),
        out_shape=jax.ShapeDtypeStruct((n_tok, ATT_Q), BF16),
        scratch_shapes=[pltpu.VMEM((seq + past, ATT_KV), BF16),
                        pltpu.VMEM((ATT_KV, seq + past), BF16),
                        pltpu.SMEM((1,), F32)],
        compiler_params=_cparams(("parallel", "arbitrary")),
        name="att_lat",
    )(p_act, p_act, p_act, cache_k, cache_v, q_norm, k_norm, cos_t, sin_t, cos_t, sin_t)


RET_ROWS = 2048


def _log_sigmoid(d):
    return jnp.minimum(d, 0.0) - jnp.log1p(jnp.exp(-jnp.abs(d)))


TAB_MASK, TAB_QF, TAB_QB, TAB_KF, TAB_KB, TAB_C, N_TAB = 0, 2, 4, 6, 7, 8, 9
TAB_CF, TAB_CB = 0, 2


def _ret_tables_kernel(dec_ref, tab_ref):
    hp = pl.program_id(0)
    ii = lax.broadcasted_iota(jnp.int32, (CHUNK, CHUNK), 0)
    jj = lax.broadcasted_iota(jnp.int32, (CHUNK, CHUNK), 1)
    rel = (ii - jj).astype(F32)
    row = ii.astype(F32)
    lane = jj.astype(F32)
    lgf = [_log_sigmoid(dec_ref[pl.ds(2 * hp + t, 1), :]) for t in range(2)]
    lgb = [_log_sigmoid(dec_ref[pl.ds(RET_HEADS + 2 * hp + t, 1), :]) for t in range(2)]
    for t in range(2):
        tab_ref[0, TAB_MASK + t] = jnp.where(
            rel > 0, jnp.exp(jnp.maximum(rel, 0.0) * lgf[t]),
            jnp.where(rel < 0, jnp.exp(jnp.maximum(-rel, 0.0) * lgb[t]), 2.0))
        tab_ref[0, TAB_QF + t] = jnp.exp((row + 1.0) * lgf[t])
        tab_ref[0, TAB_QB + t] = jnp.exp((CHUNK - row) * lgb[t])
    tab_ref[0, TAB_KF] = jnp.exp((CHUNK - 1.0 - lane) * jnp.where(ii < RET_DK, lgf[0], lgf[1]))
    tab_ref[0, TAB_KB] = jnp.exp(lane * jnp.where(ii < RET_DK, lgb[0], lgb[1]))
    c_rows = jnp.where(ii == TAB_CF, lgf[0], jnp.where(ii == TAB_CF + 1, lgf[1],
                       jnp.where(ii == TAB_CB, lgb[0], lgb[1])))
    tab_ref[0, TAB_C] = jnp.exp(CHUNK * c_rows)


def _ret_tables(dec):
    return pl.pallas_call(
        _ret_tables_kernel,
        grid=(RET_HEADS // 2,),
        in_specs=[pl.BlockSpec((8, 128), lambda p: (0, 0))],
        out_specs=pl.BlockSpec((1, N_TAB, CHUNK, CHUNK), lambda p: (p, 0, 0, 0)),
        out_shape=jax.ShapeDtypeStruct((RET_HEADS // 2, N_TAB, CHUNK, CHUNK), F32),
        compiler_params=_cparams(("parallel",)),
        name="ret_tables",
    )(dec)


def _fourier_pair(u_ref, wc_ref, csn_ref, of_ref):
    gd = FOURIER_GROUP_DIM
    tc, ts = [], []
    for g in range(u_ref.shape[1] // gd):
        t = _dot(u_ref[:, g * gd:(g + 1) * gd].astype(BF16), wc_ref[...])
        tc.append(t[:, :gd])
        ts.append(t[:, gd:])
    tc = jnp.concatenate(tc, axis=1).astype(BF16)
    ts = jnp.concatenate(ts, axis=1).astype(BF16)
    seq = csn_ref.shape[0]
    for sq in range(of_ref.shape[0] // seq):
        rows = slice(sq * seq, (sq + 1) * seq)
        t_sq = jnp.concatenate([tc[rows], ts[rows]], axis=0)
        of_ref[rows, :] = _dot(csn_ref[...], t_sq).astype(of_ref.dtype)


def _ret_kernel(q_ref, k_ref, v_ref, g_ref, tab_ref, rn_ref, u_ref, wc_ref, csn_ref, *rest,
                seq, has_state, slab):
    if has_state:
        s0f_ref, s0b_ref, o_ref, of_ref = rest
    else:
        o_ref, of_ref, sf_ref, sb_ref = rest[-4:]
        if len(rest) == 4:
            for other in range(sf_ref.shape[1]):
                if other != slab:
                    sf_ref[:, other] = jnp.zeros_like(sf_ref[:, other])
                    sb_ref[:, other] = jnp.zeros_like(sb_ref[:, other])
    _fourier_pair(u_ref, wc_ref, csn_ref, of_ref)
    n_chunks = q_ref.shape[0] // CHUNK
    jj = lax.broadcasted_iota(jnp.int32, (CHUNK, CHUNK), 1)
    mask2 = jnp.concatenate([tab_ref[0, TAB_MASK], tab_ref[0, TAB_MASK + 1]], axis=0)
    qdec_f = [tab_ref[0, TAB_QF + t] for t in range(2)]
    qdec_b = [tab_ref[0, TAB_QB + t] for t in range(2)]
    cdec_f = [tab_ref[0, TAB_C, TAB_CF + t:TAB_CF + t + 1, :] for t in range(2)]
    cdec_b = [tab_ref[0, TAB_C, TAB_CB + t:TAB_CB + t + 1, :] for t in range(2)]
    kdec_f = tab_ref[0, TAB_KF]
    kdec_b = tab_ref[0, TAB_KB]

    k_t = (k_ref[...] * (RET_DK ** -0.5)).T
    first_head = jj < RET_DK

    def chunk(c):
        return slice(c * CHUNK, (c + 1) * CHUNK)

    kv_f = [[None] * n_chunks for _ in range(2)]
    kv_b = [[None] * n_chunks for _ in range(2)]
    for c in range(n_chunks):
        k_c = k_t[:, chunk(c)]
        kd = jnp.concatenate([k_c * kdec_f, k_c * kdec_b], axis=0).astype(BF16)
        kv = _dot(kd, v_ref[chunk(c), :].astype(BF16))
        for t in range(2):
            kv_f[t][c] = kv[t * RET_DK:(t + 1) * RET_DK, t * RET_DV:(t + 1) * RET_DV]
            kv_b[t][c] = kv[CHUNK + t * RET_DK:CHUNK + (t + 1) * RET_DK, t * RET_DV:(t + 1) * RET_DV]

    st_f = [[None] * n_chunks for _ in range(2)]
    st_b = [[None] * n_chunks for _ in range(2)]
    per_seq = seq // CHUNK
    for t in range(2):
        for sq in range(n_chunks // per_seq):
            own = range(sq * per_seq, (sq + 1) * per_seq)
            if has_state:
                s_f = s0f_ref[sq, 0, t]
                s_b = s0b_ref[sq, 0, t]
            else:
                s_f = jnp.zeros((RET_DK, RET_DV), F32)
                s_b = s_f
            for c in own:
                st_f[t][c] = s_f
                s_f = s_f * cdec_f[t] + kv_f[t][c]
            for c in reversed(own):
                st_b[t][c] = s_b
                s_b = s_b * cdec_b[t] + kv_b[t][c]
            if not has_state:
                sf_ref[sq, slab, t] = s_f
                sb_ref[sq, slab, t] = s_b

    for c in range(n_chunks):
        q_c = q_ref[chunk(c), :]
        qm = jnp.concatenate([jnp.where(first_head, q_c, 0.0), jnp.where(first_head, 0.0, q_c)],
                             axis=0).astype(BF16)
        att = (_dot(qm, k_t[:, chunk(c)].astype(BF16)) * mask2).astype(BF16)
        states = jnp.concatenate(
            [jnp.concatenate([st_f[t][c], st_b[t][c]], axis=1) for t in range(2)], axis=0)
        qs = _dot(qm, states.astype(BF16))
        v_c = v_ref[chunk(c), :].astype(BF16)
        for t in range(2):
            rows = slice(t * CHUNK, (t + 1) * CHUNK)
            vsl = slice(t * RET_DV, (t + 1) * RET_DV)
            o = (_dot(att[rows], v_c[:, vsl]) + qs[rows, :RET_DV] * qdec_f[t]
                 + qs[rows, RET_DV:] * qdec_b[t])
            o = (o * lax.rsqrt(jnp.mean(o * o, axis=-1, keepdims=True) + EPS)) * rn_ref[...]
            hg = 0.5 * g_ref[chunk(c), vsl]
            o_ref[chunk(c), vsl] = (o * (hg * (1.0 + jnp.tanh(hg)))).astype(o_ref.dtype)


def _retention(p_act, seq, tables, ret_norm, layer, states=None, new_states=None):
    n_tok = p_act.shape[0]
    n_seq = n_tok // seq
    has_state = states is not None
    pair_qk, pair_v = 2 * RET_DK, 2 * RET_DV
    pair_f = FOURIER_W // (RET_HEADS // 2)
    rows = max(seq, RET_ROWS)
    blk_seqs = rows // seq
    in_specs = [
        pl.BlockSpec((rows, pair_qk), lambda p, s: (s, OFF_QR // pair_qk + p)),
        pl.BlockSpec((rows, pair_qk), lambda p, s: (s, OFF_KR // pair_qk + p)),
        pl.BlockSpec((rows, pair_v), lambda p, s: (s, OFF_VR // pair_v + p)),
        pl.BlockSpec((rows, pair_v), lambda p, s: (s, OFF_GR // pair_v + p)),
        pl.BlockSpec((1, N_TAB, CHUNK, CHUNK), lambda p, s: (p, 0, 0, 0)),
        pl.BlockSpec((1, RET_DV), lambda p, s: (0, 0)),
        pl.BlockSpec((rows, pair_f), lambda p, s: (s, OFF_UF // pair_f + p)),
        pl.BlockSpec((FOURIER_GROUP_DIM, 2 * FOURIER_GROUP_DIM), lambda p, s: (0, 0)),
        pl.BlockSpec((seq, 2 * seq), lambda p, s: (0, 0)),
    ]
    cc, sc = _dft_tables(FOURIER_GROUP_DIM)
    wc = jnp.asarray(np.concatenate([cc, -sc], axis=1), F32).astype(BF16)
    cn, sn = _dft_tables(seq)
    csn = jnp.asarray(np.concatenate([cn, sn], axis=1), F32).astype(BF16)
    args = [p_act, p_act, p_act, p_act, tables, ret_norm, p_act, wc, csn]
    o_spec = pl.BlockSpec((rows, pair_v), lambda p, s: (s, p))
    o_shape = jax.ShapeDtypeStruct((n_tok, RET_V), BF16)
    of_spec = pl.BlockSpec((rows, pair_f), lambda p, s: (s, p))
    of_shape = jax.ShapeDtypeStruct((n_tok, FOURIER_W), BF16)
    st_spec = pl.BlockSpec((blk_seqs, 1, 2, RET_DK, RET_DV), lambda p, s: (s, layer, p, 0, 0))
    aliases, slab = {}, 0
    if has_state:
        in_specs += [st_spec, st_spec]
        args += list(states)
        out_specs, out_shape = [o_spec, of_spec], [o_shape, of_shape]
    else:
        if new_states is None:
            st_spec = pl.BlockSpec((blk_seqs, DEPTH, 2, RET_DK, RET_DV), lambda p, s: (s, 0, p, 0, 0))
            slab = layer
        else:
            in_specs += [pl.BlockSpec(memory_space=pl.ANY)] * 2
            aliases = {len(args): 2, len(args) + 1: 3}
            args += list(new_states)
        st_shape = jax.ShapeDtypeStruct((n_seq, DEPTH, RET_HEADS, RET_DK, RET_DV), F32)
        out_specs = [o_spec, of_spec, st_spec, st_spec]
        out_shape = [o_shape, of_shape, st_shape, st_shape]
    return pl.pallas_call(
        functools.partial(_ret_kernel, seq=seq, has_state=has_state, slab=slab),
        grid=(RET_HEADS // 2, n_tok // rows),
        in_specs=in_specs,
        out_specs=out_specs,
        out_shape=out_shape,
        input_output_aliases=aliases,
        compiler_params=_cparams(("parallel", "parallel")),
        name="retention_lat" if has_state else "retention_ctx",
    )(*args)


def _dft_tables(n):
    k = np.arange(n, dtype=np.int64)
    ang = 2.0 * np.pi * ((k[:, None] * k[None, :]) % n).astype(np.float64) / n
    scale = 1.0 / np.sqrt(n)
    return np.cos(ang) * scale, np.sin(ang) * scale


MIX_TM = 1024
MERGE_STEPS = 4
MERGE_TM = MIX_TM // MERGE_STEPS
MERGE_COLS = 256
FF_TILE = 256
FF_STEPS = D_FF // FF_TILE
FF_ROWS = 256


def _mix_kernel(oa_ref, or_ref, of_ref, ga_ref, gr_ref, gf_ref, x_ref, g1_ref, sh_ref, sc_ref,
                g2_ref, n2_ref, wba_ref, wbr_ref, wbf_ref, wo_ref, wa_ref, wv_ref, cw_ref, cb_ref,
                wd_ref, o_ref, h_scr, acc_scr, wu_scr, wd_scr, *, seq):
    i = pl.program_id(0)
    j = pl.program_id(1)
    chunk = jnp.maximum(j - MERGE_STEPS, 0)
    a_cols = pl.ds(pl.multiple_of(chunk * FF_TILE, FF_TILE), FF_TILE)
    v_cols = pl.ds(pl.multiple_of((FF_STEPS + chunk) * FF_TILE, FF_TILE), FF_TILE)

    @pl.when(j < MERGE_STEPS)
    def _():
        gate = lambda ref, cols: jnp.tanh(ref[:, cols].astype(F32)) + 1.0
        dst = pl.ds(pl.multiple_of(j * MERGE_TM, MERGE_TM), MERGE_TM)
        merged = []
        for c in range(D_MODEL // MERGE_COLS):
            cols = slice(c * MERGE_COLS, (c + 1) * MERGE_COLS)
            m = (gate(ga_ref, cols) * _dot(oa_ref[...], wba_ref[:, cols])
                 + gate(gr_ref, cols) * _dot(or_ref[...], wbr_ref[:, cols])
                 + gate(gf_ref, cols) * _dot(of_ref[...], wbf_ref[:, cols]))
            merged.append((0.5 * m).astype(BF16))
        x1 = x_ref[...] + g1_ref[0] * _dot(jnp.concatenate(merged, axis=1), wo_ref[...])
        o_ref[dst, :] = x1
        r = lax.rsqrt(jnp.mean(x1 * x1, axis=-1, keepdims=True) + EPS)
        h = (x1 * r) * n2_ref[...] * (1.0 + sc_ref[0]) + sh_ref[0]
        h_scr[dst, :] = h.astype(BF16)
        acc_scr[dst, :] = jnp.zeros((MERGE_TM, D_MODEL), F32)

    @pl.when(jnp.logical_and(i == 0, j >= MERGE_STEPS))
    def _():
        wu_scr[:, a_cols] = wa_ref[0].astype(BF16)
        wu_scr[:, v_cols] = wv_ref[0].astype(BF16)
        wd_scr[a_cols, :] = wd_ref[0].astype(BF16)

    def mlp(last):
        blocks = [slice(b * FF_ROWS, (b + 1) * FF_ROWS) for b in range(MIX_TM // FF_ROWS)]
        wa, wv, wd = wu_scr[:, a_cols], wu_scr[:, v_cols], wd_scr[a_cols, :]
        a = jnp.concatenate([_dot(h_scr[rows, :], wa) for rows in blocks], axis=0)
        val = [_dot(h_scr[rows, :], wv) for rows in blocks]
        pos = lax.broadcasted_iota(jnp.int32, a.shape, 0) % seq
        prev = jnp.where(pos == 0, 0.0, pltpu.roll(a, 1, 0))
        nxt = jnp.where(pos == seq - 1, 0.0, pltpu.roll(a, MIX_TM - 1, 0))
        cw = cw_ref[...]
        ac = prev * cw[0:1] + a * cw[1:2] + nxt * cw[2:3] + cb_ref[...]
        for b, rows in enumerate(blocks):
            act = jax.nn.gelu(ac[rows]) * val[b]
            down = _dot(act.astype(BF16), wd)
            if last:
                o_ref[rows, :] = o_ref[rows, :] + g2_ref[0] * (acc_scr[rows, :] + down)
            else:
                acc_scr[rows, :] += down

    last_step = pl.num_programs(1) - 1

    @pl.when(jnp.logical_and(j >= MERGE_STEPS, j < last_step))
    def _():
        mlp(last=False)

    @pl.when(j == last_step)
    def _():
        mlp(last=True)


def _mix(o_att, o_ret, o_four, p_gate, x2d, seq, mod, rows_per_mod, w_att, w_ret, w_four, w_out,
         norm2, w_up, conv_w, conv_b, w_down, layer):
    n_tok = x2d.shape[0]
    mod_idx = lambda i: (i * MIX_TM) // rows_per_mod
    sub = lambda i, j: i * MERGE_STEPS + jnp.minimum(j, MERGE_STEPS - 1)
    ff = lambda j: jnp.maximum(j - MERGE_STEPS, 0)
    ff_w = lambda i, j: jnp.where(i == 0, ff(j), FF_STEPS - 1)
    br = lambda: pl.BlockSpec((MERGE_TM, 512), lambda i, j: (sub(i, j), 0))
    gate = lambda k: pl.BlockSpec((MERGE_TM, D_MODEL), lambda i, j: (sub(i, j), k))
    modv = lambda k: pl.BlockSpec((1, 1, D_MODEL), lambda i, j: (mod_idx(i), 0, k))
    const = lambda shape: pl.BlockSpec(shape, lambda i, j: (0, 0))
    return pl.pallas_call(
        functools.partial(_mix_kernel, seq=seq),
        grid=(n_tok // MIX_TM, MERGE_STEPS + FF_STEPS),
        in_specs=[
            br(), br(), br(), gate(0), gate(1), gate(2),
            pl.BlockSpec((MERGE_TM, D_MODEL), lambda i, j: (sub(i, j), 0)),
            modv(2), modv(3), modv(4), modv(5),
            const((1, D_MODEL)),
            const((512, D_MODEL)), const((512, D_MODEL)), const((512, D_MODEL)),
            const((D_MODEL, D_MODEL)),
            pl.BlockSpec((1, D_MODEL, FF_TILE), lambda i, j: (layer, 0, ff_w(i, j))),
            pl.BlockSpec((1, D_MODEL, FF_TILE), lambda i, j: (layer, 0, FF_STEPS + ff_w(i, j))),
            pl.BlockSpec((3, FF_TILE), lambda i, j: (0, ff(j))),
            pl.BlockSpec((1, FF_TILE), lambda i, j: (0, ff(j))),
            pl.BlockSpec((1, FF_TILE, D_MODEL), lambda i, j: (layer, ff_w(i, j), 0)),
        ],
        out_specs=pl.BlockSpec((MIX_TM, D_MODEL), lambda i, j: (i, 0)),
        out_shape=jax.ShapeDtypeStruct((n_tok, D_MODEL), F32),
        scratch_shapes=[pltpu.VMEM((MIX_TM, D_MODEL), BF16), pltpu.VMEM((MIX_TM, D_MODEL), F32),
                        pltpu.VMEM((D_MODEL, 2 * D_FF), BF16), pltpu.VMEM((D_FF, D_MODEL), BF16)],
        compiler_params=_cparams(("arbitrary", "arbitrary")),
        name="mix",
    )(o_att, o_ret, o_four, p_gate, p_gate, p_gate, x2d, mod, mod, mod, mod, norm2,
      w_att, w_ret, w_four, w_out, w_up, w_up, conv_w, conv_b, w_down)


def _rope_tables(n_tok):
    rows = n_tok // GRID_W
    row_id = jnp.repeat(jnp.arange(rows), GRID_W).astype(F32)
    col_id = jnp.tile(jnp.arange(GRID_W), rows).astype(F32)
    n_freq = HEAD_DIM // 4
    inv = ROPE_THETA ** (-jnp.arange(n_freq, dtype=F32) / n_freq)
    ang = jnp.concatenate([row_id[None, :] * inv[:, None], col_id[None, :] * inv[:, None]], axis=0)
    return jnp.cos(ang), jnp.sin(ang)


def kernel(x_prompt, x_sample, cache_k, cache_v, state_ret_fwd, state_ret_bwd, c, c_ctx, w_ada, b_ada, norm1, w_in, q_norm, k_norm, ret_decay_f, ret_decay_b, ret_norm, w_br_att, w_br_ret, w_br_four, w_out, norm2, w_up, conv_w, conv_b, w_down):
    batch, seq, _ = x_prompt.shape
    dec_batch, dec_seq, _ = x_sample.shape
    past = cache_k.shape[2]
    for n_seq, length in ((batch, seq), (dec_batch, dec_seq)):
        assert length % CHUNK == 0 and MIX_TM % length == 0 and (n_seq * length) % MIX_TM == 0
    assert dec_seq % Q_BLOCK == 0 and dec_seq % GRID_W == 0 and past % CHUNK == 0

    cond_all = jnp.concatenate([c_ctx[None, :], c], axis=0)
    mod_all = _ada(cond_all, w_ada, b_ada)
    cos_t, sin_t = _rope_tables(dec_seq)
    ck = cache_k.reshape(dec_batch, DEPTH, past, ATT_KV)
    cv = cache_v.reshape(dec_batch, DEPTH, past, ATT_KV)

    xp = x_prompt.reshape(batch * seq, D_MODEL)
    xs = x_sample.reshape(dec_batch * dec_seq, D_MODEL)
    new_kv = new_st = None
    for l in range(DEPTH):
        w_att, w_ret, w_four = (w_br_att[l].astype(BF16), w_br_ret[l].astype(BF16),
                                w_br_four[l].astype(BF16))
        w_o = w_out[l].astype(BF16)
        n1, n2 = norm1[l][None, :], norm2[l][None, :]
        qn, kn = q_norm[l][:, None], k_norm[l][:, None]
        rn = ret_norm[l][None, :]
        dec = _ret_tables(jnp.broadcast_to(
            jnp.concatenate([ret_decay_f[l], ret_decay_b[l]])[:, None].astype(F32), (2 * RET_HEADS, 128)))
        cw, cb = conv_w[l], conv_b[l][None, :]
        mod_ctx = mod_all[l, 0:1].reshape(1, 1, 6 * D_MODEL)
        mod_lat = mod_all[l, 1:].reshape(dec_batch, 1, 6 * D_MODEL)

        p_act, p_gate = _in_proj(xp, mod_ctx, batch * seq, n1, w_in, l)
        o_att, *new_kv = _att_ctx(p_act, seq, qn, kn, l, new_kv)
        o_ret, o_four, *new_st = _retention(p_act, seq, dec, rn, l, new_states=new_st)
        xp = _mix(o_att, o_ret, o_four, p_gate, xp, seq, mod_ctx, batch * seq,
                  w_att, w_ret, w_four, w_o, n2, w_up, cw, cb, w_down, l)

        p_act, p_gate = _in_proj(xs, mod_lat, dec_seq, n1, w_in, l)
        o_att = _att_lat(p_act, dec_seq, ck, cv, l, qn, kn, cos_t, sin_t)
        o_ret, o_four = _retention(p_act, dec_seq, dec, rn, l, states=(state_ret_fwd, state_ret_bwd))
        xs = _mix(o_att, o_ret, o_four, p_gate, xs, dec_seq, mod_lat, dec_seq,
                  w_att, w_ret, w_four, w_o, n2, w_up, cw, cb, w_down, l)

    kv_shape = (batch, DEPTH, seq, N_KV_HEADS, HEAD_DIM)
    return (xp.reshape(batch, seq, D_MODEL), xs.reshape(dec_batch, dec_seq, D_MODEL),
            new_kv[0].reshape(kv_shape), new_kv[1].reshape(kv_shape), new_st[0], new_st[1])
```
